```python
import math
import jax, jax.numpy as jnp
from jax import lax
import numpy as np

D_MODEL = 1024
BATCH = 8
SEQ = 8192
DEPTH = 4

CTX_LEN = 256
GRID_W = 64
N_MIXERS = 3
MIXER_POOL = 0
MIXER_ATTN = 1
MIXER_RET = 2
POOL_WINDOWS = (2, 4, 8, 16)
POOL_GROUP = D_MODEL // 4
ATTN_HEADS = 8
ATTN_KV_HEADS = 2
ATTN_HEAD_DIM = D_MODEL // ATTN_HEADS
ATTN_GROUP = ATTN_HEADS // ATTN_KV_HEADS
Q_BLOCK = 128
ROPE_THETA = 10000.0
RET_HEADS = 4
RET_DK = D_MODEL // RET_HEADS
RET_DV = 2 * D_MODEL // RET_HEADS
RET_CHUNK = 128
D_FF = 2816
CONV_WIDTH = 3
EPS = 1e-6

kernel_name = 'hybrid_pool_gqa_retention_convffn_dit'


def _rmsnorm(x, gain):
    xf = x.astype(jnp.float32)
    y = xf * lax.rsqrt(jnp.mean(xf * xf, axis=-1, keepdims=True) + EPS)
    return (y * gain.astype(jnp.float32)).astype(x.dtype)


def _modulate(h, shift, scale):
    return h * (1.0 + scale) + shift


def _pool_mixer(h, w, b, scale):
    s = h.shape[1]
    hf = h.astype(jnp.float32)
    cs = jnp.concatenate([jnp.zeros_like(hf[:, :1]), jnp.cumsum(hf, axis=1)], axis=1)
    t = jnp.arange(s)
    outs = []
    for g, win in enumerate(POOL_WINDOWS):
        lo = jnp.clip(t - win // 2, 0, s)
        hi = jnp.clip(t + win // 2, 0, s)
        sl = slice(g * POOL_GROUP, (g + 1) * POOL_GROUP)
        csg = cs[..., sl]
        mean = (csg[:, hi] - csg[:, lo]) / (hi - lo).astype(jnp.float32)[None, :, None]
        outs.append((mean - hf[..., sl]).astype(h.dtype) @ w[g])
    return (jnp.concatenate(outs, axis=-1) + b) * scale


def _axial_angles(n_tokens):
    rows = n_tokens // GRID_W
    row = jnp.broadcast_to(jnp.arange(rows)[:, None], (rows, GRID_W)).reshape(-1).astype(jnp.float32)
    col = jnp.broadcast_to(jnp.arange(GRID_W)[None, :], (rows, GRID_W)).reshape(-1).astype(jnp.float32)
    axis_dim = ATTN_HEAD_DIM // 2
    inv = ROPE_THETA ** (-jnp.arange(0, axis_dim, 2, dtype=jnp.float32) / axis_dim)
    return row[:, None] * inv, col[:, None] * inv


def _rotate(xa, ang):
    x1, x2 = jnp.split(xa, 2, axis=-1)
    cos = jnp.cos(ang)[None, :, None, :]
    sin = jnp.sin(ang)[None, :, None, :]
    return jnp.concatenate([x1 * cos - x2 * sin, x1 * sin + x2 * cos], axis=-1)


def _apply_axial_rope(x, ang_row, ang_col):
    xr, xc = jnp.split(x.astype(jnp.float32), 2, axis=-1)
    return jnp.concatenate([_rotate(xr, ang_row), _rotate(xc, ang_col)], axis=-1).astype(x.dtype)


def _gqa(q, k, v):
    s = jnp.einsum('bqkgd,bskd->bkgqs', q, k).astype(jnp.float32) * (ATTN_HEAD_DIM ** -0.5)
    p = jax.nn.softmax(s, axis=-1).astype(v.dtype)
    return jnp.einsum('bkgqs,bskd->bqkgd', p, v)


def _attention_mixer(h_lat, h_ctx, w_qkv, q_gain, k_gain, w_o, need_ctx_out):
    def project(h):
        b, s, _ = h.shape
        nq = ATTN_HEADS * ATTN_HEAD_DIM
        nk = ATTN_KV_HEADS * ATTN_HEAD_DIM
        q, k, v = jnp.split(h @ w_qkv, [nq, nq + nk], axis=-1)
        q = _rmsnorm(q.reshape(b, s, ATTN_HEADS, ATTN_HEAD_DIM), q_gain)
        k = _rmsnorm(k.reshape(b, s, ATTN_KV_HEADS, ATTN_HEAD_DIM), k_gain)
        return q, k, v.reshape(b, s, ATTN_KV_HEADS, ATTN_HEAD_DIM)

    b, s = h_lat.shape[:2]
    qc, kc, vc = project(h_ctx)
    ql, kl, vl = project(h_lat)
    ang_r, ang_c = _axial_angles(s)
    ql = _apply_axial_rope(ql, ang_r, ang_c)
    kl = _apply_axial_rope(kl, ang_r, ang_c)
    keys = jnp.concatenate([kl, kc], axis=1)
    vals = jnp.concatenate([vl, vc], axis=1)
    q_blocks = ql.reshape(b, s // Q_BLOCK, Q_BLOCK, ATTN_KV_HEADS, ATTN_GROUP, ATTN_HEAD_DIM)
    q_blocks = q_blocks.transpose(1, 0, 2, 3, 4, 5)
    o = lax.map(lambda qb: _gqa(qb, keys, vals), q_blocks)
    o = o.transpose(1, 0, 2, 3, 4, 5).reshape(b, s, ATTN_HEADS * ATTN_HEAD_DIM)
    y_lat = o @ w_o
    y_ctx = None
    if need_ctx_out:
        l = h_ctx.shape[1]
        oc = _gqa(qc.reshape(b, l, ATTN_KV_HEADS, ATTN_GROUP, ATTN_HEAD_DIM), kc, vc)
        y_ctx = oc.reshape(b, l, ATTN_HEADS * ATTN_HEAD_DIM) @ w_o
    return y_lat, y_ctx


def _retention_chunks(q, k, v, log_gamma, state):
    b, h, s, _ = q.shape
    dv = v.shape[-1]
    n = s // RET_CHUNK

    def chunks(a):
        return jnp.moveaxis(a.reshape(b, h, n, RET_CHUNK, a.shape[-1]), 2, 0)

    idx = jnp.arange(RET_CHUNK, dtype=jnp.float32)
    diff = idx[:, None] - idx[None, :]
    intra = jnp.where(diff >= 0, jnp.exp(jnp.maximum(diff, 0.0) * log_gamma[:, None, None]), 0.0)
    q_dec = jnp.exp((idx + 1.0) * log_gamma[:, None])[None, :, :, None]
    k_dec = jnp.exp((RET_CHUNK - 1.0 - idx) * log_gamma[:, None])[None, :, :, None]
    chunk_dec = jnp.exp(RET_CHUNK * log_gamma)[None, :, None, None]

    def step(r, qkv):
        qc, kc, vc = qkv
        att = jnp.einsum('bhid,bhjd->bhij', qc, kc) * intra
        o = jnp.einsum('bhij,bhje->bhie', att, vc) + jnp.einsum('bhid,bhde->bhie', qc, r) * q_dec
        r = r * chunk_dec + jnp.einsum('bhjd,bhje->bhde', kc * k_dec, vc)
        return r, o

    r, o = lax.scan(step, state, (chunks(q), chunks(k), chunks(v)))
    return jnp.moveaxis(o, 0, 2).reshape(b, h, s, dv), r


def _final_state(k, v, log_gamma):
    l = k.shape[2]
    w = jnp.exp((l - 1.0 - jnp.arange(l, dtype=jnp.float32)) * log_gamma[:, None])
    return jnp.einsum('bhjd,bhje->bhde', k * w[None, :, :, None], v)


def _retention_mixer(h_lat, h_ctx, w_in, decay_logit, gn_w, w_out, need_ctx_out):
    log_gamma = jax.nn.log_sigmoid(decay_logit.astype(jnp.float32))
    nq = RET_HEADS * RET_DK
    nv = RET_HEADS * RET_DV

    def project(h):
        b, s, _ = h.shape
        q, k, v, g = jnp.split(h @ w_in, [nq, 2 * nq, 2 * nq + nv], axis=-1)

        def heads(a, d):
            return a.reshape(b, s, RET_HEADS, d).transpose(0, 2, 1, 3).astype(jnp.float32)
        return heads(q, RET_DK), heads(k, RET_DK) * (RET_DK ** -0.5), heads(v, RET_DV), g

    def flip(a):
        return jnp.flip(a, axis=2)

    def readout(y, g):
        b, h, s, dv = y.shape
        mu = jnp.mean(y, axis=-1, keepdims=True)
        var = jnp.mean(jnp.square(y - mu), axis=-1, keepdims=True)
        yn = (y - mu) * lax.rsqrt(var + EPS) * gn_w.astype(jnp.float32).reshape(1, h, 1, dv)
        yn = yn.transpose(0, 2, 1, 3).reshape(b, s, h * dv).astype(g.dtype)
        return (jax.nn.silu(g) * yn) @ w_out

    y_ctx = None
    if need_ctx_out:
        qc, kc, vc, gc = project(h_ctx)
        zero = jnp.zeros(qc.shape[:2] + (RET_DK, RET_DV), jnp.float32)
        oc_f, r_f = _retention_chunks(qc, kc, vc, log_gamma[0], zero)
        oc_b, r_b = _retention_chunks(flip(qc), flip(kc), flip(vc), log_gamma[1], zero)
        y_ctx = readout(oc_f + flip(oc_b), gc)
    else:
        b, l, _ = h_ctx.shape
        kv_c = (h_ctx @ w_in[:, nq:2 * nq + nv])
        kc = kv_c[..., :nq].reshape(b, l, RET_HEADS, RET_DK).transpose(0, 2, 1, 3).astype(jnp.float32) * (RET_DK ** -0.5)
        vc = kv_c[..., nq:].reshape(b, l, RET_HEADS, RET_DV).transpose(0, 2, 1, 3).astype(jnp.float32)
        r_f = _final_state(kc, vc, log_gamma[0])
        r_b = _final_state(flip(kc), flip(vc), log_gamma[1])
    ql, kl, vl, gl = project(h_lat)
    ol_f, _ = _retention_chunks(ql, kl, vl, log_gamma[0], r_f)
    ol_b, _ = _retention_chunks(flip(ql), flip(kl), flip(vl), log_gamma[1], r_b)
    y_lat = readout(ol_f + flip(ol_b), gl)
    return y_lat, y_ctx


def _conv_ffn(h, w_up, conv_w, conv_b, w_down):
    u = h @ w_up
    up = jnp.pad(u, ((0, 0), (1, 1), (0, 0)))
    u = up[:, :-2] * conv_w[0] + up[:, 1:-1] * conv_w[1] + up[:, 2:] * conv_w[2] + conv_b
    a, v = jnp.split(u, 2, axis=-1)
    return (jax.nn.silu(a) * v) @ w_down


def _layer_counts():
    kinds = [i % N_MIXERS for i in range(DEPTH)]
    return kinds.count(MIXER_POOL), kinds.count(MIXER_ATTN), kinds.count(MIXER_RET)


def _fwd_setup_inputs(seed: int = 0) -> dict:
    key = jax.random.key(seed)
    ks = jax.random.split(key, 22)
    n_pool, n_attn, n_ret = _layer_counts()
    d = D_MODEL

    def nrm(k, shape):
        return jax.random.normal(k, shape, jnp.float32)

    def w(k, shape, fan_in, gain=1.0):
        return nrm(k, shape) * (gain * fan_in ** -0.5)

    def ones_noise(k, shape):
        return 1.0 + 0.05 * nrm(k, shape)

    decay_base = jnp.asarray(np.log(2.0 ** (5 + np.arange(RET_HEADS)) - 1.0).astype(np.float32))
    return {
        'x': nrm(ks[0], (BATCH, SEQ, d)),
        'c': nrm(ks[1], (BATCH, d)),
        'ctx': nrm(ks[2], (BATCH, CTX_LEN, d)),
        'c_ctx': nrm(ks[3], (d,)),
        'ada_w': w(ks[4], (DEPTH, d, 6 * d), d, 0.5),
        'ada_b': 0.01 * nrm(ks[5], (DEPTH, 6 * d)),
        'norm_w': ones_noise(ks[6], (DEPTH, 2, d)),
        'pool_w': w(ks[7], (n_pool, 4, POOL_GROUP, POOL_GROUP), POOL_GROUP),
        'pool_b': 0.01 * nrm(ks[8], (n_pool, d)),
        'pool_scale': ones_noise(ks[9], (n_pool, d)),
        'attn_w_qkv': w(ks[10], (n_attn, d, (ATTN_HEADS + 2 * ATTN_KV_HEADS) * ATTN_HEAD_DIM), d),
        'attn_q_gain': ones_noise(ks[11], (n_attn, ATTN_HEAD_DIM)),
        'attn_k_gain': ones_noise(ks[12], (n_attn, ATTN_HEAD_DIM)),
        'attn_w_o': w(ks[13], (n_attn, ATTN_HEADS * ATTN_HEAD_DIM, d), ATTN_HEADS * ATTN_HEAD_DIM),
        'ret_w_in': w(ks[14], (n_ret, d, 2 * RET_HEADS * RET_DK + 2 * RET_HEADS * RET_DV), d),
        'ret_decay_logit': decay_base[None, None, :] + 0.1 * nrm(ks[15], (n_ret, 2, RET_HEADS)),
        'ret_gn_w': ones_noise(ks[16], (n_ret, RET_HEADS * RET_DV)),
        'ret_w_out': w(ks[17], (n_ret, RET_HEADS * RET_DV, d), RET_HEADS * RET_DV),
        'ffn_w_up': w(ks[18], (DEPTH, d, 2 * D_FF), d),
        'ffn_conv_w': w(ks[19], (DEPTH, CONV_WIDTH, 2 * D_FF), CONV_WIDTH),
        'ffn_conv_b': 0.01 * nrm(ks[20], (DEPTH, 2 * D_FF)),
        'ffn_w_down': w(ks[21], (DEPTH, D_FF, d), D_FF),
    }


def _fwd_reference(x, c, ctx, c_ctx, ada_w, ada_b, norm_w, pool_w, pool_b, pool_scale,
              attn_w_qkv, attn_q_gain, attn_k_gain, attn_w_o,
              ret_w_in, ret_decay_logit, ret_gn_w, ret_w_out,
              ffn_w_up, ffn_conv_w, ffn_conv_b, ffn_w_down):
    ctx_s = ctx
    silu_c = jax.nn.silu(c)
    silu_cc = jax.nn.silu(c_ctx)
    for i in range(DEPTH):
        kind = i % N_MIXERS
        j = i // N_MIXERS
        need_ctx_out = any(k % N_MIXERS != MIXER_POOL for k in range(i + 1, DEPTH))
        need_ctx_in = need_ctx_out or kind != MIXER_POOL

        sh1, sc1, g1, sh2, sc2, g2 = [m[:, None, :] for m in jnp.split(silu_c @ ada_w[i] + ada_b[i], 6, axis=-1)]
        h = _modulate(_rmsnorm(x, norm_w[i, 0]), sh1, sc1)
        hc = None
        if need_ctx_in:
            csh1, csc1, cg1, csh2, csc2, cg2 = jnp.split(silu_cc @ ada_w[i] + ada_b[i], 6, axis=-1)
            hc = _modulate(_rmsnorm(ctx_s, norm_w[i, 0]), csh1, csc1)

        if kind == MIXER_POOL:
            y = _pool_mixer(h, pool_w[j], pool_b[j], pool_scale[j])
            y_c = _pool_mixer(hc, pool_w[j], pool_b[j], pool_scale[j]) if need_ctx_out else None
        elif kind == MIXER_ATTN:
            y, y_c = _attention_mixer(h, hc, attn_w_qkv[j], attn_q_gain[j], attn_k_gain[j], attn_w_o[j], need_ctx_out)
        else:
            y, y_c = _retention_mixer(h, hc, ret_w_in[j], ret_decay_logit[j], ret_gn_w[j], ret_w_out[j], need_ctx_out)

        x = x + g1 * y
        x = x + g2 * _conv_ffn(_modulate(_rmsnorm(x, norm_w[i, 1]), sh2, sc2),
                               ffn_w_up[i], ffn_conv_w[i], ffn_conv_b[i], ffn_w_down[i])
        if need_ctx_out:
            ctx_s = ctx_s + cg1 * y_c
            ctx_s = ctx_s + cg2 * _conv_ffn(_modulate(_rmsnorm(ctx_s, norm_w[i, 1]), csh2, csc2),
                                            ffn_w_up[i], ffn_conv_w[i], ffn_conv_b[i], ffn_w_down[i])
    return x


import jax as _jax
import jax.numpy as _jnp

TWIN_FORMAT = 'train_step'
FWD_PARAMS = ['x', 'c', 'ctx', 'c_ctx', 'ada_w', 'ada_b', 'norm_w', 'pool_w', 'pool_b', 'pool_scale', 'attn_w_qkv', 'attn_q_gain', 'attn_k_gain', 'attn_w_o', 'ret_w_in', 'ret_decay_logit', 'ret_gn_w', 'ret_w_out', 'ffn_w_up', 'ffn_conv_w', 'ffn_conv_b', 'ffn_w_down']
TWIN_WEIGHTS = ['c_ctx', 'ada_w', 'ada_b', 'norm_w', 'pool_w', 'pool_b', 'pool_scale', 'attn_w_qkv', 'attn_q_gain', 'attn_k_gain', 'attn_w_o', 'ret_w_in', 'ret_decay_logit', 'ret_gn_w', 'ret_w_out', 'ffn_w_up', 'ffn_conv_w', 'ffn_conv_b', 'ffn_w_down']
TWIN_DIFF_INPUT = 'x'
TWIN_INPUTS = ['x', 'c', 'ctx', 'c_ctx', 'ada_w', 'ada_b', 'norm_w', 'pool_w', 'pool_b', 'pool_scale', 'attn_w_qkv', 'attn_q_gain', 'attn_k_gain', 'attn_w_o', 'ret_w_in', 'ret_decay_logit', 'ret_gn_w', 'ret_w_out', 'ffn_w_up', 'ffn_conv_w', 'ffn_conv_b', 'ffn_w_down', 'loss_target', 'm_c_ctx', 'm_ada_w', 'm_ada_b', 'm_norm_w', 'm_pool_w', 'm_pool_b', 'm_pool_scale', 'm_attn_w_qkv', 'm_attn_q_gain', 'm_attn_k_gain', 'm_attn_w_o', 'm_ret_w_in', 'm_ret_decay_logit', 'm_ret_gn_w', 'm_ret_w_out', 'm_ffn_w_up', 'm_ffn_conv_w', 'm_ffn_conv_b', 'm_ffn_w_down', 'v_c_ctx', 'v_ada_w', 'v_ada_b', 'v_norm_w', 'v_pool_w', 'v_pool_b', 'v_pool_scale', 'v_attn_w_qkv', 'v_attn_q_gain', 'v_attn_k_gain', 'v_attn_w_o', 'v_ret_w_in', 'v_ret_decay_logit', 'v_ret_gn_w', 'v_ret_w_out', 'v_ffn_w_up', 'v_ffn_conv_w', 'v_ffn_conv_b', 'v_ffn_w_down']
TWIN_OUTPUTS = ['loss', 'grad_x', 'grad_c_ctx', 'grad_ada_w', 'grad_ada_b', 'grad_norm_w', 'grad_pool_w', 'grad_pool_b', 'grad_pool_scale', 'grad_attn_w_qkv', 'grad_attn_q_gain', 'grad_attn_k_gain', 'grad_attn_w_o', 'grad_ret_w_in', 'grad_ret_decay_logit', 'grad_ret_gn_w', 'grad_ret_w_out', 'grad_ffn_w_up', 'grad_ffn_conv_w', 'grad_ffn_conv_b', 'grad_ffn_w_down', 'delta_c_ctx', 'delta_ada_w', 'delta_ada_b', 'delta_norm_w', 'delta_pool_w', 'delta_pool_b', 'delta_pool_scale', 'delta_attn_w_qkv', 'delta_attn_q_gain', 'delta_attn_k_gain', 'delta_attn_w_o', 'delta_ret_w_in', 'delta_ret_decay_logit', 'delta_ret_gn_w', 'delta_ret_w_out', 'delta_ffn_w_up', 'delta_ffn_conv_w', 'delta_ffn_conv_b', 'delta_ffn_w_down', 'new_m_c_ctx', 'new_m_ada_w', 'new_m_ada_b', 'new_m_norm_w', 'new_m_pool_w', 'new_m_pool_b', 'new_m_pool_scale', 'new_m_attn_w_qkv', 'new_m_attn_q_gain', 'new_m_attn_k_gain', 'new_m_attn_w_o', 'new_m_ret_w_in', 'new_m_ret_decay_logit', 'new_m_ret_gn_w', 'new_m_ret_w_out', 'new_m_ffn_w_up', 'new_m_ffn_conv_w', 'new_m_ffn_conv_b', 'new_m_ffn_w_down', 'new_v_c_ctx', 'new_v_ada_w', 'new_v_ada_b', 'new_v_norm_w', 'new_v_pool_w', 'new_v_pool_b', 'new_v_pool_scale', 'new_v_attn_w_qkv', 'new_v_attn_q_gain', 'new_v_attn_k_gain', 'new_v_attn_w_o', 'new_v_ret_w_in', 'new_v_ret_decay_logit', 'new_v_ret_gn_w', 'new_v_ret_w_out', 'new_v_ffn_w_up', 'new_v_ffn_conv_w', 'new_v_ffn_conv_b', 'new_v_ffn_w_down']
TWIN_LEAF_KINDS = {'loss': 'loss', 'grad_x': 'grad_x', 'grad_c_ctx': 'grad_w', 'grad_ada_w': 'grad_w', 'grad_ada_b': 'grad_w', 'grad_norm_w': 'grad_w', 'grad_pool_w': 'grad_w', 'grad_pool_b': 'grad_w', 'grad_pool_scale': 'grad_w', 'grad_attn_w_qkv': 'grad_w', 'grad_attn_q_gain': 'grad_w', 'grad_attn_k_gain': 'grad_w', 'grad_attn_w_o': 'grad_w', 'grad_ret_w_in': 'grad_w', 'grad_ret_decay_logit': 'grad_w', 'grad_ret_gn_w': 'grad_w', 'grad_ret_w_out': 'grad_w', 'grad_ffn_w_up': 'grad_w', 'grad_ffn_conv_w': 'grad_w', 'grad_ffn_conv_b': 'grad_w', 'grad_ffn_w_down': 'grad_w', 'delta_c_ctx': 'delta_w', 'delta_ada_w': 'delta_w', 'delta_ada_b': 'delta_w', 'delta_norm_w': 'delta_w', 'delta_pool_w': 'delta_w', 'delta_pool_b': 'delta_w', 'delta_pool_scale': 'delta_w', 'delta_attn_w_qkv': 'delta_w', 'delta_attn_q_gain': 'delta_w', 'delta_attn_k_gain': 'delta_w', 'delta_attn_w_o': 'delta_w', 'delta_ret_w_in': 'delta_w', 'delta_ret_decay_logit': 'delta_w', 'delta_ret_gn_w': 'delta_w', 'delta_ret_w_out': 'delta_w', 'delta_ffn_w_up': 'delta_w', 'delta_ffn_conv_w': 'delta_w', 'delta_ffn_conv_b': 'delta_w', 'delta_ffn_w_down': 'delta_w', 'new_m_c_ctx': 'new_m', 'new_m_ada_w': 'new_m', 'new_m_ada_b': 'new_m', 'new_m_norm_w': 'new_m', 'new_m_pool_w': 'new_m', 'new_m_pool_b': 'new_m', 'new_m_pool_scale': 'new_m', 'new_m_attn_w_qkv': 'new_m', 'new_m_attn_q_gain': 'new_m', 'new_m_attn_k_gain': 'new_m', 'new_m_attn_w_o': 'new_m', 'new_m_ret_w_in': 'new_m', 'new_m_ret_decay_logit': 'new_m', 'new_m_ret_gn_w': 'new_m', 'new_m_ret_w_out': 'new_m', 'new_m_ffn_w_up': 'new_m', 'new_m_ffn_conv_w': 'new_m', 'new_m_ffn_conv_b': 'new_m', 'new_m_ffn_w_down': 'new_m', 'new_v_c_ctx': 'new_v', 'new_v_ada_w': 'new_v', 'new_v_ada_b': 'new_v', 'new_v_norm_w': 'new_v', 'new_v_pool_w': 'new_v', 'new_v_pool_b': 'new_v', 'new_v_pool_scale': 'new_v', 'new_v_attn_w_qkv': 'new_v', 'new_v_attn_q_gain': 'new_v', 'new_v_attn_k_gain': 'new_v', 'new_v_attn_w_o': 'new_v', 'new_v_ret_w_in': 'new_v', 'new_v_ret_decay_logit': 'new_v', 'new_v_ret_gn_w': 'new_v', 'new_v_ret_w_out': 'new_v', 'new_v_ffn_w_up': 'new_v', 'new_v_ffn_conv_w': 'new_v', 'new_v_ffn_conv_b': 'new_v', 'new_v_ffn_w_down': 'new_v'}


def _forward(args):
    return _fwd_reference(*[args[k] for k in FWD_PARAMS])


def _output_shape():
    def fwd():
        inp = _fwd_setup_inputs(0)
        return _fwd_reference(*[inp[k] for k in FWD_PARAMS])
    out = _jax.eval_shape(fwd)
    return out.shape, out.dtype

N_MICROBATCH = 1
ADAM_LR = 0.001
ADAM_B1 = 0.9
ADAM_B2 = 0.999
ADAM_EPS = 1e-08
ADAM_WD = 0.01
ADAM_STEP = 10
PER_EXAMPLE_BATCH_AXIS = {'x': 0, 'c': 0, 'ctx': 0, 'loss_target': 0}
SHARED_INPUTS = []
_WEIGHT_DTYPES = {'c_ctx': _jnp.float32, 'ada_w': _jnp.float32, 'ada_b': _jnp.float32, 'norm_w': _jnp.float32, 'pool_w': _jnp.float32, 'pool_b': _jnp.float32, 'pool_scale': _jnp.float32, 'attn_w_qkv': _jnp.float32, 'attn_q_gain': _jnp.float32, 'attn_k_gain': _jnp.float32, 'attn_w_o': _jnp.float32, 'ret_w_in': _jnp.float32, 'ret_decay_logit': _jnp.float32, 'ret_gn_w': _jnp.float32, 'ret_w_out': _jnp.float32, 'ffn_w_up': _jnp.float32, 'ffn_conv_w': _jnp.float32, 'ffn_conv_b': _jnp.float32, 'ffn_w_down': _jnp.float32}
MOMENT_SCALE = {'c_ctx': 1.108996e-01, 'ada_w': 1.747136e+00, 'ada_b': 4.550487e+00, 'norm_w': 5.437889e+00, 'pool_w': 4.927974e-01, 'pool_b': 9.510950e-01, 'pool_scale': 5.750868e+00, 'attn_w_qkv': 3.081188e-01, 'attn_q_gain': 6.932013e-02, 'attn_k_gain': 6.941301e-02, 'attn_w_o': 3.345577e-01, 'ret_w_in': 1.152832e-01, 'ret_decay_logit': 3.682738e-01, 'ret_gn_w': 1.318478e+00, 'ret_w_out': 1.143366e-01, 'ffn_w_up': 1.634443e-01, 'ffn_conv_w': 9.868173e-01, 'ffn_conv_b': 8.046046e-01, 'ffn_w_down': 1.413949e-01}


def _to_microbatches(a, axis):
    t = _jnp.moveaxis(a, axis, 0)
    t = t.reshape((N_MICROBATCH, t.shape[0] // N_MICROBATCH) + t.shape[1:])
    return _jnp.moveaxis(t, 1, axis + 1)


def setup_inputs(seed: int = 0) -> dict:
    inp = _fwd_setup_inputs(seed)
    key = _jax.random.fold_in(_jax.random.key(seed), 7919)
    shape, _ = _output_shape()
    out = dict(inp)
    out["loss_target"] = _jax.random.normal(_jax.random.fold_in(key, 0), shape, _jnp.float32)
    for i, name in enumerate(TWIN_WEIGHTS):
        w = inp[name].astype(_jnp.float32)
        if MOMENT_SCALE is None:
            s = _jnp.sqrt(_jnp.mean(_jnp.square(w)) + 1e-30)
        else:
            s = MOMENT_SCALE[name]
        km, kv = _jax.random.split(_jax.random.fold_in(key, i + 1))
        out[name] = w
        out["m_" + name] = s * _jax.random.normal(km, w.shape, _jnp.float32)
        out["v_" + name] = (s * s) * _jax.random.uniform(kv, w.shape, _jnp.float32, 0.5, 1.5)
    if N_MICROBATCH > 1:
        for name, axis in PER_EXAMPLE_BATCH_AXIS.items():
            out[name] = _to_microbatches(out[name], axis)
    return {'x': out['x'], 'c': out['c'], 'ctx': out['ctx'], 'c_ctx': out['c_ctx'], 'ada_w': out['ada_w'], 'ada_b': out['ada_b'], 'norm_w': out['norm_w'], 'pool_w': out['pool_w'], 'pool_b': out['pool_b'], 'pool_scale': out['pool_scale'], 'attn_w_qkv': out['attn_w_qkv'], 'attn_q_gain': out['attn_q_gain'], 'attn_k_gain': out['attn_k_gain'], 'attn_w_o': out['attn_w_o'], 'ret_w_in': out['ret_w_in'], 'ret_decay_logit': out['ret_decay_logit'], 'ret_gn_w': out['ret_gn_w'], 'ret_w_out': out['ret_w_out'], 'ffn_w_up': out['ffn_w_up'], 'ffn_conv_w': out['ffn_conv_w'], 'ffn_conv_b': out['ffn_conv_b'], 'ffn_w_down': out['ffn_w_down'], 'loss_target': out['loss_target'], 'm_c_ctx': out['m_c_ctx'], 'm_ada_w': out['m_ada_w'], 'm_ada_b': out['m_ada_b'], 'm_norm_w': out['m_norm_w'], 'm_pool_w': out['m_pool_w'], 'm_pool_b': out['m_pool_b'], 'm_pool_scale': out['m_pool_scale'], 'm_attn_w_qkv': out['m_attn_w_qkv'], 'm_attn_q_gain': out['m_attn_q_gain'], 'm_attn_k_gain': out['m_attn_k_gain'], 'm_attn_w_o': out['m_attn_w_o'], 'm_ret_w_in': out['m_ret_w_in'], 'm_ret_decay_logit': out['m_ret_decay_logit'], 'm_ret_gn_w': out['m_ret_gn_w'], 'm_ret_w_out': out['m_ret_w_out'], 'm_ffn_w_up': out['m_ffn_w_up'], 'm_ffn_conv_w': out['m_ffn_conv_w'], 'm_ffn_conv_b': out['m_ffn_conv_b'], 'm_ffn_w_down': out['m_ffn_w_down'], 'v_c_ctx': out['v_c_ctx'], 'v_ada_w': out['v_ada_w'], 'v_ada_b': out['v_ada_b'], 'v_norm_w': out['v_norm_w'], 'v_pool_w': out['v_pool_w'], 'v_pool_b': out['v_pool_b'], 'v_pool_scale': out['v_pool_scale'], 'v_attn_w_qkv': out['v_attn_w_qkv'], 'v_attn_q_gain': out['v_attn_q_gain'], 'v_attn_k_gain': out['v_attn_k_gain'], 'v_attn_w_o': out['v_attn_w_o'], 'v_ret_w_in': out['v_ret_w_in'], 'v_ret_decay_logit': out['v_ret_decay_logit'], 'v_ret_gn_w': out['v_ret_gn_w'], 'v_ret_w_out': out['v_ret_w_out'], 'v_ffn_w_up': out['v_ffn_w_up'], 'v_ffn_conv_w': out['v_ffn_conv_w'], 'v_ffn_conv_b': out['v_ffn_conv_b'], 'v_ffn_w_down': out['v_ffn_w_down']}


def _loss(weights, diff, rest, loss_target):
    with _jax.named_scope("forward"):
        args = {**rest, TWIN_DIFF_INPUT: diff, **{k: w.astype(_WEIGHT_DTYPES[k]) for k, w in weights.items()}}
        y = _forward(args)
    with _jax.named_scope("loss_head"):
        err = _jnp.square(y.astype(_jnp.float32) - loss_target)
        return 0.5 * _jnp.sum(_jnp.mean(err, axis=-1)) if err.ndim else 0.5 * err


def _adamw(w, g, m, v):
    m = ADAM_B1 * m + (1.0 - ADAM_B1) * g
    v = ADAM_B2 * v + (1.0 - ADAM_B2) * _jnp.square(g)
    m_hat = m / (1.0 - ADAM_B1 ** ADAM_STEP)
    v_hat = v / (1.0 - ADAM_B2 ** ADAM_STEP)
    delta = -ADAM_LR * (m_hat / (_jnp.sqrt(v_hat) + ADAM_EPS) + ADAM_WD * w)
    return delta, m, v


def reference(x, c, ctx, c_ctx, ada_w, ada_b, norm_w, pool_w, pool_b, pool_scale, attn_w_qkv, attn_q_gain, attn_k_gain, attn_w_o, ret_w_in, ret_decay_logit, ret_gn_w, ret_w_out, ffn_w_up, ffn_conv_w, ffn_conv_b, ffn_w_down, loss_target, m_c_ctx, m_ada_w, m_ada_b, m_norm_w, m_pool_w, m_pool_b, m_pool_scale, m_attn_w_qkv, m_attn_q_gain, m_attn_k_gain, m_attn_w_o, m_ret_w_in, m_ret_decay_logit, m_ret_gn_w, m_ret_w_out, m_ffn_w_up, m_ffn_conv_w, m_ffn_conv_b, m_ffn_w_down, v_c_ctx, v_ada_w, v_ada_b, v_norm_w, v_pool_w, v_pool_b, v_pool_scale, v_attn_w_qkv, v_attn_q_gain, v_attn_k_gain, v_attn_w_o, v_ret_w_in, v_ret_decay_logit, v_ret_gn_w, v_ret_w_out, v_ffn_w_up, v_ffn_conv_w, v_ffn_conv_b, v_ffn_w_down):
    given = dict(x=x, c=c, ctx=ctx, c_ctx=c_ctx, ada_w=ada_w, ada_b=ada_b, norm_w=norm_w, pool_w=pool_w, pool_b=pool_b, pool_scale=pool_scale, attn_w_qkv=attn_w_qkv, attn_q_gain=attn_q_gain, attn_k_gain=attn_k_gain, attn_w_o=attn_w_o, ret_w_in=ret_w_in, ret_decay_logit=ret_decay_logit, ret_gn_w=ret_gn_w, ret_w_out=ret_w_out, ffn_w_up=ffn_w_up, ffn_conv_w=ffn_conv_w, ffn_conv_b=ffn_conv_b, ffn_w_down=ffn_w_down, loss_target=loss_target, m_c_ctx=m_c_ctx, m_ada_w=m_ada_w, m_ada_b=m_ada_b, m_norm_w=m_norm_w, m_pool_w=m_pool_w, m_pool_b=m_pool_b, m_pool_scale=m_pool_scale, m_attn_w_qkv=m_attn_w_qkv, m_attn_q_gain=m_attn_q_gain, m_attn_k_gain=m_attn_k_gain, m_attn_w_o=m_attn_w_o, m_ret_w_in=m_ret_w_in, m_ret_decay_logit=m_ret_decay_logit, m_ret_gn_w=m_ret_gn_w, m_ret_w_out=m_ret_w_out, m_ffn_w_up=m_ffn_w_up, m_ffn_conv_w=m_ffn_conv_w, m_ffn_conv_b=m_ffn_conv_b, m_ffn_w_down=m_ffn_w_down, v_c_ctx=v_c_ctx, v_ada_w=v_ada_w, v_ada_b=v_ada_b, v_norm_w=v_norm_w, v_pool_w=v_pool_w, v_pool_b=v_pool_b, v_pool_scale=v_pool_scale, v_attn_w_qkv=v_attn_w_qkv, v_attn_q_gain=v_attn_q_gain, v_attn_k_gain=v_attn_k_gain, v_attn_w_o=v_attn_w_o, v_ret_w_in=v_ret_w_in, v_ret_decay_logit=v_ret_decay_logit, v_ret_gn_w=v_ret_gn_w, v_ret_w_out=v_ret_w_out, v_ffn_w_up=v_ffn_w_up, v_ffn_conv_w=v_ffn_conv_w, v_ffn_conv_b=v_ffn_conv_b, v_ffn_w_down=v_ffn_w_down)
    weights = {n: given[n] for n in TWIN_WEIGHTS}
    shared = {n: given[n] for n in SHARED_INPUTS}
    per_example = {n: given[n] for n in ['x', 'c', 'ctx']}
    grad_fn = _jax.value_and_grad(_loss, argnums=(0, 1))

    def one_microbatch(ex, loss_target):
        ex = dict(ex)
        diff = ex.pop(TWIN_DIFF_INPUT)
        return grad_fn(weights, diff, {**shared, **ex}, loss_target)

    if N_MICROBATCH == 1:
        loss, (grad_w, grad_x) = one_microbatch(per_example, given["loss_target"])
    else:
        def body(carry, xs):
            loss_sum, grad_sum = carry
            l_k, (gw_k, gx_k) = one_microbatch(xs[0], xs[1])
            with _jax.named_scope("update"):
                return (loss_sum + l_k, _jax.tree.map(_jnp.add, grad_sum, gw_k)), gx_k

        init = (_jnp.zeros((), _jnp.float32), _jax.tree.map(_jnp.zeros_like, weights))
        (loss, grad_w), grad_x = _jax.lax.scan(body, init, (per_example, given["loss_target"]))
    with _jax.named_scope("update"):
        delta_w, new_m, new_v = {}, {}, {}
        for n in TWIN_WEIGHTS:
            delta_w[n], new_m[n], new_v[n] = _adamw(weights[n], grad_w[n], given["m_" + n], given["v_" + n])
    return (loss, grad_x, *[grad_w[n] for n in TWIN_WEIGHTS], *[delta_w[n] for n in TWIN_WEIGHTS],
            *[new_m[n] for n in TWIN_WEIGHTS], *[new_v[n] for n in TWIN_WEIGHTS])
```

```python
import functools

import jax
import jax.numpy as jnp
from jax import lax
from jax.experimental import pallas as pl
from jax.experimental.pallas import tpu as pltpu

F32 = jnp.float32
BF16 = jnp.bfloat16

D = 1024
BM = 256
EPS = 1e-6
POOL_WINDOWS = (2, 4, 8, 16)
POOL_GROUP = D // 4
ATTN_HEADS = 8
ATTN_KV_HEADS = 2
HD = D // ATTN_HEADS
ATTN_GROUP = ATTN_HEADS // ATTN_KV_HEADS
NQ = ATTN_HEADS * HD
NKV = ATTN_KV_HEADS * HD
GRID_W = 64
ROPE_THETA = 10000.0
RET_HEADS = 4
RET_DK = D // RET_HEADS
RET_DV = 2 * D // RET_HEADS
RNQ = RET_HEADS * RET_DK
RNV = RET_HEADS * RET_DV
D_FF = 2816
FF_CHUNK = 256
ADAM_LR, ADAM_B1, ADAM_B2, ADAM_EPS, ADAM_WD, ADAM_STEP = 0.001, 0.9, 0.999, 1e-08, 0.01, 10
HALO_F32 = 8
HALO_BF16 = 16
VMEM_LIMIT = 60 * 1024 * 1024

MESH_AXES = ("x", "y", "c")
N_DEV = 8
N_CHIP = 4


def _pcall(body, **kw):
    return pl.pallas_call(body, **kw)


def _spec(shape, kind, nlb, nblk):
    nd = len(shape)
    if kind == "row":
        return pl.BlockSpec((BM, shape[1]), lambda i: (i, 0))
    if kind == "full":
        return pl.BlockSpec(tuple(shape), lambda i: (0,) * nd, pipeline_mode=pl.Buffered(1))
    if isinstance(kind, tuple) and kind[0] == "fullat":
        return pl.BlockSpec((None,) + tuple(shape[1:]), lambda i: (kind[1],) + (0,) * (nd - 1), pipeline_mode=pl.Buffered(1))
    if kind == "acc":
        return pl.BlockSpec(tuple(shape), lambda i: (0,) * nd)
    if kind == "stream":
        return pl.BlockSpec((1,) + tuple(shape[1:]), lambda i: (i // nlb,) + (0,) * (nd - 1))
    if kind in ("prev8", "prev16"):
        hb = int(kind[4:])
        return pl.BlockSpec((hb, shape[1]), lambda i: (jnp.maximum(i * (BM // hb) - 1, 0), 0))
    if kind in ("next8", "next16"):
        hb = int(kind[4:])
        last = nblk * BM // hb - 1
        return pl.BlockSpec((hb, shape[1]), lambda i: (jnp.minimum((i + 1) * (BM // hb), last), 0))
    raise ValueError(kind)


def _rowcall(name, body, nblk, nlb, ins, outs, scratch=()):
    return _pcall(
        body,
        name=name,
        grid=(nblk,),
        in_specs=[_spec(a.shape, k, nlb, nblk) for a, k in ins],
        out_specs=[_spec(s, k, nlb, nblk) for s, _, k in outs],
        out_shape=[jax.ShapeDtypeStruct(s, d) for s, d, _ in outs],
        scratch_shapes=list(scratch),
        compiler_params=pltpu.CompilerParams(dimension_semantics=("arbitrary",), vmem_limit_bytes=VMEM_LIMIT),
    )(*[a for a, _ in ins])


def _wk(w):
    return (w[0], ("fullat", w[1])) if isinstance(w, tuple) else (w, "full")


def _stream_edges(i, nlb):
    is_ctx = i == nlb
    return (i == 0) | is_ctx, (i == nlb - 1) | is_ctx, is_ctx


def _dotf(a, b):
    return jnp.dot(a, b, preferred_element_type=F32)


def _dot_nt(a, b):
    return lax.dot_general(a, b, (((1,), (1,)), ((), ())), preferred_element_type=F32)


def _dot_tn(a, b):
    return lax.dot_general(a, b, (((0,), (0,)), ((), ())), preferred_element_type=F32)


def _sigmoid(x):
    return 1.0 / (1.0 + jnp.exp(-x))


def _norm_mod(x, nw, sh, sc):
    r = lax.rsqrt(jnp.mean(x * x, axis=-1, keepdims=True) + EPS)
    xhat = x * r
    n = xhat * nw
    return n * (1.0 + sc) + sh, n, xhat, r


def _norm_mod_bwd(dh, n, xhat, r, nw, sc):
    dsh = jnp.sum(dh, axis=0, keepdims=True)
    dsc = jnp.sum(dh * n, axis=0, keepdims=True)
    dn = dh * (1.0 + sc)
    dnw = jnp.sum(dn * xhat, axis=0, keepdims=True)
    dxhat = dn * nw
    dx = r * (dxhat - xhat * jnp.mean(dxhat * xhat, axis=-1, keepdims=True))
    return dx, dsh, dsc, dnw


def _acc_init(i, *refs):
    @pl.when(i == 0)
    def _():
        for r in refs:
            r[...] = jnp.zeros(r.shape, r.dtype)


SH1, SC1, G1, SH2, SC2, G2 = range(6)


def _mrow(mod_ref, k):
    return mod_ref[0, k : k + 1, :]


def _shift_rows(cur, halo_prev_row, halo_next_row, first, last):
    rows = lax.broadcasted_iota(jnp.int32, cur.shape, 0)
    pr = jnp.where(first, 0.0, halo_prev_row)
    nx = jnp.where(last, 0.0, halo_next_row)
    dn = jnp.where(rows == 0, pr, pltpu.roll(cur, 1, 0))
    up = jnp.where(rows == BM - 1, nx, pltpu.roll(cur, BM - 1, 0))
    return dn, up


def ffn_up(x1, mod, nw2, w_up, nblk, nlb):
    r = x1.shape[0]

    def body(x_ref, mod_ref, nw_ref, w_ref, u_ref, h_ref):
        h, _, _, _ = _norm_mod(x_ref[...], nw_ref[...], _mrow(mod_ref, SH2), _mrow(mod_ref, SC2))
        hb = h.astype(BF16)
        h_ref[...] = hb
        u_ref[...] = _dotf(hb, w_ref[...]).astype(BF16)

    return _rowcall(
        "ffn_up", body, nblk, nlb,
        [(x1, "row"), (mod, "stream"), (nw2, "full"), _wk(w_up)],
        [((r, 2 * D_FF), BF16, "row"), ((r, D), BF16, "row")],
    )


def _conv_gate_chunk(u_ref, up_ref, un_ref, cw_ref, j, first, last):
    res = []
    for half in range(2):
        c0 = half * D_FF + j * FF_CHUNK
        cs = slice(c0, c0 + FF_CHUNK)
        cur = u_ref[:, cs].astype(F32)
        dn, up = _shift_rows(cur, up_ref[HALO_BF16 - 1 : HALO_BF16, cs].astype(F32), un_ref[0:1, cs].astype(F32), first, last)
        val = dn * cw_ref[0:1, cs] + cur * cw_ref[1:2, cs] + up * cw_ref[2:3, cs] + cw_ref[3:4, cs]
        res.append((val, dn, cur, up, cs))
    return res


def ffn_down(u, cw, w_down, x1, mod, nblk, nlb):
    r = x1.shape[0]

    def body(u_ref, up_ref, un_ref, cw_ref, w_ref, x_ref, mod_ref, x2_ref, f_ref, gated):
        first, last, _ = _stream_edges(pl.program_id(0), nlb)
        for j in range(D_FF // FF_CHUNK):
            (a, _, _, _, _), (v, _, _, _, _) = _conv_gate_chunk(u_ref, up_ref, un_ref, cw_ref, j, first, last)
            gated[:, j * FF_CHUNK : (j + 1) * FF_CHUNK] = (a * _sigmoid(a) * v).astype(BF16)
        f = _dotf(gated[...], w_ref[...])
        f_ref[...] = f.astype(BF16)
        x2_ref[...] = x_ref[...] + _mrow(mod_ref, G2) * f

    return _rowcall(
        "ffn_down", body, nblk, nlb,
        [(u, "row"), (u, "prev16"), (u, "next16"), _wk(cw), _wk(w_down), (x1, "row"), (mod, "stream")],
        [((r, D), F32, "row"), ((r, D), BF16, "row")],
        scratch=[pltpu.VMEM((BM, D_FF), BF16)],
    )


def ffn_bwd1(dx2, f, u, cw, w_down, mod, nblk, nlb):
    r = dx2.shape[0]

    def body(dx_ref, f_ref, u_ref, up_ref, un_ref, cw_ref, w_ref, mod_ref, gated_ref, df_ref, duc_ref, dcw_ref, dmod_ref, dgated):
        i = pl.program_id(0)
        first, last, _ = _stream_edges(i, nlb)
        _acc_init(i, dcw_ref, dmod_ref)
        dx = dx_ref[...]
        df = (_mrow(mod_ref, G2) * dx).astype(BF16)
        df_ref[...] = df
        s = i // nlb
        dmod_ref[s, G2 : G2 + 1, :] += jnp.sum(dx * f_ref[...].astype(F32), axis=0, keepdims=True)
        dgated[...] = _dot_nt(df, w_ref[...])
        for j in range(D_FF // FF_CHUNK):
            (a, adn, acur, aup, acs), (v, vdn, vcur, vup, vcs) = _conv_gate_chunk(u_ref, up_ref, un_ref, cw_ref, j, first, last)
            sa = _sigmoid(a)
            gated_ref[:, j * FF_CHUNK : (j + 1) * FF_CHUNK] = (a * sa * v).astype(BF16)
            dg = dgated[:, j * FF_CHUNK : (j + 1) * FF_CHUNK]
            dv = dg * (a * sa)
            da = dg * v * (sa * (1.0 + a * (1.0 - sa)))
            for dval, dn, cur, up, cs in ((da, adn, acur, aup, acs), (dv, vdn, vcur, vup, vcs)):
                duc_ref[:, cs] = dval.astype(BF16)
                dcw_ref[0:1, cs] += jnp.sum(dval * dn, axis=0, keepdims=True)
                dcw_ref[1:2, cs] += jnp.sum(dval * cur, axis=0, keepdims=True)
                dcw_ref[2:3, cs] += jnp.sum(dval * up, axis=0, keepdims=True)
                dcw_ref[3:4, cs] += jnp.sum(dval, axis=0, keepdims=True)

    return _rowcall(
        "ffn_bwd1", body, nblk, nlb,
        [(dx2, "row"), (f, "row"), (u, "row"), (u, "prev16"), (u, "next16"), _wk(cw), _wk(w_down), (mod, "stream")],
        [((r, D_FF), BF16, "row"), ((r, D), BF16, "row"), ((r, 2 * D_FF), BF16, "row"),
         ((8, 2 * D_FF), F32, "acc"), ((2, 8, D), F32, "acc")],
        scratch=[pltpu.VMEM((BM, D_FF), F32)],
    )


def ffn_bwd3(duc, cw, w_up, x1, dx2, mod, nw2, nblk, nlb):
    r = dx2.shape[0]

    def body(d_ref, dp_ref, dn_ref, cw_ref, w_ref, x_ref, dx_ref, mod_ref, nw_ref, du_ref, dx1_ref, dnw_ref, dmod_ref):
        i = pl.program_id(0)
        first, last, _ = _stream_edges(i, nlb)
        _acc_init(i, dnw_ref, dmod_ref)
        for j in range(2 * D_FF // FF_CHUNK):
            cs = slice(j * FF_CHUNK, (j + 1) * FF_CHUNK)
            cur = d_ref[:, cs].astype(F32)
            dn, up = _shift_rows(cur, dp_ref[HALO_BF16 - 1 : HALO_BF16, cs].astype(F32), dn_ref[0:1, cs].astype(F32), first, last)
            du_ref[:, cs] = (up * cw_ref[0:1, cs] + cur * cw_ref[1:2, cs] + dn * cw_ref[2:3, cs]).astype(BF16)
        dh = _dot_nt(du_ref[...], w_ref[...])
        sc = _mrow(mod_ref, SC2)
        nw = nw_ref[...]
        _, n, xhat, rr = _norm_mod(x_ref[...], nw, _mrow(mod_ref, SH2), sc)
        dxn, dsh, dsc, dnw = _norm_mod_bwd(dh, n, xhat, rr, nw, sc)
        dx1_ref[...] = dx_ref[...] + dxn
        s = i // nlb
        dmod_ref[s, SH2 : SH2 + 1, :] += dsh
        dmod_ref[s, SC2 : SC2 + 1, :] += dsc
        dnw_ref[0:1, :] += dnw

    return _rowcall(
        "ffn_bwd3", body, nblk, nlb,
        [(duc, "row"), (duc, "prev16"), (duc, "next16"), _wk(cw), _wk(w_up), (x1, "row"), (dx2, "row"),
         (mod, "stream"), (nw2, "full")],
        [((r, 2 * D_FF), BF16, "row"), ((r, D), F32, "row"), ((8, D), F32, "acc"), ((2, 8, D), F32, "acc")],
    )


def matmul_tn(a, b, nblk, tn=None):
    k, n = a.shape[1], b.shape[1]
    rows = nblk * BM
    tr = 768 if rows % 768 == 0 else (1024 if rows % 1024 == 0 else BM)
    if tn is None:
        tn = n
        while k * tn * 4 > 6 * 1024 * 1024 and tn % 256 == 0:
            tn //= 2
    steps = rows // tr

    def body(a_ref, b_ref, o_ref, acc):
        t = pl.program_id(1)

        @pl.when(t == 0)
        def _():
            acc[...] = jnp.zeros(acc.shape, acc.dtype)

        acc[...] += _dot_tn(a_ref[...], b_ref[...])

        @pl.when(t == steps - 1)
        def _():
            o_ref[...] = acc[...].astype(o_ref.dtype)

    return _pcall(
        body,
        name="matmul_tn",
        grid=(n // tn, steps),
        in_specs=[pl.BlockSpec((tr, k), lambda j, t: (t, 0)), pl.BlockSpec((tr, tn), lambda j, t: (t, j))],
        out_specs=pl.BlockSpec((k, tn), lambda j, t: (0, j)),
        out_shape=jax.ShapeDtypeStruct((k, n), BF16),
        scratch_shapes=[pltpu.VMEM((k, tn), F32)],
        compiler_params=pltpu.CompilerParams(dimension_semantics=("parallel", "arbitrary"), vmem_limit_bytes=VMEM_LIMIT),
    )(a, b)


EXT = BM + 2 * HALO_F32


def _pool_positions(i, nlb, nrows, row0):
    is_ctx = i == nlb
    t = (i - jnp.where(is_ctx, nlb, 0)) * BM + row0 + lax.broadcasted_iota(jnp.int32, (nrows, 1), 0)
    return t, jnp.where(is_ctx, BM, nlb * BM)


def _pool_cnt(t, win, slen):
    return (jnp.minimum(t + win // 2, slen) - jnp.maximum(t - win // 2, 0)).astype(F32)


def _pool_fill_ext(ext, i, nlb, x_ref, xp_ref, xn_ref, mod_ref, nw_ref):
    first, last, _ = _stream_edges(i, nlb)
    sh, sc, nw = _mrow(mod_ref, SH1), _mrow(mod_ref, SC1), nw_ref[...]
    hcur, n, xhat, r = _norm_mod(x_ref[...], nw, sh, sc)
    ext[0:HALO_F32, :] = jnp.where(first, 0.0, _norm_mod(xp_ref[...], nw, sh, sc)[0])
    ext[HALO_F32 : HALO_F32 + BM, :] = hcur
    ext[HALO_F32 + BM :, :] = jnp.where(last, 0.0, _norm_mod(xn_ref[...], nw, sh, sc)[0])
    return n, xhat, r


def _window_sum(ref, cols, offs):
    acc = None
    for o in offs:
        v = ref[HALO_F32 + o : HALO_F32 + o + BM, cols]
        acc = v if acc is None else acc + v
    return acc


def _pool_diff(ext, g, win, t, slen):
    cols = slice(g * POOL_GROUP, (g + 1) * POOL_GROUP)
    ssum = _window_sum(ext, cols, range(-(win // 2), win // 2))
    return ssum / _pool_cnt(t, win, slen) - ext[HALO_F32 : HALO_F32 + BM, cols]


def pool_fwd(x, mod, nw1, pw, pbs, nblk, nlb):
    r = x.shape[0]

    def body(x_ref, xp_ref, xn_ref, mod_ref, nw_ref, pw_ref, pbs_ref, x1_ref, ypre_ref, ext):
        i = pl.program_id(0)
        _pool_fill_ext(ext, i, nlb, x_ref, xp_ref, xn_ref, mod_ref, nw_ref)
        t, slen = _pool_positions(i, nlb, BM, 0)
        for g, win in enumerate(POOL_WINDOWS):
            cols = slice(g * POOL_GROUP, (g + 1) * POOL_GROUP)
            diff = _pool_diff(ext, g, win, t, slen)
            ypre = _dotf(diff.astype(BF16), pw_ref[g]) + pbs_ref[0:1, cols]
            ypre_ref[:, cols] = ypre
            x1_ref[:, cols] = x_ref[:, cols] + mod_ref[0, G1 : G1 + 1, cols] * (ypre * pbs_ref[1:2, cols])

    return _rowcall(
        "pool_fwd", body, nblk, nlb,
        [(x, "row"), (x, "prev8"), (x, "next8"), (mod, "stream"), (nw1, "full"), _wk(pw), _wk(pbs)],
        [((r, D), F32, "row"), ((r, D), F32, "row")],
        scratch=[pltpu.VMEM((EXT, D), F32)],
    )


def pool_bwd(dx1, x, ypre, mod, nw1, pw, pbs, nblk, nlb):
    r = x.shape[0]

    def body(d_ref, dp_ref, dn_ref, x_ref, xp_ref, xn_ref, ypre_ref, mod_ref, nw_ref, pw_ref, pbs_ref,
             dx_ref, dpw_ref, dpbs_ref, dnw_ref, dmod_ref, ext, dext, eext, dh):
        i = pl.program_id(0)
        first, last, _ = _stream_edges(i, nlb)
        _acc_init(i, dpw_ref, dpbs_ref, dnw_ref, dmod_ref)
        n, xhat, rr = _pool_fill_ext(ext, i, nlb, x_ref, xp_ref, xn_ref, mod_ref, nw_ref)
        g1, scale = _mrow(mod_ref, G1), pbs_ref[1:2, :]
        dcur = d_ref[...]
        ypre = ypre_ref[...]
        s = i // nlb
        dmod_ref[s, G1 : G1 + 1, :] += jnp.sum(dcur * (ypre * scale), axis=0, keepdims=True)
        dy = g1 * dcur
        dpbs_ref[1:2, :] += jnp.sum(dy * ypre, axis=0, keepdims=True)
        dpbs_ref[0:1, :] += jnp.sum(dy * scale, axis=0, keepdims=True)
        gs = g1 * scale
        dext[0:HALO_F32, :] = jnp.where(first, 0.0, gs * dp_ref[...])
        dext[HALO_F32 : HALO_F32 + BM, :] = dy * scale
        dext[HALO_F32 + BM :, :] = jnp.where(last, 0.0, gs * dn_ref[...])
        t, slen = _pool_positions(i, nlb, BM, 0)
        text, _ = _pool_positions(i, nlb, EXT, -HALO_F32)
        for g, win in enumerate(POOL_WINDOWS):
            cols = slice(g * POOL_GROUP, (g + 1) * POOL_GROUP)
            diff = _pool_diff(ext, g, win, t, slen)
            dpre = dext[:, cols].astype(BF16)
            ddiff = _dot_nt(dpre, pw_ref[g])
            eext[...] = ddiff / jnp.maximum(_pool_cnt(text, win, slen), 1.0)
            dh[:, cols] = _window_sum(eext, slice(None), range(-(win // 2) + 1, win // 2 + 1)) - ddiff[HALO_F32 : HALO_F32 + BM, :]
            dpw_ref[g] += _dot_tn(diff.astype(BF16), dpre[HALO_F32 : HALO_F32 + BM, :])
        sc, nw = _mrow(mod_ref, SC1), nw_ref[...]
        dxn, dsh, dsc, dnw = _norm_mod_bwd(dh[...], n, xhat, rr, nw, sc)
        dx_ref[...] = dcur + dxn
        dmod_ref[s, SH1 : SH1 + 1, :] += dsh
        dmod_ref[s, SC1 : SC1 + 1, :] += dsc
        dnw_ref[0:1, :] += dnw

    return _rowcall(
        "pool_bwd", body, nblk, nlb,
        [(dx1, "row"), (dx1, "prev8"), (dx1, "next8"), (x, "row"), (x, "prev8"), (x, "next8"), (ypre, "row"),
         (mod, "stream"), (nw1, "full"), _wk(pw), _wk(pbs)],
        [((r, D), F32, "row"), ((4, POOL_GROUP, POOL_GROUP), F32, "acc"), ((8, D), F32, "acc"), ((8, D), F32, "acc"),
         ((2, 8, D), F32, "acc")],
        scratch=[pltpu.VMEM((EXT, D), F32), pltpu.VMEM((EXT, D), F32), pltpu.VMEM((EXT, POOL_GROUP), F32), pltpu.VMEM((BM, D), F32)],
    )


def rope_tables(s, l):
    rows = s // GRID_W
    row = jnp.broadcast_to(jnp.arange(rows)[:, None], (rows, GRID_W)).reshape(-1).astype(F32)
    col = jnp.broadcast_to(jnp.arange(GRID_W)[None, :], (rows, GRID_W)).reshape(-1).astype(F32)
    axis_dim = HD // 2
    inv = ROPE_THETA ** (-jnp.arange(0, axis_dim, 2, dtype=F32) / axis_dim)
    ar, ac = row[:, None] * inv, col[:, None] * inv
    cos = jnp.concatenate([jnp.cos(ar), jnp.cos(ar), jnp.cos(ac), jnp.cos(ac)], axis=-1)
    sin = jnp.concatenate([-jnp.sin(ar), jnp.sin(ar), -jnp.sin(ac), jnp.sin(ac)], axis=-1)
    return (jnp.concatenate([cos, jnp.ones((l, HD), F32)], axis=0), jnp.concatenate([sin, jnp.zeros((l, HD), F32)], axis=0))


def _partner(x):
    q = HD // 4
    lane = lax.broadcasted_iota(jnp.int32, x.shape, 1)
    return jnp.where((lane // q) % 2 == 0, pltpu.roll(x, HD - q, 1), pltpu.roll(x, q, 1))


def _head_norm(raw, gain):
    r = lax.rsqrt(jnp.mean(raw * raw, axis=-1, keepdims=True) + EPS)
    return raw * r, r


def qkv_fwd(x, mod, nw1, w_qkv, gains, cos_t, sin_t, nblk, nlb):
    r = x.shape[0]

    def body(x_ref, mod_ref, nw_ref, w_ref, g_ref, c_ref, s_ref, raw_ref, q_ref, k_ref, v_ref, h_ref):
        h = _norm_mod(x_ref[...], nw_ref[...], _mrow(mod_ref, SH1), _mrow(mod_ref, SC1))[0].astype(BF16)
        h_ref[...] = h
        raw_ref[...] = _dotf(h, w_ref[...])
        cos, sin = c_ref[...], s_ref[...]
        for j in range(ATTN_HEADS + ATTN_KV_HEADS):
            isq = j < ATTN_HEADS
            xn = _head_norm(raw_ref[:, j * HD : (j + 1) * HD], None)[0] * (g_ref[0:1, :] if isq else g_ref[1:2, :])
            rot = (xn * cos + _partner(xn) * sin).astype(BF16)
            if isq:
                q_ref[:, j * HD : (j + 1) * HD] = rot
            else:
                k_ref[:, (j - ATTN_HEADS) * HD : (j - ATTN_HEADS + 1) * HD] = rot
        v_ref[...] = raw_ref[:, NQ + NKV :].astype(BF16)

    return _rowcall(
        "qkv_fwd", body, nblk, nlb,
        [(x, "row"), (mod, "stream"), (nw1, "full"), (w_qkv, "full"), (gains, "full"), (cos_t, "row"), (sin_t, "row")],
        [((r, NQ + 2 * NKV), F32, "row"), ((r, NQ), BF16, "row"), ((r, NKV), BF16, "row"), ((r, NKV), BF16, "row"),
         ((r, D), BF16, "row")],
    )


def attn_fwd(q, k, v, nblk, nlb):
    r = q.shape[0]
    scale = HD ** -0.5

    def body(q_ref, k_ref, v_ref, o_ref, lse_ref):
        is_ctx = pl.program_id(0) == nlb
        hide = is_ctx & (lax.broadcasted_iota(jnp.int32, (1, r), 1) < nlb * BM)
        for kvh in range(ATTN_KV_HEADS):
            kh = k_ref[:, kvh * HD : (kvh + 1) * HD]
            vh = v_ref[:, kvh * HD : (kvh + 1) * HD]
            for g in range(ATTN_GROUP):
                cs = slice((kvh * ATTN_GROUP + g) * HD, (kvh * ATTN_GROUP + g + 1) * HD)
                s = jnp.where(hide, -1e30, _dot_nt(q_ref[:, cs], kh) * scale)
                m = jnp.max(s, axis=-1, keepdims=True)
                p = jnp.exp(s - m)
                l = jnp.sum(p, axis=-1, keepdims=True)
                o_ref[:, cs] = (_dotf(p.astype(BF16), vh) / l).astype(BF16)
                j = kvh * ATTN_GROUP + g
                lse_ref[:, j : j + 1] = m + jnp.log(l)

    return _rowcall(
        "attn_fwd", body, nblk, nlb,
        [(q, "row"), (k, "full"), (v, "full")],
        [((r, NQ), BF16, "row"), ((r, ATTN_HEADS), F32, "row")],
    )


def attn_out_fwd(o, w_o, x, mod, nblk, nlb):
    r = x.shape[0]

    def body(o_ref, w_ref, x_ref, mod_ref, x1_ref, y_ref):
        y = _dotf(o_ref[...], w_ref[...])
        y_ref[...] = y
        x1_ref[...] = x_ref[...] + _mrow(mod_ref, G1) * y

    return _rowcall(
        "attn_out_fwd", body, nblk, nlb,
        [(o, "row"), (w_o, "full"), (x, "row"), (mod, "stream")],
        [((r, D), F32, "row"), ((r, D), F32, "row")],
    )


def mix_out_bwd(name, dx1, y, w_out, mod, nblk, nlb):
    r = dx1.shape[0]
    kin = w_out.shape[0]

    def body(d_ref, y_ref, w_ref, mod_ref, dy_ref, do_ref, dmod_ref):
        i = pl.program_id(0)
        _acc_init(i, dmod_ref)
        d = d_ref[...]
        dmod_ref[i // nlb, G1 : G1 + 1, :] += jnp.sum(d * y_ref[...], axis=0, keepdims=True)
        dy = (_mrow(mod_ref, G1) * d).astype(BF16)
        dy_ref[...] = dy
        do_ref[...] = _dot_nt(dy, w_ref[...]).astype(do_ref.dtype)

    return _rowcall(
        name, body, nblk, nlb,
        [(dx1, "row"), (y, "row"), (w_out, "full"), (mod, "stream")],
        [((r, D), BF16, "row"), ((r, kin), BF16, "row"), ((2, 8, D), F32, "acc")],
    )


ATTN_KCHUNK = 11 * BM


def attn_bwd(q, k, v, o, do, lse, nblk, nlb):
    r = q.shape[0]
    scale = HD ** -0.5
    kc = ATTN_KCHUNK if r % ATTN_KCHUNK == 0 else BM
    nkc = r // kc

    def body(q_ref, k_ref, v_ref, o_ref, do_ref, lse_ref, dq_ref, dk_ref, dv_ref):
        i = pl.program_id(0)
        _acc_init(i, dk_ref, dv_ref)
        is_ctx = i == nlb
        for kvh in range(ATTN_KV_HEADS):
            ks = slice(kvh * HD, (kvh + 1) * HD)
            for g in range(ATTN_GROUP):
                j = kvh * ATTN_GROUP + g
                cs = slice(j * HD, (j + 1) * HD)
                qh, doh = q_ref[:, cs], do_ref[:, cs]
                delta = jnp.sum(doh.astype(F32) * o_ref[:, cs].astype(F32), axis=-1, keepdims=True)
                lse = lse_ref[:, j : j + 1]
                dq = jnp.zeros((BM, HD), F32)
                for c in range(nkc):
                    rs = slice(c * kc, (c + 1) * kc)
                    kh, vh = k_ref[rs, ks], v_ref[rs, ks]
                    hide = is_ctx & (c * kc + lax.broadcasted_iota(jnp.int32, (1, kc), 1) < nlb * BM)
                    p = jnp.where(hide, 0.0, jnp.exp(_dot_nt(qh, kh) * scale - lse))
                    ds = (p * (_dot_nt(doh, vh) - delta) * scale).astype(BF16)
                    dq = dq + _dotf(ds, kh)
                    dk_ref[rs, ks] += _dot_tn(ds, qh)
                    dv_ref[rs, ks] += _dot_tn(p.astype(BF16), doh)
                dq_ref[:, cs] = dq

    return _rowcall(
        "attn_bwd", body, nblk, nlb,
        [(q, "row"), (k, "full"), (v, "full"), (o, "row"), (do, "row"), (lse, "row")],
        [((r, NQ), F32, "row"), ((r, NKV), F32, "acc"), ((r, NKV), F32, "acc")],
    )


def qkv_bwd(dq, dk, dv, raw, gains, cos_t, sin_t, w_qkv, x, dx1, mod, nw1, nblk, nlb, ctx_dx_zero):
    r = x.shape[0]

    def body(dq_ref, dk_ref, dv_ref, raw_ref, g_ref, c_ref, s_ref, w_ref, x_ref, dx1_ref, mod_ref, nw_ref,
             dx_ref, draw_ref, dg_ref, dnw_ref, dmod_ref):
        i = pl.program_id(0)
        _acc_init(i, dg_ref, dnw_ref, dmod_ref)
        cos, sin = c_ref[...], s_ref[...]
        for j in range(ATTN_HEADS + ATTN_KV_HEADS):
            isq = j < ATTN_HEADS
            cs = slice(j * HD, (j + 1) * HD)
            dr = dq_ref[:, cs] if isq else dk_ref[:, (j - ATTN_HEADS) * HD : (j - ATTN_HEADS + 1) * HD]
            dxn = dr * cos + _partner(dr * sin)
            xhat, rr = _head_norm(raw_ref[:, cs], None)
            gi = 0 if isq else 1
            dg_ref[gi : gi + 1, :] += jnp.sum(dxn * xhat, axis=0, keepdims=True)
            dxhat = dxn * g_ref[gi : gi + 1, :]
            draw_ref[:, cs] = (rr * (dxhat - xhat * jnp.mean(dxhat * xhat, axis=-1, keepdims=True))).astype(BF16)
        draw_ref[:, NQ + NKV :] = dv_ref[...].astype(BF16)
        dh = _dot_nt(draw_ref[...], w_ref[...])
        sc, nw = _mrow(mod_ref, SC1), nw_ref[...]
        _, n, xhat, rr = _norm_mod(x_ref[...], nw, _mrow(mod_ref, SH1), sc)
        dxn, dsh, dsc, dnw = _norm_mod_bwd(dh, n, xhat, rr, nw, sc)
        dres = dx1_ref[...]
        if ctx_dx_zero:
            dres = jnp.where(i == nlb, 0.0, dres)
        dx_ref[...] = dres + dxn
        s = i // nlb
        dmod_ref[s, SH1 : SH1 + 1, :] += dsh
        dmod_ref[s, SC1 : SC1 + 1, :] += dsc
        dnw_ref[0:1, :] += dnw

    return _rowcall(
        "qkv_bwd", body, nblk, nlb,
        [(dq, "row"), (dk, "row"), (dv, "row"), (raw, "row"), (gains, "full"), (cos_t, "row"), (sin_t, "row"),
         (w_qkv, "full"), (x, "row"), (dx1, "row"), (mod, "stream"), (nw1, "full")],
        [((r, D), F32, "row"), ((r, NQ + 2 * NKV), BF16, "row"), ((8, HD), F32, "acc"), ((8, D), F32, "acc"),
         ((2, 8, D), F32, "acc")],
    )


RET_KSCALE = RET_DK ** -0.5


def ret_in_fwd(x, mod, nw1, w_in, nblk, nlb):
    r = x.shape[0]

    def body(x_ref, mod_ref, nw_ref, w_ref, q_ref, k_ref, v_ref, g_ref, h_ref):
        h = _norm_mod(x_ref[...], nw_ref[...], _mrow(mod_ref, SH1), _mrow(mod_ref, SC1))[0].astype(BF16)
        h_ref[...] = h
        q_ref[...] = _dotf(h, w_ref[:, 0:RNQ]).astype(BF16)
        k_ref[...] = (_dotf(h, w_ref[:, RNQ : 2 * RNQ]) * RET_KSCALE).astype(BF16)
        v_ref[...] = _dotf(h, w_ref[:, 2 * RNQ : 2 * RNQ + RNV]).astype(BF16)
        g_ref[...] = _dotf(h, w_ref[:, 2 * RNQ + RNV :]).astype(BF16)

    return _rowcall(
        "ret_in_fwd", body, nblk, nlb,
        [(x, "row"), (mod, "stream"), (nw1, "full"), (w_in, "full")],
        [((r, RNQ), BF16, "row"), ((r, RNQ), BF16, "row"), ((r, RNV), BF16, "row"), ((r, RNV), BF16, "row"), ((r, D), BF16, "row")],
    )


def _log_sigmoid(x):
    return jnp.minimum(x, 0.0) - jnp.log(1.0 + jnp.exp(-jnp.abs(x)))


def _ret_decays(lg, reverse):
    c = BM
    i = lax.broadcasted_iota(jnp.int32, (c, c), 0)
    j = lax.broadcasted_iota(jnp.int32, (c, c), 1)
    diff = (j - i) if reverse else (i - j)
    ediff = jnp.maximum(diff, 0).astype(F32)
    dm = jnp.where(diff >= 0, jnp.exp(ediff * lg), 0.0)
    rr = lax.broadcasted_iota(jnp.int32, (c, 1), 0).astype(F32)
    eq = (c - rr) if reverse else (rr + 1.0)
    ek = rr if reverse else (c - 1.0 - rr)
    return dm, ediff, jnp.exp(eq * lg), eq, jnp.exp(ek * lg), ek, jnp.exp(c * lg)


def _ret_chunk_index(nlb):
    return (lambda s: jnp.where(s == 0, nlb, s - 1)), (lambda s: jnp.where(s == 0, nlb, nlb - s))


def ret_scan_fwd(q, k, v, logit_b, nlb):
    r = q.shape[0]
    nb = nlb + 1
    fidx, bidx = _ret_chunk_index(nlb)

    def body(qf, kf, vf, qb, kb, vb, lg_ref, of_ref, ob_ref, rf_ref, rb_ref, stf, stb):
        s = pl.program_id(1)

        @pl.when(s == 0)
        def _():
            stf[...] = jnp.zeros(stf.shape, F32)
            stb[...] = jnp.zeros(stb.shape, F32)

        for d, (q_ref, k_ref, v_ref, o_ref, rs_ref, st) in enumerate(((qf, kf, vf, of_ref, rf_ref, stf), (qb, kb, vb, ob_ref, rb_ref, stb))):
            lg = _log_sigmoid(lg_ref[d, 0])[0:1, 0:1]
            dm, _, qd, _, kd, _, gc = _ret_decays(lg, d == 1)
            qq, kk, vv, st0 = q_ref[...], k_ref[...], v_ref[...], st[...]
            rs_ref[0, 0] = st0
            a = _dot_nt(qq, kk) * dm
            o = _dotf(a.astype(BF16), vv) + _dotf(qq, st0.astype(BF16)) * qd
            o_ref[...] = jnp.where(s == 0, 0.0, o)
            st[...] = st0 * gc + _dot_tn((kk.astype(F32) * kd).astype(BF16), vv)

    qspec = lambda f: pl.BlockSpec((BM, RET_DK), lambda h, s: (f(s), h))
    vspec = lambda f: pl.BlockSpec((BM, RET_DV), lambda h, s: (f(s), h))
    sspec = pl.BlockSpec((1, 1, RET_DK, RET_DV), lambda h, s: (h, s, 0, 0))
    return _pcall(
        body,
        name="ret_scan_fwd",
        grid=(RET_HEADS, nb),
        in_specs=[qspec(fidx), qspec(fidx), vspec(fidx), qspec(bidx), qspec(bidx), vspec(bidx),
                  pl.BlockSpec((2, 1, 8, 128), lambda h, s: (0, h, 0, 0))],
        out_specs=[vspec(fidx), vspec(bidx), sspec, sspec],
        out_shape=[jax.ShapeDtypeStruct((r, RNV), F32), jax.ShapeDtypeStruct((r, RNV), F32),
                   jax.ShapeDtypeStruct((RET_HEADS, nb, RET_DK, RET_DV), F32), jax.ShapeDtypeStruct((RET_HEADS, nb, RET_DK, RET_DV), F32)],
        scratch_shapes=[pltpu.VMEM((RET_DK, RET_DV), F32), pltpu.VMEM((RET_DK, RET_DV), F32)],
        compiler_params=pltpu.CompilerParams(dimension_semantics=("parallel", "arbitrary"), vmem_limit_bytes=VMEM_LIMIT),
    )(q, k, v, q, k, v, logit_b)


def _group_norm(o):
    mu = jnp.mean(o, axis=-1, keepdims=True)
    oc = o - mu
    rstd = lax.rsqrt(jnp.mean(oc * oc, axis=-1, keepdims=True) + EPS)
    return oc * rstd, rstd


def ret_out_fwd(o_f, o_b, g, gnw, w_out, x, mod, nblk, nlb):
    r = x.shape[0]

    def body(of_ref, ob_ref, g_ref, gn_ref, w_ref, x_ref, mod_ref, x1_ref, y_ref, z_ref):
        for hh in range(RET_HEADS):
            cs = slice(hh * RET_DV, (hh + 1) * RET_DV)
            yhat, _ = _group_norm(of_ref[:, cs] + ob_ref[:, cs])
            gg = g_ref[:, cs].astype(F32)
            z_ref[:, cs] = (gg * _sigmoid(gg) * (yhat * gn_ref[0:1, cs])).astype(BF16)
        y = _dotf(z_ref[...], w_ref[...])
        y_ref[...] = y
        x1_ref[...] = x_ref[...] + _mrow(mod_ref, G1) * y

    return _rowcall(
        "ret_out_fwd", body, nblk, nlb,
        [(o_f, "row"), (o_b, "row"), (g, "row"), (gnw, "full"), (w_out, "full"), (x, "row"), (mod, "stream")],
        [((r, D), F32, "row"), ((r, D), F32, "row"), ((r, RNV), BF16, "row")],
    )


def ret_out_bwd(dx1, y, o_f, o_b, g, gnw, w_out, mod, nblk, nlb):
    r = dx1.shape[0]

    def body(d_ref, y_ref, of_ref, ob_ref, g_ref, gn_ref, w_ref, mod_ref, dy_ref, do_ref, dg_ref, dgn_ref, dmod_ref, dz):
        i = pl.program_id(0)
        _acc_init(i, dgn_ref, dmod_ref)
        d = d_ref[...]
        dmod_ref[i // nlb, G1 : G1 + 1, :] += jnp.sum(d * y_ref[...], axis=0, keepdims=True)
        dy = (_mrow(mod_ref, G1) * d).astype(BF16)
        dy_ref[...] = dy
        dz[...] = _dot_nt(dy, w_ref[...])
        for hh in range(RET_HEADS):
            cs = slice(hh * RET_DV, (hh + 1) * RET_DV)
            yhat, rstd = _group_norm(of_ref[:, cs] + ob_ref[:, cs])
            gg = g_ref[:, cs].astype(F32)
            sg = _sigmoid(gg)
            gn = gn_ref[0:1, cs]
            dzz = dz[:, cs]
            dg_ref[:, cs] = (dzz * (yhat * gn) * (sg * (1.0 + gg * (1.0 - sg)))).astype(BF16)
            dyn = dzz * (gg * sg)
            dgn_ref[0:1, cs] += jnp.sum(dyn * yhat, axis=0, keepdims=True)
            dyh = dyn * gn
            do = rstd * (dyh - jnp.mean(dyh, axis=-1, keepdims=True) - yhat * jnp.mean(dyh * yhat, axis=-1, keepdims=True))
            do_ref[:, cs] = do.astype(BF16)

    return _rowcall(
        "ret_out_bwd", body, nblk, nlb,
        [(dx1, "row"), (y, "row"), (o_f, "row"), (o_b, "row"), (g, "row"), (gnw, "full"), (w_out, "full"), (mod, "stream")],
        [((r, D), BF16, "row"), ((r, RNV), BF16, "row"), ((r, RNV), BF16, "row"), ((8, RNV), F32, "acc"), ((2, 8, D), F32, "acc")],
        scratch=[pltpu.VMEM((BM, RNV), F32)],
    )


def ret_scan_bwd(q, k, v, do, st_f, st_b, logit_b, nlb):
    r = q.shape[0]
    nb = nlb + 1
    fidx, bidx = _ret_chunk_index(nlb)
    step = lambda t: nb - 1 - t

    def body(qf, kf, vf, dof, rf, qb, kb, vb, dob, rb, lg_ref,
             dqf, dkf, dvf, dqb, dkb, dvb, dlg_ref, drf, drb):
        t = pl.program_id(1)
        s = step(t)

        @pl.when(t == 0)
        def _():
            drf[...] = jnp.zeros(drf.shape, F32)
            drb[...] = jnp.zeros(drb.shape, F32)
            dlg_ref[...] = jnp.zeros(dlg_ref.shape, F32)

        dirs = ((qf, kf, vf, dof, rf, dqf, dkf, dvf, drf), (qb, kb, vb, dob, rb, dqb, dkb, dvb, drb))
        for d, (q_ref, k_ref, v_ref, do_ref, rs_ref, dq_ref, dk_ref, dv_ref, dr) in enumerate(dirs):
            lg = _log_sigmoid(lg_ref[d, 0])[0:1, 0:1]
            dm, ediff, qd, eq, kd, ek, gc = _ret_decays(lg, d == 1)
            qq, kk, vv = q_ref[...], k_ref[...], v_ref[...]
            dob16 = jnp.where(s == 0, jnp.zeros((), BF16), do_ref[...])
            do32 = dob16.astype(F32)
            st0 = rs_ref[0, 0]
            st16 = st0.astype(BF16)
            dr0 = dr[...]
            dr16 = dr0.astype(BF16)
            a = _dot_nt(qq, kk) * dm
            daf = _dot_nt(dob16, vv)
            ds = (daf * dm).astype(BF16)
            qr = _dotf(qq, st16)
            k32 = kk.astype(F32)
            kdec = (k32 * kd).astype(BF16)
            dv_ref[...] = (_dot_tn(a.astype(BF16), dob16) + _dotf(kdec, dr16)).astype(BF16)
            dq_ref[...] = (_dotf(ds, kk) + _dot_nt(dob16, st16) * qd).astype(BF16)
            vdr = _dot_nt(vv, dr16)
            dk_ref[...] = (_dot_tn(ds, qq) + vdr * kd).astype(BF16)
            tot = (jnp.sum(daf * a * ediff)
                   + jnp.sum(eq * qd * jnp.sum(do32 * qr, axis=-1, keepdims=True))
                   + jnp.sum(ek * kd * jnp.sum(k32 * vdr, axis=-1, keepdims=True))
                   + jnp.sum(BM * gc * jnp.sum(dr0 * st0, axis=-1, keepdims=True)))
            dlg_ref[d, 0] += tot
            dr[...] = gc * dr0 + _dot_tn(qq, (do32 * qd).astype(BF16))

        @pl.when(t == nb - 1)
        def _():
            dlg_ref[...] = dlg_ref[...] * _sigmoid(-lg_ref[...])

    qspec = lambda f: pl.BlockSpec((BM, RET_DK), lambda h, t: (f(step(t)), h))
    vspec = lambda f: pl.BlockSpec((BM, RET_DV), lambda h, t: (f(step(t)), h))
    sspec = pl.BlockSpec((1, 1, RET_DK, RET_DV), lambda h, t: (h, step(t), 0, 0))
    lspec = pl.BlockSpec((2, 1, 8, 128), lambda h, t: (0, h, 0, 0))
    sq, sv = jax.ShapeDtypeStruct((r, RNQ), BF16), jax.ShapeDtypeStruct((r, RNV), BF16)
    return _pcall(
        body,
        name="ret_scan_bwd",
        grid=(RET_HEADS, nb),
        in_specs=[qspec(fidx), qspec(fidx), vspec(fidx), vspec(fidx), sspec,
                  qspec(bidx), qspec(bidx), vspec(bidx), vspec(bidx), sspec, lspec],
        out_specs=[qspec(fidx), qspec(fidx), vspec(fidx), qspec(bidx), qspec(bidx), vspec(bidx), lspec],
        out_shape=[sq, sq, sv, sq, sq, sv, jax.ShapeDtypeStruct((2, RET_HEADS, 8, 128), F32)],
        scratch_shapes=[pltpu.VMEM((RET_DK, RET_DV), F32), pltpu.VMEM((RET_DK, RET_DV), F32)],
        compiler_params=pltpu.CompilerParams(dimension_semantics=("parallel", "arbitrary"), vmem_limit_bytes=VMEM_LIMIT),
    )(q, k, v, do, st_f, q, k, v, do, st_b, logit_b)


def ret_in_bwd(dqf, dkf, dvf, dqb, dkb, dvb, dgate, w_in, x, dx1, mod, nw1, nblk, nlb):
    r = x.shape[0]
    nin = 2 * RNQ + 2 * RNV

    def body(dqf_ref, dkf_ref, dvf_ref, dqb_ref, dkb_ref, dvb_ref, dg_ref, w_ref, x_ref, dx1_ref, mod_ref, nw_ref,
             dx_ref, din_ref, dnw_ref, dmod_ref):
        i = pl.program_id(0)
        _acc_init(i, dnw_ref, dmod_ref)
        is_ctx = i == nlb
        din_ref[:, 0:RNQ] = (dqf_ref[...].astype(F32) + dqb_ref[...].astype(F32)).astype(BF16)
        din_ref[:, RNQ : 2 * RNQ] = ((dkf_ref[...].astype(F32) + dkb_ref[...].astype(F32)) * RET_KSCALE).astype(BF16)
        din_ref[:, 2 * RNQ : 2 * RNQ + RNV] = (dvf_ref[...].astype(F32) + dvb_ref[...].astype(F32)).astype(BF16)
        din_ref[:, 2 * RNQ + RNV :] = jnp.where(is_ctx, jnp.zeros((), BF16), dg_ref[...])
        dh = _dot_nt(din_ref[...], w_ref[...])
        sc, nw = _mrow(mod_ref, SC1), nw_ref[...]
        _, n, xhat, rr = _norm_mod(x_ref[...], nw, _mrow(mod_ref, SH1), sc)
        dxn, dsh, dsc, dnw = _norm_mod_bwd(dh, n, xhat, rr, nw, sc)
        dx_ref[...] = jnp.where(is_ctx, 0.0, dx1_ref[...]) + dxn
        s = i // nlb
        dmod_ref[s, SH1 : SH1 + 1, :] += dsh
        dmod_ref[s, SC1 : SC1 + 1, :] += dsc
        dnw_ref[0:1, :] += dnw

    return _rowcall(
        "ret_in_bwd", body, nblk, nlb,
        [(dqf, "row"), (dkf, "row"), (dvf, "row"), (dqb, "row"), (dkb, "row"), (dvb, "row"), (dgate, "row"),
         (w_in, "full"), (x, "row"), (dx1, "row"), (mod, "stream"), (nw1, "full")],
        [((r, D), F32, "row"), ((r, nin), BF16, "row"), ((8, D), F32, "acc"), ((2, 8, D), F32, "acc")],
    )


def loss_head(xout, target, nlb):
    r = xout.shape[0]

    def body(x_ref, t_ref, dx_ref, l_ref):
        _acc_init(pl.program_id(0), l_ref)
        err = x_ref[...] - t_ref[...]
        dx_ref[...] = err * (1.0 / D)
        l_ref[...] += 0.5 * jnp.sum(jnp.mean(err * err, axis=-1, keepdims=True))

    return _rowcall(
        "loss_head", body, nlb, nlb,
        [(xout, "row"), (target, "row")],
        [((r, D), F32, "row"), ((8, 128), F32, "acc")],
    )


N_MIXERS = 3
POOL, ATTN, RET = range(3)


def _layer_plan(depth):
    plan = []
    for i in range(depth):
        kind = i % N_MIXERS
        ctx_out = any(k % N_MIXERS != POOL for k in range(i + 1, depth))
        plan.append((kind, i // N_MIXERS, ctx_out or kind != POOL, ctx_out))
    return plan


def local_step(xs, target, mods, w, nlb, depth):
    nb = nlb + 1
    plan = _layer_plan(depth)
    saved = []
    x = xs
    for i, (kind, j, ctx_in, ctx_out) in enumerate(plan):
        nmix = nb if ctx_out else nlb
        mod, nw1, nw2 = mods[i], w["nw"][i, 0:1], w["nw"][i, 1:2]
        sv = {"x": x}
        if kind == POOL:
            x1, sv["ypre"] = pool_fwd(x, mod, nw1, (w["pool_w"], j), (w["pbs"], j), nmix, nlb)
        elif kind == ATTN:
            assert ctx_out
            sv["raw"], sv["q"], sv["k"], sv["v"], sv["h"] = qkv_fwd(x, mod, nw1, w["w_qkv"], w["gains"], w["cos"], w["sin"], nb, nlb)
            sv["o"], sv["lse"] = attn_fwd(sv["q"], sv["k"], sv["v"], nb, nlb)
            x1, sv["y"] = attn_out_fwd(sv["o"], w["w_o"], x, mod, nb, nlb)
        else:
            assert ctx_in and not ctx_out
            sv["q"], sv["k"], sv["v"], sv["g"], sv["h"] = ret_in_fwd(x, mod, nw1, w["w_in"], nb, nlb)
            sv["o_f"], sv["o_b"], sv["st_f"], sv["st_b"] = ret_scan_fwd(sv["q"], sv["k"], sv["v"], w["logit_b"], nlb)
            x1, sv["y"], sv["z"] = ret_out_fwd(sv["o_f"], sv["o_b"], sv["g"], w["gnw"], w["w_out"], x, mod, nlb, nlb)
        sv["x1"] = x1
        sv["u"], sv["h2"] = ffn_up(x1, mod, nw2, (w["w_up"], i), nmix, nlb)
        x, sv["f"] = ffn_down(sv["u"], (w["cw"], i), (w["w_down"], i), x1, mod, nmix, nlb)
        saved.append(sv)

    dx, loss_tile = loss_head(x, target, nlb)
    g = {k: [None] * depth for k in ("w_up", "w_down", "dcw", "dnw1", "dnw2", "dmod")}
    for i in reversed(range(depth)):
        kind, j, ctx_in, ctx_out = plan[i]
        sv = saved[i]
        nmix = nb if ctx_out else nlb
        mod, nw1, nw2 = mods[i], w["nw"][i, 0:1], w["nw"][i, 1:2]
        gated, df, duc, g["dcw"][i], dmod = ffn_bwd1(dx, sv["f"], sv["u"], (w["cw"], i), (w["w_down"], i), mod, nmix, nlb)
        g["w_down"][i] = matmul_tn(gated, df, nmix)
        du, dx1, g["dnw2"][i], dm = ffn_bwd3(duc, (w["cw"], i), (w["w_up"], i), sv["x1"], dx, mod, nw2, nmix, nlb)
        dmod = dmod + dm
        g["w_up"][i] = matmul_tn(sv["h2"], du, nmix)
        if kind == POOL:
            dx, dpw, dpbs, g["dnw1"][i], dm = pool_bwd(dx1, sv["x"], sv["ypre"], mod, nw1, (w["pool_w"], j), (w["pbs"], j), nmix, nlb)
            g.setdefault("pool_w", {})[j] = dpw
            g.setdefault("dpbs", {})[j] = dpbs
        elif kind == ATTN:
            dy, do, dm1 = mix_out_bwd("attn_out_bwd", dx1, sv["y"], w["w_o"], mod, nb, nlb)
            g["w_o"] = matmul_tn(sv["o"], dy, nb)
            dq, dk, dv = attn_bwd(sv["q"], sv["k"], sv["v"], sv["o"], do, sv["lse"], nb, nlb)
            dx, draw, g["dgains"], g["dnw1"][i], dm = qkv_bwd(
                dq, dk, dv, sv["raw"], w["gains"], w["cos"], w["sin"], w["w_qkv"], sv["x"], dx1, mod, nw1, nb, nlb, False)
            g["w_qkv"] = matmul_tn(sv["h"], draw, nb)
            dm = dm + dm1
        else:
            dy, do, dgate, g["dgnw"], dm1 = ret_out_bwd(dx1, sv["y"], sv["o_f"], sv["o_b"], sv["g"], w["gnw"], w["w_out"], mod, nlb, nlb)
            g["w_out"] = matmul_tn(sv["z"], dy, nlb)
            dqf, dkf, dvf, dqb, dkb, dvb, g["dlogit"] = ret_scan_bwd(sv["q"], sv["k"], sv["v"], do, sv["st_f"], sv["st_b"], w["logit_b"], nlb)
            dx, din, g["dnw1"][i], dm = ret_in_bwd(dqf, dkf, dvf, dqb, dkb, dvb, dgate, w["w_in"], sv["x"], dx1, mod, nw1, nb, nlb)
            g["w_in"] = matmul_tn(sv["h"], din, nb)
            dm = dm + dm1
        g["dmod"][i] = dmod + dm
    return loss_tile, dx, g


MESH_ID = pl.DeviceIdType.MESH
CHIP_FLIPS = ((1, 0), (0, 1), (1, 1))


def _pos():
    return lax.axis_index("x"), lax.axis_index("y"), lax.axis_index("c")


def _flip(v, b):
    return 1 - v if b else v


def small_all_gather(name, x):
    rows, n = x.shape

    def body(x_ref, out_ref, send_sems, recv_sems, local_sem):
        mx, my, mc = _pos()
        me = 4 * mx + 2 * my + mc
        mine = pltpu.make_async_copy(x_ref, out_ref.at[me], local_sem)
        mine.start()
        sends, peers = [], []
        for kk in range(1, N_DEV):
            peer = (_flip(mx, (kk >> 2) & 1), _flip(my, (kk >> 1) & 1), _flip(mc, kk & 1))
            cp = pltpu.make_async_remote_copy(src_ref=x_ref, dst_ref=out_ref.at[me], send_sem=send_sems.at[kk - 1],
                                              recv_sem=recv_sems.at[kk - 1], device_id=peer, device_id_type=MESH_ID)
            cp.start()
            sends.append(cp)
            peers.append(peer)
        for kk, peer in enumerate(peers):
            pidx = 4 * peer[0] + 2 * peer[1] + peer[2]
            pltpu.make_async_remote_copy(src_ref=x_ref, dst_ref=out_ref.at[pidx], send_sem=send_sems.at[kk],
                                         recv_sem=recv_sems.at[kk], device_id=peer, device_id_type=MESH_ID).wait_recv()
        for cp in sends:
            cp.wait_send()
        mine.wait()

    return _pcall(
        body,
        name=name,
        out_shape=jax.ShapeDtypeStruct((N_DEV, rows, n), x.dtype),
        in_specs=[pl.BlockSpec(memory_space=pltpu.VMEM)],
        out_specs=pl.BlockSpec(memory_space=pltpu.VMEM),
        scratch_shapes=[pltpu.SemaphoreType.DMA((N_DEV - 1,)), pltpu.SemaphoreType.DMA((N_DEV - 1,)), pltpu.SemaphoreType.DMA],
        compiler_params=pltpu.CompilerParams(vmem_limit_bytes=VMEM_LIMIT),
    )(x)


def _hbm_exchange(name, ins, out_shapes, plan):
    n_in = len(ins)
    probe_local, probe_remote = plan([None] * n_in, [None] * len(out_shapes), probe=True)

    def body(*refs):
        in_refs, out_refs = refs[:n_in], refs[n_in : n_in + len(out_shapes)]
        send_sems, recv_sems, local_sems = refs[n_in + len(out_shapes) :]
        local, remote = plan(in_refs, out_refs, probe=False)
        lcs = [pltpu.make_async_copy(s, d, local_sems.at[k]) for k, (s, d) in enumerate(local)]
        for cp in lcs:
            cp.start()
        rcs = []
        for k, (s, d, peer, _) in enumerate(remote):
            cp = pltpu.make_async_remote_copy(src_ref=s, dst_ref=d, send_sem=send_sems.at[k], recv_sem=recv_sems.at[k],
                                              device_id=peer, device_id_type=MESH_ID)
            cp.start()
            rcs.append(cp)
        for k, (s, _, peer, here) in enumerate(remote):
            pltpu.make_async_remote_copy(src_ref=s, dst_ref=here, send_sem=send_sems.at[k], recv_sem=recv_sems.at[k],
                                         device_id=peer, device_id_type=MESH_ID).wait_recv()
        for cp in rcs:
            cp.wait_send()
        for cp in lcs:
            cp.wait()

    return _pcall(
        body,
        name=name,
        out_shape=list(out_shapes),
        in_specs=[pl.BlockSpec(memory_space=pl.ANY)] * n_in,
        out_specs=[pl.BlockSpec(memory_space=pl.ANY)] * len(out_shapes),
        scratch_shapes=[pltpu.SemaphoreType.DMA((max(probe_remote, 1),)), pltpu.SemaphoreType.DMA((max(probe_remote, 1),)),
                        pltpu.SemaphoreType.DMA((max(probe_local, 1),))],
    )(*ins)


def _at_axis(ref, axis, start, size):
    return ref.at[tuple(pl.ds(start, size) if a == axis else slice(None) for a in range(len(ref.shape)))]


def gather_weights(shards, axes):
    sizes = [s.shape[a] for s, a in zip(shards, axes)]
    out_shapes = [jax.ShapeDtypeStruct(tuple(d * N_CHIP if k == a else d for k, d in enumerate(s.shape)), s.dtype)
                  for s, a in zip(shards, axes)]

    def plan(in_refs, out_refs, probe):
        if probe:
            return len(shards), len(shards) * len(CHIP_FLIPS)
        mx, my, mc = _pos()
        q = 2 * mx + my
        local, remote = [], []
        for s, o, a, n in zip(in_refs, out_refs, axes, sizes):
            local.append((s, _at_axis(o, a, q * n, n)))
            for fx, fy in CHIP_FLIPS:
                px, py = _flip(mx, fx), _flip(my, fy)
                remote.append((s, _at_axis(o, a, q * n, n), (px, py, mc), _at_axis(o, a, (2 * px + py) * n, n)))
        return local, remote

    return _hbm_exchange("gather_weights", shards, out_shapes, plan)


def scatter_grads(grads, axes):
    flat, where = [], []
    out_shapes = []
    for i, (gl, a) in enumerate(zip(grads, axes)):
        shp = list(gl[0].shape)
        shp[a] //= N_CHIP
        out_shapes.append(jax.ShapeDtypeStruct((N_CHIP, len(gl)) + tuple(shp), gl[0].dtype))
        for layer, arr in enumerate(gl):
            flat.append(arr)
            where.append((i, layer, a, shp[a]))

    def plan(in_refs, out_refs, probe):
        if probe:
            return len(flat), len(flat) * len(CHIP_FLIPS)
        mx, my, mc = _pos()
        local, remote = [], []
        for s, (i, layer, a, n) in zip(in_refs, where):
            o = out_refs[i]
            local.append((_at_axis(s, a, (2 * mx + my) * n, n), o.at[N_CHIP - 1, layer]))
            for slot, (fx, fy) in enumerate(CHIP_FLIPS):
                px, py = _flip(mx, fx), _flip(my, fy)
                remote.append((_at_axis(s, a, (2 * px + py) * n, n), o.at[slot, layer], (px, py, mc), o.at[slot, layer]))
        return local, remote

    return _hbm_exchange("scatter_grads", flat, out_shapes, plan)


def sibling_swap(parts):
    def plan(in_refs, out_refs, probe):
        if probe:
            return 0, len(parts)
        mx, my, mc = _pos()
        return [], [(s, o, (mx, my, 1 - mc), o) for s, o in zip(in_refs, out_refs)]

    return _hbm_exchange("sibling_swap", parts, [jax.ShapeDtypeStruct(p.shape, p.dtype) for p in parts], plan)


EW_ROWS = 256


def _ew_call(name, fn, ins, n_out):
    rows, cols = ins[0].shape[-2:]
    tr = EW_ROWS if rows % EW_ROWS == 0 else rows

    def body(*refs):
        outs = fn(*[r[...] for r in refs[: len(ins)]])
        for o_ref, o in zip(refs[len(ins) :], outs):
            o_ref[...] = o

    def spec(a):
        if a.ndim == 3:
            return pl.BlockSpec((a.shape[0], tr, cols), lambda i: (0, i, 0))
        return pl.BlockSpec((tr, cols), lambda i: (i, 0))

    return _pcall(
        body,
        name=name,
        grid=(rows // tr,),
        in_specs=[spec(a) for a in ins],
        out_specs=[pl.BlockSpec((tr, cols), lambda i: (i, 0))] * n_out,
        out_shape=[jax.ShapeDtypeStruct((rows, cols), F32)] * n_out,
        compiler_params=pltpu.CompilerParams(dimension_semantics=("parallel",), vmem_limit_bytes=VMEM_LIMIT),
    )(*ins)


def _adamw(w, g, m, v):
    m = ADAM_B1 * m + (1.0 - ADAM_B1) * g
    v = ADAM_B2 * v + (1.0 - ADAM_B2) * (g * g)
    m_hat = m / (1.0 - ADAM_B1 ** ADAM_STEP)
    v_hat = v / (1.0 - ADAM_B2 ** ADAM_STEP)
    return -ADAM_LR * (m_hat / (jnp.sqrt(v_hat) + ADAM_EPS) + ADAM_WD * w), m, v


def sum_slots(name, landing):
    def fn(l):
        acc = l[l.shape[0] - 1].astype(F32)
        for k in range(l.shape[0] - 1):
            acc = acc + l[k].astype(F32)
        return (acc,)

    return _ew_call(name, fn, [landing], 1)[0]


def adamw_pair(name, w, m, v, p, ps):
    def fn(w, m, v, p, ps):
        g = p + ps
        return (g,) + _adamw(w, g, m, v)

    return _ew_call(name, fn, [w, m, v, p, ps], 4)


def adamw_one(name, w, m, v, g):
    return _ew_call(name, lambda w, m, v, g: _adamw(w, g, m, v), [w, m, v, g], 3)


def reduce_devices(name, x):
    def fn(a):
        acc = a[0]
        for k in range(1, a.shape[0]):
            acc = acc + a[k]
        return (acc,)

    return _ew_call(name, fn, [x], 1)[0]


ADA_ROWS = 16
ADA_CTX = N_DEV


def ada_fwd(s9, ada_w, ada_b):
    depth, _, n = ada_w.shape

    def body(s_ref, w_ref, b_ref, o_ref):
        s = s_ref[...]
        o_ref[...] = _dotf((s * _sigmoid(s)).astype(BF16), w_ref[...].astype(BF16)) + b_ref[...]

    return _pcall(
        body,
        name="ada_fwd",
        grid=(depth,),
        in_specs=[pl.BlockSpec((ADA_ROWS, D), lambda i: (0, 0)), pl.BlockSpec((None, D, n), lambda i: (i, 0, 0)),
                  pl.BlockSpec((None, 1, n), lambda i: (i, 0, 0))],
        out_specs=pl.BlockSpec((None, ADA_ROWS, n), lambda i: (i, 0, 0)),
        out_shape=jax.ShapeDtypeStruct((depth, ADA_ROWS, n), F32),
        compiler_params=pltpu.CompilerParams(dimension_semantics=("arbitrary",), vmem_limit_bytes=VMEM_LIMIT),
    )(s9, ada_w, ada_b)


def ada_bwd(s9, dm, ada_w):
    depth, _, n = ada_w.shape

    def body(s_ref, dm_ref, w_ref, gw_ref, ds_ref):
        _acc_init(pl.program_id(0), ds_ref)
        s = s_ref[...]
        dmb = dm_ref[...].astype(BF16)
        gw_ref[...] = _dot_tn((s * _sigmoid(s)).astype(BF16), dmb)
        ds_ref[...] += _dot_nt(dmb, w_ref[...].astype(BF16))

    return _pcall(
        body,
        name="ada_bwd",
        grid=(depth,),
        in_specs=[pl.BlockSpec((ADA_ROWS, D), lambda i: (0, 0)), pl.BlockSpec((None, ADA_ROWS, n), lambda i: (i, 0, 0)),
                  pl.BlockSpec((None, D, n), lambda i: (i, 0, 0))],
        out_specs=[pl.BlockSpec((None, D, n), lambda i: (i, 0, 0)), pl.BlockSpec((ADA_ROWS, D), lambda i: (0, 0))],
        out_shape=[jax.ShapeDtypeStruct((depth, D, n), F32), jax.ShapeDtypeStruct((ADA_ROWS, D), F32)],
        compiler_params=pltpu.CompilerParams(dimension_semantics=("arbitrary",), vmem_limit_bytes=VMEM_LIMIT),
    )(s9, dm, ada_w)


def cctx_grad(parts, c_ctx):
    def body(p_ref, c_ref, o_ref):
        acc = p_ref[0, ADA_CTX : ADA_CTX + 1, :]
        for chip in range(1, N_CHIP):
            acc = acc + p_ref[2 * chip, ADA_CTX : ADA_CTX + 1, :]
        c = c_ref[...]
        sg = _sigmoid(c)
        o_ref[...] = acc * (sg * (1.0 + c * (1.0 - sg)))

    return _pcall(body, name="cctx_grad", out_shape=jax.ShapeDtypeStruct((1, D), F32))(parts, c_ctx)


def _pack(arrs):
    flat = jnp.concatenate([a.astype(F32).reshape(-1) for a in arrs])
    rows = -(-flat.shape[0] // (8 * D)) * 8
    return jnp.pad(flat, (0, rows * D - flat.shape[0])).reshape(rows, D)


def _unpack(slab, shapes):
    lead = slab.shape[:-2]
    flat = slab.reshape(lead + (-1,))
    out, off = [], 0
    for shp in shapes:
        n = 1
        for d in shp:
            n *= d
        out.append(flat[..., off : off + n].reshape(lead + tuple(shp)))
        off += n
    return out


def _unshard(per_dev, axis):
    return jnp.concatenate([per_dev[2 * chip] for chip in range(N_CHIP)], axis=axis)


BIG = (("pool_w", 1), ("attn_w_qkv", 1), ("attn_w_o", 0), ("ret_w_in", 1), ("ret_w_out", 0), ("ffn_w_up", 1), ("ffn_w_down", 0))
WEIGHTS = ("c_ctx", "ada_w", "ada_b", "norm_w", "pool_w", "pool_b", "pool_scale", "attn_w_qkv", "attn_q_gain", "attn_k_gain",
           "attn_w_o", "ret_w_in", "ret_decay_logit", "ret_gn_w", "ret_w_out", "ffn_w_up", "ffn_conv_w", "ffn_conv_b", "ffn_w_down")
SMALL = tuple(n for n in WEIGHTS if n != "ada_w" and n not in dict(BIG))
SMALL_SHARD_AXIS = {"norm_w": 2, "pool_b": 1, "pool_scale": 1, "ret_gn_w": 1, "ffn_conv_w": 2}


def kernel(x, c, ctx, c_ctx, ada_w, ada_b, norm_w, pool_w, pool_b, pool_scale, attn_w_qkv, attn_q_gain, attn_k_gain, attn_w_o, ret_w_in, ret_decay_logit, ret_gn_w, ret_w_out, ffn_w_up, ffn_conv_w, ffn_conv_b, ffn_w_down, loss_target, m_c_ctx, m_ada_w, m_ada_b, m_norm_w, m_pool_w, m_pool_b, m_pool_scale, m_attn_w_qkv, m_attn_q_gain, m_attn_k_gain, m_attn_w_o, m_ret_w_in, m_ret_decay_logit, m_ret_gn_w, m_ret_w_out, m_ffn_w_up, m_ffn_conv_w, m_ffn_conv_b, m_ffn_w_down, v_c_ctx, v_ada_w, v_ada_b, v_norm_w, v_pool_w, v_pool_b, v_pool_scale, v_attn_w_qkv, v_attn_q_gain, v_attn_k_gain, v_attn_w_o, v_ret_w_in, v_ret_decay_logit, v_ret_gn_w, v_ret_w_out, v_ffn_w_up, v_ffn_conv_w, v_ffn_conv_b, v_ffn_w_down):
    P = dict(zip(WEIGHTS, (c_ctx, ada_w, ada_b, norm_w, pool_w, pool_b, pool_scale, attn_w_qkv, attn_q_gain, attn_k_gain, attn_w_o,
                           ret_w_in, ret_decay_logit, ret_gn_w, ret_w_out, ffn_w_up, ffn_conv_w, ffn_conv_b, ffn_w_down)))
    M = dict(zip(WEIGHTS, (m_c_ctx, m_ada_w, m_ada_b, m_norm_w, m_pool_w, m_pool_b, m_pool_scale, m_attn_w_qkv, m_attn_q_gain,
                           m_attn_k_gain, m_attn_w_o, m_ret_w_in, m_ret_decay_logit, m_ret_gn_w, m_ret_w_out, m_ffn_w_up,
                           m_ffn_conv_w, m_ffn_conv_b, m_ffn_w_down)))
    V = dict(zip(WEIGHTS, (v_c_ctx, v_ada_w, v_ada_b, v_norm_w, v_pool_w, v_pool_b, v_pool_scale, v_attn_w_qkv, v_attn_q_gain,
                           v_attn_k_gain, v_attn_w_o, v_ret_w_in, v_ret_decay_logit, v_ret_gn_w, v_ret_w_out, v_ffn_w_up,
                           v_ffn_conv_w, v_ffn_conv_b, v_ffn_w_down)))
    depth, s, l = ada_w.shape[0], x.shape[1], ctx.shape[1]
    assert l == BM and s % BM == 0 and s % GRID_W == 0
    nlb = s // BM
    n_pool = pool_w.shape[0]
    mx, my, mc = _pos()
    chip, dev = 2 * mx + my, 4 * mx + 2 * my + mc
    nada = ada_w.shape[2]

    sharded = [n for n in SMALL if n in SMALL_SHARD_AXIS]
    got = small_all_gather("gather_small", _pack([c[0]] + [P[n] for n in sharded]))
    got = _unpack(got, [(D,)] + [P[n].shape for n in sharded])
    c_all = got[0]
    full = {n: _unshard(g_, SMALL_SHARD_AXIS[n]) for n, g_ in zip(sharded, got[1:])}

    s9 = jnp.concatenate([c_all, c_ctx[None, :], jnp.zeros((ADA_ROWS - N_DEV - 1, D), F32)], axis=0)
    ada_b_mine = lax.dynamic_slice_in_dim(ada_b, chip * nada, nada, axis=1)[:, None, :]
    mod_part = ada_fwd(s9, ada_w, ada_b_mine)
    mod_all = _unshard(small_all_gather("gather_mod", mod_part.reshape(depth * ADA_ROWS, nada)), 1).reshape(depth, ADA_ROWS, 6, D)
    mod_mine = lax.dynamic_index_in_dim(mod_all, dev, axis=1, keepdims=False)
    mods_all = jnp.pad(jnp.stack([mod_mine, mod_all[:, ADA_CTX]], axis=1), ((0, 0), (0, 0), (0, 2), (0, 0)))
    mods = [mods_all[i] for i in range(depth)]

    gathered = gather_weights([P[n].astype(BF16) for n, _ in BIG], [a + 1 for _, a in BIG])
    W = dict(zip([n for n, _ in BIG], gathered))
    w = {
        "nw": full["norm_w"],
        "pool_w": W["pool_w"],
        "pbs": jnp.concatenate([full["pool_b"][:, None], full["pool_scale"][:, None], jnp.zeros((n_pool, 6, D), F32)], axis=1),
        "w_qkv": W["attn_w_qkv"][0], "w_o": W["attn_w_o"][0],
        "gains": jnp.concatenate([attn_q_gain, attn_k_gain, jnp.zeros((6, HD), F32)], axis=0),
        "w_in": W["ret_w_in"][0], "w_out": W["ret_w_out"][0], "gnw": full["ret_gn_w"],
        "logit_b": jnp.broadcast_to(ret_decay_logit[0][:, :, None, None], (2, RET_HEADS, 8, 128)),
        "w_up": W["ffn_w_up"], "w_down": W["ffn_w_down"],
        "cw": jnp.concatenate([full["ffn_conv_w"], ffn_conv_b[:, None, :], jnp.zeros((depth, 4, 2 * D_FF), F32)], axis=1),
    }
    w["cos"], w["sin"] = rope_tables(s, l)

    xs = jnp.concatenate([x[0], ctx[0]], axis=0)
    loss_tile, dxs, g = local_step(xs, loss_target[0], mods, w, nlb, depth)
    loss = lax.psum(loss_tile[0, 0], MESH_AXES)
    grad_x = dxs[:s][None]

    small_shapes = [(depth, 2, 8, D), (depth, 2, D), (n_pool, 2, D), (2, HD), (2, RET_HEADS), (RNV,), (depth, 4, 2 * D_FF)]
    slab = _pack([
        jnp.stack(g["dmod"]),
        jnp.stack([jnp.stack([g["dnw1"][i][0], g["dnw2"][i][0]]) for i in range(depth)]),
        jnp.stack([g["dpbs"][j][0:2] for j in range(n_pool)]),
        g["dgains"][0:2], g["dlogit"][:, :, 0, 0], g["dgnw"][0], jnp.stack([g["dcw"][i][0:4] for i in range(depth)]),
    ])
    slabs = small_all_gather("gather_small_grads", slab)
    dmod_dev = _unpack(slabs, small_shapes[:1])[0]
    t_dmod, t_nw, t_pbs, t_gains, t_logit, t_gnw, t_cw = _unpack(reduce_devices("reduce_small_grads", slabs), small_shapes)

    def cols(a):
        return lax.dynamic_slice_in_dim(a, chip * nada, nada, axis=a.ndim - 1)

    dm_lat = jnp.swapaxes(cols(dmod_dev[:, :, 0, :6].reshape(N_DEV, depth, 6 * D)), 0, 1)
    dm_ctx = cols(t_dmod[:, 1, :6].reshape(depth, 1, 6 * D))
    dm = jnp.concatenate([dm_lat, dm_ctx, jnp.zeros((depth, ADA_ROWS - N_DEV - 1, nada), F32)], axis=1)
    g_ada_w, ds9 = ada_bwd(s9, dm, ada_w)
    g_c_ctx = cctx_grad(small_all_gather("gather_dcctx", ds9), c_ctx[None, :])[0]

    def mine(a, name):
        n = P[name].shape[SMALL_SHARD_AXIS[name]]
        return lax.dynamic_slice_in_dim(a, chip * n, n, axis=SMALL_SHARD_AXIS[name])

    G = {
        "c_ctx": g_c_ctx,
        "ada_b": (t_dmod[:, 0, :6] + t_dmod[:, 1, :6]).reshape(depth, 6 * D),
        "norm_w": mine(t_nw, "norm_w"),
        "pool_b": mine(t_pbs[:, 0], "pool_b"), "pool_scale": mine(t_pbs[:, 1], "pool_scale"),
        "attn_q_gain": t_gains[0:1], "attn_k_gain": t_gains[1:2],
        "ret_decay_logit": t_logit[None], "ret_gn_w": mine(t_gnw[None], "ret_gn_w"),
        "ffn_conv_w": mine(t_cw[:, 0:3], "ffn_conv_w"), "ffn_conv_b": t_cw[:, 3],
    }
    sw, sg, sm, sv = (_pack([d_[n] for n in SMALL]) for d_ in (P, G, M, V))
    outs = adamw_one("adamw_small", sw, sm, sv, sg)
    D_, NM, NV = ({n: a for n, a in zip(SMALL, _unpack(o, [P[n].shape for n in SMALL]))} for o in outs)

    flat2 = lambda a: a.reshape(-1, a.shape[-1])
    G["ada_w"] = g_ada_w
    o3 = adamw_one("adamw_ada", flat2(ada_w), flat2(M["ada_w"]), flat2(V["ada_w"]), flat2(g_ada_w))
    D_["ada_w"], NM["ada_w"], NV["ada_w"] = (o.reshape(ada_w.shape) for o in o3)

    per_layer = {
        "pool_w": [g["pool_w"][j].astype(BF16) for j in range(n_pool)],
        "attn_w_qkv": [g["w_qkv"]], "attn_w_o": [g["w_o"]], "ret_w_in": [g["w_in"]], "ret_w_out": [g["w_out"]],
        "ffn_w_up": g["w_up"], "ffn_w_down": g["w_down"],
    }
    landing = scatter_grads([per_layer[n] for n, _ in BIG], [a for _, a in BIG])
    partial = [sum_slots("sum_" + n, lnd.reshape(N_CHIP, -1, lnd.shape[-1])) for (n, _), lnd in zip(BIG, landing)]
    theirs = sibling_swap(partial)
    for (n, _), p, ps in zip(BIG, partial, theirs):
        o4 = adamw_pair("adamw_" + n, flat2(P[n]), flat2(M[n]), flat2(V[n]), p, ps)
        G[n], D_[n], NM[n], NV[n] = (o.reshape(P[n].shape) for o in o4)

    return (loss, grad_x, *[G[n] for n in WEIGHTS], *[D_[n] for n in WEIGHTS], *[NM[n] for n in WEIGHTS], *[NV[n] for n in WEIGHTS])
```

```python
import functools

import jax
import jax.numpy as jnp
from jax import lax
from jax.experimental import pallas as pl
from jax.experimental.pallas import tpu as pltpu

F32 = jnp.float32
BF16 = jnp.bfloat16

D = 1024
BM = 256
EPS = 1e-6
POOL_WINDOWS = (2, 4, 8, 16)
POOL_GROUP = D // 4
ATTN_HEADS = 8
ATTN_KV_HEADS = 2
HD = D // ATTN_HEADS
ATTN_GROUP = ATTN_HEADS // ATTN_KV_HEADS
NQ = ATTN_HEADS * HD
NKV = ATTN_KV_HEADS * HD
GRID_W = 64
ROPE_THETA = 10000.0
RET_HEADS = 4
RET_DK = D // RET_HEADS
RET_DV = 2 * D // RET_HEADS
RNQ = RET_HEADS * RET_DK
RNV = RET_HEADS * RET_DV
D_FF = 2816
FF_CHUNK = 256
ADAM_LR, ADAM_B1, ADAM_B2, ADAM_EPS, ADAM_WD, ADAM_STEP = 0.001, 0.9, 0.999, 1e-08, 0.01, 10
HALO_F32 = 8
HALO_BF16 = 16
VMEM_LIMIT = 60 * 1024 * 1024

MESH_AXES = ("x", "y", "c")
N_DEV = 8
N_CHIP = 4


def _pcall(body, **kw):
    return pl.pallas_call(body, **kw)


def _spec(shape, kind, nlb, nblk):
    nd = len(shape)
    if kind == "row":
        return pl.BlockSpec((BM, shape[1]), lambda i: (i, 0))
    if kind == "full":
        return pl.BlockSpec(tuple(shape), lambda i: (0,) * nd, pipeline_mode=pl.Buffered(1))
    if isinstance(kind, tuple) and kind[0] == "fullat":
        return pl.BlockSpec((None,) + tuple(shape[1:]), lambda i: (kind[1],) + (0,) * (nd - 1), pipeline_mode=pl.Buffered(1))
    if kind == "acc":
        return pl.BlockSpec(tuple(shape), lambda i: (0,) * nd)
    if kind == "any":
        return pl.BlockSpec(memory_space=pl.ANY)
    if kind == "stream":
        return pl.BlockSpec((1,) + tuple(shape[1:]), lambda i: (i // nlb,) + (0,) * (nd - 1))
    if kind in ("prev8", "prev16"):
        hb = int(kind[4:])
        return pl.BlockSpec((hb, shape[1]), lambda i: (jnp.maximum(i * (BM // hb) - 1, 0), 0))
    if kind in ("next8", "next16"):
        hb = int(kind[4:])
        last = nblk * BM // hb - 1
        return pl.BlockSpec((hb, shape[1]), lambda i: (jnp.minimum((i + 1) * (BM // hb), last), 0))
    raise ValueError(kind)


def _rowcall(name, body, nblk, nlb, ins, outs, scratch=()):
    return _pcall(
        body,
        name=name,
        grid=(nblk,),
        in_specs=[_spec(a.shape, k, nlb, nblk) for a, k in ins],
        out_specs=[_spec(s, k, nlb, nblk) for s, _, k in outs],
        out_shape=[jax.ShapeDtypeStruct(s, d) for s, d, _ in outs],
        scratch_shapes=list(scratch),
        compiler_params=pltpu.CompilerParams(dimension_semantics=("arbitrary",), vmem_limit_bytes=VMEM_LIMIT),
    )(*[a for a, _ in ins])


def _wk(w):
    return (w[0], ("fullat", w[1])) if isinstance(w, tuple) else (w, "full")


def _stream_edges(i, nlb):
    is_ctx = i == nlb
    return (i == 0) | is_ctx, (i == nlb - 1) | is_ctx, is_ctx


def _dotf(a, b):
    return jnp.dot(a, b, preferred_element_type=F32)


def _dot_nt(a, b):
    return lax.dot_general(a, b, (((1,), (1,)), ((), ())), preferred_element_type=F32)


def _dot_tn(a, b):
    return lax.dot_general(a, b, (((0,), (0,)), ((), ())), preferred_element_type=F32)


def _sigmoid(x):
    return 1.0 / (1.0 + jnp.exp(-x))


def _norm_mod(x, nw, sh, sc):
    r = lax.rsqrt(jnp.mean(x * x, axis=-1, keepdims=True) + EPS)
    xhat = x * r
    n = xhat * nw
    return n * (1.0 + sc) + sh, n, xhat, r


def _norm_mod_bwd(dh, n, xhat, r, nw, sc):
    dsh = jnp.sum(dh, axis=0, keepdims=True)
    dsc = jnp.sum(dh * n, axis=0, keepdims=True)
    dn = dh * (1.0 + sc)
    dnw = jnp.sum(dn * xhat, axis=0, keepdims=True)
    dxhat = dn * nw
    dx = r * (dxhat - xhat * jnp.mean(dxhat * xhat, axis=-1, keepdims=True))
    return dx, dsh, dsc, dnw


def _acc_init(i, *refs):
    @pl.when(i == 0)
    def _():
        for r in refs:
            r[...] = jnp.zeros(r.shape, r.dtype)


SH1, SC1, G1, SH2, SC2, G2 = range(6)


def _mrow(mod_ref, k):
    return mod_ref[0, k : k + 1, :]


def shift_select():
    return jnp.concatenate([jnp.eye(BM, k=-1, dtype=BF16), jnp.eye(BM, k=1, dtype=BF16)], axis=0)


def _shift_rows(x_ref, xp_ref, xn_ref, cs, first, last, sel_ref, sh):
    xb = x_ref[:, cs]
    sh[...] = _dotf(sel_ref[...], xb)
    sh[0:1, :] = jnp.where(first, 0.0, xp_ref[HALO_BF16 - 1 : HALO_BF16, cs].astype(F32))
    sh[2 * BM - 1 : 2 * BM, :] = jnp.where(last, 0.0, xn_ref[0:1, cs].astype(F32))
    return sh[0:BM, :], xb.astype(F32), sh[BM : 2 * BM, :]


SHIFT_SCRATCH = pltpu.VMEM((2, 2 * BM, FF_CHUNK), F32)


def ffn_up(x1, mod, nw2, w_up, nblk, nlb):
    r = x1.shape[0]

    def body(x_ref, mod_ref, nw_ref, w_ref, u_ref, h_ref):
        h, _, _, _ = _norm_mod(x_ref[...], nw_ref[...], _mrow(mod_ref, SH2), _mrow(mod_ref, SC2))
        hb = h.astype(BF16)
        h_ref[...] = hb
        u_ref[...] = _dotf(hb, w_ref[...]).astype(BF16)

    return _rowcall(
        "ffn_up", body, nblk, nlb,
        [(x1, "row"), (mod, "stream"), (nw2, "full"), _wk(w_up)],
        [((r, 2 * D_FF), BF16, "row"), ((r, D), BF16, "row")],
    )


def _conv_gate_chunk(u_ref, up_ref, un_ref, cw_ref, j, first, last, sel_ref, sh):
    res = []
    for half in range(2):
        c0 = half * D_FF + j * FF_CHUNK
        cs = slice(c0, c0 + FF_CHUNK)
        dn, cur, up = _shift_rows(u_ref, up_ref, un_ref, cs, first, last, sel_ref, sh.at[half])
        val = dn * cw_ref[0:1, cs] + cur * cw_ref[1:2, cs] + up * cw_ref[2:3, cs] + cw_ref[3:4, cs]
        res.append((val, dn, cur, up, cs))
    return res


def ffn_down(u, cw, w_down, x1, mod, sel, nblk, nlb):
    r = x1.shape[0]

    def body(u_ref, up_ref, un_ref, cw_ref, w_ref, x_ref, mod_ref, sel_ref, x2_ref, f_ref, gated, sh):
        first, last, _ = _stream_edges(pl.program_id(0), nlb)
        for j in range(D_FF // FF_CHUNK):
            (a, _, _, _, _), (v, _, _, _, _) = _conv_gate_chunk(u_ref, up_ref, un_ref, cw_ref, j, first, last, sel_ref, sh)
            gated[:, j * FF_CHUNK : (j + 1) * FF_CHUNK] = (a * _sigmoid(a) * v).astype(BF16)
        f = _dotf(gated[...], w_ref[...])
        f_ref[...] = f.astype(BF16)
        x2_ref[...] = x_ref[...] + _mrow(mod_ref, G2) * f

    return _rowcall(
        "ffn_down", body, nblk, nlb,
        [(u, "row"), (u, "prev16"), (u, "next16"), _wk(cw), _wk(w_down), (x1, "row"), (mod, "stream"), (sel, "full")],
        [((r, D), F32, "row"), ((r, D), BF16, "row")],
        scratch=[pltpu.VMEM((BM, D_FF), BF16), SHIFT_SCRATCH],
    )


def ffn_bwd1(dx2, f, u, cw, w_down, mod, sel, dep, nblk, nlb):
    r = dx2.shape[0]

    def body(dx_ref, f_ref, u_ref, up_ref, un_ref, cw_ref, w_ref, mod_ref, sel_ref, dep_ref, gated_ref, df_ref, duc_ref, dcw_ref, dmod_ref,
             dgated, sh):
        i = pl.program_id(0)
        first, last, _ = _stream_edges(i, nlb)
        _acc_init(i, dcw_ref, dmod_ref)
        dx = dx_ref[...]
        df = (_mrow(mod_ref, G2) * dx).astype(BF16)
        df_ref[...] = df
        s = i // nlb
        dmod_ref[s, G2 : G2 + 1, :] += jnp.sum(dx * f_ref[...].astype(F32), axis=0, keepdims=True)
        dgated[...] = _dot_nt(df, w_ref[...])
        for j in range(D_FF // FF_CHUNK):
            (a, adn, acur, aup, acs), (v, vdn, vcur, vup, vcs) = _conv_gate_chunk(u_ref, up_ref, un_ref, cw_ref, j, first, last, sel_ref, sh)
            sa = _sigmoid(a)
            gated_ref[:, j * FF_CHUNK : (j + 1) * FF_CHUNK] = (a * sa * v).astype(BF16)
            dg = dgated[:, j * FF_CHUNK : (j + 1) * FF_CHUNK]
            dv = dg * (a * sa)
            da = dg * v * (sa * (1.0 + a * (1.0 - sa)))
            for dval, dn, cur, up, cs in ((da, adn, acur, aup, acs), (dv, vdn, vcur, vup, vcs)):
                duc_ref[:, cs] = dval.astype(BF16)
                dcw_ref[0:1, cs] += jnp.sum(dval * dn, axis=0, keepdims=True)
                dcw_ref[1:2, cs] += jnp.sum(dval * cur, axis=0, keepdims=True)
                dcw_ref[2:3, cs] += jnp.sum(dval * up, axis=0, keepdims=True)
                dcw_ref[3:4, cs] += jnp.sum(dval, axis=0, keepdims=True)

    return _rowcall(
        "ffn_bwd1", body, nblk, nlb,
        [(dx2, "row"), (f, "row"), (u, "row"), (u, "prev16"), (u, "next16"), _wk(cw), _wk(w_down), (mod, "stream"), (sel, "full"), (dep, "any")],
        [((r, D_FF), BF16, "row"), ((r, D), BF16, "row"), ((r, 2 * D_FF), BF16, "row"),
         ((8, 2 * D_FF), F32, "acc"), ((2, 8, D), F32, "acc")],
        scratch=[pltpu.VMEM((BM, D_FF), F32), SHIFT_SCRATCH],
    )


def ffn_bwd3(duc, cw, w_up, x1, dx2, mod, nw2, sel, nblk, nlb):
    r = dx2.shape[0]

    def body(d_ref, dp_ref, dn_ref, cw_ref, w_ref, x_ref, dx_ref, mod_ref, nw_ref, sel_ref, du_ref, dx1_ref, dnw_ref, dmod_ref, sh):
        i = pl.program_id(0)
        first, last, _ = _stream_edges(i, nlb)
        _acc_init(i, dnw_ref, dmod_ref)
        for j in range(2 * D_FF // FF_CHUNK):
            cs = slice(j * FF_CHUNK, (j + 1) * FF_CHUNK)
            dn, cur, up = _shift_rows(d_ref, dp_ref, dn_ref, cs, first, last, sel_ref, sh.at[j % 2])
            du_ref[:, cs] = (up * cw_ref[0:1, cs] + cur * cw_ref[1:2, cs] + dn * cw_ref[2:3, cs]).astype(BF16)
        dh = _dot_nt(du_ref[...], w_ref[...])
        sc = _mrow(mod_ref, SC2)
        nw = nw_ref[...]
        _, n, xhat, rr = _norm_mod(x_ref[...], nw, _mrow(mod_ref, SH2), sc)
        dxn, dsh, dsc, dnw = _norm_mod_bwd(dh, n, xhat, rr, nw, sc)
        dx1_ref[...] = dx_ref[...] + dxn
        s = i // nlb
        dmod_ref[s, SH2 : SH2 + 1, :] += dsh
        dmod_ref[s, SC2 : SC2 + 1, :] += dsc
        dnw_ref[0:1, :] += dnw

    return _rowcall(
        "ffn_bwd3", body, nblk, nlb,
        [(duc, "row"), (duc, "prev16"), (duc, "next16"), _wk(cw), _wk(w_up), (x1, "row"), (dx2, "row"),
         (mod, "stream"), (nw2, "full"), (sel, "full")],
        [((r, 2 * D_FF), BF16, "row"), ((r, D), F32, "row"), ((8, D), F32, "acc"), ((2, 8, D), F32, "acc")],
        scratch=[SHIFT_SCRATCH],
    )


def matmul_tn(a, b, nblk, tn=None):
    k, n = a.shape[1], b.shape[1]
    rows = nblk * BM
    tr = 768 if rows % 768 == 0 else (1024 if rows % 1024 == 0 else BM)
    if tn is None:
        tn = n
        while k * tn * 4 > 6 * 1024 * 1024 and tn % 256 == 0:
            tn //= 2
    steps = rows // tr

    def body(a_ref, b_ref, o_ref, acc):
        t = pl.program_id(1)

        @pl.when(t == 0)
        def _():
            acc[...] = jnp.zeros(acc.shape, acc.dtype)

        acc[...] += _dot_tn(a_ref[...], b_ref[...])

        @pl.when(t == steps - 1)
        def _():
            o_ref[...] = acc[...].astype(o_ref.dtype)

    return _pcall(
        body,
        name="matmul_tn",
        grid=(n // tn, steps),
        in_specs=[pl.BlockSpec((tr, k), lambda j, t: (t, 0)), pl.BlockSpec((tr, tn), lambda j, t: (t, j))],
        out_specs=pl.BlockSpec((k, tn), lambda j, t: (0, j)),
        out_shape=jax.ShapeDtypeStruct((k, n), BF16),
        scratch_shapes=[pltpu.VMEM((k, tn), F32)],
        compiler_params=pltpu.CompilerParams(dimension_semantics=("parallel", "arbitrary"), vmem_limit_bytes=VMEM_LIMIT),
    )(a, b)


EXT = BM + 2 * HALO_F32


def _pool_positions(i, nlb, nrows, row0):
    is_ctx = i == nlb
    t = (i - jnp.where(is_ctx, nlb, 0)) * BM + row0 + lax.broadcasted_iota(jnp.int32, (nrows, 1), 0)
    return t, jnp.where(is_ctx, BM, nlb * BM)


def _pool_cnt(t, win, slen):
    return (jnp.minimum(t + win // 2, slen) - jnp.maximum(t - win // 2, 0)).astype(F32)


def _pool_fill_ext(ext, i, nlb, x_ref, xp_ref, xn_ref, mod_ref, nw_ref):
    first, last, _ = _stream_edges(i, nlb)
    sh, sc, nw = _mrow(mod_ref, SH1), _mrow(mod_ref, SC1), nw_ref[...]
    hcur, n, xhat, r = _norm_mod(x_ref[...], nw, sh, sc)
    ext[0:HALO_F32, :] = jnp.where(first, 0.0, _norm_mod(xp_ref[...], nw, sh, sc)[0])
    ext[HALO_F32 : HALO_F32 + BM, :] = hcur
    ext[HALO_F32 + BM :, :] = jnp.where(last, 0.0, _norm_mod(xn_ref[...], nw, sh, sc)[0])
    return n, xhat, r


def _window_sum(ref, cols, offs):
    acc = None
    for o in offs:
        v = ref[HALO_F32 + o : HALO_F32 + o + BM, cols]
        acc = v if acc is None else acc + v
    return acc


def _pool_diff(ext, g, win, t, slen):
    cols = slice(g * POOL_GROUP, (g + 1) * POOL_GROUP)
    ssum = _window_sum(ext, cols, range(-(win // 2), win // 2))
    return ssum / _pool_cnt(t, win, slen) - ext[HALO_F32 : HALO_F32 + BM, cols]


def pool_fwd(x, mod, nw1, pw, pbs, nblk, nlb):
    r = x.shape[0]

    def body(x_ref, xp_ref, xn_ref, mod_ref, nw_ref, pw_ref, pbs_ref, x1_ref, ypre_ref, ext):
        i = pl.program_id(0)
        _pool_fill_ext(ext, i, nlb, x_ref, xp_ref, xn_ref, mod_ref, nw_ref)
        t, slen = _pool_positions(i, nlb, BM, 0)
        for g, win in enumerate(POOL_WINDOWS):
            cols = slice(g * POOL_GROUP, (g + 1) * POOL_GROUP)
            diff = _pool_diff(ext, g, win, t, slen)
            ypre = _dotf(diff.astype(BF16), pw_ref[g]) + pbs_ref[0:1, cols]
            ypre_ref[:, cols] = ypre
            x1_ref[:, cols] = x_ref[:, cols] + mod_ref[0, G1 : G1 + 1, cols] * (ypre * pbs_ref[1:2, cols])

    return _rowcall(
        "pool_fwd", body, nblk, nlb,
        [(x, "row"), (x, "prev8"), (x, "next8"), (mod, "stream"), (nw1, "full"), _wk(pw), _wk(pbs)],
        [((r, D), F32, "row"), ((r, D), F32, "row")],
        scratch=[pltpu.VMEM((EXT, D), F32)],
    )


def pool_bwd(dx1, x, ypre, mod, nw1, pw, pbs, nblk, nlb):
    r = x.shape[0]

    def body(d_ref, dp_ref, dn_ref, x_ref, xp_ref, xn_ref, ypre_ref, mod_ref, nw_ref, pw_ref, pbs_ref,
             dx_ref, dpw_ref, dpbs_ref, dnw_ref, dmod_ref, ext, dext, eext, dh):
        i = pl.program_id(0)
        first, last, _ = _stream_edges(i, nlb)
        _acc_init(i, dpw_ref, dpbs_ref, dnw_ref, dmod_ref)
        n, xhat, rr = _pool_fill_ext(ext, i, nlb, x_ref, xp_ref, xn_ref, mod_ref, nw_ref)
        g1, scale = _mrow(mod_ref, G1), pbs_ref[1:2, :]
        dcur = d_ref[...]
        ypre = ypre_ref[...]
        s = i // nlb
        dmod_ref[s, G1 : G1 + 1, :] += jnp.sum(dcur * (ypre * scale), axis=0, keepdims=True)
        dy = g1 * dcur
        dpbs_ref[1:2, :] += jnp.sum(dy * ypre, axis=0, keepdims=True)
        dpbs_ref[0:1, :] += jnp.sum(dy * scale, axis=0, keepdims=True)
        gs = g1 * scale
        dext[0:HALO_F32, :] = jnp.where(first, 0.0, gs * dp_ref[...])
        dext[HALO_F32 : HALO_F32 + BM, :] = dy * scale
        dext[HALO_F32 + BM :, :] = jnp.where(last, 0.0, gs * dn_ref[...])
        t, slen = _pool_positions(i, nlb, BM, 0)
        text, _ = _pool_positions(i, nlb, EXT, -HALO_F32)
        for g, win in enumerate(POOL_WINDOWS):
            cols = slice(g * POOL_GROUP, (g + 1) * POOL_GROUP)
            diff = _pool_diff(ext, g, win, t, slen)
            dpre = dext[:, cols].astype(BF16)
            ddiff = _dot_nt(dpre, pw_ref[g])
            eext[...] = ddiff / jnp.maximum(_pool_cnt(text, win, slen), 1.0)
            dh[:, cols] = _window_sum(eext, slice(None), range(-(win // 2) + 1, win // 2 + 1)) - ddiff[HALO_F32 : HALO_F32 + BM, :]
            dpw_ref[g] += _dot_tn(diff.astype(BF16), dpre[HALO_F32 : HALO_F32 + BM, :])
        sc, nw = _mrow(mod_ref, SC1), nw_ref[...]
        dxn, dsh, dsc, dnw = _norm_mod_bwd(dh[...], n, xhat, rr, nw, sc)
        dx_ref[...] = dcur + dxn
        dmod_ref[s, SH1 : SH1 + 1, :] += dsh
        dmod_ref[s, SC1 : SC1 + 1, :] += dsc
        dnw_ref[0:1, :] += dnw

    return _rowcall(
        "pool_bwd", body, nblk, nlb,
        [(dx1, "row"), (dx1, "prev8"), (dx1, "next8"), (x, "row"), (x, "prev8"), (x, "next8"), (ypre, "row"),
         (mod, "stream"), (nw1, "full"), _wk(pw), _wk(pbs)],
        [((r, D), F32, "row"), ((4, POOL_GROUP, POOL_GROUP), F32, "acc"), ((8, D), F32, "acc"), ((8, D), F32, "acc"),
         ((2, 8, D), F32, "acc")],
        scratch=[pltpu.VMEM((EXT, D), F32), pltpu.VMEM((EXT, D), F32), pltpu.VMEM((EXT, POOL_GROUP), F32), pltpu.VMEM((BM, D), F32)],
    )


def rope_tables(s, l):
    rows = s // GRID_W
    row = jnp.broadcast_to(jnp.arange(rows)[:, None], (rows, GRID_W)).reshape(-1).astype(F32)
    col = jnp.broadcast_to(jnp.arange(GRID_W)[None, :], (rows, GRID_W)).reshape(-1).astype(F32)
    axis_dim = HD // 2
    inv = ROPE_THETA ** (-jnp.arange(0, axis_dim, 2, dtype=F32) / axis_dim)
    ar, ac = row[:, None] * inv, col[:, None] * inv
    cos = jnp.concatenate([jnp.cos(ar), jnp.cos(ar), jnp.cos(ac), jnp.cos(ac)], axis=-1)
    sin = jnp.concatenate([-jnp.sin(ar), jnp.sin(ar), -jnp.sin(ac), jnp.sin(ac)], axis=-1)
    return (jnp.concatenate([cos, jnp.ones((l, HD), F32)], axis=0), jnp.concatenate([sin, jnp.zeros((l, HD), F32)], axis=0))


def _partner(x):
    q = HD // 4
    lane = lax.broadcasted_iota(jnp.int32, x.shape, 1)
    return jnp.where((lane // q) % 2 == 0, pltpu.roll(x, HD - q, 1), pltpu.roll(x, q, 1))


def _head_norm(raw, gain):
    r = lax.rsqrt(jnp.mean(raw * raw, axis=-1, keepdims=True) + EPS)
    return raw * r, r


def qkv_fwd(x, mod, nw1, w_qkv, gains, cos_t, sin_t, nblk, nlb):
    r = x.shape[0]

    def body(x_ref, mod_ref, nw_ref, w_ref, g_ref, c_ref, s_ref, raw_ref, q_ref, k_ref, v_ref, h_ref):
        h = _norm_mod(x_ref[...], nw_ref[...], _mrow(mod_ref, SH1), _mrow(mod_ref, SC1))[0].astype(BF16)
        h_ref[...] = h
        raw_ref[...] = _dotf(h, w_ref[...])
        cos, sin = c_ref[...], s_ref[...]
        for j in range(ATTN_HEADS + ATTN_KV_HEADS):
            isq = j < ATTN_HEADS
            xn = _head_norm(raw_ref[:, j * HD : (j + 1) * HD], None)[0] * (g_ref[0:1, :] if isq else g_ref[1:2, :])
            rot = (xn * cos + _partner(xn) * sin).astype(BF16)
            if isq:
                q_ref[:, j * HD : (j + 1) * HD] = rot
            else:
                k_ref[:, (j - ATTN_HEADS) * HD : (j - ATTN_HEADS + 1) * HD] = rot
        v_ref[...] = raw_ref[:, NQ + NKV :].astype(BF16)

    return _rowcall(
        "qkv_fwd", body, nblk, nlb,
        [(x, "row"), (mod, "stream"), (nw1, "full"), (w_qkv, "full"), (gains, "full"), (cos_t, "row"), (sin_t, "row")],
        [((r, NQ + 2 * NKV), F32, "row"), ((r, NQ), BF16, "row"), ((r, NKV), BF16, "row"), ((r, NKV), BF16, "row"),
         ((r, D), BF16, "row")],
    )


def attn_fwd(q, k, v, nblk, nlb):
    r = q.shape[0]
    scale = HD ** -0.5

    def body(q_ref, k_ref, v_ref, o_ref, lse_ref):
        is_ctx = pl.program_id(0) == nlb
        hide = is_ctx & (lax.broadcasted_iota(jnp.int32, (1, r), 1) < nlb * BM)
        for kvh in range(ATTN_KV_HEADS):
            kh = k_ref[:, kvh * HD : (kvh + 1) * HD]
            vh = v_ref[:, kvh * HD : (kvh + 1) * HD]
            for g in range(ATTN_GROUP):
                cs = slice((kvh * ATTN_GROUP + g) * HD, (kvh * ATTN_GROUP + g + 1) * HD)
                s = jnp.where(hide, -1e30, _dot_nt(q_ref[:, cs], kh) * scale)
                m = jnp.max(s, axis=-1, keepdims=True)
                p = jnp.exp(s - m)
                l = jnp.sum(p, axis=-1, keepdims=True)
                o_ref[:, cs] = (_dotf(p.astype(BF16), vh) / l).astype(BF16)
                j = kvh * ATTN_GROUP + g
                lse_ref[:, j : j + 1] = m + jnp.log(l)

    return _rowcall(
        "attn_fwd", body, nblk, nlb,
        [(q, "row"), (k, "full"), (v, "full")],
        [((r, NQ), BF16, "row"), ((r, ATTN_HEADS), F32, "row")],
    )


def attn_out_fwd(o, w_o, x, mod, nblk, nlb):
    r = x.shape[0]

    def body(o_ref, w_ref, x_ref, mod_ref, x1_ref, y_ref):
        y = _dotf(o_ref[...], w_ref[...])
        y_ref[...] = y
        x1_ref[...] = x_ref[...] + _mrow(mod_ref, G1) * y

    return _rowcall(
        "attn_out_fwd", body, nblk, nlb,
        [(o, "row"), (w_o, "full"), (x, "row"), (mod, "stream")],
        [((r, D), F32, "row"), ((r, D), F32, "row")],
    )


def mix_out_bwd(name, dx1, y, w_out, mod, nblk, nlb):
    r = dx1.shape[0]
    kin = w_out.shape[0]

    def body(d_ref, y_ref, w_ref, mod_ref, dy_ref, do_ref, dmod_ref):
        i = pl.program_id(0)
        _acc_init(i, dmod_ref)
        d = d_ref[...]
        dmod_ref[i // nlb, G1 : G1 + 1, :] += jnp.sum(d * y_ref[...], axis=0, keepdims=True)
        dy = (_mrow(mod_ref, G1) * d).astype(BF16)
        dy_ref[...] = dy
        do_ref[...] = _dot_nt(dy, w_ref[...]).astype(do_ref.dtype)

    return _rowcall(
        name, body, nblk, nlb,
        [(dx1, "row"), (y, "row"), (w_out, "full"), (mod, "stream")],
        [((r, D), BF16, "row"), ((r, kin), BF16, "row"), ((2, 8, D), F32, "acc")],
    )


ATTN_KCHUNK = 11 * BM


def attn_bwd(q, k, v, o, do, lse, nblk, nlb):
    r = q.shape[0]
    scale = HD ** -0.5
    kc = ATTN_KCHUNK if r % ATTN_KCHUNK == 0 else BM
    nkc = r // kc

    def body(q_ref, k_ref, v_ref, o_ref, do_ref, lse_ref, dq_ref, dk_ref, dv_ref):
        i = pl.program_id(0)
        _acc_init(i, dk_ref, dv_ref)
        is_ctx = i == nlb
        for kvh in range(ATTN_KV_HEADS):
            ks = slice(kvh * HD, (kvh + 1) * HD)
            for g in range(ATTN_GROUP):
                j = kvh * ATTN_GROUP + g
                cs = slice(j * HD, (j + 1) * HD)
                qh, doh = q_ref[:, cs], do_ref[:, cs]
                delta = jnp.sum(doh.astype(F32) * o_ref[:, cs].astype(F32), axis=-1, keepdims=True)
                lse = lse_ref[:, j : j + 1]
                dq = jnp.zeros((BM, HD), F32)
                for c in range(nkc):
                    rs = slice(c * kc, (c + 1) * kc)
                    kh, vh = k_ref[rs, ks], v_ref[rs, ks]
                    hide = is_ctx & (c * kc + lax.broadcasted_iota(jnp.int32, (1, kc), 1) < nlb * BM)
                    p = jnp.where(hide, 0.0, jnp.exp(_dot_nt(qh, kh) * scale - lse))
                    ds = (p * (_dot_nt(doh, vh) - delta) * scale).astype(BF16)
                    dq = dq + _dotf(ds, kh)
                    dk_ref[rs, ks] += _dot_tn(ds, qh)
                    dv_ref[rs, ks] += _dot_tn(p.astype(BF16), doh)
                dq_ref[:, cs] = dq

    return _rowcall(
        "attn_bwd", body, nblk, nlb,
        [(q, "row"), (k, "full"), (v, "full"), (o, "row"), (do, "row"), (lse, "row")],
        [((r, NQ), F32, "row"), ((r, NKV), F32, "acc"), ((r, NKV), F32, "acc")],
    )


def qkv_bwd(dq, dk, dv, raw, gains, cos_t, sin_t, w_qkv, x, dx1, mod, nw1, nblk, nlb, ctx_dx_zero):
    r = x.shape[0]

    def body(dq_ref, dk_ref, dv_ref, raw_ref, g_ref, c_ref, s_ref, w_ref, x_ref, dx1_ref, mod_ref, nw_ref,
             dx_ref, draw_ref, dg_ref, dnw_ref, dmod_ref):
        i = pl.program_id(0)
        _acc_init(i, dg_ref, dnw_ref, dmod_ref)
        cos, sin = c_ref[...], s_ref[...]
        for j in range(ATTN_HEADS + ATTN_KV_HEADS):
            isq = j < ATTN_HEADS
            cs = slice(j * HD, (j + 1) * HD)
            dr = dq_ref[:, cs] if isq else dk_ref[:, (j - ATTN_HEADS) * HD : (j - ATTN_HEADS + 1) * HD]
            dxn = dr * cos + _partner(dr * sin)
            xhat, rr = _head_norm(raw_ref[:, cs], None)
            gi = 0 if isq else 1
            dg_ref[gi : gi + 1, :] += jnp.sum(dxn * xhat, axis=0, keepdims=True)
            dxhat = dxn * g_ref[gi : gi + 1, :]
            draw_ref[:, cs] = (rr * (dxhat - xhat * jnp.mean(dxhat * xhat, axis=-1, keepdims=True))).astype(BF16)
        draw_ref[:, NQ + NKV :] = dv_ref[...].astype(BF16)
        dh = _dot_nt(draw_ref[...], w_ref[...])
        sc, nw = _mrow(mod_ref, SC1), nw_ref[...]
        _, n, xhat, rr = _norm_mod(x_ref[...], nw, _mrow(mod_ref, SH1), sc)
        dxn, dsh, dsc, dnw = _norm_mod_bwd(dh, n, xhat, rr, nw, sc)
        dres = dx1_ref[...]
        if ctx_dx_zero:
            dres = jnp.where(i == nlb, 0.0, dres)
        dx_ref[...] = dres + dxn
        s = i // nlb
        dmod_ref[s, SH1 : SH1 + 1, :] += dsh
        dmod_ref[s, SC1 : SC1 + 1, :] += dsc
        dnw_ref[0:1, :] += dnw

    return _rowcall(
        "qkv_bwd", body, nblk, nlb,
        [(dq, "row"), (dk, "row"), (dv, "row"), (raw, "row"), (gains, "full"), (cos_t, "row"), (sin_t, "row"),
         (w_qkv, "full"), (x, "row"), (dx1, "row"), (mod, "stream"), (nw1, "full")],
        [((r, D), F32, "row"), ((r, NQ + 2 * NKV), BF16, "row"), ((8, HD), F32, "acc"), ((8, D), F32, "acc"),
         ((2, 8, D), F32, "acc")],
    )


RET_KSCALE = RET_DK ** -0.5


def ret_in_fwd(x, mod, nw1, w_in, nblk, nlb):
    r = x.shape[0]

    def body(x_ref, mod_ref, nw_ref, w_ref, q_ref, k_ref, v_ref, g_ref, h_ref):
        h = _norm_mod(x_ref[...], nw_ref[...], _mrow(mod_ref, SH1), _mrow(mod_ref, SC1))[0].astype(BF16)
        h_ref[...] = h
        q_ref[...] = _dotf(h, w_ref[:, 0:RNQ]).astype(BF16)
        k_ref[...] = (_dotf(h, w_ref[:, RNQ : 2 * RNQ]) * RET_KSCALE).astype(BF16)
        v_ref[...] = _dotf(h, w_ref[:, 2 * RNQ : 2 * RNQ + RNV]).astype(BF16)
        g_ref[...] = _dotf(h, w_ref[:, 2 * RNQ + RNV :]).astype(BF16)

    return _rowcall(
        "ret_in_fwd", body, nblk, nlb,
        [(x, "row"), (mod, "stream"), (nw1, "full"), (w_in, "full")],
        [((r, RNQ), BF16, "row"), ((r, RNQ), BF16, "row"), ((r, RNV), BF16, "row"), ((r, RNV), BF16, "row"), ((r, D), BF16, "row")],
    )


def _log_sigmoid(x):
    return jnp.minimum(x, 0.0) - jnp.log(1.0 + jnp.exp(-jnp.abs(x)))


def _ret_decays(lg, reverse):
    c = BM
    i = lax.broadcasted_iota(jnp.int32, (c, c), 0)
    j = lax.broadcasted_iota(jnp.int32, (c, c), 1)
    diff = (j - i) if reverse else (i - j)
    ediff = jnp.maximum(diff, 0).astype(F32)
    dm = jnp.where(diff >= 0, jnp.exp(ediff * lg), 0.0)
    rr = lax.broadcasted_iota(jnp.int32, (c, 1), 0).astype(F32)
    eq = (c - rr) if reverse else (rr + 1.0)
    ek = rr if reverse else (c - 1.0 - rr)
    return dm, ediff, jnp.exp(eq * lg), eq, jnp.exp(ek * lg), ek, jnp.exp(c * lg)


def _ret_chunk_index(nlb):
    return (lambda s: jnp.where(s == 0, nlb, s - 1)), (lambda s: jnp.where(s == 0, nlb, nlb - s))


def ret_scan_fwd(q, k, v, logit_b, nlb):
    r = q.shape[0]
    nb = nlb + 1
    fidx, bidx = _ret_chunk_index(nlb)

    def body(qf, kf, vf, qb, kb, vb, lg_ref, of_ref, ob_ref, rf_ref, rb_ref, stf, stb):
        s = pl.program_id(1)

        @pl.when(s == 0)
        def _():
            stf[...] = jnp.zeros(stf.shape, F32)
            stb[...] = jnp.zeros(stb.shape, F32)

        for d, (q_ref, k_ref, v_ref, o_ref, rs_ref, st) in enumerate(((qf, kf, vf, of_ref, rf_ref, stf), (qb, kb, vb, ob_ref, rb_ref, stb))):
            lg = _log_sigmoid(lg_ref[d, 0])[0:1, 0:1]
            dm, _, qd, _, kd, _, gc = _ret_decays(lg, d == 1)
            qq, kk, vv, st0 = q_ref[...], k_ref[...], v_ref[...], st[...]
            rs_ref[0, 0] = st0
            a = _dot_nt(qq, kk) * dm
            o = _dotf(a.astype(BF16), vv) + _dotf(qq, st0.astype(BF16)) * qd
            o_ref[...] = jnp.where(s == 0, 0.0, o)
            st[...] = st0 * gc + _dot_tn((kk.astype(F32) * kd).astype(BF16), vv)

    qspec = lambda f: pl.BlockSpec((BM, RET_DK), lambda h, s: (f(s), h))
    vspec = lambda f: pl.BlockSpec((BM, RET_DV), lambda h, s: (f(s), h))
    sspec = pl.BlockSpec((1, 1, RET_DK, RET_DV), lambda h, s: (h, s, 0, 0))
    return _pcall(
        body,
        name="ret_scan_fwd",
        grid=(RET_HEADS, nb),
        in_specs=[qspec(fidx), qspec(fidx), vspec(fidx), qspec(bidx), qspec(bidx), vspec(bidx),
                  pl.BlockSpec((2, 1, 8, 128), lambda h, s: (0, h, 0, 0))],
        out_specs=[vspec(fidx), vspec(bidx), sspec, sspec],
        out_shape=[jax.ShapeDtypeStruct((r, RNV), F32), jax.ShapeDtypeStruct((r, RNV), F32),
                   jax.ShapeDtypeStruct((RET_HEADS, nb, RET_DK, RET_DV), F32), jax.ShapeDtypeStruct((RET_HEADS, nb, RET_DK, RET_DV), F32)],
        scratch_shapes=[pltpu.VMEM((RET_DK, RET_DV), F32), pltpu.VMEM((RET_DK, RET_DV), F32)],
        compiler_params=pltpu.CompilerParams(dimension_semantics=("parallel", "arbitrary"), vmem_limit_bytes=VMEM_LIMIT),
    )(q, k, v, q, k, v, logit_b)


def _group_norm(o):
    mu = jnp.mean(o, axis=-1, keepdims=True)
    oc = o - mu
    rstd = lax.rsqrt(jnp.mean(oc * oc, axis=-1, keepdims=True) + EPS)
    return oc * rstd, rstd


def ret_out_fwd(o_f, o_b, g, gnw, w_out, x, mod, nblk, nlb):
    r = x.shape[0]

    def body(of_ref, ob_ref, g_ref, gn_ref, w_ref, x_ref, mod_ref, x1_ref, y_ref, z_ref):
        for hh in range(RET_HEADS):
            cs = slice(hh * RET_DV, (hh + 1) * RET_DV)
            yhat, _ = _group_norm(of_ref[:, cs] + ob_ref[:, cs])
            gg = g_ref[:, cs].astype(F32)
            z_ref[:, cs] = (gg * _sigmoid(gg) * (yhat * gn_ref[0:1, cs])).astype(BF16)
        y = _dotf(z_ref[...], w_ref[...])
        y_ref[...] = y
        x1_ref[...] = x_ref[...] + _mrow(mod_ref, G1) * y

    return _rowcall(
        "ret_out_fwd", body, nblk, nlb,
        [(o_f, "row"), (o_b, "row"), (g, "row"), (gnw, "full"), (w_out, "full"), (x, "row"), (mod, "stream")],
        [((r, D), F32, "row"), ((r, D), F32, "row"), ((r, RNV), BF16, "row")],
    )


def ret_out_bwd(dx1, y, o_f, o_b, g, gnw, w_out, mod, nblk, nlb):
    r = dx1.shape[0]

    def body(d_ref, y_ref, of_ref, ob_ref, g_ref, gn_ref, w_ref, mod_ref, dy_ref, do_ref, dg_ref, dgn_ref, dmod_ref, dz):
        i = pl.program_id(0)
        _acc_init(i, dgn_ref, dmod_ref)
        d = d_ref[...]
        dmod_ref[i // nlb, G1 : G1 + 1, :] += jnp.sum(d * y_ref[...], axis=0, keepdims=True)
        dy = (_mrow(mod_ref, G1) * d).astype(BF16)
        dy_ref[...] = dy
        dz[...] = _dot_nt(dy, w_ref[...])
        for hh in range(RET_HEADS):
            cs = slice(hh * RET_DV, (hh + 1) * RET_DV)
            yhat, rstd = _group_norm(of_ref[:, cs] + ob_ref[:, cs])
            gg = g_ref[:, cs].astype(F32)
            sg = _sigmoid(gg)
            gn = gn_ref[0:1, cs]
            dzz = dz[:, cs]
            dg_ref[:, cs] = (dzz * (yhat * gn) * (sg * (1.0 + gg * (1.0 - sg)))).astype(BF16)
            dyn = dzz * (gg * sg)
            dgn_ref[0:1, cs] += jnp.sum(dyn * yhat, axis=0, keepdims=True)
            dyh = dyn * gn
            do = rstd * (dyh - jnp.mean(dyh, axis=-1, keepdims=True) - yhat * jnp.mean(dyh * yhat, axis=-1, keepdims=True))
            do_ref[:, cs] = do.astype(BF16)

    return _rowcall(
        "ret_out_bwd", body, nblk, nlb,
        [(dx1, "row"), (y, "row"), (o_f, "row"), (o_b, "row"), (g, "row"), (gnw, "full"), (w_out, "full"), (mod, "stream")],
        [((r, D), BF16, "row"), ((r, RNV), BF16, "row"), ((r, RNV), BF16, "row"), ((8, RNV), F32, "acc"), ((2, 8, D), F32, "acc")],
        scratch=[pltpu.VMEM((BM, RNV), F32)],
    )


def ret_scan_bwd(q, k, v, do, st_f, st_b, logit_b, nlb):
    r = q.shape[0]
    nb = nlb + 1
    fidx, bidx = _ret_chunk_index(nlb)
    step = lambda t: nb - 1 - t

    def body(qf, kf, vf, dof, rf, qb, kb, vb, dob, rb, lg_ref,
             dqf, dkf, dvf, dqb, dkb, dvb, dlg_ref, drf, drb):
        t = pl.program_id(1)
        s = step(t)

        @pl.when(t == 0)
        def _():
            drf[...] = jnp.zeros(drf.shape, F32)
            drb[...] = jnp.zeros(drb.shape, F32)
            dlg_ref[...] = jnp.zeros(dlg_ref.shape, F32)

        dirs = ((qf, kf, vf, dof, rf, dqf, dkf, dvf, drf), (qb, kb, vb, dob, rb, dqb, dkb, dvb, drb))
        for d, (q_ref, k_ref, v_ref, do_ref, rs_ref, dq_ref, dk_ref, dv_ref, dr) in enumerate(dirs):
            lg = _log_sigmoid(lg_ref[d, 0])[0:1, 0:1]
            dm, ediff, qd, eq, kd, ek, gc = _ret_decays(lg, d == 1)
            qq, kk, vv = q_ref[...], k_ref[...], v_ref[...]
            dob16 = jnp.where(s == 0, jnp.zeros((), BF16), do_ref[...])
            do32 = dob16.astype(F32)
            st0 = rs_ref[0, 0]
            st16 = st0.astype(BF16)
            dr0 = dr[...]
            dr16 = dr0.astype(BF16)
            a = _dot_nt(qq, kk) * dm
            daf = _dot_nt(dob16, vv)
            ds = (daf * dm).astype(BF16)
            qr = _dotf(qq, st16)
            k32 = kk.astype(F32)
            kdec = (k32 * kd).astype(BF16)
            dv_ref[...] = (_dot_tn(a.astype(BF16), dob16) + _dotf(kdec, dr16)).astype(BF16)
            dq_ref[...] = (_dotf(ds, kk) + _dot_nt(dob16, st16) * qd).astype(BF16)
            vdr = _dot_nt(vv, dr16)
            dk_ref[...] = (_dot_tn(ds, qq) + vdr * kd).astype(BF16)
            tot = (jnp.sum(daf * a * ediff)
                   + jnp.sum(eq * qd * jnp.sum(do32 * qr, axis=-1, keepdims=True))
                   + jnp.sum(ek * kd * jnp.sum(k32 * vdr, axis=-1, keepdims=True))
                   + jnp.sum(BM * gc * jnp.sum(dr0 * st0, axis=-1, keepdims=True)))
            dlg_ref[d, 0] += tot
            dr[...] = gc * dr0 + _dot_tn(qq, (do32 * qd).astype(BF16))

        @pl.when(t == nb - 1)
        def _():
            dlg_ref[...] = dlg_ref[...] * _sigmoid(-lg_ref[...])

    qspec = lambda f: pl.BlockSpec((BM, RET_DK), lambda h, t: (f(step(t)), h))
    vspec = lambda f: pl.BlockSpec((BM, RET_DV), lambda h, t: (f(step(t)), h))
    sspec = pl.BlockSpec((1, 1, RET_DK, RET_DV), lambda h, t: (h, step(t), 0, 0))
    lspec = pl.BlockSpec((2, 1, 8, 128), lambda h, t: (0, h, 0, 0))
    sq, sv = jax.ShapeDtypeStruct((r, RNQ), BF16), jax.ShapeDtypeStruct((r, RNV), BF16)
    return _pcall(
        body,
        name="ret_scan_bwd",
        grid=(RET_HEADS, nb),
        in_specs=[qspec(fidx), qspec(fidx), vspec(fidx), vspec(fidx), sspec,
                  qspec(bidx), qspec(bidx), vspec(bidx), vspec(bidx), sspec, lspec],
        out_specs=[qspec(fidx), qspec(fidx), vspec(fidx), qspec(bidx), qspec(bidx), vspec(bidx), lspec],
        out_shape=[sq, sq, sv, sq, sq, sv, jax.ShapeDtypeStruct((2, RET_HEADS, 8, 128), F32)],
        scratch_shapes=[pltpu.VMEM((RET_DK, RET_DV), F32), pltpu.VMEM((RET_DK, RET_DV), F32)],
        compiler_params=pltpu.CompilerParams(dimension_semantics=("parallel", "arbitrary"), vmem_limit_bytes=VMEM_LIMIT),
    )(q, k, v, do, st_f, q, k, v, do, st_b, logit_b)


def ret_in_bwd(dqf, dkf, dvf, dqb, dkb, dvb, dgate, w_in, x, dx1, mod, nw1, nblk, nlb):
    r = x.shape[0]
    nin = 2 * RNQ + 2 * RNV

    def body(dqf_ref, dkf_ref, dvf_ref, dqb_ref, dkb_ref, dvb_ref, dg_ref, w_ref, x_ref, dx1_ref, mod_ref, nw_ref,
             dx_ref, din_ref, dnw_ref, dmod_ref):
        i = pl.program_id(0)
        _acc_init(i, dnw_ref, dmod_ref)
        is_ctx = i == nlb
        din_ref[:, 0:RNQ] = (dqf_ref[...].astype(F32) + dqb_ref[...].astype(F32)).astype(BF16)
        din_ref[:, RNQ : 2 * RNQ] = ((dkf_ref[...].astype(F32) + dkb_ref[...].astype(F32)) * RET_KSCALE).astype(BF16)
        din_ref[:, 2 * RNQ : 2 * RNQ + RNV] = (dvf_ref[...].astype(F32) + dvb_ref[...].astype(F32)).astype(BF16)
        din_ref[:, 2 * RNQ + RNV :] = jnp.where(is_ctx, jnp.zeros((), BF16), dg_ref[...])
        dh = _dot_nt(din_ref[...], w_ref[...])
        sc, nw = _mrow(mod_ref, SC1), nw_ref[...]
        _, n, xhat, rr = _norm_mod(x_ref[...], nw, _mrow(mod_ref, SH1), sc)
        dxn, dsh, dsc, dnw = _norm_mod_bwd(dh, n, xhat, rr, nw, sc)
        dx_ref[...] = jnp.where(is_ctx, 0.0, dx1_ref[...]) + dxn
        s = i // nlb
        dmod_ref[s, SH1 : SH1 + 1, :] += dsh
        dmod_ref[s, SC1 : SC1 + 1, :] += dsc
        dnw_ref[0:1, :] += dnw

    return _rowcall(
        "ret_in_bwd", body, nblk, nlb,
        [(dqf, "row"), (dkf, "row"), (dvf, "row"), (dqb, "row"), (dkb, "row"), (dvb, "row"), (dgate, "row"),
         (w_in, "full"), (x, "row"), (dx1, "row"), (mod, "stream"), (nw1, "full")],
        [((r, D), F32, "row"), ((r, nin), BF16, "row"), ((8, D), F32, "acc"), ((2, 8, D), F32, "acc")],
    )


def loss_head(xout, target, nlb):
    r = xout.shape[0]

    def body(x_ref, t_ref, dx_ref, l_ref):
        _acc_init(pl.program_id(0), l_ref)
        err = x_ref[...] - t_ref[...]
        dx_ref[...] = err * (1.0 / D)
        l_ref[...] += 0.5 * jnp.sum(jnp.mean(err * err, axis=-1, keepdims=True))

    return _rowcall(
        "loss_head", body, nlb, nlb,
        [(xout, "row"), (target, "row")],
        [((r, D), F32, "row"), ((8, 128), F32, "acc")],
    )


N_MIXERS = 3
POOL, ATTN, RET = range(3)


def _layer_plan(depth):
    plan = []
    for i in range(depth):
        kind = i % N_MIXERS
        ctx_out = any(k % N_MIXERS != POOL for k in range(i + 1, depth))
        plan.append((kind, i // N_MIXERS, ctx_out or kind != POOL, ctx_out))
    return plan


def local_step(xs, target, mods, w, nlb, depth, fetch, emit):
    nb = nlb + 1
    plan = _layer_plan(depth)
    saved = []
    x = xs
    for i, (kind, j, ctx_in, ctx_out) in enumerate(plan):
        nmix = nb if ctx_out else nlb
        mod, nw1, nw2 = mods[i], w["nw"][i, 0:1], w["nw"][i, 1:2]
        lw = fetch(i, x)
        sv = {"x": x, "lw": lw}
        if kind == POOL:
            x1, sv["ypre"] = pool_fwd(x, mod, nw1, lw["pool_w"], (w["pbs"], j), nmix, nlb)
        elif kind == ATTN:
            assert ctx_out
            sv["raw"], sv["q"], sv["k"], sv["v"], sv["h"] = qkv_fwd(x, mod, nw1, lw["attn_w_qkv"], w["gains"], w["cos"], w["sin"], nb, nlb)
            sv["o"], sv["lse"] = attn_fwd(sv["q"], sv["k"], sv["v"], nb, nlb)
            x1, sv["y"] = attn_out_fwd(sv["o"], lw["attn_w_o"], x, mod, nb, nlb)
        else:
            assert ctx_in and not ctx_out
            sv["q"], sv["k"], sv["v"], sv["g"], sv["h"] = ret_in_fwd(x, mod, nw1, lw["ret_w_in"], nb, nlb)
            sv["o_f"], sv["o_b"], sv["st_f"], sv["st_b"] = ret_scan_fwd(sv["q"], sv["k"], sv["v"], w["logit_b"], nlb)
            x1, sv["y"], sv["z"] = ret_out_fwd(sv["o_f"], sv["o_b"], sv["g"], w["gnw"], lw["ret_w_out"], x, mod, nlb, nlb)
        sv["x1"] = x1
        sv["u"], sv["h2"] = ffn_up(x1, mod, nw2, lw["ffn_w_up"], nmix, nlb)
        x, sv["f"] = ffn_down(sv["u"], (w["cw"], i), lw["ffn_w_down"], x1, mod, w["sel"], nmix, nlb)
        saved.append(sv)

    dx, loss_tile = loss_head(x, target, nlb)
    g = {k: [None] * depth for k in ("dcw", "dnw1", "dnw2", "dmod")}
    dep = loss_tile
    for i in reversed(range(depth)):
        kind, j, ctx_in, ctx_out = plan[i]
        sv = saved[i]
        lw, big = sv["lw"], {}
        nmix = nb if ctx_out else nlb
        mod, nw1, nw2 = mods[i], w["nw"][i, 0:1], w["nw"][i, 1:2]
        gated, df, duc, g["dcw"][i], dmod = ffn_bwd1(dx, sv["f"], sv["u"], (w["cw"], i), lw["ffn_w_down"], mod, w["sel"], dep, nmix, nlb)
        big["ffn_w_down"] = matmul_tn(gated, df, nmix)
        du, dx1, g["dnw2"][i], dm = ffn_bwd3(duc, (w["cw"], i), lw["ffn_w_up"], sv["x1"], dx, mod, nw2, w["sel"], nmix, nlb)
        dmod = dmod + dm
        big["ffn_w_up"] = matmul_tn(sv["h2"], du, nmix)
        if kind == POOL:
            dx, dpw, dpbs, g["dnw1"][i], dm = pool_bwd(dx1, sv["x"], sv["ypre"], mod, nw1, lw["pool_w"], (w["pbs"], j), nmix, nlb)
            big["pool_w"] = dpw.astype(BF16)
            g.setdefault("dpbs", {})[j] = dpbs
        elif kind == ATTN:
            dy, do, dm1 = mix_out_bwd("attn_out_bwd", dx1, sv["y"], lw["attn_w_o"], mod, nb, nlb)
            big["attn_w_o"] = matmul_tn(sv["o"], dy, nb)
            dq, dk, dv = attn_bwd(sv["q"], sv["k"], sv["v"], sv["o"], do, sv["lse"], nb, nlb)
            dx, draw, g["dgains"], g["dnw1"][i], dm = qkv_bwd(
                dq, dk, dv, sv["raw"], w["gains"], w["cos"], w["sin"], lw["attn_w_qkv"], sv["x"], dx1, mod, nw1, nb, nlb, False)
            big["attn_w_qkv"] = matmul_tn(sv["h"], draw, nb)
            dm = dm + dm1
        else:
            dy, do, dgate, g["dgnw"], dm1 = ret_out_bwd(dx1, sv["y"], sv["o_f"], sv["o_b"], sv["g"], w["gnw"], lw["ret_w_out"], mod, nlb, nlb)
            big["ret_w_out"] = matmul_tn(sv["z"], dy, nlb)
            dqf, dkf, dvf, dqb, dkb, dvb, g["dlogit"] = ret_scan_bwd(sv["q"], sv["k"], sv["v"], do, sv["st_f"], sv["st_b"], w["logit_b"], nlb)
            dx, din, g["dnw1"][i], dm = ret_in_bwd(dqf, dkf, dvf, dqb, dkb, dvb, dgate, lw["ret_w_in"], sv["x"], dx1, mod, nw1, nb, nlb)
            big["ret_w_in"] = matmul_tn(sv["h"], din, nb)
            dm = dm + dm1
        g["dmod"][i] = dmod + dm
        dep = emit(i, big)
    return loss_tile, dx, g


MESH_ID = pl.DeviceIdType.MESH
CHIP_FLIPS = ((1, 0), (0, 1), (1, 1))


def _pos():
    return lax.axis_index("x"), lax.axis_index("y"), lax.axis_index("c")


def _flip(v, b):
    return 1 - v if b else v


def small_all_gather(name, x):
    rows, n = x.shape

    def body(x_ref, out_ref, send_sems, recv_sems, local_sem):
        mx, my, mc = _pos()
        me = 4 * mx + 2 * my + mc
        mine = pltpu.make_async_copy(x_ref, out_ref.at[me], local_sem)
        mine.start()
        sends, peers = [], []
        for kk in range(1, N_DEV):
            peer = (_flip(mx, (kk >> 2) & 1), _flip(my, (kk >> 1) & 1), _flip(mc, kk & 1))
            cp = pltpu.make_async_remote_copy(src_ref=x_ref, dst_ref=out_ref.at[me], send_sem=send_sems.at[kk - 1],
                                              recv_sem=recv_sems.at[kk - 1], device_id=peer, device_id_type=MESH_ID)
            cp.start()
            sends.append(cp)
            peers.append(peer)
        for kk, peer in enumerate(peers):
            pidx = 4 * peer[0] + 2 * peer[1] + peer[2]
            pltpu.make_async_remote_copy(src_ref=x_ref, dst_ref=out_ref.at[pidx], send_sem=send_sems.at[kk],
                                         recv_sem=recv_sems.at[kk], device_id=peer, device_id_type=MESH_ID).wait_recv()
        for cp in sends:
            cp.wait_send()
        mine.wait()

    return _pcall(
        body,
        name=name,
        out_shape=jax.ShapeDtypeStruct((N_DEV, rows, n), x.dtype),
        in_specs=[pl.BlockSpec(memory_space=pltpu.VMEM)],
        out_specs=pl.BlockSpec(memory_space=pltpu.VMEM),
        scratch_shapes=[pltpu.SemaphoreType.DMA((N_DEV - 1,)), pltpu.SemaphoreType.DMA((N_DEV - 1,)), pltpu.SemaphoreType.DMA],
        compiler_params=pltpu.CompilerParams(vmem_limit_bytes=VMEM_LIMIT),
    )(x)


def _hbm_exchange(name, ins, out_shapes, plan):
    n_in = len(ins)
    probe_local, probe_remote = plan([None] * n_in, [None] * len(out_shapes), probe=True)

    def body(*refs):
        in_refs, out_refs = refs[:n_in], refs[n_in : n_in + len(out_shapes)]
        send_sems, recv_sems, local_sems = refs[n_in + len(out_shapes) :]
        local, remote = plan(in_refs, out_refs, probe=False)
        lcs = [pltpu.make_async_copy(s, d, local_sems.at[k]) for k, (s, d) in enumerate(local)]
        for cp in lcs:
            cp.start()
        rcs = []
        for k, (s, d, peer, _) in enumerate(remote):
            cp = pltpu.make_async_remote_copy(src_ref=s, dst_ref=d, send_sem=send_sems.at[k], recv_sem=recv_sems.at[k],
                                              device_id=peer, device_id_type=MESH_ID)
            cp.start()
            rcs.append(cp)
        for k, (s, _, peer, here) in enumerate(remote):
            pltpu.make_async_remote_copy(src_ref=s, dst_ref=here, send_sem=send_sems.at[k], recv_sem=recv_sems.at[k],
                                         device_id=peer, device_id_type=MESH_ID).wait_recv()
        for cp in rcs:
            cp.wait_send()
        for cp in lcs:
            cp.wait()

    return _pcall(
        body,
        name=name,
        out_shape=list(out_shapes),
        in_specs=[pl.BlockSpec(memory_space=pl.ANY)] * n_in,
        out_specs=[pl.BlockSpec(memory_space=pl.ANY)] * len(out_shapes),
        scratch_shapes=[pltpu.SemaphoreType.DMA((max(probe_remote, 1),)), pltpu.SemaphoreType.DMA((max(probe_remote, 1),)),
                        pltpu.SemaphoreType.DMA((max(probe_local, 1),))],
    )(*ins)


def _at_axis(ref, axis, start, size):
    return ref.at[tuple(pl.ds(start, size) if a == axis else slice(None) for a in range(len(ref.shape)))]


HBM_SPEC = pl.BlockSpec(memory_space=pltpu.HBM)
SEM_SPEC = pl.BlockSpec(memory_space=pltpu.SEMAPHORE)
SIDE_EFFECT = pltpu.SideEffectType.DATAFLOW_SIDE_EFFECTING


def _in_hbm(a):
    return pltpu.with_memory_space_constraint(a, pltpu.HBM)


def _copies_start(name, bufs, counts, plan):
    n, ng = len(bufs), len(counts)

    def body(*refs):
        sems = refs[n : n + 2 * ng]
        token = refs[n + 2 * ng + n]
        for gi, copies in enumerate(plan(refs[:n])):
            for k, (s, d, peer) in enumerate(copies):
                pltpu.make_async_remote_copy(src_ref=s, dst_ref=d, send_sem=sems[2 * gi].at[k], recv_sem=sems[2 * gi + 1].at[k],
                                             device_id=peer, device_id_type=MESH_ID).start()
        token[...] = jnp.zeros(token.shape, token.dtype)

    out = _pcall(
        body,
        name=name,
        out_shape=tuple(pltpu.SemaphoreType.DMA((c,)) for c in counts for _ in range(2))
        + tuple(pltpu.HBM(b.shape, b.dtype) for b in bufs) + (jax.ShapeDtypeStruct((8, 128), F32),),
        in_specs=(HBM_SPEC,) * n,
        out_specs=(SEM_SPEC,) * (2 * ng) + (HBM_SPEC,) * n + (pl.BlockSpec(memory_space=pltpu.VMEM),),
        input_output_aliases={i: 2 * ng + i for i in range(n)},
        compiler_params=pltpu.CompilerParams(has_side_effects=SIDE_EFFECT),
    )(*[_in_hbm(b) for b in bufs])
    return [(out[2 * g], out[2 * g + 1]) for g in range(ng)], list(out[2 * ng : 2 * ng + n]), out[2 * ng + n]


def _copies_wait(name, sems, bufs, plan, after):
    n, ng = len(bufs), len(sems)

    def body(*refs):
        for gi, copies in enumerate(plan(refs[:n])):
            for k, (s, d, peer) in enumerate(copies):
                cp = pltpu.make_async_remote_copy(src_ref=s, dst_ref=d, send_sem=refs[n + 2 * gi].at[k], recv_sem=refs[n + 2 * gi + 1].at[k],
                                                  device_id=peer, device_id_type=MESH_ID)
                cp.wait_send()
                cp.wait_recv()

    out = _pcall(
        body,
        name=name,
        out_shape=tuple(pltpu.HBM(b.shape, b.dtype) for b in bufs),
        in_specs=(HBM_SPEC,) * n + (SEM_SPEC,) * (2 * ng) + (pl.BlockSpec(memory_space=pl.ANY),),
        out_specs=(HBM_SPEC,) * n,
        input_output_aliases={i: i for i in range(n)},
        compiler_params=pltpu.CompilerParams(has_side_effects=SIDE_EFFECT),
    )(*bufs, *[s for pair in sems for s in pair], after)
    return list(out)


def _layer_matrices(depth):
    out = []
    for i, (kind, j, _, _) in enumerate(_layer_plan(depth)):
        mix = ([("pool_w", j, 1)], [("attn_w_qkv", j, 1), ("attn_w_o", j, 0)], [("ret_w_in", j, 1), ("ret_w_out", j, 0)])[kind]
        out.append(mix + [("ffn_w_up", i, 1), ("ffn_w_down", i, 0)])
    return out


def _peers(mx, my, mc):
    out = []
    for fx, fy in CHIP_FLIPS:
        px, py = _flip(mx, fx), _flip(my, fy)
        out.append(((px, py, mc), 2 * px + py))
    return out


def place_own(shards, names, layers):
    entries = [e for layer in layers for e in layer]
    shapes = []
    for name, _, axis in entries:
        shp = list(shards[names.index(name)].shape[1:])
        shp[axis] *= N_CHIP
        shapes.append(jax.ShapeDtypeStruct(tuple(shp), BF16))

    def body(*refs):
        s_refs, o_refs, sems = refs[: len(shards)], refs[len(shards) : len(shards) + len(entries)], refs[-1]
        mx, my, _ = _pos()
        cps = []
        for k, (name, idx, axis) in enumerate(entries):
            src = s_refs[names.index(name)].at[idx]
            n = src.shape[axis]
            cps.append(pltpu.make_async_copy(src, _at_axis(o_refs[k], axis, (2 * mx + my) * n, n), sems.at[k]))
            cps[-1].start()
        for cp in cps:
            cp.wait()

    return _pcall(
        body,
        name="place_own",
        out_shape=shapes,
        in_specs=[pl.BlockSpec(memory_space=pl.ANY)] * len(shards),
        out_specs=[pl.BlockSpec(memory_space=pl.ANY)] * len(entries),
        scratch_shapes=[pltpu.SemaphoreType.DMA((len(entries),))],
    )(*shards)


def _gather_plan(names, layers, group, n_shards, here):
    def plan(refs):
        mx, my, mc = _pos()
        s_refs, f_refs = refs[:n_shards], refs[n_shards:]
        groups, k = [], 0
        for gi, layer in enumerate(layers):
            if group is not None and gi != group:
                continue
            copies = []
            for name, idx, axis in layer:
                src = s_refs[names.index(name)].at[idx]
                n = src.shape[axis]
                for peer, pchip in _peers(mx, my, mc):
                    at = pchip if here else 2 * mx + my
                    copies.append((src, _at_axis(f_refs[k], axis, at * n, n), peer))
                k += 1
            groups.append(copies)
        return groups

    return plan


def gather_start(shards, names, layers, fulls):
    counts = [len(layer) * len(CHIP_FLIPS) for layer in layers]
    sems, bufs, _ = _copies_start("gather_start", list(shards) + list(fulls), counts, _gather_plan(names, layers, None, len(shards), False))
    return sems, bufs[: len(shards)], bufs[len(shards) :]


def gather_wait(g, sems_g, shards, names, layers, fulls_g, after):
    bufs = _copies_wait(f"gather_wait_{g}", [sems_g], list(shards) + list(fulls_g), _gather_plan(names, layers, g, len(shards), True), after)
    return bufs[: len(shards)], bufs[len(shards) :]


def _scatter_plan(layer_entries, n_grads, land_of):
    def plan(refs):
        mx, my, mc = _pos()
        groups, k = [], 0
        for layer in layer_entries:
            copies = []
            for name, idx, axis in layer:
                gref, land = refs[k], refs[n_grads + land_of(k, name)]
                n = gref.shape[axis] // N_CHIP
                for slot, (peer, pchip) in enumerate(_peers(mx, my, mc)):
                    copies.append((_at_axis(gref, axis, pchip * n, n), land.at[slot, idx], peer))
                k += 1
            groups.append(copies)
        return groups

    return plan


def scatter_start(i, layer, grads, lands):
    sems, bufs, token = _copies_start(f"scatter_start_{i}", list(grads) + list(lands), [len(layer) * len(CHIP_FLIPS)],
                                      _scatter_plan([layer], len(grads), lambda k, name: k))
    return sems[0], bufs[: len(grads)], bufs[len(grads) :], token


def scatter_wait(sems, layers, grads, names, lands, after):
    bufs = _copies_wait("scatter_wait", sems, list(grads) + list(lands),
                        _scatter_plan(layers, len(grads), lambda k, name: names.index(name)), after)
    return bufs[: len(grads)], bufs[len(grads) :]


def sibling_swap(parts):
    def plan(in_refs, out_refs, probe):
        if probe:
            return 0, len(parts)
        mx, my, mc = _pos()
        return [], [(s, o, (mx, my, 1 - mc), o) for s, o in zip(in_refs, out_refs)]

    return _hbm_exchange("sibling_swap", parts, [jax.ShapeDtypeStruct(p.shape, p.dtype) for p in parts], plan)


EW_ROWS = 256


def _ew_call(name, fn, ins, n_out):
    rows, cols = ins[0].shape[-2:]
    tr = EW_ROWS if rows % EW_ROWS == 0 else rows

    def body(*refs):
        outs = fn(*[r[...] for r in refs[: len(ins)]])
        for o_ref, o in zip(refs[len(ins) :], outs):
            o_ref[...] = o

    def spec(a):
        if a.ndim == 3:
            return pl.BlockSpec((a.shape[0], tr, cols), lambda i: (0, i, 0))
        return pl.BlockSpec((tr, cols), lambda i: (i, 0))

    return _pcall(
        body,
        name=name,
        grid=(rows // tr,),
        in_specs=[spec(a) for a in ins],
        out_specs=[pl.BlockSpec((tr, cols), lambda i: (i, 0))] * n_out,
        out_shape=[jax.ShapeDtypeStruct((rows, cols), F32)] * n_out,
        compiler_params=pltpu.CompilerParams(dimension_semantics=("parallel",), vmem_limit_bytes=VMEM_LIMIT),
    )(*ins)


def _adamw(w, g, m, v):
    m = ADAM_B1 * m + (1.0 - ADAM_B1) * g
    v = ADAM_B2 * v + (1.0 - ADAM_B2) * (g * g)
    m_hat = m / (1.0 - ADAM_B1 ** ADAM_STEP)
    v_hat = v / (1.0 - ADAM_B2 ** ADAM_STEP)
    return -ADAM_LR * (m_hat / (jnp.sqrt(v_hat) + ADAM_EPS) + ADAM_WD * w), m, v


def sum_slots(name, own, landing):
    def fn(o, l):
        acc = o.astype(F32)
        for k in range(l.shape[0]):
            acc = acc + l[k].astype(F32)
        return (acc,)

    return _ew_call(name, fn, [own, landing], 1)[0]


def adamw_pair(name, w, m, v, p, ps):
    def fn(w, m, v, p, ps):
        g = p + ps
        return (g,) + _adamw(w, g, m, v)

    return _ew_call(name, fn, [w, m, v, p, ps], 4)


def adamw_one(name, w, m, v, g):
    return _ew_call(name, lambda w, m, v, g: _adamw(w, g, m, v), [w, m, v, g], 3)


def reduce_devices(name, x):
    def fn(a):
        acc = a[0]
        for k in range(1, a.shape[0]):
            acc = acc + a[k]
        return (acc,)

    return _ew_call(name, fn, [x], 1)[0]


ADA_ROWS = 16
ADA_CTX = N_DEV


def ada_fwd(s9, ada_w, ada_b):
    depth, _, n = ada_w.shape

    def body(s_ref, w_ref, b_ref, o_ref):
        s = s_ref[...]
        o_ref[...] = _dotf((s * _sigmoid(s)).astype(BF16), w_ref[...].astype(BF16)) + b_ref[...]

    return _pcall(
        body,
        name="ada_fwd",
        grid=(depth,),
        in_specs=[pl.BlockSpec((ADA_ROWS, D), lambda i: (0, 0)), pl.BlockSpec((None, D, n), lambda i: (i, 0, 0)),
                  pl.BlockSpec((None, 1, n), lambda i: (i, 0, 0))],
        out_specs=pl.BlockSpec((None, ADA_ROWS, n), lambda i: (i, 0, 0)),
        out_shape=jax.ShapeDtypeStruct((depth, ADA_ROWS, n), F32),
        compiler_params=pltpu.CompilerParams(dimension_semantics=("arbitrary",), vmem_limit_bytes=VMEM_LIMIT),
    )(s9, ada_w, ada_b)


def ada_bwd(s9, dm, ada_w):
    depth, _, n = ada_w.shape

    def body(s_ref, dm_ref, w_ref, gw_ref, ds_ref):
        _acc_init(pl.program_id(0), ds_ref)
        s = s_ref[...]
        dmb = dm_ref[...].astype(BF16)
        gw_ref[...] = _dot_tn((s * _sigmoid(s)).astype(BF16), dmb)
        ds_ref[...] += _dot_nt(dmb, w_ref[...].astype(BF16))

    return _pcall(
        body,
        name="ada_bwd",
        grid=(depth,),
        in_specs=[pl.BlockSpec((ADA_ROWS, D), lambda i: (0, 0)), pl.BlockSpec((None, ADA_ROWS, n), lambda i: (i, 0, 0)),
                  pl.BlockSpec((None, D, n), lambda i: (i, 0, 0))],
        out_specs=[pl.BlockSpec((None, D, n), lambda i: (i, 0, 0)), pl.BlockSpec((ADA_ROWS, D), lambda i: (0, 0))],
        out_shape=[jax.ShapeDtypeStruct((depth, D, n), F32), jax.ShapeDtypeStruct((ADA_ROWS, D), F32)],
        compiler_params=pltpu.CompilerParams(dimension_semantics=("arbitrary",), vmem_limit_bytes=VMEM_LIMIT),
    )(s9, dm, ada_w)


def cctx_grad(parts, c_ctx):
    def body(p_ref, c_ref, o_ref):
        acc = p_ref[0, ADA_CTX : ADA_CTX + 1, :]
        for chip in range(1, N_CHIP):
            acc = acc + p_ref[2 * chip, ADA_CTX : ADA_CTX + 1, :]
        c = c_ref[...]
        sg = _sigmoid(c)
        o_ref[...] = acc * (sg * (1.0 + c * (1.0 - sg)))

    return _pcall(body, name="cctx_grad", out_shape=jax.ShapeDtypeStruct((1, D), F32))(parts, c_ctx)


def _pack(arrs):
    flat = jnp.concatenate([a.astype(F32).reshape(-1) for a in arrs])
    rows = -(-flat.shape[0] // (8 * D)) * 8
    return jnp.pad(flat, (0, rows * D - flat.shape[0])).reshape(rows, D)


def _unpack(slab, shapes):
    lead = slab.shape[:-2]
    flat = slab.reshape(lead + (-1,))
    out, off = [], 0
    for shp in shapes:
        n = 1
        for d in shp:
            n *= d
        out.append(flat[..., off : off + n].reshape(lead + tuple(shp)))
        off += n
    return out


def _unshard(per_dev, axis):
    return jnp.concatenate([per_dev[2 * chip] for chip in range(N_CHIP)], axis=axis)


BIG = (("pool_w", 1), ("attn_w_qkv", 1), ("attn_w_o", 0), ("ret_w_in", 1), ("ret_w_out", 0), ("ffn_w_up", 1), ("ffn_w_down", 0))
WEIGHTS = ("c_ctx", "ada_w", "ada_b", "norm_w", "pool_w", "pool_b", "pool_scale", "attn_w_qkv", "attn_q_gain", "attn_k_gain",
           "attn_w_o", "ret_w_in", "ret_decay_logit", "ret_gn_w", "ret_w_out", "ffn_w_up", "ffn_conv_w", "ffn_conv_b", "ffn_w_down")
SMALL = tuple(n for n in WEIGHTS if n != "ada_w" and n not in dict(BIG))
SMALL_SHARD_AXIS = {"norm_w": 2, "pool_b": 1, "pool_scale": 1, "ret_gn_w": 1, "ffn_conv_w": 2}


def kernel(x, c, ctx, c_ctx, ada_w, ada_b, norm_w, pool_w, pool_b, pool_scale, attn_w_qkv, attn_q_gain, attn_k_gain, attn_w_o, ret_w_in, ret_decay_logit, ret_gn_w, ret_w_out, ffn_w_up, ffn_conv_w, ffn_conv_b, ffn_w_down, loss_target, m_c_ctx, m_ada_w, m_ada_b, m_norm_w, m_pool_w, m_pool_b, m_pool_scale, m_attn_w_qkv, m_attn_q_gain, m_attn_k_gain, m_attn_w_o, m_ret_w_in, m_ret_decay_logit, m_ret_gn_w, m_ret_w_out, m_ffn_w_up, m_ffn_conv_w, m_ffn_conv_b, m_ffn_w_down, v_c_ctx, v_ada_w, v_ada_b, v_norm_w, v_pool_w, v_pool_b, v_pool_scale, v_attn_w_qkv, v_attn_q_gain, v_attn_k_gain, v_attn_w_o, v_ret_w_in, v_ret_decay_logit, v_ret_gn_w, v_ret_w_out, v_ffn_w_up, v_ffn_conv_w, v_ffn_conv_b, v_ffn_w_down):
    P = dict(zip(WEIGHTS, (c_ctx, ada_w, ada_b, norm_w, pool_w, pool_b, pool_scale, attn_w_qkv, attn_q_gain, attn_k_gain, attn_w_o,
                           ret_w_in, ret_decay_logit, ret_gn_w, ret_w_out, ffn_w_up, ffn_conv_w, ffn_conv_b, ffn_w_down)))
    M = dict(zip(WEIGHTS, (m_c_ctx, m_ada_w, m_ada_b, m_norm_w, m_pool_w, m_pool_b, m_pool_scale, m_attn_w_qkv, m_attn_q_gain,
                           m_attn_k_gain, m_attn_w_o, m_ret_w_in, m_ret_decay_logit, m_ret_gn_w, m_ret_w_out, m_ffn_w_up,
                           m_ffn_conv_w, m_ffn_conv_b, m_ffn_w_down)))
    V = dict(zip(WEIGHTS, (v_c_ctx, v_ada_w, v_ada_b, v_norm_w, v_pool_w, v_pool_b, v_pool_scale, v_attn_w_qkv, v_attn_q_gain,
                           v_attn_k_gain, v_attn_w_o, v_ret_w_in, v_ret_decay_logit, v_ret_gn_w, v_ret_w_out, v_ffn_w_up,
                           v_ffn_conv_w, v_ffn_conv_b, v_ffn_w_down)))
    depth, s, l = ada_w.shape[0], x.shape[1], ctx.shape[1]
    assert l == BM and s % BM == 0 and s % GRID_W == 0
    nlb = s // BM
    n_pool = pool_w.shape[0]
    mx, my, mc = _pos()
    chip, dev = 2 * mx + my, 4 * mx + 2 * my + mc
    nada = ada_w.shape[2]

    sharded = [n for n in SMALL if n in SMALL_SHARD_AXIS]
    got = small_all_gather("gather_small", _pack([c[0]] + [P[n] for n in sharded]))
    got = _unpack(got, [(D,)] + [P[n].shape for n in sharded])
    c_all = got[0]
    full = {n: _unshard(g_, SMALL_SHARD_AXIS[n]) for n, g_ in zip(sharded, got[1:])}

    s9 = jnp.concatenate([c_all, c_ctx[None, :], jnp.zeros((ADA_ROWS - N_DEV - 1, D), F32)], axis=0)
    ada_b_mine = lax.dynamic_slice_in_dim(ada_b, chip * nada, nada, axis=1)[:, None, :]
    mod_part = ada_fwd(s9, ada_w, ada_b_mine)
    mod_all = _unshard(small_all_gather("gather_mod", mod_part.reshape(depth * ADA_ROWS, nada)), 1).reshape(depth, ADA_ROWS, 6, D)
    mod_mine = lax.dynamic_index_in_dim(mod_all, dev, axis=1, keepdims=False)
    mods_all = jnp.pad(jnp.stack([mod_mine, mod_all[:, ADA_CTX]], axis=1), ((0, 0), (0, 0), (0, 2), (0, 0)))
    mods = [mods_all[i] for i in range(depth)]

    names = [n for n, _ in BIG]
    layers = _layer_matrices(depth)
    shards = [P[n].astype(BF16) for n in names]
    gsems, shards, fulls = gather_start(shards, names, layers, place_own(shards, names, layers))
    first = [sum(len(layer) for layer in layers[:i]) for i in range(depth + 1)]
    flight = {"shards": shards}

    def fetch(i, after):
        flight["shards"], mats = gather_wait(i, gsems[i], flight["shards"], names, layers, fulls[first[i] : first[i + 1]], after)
        return {name: m for (name, _, _), m in zip(layers[i], mats)}

    lands = {n: lax.empty((len(CHIP_FLIPS),) + P[n].shape, BF16) for n in names}
    sent = {}

    def emit(i, big):
        lnames = [name for name, _, _ in layers[i]]
        sems, gl, ll, token = scatter_start(i, layers[i], [big[n] for n in lnames], [lands[n] for n in lnames])
        lands.update(zip(lnames, ll))
        sent[i] = (sems, gl)
        return token

    w = {
        "nw": full["norm_w"],
        "pbs": jnp.concatenate([full["pool_b"][:, None], full["pool_scale"][:, None], jnp.zeros((n_pool, 6, D), F32)], axis=1),
        "gains": jnp.concatenate([attn_q_gain, attn_k_gain, jnp.zeros((6, HD), F32)], axis=0),
        "gnw": full["ret_gn_w"],
        "logit_b": jnp.broadcast_to(ret_decay_logit[0][:, :, None, None], (2, RET_HEADS, 8, 128)),
        "cw": jnp.concatenate([full["ffn_conv_w"], ffn_conv_b[:, None, :], jnp.zeros((depth, 4, 2 * D_FF), F32)], axis=1),
    }
    w["cos"], w["sin"] = rope_tables(s, l)
    w["sel"] = shift_select()

    xs = jnp.concatenate([x[0], ctx[0]], axis=0)
    loss_tile, dxs, g = local_step(xs, loss_target[0], mods, w, nlb, depth, fetch, emit)
    loss = lax.psum(loss_tile[0, 0], MESH_AXES)
    grad_x = dxs[:s][None]

    small_shapes = [(depth, 2, 8, D), (depth, 2, D), (n_pool, 2, D), (2, HD), (2, RET_HEADS), (RNV,), (depth, 4, 2 * D_FF)]
    slab = _pack([
        jnp.stack(g["dmod"]),
        jnp.stack([jnp.stack([g["dnw1"][i][0], g["dnw2"][i][0]]) for i in range(depth)]),
        jnp.stack([g["dpbs"][j][0:2] for j in range(n_pool)]),
        g["dgains"][0:2], g["dlogit"][:, :, 0, 0], g["dgnw"][0], jnp.stack([g["dcw"][i][0:4] for i in range(depth)]),
    ])
    slabs = small_all_gather("gather_small_grads", slab)
    dmod_dev = _unpack(slabs, small_shapes[:1])[0]
    t_dmod, t_nw, t_pbs, t_gains, t_logit, t_gnw, t_cw = _unpack(reduce_devices("reduce_small_grads", slabs), small_shapes)

    def cols(a):
        return lax.dynamic_slice_in_dim(a, chip * nada, nada, axis=a.ndim - 1)

    dm_lat = jnp.swapaxes(cols(dmod_dev[:, :, 0, :6].reshape(N_DEV, depth, 6 * D)), 0, 1)
    dm_ctx = cols(t_dmod[:, 1, :6].reshape(depth, 1, 6 * D))
    dm = jnp.concatenate([dm_lat, dm_ctx, jnp.zeros((depth, ADA_ROWS - N_DEV - 1, nada), F32)], axis=1)
    g_ada_w, ds9 = ada_bwd(s9, dm, ada_w)
    g_c_ctx = cctx_grad(small_all_gather("gather_dcctx", ds9), c_ctx[None, :])[0]

    def mine(a, name):
        n = P[name].shape[SMALL_SHARD_AXIS[name]]
        return lax.dynamic_slice_in_dim(a, chip * n, n, axis=SMALL_SHARD_AXIS[name])

    G = {
        "c_ctx": g_c_ctx,
        "ada_b": (t_dmod[:, 0, :6] + t_dmod[:, 1, :6]).reshape(depth, 6 * D),
        "norm_w": mine(t_nw, "norm_w"),
        "pool_b": mine(t_pbs[:, 0], "pool_b"), "pool_scale": mine(t_pbs[:, 1], "pool_scale"),
        "attn_q_gain": t_gains[0:1], "attn_k_gain": t_gains[1:2],
        "ret_decay_logit": t_logit[None], "ret_gn_w": mine(t_gnw[None], "ret_gn_w"),
        "ffn_conv_w": mine(t_cw[:, 0:3], "ffn_conv_w"), "ffn_conv_b": t_cw[:, 3],
    }
    sw, sg, sm, sv = (_pack([d_[n] for n in SMALL]) for d_ in (P, G, M, V))
    outs = adamw_one("adamw_small", sw, sm, sv, sg)
    D_, NM, NV = ({n: a for n, a in zip(SMALL, _unpack(o, [P[n].shape for n in SMALL]))} for o in outs)

    flat2 = lambda a: a.reshape(-1, a.shape[-1])
    G["ada_w"] = g_ada_w
    o3 = adamw_one("adamw_ada", flat2(ada_w), flat2(M["ada_w"]), flat2(V["ada_w"]), flat2(g_ada_w))
    D_["ada_w"], NM["ada_w"], NV["ada_w"] = (o.reshape(ada_w.shape) for o in o3)

    sent_grads, landed = scatter_wait([sent[i][0] for i in range(depth)], layers, [a for i in range(depth) for a in sent[i][1]],
                                      names, [lands[n] for n in names], dxs)
    own = {n: [None] * P[n].shape[0] for n in names}
    for (name, idx, axis), a in zip([e for layer in layers for e in layer], sent_grads):
        n_ = a.shape[axis] // N_CHIP
        own[name][idx] = lax.dynamic_slice_in_dim(a, chip * n_, n_, axis=axis)
    partial = [sum_slots("sum_" + n, jnp.stack(own[n]).reshape(-1, lnd.shape[-1]), lnd.reshape(len(CHIP_FLIPS), -1, lnd.shape[-1]))
               for n, lnd in zip(names, landed)]
    theirs = sibling_swap(partial)
    for (n, _), p, ps in zip(BIG, partial, theirs):
        o4 = adamw_pair("adamw_" + n, flat2(P[n]), flat2(M[n]), flat2(V[n]), p, ps)
        G[n], D_[n], NM[n], NV[n] = (o.reshape(P[n].shape) for o in o4)

    return (loss, grad_x, *[G[n] for n in WEIGHTS], *[D_[n] for n in WEIGHTS], *[NM[n] for n in WEIGHTS], *[NV[n] for n in WEIGHTS])
```

```python
import functools

import jax
import jax.numpy as jnp
from jax import lax
from jax.experimental import pallas as pl
from jax.experimental.pallas import tpu as pltpu

F32 = jnp.float32
BF16 = jnp.bfloat16

D = 1024
BM = 256
EPS = 1e-6
POOL_WINDOWS = (2, 4, 8, 16)
POOL_GROUP = D // 4
ATTN_HEADS = 8
ATTN_KV_HEADS = 2
HD = D // ATTN_HEADS
ATTN_GROUP = ATTN_HEADS // ATTN_KV_HEADS
NQ = ATTN_HEADS * HD
NKV = ATTN_KV_HEADS * HD
GRID_W = 64
ROPE_THETA = 10000.0
RET_HEADS = 4
RET_DK = D // RET_HEADS
RET_DV = 2 * D // RET_HEADS
RNQ = RET_HEADS * RET_DK
RNV = RET_HEADS * RET_DV
D_FF = 2816
FF_CHUNK = 256
ADAM_LR, ADAM_B1, ADAM_B2, ADAM_EPS, ADAM_WD, ADAM_STEP = 0.001, 0.9, 0.999, 1e-08, 0.01, 10
HALO_F32 = 8
HALO_BF16 = 16
VMEM_LIMIT = 60 * 1024 * 1024

MESH_AXES = ("x", "y", "c")
N_DEV = 8
N_CHIP = 4


def _pcall(body, **kw):
    return pl.pallas_call(body, **kw)


def _spec(shape, kind, nlb, nblk):
    nd = len(shape)
    if kind == "row":
        return pl.BlockSpec((BM, shape[1]), lambda i: (i, 0))
    if kind == "full":
        return pl.BlockSpec(tuple(shape), lambda i: (0,) * nd, pipeline_mode=pl.Buffered(1))
    if isinstance(kind, tuple) and kind[0] == "fullat":
        return pl.BlockSpec((None,) + tuple(shape[1:]), lambda i: (kind[1],) + (0,) * (nd - 1), pipeline_mode=pl.Buffered(1))
    if kind == "acc":
        return pl.BlockSpec(tuple(shape), lambda i: (0,) * nd)
    if kind == "any":
        return pl.BlockSpec(memory_space=pl.ANY)
    if kind == "stream":
        return pl.BlockSpec((1,) + tuple(shape[1:]), lambda i: (i // nlb,) + (0,) * (nd - 1))
    if kind in ("prev8", "prev16"):
        hb = int(kind[4:])
        return pl.BlockSpec((hb, shape[1]), lambda i: (jnp.maximum(i * (BM // hb) - 1, 0), 0))
    if kind in ("next8", "next16"):
        hb = int(kind[4:])
        last = nblk * BM // hb - 1
        return pl.BlockSpec((hb, shape[1]), lambda i: (jnp.minimum((i + 1) * (BM // hb), last), 0))
    raise ValueError(kind)


def _rowcall(name, body, nblk, nlb, ins, outs, scratch=()):
    return _pcall(
        body,
        name=name,
        grid=(nblk,),
        in_specs=[_spec(a.shape, k, nlb, nblk) for a, k in ins],
        out_specs=[_spec(s, k, nlb, nblk) for s, _, k in outs],
        out_shape=[jax.ShapeDtypeStruct(s, d) for s, d, _ in outs],
        scratch_shapes=list(scratch),
        compiler_params=pltpu.CompilerParams(dimension_semantics=("arbitrary",), vmem_limit_bytes=VMEM_LIMIT),
    )(*[a for a, _ in ins])


def _wk(w):
    return (w[0], ("fullat", w[1])) if isinstance(w, tuple) else (w, "full")


def _stream_edges(i, nlb):
    is_ctx = i == nlb
    return (i == 0) | is_ctx, (i == nlb - 1) | is_ctx, is_ctx


def _dotf(a, b):
    return jnp.dot(a, b, preferred_element_type=F32)


def _dot_nt(a, b):
    return lax.dot_general(a, b, (((1,), (1,)), ((), ())), preferred_element_type=F32)


def _dot_tn(a, b):
    return lax.dot_general(a, b, (((0,), (0,)), ((), ())), preferred_element_type=F32)


def _sigmoid(x):
    return 0.5 * jnp.tanh(0.5 * x) + 0.5


def _norm_mod(x, nw, sh, sc):
    r = lax.rsqrt(jnp.mean(x * x, axis=-1, keepdims=True) + EPS)
    xhat = x * r
    n = xhat * nw
    return n * (1.0 + sc) + sh, n, xhat, r


def _norm_mod_bwd(dh, n, xhat, r, nw, sc):
    dsh = jnp.sum(dh, axis=0, keepdims=True)
    dsc = jnp.sum(dh * n, axis=0, keepdims=True)
    dn = dh * (1.0 + sc)
    dnw = jnp.sum(dn * xhat, axis=0, keepdims=True)
    dxhat = dn * nw
    dx = r * (dxhat - xhat * jnp.mean(dxhat * xhat, axis=-1, keepdims=True))
    return dx, dsh, dsc, dnw


def _acc_init(i, *refs):
    @pl.when(i == 0)
    def _():
        for r in refs:
            r[...] = jnp.zeros(r.shape, r.dtype)


SH1, SC1, G1, SH2, SC2, G2 = range(6)


def _mrow(mod_ref, k):
    return mod_ref[0, k : k + 1, :]


def _shift_rows(x_ref, xp_ref, xn_ref, cs, first, last):
    cur = x_ref[:, cs].astype(F32)
    rows = lax.broadcasted_iota(jnp.int32, cur.shape, 0)
    pr = jnp.where(first, 0.0, xp_ref[HALO_BF16 - 1 : HALO_BF16, cs].astype(F32))
    nx = jnp.where(last, 0.0, xn_ref[0:1, cs].astype(F32))
    dn = jnp.where(rows == 0, pr, pltpu.roll(cur, 1, 0))
    up = jnp.where(rows == BM - 1, nx, pltpu.roll(cur, BM - 1, 0))
    return dn, cur, up


def ffn_up(x1, mod, nw2, w_up, nblk, nlb):
    r = x1.shape[0]

    def body(x_ref, mod_ref, nw_ref, w_ref, u_ref, h_ref):
        h, _, _, _ = _norm_mod(x_ref[...], nw_ref[...], _mrow(mod_ref, SH2), _mrow(mod_ref, SC2))
        hb = h.astype(BF16)
        h_ref[...] = hb
        u_ref[...] = _dotf(hb, w_ref[...]).astype(BF16)

    return _rowcall(
        "ffn_up", body, nblk, nlb,
        [(x1, "row"), (mod, "stream"), (nw2, "full"), _wk(w_up)],
        [((r, 2 * D_FF), BF16, "row"), ((r, D), BF16, "row")],
    )


def _conv_gate_chunk(u_ref, up_ref, un_ref, cw_ref, j, first, last):
    res = []
    for half in range(2):
        c0 = half * D_FF + j * FF_CHUNK
        cs = slice(c0, c0 + FF_CHUNK)
        dn, cur, up = _shift_rows(u_ref, up_ref, un_ref, cs, first, last)
        val = dn * cw_ref[0:1, cs] + cur * cw_ref[1:2, cs] + up * cw_ref[2:3, cs] + cw_ref[3:4, cs]
        res.append((val, dn, cur, up, cs))
    return res


def ffn_down(u, cw, w_down, x1, mod, nblk, nlb):
    r = x1.shape[0]

    def body(u_ref, up_ref, un_ref, cw_ref, w_ref, x_ref, mod_ref, x2_ref, f_ref, gated):
        first, last, _ = _stream_edges(pl.program_id(0), nlb)
        for j in range(D_FF // FF_CHUNK):
            (a, _, _, _, _), (v, _, _, _, _) = _conv_gate_chunk(u_ref, up_ref, un_ref, cw_ref, j, first, last)
            gated[:, j * FF_CHUNK : (j + 1) * FF_CHUNK] = (a * _sigmoid(a) * v).astype(BF16)
        f = _dotf(gated[...], w_ref[...])
        f_ref[...] = f.astype(BF16)
        x2_ref[...] = x_ref[...] + _mrow(mod_ref, G2) * f

    return _rowcall(
        "ffn_down", body, nblk, nlb,
        [(u, "row"), (u, "prev16"), (u, "next16"), _wk(cw), _wk(w_down), (x1, "row"), (mod, "stream")],
        [((r, D), F32, "row"), ((r, D), BF16, "row")],
        scratch=[pltpu.VMEM((BM, D_FF), BF16)],
    )


def ffn_bwd1(dx2, f, u, cw, w_down, mod, dep, nblk, nlb):
    r = dx2.shape[0]

    def body(dx_ref, f_ref, u_ref, up_ref, un_ref, cw_ref, w_ref, mod_ref, dep_ref, gated_ref, df_ref, duc_ref, dcw_ref, dmod_ref,
             dgated):
        i = pl.program_id(0)
        first, last, _ = _stream_edges(i, nlb)
        _acc_init(i, dcw_ref, dmod_ref)
        dx = dx_ref[...]
        df = (_mrow(mod_ref, G2) * dx).astype(BF16)
        df_ref[...] = df
        s = i // nlb
        dmod_ref[s, G2 : G2 + 1, :] += jnp.sum(dx * f_ref[...].astype(F32), axis=0, keepdims=True)
        dgated[...] = _dot_nt(df, w_ref[...])
        for j in range(D_FF // FF_CHUNK):
            (a, adn, acur, aup, acs), (v, vdn, vcur, vup, vcs) = _conv_gate_chunk(u_ref, up_ref, un_ref, cw_ref, j, first, last)
            sa = _sigmoid(a)
            gated_ref[:, j * FF_CHUNK : (j + 1) * FF_CHUNK] = (a * sa * v).astype(BF16)
            dg = dgated[:, j * FF_CHUNK : (j + 1) * FF_CHUNK]
            dv = dg * (a * sa)
            da = dg * v * (sa * (1.0 + a * (1.0 - sa)))
            for dval, dn, cur, up, cs in ((da, adn, acur, aup, acs), (dv, vdn, vcur, vup, vcs)):
                duc_ref[:, cs] = dval.astype(BF16)
                dcw_ref[0:1, cs] += jnp.sum(dval * dn, axis=0, keepdims=True)
                dcw_ref[1:2, cs] += jnp.sum(dval * cur, axis=0, keepdims=True)
                dcw_ref[2:3, cs] += jnp.sum(dval * up, axis=0, keepdims=True)
                dcw_ref[3:4, cs] += jnp.sum(dval, axis=0, keepdims=True)

    return _rowcall(
        "ffn_bwd1", body, nblk, nlb,
        [(dx2, "row"), (f, "row"), (u, "row"), (u, "prev16"), (u, "next16"), _wk(cw), _wk(w_down), (mod, "stream"), (dep, "any")],
        [((r, D_FF), BF16, "row"), ((r, D), BF16, "row"), ((r, 2 * D_FF), BF16, "row"),
         ((8, 2 * D_FF), F32, "acc"), ((2, 8, D), F32, "acc")],
        scratch=[pltpu.VMEM((BM, D_FF), F32)],
    )


def ffn_bwd3(duc, cw, w_up, x1, dx2, mod, nw2, nblk, nlb):
    r = dx2.shape[0]

    def body(d_ref, dp_ref, dn_ref, cw_ref, w_ref, x_ref, dx_ref, mod_ref, nw_ref, du_ref, dx1_ref, dnw_ref, dmod_ref):
        i = pl.program_id(0)
        first, last, _ = _stream_edges(i, nlb)
        _acc_init(i, dnw_ref, dmod_ref)
        for j in range(2 * D_FF // FF_CHUNK):
            cs = slice(j * FF_CHUNK, (j + 1) * FF_CHUNK)
            dn, cur, up = _shift_rows(d_ref, dp_ref, dn_ref, cs, first, last)
            du_ref[:, cs] = (up * cw_ref[0:1, cs] + cur * cw_ref[1:2, cs] + dn * cw_ref[2:3, cs]).astype(BF16)
        dh = _dot_nt(du_ref[...], w_ref[...])
        sc = _mrow(mod_ref, SC2)
        nw = nw_ref[...]
        _, n, xhat, rr = _norm_mod(x_ref[...], nw, _mrow(mod_ref, SH2), sc)
        dxn, dsh, dsc, dnw = _norm_mod_bwd(dh, n, xhat, rr, nw, sc)
        dx1_ref[...] = dx_ref[...] + dxn
        s = i // nlb
        dmod_ref[s, SH2 : SH2 + 1, :] += dsh
        dmod_ref[s, SC2 : SC2 + 1, :] += dsc
        dnw_ref[0:1, :] += dnw

    return _rowcall(
        "ffn_bwd3", body, nblk, nlb,
        [(duc, "row"), (duc, "prev16"), (duc, "next16"), _wk(cw), _wk(w_up), (x1, "row"), (dx2, "row"),
         (mod, "stream"), (nw2, "full")],
        [((r, 2 * D_FF), BF16, "row"), ((r, D), F32, "row"), ((8, D), F32, "acc"), ((2, 8, D), F32, "acc")],
    )


def matmul_tn(a, b, nblk, tn=None):
    k, n = a.shape[1], b.shape[1]
    rows = nblk * BM
    tr = 768 if rows % 768 == 0 else (1024 if rows % 1024 == 0 else BM)
    if tn is None:
        tn = n
        while k * tn * 4 > 6 * 1024 * 1024 and tn % 256 == 0:
            tn //= 2
    steps = rows // tr

    def body(a_ref, b_ref, o_ref, acc):
        t = pl.program_id(1)

        @pl.when(t == 0)
        def _():
            acc[...] = jnp.zeros(acc.shape, acc.dtype)

        acc[...] += _dot_tn(a_ref[...], b_ref[...])

        @pl.when(t == steps - 1)
        def _():
            o_ref[...] = acc[...].astype(o_ref.dtype)

    return _pcall(
        body,
        name="matmul_tn",
        grid=(n // tn, steps),
        in_specs=[pl.BlockSpec((tr, k), lambda j, t: (t, 0)), pl.BlockSpec((tr, tn), lambda j, t: (t, j))],
        out_specs=pl.BlockSpec((k, tn), lambda j, t: (0, j)),
        out_shape=jax.ShapeDtypeStruct((k, n), BF16),
        scratch_shapes=[pltpu.VMEM((k, tn), F32)],
        compiler_params=pltpu.CompilerParams(dimension_semantics=("parallel", "arbitrary"), vmem_limit_bytes=VMEM_LIMIT),
    )(a, b)


EXT = BM + 2 * HALO_F32


def _pool_positions(i, nlb, nrows, row0):
    is_ctx = i == nlb
    t = (i - jnp.where(is_ctx, nlb, 0)) * BM + row0 + lax.broadcasted_iota(jnp.int32, (nrows, 1), 0)
    return t, jnp.where(is_ctx, BM, nlb * BM)


def _pool_cnt(t, win, slen):
    return (jnp.minimum(t + win // 2, slen) - jnp.maximum(t - win // 2, 0)).astype(F32)


def _pool_fill_ext(ext, i, nlb, x_ref, xp_ref, xn_ref, mod_ref, nw_ref):
    first, last, _ = _stream_edges(i, nlb)
    sh, sc, nw = _mrow(mod_ref, SH1), _mrow(mod_ref, SC1), nw_ref[...]
    hcur, n, xhat, r = _norm_mod(x_ref[...], nw, sh, sc)
    ext[0:HALO_F32, :] = jnp.where(first, 0.0, _norm_mod(xp_ref[...], nw, sh, sc)[0])
    ext[HALO_F32 : HALO_F32 + BM, :] = hcur
    ext[HALO_F32 + BM :, :] = jnp.where(last, 0.0, _norm_mod(xn_ref[...], nw, sh, sc)[0])
    return n, xhat, r


def _window_sum(ref, cols, offs):
    acc = None
    for o in offs:
        v = ref[HALO_F32 + o : HALO_F32 + o + BM, cols]
        acc = v if acc is None else acc + v
    return acc


def _pool_diff(ext, g, win, t, slen):
    cols = slice(g * POOL_GROUP, (g + 1) * POOL_GROUP)
    ssum = _window_sum(ext, cols, range(-(win // 2), win // 2))
    return ssum / _pool_cnt(t, win, slen) - ext[HALO_F32 : HALO_F32 + BM, cols]


def pool_fwd(x, mod, nw1, pw, pbs, nblk, nlb):
    r = x.shape[0]

    def body(x_ref, xp_ref, xn_ref, mod_ref, nw_ref, pw_ref, pbs_ref, x1_ref, ypre_ref, ext):
        i = pl.program_id(0)
        _pool_fill_ext(ext, i, nlb, x_ref, xp_ref, xn_ref, mod_ref, nw_ref)
        t, slen = _pool_positions(i, nlb, BM, 0)
        for g, win in enumerate(POOL_WINDOWS):
            cols = slice(g * POOL_GROUP, (g + 1) * POOL_GROUP)
            diff = _pool_diff(ext, g, win, t, slen)
            ypre = _dotf(diff.astype(BF16), pw_ref[g]) + pbs_ref[0:1, cols]
            ypre_ref[:, cols] = ypre
            x1_ref[:, cols] = x_ref[:, cols] + mod_ref[0, G1 : G1 + 1, cols] * (ypre * pbs_ref[1:2, cols])

    return _rowcall(
        "pool_fwd", body, nblk, nlb,
        [(x, "row"), (x, "prev8"), (x, "next8"), (mod, "stream"), (nw1, "full"), _wk(pw), _wk(pbs)],
        [((r, D), F32, "row"), ((r, D), F32, "row")],
        scratch=[pltpu.VMEM((EXT, D), F32)],
    )


def pool_bwd(dx1, x, ypre, mod, nw1, pw, pbs, nblk, nlb):
    r = x.shape[0]

    def body(d_ref, dp_ref, dn_ref, x_ref, xp_ref, xn_ref, ypre_ref, mod_ref, nw_ref, pw_ref, pbs_ref,
             dx_ref, dpw_ref, dpbs_ref, dnw_ref, dmod_ref, ext, dext, eext, dh):
        i = pl.program_id(0)
        first, last, _ = _stream_edges(i, nlb)
        _acc_init(i, dpw_ref, dpbs_ref, dnw_ref, dmod_ref)
        n, xhat, rr = _pool_fill_ext(ext, i, nlb, x_ref, xp_ref, xn_ref, mod_ref, nw_ref)
        g1, scale = _mrow(mod_ref, G1), pbs_ref[1:2, :]
        dcur = d_ref[...]
        ypre = ypre_ref[...]
        s = i // nlb
        dmod_ref[s, G1 : G1 + 1, :] += jnp.sum(dcur * (ypre * scale), axis=0, keepdims=True)
        dy = g1 * dcur
        dpbs_ref[1:2, :] += jnp.sum(dy * ypre, axis=0, keepdims=True)
        dpbs_ref[0:1, :] += jnp.sum(dy * scale, axis=0, keepdims=True)
        gs = g1 * scale
        dext[0:HALO_F32, :] = jnp.where(first, 0.0, gs * dp_ref[...])
        dext[HALO_F32 : HALO_F32 + BM, :] = dy * scale
        dext[HALO_F32 + BM :, :] = jnp.where(last, 0.0, gs * dn_ref[...])
        t, slen = _pool_positions(i, nlb, BM, 0)
        text, _ = _pool_positions(i, nlb, EXT, -HALO_F32)
        for g, win in enumerate(POOL_WINDOWS):
            cols = slice(g * POOL_GROUP, (g + 1) * POOL_GROUP)
            diff = _pool_diff(ext, g, win, t, slen)
            dpre = dext[:, cols].astype(BF16)
            ddiff = _dot_nt(dpre, pw_ref[g])
            eext[...] = ddiff / jnp.maximum(_pool_cnt(text, win, slen), 1.0)
            dh[:, cols] = _window_sum(eext, slice(None), range(-(win // 2) + 1, win // 2 + 1)) - ddiff[HALO_F32 : HALO_F32 + BM, :]
            dpw_ref[g] += _dot_tn(diff.astype(BF16), dpre[HALO_F32 : HALO_F32 + BM, :])
        sc, nw = _mrow(mod_ref, SC1), nw_ref[...]
        dxn, dsh, dsc, dnw = _norm_mod_bwd(dh[...], n, xhat, rr, nw, sc)
        dx_ref[...] = dcur + dxn
        dmod_ref[s, SH1 : SH1 + 1, :] += dsh
        dmod_ref[s, SC1 : SC1 + 1, :] += dsc
        dnw_ref[0:1, :] += dnw

    return _rowcall(
        "pool_bwd", body, nblk, nlb,
        [(dx1, "row"), (dx1, "prev8"), (dx1, "next8"), (x, "row"), (x, "prev8"), (x, "next8"), (ypre, "row"),
         (mod, "stream"), (nw1, "full"), _wk(pw), _wk(pbs)],
        [((r, D), F32, "row"), ((4, POOL_GROUP, POOL_GROUP), F32, "acc"), ((8, D), F32, "acc"), ((8, D), F32, "acc"),
         ((2, 8, D), F32, "acc")],
        scratch=[pltpu.VMEM((EXT, D), F32), pltpu.VMEM((EXT, D), F32), pltpu.VMEM((EXT, POOL_GROUP), F32), pltpu.VMEM((BM, D), F32)],
    )


def rope_tables(s, l):
    rows = s // GRID_W
    row = jnp.broadcast_to(jnp.arange(rows)[:, None], (rows, GRID_W)).reshape(-1).astype(F32)
    col = jnp.broadcast_to(jnp.arange(GRID_W)[None, :], (rows, GRID_W)).reshape(-1).astype(F32)
    axis_dim = HD // 2
    inv = ROPE_THETA ** (-jnp.arange(0, axis_dim, 2, dtype=F32) / axis_dim)
    ar, ac = row[:, None] * inv, col[:, None] * inv
    cos = jnp.concatenate([jnp.cos(ar), jnp.cos(ar), jnp.cos(ac), jnp.cos(ac)], axis=-1)
    sin = jnp.concatenate([-jnp.sin(ar), jnp.sin(ar), -jnp.sin(ac), jnp.sin(ac)], axis=-1)
    return (jnp.concatenate([cos, jnp.ones((l, HD), F32)], axis=0), jnp.concatenate([sin, jnp.zeros((l, HD), F32)], axis=0))


def _partner(x):
    q = HD // 4
    lane = lax.broadcasted_iota(jnp.int32, x.shape, 1)
    return jnp.where((lane // q) % 2 == 0, pltpu.roll(x, HD - q, 1), pltpu.roll(x, q, 1))


def _head_norm(raw, gain):
    r = lax.rsqrt(jnp.mean(raw * raw, axis=-1, keepdims=True) + EPS)
    return raw * r, r


def qkv_fwd(x, mod, nw1, w_qkv, gains, cos_t, sin_t, nblk, nlb):
    r = x.shape[0]

    def body(x_ref, mod_ref, nw_ref, w_ref, g_ref, c_ref, s_ref, raw_ref, q_ref, k_ref, v_ref, h_ref):
        h = _norm_mod(x_ref[...], nw_ref[...], _mrow(mod_ref, SH1), _mrow(mod_ref, SC1))[0].astype(BF16)
        h_ref[...] = h
        raw_ref[...] = _dotf(h, w_ref[...])
        cos, sin = c_ref[...], s_ref[...]
        for j in range(ATTN_HEADS + ATTN_KV_HEADS):
            isq = j < ATTN_HEADS
            xn = _head_norm(raw_ref[:, j * HD : (j + 1) * HD], None)[0] * (g_ref[0:1, :] if isq else g_ref[1:2, :])
            rot = (xn * cos + _partner(xn) * sin).astype(BF16)
            if isq:
                q_ref[:, j * HD : (j + 1) * HD] = rot
            else:
                k_ref[:, (j - ATTN_HEADS) * HD : (j - ATTN_HEADS + 1) * HD] = rot
        v_ref[...] = raw_ref[:, NQ + NKV :].astype(BF16)

    return _rowcall(
        "qkv_fwd", body, nblk, nlb,
        [(x, "row"), (mod, "stream"), (nw1, "full"), (w_qkv, "full"), (gains, "full"), (cos_t, "row"), (sin_t, "row")],
        [((r, NQ + 2 * NKV), F32, "row"), ((r, NQ), BF16, "row"), ((r, NKV), BF16, "row"), ((r, NKV), BF16, "row"),
         ((r, D), BF16, "row")],
    )


def attn_fwd(q, k, v, nblk, nlb):
    r = q.shape[0]
    scale = HD ** -0.5

    def body(q_ref, k_ref, v_ref, o_ref, lse_ref):
        is_ctx = pl.program_id(0) == nlb
        hide = is_ctx & (lax.broadcasted_iota(jnp.int32, (1, r), 1) < nlb * BM)
        for kvh in range(ATTN_KV_HEADS):
            kh = k_ref[:, kvh * HD : (kvh + 1) * HD]
            vh = v_ref[:, kvh * HD : (kvh + 1) * HD]
            for g in range(ATTN_GROUP):
                cs = slice((kvh * ATTN_GROUP + g) * HD, (kvh * ATTN_GROUP + g + 1) * HD)
                s = jnp.where(hide, -1e30, _dot_nt(q_ref[:, cs], kh) * scale)
                m = jnp.max(s, axis=-1, keepdims=True)
                p = jnp.exp(s - m)
                l = jnp.sum(p, axis=-1, keepdims=True)
                o_ref[:, cs] = (_dotf(p.astype(BF16), vh) / l).astype(BF16)
                j = kvh * ATTN_GROUP + g
                lse_ref[:, j : j + 1] = m + jnp.log(l)

    return _rowcall(
        "attn_fwd", body, nblk, nlb,
        [(q, "row"), (k, "full"), (v, "full")],
        [((r, NQ), BF16, "row"), ((r, ATTN_HEADS), F32, "row")],
    )


def attn_out_fwd(o, w_o, x, mod, nblk, nlb):
    r = x.shape[0]

    def body(o_ref, w_ref, x_ref, mod_ref, x1_ref, y_ref):
        y = _dotf(o_ref[...], w_ref[...])
        y_ref[...] = y
        x1_ref[...] = x_ref[...] + _mrow(mod_ref, G1) * y

    return _rowcall(
        "attn_out_fwd", body, nblk, nlb,
        [(o, "row"), (w_o, "full"), (x, "row"), (mod, "stream")],
        [((r, D), F32, "row"), ((r, D), F32, "row")],
    )


def mix_out_bwd(name, dx1, y, w_out, mod, nblk, nlb):
    r = dx1.shape[0]
    kin = w_out.shape[0]

    def body(d_ref, y_ref, w_ref, mod_ref, dy_ref, do_ref, dmod_ref):
        i = pl.program_id(0)
        _acc_init(i, dmod_ref)
        d = d_ref[...]
        dmod_ref[i // nlb, G1 : G1 + 1, :] += jnp.sum(d * y_ref[...], axis=0, keepdims=True)
        dy = (_mrow(mod_ref, G1) * d).astype(BF16)
        dy_ref[...] = dy
        do_ref[...] = _dot_nt(dy, w_ref[...]).astype(do_ref.dtype)

    return _rowcall(
        name, body, nblk, nlb,
        [(dx1, "row"), (y, "row"), (w_out, "full"), (mod, "stream")],
        [((r, D), BF16, "row"), ((r, kin), BF16, "row"), ((2, 8, D), F32, "acc")],
    )


ATTN_KCHUNK = 11 * BM


def attn_bwd(q, k, v, o, do, lse, nblk, nlb):
    r = q.shape[0]
    scale = HD ** -0.5
    kc = ATTN_KCHUNK if r % ATTN_KCHUNK == 0 else BM
    nkc = r // kc

    def body(q_ref, k_ref, v_ref, o_ref, do_ref, lse_ref, dq_ref, dk_ref, dv_ref):
        i = pl.program_id(0)
        _acc_init(i, dk_ref, dv_ref)
        is_ctx = i == nlb
        for kvh in range(ATTN_KV_HEADS):
            ks = slice(kvh * HD, (kvh + 1) * HD)
            for g in range(ATTN_GROUP):
                j = kvh * ATTN_GROUP + g
                cs = slice(j * HD, (j + 1) * HD)
                qh, doh = q_ref[:, cs], do_ref[:, cs]
                delta = jnp.sum(doh.astype(F32) * o_ref[:, cs].astype(F32), axis=-1, keepdims=True)
                lse = lse_ref[:, j : j + 1]
                dq = jnp.zeros((BM, HD), F32)
                for c in range(nkc):
                    rs = slice(c * kc, (c + 1) * kc)
                    kh, vh = k_ref[rs, ks], v_ref[rs, ks]
                    hide = is_ctx & (c * kc + lax.broadcasted_iota(jnp.int32, (1, kc), 1) < nlb * BM)
                    p = jnp.where(hide, 0.0, jnp.exp(_dot_nt(qh, kh) * scale - lse))
                    ds = (p * (_dot_nt(doh, vh) - delta) * scale).astype(BF16)
                    dq = dq + _dotf(ds, kh)
                    dk_ref[rs, ks] += _dot_tn(ds, qh)
                    dv_ref[rs, ks] += _dot_tn(p.astype(BF16), doh)
                dq_ref[:, cs] = dq

    return _rowcall(
        "attn_bwd", body, nblk, nlb,
        [(q, "row"), (k, "full"), (v, "full"), (o, "row"), (do, "row"), (lse, "row")],
        [((r, NQ), F32, "row"), ((r, NKV), F32, "acc"), ((r, NKV), F32, "acc")],
    )


def qkv_bwd(dq, dk, dv, raw, gains, cos_t, sin_t, w_qkv, x, dx1, mod, nw1, nblk, nlb, ctx_dx_zero):
    r = x.shape[0]

    def body(dq_ref, dk_ref, dv_ref, raw_ref, g_ref, c_ref, s_ref, w_ref, x_ref, dx1_ref, mod_ref, nw_ref,
             dx_ref, draw_ref, dg_ref, dnw_ref, dmod_ref):
        i = pl.program_id(0)
        _acc_init(i, dg_ref, dnw_ref, dmod_ref)
        cos, sin = c_ref[...], s_ref[...]
        for j in range(ATTN_HEADS + ATTN_KV_HEADS):
            isq = j < ATTN_HEADS
            cs = slice(j * HD, (j + 1) * HD)
            dr = dq_ref[:, cs] if isq else dk_ref[:, (j - ATTN_HEADS) * HD : (j - ATTN_HEADS + 1) * HD]
            dxn = dr * cos + _partner(dr * sin)
            xhat, rr = _head_norm(raw_ref[:, cs], None)
            gi = 0 if isq else 1
            dg_ref[gi : gi + 1, :] += jnp.sum(dxn * xhat, axis=0, keepdims=True)
            dxhat = dxn * g_ref[gi : gi + 1, :]
            draw_ref[:, cs] = (rr * (dxhat - xhat * jnp.mean(dxhat * xhat, axis=-1, keepdims=True))).astype(BF16)
        draw_ref[:, NQ + NKV :] = dv_ref[...].astype(BF16)
        dh = _dot_nt(draw_ref[...], w_ref[...])
        sc, nw = _mrow(mod_ref, SC1), nw_ref[...]
        _, n, xhat, rr = _norm_mod(x_ref[...], nw, _mrow(mod_ref, SH1), sc)
        dxn, dsh, dsc, dnw = _norm_mod_bwd(dh, n, xhat, rr, nw, sc)
        dres = dx1_ref[...]
        if ctx_dx_zero:
            dres = jnp.where(i == nlb, 0.0, dres)
        dx_ref[...] = dres + dxn
        s = i // nlb
        dmod_ref[s, SH1 : SH1 + 1, :] += dsh
        dmod_ref[s, SC1 : SC1 + 1, :] += dsc
        dnw_ref[0:1, :] += dnw

    return _rowcall(
        "qkv_bwd", body, nblk, nlb,
        [(dq, "row"), (dk, "row"), (dv, "row"), (raw, "row"), (gains, "full"), (cos_t, "row"), (sin_t, "row"),
         (w_qkv, "full"), (x, "row"), (dx1, "row"), (mod, "stream"), (nw1, "full")],
        [((r, D), F32, "row"), ((r, NQ + 2 * NKV), BF16, "row"), ((8, HD), F32, "acc"), ((8, D), F32, "acc"),
         ((2, 8, D), F32, "acc")],
    )


RET_KSCALE = RET_DK ** -0.5


def ret_in_fwd(x, mod, nw1, w_in, nblk, nlb):
    r = x.shape[0]

    def body(x_ref, mod_ref, nw_ref, w_ref, q_ref, k_ref, v_ref, g_ref, h_ref):
        h = _norm_mod(x_ref[...], nw_ref[...], _mrow(mod_ref, SH1), _mrow(mod_ref, SC1))[0].astype(BF16)
        h_ref[...] = h
        q_ref[...] = _dotf(h, w_ref[:, 0:RNQ]).astype(BF16)
        k_ref[...] = (_dotf(h, w_ref[:, RNQ : 2 * RNQ]) * RET_KSCALE).astype(BF16)
        v_ref[...] = _dotf(h, w_ref[:, 2 * RNQ : 2 * RNQ + RNV]).astype(BF16)
        g_ref[...] = _dotf(h, w_ref[:, 2 * RNQ + RNV :]).astype(BF16)

    return _rowcall(
        "ret_in_fwd", body, nblk, nlb,
        [(x, "row"), (mod, "stream"), (nw1, "full"), (w_in, "full")],
        [((r, RNQ), BF16, "row"), ((r, RNQ), BF16, "row"), ((r, RNV), BF16, "row"), ((r, RNV), BF16, "row"), ((r, D), BF16, "row")],
    )


def _log_sigmoid(x):
    return jnp.minimum(x, 0.0) - jnp.log(1.0 + jnp.exp(-jnp.abs(x)))


def _ret_decays(lg, reverse):
    c = BM
    i = lax.broadcasted_iota(jnp.int32, (c, c), 0)
    j = lax.broadcasted_iota(jnp.int32, (c, c), 1)
    diff = (j - i) if reverse else (i - j)
    ediff = jnp.maximum(diff, 0).astype(F32)
    dm = jnp.where(diff >= 0, jnp.exp(ediff * lg), 0.0)
    rr = lax.broadcasted_iota(jnp.int32, (c, 1), 0).astype(F32)
    eq = (c - rr) if reverse else (rr + 1.0)
    ek = rr if reverse else (c - 1.0 - rr)
    return dm, ediff, jnp.exp(eq * lg), eq, jnp.exp(ek * lg), ek, jnp.exp(c * lg)


def _ret_chunk_index(nlb):
    return (lambda s: jnp.where(s == 0, nlb, s - 1)), (lambda s: jnp.where(s == 0, nlb, nlb - s))


def ret_scan_fwd(q, k, v, logit_b, nlb):
    r = q.shape[0]
    nb = nlb + 1
    fidx, bidx = _ret_chunk_index(nlb)

    def body(qf, kf, vf, qb, kb, vb, lg_ref, of_ref, ob_ref, rf_ref, rb_ref, stf, stb):
        s = pl.program_id(1)

        @pl.when(s == 0)
        def _():
            stf[...] = jnp.zeros(stf.shape, F32)
            stb[...] = jnp.zeros(stb.shape, F32)

        for d, (q_ref, k_ref, v_ref, o_ref, rs_ref, st) in enumerate(((qf, kf, vf, of_ref, rf_ref, stf), (qb, kb, vb, ob_ref, rb_ref, stb))):
            lg = _log_sigmoid(lg_ref[d, 0])[0:1, 0:1]
            dm, _, qd, _, kd, _, gc = _ret_decays(lg, d == 1)
            qq, kk, vv, st0 = q_ref[...], k_ref[...], v_ref[...], st[...]
            rs_ref[0, 0] = st0
            a = _dot_nt(qq, kk) * dm
            o = _dotf(a.astype(BF16), vv) + _dotf(qq, st0.astype(BF16)) * qd
            o_ref[...] = jnp.where(s == 0, 0.0, o)
            st[...] = st0 * gc + _dot_tn((kk.astype(F32) * kd).astype(BF16), vv)

    qspec = lambda f: pl.BlockSpec((BM, RET_DK), lambda h, s: (f(s), h))
    vspec = lambda f: pl.BlockSpec((BM, RET_DV), lambda h, s: (f(s), h))
    sspec = pl.BlockSpec((1, 1, RET_DK, RET_DV), lambda h, s: (h, s, 0, 0))
    return _pcall(
        body,
        name="ret_scan_fwd",
        grid=(RET_HEADS, nb),
        in_specs=[qspec(fidx), qspec(fidx), vspec(fidx), qspec(bidx), qspec(bidx), vspec(bidx),
                  pl.BlockSpec((2, 1, 8, 128), lambda h, s: (0, h, 0, 0))],
        out_specs=[vspec(fidx), vspec(bidx), sspec, sspec],
        out_shape=[jax.ShapeDtypeStruct((r, RNV), F32), jax.ShapeDtypeStruct((r, RNV), F32),
                   jax.ShapeDtypeStruct((RET_HEADS, nb, RET_DK, RET_DV), F32), jax.ShapeDtypeStruct((RET_HEADS, nb, RET_DK, RET_DV), F32)],
        scratch_shapes=[pltpu.VMEM((RET_DK, RET_DV), F32), pltpu.VMEM((RET_DK, RET_DV), F32)],
        compiler_params=pltpu.CompilerParams(dimension_semantics=("parallel", "arbitrary"), vmem_limit_bytes=VMEM_LIMIT),
    )(q, k, v, q, k, v, logit_b)


def _group_norm(o):
    mu = jnp.mean(o, axis=-1, keepdims=True)
    oc = o - mu
    rstd = lax.rsqrt(jnp.mean(oc * oc, axis=-1, keepdims=True) + EPS)
    return oc * rstd, rstd


def ret_out_fwd(o_f, o_b, g, gnw, w_out, x, mod, nblk, nlb):
    r = x.shape[0]

    def body(of_ref, ob_ref, g_ref, gn_ref, w_ref, x_ref, mod_ref, x1_ref, y_ref, z_ref):
        for hh in range(RET_HEADS):
            cs = slice(hh * RET_DV, (hh + 1) * RET_DV)
            yhat, _ = _group_norm(of_ref[:, cs] + ob_ref[:, cs])
            gg = g_ref[:, cs].astype(F32)
            z_ref[:, cs] = (gg * _sigmoid(gg) * (yhat * gn_ref[0:1, cs])).astype(BF16)
        y = _dotf(z_ref[...], w_ref[...])
        y_ref[...] = y
        x1_ref[...] = x_ref[...] + _mrow(mod_ref, G1) * y

    return _rowcall(
        "ret_out_fwd", body, nblk, nlb,
        [(o_f, "row"), (o_b, "row"), (g, "row"), (gnw, "full"), (w_out, "full"), (x, "row"), (mod, "stream")],
        [((r, D), F32, "row"), ((r, D), F32, "row"), ((r, RNV), BF16, "row")],
    )


def ret_out_bwd(dx1, y, o_f, o_b, g, gnw, w_out, mod, nblk, nlb):
    r = dx1.shape[0]

    def body(d_ref, y_ref, of_ref, ob_ref, g_ref, gn_ref, w_ref, mod_ref, dy_ref, do_ref, dg_ref, dgn_ref, dmod_ref, dz):
        i = pl.program_id(0)
        _acc_init(i, dgn_ref, dmod_ref)
        d = d_ref[...]
        dmod_ref[i // nlb, G1 : G1 + 1, :] += jnp.sum(d * y_ref[...], axis=0, keepdims=True)
        dy = (_mrow(mod_ref, G1) * d).astype(BF16)
        dy_ref[...] = dy
        dz[...] = _dot_nt(dy, w_ref[...])
        for hh in range(RET_HEADS):
            cs = slice(hh * RET_DV, (hh + 1) * RET_DV)
            yhat, rstd = _group_norm(of_ref[:, cs] + ob_ref[:, cs])
            gg = g_ref[:, cs].astype(F32)
            sg = _sigmoid(gg)
            gn = gn_ref[0:1, cs]
            dzz = dz[:, cs]
            dg_ref[:, cs] = (dzz * (yhat * gn) * (sg * (1.0 + gg * (1.0 - sg)))).astype(BF16)
            dyn = dzz * (gg * sg)
            dgn_ref[0:1, cs] += jnp.sum(dyn * yhat, axis=0, keepdims=True)
            dyh = dyn * gn
            do = rstd * (dyh - jnp.mean(dyh, axis=-1, keepdims=True) - yhat * jnp.mean(dyh * yhat, axis=-1, keepdims=True))
            do_ref[:, cs] = do.astype(BF16)

    return _rowcall(
        "ret_out_bwd", body, nblk, nlb,
        [(dx1, "row"), (y, "row"), (o_f, "row"), (o_b, "row"), (g, "row"), (gnw, "full"), (w_out, "full"), (mod, "stream")],
        [((r, D), BF16, "row"), ((r, RNV), BF16, "row"), ((r, RNV), BF16, "row"), ((8, RNV), F32, "acc"), ((2, 8, D), F32, "acc")],
        scratch=[pltpu.VMEM((BM, RNV), F32)],
    )


def ret_scan_bwd(q, k, v, do, st_f, st_b, logit_b, nlb):
    r = q.shape[0]
    nb = nlb + 1
    fidx, bidx = _ret_chunk_index(nlb)
    step = lambda t: nb - 1 - t

    def body(qf, kf, vf, dof, rf, qb, kb, vb, dob, rb, lg_ref,
             dqf, dkf, dvf, dqb, dkb, dvb, dlg_ref, drf, drb):
        t = pl.program_id(1)
        s = step(t)

        @pl.when(t == 0)
        def _():
            drf[...] = jnp.zeros(drf.shape, F32)
            drb[...] = jnp.zeros(drb.shape, F32)
            dlg_ref[...] = jnp.zeros(dlg_ref.shape, F32)

        dirs = ((qf, kf, vf, dof, rf, dqf, dkf, dvf, drf), (qb, kb, vb, dob, rb, dqb, dkb, dvb, drb))
        for d, (q_ref, k_ref, v_ref, do_ref, rs_ref, dq_ref, dk_ref, dv_ref, dr) in enumerate(dirs):
            lg = _log_sigmoid(lg_ref[d, 0])[0:1, 0:1]
            dm, ediff, qd, eq, kd, ek, gc = _ret_decays(lg, d == 1)
            qq, kk, vv = q_ref[...], k_ref[...], v_ref[...]
            dob16 = jnp.where(s == 0, jnp.zeros((), BF16), do_ref[...])
            do32 = dob16.astype(F32)
            st0 = rs_ref[0, 0]
            st16 = st0.astype(BF16)
            dr0 = dr[...]
            dr16 = dr0.astype(BF16)
            a = _dot_nt(qq, kk) * dm
            daf = _dot_nt(dob16, vv)
            ds = (daf * dm).astype(BF16)
            qr = _dotf(qq, st16)
            k32 = kk.astype(F32)
            kdec = (k32 * kd).astype(BF16)
            dv_ref[...] = (_dot_tn(a.astype(BF16), dob16) + _dotf(kdec, dr16)).astype(BF16)
            dq_ref[...] = (_dotf(ds, kk) + _dot_nt(dob16, st16) * qd).astype(BF16)
            vdr = _dot_nt(vv, dr16)
            dk_ref[...] = (_dot_tn(ds, qq) + vdr * kd).astype(BF16)
            tot = (jnp.sum(daf * a * ediff)
                   + jnp.sum(eq * qd * jnp.sum(do32 * qr, axis=-1, keepdims=True))
                   + jnp.sum(ek * kd * jnp.sum(k32 * vdr, axis=-1, keepdims=True))
                   + jnp.sum(BM * gc * jnp.sum(dr0 * st0, axis=-1, keepdims=True)))
            dlg_ref[d, 0] += tot
            dr[...] = gc * dr0 + _dot_tn(qq, (do32 * qd).astype(BF16))

        @pl.when(t == nb - 1)
        def _():
            dlg_ref[...] = dlg_ref[...] * _sigmoid(-lg_ref[...])

    qspec = lambda f: pl.BlockSpec((BM, RET_DK), lambda h, t: (f(step(t)), h))
    vspec = lambda f: pl.BlockSpec((BM, RET_DV), lambda h, t: (f(step(t)), h))
    sspec = pl.BlockSpec((1, 1, RET_DK, RET_DV), lambda h, t: (h, step(t), 0, 0))
    lspec = pl.BlockSpec((2, 1, 8, 128), lambda h, t: (0, h, 0, 0))
    sq, sv = jax.ShapeDtypeStruct((r, RNQ), BF16), jax.ShapeDtypeStruct((r, RNV), BF16)
    return _pcall(
        body,
        name="ret_scan_bwd",
        grid=(RET_HEADS, nb),
        in_specs=[qspec(fidx), qspec(fidx), vspec(fidx), vspec(fidx), sspec,
                  qspec(bidx), qspec(bidx), vspec(bidx), vspec(bidx), sspec, lspec],
        out_specs=[qspec(fidx), qspec(fidx), vspec(fidx), qspec(bidx), qspec(bidx), vspec(bidx), lspec],
        out_shape=[sq, sq, sv, sq, sq, sv, jax.ShapeDtypeStruct((2, RET_HEADS, 8, 128), F32)],
        scratch_shapes=[pltpu.VMEM((RET_DK, RET_DV), F32), pltpu.VMEM((RET_DK, RET_DV), F32)],
        compiler_params=pltpu.CompilerParams(dimension_semantics=("parallel", "arbitrary"), vmem_limit_bytes=VMEM_LIMIT),
    )(q, k, v, do, st_f, q, k, v, do, st_b, logit_b)


def ret_in_bwd(dqf, dkf, dvf, dqb, dkb, dvb, dgate, w_in, x, dx1, mod, nw1, nblk, nlb):
    r = x.shape[0]
    nin = 2 * RNQ + 2 * RNV

    def body(dqf_ref, dkf_ref, dvf_ref, dqb_ref, dkb_ref, dvb_ref, dg_ref, w_ref, x_ref, dx1_ref, mod_ref, nw_ref,
             dx_ref, din_ref, dnw_ref, dmod_ref):
        i = pl.program_id(0)
        _acc_init(i, dnw_ref, dmod_ref)
        is_ctx = i == nlb
        din_ref[:, 0:RNQ] = (dqf_ref[...].astype(F32) + dqb_ref[...].astype(F32)).astype(BF16)
        din_ref[:, RNQ : 2 * RNQ] = ((dkf_ref[...].astype(F32) + dkb_ref[...].astype(F32)) * RET_KSCALE).astype(BF16)
        din_ref[:, 2 * RNQ : 2 * RNQ + RNV] = (dvf_ref[...].astype(F32) + dvb_ref[...].astype(F32)).astype(BF16)
        din_ref[:, 2 * RNQ + RNV :] = jnp.where(is_ctx, jnp.zeros((), BF16), dg_ref[...])
        dh = _dot_nt(din_ref[...], w_ref[...])
        sc, nw = _mrow(mod_ref, SC1), nw_ref[...]
        _, n, xhat, rr = _norm_mod(x_ref[...], nw, _mrow(mod_ref, SH1), sc)
        dxn, dsh, dsc, dnw = _norm_mod_bwd(dh, n, xhat, rr, nw, sc)
        dx_ref[...] = jnp.where(is_ctx, 0.0, dx1_ref[...]) + dxn
        s = i // nlb
        dmod_ref[s, SH1 : SH1 + 1, :] += dsh
        dmod_ref[s, SC1 : SC1 + 1, :] += dsc
        dnw_ref[0:1, :] += dnw

    return _rowcall(
        "ret_in_bwd", body, nblk, nlb,
        [(dqf, "row"), (dkf, "row"), (dvf, "row"), (dqb, "row"), (dkb, "row"), (dvb, "row"), (dgate, "row"),
         (w_in, "full"), (x, "row"), (dx1, "row"), (mod, "stream"), (nw1, "full")],
        [((r, D), F32, "row"), ((r, nin), BF16, "row"), ((8, D), F32, "acc"), ((2, 8, D), F32, "acc")],
    )


def loss_head(xout, target, nlb):
    r = xout.shape[0]

    def body(x_ref, t_ref, dx_ref, l_ref):
        _acc_init(pl.program_id(0), l_ref)
        err = x_ref[...] - t_ref[...]
        dx_ref[...] = err * (1.0 / D)
        l_ref[...] += 0.5 * jnp.sum(jnp.mean(err * err, axis=-1, keepdims=True))

    return _rowcall(
        "loss_head", body, nlb, nlb,
        [(xout, "row"), (target, "row")],
        [((r, D), F32, "row"), ((8, 128), F32, "acc")],
    )


N_MIXERS = 3
POOL, ATTN, RET = range(3)


def _layer_plan(depth):
    plan = []
    for i in range(depth):
        kind = i % N_MIXERS
        ctx_out = any(k % N_MIXERS != POOL for k in range(i + 1, depth))
        plan.append((kind, i // N_MIXERS, ctx_out or kind != POOL, ctx_out))
    return plan


def local_step(xs, target, mods, w, nlb, depth, fetch, emit):
    nb = nlb + 1
    plan = _layer_plan(depth)
    saved = []
    x = xs
    for i, (kind, j, ctx_in, ctx_out) in enumerate(plan):
        nmix = nb if ctx_out else nlb
        mod, nw1, nw2 = mods[i], w["nw"][i, 0:1], w["nw"][i, 1:2]
        lw = fetch(i, x)
        sv = {"x": x, "lw": lw}
        if kind == POOL:
            x1, sv["ypre"] = pool_fwd(x, mod, nw1, lw["pool_w"], (w["pbs"], j), nmix, nlb)
        elif kind == ATTN:
            assert ctx_out
            sv["raw"], sv["q"], sv["k"], sv["v"], sv["h"] = qkv_fwd(x, mod, nw1, lw["attn_w_qkv"], w["gains"], w["cos"], w["sin"], nb, nlb)
            sv["o"], sv["lse"] = attn_fwd(sv["q"], sv["k"], sv["v"], nb, nlb)
            x1, sv["y"] = attn_out_fwd(sv["o"], lw["attn_w_o"], x, mod, nb, nlb)
        else:
            assert ctx_in and not ctx_out
            sv["q"], sv["k"], sv["v"], sv["g"], sv["h"] = ret_in_fwd(x, mod, nw1, lw["ret_w_in"], nb, nlb)
            sv["o_f"], sv["o_b"], sv["st_f"], sv["st_b"] = ret_scan_fwd(sv["q"], sv["k"], sv["v"], w["logit_b"], nlb)
            x1, sv["y"], sv["z"] = ret_out_fwd(sv["o_f"], sv["o_b"], sv["g"], w["gnw"], lw["ret_w_out"], x, mod, nlb, nlb)
        sv["x1"] = x1
        sv["u"], sv["h2"] = ffn_up(x1, mod, nw2, lw["ffn_w_up"], nmix, nlb)
        x, sv["f"] = ffn_down(sv["u"], (w["cw"], i), lw["ffn_w_down"], x1, mod, nmix, nlb)
        saved.append(sv)

    dx, loss_tile = loss_head(x, target, nlb)
    g = {k: [None] * depth for k in ("dcw", "dnw1", "dnw2", "dmod")}
    dep = loss_tile
    for i in reversed(range(depth)):
        kind, j, ctx_in, ctx_out = plan[i]
        sv = saved[i]
        lw, big = sv["lw"], {}
        nmix = nb if ctx_out else nlb
        mod, nw1, nw2 = mods[i], w["nw"][i, 0:1], w["nw"][i, 1:2]
        gated, df, duc, g["dcw"][i], dmod = ffn_bwd1(dx, sv["f"], sv["u"], (w["cw"], i), lw["ffn_w_down"], mod, dep, nmix, nlb)
        big["ffn_w_down"] = matmul_tn(gated, df, nmix)
        du, dx1, g["dnw2"][i], dm = ffn_bwd3(duc, (w["cw"], i), lw["ffn_w_up"], sv["x1"], dx, mod, nw2, nmix, nlb)
        dmod = dmod + dm
        big["ffn_w_up"] = matmul_tn(sv["h2"], du, nmix)
        if kind == POOL:
            dx, dpw, dpbs, g["dnw1"][i], dm = pool_bwd(dx1, sv["x"], sv["ypre"], mod, nw1, lw["pool_w"], (w["pbs"], j), nmix, nlb)
            big["pool_w"] = dpw.astype(BF16)
            g.setdefault("dpbs", {})[j] = dpbs
        elif kind == ATTN:
            dy, do, dm1 = mix_out_bwd("attn_out_bwd", dx1, sv["y"], lw["attn_w_o"], mod, nb, nlb)
            big["attn_w_o"] = matmul_tn(sv["o"], dy, nb)
            dq, dk, dv = attn_bwd(sv["q"], sv["k"], sv["v"], sv["o"], do, sv["lse"], nb, nlb)
            dx, draw, g["dgains"], g["dnw1"][i], dm = qkv_bwd(
                dq, dk, dv, sv["raw"], w["gains"], w["cos"], w["sin"], lw["attn_w_qkv"], sv["x"], dx1, mod, nw1, nb, nlb, False)
            big["attn_w_qkv"] = matmul_tn(sv["h"], draw, nb)
            dm = dm + dm1
        else:
            dy, do, dgate, g["dgnw"], dm1 = ret_out_bwd(dx1, sv["y"], sv["o_f"], sv["o_b"], sv["g"], w["gnw"], lw["ret_w_out"], mod, nlb, nlb)
            big["ret_w_out"] = matmul_tn(sv["z"], dy, nlb)
            dqf, dkf, dvf, dqb, dkb, dvb, g["dlogit"] = ret_scan_bwd(sv["q"], sv["k"], sv["v"], do, sv["st_f"], sv["st_b"], w["logit_b"], nlb)
            dx, din, g["dnw1"][i], dm = ret_in_bwd(dqf, dkf, dvf, dqb, dkb, dvb, dgate, lw["ret_w_in"], sv["x"], dx1, mod, nw1, nb, nlb)
            big["ret_w_in"] = matmul_tn(sv["h"], din, nb)
            dm = dm + dm1
        g["dmod"][i] = dmod + dm
        dep = emit(i, big)
    return loss_tile, dx, g


MESH_ID = pl.DeviceIdType.MESH
CHIP_FLIPS = ((1, 0), (0, 1), (1, 1))


def _pos():
    return lax.axis_index("x"), lax.axis_index("y"), lax.axis_index("c")


def _flip(v, b):
    return 1 - v if b else v


def small_all_gather(name, x):
    rows, n = x.shape

    def body(x_ref, out_ref, send_sems, recv_sems, local_sem):
        mx, my, mc = _pos()
        me = 4 * mx + 2 * my + mc
        mine = pltpu.make_async_copy(x_ref, out_ref.at[me], local_sem)
        mine.start()
        sends, peers = [], []
        for kk in range(1, N_DEV):
            peer = (_flip(mx, (kk >> 2) & 1), _flip(my, (kk >> 1) & 1), _flip(mc, kk & 1))
            cp = pltpu.make_async_remote_copy(src_ref=x_ref, dst_ref=out_ref.at[me], send_sem=send_sems.at[kk - 1],
                                              recv_sem=recv_sems.at[kk - 1], device_id=peer, device_id_type=MESH_ID)
            cp.start()
            sends.append(cp)
            peers.append(peer)
        for kk, peer in enumerate(peers):
            pidx = 4 * peer[0] + 2 * peer[1] + peer[2]
            pltpu.make_async_remote_copy(src_ref=x_ref, dst_ref=out_ref.at[pidx], send_sem=send_sems.at[kk],
                                         recv_sem=recv_sems.at[kk], device_id=peer, device_id_type=MESH_ID).wait_recv()
        for cp in sends:
            cp.wait_send()
        mine.wait()

    return _pcall(
        body,
        name=name,
        out_shape=jax.ShapeDtypeStruct((N_DEV, rows, n), x.dtype),
        in_specs=[pl.BlockSpec(memory_space=pltpu.VMEM)],
        out_specs=pl.BlockSpec(memory_space=pltpu.VMEM),
        scratch_shapes=[pltpu.SemaphoreType.DMA((N_DEV - 1,)), pltpu.SemaphoreType.DMA((N_DEV - 1,)), pltpu.SemaphoreType.DMA],
        compiler_params=pltpu.CompilerParams(vmem_limit_bytes=VMEM_LIMIT),
    )(x)


def _hbm_exchange(name, ins, out_shapes, plan):
    n_in = len(ins)
    probe_local, probe_remote = plan([None] * n_in, [None] * len(out_shapes), probe=True)

    def body(*refs):
        in_refs, out_refs = refs[:n_in], refs[n_in : n_in + len(out_shapes)]
        send_sems, recv_sems, local_sems = refs[n_in + len(out_shapes) :]
        local, remote = plan(in_refs, out_refs, probe=False)
        lcs = [pltpu.make_async_copy(s, d, local_sems.at[k]) for k, (s, d) in enumerate(local)]
        for cp in lcs:
            cp.start()
        rcs = []
        for k, (s, d, peer, _) in enumerate(remote):
            cp = pltpu.make_async_remote_copy(src_ref=s, dst_ref=d, send_sem=send_sems.at[k], recv_sem=recv_sems.at[k],
                                              device_id=peer, device_id_type=MESH_ID)
            cp.start()
            rcs.append(cp)
        for k, (s, _, peer, here) in enumerate(remote):
            pltpu.make_async_remote_copy(src_ref=s, dst_ref=here, send_sem=send_sems.at[k], recv_sem=recv_sems.at[k],
                                         device_id=peer, device_id_type=MESH_ID).wait_recv()
        for cp in rcs:
            cp.wait_send()
        for cp in lcs:
            cp.wait()

    return _pcall(
        body,
        name=name,
        out_shape=list(out_shapes),
        in_specs=[pl.BlockSpec(memory_space=pl.ANY)] * n_in,
        out_specs=[pl.BlockSpec(memory_space=pl.ANY)] * len(out_shapes),
        scratch_shapes=[pltpu.SemaphoreType.DMA((max(probe_remote, 1),)), pltpu.SemaphoreType.DMA((max(probe_remote, 1),)),
                        pltpu.SemaphoreType.DMA((max(probe_local, 1),))],
    )(*ins)


def _at_axis(ref, axis, start, size):
    return ref.at[tuple(pl.ds(start, size) if a == axis else slice(None) for a in range(len(ref.shape)))]


HBM_SPEC = pl.BlockSpec(memory_space=pltpu.HBM)
SEM_SPEC = pl.BlockSpec(memory_space=pltpu.SEMAPHORE)
SIDE_EFFECT = pltpu.SideEffectType.DATAFLOW_SIDE_EFFECTING


def _in_hbm(a):
    return pltpu.with_memory_space_constraint(a, pltpu.HBM)


def _copies_start(name, bufs, counts, plan, after=None):
    n, ng = len(bufs), len(counts)
    extra = [] if after is None else [after]
    n_in = n + len(extra)

    def body(*refs):
        sems = refs[n_in : n_in + 2 * ng]
        token = refs[n_in + 2 * ng + n]
        for gi, copies in enumerate(plan(refs[:n])):
            for k, (s, d, peer) in enumerate(copies):
                pltpu.make_async_remote_copy(src_ref=s, dst_ref=d, send_sem=sems[2 * gi].at[k], recv_sem=sems[2 * gi + 1].at[k],
                                             device_id=peer, device_id_type=MESH_ID).start()
        token[...] = jnp.zeros(token.shape, token.dtype)

    out = _pcall(
        body,
        name=name,
        out_shape=tuple(pltpu.SemaphoreType.DMA((c,)) for c in counts for _ in range(2))
        + tuple(pltpu.HBM(b.shape, b.dtype) for b in bufs) + (jax.ShapeDtypeStruct((8, 128), F32),),
        in_specs=(HBM_SPEC,) * n + (pl.BlockSpec(memory_space=pl.ANY),) * len(extra),
        out_specs=(SEM_SPEC,) * (2 * ng) + (HBM_SPEC,) * n + (pl.BlockSpec(memory_space=pltpu.VMEM),),
        input_output_aliases={i: 2 * ng + i for i in range(n)},
        compiler_params=pltpu.CompilerParams(has_side_effects=SIDE_EFFECT),
    )(*[_in_hbm(b) for b in bufs], *extra)
    return [(out[2 * g], out[2 * g + 1]) for g in range(ng)], list(out[2 * ng : 2 * ng + n]), out[2 * ng + n]


def _copies_wait(name, sems, bufs, plan, after):
    n, ng = len(bufs), len(sems)

    def body(*refs):
        for gi, copies in enumerate(plan(refs[:n])):
            for k, (s, d, peer) in enumerate(copies):
                cp = pltpu.make_async_remote_copy(src_ref=s, dst_ref=d, send_sem=refs[n + 2 * gi].at[k], recv_sem=refs[n + 2 * gi + 1].at[k],
                                                  device_id=peer, device_id_type=MESH_ID)
                cp.wait_send()
                cp.wait_recv()

    out = _pcall(
        body,
        name=name,
        out_shape=tuple(pltpu.HBM(b.shape, b.dtype) for b in bufs),
        in_specs=(HBM_SPEC,) * n + (SEM_SPEC,) * (2 * ng) + (pl.BlockSpec(memory_space=pl.ANY),),
        out_specs=(HBM_SPEC,) * n,
        input_output_aliases={i: i for i in range(n)},
        compiler_params=pltpu.CompilerParams(has_side_effects=SIDE_EFFECT),
    )(*bufs, *[s for pair in sems for s in pair], after)
    return list(out)


def _layer_matrices(depth):
    out = []
    for i, (kind, j, _, _) in enumerate(_layer_plan(depth)):
        mix = ([("pool_w", j, 1)], [("attn_w_qkv", j, 1), ("attn_w_o", j, 0)], [("ret_w_in", j, 1), ("ret_w_out", j, 0)])[kind]
        out.append(mix + [("ffn_w_up", i, 1), ("ffn_w_down", i, 0)])
    return out


def _peers(mx, my, mc):
    out = []
    for fx, fy in CHIP_FLIPS:
        px, py = _flip(mx, fx), _flip(my, fy)
        out.append(((px, py, mc), 2 * px + py))
    return out


def place_own(shards, names, layers, chip):
    out = []
    for name, idx, axis in [e for layer in layers for e in layer]:
        slab = shards[names.index(name)][idx]
        shp = list(slab.shape)
        shp[axis] *= N_CHIP
        out.append(lax.dynamic_update_slice_in_dim(lax.empty(tuple(shp), BF16), slab, chip * slab.shape[axis], axis))
    return out


def _gather_plan(names, layers, group, n_shards, here):
    def plan(refs):
        mx, my, mc = _pos()
        s_refs, f_refs = refs[:n_shards], refs[n_shards:]
        groups, k = [], 0
        for gi, layer in enumerate(layers):
            if group is not None and gi != group:
                continue
            copies = []
            for name, idx, axis in layer:
                src = s_refs[names.index(name)].at[idx]
                n = src.shape[axis]
                for peer, pchip in _peers(mx, my, mc):
                    at = pchip if here else 2 * mx + my
                    copies.append((src, _at_axis(f_refs[k], axis, at * n, n), peer))
                k += 1
            groups.append(copies)
        return groups

    return plan


def gather_start(shards, names, layers, fulls, after):
    counts = [len(layer) * len(CHIP_FLIPS) for layer in layers]
    sems, bufs, _ = _copies_start("gather_start", list(shards) + list(fulls), counts,
                                  _gather_plan(names, layers, None, len(shards), False), after)
    return sems, bufs[: len(shards)], bufs[len(shards) :]


def gather_wait(g, sems_g, shards, names, layers, fulls_g, after):
    bufs = _copies_wait(f"gather_wait_{g}", [sems_g], list(shards) + list(fulls_g), _gather_plan(names, layers, g, len(shards), True), after)
    return bufs[: len(shards)], bufs[len(shards) :]


def _scatter_plan(layer_entries, n_grads, land_of):
    def plan(refs):
        mx, my, mc = _pos()
        groups, k = [], 0
        for layer in layer_entries:
            copies = []
            for name, idx, axis in layer:
                gref, land = refs[k], refs[n_grads + land_of(k, name)]
                n = gref.shape[axis] // N_CHIP
                for slot, (peer, pchip) in enumerate(_peers(mx, my, mc)):
                    copies.append((_at_axis(gref, axis, pchip * n, n), land.at[slot, idx], peer))
                k += 1
            groups.append(copies)
        return groups

    return plan


def scatter_start(i, layer, grads, lands):
    sems, bufs, token = _copies_start(f"scatter_start_{i}", list(grads) + list(lands), [len(layer) * len(CHIP_FLIPS)],
                                      _scatter_plan([layer], len(grads), lambda k, name: k))
    return sems[0], bufs[: len(grads)], bufs[len(grads) :], token


def scatter_wait(sems, layers, grads, names, lands, after):
    bufs = _copies_wait("scatter_wait", sems, list(grads) + list(lands),
                        _scatter_plan(layers, len(grads), lambda k, name: names.index(name)), after)
    return bufs[: len(grads)], bufs[len(grads) :]


def sibling_swap(parts):
    def plan(in_refs, out_refs, probe):
        if probe:
            return 0, len(parts)
        mx, my, mc = _pos()
        return [], [(s, o, (mx, my, 1 - mc), o) for s, o in zip(in_refs, out_refs)]

    return _hbm_exchange("sibling_swap", parts, [jax.ShapeDtypeStruct(p.shape, p.dtype) for p in parts], plan)


EW_ROWS = 256


def _ew_call(name, fn, ins, n_out):
    rows, cols = ins[0].shape[-2:]
    tr = EW_ROWS if rows % EW_ROWS == 0 else rows

    def body(*refs):
        outs = fn(*[r[...] for r in refs[: len(ins)]])
        for o_ref, o in zip(refs[len(ins) :], outs):
            o_ref[...] = o

    def spec(a):
        if a.ndim == 3:
            return pl.BlockSpec((a.shape[0], tr, cols), lambda i: (0, i, 0))
        return pl.BlockSpec((tr, cols), lambda i: (i, 0))

    return _pcall(
        body,
        name=name,
        grid=(rows // tr,),
        in_specs=[spec(a) for a in ins],
        out_specs=[pl.BlockSpec((tr, cols), lambda i: (i, 0))] * n_out,
        out_shape=[jax.ShapeDtypeStruct((rows, cols), F32)] * n_out,
        compiler_params=pltpu.CompilerParams(dimension_semantics=("parallel",), vmem_limit_bytes=VMEM_LIMIT),
    )(*ins)


def _adamw(w, g, m, v):
    m = ADAM_B1 * m + (1.0 - ADAM_B1) * g
    v = ADAM_B2 * v + (1.0 - ADAM_B2) * (g * g)
    m_hat = m / (1.0 - ADAM_B1 ** ADAM_STEP)
    v_hat = v / (1.0 - ADAM_B2 ** ADAM_STEP)
    return -ADAM_LR * (m_hat / (jnp.sqrt(v_hat) + ADAM_EPS) + ADAM_WD * w), m, v


def sum_slots(name, own, landing):
    def fn(o, l):
        acc = o.astype(F32)
        for k in range(l.shape[0]):
            acc = acc + l[k].astype(F32)
        return (acc,)

    return _ew_call(name, fn, [own, landing], 1)[0]


def adamw_pair(name, w, m, v, p, ps):
    def fn(w, m, v, p, ps):
        g = p + ps
        return (g,) + _adamw(w, g, m, v)

    return _ew_call(name, fn, [w, m, v, p, ps], 4)


def adamw_one(name, w, m, v, g):
    return _ew_call(name, lambda w, m, v, g: _adamw(w, g, m, v), [w, m, v, g], 3)


def reduce_devices(name, x):
    def fn(a):
        acc = a[0]
        for k in range(1, a.shape[0]):
            acc = acc + a[k]
        return (acc,)

    return _ew_call(name, fn, [x], 1)[0]


ADA_ROWS = 16
ADA_CTX = N_DEV


def ada_fwd(s9, ada_w, ada_b):
    depth, _, n = ada_w.shape

    def body(s_ref, w_ref, b_ref, o_ref):
        s = s_ref[...]
        o_ref[...] = _dotf((s * _sigmoid(s)).astype(BF16), w_ref[...].astype(BF16)) + b_ref[...]

    return _pcall(
        body,
        name="ada_fwd",
        grid=(depth,),
        in_specs=[pl.BlockSpec((ADA_ROWS, D), lambda i: (0, 0)), pl.BlockSpec((None, D, n), lambda i: (i, 0, 0)),
                  pl.BlockSpec((None, 1, n), lambda i: (i, 0, 0))],
        out_specs=pl.BlockSpec((None, ADA_ROWS, n), lambda i: (i, 0, 0)),
        out_shape=jax.ShapeDtypeStruct((depth, ADA_ROWS, n), F32),
        compiler_params=pltpu.CompilerParams(dimension_semantics=("arbitrary",), vmem_limit_bytes=VMEM_LIMIT),
    )(s9, ada_w, ada_b)


def ada_bwd(s9, dm, ada_w):
    depth, _, n = ada_w.shape

    def body(s_ref, dm_ref, w_ref, gw_ref, ds_ref):
        _acc_init(pl.program_id(0), ds_ref)
        s = s_ref[...]
        dmb = dm_ref[...].astype(BF16)
        gw_ref[...] = _dot_tn((s * _sigmoid(s)).astype(BF16), dmb)
        ds_ref[...] += _dot_nt(dmb, w_ref[...].astype(BF16))

    return _pcall(
        body,
        name="ada_bwd",
        grid=(depth,),
        in_specs=[pl.BlockSpec((ADA_ROWS, D), lambda i: (0, 0)), pl.BlockSpec((None, ADA_ROWS, n), lambda i: (i, 0, 0)),
                  pl.BlockSpec((None, D, n), lambda i: (i, 0, 0))],
        out_specs=[pl.BlockSpec((None, D, n), lambda i: (i, 0, 0)), pl.BlockSpec((ADA_ROWS, D), lambda i: (0, 0))],
        out_shape=[jax.ShapeDtypeStruct((depth, D, n), F32), jax.ShapeDtypeStruct((ADA_ROWS, D), F32)],
        compiler_params=pltpu.CompilerParams(dimension_semantics=("arbitrary",), vmem_limit_bytes=VMEM_LIMIT),
    )(s9, dm, ada_w)


def cctx_grad(parts, c_ctx):
    def body(p_ref, c_ref, o_ref):
        acc = p_ref[0, ADA_CTX : ADA_CTX + 1, :]
        for chip in range(1, N_CHIP):
            acc = acc + p_ref[2 * chip, ADA_CTX : ADA_CTX + 1, :]
        c = c_ref[...]
        sg = _sigmoid(c)
        o_ref[...] = acc * (sg * (1.0 + c * (1.0 - sg)))

    return _pcall(body, name="cctx_grad", out_shape=jax.ShapeDtypeStruct((1, D), F32))(parts, c_ctx)


def _pack(arrs):
    flat = jnp.concatenate([a.astype(F32).reshape(-1) for a in arrs])
    rows = -(-flat.shape[0] // (8 * D)) * 8
    return jnp.pad(flat, (0, rows * D - flat.shape[0])).reshape(rows, D)


def _unpack(slab, shapes):
    lead = slab.shape[:-2]
    flat = slab.reshape(lead + (-1,))
    out, off = [], 0
    for shp in shapes:
        n = 1
        for d in shp:
            n *= d
        out.append(flat[..., off : off + n].reshape(lead + tuple(shp)))
        off += n
    return out


def _unshard(per_dev, axis):
    return jnp.concatenate([per_dev[2 * chip] for chip in range(N_CHIP)], axis=axis)


BIG = (("pool_w", 1), ("attn_w_qkv", 1), ("attn_w_o", 0), ("ret_w_in", 1), ("ret_w_out", 0), ("ffn_w_up", 1), ("ffn_w_down", 0))
WEIGHTS = ("c_ctx", "ada_w", "ada_b", "norm_w", "pool_w", "pool_b", "pool_scale", "attn_w_qkv", "attn_q_gain", "attn_k_gain",
           "attn_w_o", "ret_w_in", "ret_decay_logit", "ret_gn_w", "ret_w_out", "ffn_w_up", "ffn_conv_w", "ffn_conv_b", "ffn_w_down")
SMALL = tuple(n for n in WEIGHTS if n != "ada_w" and n not in dict(BIG))
SMALL_SHARD_AXIS = {"norm_w": 2, "pool_b": 1, "pool_scale": 1, "ret_gn_w": 1, "ffn_conv_w": 2}


def kernel(x, c, ctx, c_ctx, ada_w, ada_b, norm_w, pool_w, pool_b, pool_scale, attn_w_qkv, attn_q_gain, attn_k_gain, attn_w_o, ret_w_in, ret_decay_logit, ret_gn_w, ret_w_out, ffn_w_up, ffn_conv_w, ffn_conv_b, ffn_w_down, loss_target, m_c_ctx, m_ada_w, m_ada_b, m_norm_w, m_pool_w, m_pool_b, m_pool_scale, m_attn_w_qkv, m_attn_q_gain, m_attn_k_gain, m_attn_w_o, m_ret_w_in, m_ret_decay_logit, m_ret_gn_w, m_ret_w_out, m_ffn_w_up, m_ffn_conv_w, m_ffn_conv_b, m_ffn_w_down, v_c_ctx, v_ada_w, v_ada_b, v_norm_w, v_pool_w, v_pool_b, v_pool_scale, v_attn_w_qkv, v_attn_q_gain, v_attn_k_gain, v_attn_w_o, v_ret_w_in, v_ret_decay_logit, v_ret_gn_w, v_ret_w_out, v_ffn_w_up, v_ffn_conv_w, v_ffn_conv_b, v_ffn_w_down):
    P = dict(zip(WEIGHTS, (c_ctx, ada_w, ada_b, norm_w, pool_w, pool_b, pool_scale, attn_w_qkv, attn_q_gain, attn_k_gain, attn_w_o,
                           ret_w_in, ret_decay_logit, ret_gn_w, ret_w_out, ffn_w_up, ffn_conv_w, ffn_conv_b, ffn_w_down)))
    M = dict(zip(WEIGHTS, (m_c_ctx, m_ada_w, m_ada_b, m_norm_w, m_pool_w, m_pool_b, m_pool_scale, m_attn_w_qkv, m_attn_q_gain,
                           m_attn_k_gain, m_attn_w_o, m_ret_w_in, m_ret_decay_logit, m_ret_gn_w, m_ret_w_out, m_ffn_w_up,
                           m_ffn_conv_w, m_ffn_conv_b, m_ffn_w_down)))
    V = dict(zip(WEIGHTS, (v_c_ctx, v_ada_w, v_ada_b, v_norm_w, v_pool_w, v_pool_b, v_pool_scale, v_attn_w_qkv, v_attn_q_gain,
                           v_attn_k_gain, v_attn_w_o, v_ret_w_in, v_ret_decay_logit, v_ret_gn_w, v_ret_w_out, v_ffn_w_up,
                           v_ffn_conv_w, v_ffn_conv_b, v_ffn_w_down)))
    depth, s, l = ada_w.shape[0], x.shape[1], ctx.shape[1]
    assert l == BM and s % BM == 0 and s % GRID_W == 0
    nlb = s // BM
    n_pool = pool_w.shape[0]
    mx, my, mc = _pos()
    chip, dev = 2 * mx + my, 4 * mx + 2 * my + mc
    nada = ada_w.shape[2]

    sharded = [n for n in SMALL if n in SMALL_SHARD_AXIS]
    got = small_all_gather("gather_small", _pack([c[0]] + [P[n] for n in sharded]))
    got = _unpack(got, [(D,)] + [P[n].shape for n in sharded])
    c_all = got[0]
    full = {n: _unshard(g_, SMALL_SHARD_AXIS[n]) for n, g_ in zip(sharded, got[1:])}

    s9 = jnp.concatenate([c_all, c_ctx[None, :], jnp.zeros((ADA_ROWS - N_DEV - 1, D), F32)], axis=0)
    ada_b_mine = lax.dynamic_slice_in_dim(ada_b, chip * nada, nada, axis=1)[:, None, :]
    mod_part = ada_fwd(s9, ada_w, ada_b_mine)
    mod_all = _unshard(small_all_gather("gather_mod", mod_part.reshape(depth * ADA_ROWS, nada)), 1).reshape(depth, ADA_ROWS, 6, D)
    mod_mine = lax.dynamic_index_in_dim(mod_all, dev, axis=1, keepdims=False)
    mods_all = jnp.pad(jnp.stack([mod_mine, mod_all[:, ADA_CTX]], axis=1), ((0, 0), (0, 0), (0, 2), (0, 0)))
    mods = [mods_all[i] for i in range(depth)]

    names = [n for n, _ in BIG]
    layers = _layer_matrices(depth)
    shards = [P[n].astype(BF16) for n in names]
    gsems, shards, fulls = gather_start(shards, names, layers, place_own(shards, names, layers, chip), mods_all)
    first = [sum(len(layer) for layer in layers[:i]) for i in range(depth + 1)]
    flight = {"shards": shards}

    def fetch(i, after):
        flight["shards"], mats = gather_wait(i, gsems[i], flight["shards"], names, layers, fulls[first[i] : first[i + 1]], after)
        return {name: m for (name, _, _), m in zip(layers[i], mats)}

    lands = {n: lax.empty((len(CHIP_FLIPS),) + P[n].shape, BF16) for n in names}
    sent = {}

    def emit(i, big):
        lnames = [name for name, _, _ in layers[i]]
        sems, gl, ll, token = scatter_start(i, layers[i], [big[n] for n in lnames], [lands[n] for n in lnames])
        lands.update(zip(lnames, ll))
        sent[i] = (sems, gl)
        return token

    w = {
        "nw": full["norm_w"],
        "pbs": jnp.concatenate([full["pool_b"][:, None], full["pool_scale"][:, None], jnp.zeros((n_pool, 6, D), F32)], axis=1),
        "gains": jnp.concatenate([attn_q_gain, attn_k_gain, jnp.zeros((6, HD), F32)], axis=0),
        "gnw": full["ret_gn_w"],
        "logit_b": jnp.broadcast_to(ret_decay_logit[0][:, :, None, None], (2, RET_HEADS, 8, 128)),
        "cw": jnp.concatenate([full["ffn_conv_w"], ffn_conv_b[:, None, :], jnp.zeros((depth, 4, 2 * D_FF), F32)], axis=1),
    }
    w["cos"], w["sin"] = rope_tables(s, l)

    xs = jnp.concatenate([x[0], ctx[0]], axis=0)
    loss_tile, dxs, g = local_step(xs, loss_target[0], mods, w, nlb, depth, fetch, emit)
    loss = lax.psum(loss_tile[0, 0], MESH_AXES)
    grad_x = dxs[:s][None]

    small_shapes = [(depth, 2, 8, D), (depth, 2, D), (n_pool, 2, D), (2, HD), (2, RET_HEADS), (RNV,), (depth, 4, 2 * D_FF)]
    slab = _pack([
        jnp.stack(g["dmod"]),
        jnp.stack([jnp.stack([g["dnw1"][i][0], g["dnw2"][i][0]]) for i in range(depth)]),
        jnp.stack([g["dpbs"][j][0:2] for j in range(n_pool)]),
        g["dgains"][0:2], g["dlogit"][:, :, 0, 0], g["dgnw"][0], jnp.stack([g["dcw"][i][0:4] for i in range(depth)]),
    ])
    slabs = small_all_gather("gather_small_grads", slab)
    dmod_dev = _unpack(slabs, small_shapes[:1])[0]
    t_dmod, t_nw, t_pbs, t_gains, t_logit, t_gnw, t_cw = _unpack(reduce_devices("reduce_small_grads", slabs), small_shapes)

    def cols(a):
        return lax.dynamic_slice_in_dim(a, chip * nada, nada, axis=a.ndim - 1)

    dm_lat = jnp.swapaxes(cols(dmod_dev[:, :, 0, :6].reshape(N_DEV, depth, 6 * D)), 0, 1)
    dm_ctx = cols(t_dmod[:, 1, :6].reshape(depth, 1, 6 * D))
    dm = jnp.concatenate([dm_lat, dm_ctx, jnp.zeros((depth, ADA_ROWS - N_DEV - 1, nada), F32)], axis=1)
    g_ada_w, ds9 = ada_bwd(s9, dm, ada_w)
    g_c_ctx = cctx_grad(small_all_gather("gather_dcctx", ds9), c_ctx[None, :])[0]

    def mine(a, name):
        n = P[name].shape[SMALL_SHARD_AXIS[name]]
        return lax.dynamic_slice_in_dim(a, chip * n, n, axis=SMALL_SHARD_AXIS[name])

    G = {
        "c_ctx": g_c_ctx,
        "ada_b": (t_dmod[:, 0, :6] + t_dmod[:, 1, :6]).reshape(depth, 6 * D),
        "norm_w": mine(t_nw, "norm_w"),
        "pool_b": mine(t_pbs[:, 0], "pool_b"), "pool_scale": mine(t_pbs[:, 1], "pool_scale"),
        "attn_q_gain": t_gains[0:1], "attn_k_gain": t_gains[1:2],
        "ret_decay_logit": t_logit[None], "ret_gn_w": mine(t_gnw[None], "ret_gn_w"),
        "ffn_conv_w": mine(t_cw[:, 0:3], "ffn_conv_w"), "ffn_conv_b": t_cw[:, 3],
    }
    sw, sg, sm, sv = (_pack([d_[n] for n in SMALL]) for d_ in (P, G, M, V))
    outs = adamw_one("adamw_small", sw, sm, sv, sg)
    D_, NM, NV = ({n: a for n, a in zip(SMALL, _unpack(o, [P[n].shape for n in SMALL]))} for o in outs)

    flat2 = lambda a: a.reshape(-1, a.shape[-1])
    G["ada_w"] = g_ada_w
    o3 = adamw_one("adamw_ada", flat2(ada_w), flat2(M["ada_w"]), flat2(V["ada_w"]), flat2(g_ada_w))
    D_["ada_w"], NM["ada_w"], NV["ada_w"] = (o.reshape(ada_w.shape) for o in o3)

    sent_grads, landed = scatter_wait([sent[i][0] for i in range(depth)], layers, [a for i in range(depth) for a in sent[i][1]],
                                      names, [lands[n] for n in names], dxs)
    own = {n: [None] * P[n].shape[0] for n in names}
    for (name, idx, axis), a in zip([e for layer in layers for e in layer], sent_grads):
        n_ = a.shape[axis] // N_CHIP
        own[name][idx] = lax.dynamic_slice_in_dim(a, chip * n_, n_, axis=axis)
    partial = [sum_slots("sum_" + n, jnp.stack(own[n]).reshape(-1, lnd.shape[-1]), lnd.reshape(len(CHIP_FLIPS), -1, lnd.shape[-1]))
               for n, lnd in zip(names, landed)]
    theirs = sibling_swap(partial)
    for (n, _), p, ps in zip(BIG, partial, theirs):
        o4 = adamw_pair("adamw_" + n, flat2(P[n]), flat2(M[n]), flat2(V[n]), p, ps)
        G[n], D_[n], NM[n], NV[n] = (o.reshape(P[n].shape) for o in o4)

    return (loss, grad_x, *[G[n] for n in WEIGHTS], *[D_[n] for n in WEIGHTS], *[NM[n] for n in WEIGHTS], *[NV[n] for n in WEIGHTS])
```

```python
import functools

import jax
import jax.numpy as jnp
from jax import lax
from jax.experimental import pallas as pl
from jax.experimental.pallas import tpu as pltpu

F32 = jnp.float32
BF16 = jnp.bfloat16

D = 1024
BM = 256
EPS = 1e-6
POOL_WINDOWS = (2, 4, 8, 16)
POOL_GROUP = D // 4
ATTN_HEADS = 8
ATTN_KV_HEADS = 2
HD = D // ATTN_HEADS
ATTN_GROUP = ATTN_HEADS // ATTN_KV_HEADS
NQ = ATTN_HEADS * HD
NKV = ATTN_KV_HEADS * HD
GRID_W = 64
ROPE_THETA = 10000.0
RET_HEADS = 4
RET_DK = D // RET_HEADS
RET_DV = 2 * D // RET_HEADS
RNQ = RET_HEADS * RET_DK
RNV = RET_HEADS * RET_DV
D_FF = 2816
FF_CHUNK = 256
ADAM_LR, ADAM_B1, ADAM_B2, ADAM_EPS, ADAM_WD, ADAM_STEP = 0.001, 0.9, 0.999, 1e-08, 0.01, 10
HALO_F32 = 8
HALO_BF16 = 16
VMEM_LIMIT = 60 * 1024 * 1024

MESH_AXES = ("x", "y", "c")
N_DEV = 8
N_CHIP = 4


def _pcall(body, **kw):
    return pl.pallas_call(body, **kw)


def _spec(shape, kind, nlb, nblk):
    nd = len(shape)
    if kind == "row":
        return pl.BlockSpec((BM, shape[1]), lambda i: (i, 0))
    if kind == "full":
        return pl.BlockSpec(tuple(shape), lambda i: (0,) * nd, pipeline_mode=pl.Buffered(1))
    if isinstance(kind, tuple) and kind[0] == "fullat":
        return pl.BlockSpec((None,) + tuple(shape[1:]), lambda i: (kind[1],) + (0,) * (nd - 1), pipeline_mode=pl.Buffered(1))
    if kind == "acc":
        return pl.BlockSpec(tuple(shape), lambda i: (0,) * nd)
    if kind == "any":
        return pl.BlockSpec(memory_space=pl.ANY)
    if kind == "stream":
        return pl.BlockSpec((1,) + tuple(shape[1:]), lambda i: (i // nlb,) + (0,) * (nd - 1))
    if kind in ("prev8", "prev16"):
        hb = int(kind[4:])
        return pl.BlockSpec((hb, shape[1]), lambda i: (jnp.maximum(i * (BM // hb) - 1, 0), 0))
    if kind in ("next8", "next16"):
        hb = int(kind[4:])
        last = nblk * BM // hb - 1
        return pl.BlockSpec((hb, shape[1]), lambda i: (jnp.minimum((i + 1) * (BM // hb), last), 0))
    raise ValueError(kind)


def _rowcall(name, body, nblk, nlb, ins, outs, scratch=()):
    return _pcall(
        body,
        name=name,
        grid=(nblk,),
        in_specs=[_spec(a.shape, k, nlb, nblk) for a, k in ins],
        out_specs=[_spec(s, k, nlb, nblk) for s, _, k in outs],
        out_shape=[jax.ShapeDtypeStruct(s, d) for s, d, _ in outs],
        scratch_shapes=list(scratch),
        compiler_params=pltpu.CompilerParams(dimension_semantics=("arbitrary",), vmem_limit_bytes=VMEM_LIMIT),
    )(*[a for a, _ in ins])


def _wk(w):
    return (w[0], ("fullat", w[1])) if isinstance(w, tuple) else (w, "full")


def _stream_edges(i, nlb):
    is_ctx = i == nlb
    return (i == 0) | is_ctx, (i == nlb - 1) | is_ctx, is_ctx


def _dotf(a, b):
    return jnp.dot(a, b, preferred_element_type=F32)


def _dot_nt(a, b):
    return lax.dot_general(a, b, (((1,), (1,)), ((), ())), preferred_element_type=F32)


def _dot_tn(a, b):
    return lax.dot_general(a, b, (((0,), (0,)), ((), ())), preferred_element_type=F32)


def _sigmoid(x):
    return 0.5 * jnp.tanh(0.5 * x) + 0.5


def _norm_mod(x, nw, sh, sc):
    r = lax.rsqrt(jnp.mean(x * x, axis=-1, keepdims=True) + EPS)
    xhat = x * r
    n = xhat * nw
    return n * (1.0 + sc) + sh, n, xhat, r


def _norm_mod_bwd(dh, n, xhat, r, nw, sc):
    dsh = jnp.sum(dh, axis=0, keepdims=True)
    dsc = jnp.sum(dh * n, axis=0, keepdims=True)
    dn = dh * (1.0 + sc)
    dnw = jnp.sum(dn * xhat, axis=0, keepdims=True)
    dxhat = dn * nw
    dx = r * (dxhat - xhat * jnp.mean(dxhat * xhat, axis=-1, keepdims=True))
    return dx, dsh, dsc, dnw


def _acc_init(i, *refs):
    @pl.when(i == 0)
    def _():
        for r in refs:
            r[...] = jnp.zeros(r.shape, r.dtype)


SH1, SC1, G1, SH2, SC2, G2 = range(6)


def _mrow(mod_ref, k):
    return mod_ref[0, k : k + 1, :]


def _shift_rows(x_ref, xp_ref, xn_ref, cs, first, last):
    cur = x_ref[:, cs].astype(F32)
    rows = lax.broadcasted_iota(jnp.int32, cur.shape, 0)
    pr = jnp.where(first, 0.0, xp_ref[HALO_BF16 - 1 : HALO_BF16, cs].astype(F32))
    nx = jnp.where(last, 0.0, xn_ref[0:1, cs].astype(F32))
    dn = jnp.where(rows == 0, pr, pltpu.roll(cur, 1, 0))
    up = jnp.where(rows == BM - 1, nx, pltpu.roll(cur, BM - 1, 0))
    return dn, cur, up


def ffn_up(x1, mod, nw2, w_up, nblk, nlb):
    r = x1.shape[0]

    def body(x_ref, mod_ref, nw_ref, w_ref, u_ref, h_ref):
        h, _, _, _ = _norm_mod(x_ref[...], nw_ref[...], _mrow(mod_ref, SH2), _mrow(mod_ref, SC2))
        hb = h.astype(BF16)
        h_ref[...] = hb
        u_ref[...] = _dotf(hb, w_ref[...]).astype(BF16)

    return _rowcall(
        "ffn_up", body, nblk, nlb,
        [(x1, "row"), (mod, "stream"), (nw2, "full"), _wk(w_up)],
        [((r, 2 * D_FF), BF16, "row"), ((r, D), BF16, "row")],
    )


def _conv_gate_chunk(u_ref, up_ref, un_ref, cw_ref, j, first, last):
    res = []
    for half in range(2):
        c0 = half * D_FF + j * FF_CHUNK
        cs = slice(c0, c0 + FF_CHUNK)
        dn, cur, up = _shift_rows(u_ref, up_ref, un_ref, cs, first, last)
        val = dn * cw_ref[0:1, cs] + cur * cw_ref[1:2, cs] + up * cw_ref[2:3, cs] + cw_ref[3:4, cs]
        res.append((val, dn, cur, up, cs))
    return res


def ffn_down(u, cw, w_down, x1, mod, nblk, nlb):
    r = x1.shape[0]

    def body(u_ref, up_ref, un_ref, cw_ref, w_ref, x_ref, mod_ref, x2_ref, f_ref, uc_ref, gated_ref):
        first, last, _ = _stream_edges(pl.program_id(0), nlb)
        for j in range(D_FF // FF_CHUNK):
            (a, _, _, _, acs), (v, _, _, _, vcs) = _conv_gate_chunk(u_ref, up_ref, un_ref, cw_ref, j, first, last)
            uc_ref[:, acs] = a.astype(BF16)
            uc_ref[:, vcs] = v.astype(BF16)
            gated_ref[:, j * FF_CHUNK : (j + 1) * FF_CHUNK] = (a * _sigmoid(a) * v).astype(BF16)
        f = _dotf(gated_ref[...], w_ref[...])
        f_ref[...] = f.astype(BF16)
        x2_ref[...] = x_ref[...] + _mrow(mod_ref, G2) * f

    return _rowcall(
        "ffn_down", body, nblk, nlb,
        [(u, "row"), (u, "prev16"), (u, "next16"), _wk(cw), _wk(w_down), (x1, "row"), (mod, "stream")],
        [((r, D), F32, "row"), ((r, D), BF16, "row"), ((r, 2 * D_FF), BF16, "row"), ((r, D_FF), BF16, "row")],
    )


def ffn_bwd1(dx2, f, uc, w_down, mod, dep, nblk, nlb):
    r = dx2.shape[0]

    def body(dx_ref, f_ref, uc_ref, w_ref, mod_ref, dep_ref, df_ref, duc_ref, dcb_ref, dmod_ref, dgated):
        i = pl.program_id(0)
        _acc_init(i, dcb_ref, dmod_ref)
        dx = dx_ref[...]
        df = (_mrow(mod_ref, G2) * dx).astype(BF16)
        df_ref[...] = df
        dmod_ref[i // nlb, G2 : G2 + 1, :] += jnp.sum(dx * f_ref[...].astype(F32), axis=0, keepdims=True)
        dgated[...] = _dot_nt(df, w_ref[...])
        for j in range(D_FF // FF_CHUNK):
            acs = slice(j * FF_CHUNK, (j + 1) * FF_CHUNK)
            vcs = slice(D_FF + j * FF_CHUNK, D_FF + (j + 1) * FF_CHUNK)
            a, v = uc_ref[:, acs].astype(F32), uc_ref[:, vcs].astype(F32)
            sa = _sigmoid(a)
            dg = dgated[:, acs]
            for dval, cs in ((dg * v * (sa * (1.0 + a * (1.0 - sa))), acs), (dg * (a * sa), vcs)):
                duc_ref[:, cs] = dval.astype(BF16)
                dcb_ref[3:4, cs] += jnp.sum(dval, axis=0, keepdims=True)

    return _rowcall(
        "ffn_bwd1", body, nblk, nlb,
        [(dx2, "row"), (f, "row"), (uc, "row"), _wk(w_down), (mod, "stream"), (dep, "any")],
        [((r, D), BF16, "row"), ((r, 2 * D_FF), BF16, "row"), ((8, 2 * D_FF), F32, "acc"), ((2, 8, D), F32, "acc")],
        scratch=[pltpu.VMEM((BM, D_FF), F32)],
    )


def ffn_bwd3(duc, u, cw, w_up, x1, dx2, mod, nw2, nblk, nlb):
    r = dx2.shape[0]

    def body(d_ref, dp_ref, dn_ref, u_ref, cw_ref, w_ref, x_ref, dx_ref, mod_ref, nw_ref, du_ref, dx1_ref, dcw_ref, dnw_ref, dmod_ref):
        i = pl.program_id(0)
        first, last, _ = _stream_edges(i, nlb)
        _acc_init(i, dcw_ref, dnw_ref, dmod_ref)
        for j in range(2 * D_FF // FF_CHUNK):
            cs = slice(j * FF_CHUNK, (j + 1) * FF_CHUNK)
            dn, cur, up = _shift_rows(d_ref, dp_ref, dn_ref, cs, first, last)
            du_ref[:, cs] = (up * cw_ref[0:1, cs] + cur * cw_ref[1:2, cs] + dn * cw_ref[2:3, cs]).astype(BF16)
            uu = u_ref[:, cs].astype(F32)
            dcw_ref[0:1, cs] += jnp.sum(up * uu, axis=0, keepdims=True)
            dcw_ref[1:2, cs] += jnp.sum(cur * uu, axis=0, keepdims=True)
            dcw_ref[2:3, cs] += jnp.sum(dn * uu, axis=0, keepdims=True)
        dh = _dot_nt(du_ref[...], w_ref[...])
        sc = _mrow(mod_ref, SC2)
        nw = nw_ref[...]
        _, n, xhat, rr = _norm_mod(x_ref[...], nw, _mrow(mod_ref, SH2), sc)
        dxn, dsh, dsc, dnw = _norm_mod_bwd(dh, n, xhat, rr, nw, sc)
        dx1_ref[...] = dx_ref[...] + dxn
        s = i // nlb
        dmod_ref[s, SH2 : SH2 + 1, :] += dsh
        dmod_ref[s, SC2 : SC2 + 1, :] += dsc
        dnw_ref[0:1, :] += dnw

    return _rowcall(
        "ffn_bwd3", body, nblk, nlb,
        [(duc, "row"), (duc, "prev16"), (duc, "next16"), (u, "row"), _wk(cw), _wk(w_up), (x1, "row"), (dx2, "row"),
         (mod, "stream"), (nw2, "full")],
        [((r, 2 * D_FF), BF16, "row"), ((r, D), F32, "row"), ((8, 2 * D_FF), F32, "acc"), ((8, D), F32, "acc"),
         ((2, 8, D), F32, "acc")],
    )


def matmul_tn(a, b, nblk, tn=None):
    k, n = a.shape[1], b.shape[1]
    rows = nblk * BM
    tr = 768 if rows % 768 == 0 else (1024 if rows % 1024 == 0 else BM)
    if tn is None:
        tn = n
        while k * tn * 4 > 6 * 1024 * 1024 and tn % 256 == 0:
            tn //= 2
    steps = rows // tr

    def body(a_ref, b_ref, o_ref, acc):
        t = pl.program_id(1)

        @pl.when(t == 0)
        def _():
            acc[...] = jnp.zeros(acc.shape, acc.dtype)

        acc[...] += _dot_tn(a_ref[...], b_ref[...])

        @pl.when(t == steps - 1)
        def _():
            o_ref[...] = acc[...].astype(o_ref.dtype)

    return _pcall(
        body,
        name="matmul_tn",
        grid=(n // tn, steps),
        in_specs=[pl.BlockSpec((tr, k), lambda j, t: (t, 0)), pl.BlockSpec((tr, tn), lambda j, t: (t, j))],
        out_specs=pl.BlockSpec((k, tn), lambda j, t: (0, j)),
        out_shape=jax.ShapeDtypeStruct((k, n), BF16),
        scratch_shapes=[pltpu.VMEM((k, tn), F32)],
        compiler_params=pltpu.CompilerParams(dimension_semantics=("parallel", "arbitrary"), vmem_limit_bytes=VMEM_LIMIT),
    )(a, b)


EXT = BM + 2 * HALO_F32


def _pool_positions(i, nlb, nrows, row0):
    is_ctx = i == nlb
    t = (i - jnp.where(is_ctx, nlb, 0)) * BM + row0 + lax.broadcasted_iota(jnp.int32, (nrows, 1), 0)
    return t, jnp.where(is_ctx, BM, nlb * BM)


def _pool_cnt(t, win, slen):
    return (jnp.minimum(t + win // 2, slen) - jnp.maximum(t - win // 2, 0)).astype(F32)


def _pool_fill_ext(ext, i, nlb, x_ref, xp_ref, xn_ref, mod_ref, nw_ref):
    first, last, _ = _stream_edges(i, nlb)
    sh, sc, nw = _mrow(mod_ref, SH1), _mrow(mod_ref, SC1), nw_ref[...]
    hcur, n, xhat, r = _norm_mod(x_ref[...], nw, sh, sc)
    ext[0:HALO_F32, :] = jnp.where(first, 0.0, _norm_mod(xp_ref[...], nw, sh, sc)[0])
    ext[HALO_F32 : HALO_F32 + BM, :] = hcur
    ext[HALO_F32 + BM :, :] = jnp.where(last, 0.0, _norm_mod(xn_ref[...], nw, sh, sc)[0])
    return n, xhat, r


def _window_sum(ref, cols, offs):
    acc = None
    for o in offs:
        v = ref[HALO_F32 + o : HALO_F32 + o + BM, cols]
        acc = v if acc is None else acc + v
    return acc


def _pool_diff(ext, g, win, t, slen):
    cols = slice(g * POOL_GROUP, (g + 1) * POOL_GROUP)
    ssum = _window_sum(ext, cols, range(-(win // 2), win // 2))
    return ssum / _pool_cnt(t, win, slen) - ext[HALO_F32 : HALO_F32 + BM, cols]


def pool_fwd(x, mod, nw1, pw, pbs, nblk, nlb):
    r = x.shape[0]

    def body(x_ref, xp_ref, xn_ref, mod_ref, nw_ref, pw_ref, pbs_ref, x1_ref, ypre_ref, ext):
        i = pl.program_id(0)
        _pool_fill_ext(ext, i, nlb, x_ref, xp_ref, xn_ref, mod_ref, nw_ref)
        t, slen = _pool_positions(i, nlb, BM, 0)
        for g, win in enumerate(POOL_WINDOWS):
            cols = slice(g * POOL_GROUP, (g + 1) * POOL_GROUP)
            diff = _pool_diff(ext, g, win, t, slen)
            ypre = _dotf(diff.astype(BF16), pw_ref[g]) + pbs_ref[0:1, cols]
            ypre_ref[:, cols] = ypre
            x1_ref[:, cols] = x_ref[:, cols] + mod_ref[0, G1 : G1 + 1, cols] * (ypre * pbs_ref[1:2, cols])

    return _rowcall(
        "pool_fwd", body, nblk, nlb,
        [(x, "row"), (x, "prev8"), (x, "next8"), (mod, "stream"), (nw1, "full"), _wk(pw), _wk(pbs)],
        [((r, D), F32, "row"), ((r, D), F32, "row")],
        scratch=[pltpu.VMEM((EXT, D), F32)],
    )


def pool_bwd(dx1, x, ypre, mod, nw1, pw, pbs, nblk, nlb):
    r = x.shape[0]

    def body(d_ref, dp_ref, dn_ref, x_ref, xp_ref, xn_ref, ypre_ref, mod_ref, nw_ref, pw_ref, pbs_ref,
             dx_ref, dpw_ref, dpbs_ref, dnw_ref, dmod_ref, ext, dext, eext, dh):
        i = pl.program_id(0)
        first, last, _ = _stream_edges(i, nlb)
        _acc_init(i, dpw_ref, dpbs_ref, dnw_ref, dmod_ref)
        n, xhat, rr = _pool_fill_ext(ext, i, nlb, x_ref, xp_ref, xn_ref, mod_ref, nw_ref)
        g1, scale = _mrow(mod_ref, G1), pbs_ref[1:2, :]
        dcur = d_ref[...]
        ypre = ypre_ref[...]
        s = i // nlb
        dmod_ref[s, G1 : G1 + 1, :] += jnp.sum(dcur * (ypre * scale), axis=0, keepdims=True)
        dy = g1 * dcur
        dpbs_ref[1:2, :] += jnp.sum(dy * ypre, axis=0, keepdims=True)
        dpbs_ref[0:1, :] += jnp.sum(dy * scale, axis=0, keepdims=True)
        gs = g1 * scale
        dext[0:HALO_F32, :] = jnp.where(first, 0.0, gs * dp_ref[...])
        dext[HALO_F32 : HALO_F32 + BM, :] = dy * scale
        dext[HALO_F32 + BM :, :] = jnp.where(last, 0.0, gs * dn_ref[...])
        t, slen = _pool_positions(i, nlb, BM, 0)
        text, _ = _pool_positions(i, nlb, EXT, -HALO_F32)
        for g, win in enumerate(POOL_WINDOWS):
            cols = slice(g * POOL_GROUP, (g + 1) * POOL_GROUP)
            diff = _pool_diff(ext, g, win, t, slen)
            dpre = dext[:, cols].astype(BF16)
            ddiff = _dot_nt(dpre, pw_ref[g])
            eext[...] = ddiff / jnp.maximum(_pool_cnt(text, win, slen), 1.0)
            dh[:, cols] = _window_sum(eext, slice(None), range(-(win // 2) + 1, win // 2 + 1)) - ddiff[HALO_F32 : HALO_F32 + BM, :]
            dpw_ref[g] += _dot_tn(diff.astype(BF16), dpre[HALO_F32 : HALO_F32 + BM, :])
        sc, nw = _mrow(mod_ref, SC1), nw_ref[...]
        dxn, dsh, dsc, dnw = _norm_mod_bwd(dh[...], n, xhat, rr, nw, sc)
        dx_ref[...] = dcur + dxn
        dmod_ref[s, SH1 : SH1 + 1, :] += dsh
        dmod_ref[s, SC1 : SC1 + 1, :] += dsc
        dnw_ref[0:1, :] += dnw

    return _rowcall(
        "pool_bwd", body, nblk, nlb,
        [(dx1, "row"), (dx1, "prev8"), (dx1, "next8"), (x, "row"), (x, "prev8"), (x, "next8"), (ypre, "row"),
         (mod, "stream"), (nw1, "full"), _wk(pw), _wk(pbs)],
        [((r, D), F32, "row"), ((4, POOL_GROUP, POOL_GROUP), F32, "acc"), ((8, D), F32, "acc"), ((8, D), F32, "acc"),
         ((2, 8, D), F32, "acc")],
        scratch=[pltpu.VMEM((EXT, D), F32), pltpu.VMEM((EXT, D), F32), pltpu.VMEM((EXT, POOL_GROUP), F32), pltpu.VMEM((BM, D), F32)],
    )


def rope_tables(s, l):
    rows = s // GRID_W
    row = jnp.broadcast_to(jnp.arange(rows)[:, None], (rows, GRID_W)).reshape(-1).astype(F32)
    col = jnp.broadcast_to(jnp.arange(GRID_W)[None, :], (rows, GRID_W)).reshape(-1).astype(F32)
    axis_dim = HD // 2
    inv = ROPE_THETA ** (-jnp.arange(0, axis_dim, 2, dtype=F32) / axis_dim)
    ar, ac = row[:, None] * inv, col[:, None] * inv
    cos = jnp.concatenate([jnp.cos(ar), jnp.cos(ar), jnp.cos(ac), jnp.cos(ac)], axis=-1)
    sin = jnp.concatenate([-jnp.sin(ar), jnp.sin(ar), -jnp.sin(ac), jnp.sin(ac)], axis=-1)
    return (jnp.concatenate([cos, jnp.ones((l, HD), F32)], axis=0), jnp.concatenate([sin, jnp.zeros((l, HD), F32)], axis=0))


def _partner(x):
    q = HD // 4
    lane = lax.broadcasted_iota(jnp.int32, x.shape, 1)
    return jnp.where((lane // q) % 2 == 0, pltpu.roll(x, HD - q, 1), pltpu.roll(x, q, 1))


def _head_norm(raw, gain):
    r = lax.rsqrt(jnp.mean(raw * raw, axis=-1, keepdims=True) + EPS)
    return raw * r, r


ATTN_SCALE = HD ** -0.5


def qkv_fwd(x, mod, nw1, w_qkv, gains, cos_t, sin_t, nblk, nlb):
    r = x.shape[0]

    def body(x_ref, mod_ref, nw_ref, w_ref, g_ref, c_ref, s_ref, raw_ref, q_ref, k_ref, v_ref, h_ref):
        h = _norm_mod(x_ref[...], nw_ref[...], _mrow(mod_ref, SH1), _mrow(mod_ref, SC1))[0].astype(BF16)
        h_ref[...] = h
        raw_ref[...] = _dotf(h, w_ref[...])
        cos, sin = c_ref[...], s_ref[...]
        for j in range(ATTN_HEADS + ATTN_KV_HEADS):
            isq = j < ATTN_HEADS
            xn = _head_norm(raw_ref[:, j * HD : (j + 1) * HD], None)[0] * (g_ref[0:1, :] if isq else g_ref[1:2, :])
            rot = xn * cos + _partner(xn) * sin
            if isq:
                rot = rot * ATTN_SCALE
            rot = rot.astype(BF16)
            if isq:
                q_ref[:, j * HD : (j + 1) * HD] = rot
            else:
                k_ref[:, (j - ATTN_HEADS) * HD : (j - ATTN_HEADS + 1) * HD] = rot
        v_ref[...] = raw_ref[:, NQ + NKV :].astype(BF16)

    return _rowcall(
        "qkv_fwd", body, nblk, nlb,
        [(x, "row"), (mod, "stream"), (nw1, "full"), (w_qkv, "full"), (gains, "full"), (cos_t, "row"), (sin_t, "row")],
        [((r, NQ + 2 * NKV), F32, "row"), ((r, NQ), BF16, "row"), ((r, NKV), BF16, "row"), ((r, NKV), BF16, "row"),
         ((r, D), BF16, "row")],
    )


def attn_fwd(q, k, v, nblk, nlb):
    r = q.shape[0]

    def body(q_ref, k_ref, v_ref, o_ref, lse_ref):
        is_ctx = pl.program_id(0) == nlb
        hide = is_ctx & (lax.broadcasted_iota(jnp.int32, (1, r), 1) < nlb * BM)
        for kvh in range(ATTN_KV_HEADS):
            kh = k_ref[:, kvh * HD : (kvh + 1) * HD]
            vh = v_ref[:, kvh * HD : (kvh + 1) * HD]
            for g in range(ATTN_GROUP):
                cs = slice((kvh * ATTN_GROUP + g) * HD, (kvh * ATTN_GROUP + g + 1) * HD)
                s = jnp.where(hide, -1e30, _dot_nt(q_ref[:, cs], kh))
                m = jnp.max(s, axis=-1, keepdims=True)
                p = jnp.exp(s - m)
                l = jnp.sum(p, axis=-1, keepdims=True)
                o_ref[:, cs] = (_dotf(p.astype(BF16), vh) / l).astype(BF16)
                j = kvh * ATTN_GROUP + g
                lse_ref[:, j : j + 1] = m + jnp.log(l)

    return _rowcall(
        "attn_fwd", body, nblk, nlb,
        [(q, "row"), (k, "full"), (v, "full")],
        [((r, NQ), BF16, "row"), ((r, ATTN_HEADS), F32, "row")],
    )


def attn_out_fwd(o, w_o, x, mod, nblk, nlb):
    r = x.shape[0]

    def body(o_ref, w_ref, x_ref, mod_ref, x1_ref, y_ref):
        y = _dotf(o_ref[...], w_ref[...])
        y_ref[...] = y
        x1_ref[...] = x_ref[...] + _mrow(mod_ref, G1) * y

    return _rowcall(
        "attn_out_fwd", body, nblk, nlb,
        [(o, "row"), (w_o, "full"), (x, "row"), (mod, "stream")],
        [((r, D), F32, "row"), ((r, D), F32, "row")],
    )


def mix_out_bwd(name, dx1, y, w_out, mod, nblk, nlb):
    r = dx1.shape[0]
    kin = w_out.shape[0]

    def body(d_ref, y_ref, w_ref, mod_ref, dy_ref, do_ref, dmod_ref):
        i = pl.program_id(0)
        _acc_init(i, dmod_ref)
        d = d_ref[...]
        dmod_ref[i // nlb, G1 : G1 + 1, :] += jnp.sum(d * y_ref[...], axis=0, keepdims=True)
        dy = (_mrow(mod_ref, G1) * d).astype(BF16)
        dy_ref[...] = dy
        do_ref[...] = _dot_nt(dy, w_ref[...]).astype(do_ref.dtype)

    return _rowcall(
        name, body, nblk, nlb,
        [(dx1, "row"), (y, "row"), (w_out, "full"), (mod, "stream")],
        [((r, D), BF16, "row"), ((r, kin), BF16, "row"), ((2, 8, D), F32, "acc")],
    )


ATTN_KCHUNK = 11 * BM


def attn_bwd(q, k, v, o, do, lse, nblk, nlb):
    r = q.shape[0]
    kc = ATTN_KCHUNK if r % ATTN_KCHUNK == 0 else BM
    nkc = r // kc

    def body(q_ref, k_ref, v_ref, o_ref, do_ref, lse_ref, dq_ref, dk_ref, dv_ref):
        i = pl.program_id(0)
        _acc_init(i, dk_ref, dv_ref)
        is_ctx = i == nlb
        for kvh in range(ATTN_KV_HEADS):
            ks = slice(kvh * HD, (kvh + 1) * HD)
            for g in range(ATTN_GROUP):
                j = kvh * ATTN_GROUP + g
                cs = slice(j * HD, (j + 1) * HD)
                qh, doh = q_ref[:, cs], do_ref[:, cs]
                delta = jnp.sum(doh.astype(F32) * o_ref[:, cs].astype(F32), axis=-1, keepdims=True)
                lse = lse_ref[:, j : j + 1]
                dq = jnp.zeros((BM, HD), F32)
                for c in range(nkc):
                    rs = slice(c * kc, (c + 1) * kc)
                    kh, vh = k_ref[rs, ks], v_ref[rs, ks]
                    hide = is_ctx & (c * kc + lax.broadcasted_iota(jnp.int32, (1, kc), 1) < nlb * BM)
                    p = jnp.where(hide, 0.0, jnp.exp(_dot_nt(qh, kh) - lse))
                    ds = (p * (_dot_nt(doh, vh) - delta)).astype(BF16)
                    dq = dq + _dotf(ds, kh)
                    dk_ref[rs, ks] += _dot_tn(ds, qh)
                    dv_ref[rs, ks] += _dot_tn(p.astype(BF16), doh)
                dq_ref[:, cs] = dq * ATTN_SCALE

    return _rowcall(
        "attn_bwd", body, nblk, nlb,
        [(q, "row"), (k, "full"), (v, "full"), (o, "row"), (do, "row"), (lse, "row")],
        [((r, NQ), F32, "row"), ((r, NKV), F32, "acc"), ((r, NKV), F32, "acc")],
    )


def qkv_bwd(dq, dk, dv, raw, gains, cos_t, sin_t, w_qkv, x, dx1, mod, nw1, nblk, nlb, ctx_dx_zero):
    r = x.shape[0]

    def body(dq_ref, dk_ref, dv_ref, raw_ref, g_ref, c_ref, s_ref, w_ref, x_ref, dx1_ref, mod_ref, nw_ref,
             dx_ref, draw_ref, dg_ref, dnw_ref, dmod_ref):
        i = pl.program_id(0)
        _acc_init(i, dg_ref, dnw_ref, dmod_ref)
        cos, sin = c_ref[...], s_ref[...]
        for j in range(ATTN_HEADS + ATTN_KV_HEADS):
            isq = j < ATTN_HEADS
            cs = slice(j * HD, (j + 1) * HD)
            dr = dq_ref[:, cs] if isq else dk_ref[:, (j - ATTN_HEADS) * HD : (j - ATTN_HEADS + 1) * HD]
            dxn = dr * cos + _partner(dr * sin)
            xhat, rr = _head_norm(raw_ref[:, cs], None)
            gi = 0 if isq else 1
            dg_ref[gi : gi + 1, :] += jnp.sum(dxn * xhat, axis=0, keepdims=True)
            dxhat = dxn * g_ref[gi : gi + 1, :]
            draw_ref[:, cs] = (rr * (dxhat - xhat * jnp.mean(dxhat * xhat, axis=-1, keepdims=True))).astype(BF16)
        draw_ref[:, NQ + NKV :] = dv_ref[...].astype(BF16)
        dh = _dot_nt(draw_ref[...], w_ref[...])
        sc, nw = _mrow(mod_ref, SC1), nw_ref[...]
        _, n, xhat, rr = _norm_mod(x_ref[...], nw, _mrow(mod_ref, SH1), sc)
        dxn, dsh, dsc, dnw = _norm_mod_bwd(dh, n, xhat, rr, nw, sc)
        dres = dx1_ref[...]
        if ctx_dx_zero:
            dres = jnp.where(i == nlb, 0.0, dres)
        dx_ref[...] = dres + dxn
        s = i // nlb
        dmod_ref[s, SH1 : SH1 + 1, :] += dsh
        dmod_ref[s, SC1 : SC1 + 1, :] += dsc
        dnw_ref[0:1, :] += dnw

    return _rowcall(
        "qkv_bwd", body, nblk, nlb,
        [(dq, "row"), (dk, "row"), (dv, "row"), (raw, "row"), (gains, "full"), (cos_t, "row"), (sin_t, "row"),
         (w_qkv, "full"), (x, "row"), (dx1, "row"), (mod, "stream"), (nw1, "full")],
        [((r, D), F32, "row"), ((r, NQ + 2 * NKV), BF16, "row"), ((8, HD), F32, "acc"), ((8, D), F32, "acc"),
         ((2, 8, D), F32, "acc")],
    )


RET_KSCALE = RET_DK ** -0.5


def ret_in_fwd(x, mod, nw1, w_in, nblk, nlb):
    r = x.shape[0]

    def body(x_ref, mod_ref, nw_ref, w_ref, q_ref, k_ref, v_ref, g_ref, h_ref):
        h = _norm_mod(x_ref[...], nw_ref[...], _mrow(mod_ref, SH1), _mrow(mod_ref, SC1))[0].astype(BF16)
        h_ref[...] = h
        q_ref[...] = _dotf(h, w_ref[:, 0:RNQ]).astype(BF16)
        k_ref[...] = (_dotf(h, w_ref[:, RNQ : 2 * RNQ]) * RET_KSCALE).astype(BF16)
        v_ref[...] = _dotf(h, w_ref[:, 2 * RNQ : 2 * RNQ + RNV]).astype(BF16)
        g_ref[...] = _dotf(h, w_ref[:, 2 * RNQ + RNV :]).astype(BF16)

    return _rowcall(
        "ret_in_fwd", body, nblk, nlb,
        [(x, "row"), (mod, "stream"), (nw1, "full"), (w_in, "full")],
        [((r, RNQ), BF16, "row"), ((r, RNQ), BF16, "row"), ((r, RNV), BF16, "row"), ((r, RNV), BF16, "row"), ((r, D), BF16, "row")],
    )


def _log_sigmoid(x):
    return jnp.minimum(x, 0.0) - jnp.log(1.0 + jnp.exp(-jnp.abs(x)))


def _ret_decays(lg, reverse):
    c = BM
    i = lax.broadcasted_iota(jnp.int32, (c, c), 0)
    j = lax.broadcasted_iota(jnp.int32, (c, c), 1)
    diff = (j - i) if reverse else (i - j)
    ediff = jnp.maximum(diff, 0).astype(F32)
    dm = jnp.where(diff >= 0, jnp.exp(ediff * lg), 0.0)
    rr = lax.broadcasted_iota(jnp.int32, (c, 1), 0).astype(F32)
    eq = (c - rr) if reverse else (rr + 1.0)
    ek = rr if reverse else (c - 1.0 - rr)
    return dm, ediff, jnp.exp(eq * lg), eq, jnp.exp(ek * lg), ek, jnp.exp(c * lg)


def _ret_chunk_index(nlb):
    return (lambda s: jnp.where(s == 0, nlb, s - 1)), (lambda s: jnp.where(s == 0, nlb, nlb - s))


def ret_scan_fwd(q, k, v, logit_b, nlb):
    r = q.shape[0]
    nb = nlb + 1
    fidx, bidx = _ret_chunk_index(nlb)

    def body(qf, kf, vf, qb, kb, vb, lg_ref, of_ref, ob_ref, rf_ref, rb_ref, stf, stb):
        s = pl.program_id(1)

        @pl.when(s == 0)
        def _():
            stf[...] = jnp.zeros(stf.shape, F32)
            stb[...] = jnp.zeros(stb.shape, F32)

        for d, (q_ref, k_ref, v_ref, o_ref, rs_ref, st) in enumerate(((qf, kf, vf, of_ref, rf_ref, stf), (qb, kb, vb, ob_ref, rb_ref, stb))):
            lg = _log_sigmoid(lg_ref[d, 0])[0:1, 0:1]
            dm, _, qd, _, kd, _, gc = _ret_decays(lg, d == 1)
            qq, kk, vv, st0 = q_ref[...], k_ref[...], v_ref[...], st[...]
            rs_ref[0, 0] = st0
            a = _dot_nt(qq, kk) * dm
            o = _dotf(a.astype(BF16), vv) + _dotf(qq, st0.astype(BF16)) * qd
            o_ref[...] = jnp.where(s == 0, 0.0, o)
            st[...] = st0 * gc + _dot_tn((kk.astype(F32) * kd).astype(BF16), vv)

    qspec = lambda f: pl.BlockSpec((BM, RET_DK), lambda h, s: (f(s), h))
    vspec = lambda f: pl.BlockSpec((BM, RET_DV), lambda h, s: (f(s), h))
    sspec = pl.BlockSpec((1, 1, RET_DK, RET_DV), lambda h, s: (h, s, 0, 0))
    return _pcall(
        body,
        name="ret_scan_fwd",
        grid=(RET_HEADS, nb),
        in_specs=[qspec(fidx), qspec(fidx), vspec(fidx), qspec(bidx), qspec(bidx), vspec(bidx),
                  pl.BlockSpec((2, 1, 8, 128), lambda h, s: (0, h, 0, 0))],
        out_specs=[vspec(fidx), vspec(bidx), sspec, sspec],
        out_shape=[jax.ShapeDtypeStruct((r, RNV), F32), jax.ShapeDtypeStruct((r, RNV), F32),
                   jax.ShapeDtypeStruct((RET_HEADS, nb, RET_DK, RET_DV), F32), jax.ShapeDtypeStruct((RET_HEADS, nb, RET_DK, RET_DV), F32)],
        scratch_shapes=[pltpu.VMEM((RET_DK, RET_DV), F32), pltpu.VMEM((RET_DK, RET_DV), F32)],
        compiler_params=pltpu.CompilerParams(dimension_semantics=("parallel", "arbitrary"), vmem_limit_bytes=VMEM_LIMIT),
    )(q, k, v, q, k, v, logit_b)


def _group_norm(o):
    mu = jnp.mean(o, axis=-1, keepdims=True)
    oc = o - mu
    rstd = lax.rsqrt(jnp.mean(oc * oc, axis=-1, keepdims=True) + EPS)
    return oc * rstd, rstd


def ret_out_fwd(o_f, o_b, g, gnw, w_out, x, mod, nblk, nlb):
    r = x.shape[0]

    def body(of_ref, ob_ref, g_ref, gn_ref, w_ref, x_ref, mod_ref, x1_ref, y_ref, z_ref):
        for hh in range(RET_HEADS):
            cs = slice(hh * RET_DV, (hh + 1) * RET_DV)
            yhat, _ = _group_norm(of_ref[:, cs] + ob_ref[:, cs])
            gg = g_ref[:, cs].astype(F32)
            z_ref[:, cs] = (gg * _sigmoid(gg) * (yhat * gn_ref[0:1, cs])).astype(BF16)
        y = _dotf(z_ref[...], w_ref[...])
        y_ref[...] = y
        x1_ref[...] = x_ref[...] + _mrow(mod_ref, G1) * y

    return _rowcall(
        "ret_out_fwd", body, nblk, nlb,
        [(o_f, "row"), (o_b, "row"), (g, "row"), (gnw, "full"), (w_out, "full"), (x, "row"), (mod, "stream")],
        [((r, D), F32, "row"), ((r, D), F32, "row"), ((r, RNV), BF16, "row")],
    )


def ret_out_bwd(dx1, y, o_f, o_b, g, gnw, w_out, mod, nblk, nlb):
    r = dx1.shape[0]

    def body(d_ref, y_ref, of_ref, ob_ref, g_ref, gn_ref, w_ref, mod_ref, dy_ref, do_ref, dg_ref, dgn_ref, dmod_ref, dz):
        i = pl.program_id(0)
        _acc_init(i, dgn_ref, dmod_ref)
        d = d_ref[...]
        dmod_ref[i // nlb, G1 : G1 + 1, :] += jnp.sum(d * y_ref[...], axis=0, keepdims=True)
        dy = (_mrow(mod_ref, G1) * d).astype(BF16)
        dy_ref[...] = dy
        dz[...] = _dot_nt(dy, w_ref[...])
        for hh in range(RET_HEADS):
            cs = slice(hh * RET_DV, (hh + 1) * RET_DV)
            yhat, rstd = _group_norm(of_ref[:, cs] + ob_ref[:, cs])
            gg = g_ref[:, cs].astype(F32)
            sg = _sigmoid(gg)
            gn = gn_ref[0:1, cs]
            dzz = dz[:, cs]
            dg_ref[:, cs] = (dzz * (yhat * gn) * (sg * (1.0 + gg * (1.0 - sg)))).astype(BF16)
            dyn = dzz * (gg * sg)
            dgn_ref[0:1, cs] += jnp.sum(dyn * yhat, axis=0, keepdims=True)
            dyh = dyn * gn
            do = rstd * (dyh - jnp.mean(dyh, axis=-1, keepdims=True) - yhat * jnp.mean(dyh * yhat, axis=-1, keepdims=True))
            do_ref[:, cs] = do.astype(BF16)

    return _rowcall(
        "ret_out_bwd", body, nblk, nlb,
        [(dx1, "row"), (y, "row"), (o_f, "row"), (o_b, "row"), (g, "row"), (gnw, "full"), (w_out, "full"), (mod, "stream")],
        [((r, D), BF16, "row"), ((r, RNV), BF16, "row"), ((r, RNV), BF16, "row"), ((8, RNV), F32, "acc"), ((2, 8, D), F32, "acc")],
        scratch=[pltpu.VMEM((BM, RNV), F32)],
    )


def ret_scan_bwd(q, k, v, do, st_f, st_b, logit_b, nlb):
    r = q.shape[0]
    nb = nlb + 1
    fidx, bidx = _ret_chunk_index(nlb)
    step = lambda t: nb - 1 - t

    def body(qf, kf, vf, dof, rf, qb, kb, vb, dob, rb, lg_ref,
             dqf, dkf, dvf, dqb, dkb, dvb, dlg_ref, drf, drb):
        t = pl.program_id(1)
        s = step(t)

        @pl.when(t == 0)
        def _():
            drf[...] = jnp.zeros(drf.shape, F32)
            drb[...] = jnp.zeros(drb.shape, F32)
            dlg_ref[...] = jnp.zeros(dlg_ref.shape, F32)

        dirs = ((qf, kf, vf, dof, rf, dqf, dkf, dvf, drf), (qb, kb, vb, dob, rb, dqb, dkb, dvb, drb))
        for d, (q_ref, k_ref, v_ref, do_ref, rs_ref, dq_ref, dk_ref, dv_ref, dr) in enumerate(dirs):
            lg = _log_sigmoid(lg_ref[d, 0])[0:1, 0:1]
            dm, ediff, qd, eq, kd, ek, gc = _ret_decays(lg, d == 1)
            qq, kk, vv = q_ref[...], k_ref[...], v_ref[...]
            dob16 = jnp.where(s == 0, jnp.zeros((), BF16), do_ref[...])
            do32 = dob16.astype(F32)
            st0 = rs_ref[0, 0]
            st16 = st0.astype(BF16)
            dr0 = dr[...]
            dr16 = dr0.astype(BF16)
            a = _dot_nt(qq, kk) * dm
            daf = _dot_nt(dob16, vv)
            ds = (daf * dm).astype(BF16)
            qr = _dotf(qq, st16)
            k32 = kk.astype(F32)
            kdec = (k32 * kd).astype(BF16)
            dv_ref[...] = (_dot_tn(a.astype(BF16), dob16) + _dotf(kdec, dr16)).astype(BF16)
            dq_ref[...] = (_dotf(ds, kk) + _dot_nt(dob16, st16) * qd).astype(BF16)
            vdr = _dot_nt(vv, dr16)
            dk_ref[...] = (_dot_tn(ds, qq) + vdr * kd).astype(BF16)
            tot = (jnp.sum(daf * a * ediff)
                   + jnp.sum(eq * qd * jnp.sum(do32 * qr, axis=-1, keepdims=True))
                   + jnp.sum(ek * kd * jnp.sum(k32 * vdr, axis=-1, keepdims=True))
                   + jnp.sum(BM * gc * jnp.sum(dr0 * st0, axis=-1, keepdims=True)))
            dlg_ref[d, 0] += tot
            dr[...] = gc * dr0 + _dot_tn(qq, (do32 * qd).astype(BF16))

        @pl.when(t == nb - 1)
        def _():
            dlg_ref[...] = dlg_ref[...] * _sigmoid(-lg_ref[...])

    qspec = lambda f: pl.BlockSpec((BM, RET_DK), lambda h, t: (f(step(t)), h))
    vspec = lambda f: pl.BlockSpec((BM, RET_DV), lambda h, t: (f(step(t)), h))
    sspec = pl.BlockSpec((1, 1, RET_DK, RET_DV), lambda h, t: (h, step(t), 0, 0))
    lspec = pl.BlockSpec((2, 1, 8, 128), lambda h, t: (0, h, 0, 0))
    sq, sv = jax.ShapeDtypeStruct((r, RNQ), BF16), jax.ShapeDtypeStruct((r, RNV), BF16)
    return _pcall(
        body,
        name="ret_scan_bwd",
        grid=(RET_HEADS, nb),
        in_specs=[qspec(fidx), qspec(fidx), vspec(fidx), vspec(fidx), sspec,
                  qspec(bidx), qspec(bidx), vspec(bidx), vspec(bidx), sspec, lspec],
        out_specs=[qspec(fidx), qspec(fidx), vspec(fidx), qspec(bidx), qspec(bidx), vspec(bidx), lspec],
        out_shape=[sq, sq, sv, sq, sq, sv, jax.ShapeDtypeStruct((2, RET_HEADS, 8, 128), F32)],
        scratch_shapes=[pltpu.VMEM((RET_DK, RET_DV), F32), pltpu.VMEM((RET_DK, RET_DV), F32)],
        compiler_params=pltpu.CompilerParams(dimension_semantics=("parallel", "arbitrary"), vmem_limit_bytes=VMEM_LIMIT),
    )(q, k, v, do, st_f, q, k, v, do, st_b, logit_b)


def ret_in_bwd(dqf, dkf, dvf, dqb, dkb, dvb, dgate, w_in, x, dx1, mod, nw1, nblk, nlb):
    r = x.shape[0]
    nin = 2 * RNQ + 2 * RNV

    def body(dqf_ref, dkf_ref, dvf_ref, dqb_ref, dkb_ref, dvb_ref, dg_ref, w_ref, x_ref, dx1_ref, mod_ref, nw_ref,
             dx_ref, din_ref, dnw_ref, dmod_ref):
        i = pl.program_id(0)
        _acc_init(i, dnw_ref, dmod_ref)
        is_ctx = i == nlb
        din_ref[:, 0:RNQ] = (dqf_ref[...].astype(F32) + dqb_ref[...].astype(F32)).astype(BF16)
        din_ref[:, RNQ : 2 * RNQ] = ((dkf_ref[...].astype(F32) + dkb_ref[...].astype(F32)) * RET_KSCALE).astype(BF16)
        din_ref[:, 2 * RNQ : 2 * RNQ + RNV] = (dvf_ref[...].astype(F32) + dvb_ref[...].astype(F32)).astype(BF16)
        din_ref[:, 2 * RNQ + RNV :] = jnp.where(is_ctx, jnp.zeros((), BF16), dg_ref[...])
        dh = _dot_nt(din_ref[...], w_ref[...])
        sc, nw = _mrow(mod_ref, SC1), nw_ref[...]
        _, n, xhat, rr = _norm_mod(x_ref[...], nw, _mrow(mod_ref, SH1), sc)
        dxn, dsh, dsc, dnw = _norm_mod_bwd(dh, n, xhat, rr, nw, sc)
        dx_ref[...] = jnp.where(is_ctx, 0.0, dx1_ref[...]) + dxn
        s = i // nlb
        dmod_ref[s, SH1 : SH1 + 1, :] += dsh
        dmod_ref[s, SC1 : SC1 + 1, :] += dsc
        dnw_ref[0:1, :] += dnw

    return _rowcall(
        "ret_in_bwd", body, nblk, nlb,
        [(dqf, "row"), (dkf, "row"), (dvf, "row"), (dqb, "row"), (dkb, "row"), (dvb, "row"), (dgate, "row"),
         (w_in, "full"), (x, "row"), (dx1, "row"), (mod, "stream"), (nw1, "full")],
        [((r, D), F32, "row"), ((r, nin), BF16, "row"), ((8, D), F32, "acc"), ((2, 8, D), F32, "acc")],
    )


def loss_head(xout, target, nlb):
    r = xout.shape[0]

    def body(x_ref, t_ref, dx_ref, l_ref):
        _acc_init(pl.program_id(0), l_ref)
        err = x_ref[...] - t_ref[...]
        dx_ref[...] = err * (1.0 / D)
        l_ref[...] += 0.5 * jnp.sum(jnp.mean(err * err, axis=-1, keepdims=True))

    return _rowcall(
        "loss_head", body, nlb, nlb,
        [(xout, "row"), (target, "row")],
        [((r, D), F32, "row"), ((8, 128), F32, "acc")],
    )


N_MIXERS = 3
POOL, ATTN, RET = range(3)


def _layer_plan(depth):
    plan = []
    for i in range(depth):
        kind = i % N_MIXERS
        ctx_out = any(k % N_MIXERS != POOL for k in range(i + 1, depth))
        plan.append((kind, i // N_MIXERS, ctx_out or kind != POOL, ctx_out))
    return plan


def local_step(xs, target, mods, w, nlb, depth, fetch, emit):
    nb = nlb + 1
    plan = _layer_plan(depth)
    saved = []
    x = xs
    for i, (kind, j, ctx_in, ctx_out) in enumerate(plan):
        nmix = nb if ctx_out else nlb
        mod, nw1, nw2 = mods[i], w["nw"][i, 0:1], w["nw"][i, 1:2]
        lw = fetch(i, x)
        sv = {"x": x, "lw": lw}
        if kind == POOL:
            x1, sv["ypre"] = pool_fwd(x, mod, nw1, lw["pool_w"], (w["pbs"], j), nmix, nlb)
        elif kind == ATTN:
            assert ctx_out
            sv["raw"], sv["q"], sv["k"], sv["v"], sv["h"] = qkv_fwd(x, mod, nw1, lw["attn_w_qkv"], w["gains"], w["cos"], w["sin"], nb, nlb)
            sv["o"], sv["lse"] = attn_fwd(sv["q"], sv["k"], sv["v"], nb, nlb)
            x1, sv["y"] = attn_out_fwd(sv["o"], lw["attn_w_o"], x, mod, nb, nlb)
        else:
            assert ctx_in and not ctx_out
            sv["q"], sv["k"], sv["v"], sv["g"], sv["h"] = ret_in_fwd(x, mod, nw1, lw["ret_w_in"], nb, nlb)
            sv["o_f"], sv["o_b"], sv["st_f"], sv["st_b"] = ret_scan_fwd(sv["q"], sv["k"], sv["v"], w["logit_b"], nlb)
            x1, sv["y"], sv["z"] = ret_out_fwd(sv["o_f"], sv["o_b"], sv["g"], w["gnw"], lw["ret_w_out"], x, mod, nlb, nlb)
        sv["x1"] = x1
        sv["u"], sv["h2"] = ffn_up(x1, mod, nw2, lw["ffn_w_up"], nmix, nlb)
        x, sv["f"], sv["uc"], sv["gated"] = ffn_down(sv["u"], (w["cw"], i), lw["ffn_w_down"], x1, mod, nmix, nlb)
        saved.append(sv)

    dx, loss_tile = loss_head(x, target, nlb)
    g = {k: [None] * depth for k in ("dcw", "dnw1", "dnw2", "dmod")}
    dep = loss_tile
    for i in reversed(range(depth)):
        kind, j, ctx_in, ctx_out = plan[i]
        sv = saved[i]
        lw, big = sv["lw"], {}
        nmix = nb if ctx_out else nlb
        mod, nw1, nw2 = mods[i], w["nw"][i, 0:1], w["nw"][i, 1:2]
        df, duc, dcb, dmod = ffn_bwd1(dx, sv["f"], sv["uc"], lw["ffn_w_down"], mod, dep, nmix, nlb)
        big["ffn_w_down"] = matmul_tn(sv["gated"], df, nmix)
        du, dx1, dcw, g["dnw2"][i], dm = ffn_bwd3(duc, sv["u"], (w["cw"], i), lw["ffn_w_up"], sv["x1"], dx, mod, nw2, nmix, nlb)
        g["dcw"][i] = dcw + dcb
        dmod = dmod + dm
        big["ffn_w_up"] = matmul_tn(sv["h2"], du, nmix)
        if kind == POOL:
            dx, dpw, dpbs, g["dnw1"][i], dm = pool_bwd(dx1, sv["x"], sv["ypre"], mod, nw1, lw["pool_w"], (w["pbs"], j), nmix, nlb)
            big["pool_w"] = dpw.astype(BF16)
            g.setdefault("dpbs", {})[j] = dpbs
        elif kind == ATTN:
            dy, do, dm1 = mix_out_bwd("attn_out_bwd", dx1, sv["y"], lw["attn_w_o"], mod, nb, nlb)
            big["attn_w_o"] = matmul_tn(sv["o"], dy, nb)
            dq, dk, dv = attn_bwd(sv["q"], sv["k"], sv["v"], sv["o"], do, sv["lse"], nb, nlb)
            dx, draw, g["dgains"], g["dnw1"][i], dm = qkv_bwd(
                dq, dk, dv, sv["raw"], w["gains"], w["cos"], w["sin"], lw["attn_w_qkv"], sv["x"], dx1, mod, nw1, nb, nlb, False)
            big["attn_w_qkv"] = matmul_tn(sv["h"], draw, nb)
            dm = dm + dm1
        else:
            dy, do, dgate, g["dgnw"], dm1 = ret_out_bwd(dx1, sv["y"], sv["o_f"], sv["o_b"], sv["g"], w["gnw"], lw["ret_w_out"], mod, nlb, nlb)
            big["ret_w_out"] = matmul_tn(sv["z"], dy, nlb)
            dqf, dkf, dvf, dqb, dkb, dvb, g["dlogit"] = ret_scan_bwd(sv["q"], sv["k"], sv["v"], do, sv["st_f"], sv["st_b"], w["logit_b"], nlb)
            dx, din, g["dnw1"][i], dm = ret_in_bwd(dqf, dkf, dvf, dqb, dkb, dvb, dgate, lw["ret_w_in"], sv["x"], dx1, mod, nw1, nb, nlb)
            big["ret_w_in"] = matmul_tn(sv["h"], din, nb)
            dm = dm + dm1
        g["dmod"][i] = dmod + dm
        dep = emit(i, big)
    return loss_tile, dx, g


MESH_ID = pl.DeviceIdType.MESH
CHIP_FLIPS = ((1, 0), (0, 1), (1, 1))


def _pos():
    return lax.axis_index("x"), lax.axis_index("y"), lax.axis_index("c")


def _flip(v, b):
    return 1 - v if b else v


def small_all_gather(name, x):
    rows, n = x.shape

    def body(x_ref, out_ref, send_sems, recv_sems, local_sem):
        mx, my, mc = _pos()
        me = 4 * mx + 2 * my + mc
        mine = pltpu.make_async_copy(x_ref, out_ref.at[me], local_sem)
        mine.start()
        sends, peers = [], []
        for kk in range(1, N_DEV):
            peer = (_flip(mx, (kk >> 2) & 1), _flip(my, (kk >> 1) & 1), _flip(mc, kk & 1))
            cp = pltpu.make_async_remote_copy(src_ref=x_ref, dst_ref=out_ref.at[me], send_sem=send_sems.at[kk - 1],
                                              recv_sem=recv_sems.at[kk - 1], device_id=peer, device_id_type=MESH_ID)
            cp.start()
            sends.append(cp)
            peers.append(peer)
        for kk, peer in enumerate(peers):
            pidx = 4 * peer[0] + 2 * peer[1] + peer[2]
            pltpu.make_async_remote_copy(src_ref=x_ref, dst_ref=out_ref.at[pidx], send_sem=send_sems.at[kk],
                                         recv_sem=recv_sems.at[kk], device_id=peer, device_id_type=MESH_ID).wait_recv()
        for cp in sends:
            cp.wait_send()
        mine.wait()

    return _pcall(
        body,
        name=name,
        out_shape=jax.ShapeDtypeStruct((N_DEV, rows, n), x.dtype),
        in_specs=[pl.BlockSpec(memory_space=pltpu.VMEM)],
        out_specs=pl.BlockSpec(memory_space=pltpu.VMEM),
        scratch_shapes=[pltpu.SemaphoreType.DMA((N_DEV - 1,)), pltpu.SemaphoreType.DMA((N_DEV - 1,)), pltpu.SemaphoreType.DMA],
        compiler_params=pltpu.CompilerParams(vmem_limit_bytes=VMEM_LIMIT),
    )(x)


def _hbm_exchange(name, ins, out_shapes, plan):
    n_in = len(ins)
    probe_local, probe_remote = plan([None] * n_in, [None] * len(out_shapes), probe=True)

    def body(*refs):
        in_refs, out_refs = refs[:n_in], refs[n_in : n_in + len(out_shapes)]
        send_sems, recv_sems, local_sems = refs[n_in + len(out_shapes) :]
        local, remote = plan(in_refs, out_refs, probe=False)
        lcs = [pltpu.make_async_copy(s, d, local_sems.at[k]) for k, (s, d) in enumerate(local)]
        for cp in lcs:
            cp.start()
        rcs = []
        for k, (s, d, peer, _) in enumerate(remote):
            cp = pltpu.make_async_remote_copy(src_ref=s, dst_ref=d, send_sem=send_sems.at[k], recv_sem=recv_sems.at[k],
                                              device_id=peer, device_id_type=MESH_ID)
            cp.start()
            rcs.append(cp)
        for k, (s, _, peer, here) in enumerate(remote):
            pltpu.make_async_remote_copy(src_ref=s, dst_ref=here, send_sem=send_sems.at[k], recv_sem=recv_sems.at[k],
                                         device_id=peer, device_id_type=MESH_ID).wait_recv()
        for cp in rcs:
            cp.wait_send()
        for cp in lcs:
            cp.wait()

    return _pcall(
        body,
        name=name,
        out_shape=list(out_shapes),
        in_specs=[pl.BlockSpec(memory_space=pl.ANY)] * n_in,
        out_specs=[pl.BlockSpec(memory_space=pl.ANY)] * len(out_shapes),
        scratch_shapes=[pltpu.SemaphoreType.DMA((max(probe_remote, 1),)), pltpu.SemaphoreType.DMA((max(probe_remote, 1),)),
                        pltpu.SemaphoreType.DMA((max(probe_local, 1),))],
    )(*ins)


def _at_axis(ref, axis, start, size):
    return ref.at[tuple(pl.ds(start, size) if a == axis else slice(None) for a in range(len(ref.shape)))]


HBM_SPEC = pl.BlockSpec(memory_space=pltpu.HBM)
SEM_SPEC = pl.BlockSpec(memory_space=pltpu.SEMAPHORE)
SIDE_EFFECT = pltpu.SideEffectType.DATAFLOW_SIDE_EFFECTING


def _in_hbm(a):
    return pltpu.with_memory_space_constraint(a, pltpu.HBM)


def _copies_start(name, bufs, counts, plan, after=None):
    n, ng = len(bufs), len(counts)
    extra = [] if after is None else [after]
    n_in = n + len(extra)

    def body(*refs):
        sems = refs[n_in : n_in + 2 * ng]
        token = refs[n_in + 2 * ng + n]
        for gi, copies in enumerate(plan(refs[:n])):
            for k, (s, d, peer) in enumerate(copies):
                pltpu.make_async_remote_copy(src_ref=s, dst_ref=d, send_sem=sems[2 * gi].at[k], recv_sem=sems[2 * gi + 1].at[k],
                                             device_id=peer, device_id_type=MESH_ID).start()
        token[...] = jnp.zeros(token.shape, token.dtype)

    out = _pcall(
        body,
        name=name,
        out_shape=tuple(pltpu.SemaphoreType.DMA((c,)) for c in counts for _ in range(2))
        + tuple(pltpu.HBM(b.shape, b.dtype) for b in bufs) + (jax.ShapeDtypeStruct((8, 128), F32),),
        in_specs=(HBM_SPEC,) * n + (pl.BlockSpec(memory_space=pl.ANY),) * len(extra),
        out_specs=(SEM_SPEC,) * (2 * ng) + (HBM_SPEC,) * n + (pl.BlockSpec(memory_space=pltpu.VMEM),),
        input_output_aliases={i: 2 * ng + i for i in range(n)},
        compiler_params=pltpu.CompilerParams(has_side_effects=SIDE_EFFECT),
    )(*[_in_hbm(b) for b in bufs], *extra)
    return [(out[2 * g], out[2 * g + 1]) for g in range(ng)], list(out[2 * ng : 2 * ng + n]), out[2 * ng + n]


def _copies_wait(name, sems, bufs, plan, after):
    n, ng = len(bufs), len(sems)

    def body(*refs):
        for gi, copies in enumerate(plan(refs[:n])):
            for k, (s, d, peer) in enumerate(copies):
                cp = pltpu.make_async_remote_copy(src_ref=s, dst_ref=d, send_sem=refs[n + 2 * gi].at[k], recv_sem=refs[n + 2 * gi + 1].at[k],
                                                  device_id=peer, device_id_type=MESH_ID)
                cp.wait_send()
                cp.wait_recv()

    out = _pcall(
        body,
        name=name,
        out_shape=tuple(pltpu.HBM(b.shape, b.dtype) for b in bufs),
        in_specs=(HBM_SPEC,) * n + (SEM_SPEC,) * (2 * ng) + (pl.BlockSpec(memory_space=pl.ANY),),
        out_specs=(HBM_SPEC,) * n,
        input_output_aliases={i: i for i in range(n)},
        compiler_params=pltpu.CompilerParams(has_side_effects=SIDE_EFFECT),
    )(*bufs, *[s for pair in sems for s in pair], after)
    return list(out)


def _layer_matrices(depth):
    out = []
    for i, (kind, j, _, _) in enumerate(_layer_plan(depth)):
        mix = ([("pool_w", j, 1)], [("attn_w_qkv", j, 1), ("attn_w_o", j, 0)], [("ret_w_in", j, 1), ("ret_w_out", j, 0)])[kind]
        out.append(mix + [("ffn_w_up", i, 1), ("ffn_w_down", i, 0)])
    return out


def _peers(mx, my, mc):
    out = []
    for fx, fy in CHIP_FLIPS:
        px, py = _flip(mx, fx), _flip(my, fy)
        out.append(((px, py, mc), 2 * px + py))
    return out


def full_buffers(shards, names, layers):
    out = []
    for name, _, axis in [e for layer in layers for e in layer]:
        shp = list(shards[names.index(name)].shape[1:])
        shp[axis] *= N_CHIP
        out.append(lax.empty(tuple(shp), BF16))
    return out


def _gather_plan(names, layers, group, n_shards, here):
    def plan(refs):
        mx, my, mc = _pos()
        s_refs, f_refs = refs[:n_shards], refs[n_shards:]
        groups, k = [], 0
        for gi, layer in enumerate(layers):
            if group is not None and gi != group:
                continue
            copies = []
            for name, idx, axis in layer:
                src = s_refs[names.index(name)].at[idx]
                n = src.shape[axis]
                for peer, pchip in _peers(mx, my, mc) + [((mx, my, 1 - mc), 2 * mx + my)]:
                    at = pchip if here else 2 * mx + my
                    copies.append((src, _at_axis(f_refs[k], axis, at * n, n), peer))
                k += 1
            groups.append(copies)
        return groups

    return plan


def gather_start(shards, names, layers, fulls, after):
    counts = [len(layer) * (len(CHIP_FLIPS) + 1) for layer in layers]
    sems, bufs, _ = _copies_start("gather_start", list(shards) + list(fulls), counts,
                                  _gather_plan(names, layers, None, len(shards), False), after)
    return sems, bufs[: len(shards)], bufs[len(shards) :]


def gather_wait(g, sems_g, shards, names, layers, fulls_g, after):
    bufs = _copies_wait(f"gather_wait_{g}", [sems_g], list(shards) + list(fulls_g), _gather_plan(names, layers, g, len(shards), True), after)
    return bufs[: len(shards)], bufs[len(shards) :]


def _scatter_plan(layer_entries, n_grads, land_of):
    def plan(refs):
        mx, my, mc = _pos()
        groups, k = [], 0
        for layer in layer_entries:
            copies = []
            for name, idx, axis in layer:
                gref, land = refs[k], refs[n_grads + land_of(k, name)]
                n = gref.shape[axis] // N_CHIP
                for slot, (peer, pchip) in enumerate(_peers(mx, my, mc)):
                    copies.append((_at_axis(gref, axis, pchip * n, n), land.at[slot, idx], peer))
                k += 1
            groups.append(copies)
        return groups

    return plan


def scatter_start(i, layer, grads, lands):
    sems, bufs, token = _copies_start(f"scatter_start_{i}", list(grads) + list(lands), [len(layer) * len(CHIP_FLIPS)],
                                      _scatter_plan([layer], len(grads), lambda k, name: k))
    return sems[0], bufs[: len(grads)], bufs[len(grads) :], token


def scatter_wait(sems, layers, grads, names, lands, after):
    bufs = _copies_wait("scatter_wait", sems, list(grads) + list(lands),
                        _scatter_plan(layers, len(grads), lambda k, name: names.index(name)), after)
    return bufs[: len(grads)], bufs[len(grads) :]


def sibling_swap(parts):
    def plan(in_refs, out_refs, probe):
        if probe:
            return 0, len(parts)
        mx, my, mc = _pos()
        return [], [(s, o, (mx, my, 1 - mc), o) for s, o in zip(in_refs, out_refs)]

    return _hbm_exchange("sibling_swap", parts, [jax.ShapeDtypeStruct(p.shape, p.dtype) for p in parts], plan)


EW_ROWS = 256


def _ew_call(name, fn, ins, n_out):
    rows, cols = ins[0].shape[-2:]
    tr = EW_ROWS if rows % EW_ROWS == 0 else rows

    def body(*refs):
        outs = fn(*[r[...] for r in refs[: len(ins)]])
        for o_ref, o in zip(refs[len(ins) :], outs):
            o_ref[...] = o

    def spec(a):
        if a.ndim == 3:
            return pl.BlockSpec((a.shape[0], tr, cols), lambda i: (0, i, 0))
        return pl.BlockSpec((tr, cols), lambda i: (i, 0))

    return _pcall(
        body,
        name=name,
        grid=(rows // tr,),
        in_specs=[spec(a) for a in ins],
        out_specs=[pl.BlockSpec((tr, cols), lambda i: (i, 0))] * n_out,
        out_shape=[jax.ShapeDtypeStruct((rows, cols), F32)] * n_out,
        compiler_params=pltpu.CompilerParams(dimension_semantics=("parallel",), vmem_limit_bytes=VMEM_LIMIT),
    )(*ins)


def _adamw(w, g, m, v):
    m = ADAM_B1 * m + (1.0 - ADAM_B1) * g
    v = ADAM_B2 * v + (1.0 - ADAM_B2) * (g * g)
    m_hat = m / (1.0 - ADAM_B1 ** ADAM_STEP)
    v_hat = v / (1.0 - ADAM_B2 ** ADAM_STEP)
    return -ADAM_LR * (m_hat / (jnp.sqrt(v_hat) + ADAM_EPS) + ADAM_WD * w), m, v


def sum_slots(name, own, landing):
    def fn(o, l):
        acc = o.astype(F32)
        for k in range(l.shape[0]):
            acc = acc + l[k].astype(F32)
        return (acc,)

    return _ew_call(name, fn, [own, landing], 1)[0]


def adamw_pair(name, w, m, v, p, ps):
    def fn(w, m, v, p, ps):
        g = p + ps
        return (g,) + _adamw(w, g, m, v)

    return _ew_call(name, fn, [w, m, v, p, ps], 4)


def adamw_one(name, w, m, v, g):
    return _ew_call(name, lambda w, m, v, g: _adamw(w, g, m, v), [w, m, v, g], 3)


def reduce_devices(name, x):
    def fn(a):
        acc = a[0]
        for k in range(1, a.shape[0]):
            acc = acc + a[k]
        return (acc,)

    return _ew_call(name, fn, [x], 1)[0]


ADA_ROWS = 16
ADA_CTX = N_DEV


def ada_fwd(s9, ada_w, ada_b):
    depth, _, n = ada_w.shape

    def body(s_ref, w_ref, b_ref, o_ref):
        s = s_ref[...]
        o_ref[...] = _dotf((s * _sigmoid(s)).astype(BF16), w_ref[...].astype(BF16)) + b_ref[...]

    return _pcall(
        body,
        name="ada_fwd",
        grid=(depth,),
        in_specs=[pl.BlockSpec((ADA_ROWS, D), lambda i: (0, 0)), pl.BlockSpec((None, D, n), lambda i: (i, 0, 0)),
                  pl.BlockSpec((None, 1, n), lambda i: (i, 0, 0))],
        out_specs=pl.BlockSpec((None, ADA_ROWS, n), lambda i: (i, 0, 0)),
        out_shape=jax.ShapeDtypeStruct((depth, ADA_ROWS, n), F32),
        compiler_params=pltpu.CompilerParams(dimension_semantics=("arbitrary",), vmem_limit_bytes=VMEM_LIMIT),
    )(s9, ada_w, ada_b)


def ada_bwd(s9, dm, ada_w):
    depth, _, n = ada_w.shape

    def body(s_ref, dm_ref, w_ref, gw_ref, ds_ref):
        _acc_init(pl.program_id(0), ds_ref)
        s = s_ref[...]
        dmb = dm_ref[...].astype(BF16)
        gw_ref[...] = _dot_tn((s * _sigmoid(s)).astype(BF16), dmb)
        ds_ref[...] += _dot_nt(dmb, w_ref[...].astype(BF16))

    return _pcall(
        body,
        name="ada_bwd",
        grid=(depth,),
        in_specs=[pl.BlockSpec((ADA_ROWS, D), lambda i: (0, 0)), pl.BlockSpec((None, ADA_ROWS, n), lambda i: (i, 0, 0)),
                  pl.BlockSpec((None, D, n), lambda i: (i, 0, 0))],
        out_specs=[pl.BlockSpec((None, D, n), lambda i: (i, 0, 0)), pl.BlockSpec((ADA_ROWS, D), lambda i: (0, 0))],
        out_shape=[jax.ShapeDtypeStruct((depth, D, n), F32), jax.ShapeDtypeStruct((ADA_ROWS, D), F32)],
        compiler_params=pltpu.CompilerParams(dimension_semantics=("arbitrary",), vmem_limit_bytes=VMEM_LIMIT),
    )(s9, dm, ada_w)


def cctx_grad(parts, c_ctx):
    def body(p_ref, c_ref, o_ref):
        acc = p_ref[0, ADA_CTX : ADA_CTX + 1, :]
        for chip in range(1, N_CHIP):
            acc = acc + p_ref[2 * chip, ADA_CTX : ADA_CTX + 1, :]
        c = c_ref[...]
        sg = _sigmoid(c)
        o_ref[...] = acc * (sg * (1.0 + c * (1.0 - sg)))

    return _pcall(body, name="cctx_grad", out_shape=jax.ShapeDtypeStruct((1, D), F32))(parts, c_ctx)


def _pack(arrs):
    flat = jnp.concatenate([a.astype(F32).reshape(-1) for a in arrs])
    rows = -(-flat.shape[0] // (8 * D)) * 8
    return jnp.pad(flat, (0, rows * D - flat.shape[0])).reshape(rows, D)


def _unpack(slab, shapes):
    lead = slab.shape[:-2]
    flat = slab.reshape(lead + (-1,))
    out, off = [], 0
    for shp in shapes:
        n = 1
        for d in shp:
            n *= d
        out.append(flat[..., off : off + n].reshape(lead + tuple(shp)))
        off += n
    return out


def _unshard(per_dev, axis):
    return jnp.concatenate([per_dev[2 * chip] for chip in range(N_CHIP)], axis=axis)


BIG = (("pool_w", 1), ("attn_w_qkv", 1), ("attn_w_o", 0), ("ret_w_in", 1), ("ret_w_out", 0), ("ffn_w_up", 1), ("ffn_w_down", 0))
WEIGHTS = ("c_ctx", "ada_w", "ada_b", "norm_w", "pool_w", "pool_b", "pool_scale", "attn_w_qkv", "attn_q_gain", "attn_k_gain",
           "attn_w_o", "ret_w_in", "ret_decay_logit", "ret_gn_w", "ret_w_out", "ffn_w_up", "ffn_conv_w", "ffn_conv_b", "ffn_w_down")
SMALL = tuple(n for n in WEIGHTS if n != "ada_w" and n not in dict(BIG))
SMALL_SHARD_AXIS = {"norm_w": 2, "pool_b": 1, "pool_scale": 1, "ret_gn_w": 1, "ffn_conv_w": 2}


def kernel(x, c, ctx, c_ctx, ada_w, ada_b, norm_w, pool_w, pool_b, pool_scale, attn_w_qkv, attn_q_gain, attn_k_gain, attn_w_o, ret_w_in, ret_decay_logit, ret_gn_w, ret_w_out, ffn_w_up, ffn_conv_w, ffn_conv_b, ffn_w_down, loss_target, m_c_ctx, m_ada_w, m_ada_b, m_norm_w, m_pool_w, m_pool_b, m_pool_scale, m_attn_w_qkv, m_attn_q_gain, m_attn_k_gain, m_attn_w_o, m_ret_w_in, m_ret_decay_logit, m_ret_gn_w, m_ret_w_out, m_ffn_w_up, m_ffn_conv_w, m_ffn_conv_b, m_ffn_w_down, v_c_ctx, v_ada_w, v_ada_b, v_norm_w, v_pool_w, v_pool_b, v_pool_scale, v_attn_w_qkv, v_attn_q_gain, v_attn_k_gain, v_attn_w_o, v_ret_w_in, v_ret_decay_logit, v_ret_gn_w, v_ret_w_out, v_ffn_w_up, v_ffn_conv_w, v_ffn_conv_b, v_ffn_w_down):
    P = dict(zip(WEIGHTS, (c_ctx, ada_w, ada_b, norm_w, pool_w, pool_b, pool_scale, attn_w_qkv, attn_q_gain, attn_k_gain, attn_w_o,
                           ret_w_in, ret_decay_logit, ret_gn_w, ret_w_out, ffn_w_up, ffn_conv_w, ffn_conv_b, ffn_w_down)))
    M = dict(zip(WEIGHTS, (m_c_ctx, m_ada_w, m_ada_b, m_norm_w, m_pool_w, m_pool_b, m_pool_scale, m_attn_w_qkv, m_attn_q_gain,
                           m_attn_k_gain, m_attn_w_o, m_ret_w_in, m_ret_decay_logit, m_ret_gn_w, m_ret_w_out, m_ffn_w_up,
                           m_ffn_conv_w, m_ffn_conv_b, m_ffn_w_down)))
    V = dict(zip(WEIGHTS, (v_c_ctx, v_ada_w, v_ada_b, v_norm_w, v_pool_w, v_pool_b, v_pool_scale, v_attn_w_qkv, v_attn_q_gain,
                           v_attn_k_gain, v_attn_w_o, v_ret_w_in, v_ret_decay_logit, v_ret_gn_w, v_ret_w_out, v_ffn_w_up,
                           v_ffn_conv_w, v_ffn_conv_b, v_ffn_w_down)))
    depth, s, l = ada_w.shape[0], x.shape[1], ctx.shape[1]
    assert l == BM and s % BM == 0 and s % GRID_W == 0
    nlb = s // BM
    n_pool = pool_w.shape[0]
    mx, my, mc = _pos()
    chip, dev = 2 * mx + my, 4 * mx + 2 * my + mc
    nada = ada_w.shape[2]

    sharded = [n for n in SMALL if n in SMALL_SHARD_AXIS]
    got = small_all_gather("gather_small", _pack([c[0]] + [P[n] for n in sharded]))
    got = _unpack(got, [(D,)] + [P[n].shape for n in sharded])
    c_all = got[0]
    full = {n: _unshard(g_, SMALL_SHARD_AXIS[n]) for n, g_ in zip(sharded, got[1:])}

    s9 = jnp.concatenate([c_all, c_ctx[None, :], jnp.zeros((ADA_ROWS - N_DEV - 1, D), F32)], axis=0)
    ada_b_mine = lax.dynamic_slice_in_dim(ada_b, chip * nada, nada, axis=1)[:, None, :]
    mod_part = ada_fwd(s9, ada_w, ada_b_mine)
    mod_all = _unshard(small_all_gather("gather_mod", mod_part.reshape(depth * ADA_ROWS, nada)), 1).reshape(depth, ADA_ROWS, 6, D)
    mod_mine = lax.dynamic_index_in_dim(mod_all, dev, axis=1, keepdims=False)
    mods_all = jnp.pad(jnp.stack([mod_mine, mod_all[:, ADA_CTX]], axis=1), ((0, 0), (0, 0), (0, 2), (0, 0)))
    mods = [mods_all[i] for i in range(depth)]

    names = [n for n, _ in BIG]
    layers = _layer_matrices(depth)
    shards = [P[n].astype(BF16) for n in names]
    gsems, shards, fulls = gather_start(shards, names, layers, full_buffers(shards, names, layers), mods_all)
    first = [sum(len(layer) for layer in layers[:i]) for i in range(depth + 1)]
    flight = {"shards": shards}

    def fetch(i, after):
        flight["shards"], mats = gather_wait(i, gsems[i], flight["shards"], names, layers, fulls[first[i] : first[i + 1]], after)
        return {name: m for (name, _, _), m in zip(layers[i], mats)}

    lands = {n: lax.empty((len(CHIP_FLIPS),) + P[n].shape, BF16) for n in names}
    sent = {}

    def emit(i, big):
        lnames = [name for name, _, _ in layers[i]]
        sems, gl, ll, token = scatter_start(i, layers[i], [big[n] for n in lnames], [lands[n] for n in lnames])
        lands.update(zip(lnames, ll))
        sent[i] = (sems, gl)
        return token

    w = {
        "nw": full["norm_w"],
        "pbs": jnp.concatenate([full["pool_b"][:, None], full["pool_scale"][:, None], jnp.zeros((n_pool, 6, D), F32)], axis=1),
        "gains": jnp.concatenate([attn_q_gain, attn_k_gain, jnp.zeros((6, HD), F32)], axis=0),
        "gnw": full["ret_gn_w"],
        "logit_b": jnp.broadcast_to(ret_decay_logit[0][:, :, None, None], (2, RET_HEADS, 8, 128)),
        "cw": jnp.concatenate([full["ffn_conv_w"], ffn_conv_b[:, None, :], jnp.zeros((depth, 4, 2 * D_FF), F32)], axis=1),
    }
    w["cos"], w["sin"] = rope_tables(s, l)

    xs = jnp.concatenate([x[0], ctx[0]], axis=0)
    loss_tile, dxs, g = local_step(xs, loss_target[0], mods, w, nlb, depth, fetch, emit)
    loss = lax.psum(loss_tile[0, 0], MESH_AXES)
    grad_x = dxs[:s][None]

    small_shapes = [(depth, 2, 8, D), (depth, 2, D), (n_pool, 2, D), (2, HD), (2, RET_HEADS), (RNV,), (depth, 4, 2 * D_FF)]
    slab = _pack([
        jnp.stack(g["dmod"]),
        jnp.stack([jnp.stack([g["dnw1"][i][0], g["dnw2"][i][0]]) for i in range(depth)]),
        jnp.stack([g["dpbs"][j][0:2] for j in range(n_pool)]),
        g["dgains"][0:2], g["dlogit"][:, :, 0, 0], g["dgnw"][0], jnp.stack([g["dcw"][i][0:4] for i in range(depth)]),
    ])
    slabs = small_all_gather("gather_small_grads", slab)
    dmod_dev = _unpack(slabs, small_shapes[:1])[0]
    t_dmod, t_nw, t_pbs, t_gains, t_logit, t_gnw, t_cw = _unpack(reduce_devices("reduce_small_grads", slabs), small_shapes)

    def cols(a):
        return lax.dynamic_slice_in_dim(a, chip * nada, nada, axis=a.ndim - 1)

    dm_lat = jnp.swapaxes(cols(dmod_dev[:, :, 0, :6].reshape(N_DEV, depth, 6 * D)), 0, 1)
    dm_ctx = cols(t_dmod[:, 1, :6].reshape(depth, 1, 6 * D))
    dm = jnp.concatenate([dm_lat, dm_ctx, jnp.zeros((depth, ADA_ROWS - N_DEV - 1, nada), F32)], axis=1)
    g_ada_w, ds9 = ada_bwd(s9, dm, ada_w)
    g_c_ctx = cctx_grad(small_all_gather("gather_dcctx", ds9), c_ctx[None, :])[0]

    def mine(a, name):
        n = P[name].shape[SMALL_SHARD_AXIS[name]]
        return lax.dynamic_slice_in_dim(a, chip * n, n, axis=SMALL_SHARD_AXIS[name])

    G = {
        "c_ctx": g_c_ctx,
        "ada_b": (t_dmod[:, 0, :6] + t_dmod[:, 1, :6]).reshape(depth, 6 * D),
        "norm_w": mine(t_nw, "norm_w"),
        "pool_b": mine(t_pbs[:, 0], "pool_b"), "pool_scale": mine(t_pbs[:, 1], "pool_scale"),
        "attn_q_gain": t_gains[0:1], "attn_k_gain": t_gains[1:2],
        "ret_decay_logit": t_logit[None], "ret_gn_w": mine(t_gnw[None], "ret_gn_w"),
        "ffn_conv_w": mine(t_cw[:, 0:3], "ffn_conv_w"), "ffn_conv_b": t_cw[:, 3],
    }
    sw, sg, sm, sv = (_pack([d_[n] for n in SMALL]) for d_ in (P, G, M, V))
    outs = adamw_one("adamw_small", sw, sm, sv, sg)
    D_, NM, NV = ({n: a for n, a in zip(SMALL, _unpack(o, [P[n].shape for n in SMALL]))} for o in outs)

    flat2 = lambda a: a.reshape(-1, a.shape[-1])
    G["ada_w"] = g_ada_w
    o3 = adamw_one("adamw_ada", flat2(ada_w), flat2(M["ada_w"]), flat2(V["ada_w"]), flat2(g_ada_w))
    D_["ada_w"], NM["ada_w"], NV["ada_w"] = (o.reshape(ada_w.shape) for o in o3)

    sent_grads, landed = scatter_wait([sent[i][0] for i in range(depth)], layers, [a for i in range(depth) for a in sent[i][1]],
                                      names, [lands[n] for n in names], dxs)
    own = {n: [None] * P[n].shape[0] for n in names}
    for (name, idx, axis), a in zip([e for layer in layers for e in layer], sent_grads):
        n_ = a.shape[axis] // N_CHIP
        own[name][idx] = lax.dynamic_slice_in_dim(a, chip * n_, n_, axis=axis)
    partial = [sum_slots("sum_" + n, jnp.stack(own[n]).reshape(-1, lnd.shape[-1]), lnd.reshape(len(CHIP_FLIPS), -1, lnd.shape[-1]))
               for n, lnd in zip(names, landed)]
    theirs = sibling_swap(partial)
    for (n, _), p, ps in zip(BIG, partial, theirs):
        o4 = adamw_pair("adamw_" + n, flat2(P[n]), flat2(M[n]), flat2(V[n]), p, ps)
        G[n], D_[n], NM[n], NV[n] = (o.reshape(P[n].shape) for o in o4)

    return (loss, grad_x, *[G[n] for n in WEIGHTS], *[D_[n] for n in WEIGHTS], *[NM[n] for n in WEIGHTS], *[NV[n] for n in WEIGHTS])
```

```python
import functools

import jax
import jax.numpy as jnp
from jax import lax
from jax.experimental import pallas as pl
from jax.experimental.pallas import tpu as pltpu

F32 = jnp.float32
BF16 = jnp.bfloat16

D = 1024
BM = 256
EPS = 1e-6
POOL_WINDOWS = (2, 4, 8, 16)
POOL_GROUP = D // 4
ATTN_HEADS = 8
ATTN_KV_HEADS = 2
HD = D // ATTN_HEADS
ATTN_GROUP = ATTN_HEADS // ATTN_KV_HEADS
NQ = ATTN_HEADS * HD
NKV = ATTN_KV_HEADS * HD
GRID_W = 64
ROPE_THETA = 10000.0
RET_HEADS = 4
RET_DK = D // RET_HEADS
RET_DV = 2 * D // RET_HEADS
RNQ = RET_HEADS * RET_DK
RNV = RET_HEADS * RET_DV
D_FF = 2816
FF_CHUNK = 256
ADAM_LR, ADAM_B1, ADAM_B2, ADAM_EPS, ADAM_WD, ADAM_STEP = 0.001, 0.9, 0.999, 1e-08, 0.01, 10
HALO_F32 = 8
HALO_BF16 = 16
VMEM_LIMIT = 60 * 1024 * 1024

MESH_AXES = ("x", "y", "c")
N_DEV = 8
N_CHIP = 4


def _pcall(body, **kw):
    return pl.pallas_call(body, **kw)


def _spec(shape, kind, nlb, nblk):
    nd = len(shape)
    if kind == "row":
        return pl.BlockSpec((BM, shape[1]), lambda i: (i, 0))
    if kind == "full":
        return pl.BlockSpec(tuple(shape), lambda i: (0,) * nd, pipeline_mode=pl.Buffered(1))
    if isinstance(kind, tuple) and kind[0] == "fullat":
        return pl.BlockSpec((None,) + tuple(shape[1:]), lambda i: (kind[1],) + (0,) * (nd - 1), pipeline_mode=pl.Buffered(1))
    if kind == "acc":
        return pl.BlockSpec(tuple(shape), lambda i: (0,) * nd)
    if kind == "any":
        return pl.BlockSpec(memory_space=pl.ANY)
    if kind == "stream":
        return pl.BlockSpec((1,) + tuple(shape[1:]), lambda i: (i // nlb,) + (0,) * (nd - 1))
    if kind in ("prev8", "prev16"):
        hb = int(kind[4:])
        return pl.BlockSpec((hb, shape[1]), lambda i: (jnp.maximum(i * (BM // hb) - 1, 0), 0))
    if kind in ("next8", "next16"):
        hb = int(kind[4:])
        last = nblk * BM // hb - 1
        return pl.BlockSpec((hb, shape[1]), lambda i: (jnp.minimum((i + 1) * (BM // hb), last), 0))
    raise ValueError(kind)


def _rowcall(name, body, nblk, nlb, ins, outs, scratch=()):
    return _pcall(
        body,
        name=name,
        grid=(nblk,),
        in_specs=[_spec(a.shape, k, nlb, nblk) for a, k in ins],
        out_specs=[_spec(s, k, nlb, nblk) for s, _, k in outs],
        out_shape=[jax.ShapeDtypeStruct(s, d) for s, d, _ in outs],
        scratch_shapes=list(scratch),
        compiler_params=pltpu.CompilerParams(dimension_semantics=("arbitrary",), vmem_limit_bytes=VMEM_LIMIT),
    )(*[a for a, _ in ins])


def _wk(w):
    return (w[0], ("fullat", w[1])) if isinstance(w, tuple) else (w, "full")


def _stream_edges(i, nlb):
    is_ctx = i == nlb
    return (i == 0) | is_ctx, (i == nlb - 1) | is_ctx, is_ctx


def _dotf(a, b):
    return jnp.dot(a, b, preferred_element_type=F32)


def _dot_nt(a, b):
    return lax.dot_general(a, b, (((1,), (1,)), ((), ())), preferred_element_type=F32)


def _dot_tn(a, b):
    return lax.dot_general(a, b, (((0,), (0,)), ((), ())), preferred_element_type=F32)


def _sigmoid(x):
    return 0.5 * jnp.tanh(0.5 * x) + 0.5


def _norm_mod(x, nw, sh, sc):
    r = lax.rsqrt(jnp.mean(x * x, axis=-1, keepdims=True) + EPS)
    xhat = x * r
    n = xhat * nw
    return n * (1.0 + sc) + sh, n, xhat, r


def _norm_mod_bwd(dh, n, xhat, r, nw, sc):
    dsh = jnp.sum(dh, axis=0, keepdims=True)
    dsc = jnp.sum(dh * n, axis=0, keepdims=True)
    dn = dh * (1.0 + sc)
    dnw = jnp.sum(dn * xhat, axis=0, keepdims=True)
    dxhat = dn * nw
    dx = r * (dxhat - xhat * jnp.mean(dxhat * xhat, axis=-1, keepdims=True))
    return dx, dsh, dsc, dnw


def _acc_init(i, *refs):
    @pl.when(i == 0)
    def _():
        for r in refs:
            r[...] = jnp.zeros(r.shape, r.dtype)


SH1, SC1, G1, SH2, SC2, G2 = range(6)


def _mrow(mod_ref, k):
    return mod_ref[0, k : k + 1, :]


def _shift_rows(x_ref, xp_ref, xn_ref, cs, first, last):
    cur = x_ref[:, cs].astype(F32)
    rows = lax.broadcasted_iota(jnp.int32, cur.shape, 0)
    pr = jnp.where(first, 0.0, xp_ref[HALO_BF16 - 1 : HALO_BF16, cs].astype(F32))
    nx = jnp.where(last, 0.0, xn_ref[0:1, cs].astype(F32))
    dn = jnp.where(rows == 0, pr, pltpu.roll(cur, 1, 0))
    up = jnp.where(rows == BM - 1, nx, pltpu.roll(cur, BM - 1, 0))
    return dn, cur, up


def ffn_up(x1, mod, nw2, w_up, nblk, nlb):
    r = x1.shape[0]

    def body(x_ref, mod_ref, nw_ref, w_ref, u_ref, h_ref):
        h, _, _, _ = _norm_mod(x_ref[...], nw_ref[...], _mrow(mod_ref, SH2), _mrow(mod_ref, SC2))
        hb = h.astype(BF16)
        h_ref[...] = hb
        u_ref[...] = _dotf(hb, w_ref[...]).astype(BF16)

    return _rowcall(
        "ffn_up", body, nblk, nlb,
        [(x1, "row"), (mod, "stream"), (nw2, "full"), _wk(w_up)],
        [((r, 2 * D_FF), BF16, "row"), ((r, D), BF16, "row")],
    )


def _conv_gate_chunk(u_ref, up_ref, un_ref, cw_ref, j, first, last):
    res = []
    for half in range(2):
        c0 = half * D_FF + j * FF_CHUNK
        cs = slice(c0, c0 + FF_CHUNK)
        dn, cur, up = _shift_rows(u_ref, up_ref, un_ref, cs, first, last)
        val = dn * cw_ref[0:1, cs] + cur * cw_ref[1:2, cs] + up * cw_ref[2:3, cs] + cw_ref[3:4, cs]
        res.append((val, dn, cur, up, cs))
    return res


def ffn_down(u, cw, w_down, x1, mod, nblk, nlb):
    r = x1.shape[0]

    def body(u_ref, up_ref, un_ref, cw_ref, w_ref, x_ref, mod_ref, x2_ref, f_ref, uc_ref, gated_ref):
        first, last, _ = _stream_edges(pl.program_id(0), nlb)
        f = jnp.zeros((BM, D), F32)
        for j in range(D_FF // FF_CHUNK):
            (a, _, _, _, acs), (v, _, _, _, vcs) = _conv_gate_chunk(u_ref, up_ref, un_ref, cw_ref, j, first, last)
            uc_ref[:, acs] = a.astype(BF16)
            uc_ref[:, vcs] = v.astype(BF16)
            gated = (a * _sigmoid(a) * v).astype(BF16)
            gated_ref[:, acs] = gated
            f = f + _dotf(gated, w_ref[acs, :])
        f_ref[...] = f.astype(BF16)
        x2_ref[...] = x_ref[...] + _mrow(mod_ref, G2) * f

    return _rowcall(
        "ffn_down", body, nblk, nlb,
        [(u, "row"), (u, "prev16"), (u, "next16"), _wk(cw), _wk(w_down), (x1, "row"), (mod, "stream")],
        [((r, D), F32, "row"), ((r, D), BF16, "row"), ((r, 2 * D_FF), BF16, "row"), ((r, D_FF), BF16, "row")],
    )


def ffn_bwd1(dx2, f, uc, w_down, mod, dep, nblk, nlb):
    r = dx2.shape[0]

    def body(dx_ref, f_ref, uc_ref, w_ref, mod_ref, dep_ref, df_ref, duc_ref, dcb_ref, dmod_ref):
        i = pl.program_id(0)
        _acc_init(i, dcb_ref, dmod_ref)
        dx = dx_ref[...]
        df = (_mrow(mod_ref, G2) * dx).astype(BF16)
        df_ref[...] = df
        dmod_ref[i // nlb, G2 : G2 + 1, :] += jnp.sum(dx * f_ref[...].astype(F32), axis=0, keepdims=True)
        for j in range(D_FF // FF_CHUNK):
            acs = slice(j * FF_CHUNK, (j + 1) * FF_CHUNK)
            vcs = slice(D_FF + j * FF_CHUNK, D_FF + (j + 1) * FF_CHUNK)
            a, v = uc_ref[:, acs].astype(F32), uc_ref[:, vcs].astype(F32)
            sa = _sigmoid(a)
            dg = _dot_nt(df, w_ref[acs, :])
            for dval, cs in ((dg * v * (sa * (1.0 + a * (1.0 - sa))), acs), (dg * (a * sa), vcs)):
                duc_ref[:, cs] = dval.astype(BF16)
                dcb_ref[3:4, cs] += jnp.sum(dval, axis=0, keepdims=True)

    return _rowcall(
        "ffn_bwd1", body, nblk, nlb,
        [(dx2, "row"), (f, "row"), (uc, "row"), _wk(w_down), (mod, "stream"), (dep, "any")],
        [((r, D), BF16, "row"), ((r, 2 * D_FF), BF16, "row"), ((8, 2 * D_FF), F32, "acc"), ((2, 8, D), F32, "acc")],
    )


def ffn_bwd3(duc, u, cw, w_up, x1, dx2, mod, nw2, nblk, nlb):
    r = dx2.shape[0]

    def body(d_ref, dp_ref, dn_ref, u_ref, cw_ref, w_ref, x_ref, dx_ref, mod_ref, nw_ref, du_ref, dx1_ref, dcw_ref, dnw_ref, dmod_ref):
        i = pl.program_id(0)
        first, last, _ = _stream_edges(i, nlb)
        _acc_init(i, dcw_ref, dnw_ref, dmod_ref)
        dh = jnp.zeros((BM, D), F32)
        for j in range(2 * D_FF // FF_CHUNK):
            cs = slice(j * FF_CHUNK, (j + 1) * FF_CHUNK)
            dn, cur, up = _shift_rows(d_ref, dp_ref, dn_ref, cs, first, last)
            du = (up * cw_ref[0:1, cs] + cur * cw_ref[1:2, cs] + dn * cw_ref[2:3, cs]).astype(BF16)
            du_ref[:, cs] = du
            dh = dh + _dot_nt(du, w_ref[:, cs])
            uu = u_ref[:, cs].astype(F32)
            dcw_ref[0:1, cs] += jnp.sum(up * uu, axis=0, keepdims=True)
            dcw_ref[1:2, cs] += jnp.sum(cur * uu, axis=0, keepdims=True)
            dcw_ref[2:3, cs] += jnp.sum(dn * uu, axis=0, keepdims=True)
        sc = _mrow(mod_ref, SC2)
        nw = nw_ref[...]
        _, n, xhat, rr = _norm_mod(x_ref[...], nw, _mrow(mod_ref, SH2), sc)
        dxn, dsh, dsc, dnw = _norm_mod_bwd(dh, n, xhat, rr, nw, sc)
        dx1_ref[...] = dx_ref[...] + dxn
        s = i // nlb
        dmod_ref[s, SH2 : SH2 + 1, :] += dsh
        dmod_ref[s, SC2 : SC2 + 1, :] += dsc
        dnw_ref[0:1, :] += dnw

    return _rowcall(
        "ffn_bwd3", body, nblk, nlb,
        [(duc, "row"), (duc, "prev16"), (duc, "next16"), (u, "row"), _wk(cw), _wk(w_up), (x1, "row"), (dx2, "row"),
         (mod, "stream"), (nw2, "full")],
        [((r, 2 * D_FF), BF16, "row"), ((r, D), F32, "row"), ((8, 2 * D_FF), F32, "acc"), ((8, D), F32, "acc"),
         ((2, 8, D), F32, "acc")],
    )


def matmul_tn(a, b, nblk, tn=None):
    k, n = a.shape[1], b.shape[1]
    rows = nblk * BM
    tr = 768 if rows % 768 == 0 else (1024 if rows % 1024 == 0 else BM)
    if tn is None:
        tn = n
        while k * tn * 4 > 6 * 1024 * 1024 and tn % 256 == 0:
            tn //= 2
    steps = rows // tr

    def body(a_ref, b_ref, o_ref, acc):
        t = pl.program_id(1)

        @pl.when(t == 0)
        def _():
            acc[...] = jnp.zeros(acc.shape, acc.dtype)

        acc[...] += _dot_tn(a_ref[...], b_ref[...])

        @pl.when(t == steps - 1)
        def _():
            o_ref[...] = acc[...].astype(o_ref.dtype)

    return _pcall(
        body,
        name="matmul_tn",
        grid=(n // tn, steps),
        in_specs=[pl.BlockSpec((tr, k), lambda j, t: (t, 0)), pl.BlockSpec((tr, tn), lambda j, t: (t, j))],
        out_specs=pl.BlockSpec((k, tn), lambda j, t: (0, j)),
        out_shape=jax.ShapeDtypeStruct((k, n), BF16),
        scratch_shapes=[pltpu.VMEM((k, tn), F32)],
        compiler_params=pltpu.CompilerParams(dimension_semantics=("parallel", "arbitrary"), vmem_limit_bytes=VMEM_LIMIT),
    )(a, b)


EXT = BM + 2 * HALO_F32


def _pool_positions(i, nlb, nrows, row0):
    is_ctx = i == nlb
    t = (i - jnp.where(is_ctx, nlb, 0)) * BM + row0 + lax.broadcasted_iota(jnp.int32, (nrows, 1), 0)
    return t, jnp.where(is_ctx, BM, nlb * BM)


def _pool_cnt(t, win, slen):
    return (jnp.minimum(t + win // 2, slen) - jnp.maximum(t - win // 2, 0)).astype(F32)


def _pool_fill_ext(ext, i, nlb, x_ref, xp_ref, xn_ref, mod_ref, nw_ref):
    first, last, _ = _stream_edges(i, nlb)
    sh, sc, nw = _mrow(mod_ref, SH1), _mrow(mod_ref, SC1), nw_ref[...]
    hcur, n, xhat, r = _norm_mod(x_ref[...], nw, sh, sc)
    ext[0:HALO_F32, :] = jnp.where(first, 0.0, _norm_mod(xp_ref[...], nw, sh, sc)[0])
    ext[HALO_F32 : HALO_F32 + BM, :] = hcur
    ext[HALO_F32 + BM :, :] = jnp.where(last, 0.0, _norm_mod(xn_ref[...], nw, sh, sc)[0])
    return n, xhat, r


def _window_sum(ref, cols, offs):
    acc = None
    for o in offs:
        v = ref[HALO_F32 + o : HALO_F32 + o + BM, cols]
        acc = v if acc is None else acc + v
    return acc


def _pool_diff(ext, g, win, t, slen):
    cols = slice(g * POOL_GROUP, (g + 1) * POOL_GROUP)
    ssum = _window_sum(ext, cols, range(-(win // 2), win // 2))
    return ssum / _pool_cnt(t, win, slen) - ext[HALO_F32 : HALO_F32 + BM, cols]


def pool_fwd(x, mod, nw1, pw, pbs, nblk, nlb):
    r = x.shape[0]

    def body(x_ref, xp_ref, xn_ref, mod_ref, nw_ref, pw_ref, pbs_ref, x1_ref, ypre_ref, ext):
        i = pl.program_id(0)
        _pool_fill_ext(ext, i, nlb, x_ref, xp_ref, xn_ref, mod_ref, nw_ref)
        t, slen = _pool_positions(i, nlb, BM, 0)
        for g, win in enumerate(POOL_WINDOWS):
            cols = slice(g * POOL_GROUP, (g + 1) * POOL_GROUP)
            diff = _pool_diff(ext, g, win, t, slen)
            ypre = _dotf(diff.astype(BF16), pw_ref[g]) + pbs_ref[0:1, cols]
            ypre_ref[:, cols] = ypre
            x1_ref[:, cols] = x_ref[:, cols] + mod_ref[0, G1 : G1 + 1, cols] * (ypre * pbs_ref[1:2, cols])

    return _rowcall(
        "pool_fwd", body, nblk, nlb,
        [(x, "row"), (x, "prev8"), (x, "next8"), (mod, "stream"), (nw1, "full"), _wk(pw), _wk(pbs)],
        [((r, D), F32, "row"), ((r, D), F32, "row")],
        scratch=[pltpu.VMEM((EXT, D), F32)],
    )


def pool_bwd(dx1, x, ypre, mod, nw1, pw, pbs, nblk, nlb):
    r = x.shape[0]

    def body(d_ref, dp_ref, dn_ref, x_ref, xp_ref, xn_ref, ypre_ref, mod_ref, nw_ref, pw_ref, pbs_ref,
             dx_ref, dpw_ref, dpbs_ref, dnw_ref, dmod_ref, ext, dext, eext, dh):
        i = pl.program_id(0)
        first, last, _ = _stream_edges(i, nlb)
        _acc_init(i, dpw_ref, dpbs_ref, dnw_ref, dmod_ref)
        n, xhat, rr = _pool_fill_ext(ext, i, nlb, x_ref, xp_ref, xn_ref, mod_ref, nw_ref)
        g1, scale = _mrow(mod_ref, G1), pbs_ref[1:2, :]
        dcur = d_ref[...]
        ypre = ypre_ref[...]
        s = i // nlb
        dmod_ref[s, G1 : G1 + 1, :] += jnp.sum(dcur * (ypre * scale), axis=0, keepdims=True)
        dy = g1 * dcur
        dpbs_ref[1:2, :] += jnp.sum(dy * ypre, axis=0, keepdims=True)
        dpbs_ref[0:1, :] += jnp.sum(dy * scale, axis=0, keepdims=True)
        gs = g1 * scale
        dext[0:HALO_F32, :] = jnp.where(first, 0.0, gs * dp_ref[...])
        dext[HALO_F32 : HALO_F32 + BM, :] = dy * scale
        dext[HALO_F32 + BM :, :] = jnp.where(last, 0.0, gs * dn_ref[...])
        t, slen = _pool_positions(i, nlb, BM, 0)
        text, _ = _pool_positions(i, nlb, EXT, -HALO_F32)
        for g, win in enumerate(POOL_WINDOWS):
            cols = slice(g * POOL_GROUP, (g + 1) * POOL_GROUP)
            diff = _pool_diff(ext, g, win, t, slen)
            dpre = dext[:, cols].astype(BF16)
            ddiff = _dot_nt(dpre, pw_ref[g])
            eext[...] = ddiff / jnp.maximum(_pool_cnt(text, win, slen), 1.0)
            dh[:, cols] = _window_sum(eext, slice(None), range(-(win // 2) + 1, win // 2 + 1)) - ddiff[HALO_F32 : HALO_F32 + BM, :]
            dpw_ref[g] += _dot_tn(diff.astype(BF16), dpre[HALO_F32 : HALO_F32 + BM, :])
        sc, nw = _mrow(mod_ref, SC1), nw_ref[...]
        dxn, dsh, dsc, dnw = _norm_mod_bwd(dh[...], n, xhat, rr, nw, sc)
        dx_ref[...] = dcur + dxn
        dmod_ref[s, SH1 : SH1 + 1, :] += dsh
        dmod_ref[s, SC1 : SC1 + 1, :] += dsc
        dnw_ref[0:1, :] += dnw

    return _rowcall(
        "pool_bwd", body, nblk, nlb,
        [(dx1, "row"), (dx1, "prev8"), (dx1, "next8"), (x, "row"), (x, "prev8"), (x, "next8"), (ypre, "row"),
         (mod, "stream"), (nw1, "full"), _wk(pw), _wk(pbs)],
        [((r, D), F32, "row"), ((4, POOL_GROUP, POOL_GROUP), F32, "acc"), ((8, D), F32, "acc"), ((8, D), F32, "acc"),
         ((2, 8, D), F32, "acc")],
        scratch=[pltpu.VMEM((EXT, D), F32), pltpu.VMEM((EXT, D), F32), pltpu.VMEM((EXT, POOL_GROUP), F32), pltpu.VMEM((BM, D), F32)],
    )


def rope_tables(s, l):
    rows = s // GRID_W
    row = jnp.broadcast_to(jnp.arange(rows)[:, None], (rows, GRID_W)).reshape(-1).astype(F32)
    col = jnp.broadcast_to(jnp.arange(GRID_W)[None, :], (rows, GRID_W)).reshape(-1).astype(F32)
    axis_dim = HD // 2
    inv = ROPE_THETA ** (-jnp.arange(0, axis_dim, 2, dtype=F32) / axis_dim)
    ar, ac = row[:, None] * inv, col[:, None] * inv
    cos = jnp.concatenate([jnp.cos(ar), jnp.cos(ar), jnp.cos(ac), jnp.cos(ac)], axis=-1)
    sin = jnp.concatenate([-jnp.sin(ar), jnp.sin(ar), -jnp.sin(ac), jnp.sin(ac)], axis=-1)
    return (jnp.concatenate([cos, jnp.ones((l, HD), F32)], axis=0), jnp.concatenate([sin, jnp.zeros((l, HD), F32)], axis=0))


def _partner(x):
    q = HD // 4
    lane = lax.broadcasted_iota(jnp.int32, x.shape, 1)
    return jnp.where((lane // q) % 2 == 0, pltpu.roll(x, HD - q, 1), pltpu.roll(x, q, 1))


def _head_norm(raw, gain):
    r = lax.rsqrt(jnp.mean(raw * raw, axis=-1, keepdims=True) + EPS)
    return raw * r, r


ATTN_SCALE = HD ** -0.5


def qkv_fwd(x, mod, nw1, w_qkv, gains, cos_t, sin_t, nblk, nlb):
    r = x.shape[0]

    def body(x_ref, mod_ref, nw_ref, w_ref, g_ref, c_ref, s_ref, raw_ref, q_ref, k_ref, v_ref, h_ref):
        h = _norm_mod(x_ref[...], nw_ref[...], _mrow(mod_ref, SH1), _mrow(mod_ref, SC1))[0].astype(BF16)
        h_ref[...] = h
        raw_ref[...] = _dotf(h, w_ref[...])
        cos, sin = c_ref[...], s_ref[...]
        for j in range(ATTN_HEADS + ATTN_KV_HEADS):
            isq = j < ATTN_HEADS
            xn = _head_norm(raw_ref[:, j * HD : (j + 1) * HD], None)[0] * (g_ref[0:1, :] if isq else g_ref[1:2, :])
            rot = xn * cos + _partner(xn) * sin
            if isq:
                rot = rot * ATTN_SCALE
            rot = rot.astype(BF16)
            if isq:
                q_ref[:, j * HD : (j + 1) * HD] = rot
            else:
                k_ref[:, (j - ATTN_HEADS) * HD : (j - ATTN_HEADS + 1) * HD] = rot
        v_ref[...] = raw_ref[:, NQ + NKV :].astype(BF16)

    return _rowcall(
        "qkv_fwd", body, nblk, nlb,
        [(x, "row"), (mod, "stream"), (nw1, "full"), (w_qkv, "full"), (gains, "full"), (cos_t, "row"), (sin_t, "row")],
        [((r, NQ + 2 * NKV), F32, "row"), ((r, NQ), BF16, "row"), ((r, NKV), BF16, "row"), ((r, NKV), BF16, "row"),
         ((r, D), BF16, "row")],
    )


def attn_fwd(q, k, v, nblk, nlb):
    r = q.shape[0]

    def body(q_ref, k_ref, v_ref, o_ref, lse_ref):
        is_ctx = pl.program_id(0) == nlb
        hide = is_ctx & (lax.broadcasted_iota(jnp.int32, (1, r), 1) < nlb * BM)
        for kvh in range(ATTN_KV_HEADS):
            kh = k_ref[:, kvh * HD : (kvh + 1) * HD]
            vh = v_ref[:, kvh * HD : (kvh + 1) * HD]
            for g in range(ATTN_GROUP):
                cs = slice((kvh * ATTN_GROUP + g) * HD, (kvh * ATTN_GROUP + g + 1) * HD)
                s = jnp.where(hide, -1e30, _dot_nt(q_ref[:, cs], kh))
                m = jnp.max(s, axis=-1, keepdims=True)
                p = jnp.exp(s - m)
                l = jnp.sum(p, axis=-1, keepdims=True)
                o_ref[:, cs] = (_dotf(p.astype(BF16), vh) / l).astype(BF16)
                j = kvh * ATTN_GROUP + g
                lse_ref[:, j : j + 1] = m + jnp.log(l)

    return _rowcall(
        "attn_fwd", body, nblk, nlb,
        [(q, "row"), (k, "full"), (v, "full")],
        [((r, NQ), BF16, "row"), ((r, ATTN_HEADS), F32, "row")],
    )


def attn_out_fwd(o, w_o, x, mod, nblk, nlb):
    r = x.shape[0]

    def body(o_ref, w_ref, x_ref, mod_ref, x1_ref, y_ref):
        y = _dotf(o_ref[...], w_ref[...])
        y_ref[...] = y
        x1_ref[...] = x_ref[...] + _mrow(mod_ref, G1) * y

    return _rowcall(
        "attn_out_fwd", body, nblk, nlb,
        [(o, "row"), (w_o, "full"), (x, "row"), (mod, "stream")],
        [((r, D), F32, "row"), ((r, D), F32, "row")],
    )


def mix_out_bwd(name, dx1, y, w_out, mod, nblk, nlb):
    r = dx1.shape[0]
    kin = w_out.shape[0]

    def body(d_ref, y_ref, w_ref, mod_ref, dy_ref, do_ref, dmod_ref):
        i = pl.program_id(0)
        _acc_init(i, dmod_ref)
        d = d_ref[...]
        dmod_ref[i // nlb, G1 : G1 + 1, :] += jnp.sum(d * y_ref[...], axis=0, keepdims=True)
        dy = (_mrow(mod_ref, G1) * d).astype(BF16)
        dy_ref[...] = dy
        do_ref[...] = _dot_nt(dy, w_ref[...]).astype(do_ref.dtype)

    return _rowcall(
        name, body, nblk, nlb,
        [(dx1, "row"), (y, "row"), (w_out, "full"), (mod, "stream")],
        [((r, D), BF16, "row"), ((r, kin), BF16, "row"), ((2, 8, D), F32, "acc")],
    )


ATTN_KCHUNK = 11 * BM


def attn_bwd(q, k, v, o, do, lse, nblk, nlb):
    r = q.shape[0]
    kc = ATTN_KCHUNK if r % ATTN_KCHUNK == 0 else BM
    nkc = r // kc

    def body(q_ref, k_ref, v_ref, o_ref, do_ref, lse_ref, dq_ref, dk_ref, dv_ref):
        i = pl.program_id(0)
        _acc_init(i, dk_ref, dv_ref)
        is_ctx = i == nlb
        for kvh in range(ATTN_KV_HEADS):
            ks = slice(kvh * HD, (kvh + 1) * HD)
            for g in range(ATTN_GROUP):
                j = kvh * ATTN_GROUP + g
                cs = slice(j * HD, (j + 1) * HD)
                qh, doh = q_ref[:, cs], do_ref[:, cs]
                delta = jnp.sum(doh.astype(F32) * o_ref[:, cs].astype(F32), axis=-1, keepdims=True)
                lse = lse_ref[:, j : j + 1]
                dq = jnp.zeros((BM, HD), F32)
                for c in range(nkc):
                    rs = slice(c * kc, (c + 1) * kc)
                    kh, vh = k_ref[rs, ks], v_ref[rs, ks]
                    hide = is_ctx & (c * kc + lax.broadcasted_iota(jnp.int32, (1, kc), 1) < nlb * BM)
                    p = jnp.where(hide, 0.0, jnp.exp(_dot_nt(qh, kh) - lse))
                    ds = (p * (_dot_nt(doh, vh) - delta)).astype(BF16)
                    dq = dq + _dotf(ds, kh)
                    dk_ref[rs, ks] += _dot_tn(ds, qh)
                    dv_ref[rs, ks] += _dot_tn(p.astype(BF16), doh)
                dq_ref[:, cs] = dq * ATTN_SCALE

    return _rowcall(
        "attn_bwd", body, nblk, nlb,
        [(q, "row"), (k, "full"), (v, "full"), (o, "row"), (do, "row"), (lse, "row")],
        [((r, NQ), F32, "row"), ((r, NKV), F32, "acc"), ((r, NKV), F32, "acc")],
    )


def qkv_bwd(dq, dk, dv, raw, gains, cos_t, sin_t, w_qkv, x, dx1, mod, nw1, nblk, nlb, ctx_dx_zero):
    r = x.shape[0]

    def body(dq_ref, dk_ref, dv_ref, raw_ref, g_ref, c_ref, s_ref, w_ref, x_ref, dx1_ref, mod_ref, nw_ref,
             dx_ref, draw_ref, dg_ref, dnw_ref, dmod_ref):
        i = pl.program_id(0)
        _acc_init(i, dg_ref, dnw_ref, dmod_ref)
        cos, sin = c_ref[...], s_ref[...]
        for j in range(ATTN_HEADS + ATTN_KV_HEADS):
            isq = j < ATTN_HEADS
            cs = slice(j * HD, (j + 1) * HD)
            dr = dq_ref[:, cs] if isq else dk_ref[:, (j - ATTN_HEADS) * HD : (j - ATTN_HEADS + 1) * HD]
            dxn = dr * cos + _partner(dr * sin)
            xhat, rr = _head_norm(raw_ref[:, cs], None)
            gi = 0 if isq else 1
            dg_ref[gi : gi + 1, :] += jnp.sum(dxn * xhat, axis=0, keepdims=True)
            dxhat = dxn * g_ref[gi : gi + 1, :]
            draw_ref[:, cs] = (rr * (dxhat - xhat * jnp.mean(dxhat * xhat, axis=-1, keepdims=True))).astype(BF16)
        draw_ref[:, NQ + NKV :] = dv_ref[...].astype(BF16)
        dh = _dot_nt(draw_ref[...], w_ref[...])
        sc, nw = _mrow(mod_ref, SC1), nw_ref[...]
        _, n, xhat, rr = _norm_mod(x_ref[...], nw, _mrow(mod_ref, SH1), sc)
        dxn, dsh, dsc, dnw = _norm_mod_bwd(dh, n, xhat, rr, nw, sc)
        dres = dx1_ref[...]
        if ctx_dx_zero:
            dres = jnp.where(i == nlb, 0.0, dres)
        dx_ref[...] = dres + dxn
        s = i // nlb
        dmod_ref[s, SH1 : SH1 + 1, :] += dsh
        dmod_ref[s, SC1 : SC1 + 1, :] += dsc
        dnw_ref[0:1, :] += dnw

    return _rowcall(
        "qkv_bwd", body, nblk, nlb,
        [(dq, "row"), (dk, "row"), (dv, "row"), (raw, "row"), (gains, "full"), (cos_t, "row"), (sin_t, "row"),
         (w_qkv, "full"), (x, "row"), (dx1, "row"), (mod, "stream"), (nw1, "full")],
        [((r, D), F32, "row"), ((r, NQ + 2 * NKV), BF16, "row"), ((8, HD), F32, "acc"), ((8, D), F32, "acc"),
         ((2, 8, D), F32, "acc")],
    )


RET_KSCALE = RET_DK ** -0.5


def ret_in_fwd(x, mod, nw1, w_in, nblk, nlb):
    r = x.shape[0]

    def body(x_ref, mod_ref, nw_ref, w_ref, q_ref, k_ref, v_ref, g_ref, h_ref):
        h = _norm_mod(x_ref[...], nw_ref[...], _mrow(mod_ref, SH1), _mrow(mod_ref, SC1))[0].astype(BF16)
        h_ref[...] = h
        q_ref[...] = _dotf(h, w_ref[:, 0:RNQ]).astype(BF16)
        k_ref[...] = (_dotf(h, w_ref[:, RNQ : 2 * RNQ]) * RET_KSCALE).astype(BF16)
        v_ref[...] = _dotf(h, w_ref[:, 2 * RNQ : 2 * RNQ + RNV]).astype(BF16)
        g_ref[...] = _dotf(h, w_ref[:, 2 * RNQ + RNV :]).astype(BF16)

    return _rowcall(
        "ret_in_fwd", body, nblk, nlb,
        [(x, "row"), (mod, "stream"), (nw1, "full"), (w_in, "full")],
        [((r, RNQ), BF16, "row"), ((r, RNQ), BF16, "row"), ((r, RNV), BF16, "row"), ((r, RNV), BF16, "row"), ((r, D), BF16, "row")],
    )


def _log_sigmoid(x):
    return jnp.minimum(x, 0.0) - jnp.log(1.0 + jnp.exp(-jnp.abs(x)))


def _ret_decays(lg, reverse):
    c = BM
    i = lax.broadcasted_iota(jnp.int32, (c, c), 0)
    j = lax.broadcasted_iota(jnp.int32, (c, c), 1)
    diff = (j - i) if reverse else (i - j)
    ediff = jnp.maximum(diff, 0).astype(F32)
    dm = jnp.where(diff >= 0, jnp.exp(ediff * lg), 0.0)
    rr = lax.broadcasted_iota(jnp.int32, (c, 1), 0).astype(F32)
    eq = (c - rr) if reverse else (rr + 1.0)
    ek = rr if reverse else (c - 1.0 - rr)
    return dm, ediff, jnp.exp(eq * lg), eq, jnp.exp(ek * lg), ek, jnp.exp(c * lg)


def _ret_chunk_index(nlb):
    return (lambda s: jnp.where(s == 0, nlb, s - 1)), (lambda s: jnp.where(s == 0, nlb, nlb - s))


def ret_scan_fwd(q, k, v, logit_b, nlb):
    r = q.shape[0]
    nb = nlb + 1
    fidx, bidx = _ret_chunk_index(nlb)

    def body(qf, kf, vf, qb, kb, vb, lg_ref, of_ref, ob_ref, rf_ref, rb_ref, stf, stb):
        s = pl.program_id(1)

        @pl.when(s == 0)
        def _():
            stf[...] = jnp.zeros(stf.shape, F32)
            stb[...] = jnp.zeros(stb.shape, F32)

        for d, (q_ref, k_ref, v_ref, o_ref, rs_ref, st) in enumerate(((qf, kf, vf, of_ref, rf_ref, stf), (qb, kb, vb, ob_ref, rb_ref, stb))):
            lg = _log_sigmoid(lg_ref[d, 0])[0:1, 0:1]
            dm, _, qd, _, kd, _, gc = _ret_decays(lg, d == 1)
            qq, kk, vv, st0 = q_ref[...], k_ref[...], v_ref[...], st[...]
            rs_ref[0, 0] = st0
            a = _dot_nt(qq, kk) * dm
            o = _dotf(a.astype(BF16), vv) + _dotf(qq, st0.astype(BF16)) * qd
            o_ref[...] = jnp.where(s == 0, 0.0, o)
            st[...] = st0 * gc + _dot_tn((kk.astype(F32) * kd).astype(BF16), vv)

    qspec = lambda f: pl.BlockSpec((BM, RET_DK), lambda h, s: (f(s), h))
    vspec = lambda f: pl.BlockSpec((BM, RET_DV), lambda h, s: (f(s), h))
    sspec = pl.BlockSpec((1, 1, RET_DK, RET_DV), lambda h, s: (h, s, 0, 0))
    return _pcall(
        body,
        name="ret_scan_fwd",
        grid=(RET_HEADS, nb),
        in_specs=[qspec(fidx), qspec(fidx), vspec(fidx), qspec(bidx), qspec(bidx), vspec(bidx),
                  pl.BlockSpec((2, 1, 8, 128), lambda h, s: (0, h, 0, 0))],
        out_specs=[vspec(fidx), vspec(bidx), sspec, sspec],
        out_shape=[jax.ShapeDtypeStruct((r, RNV), F32), jax.ShapeDtypeStruct((r, RNV), F32),
                   jax.ShapeDtypeStruct((RET_HEADS, nb, RET_DK, RET_DV), F32), jax.ShapeDtypeStruct((RET_HEADS, nb, RET_DK, RET_DV), F32)],
        scratch_shapes=[pltpu.VMEM((RET_DK, RET_DV), F32), pltpu.VMEM((RET_DK, RET_DV), F32)],
        compiler_params=pltpu.CompilerParams(dimension_semantics=("parallel", "arbitrary"), vmem_limit_bytes=VMEM_LIMIT),
    )(q, k, v, q, k, v, logit_b)


def _group_norm(o):
    mu = jnp.mean(o, axis=-1, keepdims=True)
    oc = o - mu
    rstd = lax.rsqrt(jnp.mean(oc * oc, axis=-1, keepdims=True) + EPS)
    return oc * rstd, rstd


def ret_out_fwd(o_f, o_b, g, gnw, w_out, x, mod, nblk, nlb):
    r = x.shape[0]

    def body(of_ref, ob_ref, g_ref, gn_ref, w_ref, x_ref, mod_ref, x1_ref, y_ref, z_ref):
        for hh in range(RET_HEADS):
            cs = slice(hh * RET_DV, (hh + 1) * RET_DV)
            yhat, _ = _group_norm(of_ref[:, cs] + ob_ref[:, cs])
            gg = g_ref[:, cs].astype(F32)
            z_ref[:, cs] = (gg * _sigmoid(gg) * (yhat * gn_ref[0:1, cs])).astype(BF16)
        y = _dotf(z_ref[...], w_ref[...])
        y_ref[...] = y
        x1_ref[...] = x_ref[...] + _mrow(mod_ref, G1) * y

    return _rowcall(
        "ret_out_fwd", body, nblk, nlb,
        [(o_f, "row"), (o_b, "row"), (g, "row"), (gnw, "full"), (w_out, "full"), (x, "row"), (mod, "stream")],
        [((r, D), F32, "row"), ((r, D), F32, "row"), ((r, RNV), BF16, "row")],
    )


def ret_out_bwd(dx1, y, o_f, o_b, g, gnw, w_out, mod, nblk, nlb):
    r = dx1.shape[0]

    def body(d_ref, y_ref, of_ref, ob_ref, g_ref, gn_ref, w_ref, mod_ref, dy_ref, do_ref, dg_ref, dgn_ref, dmod_ref, dz):
        i = pl.program_id(0)
        _acc_init(i, dgn_ref, dmod_ref)
        d = d_ref[...]
        dmod_ref[i // nlb, G1 : G1 + 1, :] += jnp.sum(d * y_ref[...], axis=0, keepdims=True)
        dy = (_mrow(mod_ref, G1) * d).astype(BF16)
        dy_ref[...] = dy
        dz[...] = _dot_nt(dy, w_ref[...])
        for hh in range(RET_HEADS):
            cs = slice(hh * RET_DV, (hh + 1) * RET_DV)
            yhat, rstd = _group_norm(of_ref[:, cs] + ob_ref[:, cs])
            gg = g_ref[:, cs].astype(F32)
            sg = _sigmoid(gg)
            gn = gn_ref[0:1, cs]
            dzz = dz[:, cs]
            dg_ref[:, cs] = (dzz * (yhat * gn) * (sg * (1.0 + gg * (1.0 - sg)))).astype(BF16)
            dyn = dzz * (gg * sg)
            dgn_ref[0:1, cs] += jnp.sum(dyn * yhat, axis=0, keepdims=True)
            dyh = dyn * gn
            do = rstd * (dyh - jnp.mean(dyh, axis=-1, keepdims=True) - yhat * jnp.mean(dyh * yhat, axis=-1, keepdims=True))
            do_ref[:, cs] = do.astype(BF16)

    return _rowcall(
        "ret_out_bwd", body, nblk, nlb,
        [(dx1, "row"), (y, "row"), (o_f, "row"), (o_b, "row"), (g, "row"), (gnw, "full"), (w_out, "full"), (mod, "stream")],
        [((r, D), BF16, "row"), ((r, RNV), BF16, "row"), ((r, RNV), BF16, "row"), ((8, RNV), F32, "acc"), ((2, 8, D), F32, "acc")],
        scratch=[pltpu.VMEM((BM, RNV), F32)],
    )


def ret_scan_bwd(q, k, v, do, st_f, st_b, logit_b, nlb):
    r = q.shape[0]
    nb = nlb + 1
    fidx, bidx = _ret_chunk_index(nlb)
    step = lambda t: nb - 1 - t

    def body(qf, kf, vf, dof, rf, qb, kb, vb, dob, rb, lg_ref,
             dqf, dkf, dvf, dqb, dkb, dvb, dlg_ref, drf, drb):
        t = pl.program_id(1)
        s = step(t)

        @pl.when(t == 0)
        def _():
            drf[...] = jnp.zeros(drf.shape, F32)
            drb[...] = jnp.zeros(drb.shape, F32)
            dlg_ref[...] = jnp.zeros(dlg_ref.shape, F32)

        dirs = ((qf, kf, vf, dof, rf, dqf, dkf, dvf, drf), (qb, kb, vb, dob, rb, dqb, dkb, dvb, drb))
        for d, (q_ref, k_ref, v_ref, do_ref, rs_ref, dq_ref, dk_ref, dv_ref, dr) in enumerate(dirs):
            lg = _log_sigmoid(lg_ref[d, 0])[0:1, 0:1]
            dm, ediff, qd, eq, kd, ek, gc = _ret_decays(lg, d == 1)
            qq, kk, vv = q_ref[...], k_ref[...], v_ref[...]
            dob16 = jnp.where(s == 0, jnp.zeros((), BF16), do_ref[...])
            do32 = dob16.astype(F32)
            st0 = rs_ref[0, 0]
            st16 = st0.astype(BF16)
            dr0 = dr[...]
            dr16 = dr0.astype(BF16)
            a = _dot_nt(qq, kk) * dm
            daf = _dot_nt(dob16, vv)
            ds = (daf * dm).astype(BF16)
            qr = _dotf(qq, st16)
            k32 = kk.astype(F32)
            kdec = (k32 * kd).astype(BF16)
            dv_ref[...] = (_dot_tn(a.astype(BF16), dob16) + _dotf(kdec, dr16)).astype(BF16)
            dq_ref[...] = (_dotf(ds, kk) + _dot_nt(dob16, st16) * qd).astype(BF16)
            vdr = _dot_nt(vv, dr16)
            dk_ref[...] = (_dot_tn(ds, qq) + vdr * kd).astype(BF16)
            tot = (jnp.sum(daf * a * ediff)
                   + jnp.sum(eq * qd * jnp.sum(do32 * qr, axis=-1, keepdims=True))
                   + jnp.sum(ek * kd * jnp.sum(k32 * vdr, axis=-1, keepdims=True))
                   + jnp.sum(BM * gc * jnp.sum(dr0 * st0, axis=-1, keepdims=True)))
            dlg_ref[d, 0] += tot
            dr[...] = gc * dr0 + _dot_tn(qq, (do32 * qd).astype(BF16))

        @pl.when(t == nb - 1)
        def _():
            dlg_ref[...] = dlg_ref[...] * _sigmoid(-lg_ref[...])

    qspec = lambda f: pl.BlockSpec((BM, RET_DK), lambda h, t: (f(step(t)), h))
    vspec = lambda f: pl.BlockSpec((BM, RET_DV), lambda h, t: (f(step(t)), h))
    sspec = pl.BlockSpec((1, 1, RET_DK, RET_DV), lambda h, t: (h, step(t), 0, 0))
    lspec = pl.BlockSpec((2, 1, 8, 128), lambda h, t: (0, h, 0, 0))
    sq, sv = jax.ShapeDtypeStruct((r, RNQ), BF16), jax.ShapeDtypeStruct((r, RNV), BF16)
    return _pcall(
        body,
        name="ret_scan_bwd",
        grid=(RET_HEADS, nb),
        in_specs=[qspec(fidx), qspec(fidx), vspec(fidx), vspec(fidx), sspec,
                  qspec(bidx), qspec(bidx), vspec(bidx), vspec(bidx), sspec, lspec],
        out_specs=[qspec(fidx), qspec(fidx), vspec(fidx), qspec(bidx), qspec(bidx), vspec(bidx), lspec],
        out_shape=[sq, sq, sv, sq, sq, sv, jax.ShapeDtypeStruct((2, RET_HEADS, 8, 128), F32)],
        scratch_shapes=[pltpu.VMEM((RET_DK, RET_DV), F32), pltpu.VMEM((RET_DK, RET_DV), F32)],
        compiler_params=pltpu.CompilerParams(dimension_semantics=("parallel", "arbitrary"), vmem_limit_bytes=VMEM_LIMIT),
    )(q, k, v, do, st_f, q, k, v, do, st_b, logit_b)


def ret_in_bwd(dqf, dkf, dvf, dqb, dkb, dvb, dgate, w_in, x, dx1, mod, nw1, nblk, nlb):
    r = x.shape[0]
    nin = 2 * RNQ + 2 * RNV

    def body(dqf_ref, dkf_ref, dvf_ref, dqb_ref, dkb_ref, dvb_ref, dg_ref, w_ref, x_ref, dx1_ref, mod_ref, nw_ref,
             dx_ref, din_ref, dnw_ref, dmod_ref):
        i = pl.program_id(0)
        _acc_init(i, dnw_ref, dmod_ref)
        is_ctx = i == nlb
        din_ref[:, 0:RNQ] = (dqf_ref[...].astype(F32) + dqb_ref[...].astype(F32)).astype(BF16)
        din_ref[:, RNQ : 2 * RNQ] = ((dkf_ref[...].astype(F32) + dkb_ref[...].astype(F32)) * RET_KSCALE).astype(BF16)
        din_ref[:, 2 * RNQ : 2 * RNQ + RNV] = (dvf_ref[...].astype(F32) + dvb_ref[...].astype(F32)).astype(BF16)
        din_ref[:, 2 * RNQ + RNV :] = jnp.where(is_ctx, jnp.zeros((), BF16), dg_ref[...])
        dh = _dot_nt(din_ref[...], w_ref[...])
        sc, nw = _mrow(mod_ref, SC1), nw_ref[...]
        _, n, xhat, rr = _norm_mod(x_ref[...], nw, _mrow(mod_ref, SH1), sc)
        dxn, dsh, dsc, dnw = _norm_mod_bwd(dh, n, xhat, rr, nw, sc)
        dx_ref[...] = jnp.where(is_ctx, 0.0, dx1_ref[...]) + dxn
        s = i // nlb
        dmod_ref[s, SH1 : SH1 + 1, :] += dsh
        dmod_ref[s, SC1 : SC1 + 1, :] += dsc
        dnw_ref[0:1, :] += dnw

    return _rowcall(
        "ret_in_bwd", body, nblk, nlb,
        [(dqf, "row"), (dkf, "row"), (dvf, "row"), (dqb, "row"), (dkb, "row"), (dvb, "row"), (dgate, "row"),
         (w_in, "full"), (x, "row"), (dx1, "row"), (mod, "stream"), (nw1, "full")],
        [((r, D), F32, "row"), ((r, nin), BF16, "row"), ((8, D), F32, "acc"), ((2, 8, D), F32, "acc")],
    )


def loss_head(xout, target, nlb):
    r = xout.shape[0]

    def body(x_ref, t_ref, dx_ref, l_ref):
        _acc_init(pl.program_id(0), l_ref)
        err = x_ref[...] - t_ref[...]
        dx_ref[...] = err * (1.0 / D)
        l_ref[...] += 0.5 * jnp.sum(jnp.mean(err * err, axis=-1, keepdims=True))

    return _rowcall(
        "loss_head", body, nlb, nlb,
        [(xout, "row"), (target, "row")],
        [((r, D), F32, "row"), ((8, 128), F32, "acc")],
    )


N_MIXERS = 3
POOL, ATTN, RET = range(3)


def _layer_plan(depth):
    plan = []
    for i in range(depth):
        kind = i % N_MIXERS
        ctx_out = any(k % N_MIXERS != POOL for k in range(i + 1, depth))
        plan.append((kind, i // N_MIXERS, ctx_out or kind != POOL, ctx_out))
    return plan


def local_step(xs, target, mods, w, nlb, depth, fetch, emit):
    nb = nlb + 1
    plan = _layer_plan(depth)
    saved = []
    x = xs
    for i, (kind, j, ctx_in, ctx_out) in enumerate(plan):
        nmix = nb if ctx_out else nlb
        mod, nw1, nw2 = mods[i], w["nw"][i, 0:1], w["nw"][i, 1:2]
        lw = fetch(i, x)
        sv = {"x": x, "lw": lw}
        if kind == POOL:
            x1, sv["ypre"] = pool_fwd(x, mod, nw1, lw["pool_w"], (w["pbs"], j), nmix, nlb)
        elif kind == ATTN:
            assert ctx_out
            sv["raw"], sv["q"], sv["k"], sv["v"], sv["h"] = qkv_fwd(x, mod, nw1, lw["attn_w_qkv"], w["gains"], w["cos"], w["sin"], nb, nlb)
            sv["o"], sv["lse"] = attn_fwd(sv["q"], sv["k"], sv["v"], nb, nlb)
            x1, sv["y"] = attn_out_fwd(sv["o"], lw["attn_w_o"], x, mod, nb, nlb)
        else:
            assert ctx_in and not ctx_out
            sv["q"], sv["k"], sv["v"], sv["g"], sv["h"] = ret_in_fwd(x, mod, nw1, lw["ret_w_in"], nb, nlb)
            sv["o_f"], sv["o_b"], sv["st_f"], sv["st_b"] = ret_scan_fwd(sv["q"], sv["k"], sv["v"], w["logit_b"], nlb)
            x1, sv["y"], sv["z"] = ret_out_fwd(sv["o_f"], sv["o_b"], sv["g"], w["gnw"], lw["ret_w_out"], x, mod, nlb, nlb)
        sv["x1"] = x1
        sv["u"], sv["h2"] = ffn_up(x1, mod, nw2, lw["ffn_w_up"], nmix, nlb)
        x, sv["f"], sv["uc"], sv["gated"] = ffn_down(sv["u"], (w["cw"], i), lw["ffn_w_down"], x1, mod, nmix, nlb)
        saved.append(sv)

    dx, loss_tile = loss_head(x, target, nlb)
    g = {k: [None] * depth for k in ("dcw", "dnw1", "dnw2", "dmod")}
    dep = loss_tile
    for i in reversed(range(depth)):
        kind, j, ctx_in, ctx_out = plan[i]
        sv = saved[i]
        lw, big = sv["lw"], {}
        nmix = nb if ctx_out else nlb
        mod, nw1, nw2 = mods[i], w["nw"][i, 0:1], w["nw"][i, 1:2]
        df, duc, dcb, dmod = ffn_bwd1(dx, sv["f"], sv["uc"], lw["ffn_w_down"], mod, dep, nmix, nlb)
        big["ffn_w_down"] = matmul_tn(sv["gated"], df, nmix)
        du, dx1, dcw, g["dnw2"][i], dm = ffn_bwd3(duc, sv["u"], (w["cw"], i), lw["ffn_w_up"], sv["x1"], dx, mod, nw2, nmix, nlb)
        g["dcw"][i] = dcw + dcb
        dmod = dmod + dm
        big["ffn_w_up"] = matmul_tn(sv["h2"], du, nmix)
        if kind == POOL:
            dx, dpw, dpbs, g["dnw1"][i], dm = pool_bwd(dx1, sv["x"], sv["ypre"], mod, nw1, lw["pool_w"], (w["pbs"], j), nmix, nlb)
            big["pool_w"] = dpw.astype(BF16)
            g.setdefault("dpbs", {})[j] = dpbs
        elif kind == ATTN:
            dy, do, dm1 = mix_out_bwd("attn_out_bwd", dx1, sv["y"], lw["attn_w_o"], mod, nb, nlb)
            big["attn_w_o"] = matmul_tn(sv["o"], dy, nb)
            dq, dk, dv = attn_bwd(sv["q"], sv["k"], sv["v"], sv["o"], do, sv["lse"], nb, nlb)
            dx, draw, g["dgains"], g["dnw1"][i], dm = qkv_bwd(
                dq, dk, dv, sv["raw"], w["gains"], w["cos"], w["sin"], lw["attn_w_qkv"], sv["x"], dx1, mod, nw1, nb, nlb, False)
            big["attn_w_qkv"] = matmul_tn(sv["h"], draw, nb)
            dm = dm + dm1
        else:
            dy, do, dgate, g["dgnw"], dm1 = ret_out_bwd(dx1, sv["y"], sv["o_f"], sv["o_b"], sv["g"], w["gnw"], lw["ret_w_out"], mod, nlb, nlb)
            big["ret_w_out"] = matmul_tn(sv["z"], dy, nlb)
            dqf, dkf, dvf, dqb, dkb, dvb, g["dlogit"] = ret_scan_bwd(sv["q"], sv["k"], sv["v"], do, sv["st_f"], sv["st_b"], w["logit_b"], nlb)
            dx, din, g["dnw1"][i], dm = ret_in_bwd(dqf, dkf, dvf, dqb, dkb, dvb, dgate, lw["ret_w_in"], sv["x"], dx1, mod, nw1, nb, nlb)
            big["ret_w_in"] = matmul_tn(sv["h"], din, nb)
            dm = dm + dm1
        g["dmod"][i] = dmod + dm
        dep = emit(i, big)
    return loss_tile, dx, g


MESH_ID = pl.DeviceIdType.MESH
CHIP_FLIPS = ((1, 0), (0, 1), (1, 1))


def _pos():
    return lax.axis_index("x"), lax.axis_index("y"), lax.axis_index("c")


def _flip(v, b):
    return 1 - v if b else v


def small_all_gather(name, x):
    rows, n = x.shape

    def body(x_ref, out_ref, send_sems, recv_sems, local_sem):
        mx, my, mc = _pos()
        me = 4 * mx + 2 * my + mc
        mine = pltpu.make_async_copy(x_ref, out_ref.at[me], local_sem)
        mine.start()
        sends, peers = [], []
        for kk in range(1, N_DEV):
            peer = (_flip(mx, (kk >> 2) & 1), _flip(my, (kk >> 1) & 1), _flip(mc, kk & 1))
            cp = pltpu.make_async_remote_copy(src_ref=x_ref, dst_ref=out_ref.at[me], send_sem=send_sems.at[kk - 1],
                                              recv_sem=recv_sems.at[kk - 1], device_id=peer, device_id_type=MESH_ID)
            cp.start()
            sends.append(cp)
            peers.append(peer)
        for kk, peer in enumerate(peers):
            pidx = 4 * peer[0] + 2 * peer[1] + peer[2]
            pltpu.make_async_remote_copy(src_ref=x_ref, dst_ref=out_ref.at[pidx], send_sem=send_sems.at[kk],
                                         recv_sem=recv_sems.at[kk], device_id=peer, device_id_type=MESH_ID).wait_recv()
        for cp in sends:
            cp.wait_send()
        mine.wait()

    return _pcall(
        body,
        name=name,
        out_shape=jax.ShapeDtypeStruct((N_DEV, rows, n), x.dtype),
        in_specs=[pl.BlockSpec(memory_space=pltpu.VMEM)],
        out_specs=pl.BlockSpec(memory_space=pltpu.VMEM),
        scratch_shapes=[pltpu.SemaphoreType.DMA((N_DEV - 1,)), pltpu.SemaphoreType.DMA((N_DEV - 1,)), pltpu.SemaphoreType.DMA],
        compiler_params=pltpu.CompilerParams(vmem_limit_bytes=VMEM_LIMIT),
    )(x)


def _hbm_exchange(name, ins, out_shapes, plan):
    n_in = len(ins)
    probe_local, probe_remote = plan([None] * n_in, [None] * len(out_shapes), probe=True)

    def body(*refs):
        in_refs, out_refs = refs[:n_in], refs[n_in : n_in + len(out_shapes)]
        send_sems, recv_sems, local_sems = refs[n_in + len(out_shapes) :]
        local, remote = plan(in_refs, out_refs, probe=False)
        lcs = [pltpu.make_async_copy(s, d, local_sems.at[k]) for k, (s, d) in enumerate(local)]
        for cp in lcs:
            cp.start()
        rcs = []
        for k, (s, d, peer, _) in enumerate(remote):
            cp = pltpu.make_async_remote_copy(src_ref=s, dst_ref=d, send_sem=send_sems.at[k], recv_sem=recv_sems.at[k],
                                              device_id=peer, device_id_type=MESH_ID)
            cp.start()
            rcs.append(cp)
        for k, (s, _, peer, here) in enumerate(remote):
            pltpu.make_async_remote_copy(src_ref=s, dst_ref=here, send_sem=send_sems.at[k], recv_sem=recv_sems.at[k],
                                         device_id=peer, device_id_type=MESH_ID).wait_recv()
        for cp in rcs:
            cp.wait_send()
        for cp in lcs:
            cp.wait()

    return _pcall(
        body,
        name=name,
        out_shape=list(out_shapes),
        in_specs=[pl.BlockSpec(memory_space=pl.ANY)] * n_in,
        out_specs=[pl.BlockSpec(memory_space=pl.ANY)] * len(out_shapes),
        scratch_shapes=[pltpu.SemaphoreType.DMA((max(probe_remote, 1),)), pltpu.SemaphoreType.DMA((max(probe_remote, 1),)),
                        pltpu.SemaphoreType.DMA((max(probe_local, 1),))],
    )(*ins)


def _at_axis(ref, axis, start, size):
    return ref.at[tuple(pl.ds(start, size) if a == axis else slice(None) for a in range(len(ref.shape)))]


HBM_SPEC = pl.BlockSpec(memory_space=pltpu.HBM)
SEM_SPEC = pl.BlockSpec(memory_space=pltpu.SEMAPHORE)
SIDE_EFFECT = pltpu.SideEffectType.DATAFLOW_SIDE_EFFECTING


def _in_hbm(a):
    return pltpu.with_memory_space_constraint(a, pltpu.HBM)


def _copies_start(name, bufs, counts, plan, after=None):
    n, ng = len(bufs), len(counts)
    extra = [] if after is None else [after]
    n_in = n + len(extra)

    def body(*refs):
        sems = refs[n_in : n_in + 2 * ng]
        token = refs[n_in + 2 * ng + n]
        for gi, copies in enumerate(plan(refs[:n])):
            for k, (s, d, peer) in enumerate(copies):
                pltpu.make_async_remote_copy(src_ref=s, dst_ref=d, send_sem=sems[2 * gi].at[k], recv_sem=sems[2 * gi + 1].at[k],
                                             device_id=peer, device_id_type=MESH_ID).start()
        token[...] = jnp.zeros(token.shape, token.dtype)

    out = _pcall(
        body,
        name=name,
        out_shape=tuple(pltpu.SemaphoreType.DMA((c,)) for c in counts for _ in range(2))
        + tuple(pltpu.HBM(b.shape, b.dtype) for b in bufs) + (jax.ShapeDtypeStruct((8, 128), F32),),
        in_specs=(HBM_SPEC,) * n + (pl.BlockSpec(memory_space=pl.ANY),) * len(extra),
        out_specs=(SEM_SPEC,) * (2 * ng) + (HBM_SPEC,) * n + (pl.BlockSpec(memory_space=pltpu.VMEM),),
        input_output_aliases={i: 2 * ng + i for i in range(n)},
        compiler_params=pltpu.CompilerParams(has_side_effects=SIDE_EFFECT),
    )(*[_in_hbm(b) for b in bufs], *extra)
    return [(out[2 * g], out[2 * g + 1]) for g in range(ng)], list(out[2 * ng : 2 * ng + n]), out[2 * ng + n]


def _copies_wait(name, sems, bufs, plan, after):
    n, ng = len(bufs), len(sems)

    def body(*refs):
        for gi, copies in enumerate(plan(refs[:n])):
            for k, (s, d, peer) in enumerate(copies):
                cp = pltpu.make_async_remote_copy(src_ref=s, dst_ref=d, send_sem=refs[n + 2 * gi].at[k], recv_sem=refs[n + 2 * gi + 1].at[k],
                                                  device_id=peer, device_id_type=MESH_ID)
                cp.wait_send()
                cp.wait_recv()

    out = _pcall(
        body,
        name=name,
        out_shape=tuple(pltpu.HBM(b.shape, b.dtype) for b in bufs),
        in_specs=(HBM_SPEC,) * n + (SEM_SPEC,) * (2 * ng) + (pl.BlockSpec(memory_space=pl.ANY),),
        out_specs=(HBM_SPEC,) * n,
        input_output_aliases={i: i for i in range(n)},
        compiler_params=pltpu.CompilerParams(has_side_effects=SIDE_EFFECT),
    )(*bufs, *[s for pair in sems for s in pair], after)
    return list(out)


def _layer_matrices(depth):
    out = []
    for i, (kind, j, _, _) in enumerate(_layer_plan(depth)):
        mix = ([("pool_w", j, 1)], [("attn_w_qkv", j, 1), ("attn_w_o", j, 0)], [("ret_w_in", j, 1), ("ret_w_out", j, 0)])[kind]
        out.append(mix + [("ffn_w_up", i, 1), ("ffn_w_down", i, 0)])
    return out


def _peers(mx, my, mc):
    out = []
    for fx, fy in CHIP_FLIPS:
        px, py = _flip(mx, fx), _flip(my, fy)
        out.append(((px, py, mc), 2 * px + py))
    return out


def full_buffers(shards, names, layers):
    out = []
    for name, _, axis in [e for layer in layers for e in layer]:
        shp = list(shards[names.index(name)].shape[1:])
        shp[axis] *= N_CHIP
        out.append(lax.empty(tuple(shp), BF16))
    return out


def _gather_plan(names, layers, group, n_shards, here):
    def plan(refs):
        mx, my, mc = _pos()
        s_refs, f_refs = refs[:n_shards], refs[n_shards:]
        groups, k = [], 0
        for gi, layer in enumerate(layers):
            if group is not None and gi != group:
                continue
            copies = []
            for name, idx, axis in layer:
                src = s_refs[names.index(name)].at[idx]
                n = src.shape[axis]
                for peer, pchip in _peers(mx, my, mc) + [((mx, my, 1 - mc), 2 * mx + my)]:
                    at = pchip if here else 2 * mx + my
                    copies.append((src, _at_axis(f_refs[k], axis, at * n, n), peer))
                k += 1
            groups.append(copies)
        return groups

    return plan


def gather_start(shards, names, layers, fulls, after):
    counts = [len(layer) * (len(CHIP_FLIPS) + 1) for layer in layers]
    sems, bufs, _ = _copies_start("gather_start", list(shards) + list(fulls), counts,
                                  _gather_plan(names, layers, None, len(shards), False), after)
    return sems, bufs[: len(shards)], bufs[len(shards) :]


def gather_wait(g, sems_g, shards, names, layers, fulls_g, after):
    bufs = _copies_wait(f"gather_wait_{g}", [sems_g], list(shards) + list(fulls_g), _gather_plan(names, layers, g, len(shards), True), after)
    return bufs[: len(shards)], bufs[len(shards) :]


def _scatter_plan(layer_entries, n_grads, land_of):
    def plan(refs):
        mx, my, mc = _pos()
        groups, k = [], 0
        for layer in layer_entries:
            copies = []
            for name, idx, axis in layer:
                gref, land = refs[k], refs[n_grads + land_of(k, name)]
                n = gref.shape[axis] // N_CHIP
                for slot, (peer, pchip) in enumerate(_peers(mx, my, mc)):
                    copies.append((_at_axis(gref, axis, pchip * n, n), land.at[slot, idx], peer))
                k += 1
            groups.append(copies)
        return groups

    return plan


def scatter_start(i, layer, grads, lands):
    sems, bufs, token = _copies_start(f"scatter_start_{i}", list(grads) + list(lands), [len(layer) * len(CHIP_FLIPS)],
                                      _scatter_plan([layer], len(grads), lambda k, name: k))
    return sems[0], bufs[: len(grads)], bufs[len(grads) :], token


def scatter_wait(sems, layers, grads, names, lands, after):
    bufs = _copies_wait("scatter_wait", sems, list(grads) + list(lands),
                        _scatter_plan(layers, len(grads), lambda k, name: names.index(name)), after)
    return bufs[: len(grads)], bufs[len(grads) :]


def sibling_swap(parts):
    def plan(in_refs, out_refs, probe):
        if probe:
            return 0, len(parts)
        mx, my, mc = _pos()
        return [], [(s, o, (mx, my, 1 - mc), o) for s, o in zip(in_refs, out_refs)]

    return _hbm_exchange("sibling_swap", parts, [jax.ShapeDtypeStruct(p.shape, p.dtype) for p in parts], plan)


EW_ROWS = 256


def _ew_call(name, fn, ins, n_out):
    rows, cols = ins[0].shape[-2:]
    tr = EW_ROWS if rows % EW_ROWS == 0 else rows

    def body(*refs):
        outs = fn(*[r[...] for r in refs[: len(ins)]])
        for o_ref, o in zip(refs[len(ins) :], outs):
            o_ref[...] = o

    def spec(a):
        if a.ndim == 3:
            return pl.BlockSpec((a.shape[0], tr, cols), lambda i: (0, i, 0))
        return pl.BlockSpec((tr, cols), lambda i: (i, 0))

    return _pcall(
        body,
        name=name,
        grid=(rows // tr,),
        in_specs=[spec(a) for a in ins],
        out_specs=[pl.BlockSpec((tr, cols), lambda i: (i, 0))] * n_out,
        out_shape=[jax.ShapeDtypeStruct((rows, cols), F32)] * n_out,
        compiler_params=pltpu.CompilerParams(dimension_semantics=("parallel",), vmem_limit_bytes=VMEM_LIMIT),
    )(*ins)


def _adamw(w, g, m, v):
    m = ADAM_B1 * m + (1.0 - ADAM_B1) * g
    v = ADAM_B2 * v + (1.0 - ADAM_B2) * (g * g)
    m_hat = m / (1.0 - ADAM_B1 ** ADAM_STEP)
    v_hat = v / (1.0 - ADAM_B2 ** ADAM_STEP)
    return -ADAM_LR * (m_hat / (jnp.sqrt(v_hat) + ADAM_EPS) + ADAM_WD * w), m, v


def sum_slots(name, own, landing):
    def fn(o, l):
        acc = o.astype(F32)
        for k in range(l.shape[0]):
            acc = acc + l[k].astype(F32)
        return (acc,)

    return _ew_call(name, fn, [own, landing], 1)[0]


def adamw_pair(name, w, m, v, p, ps):
    def fn(w, m, v, p, ps):
        g = p + ps
        return (g,) + _adamw(w, g, m, v)

    return _ew_call(name, fn, [w, m, v, p, ps], 4)


def adamw_one(name, w, m, v, g):
    return _ew_call(name, lambda w, m, v, g: _adamw(w, g, m, v), [w, m, v, g], 3)


def reduce_devices(name, x):
    def fn(a):
        acc = a[0]
        for k in range(1, a.shape[0]):
            acc = acc + a[k]
        return (acc,)

    return _ew_call(name, fn, [x], 1)[0]


ADA_ROWS = 16
ADA_CTX = N_DEV


def ada_fwd(s9, ada_w, ada_b):
    depth, _, n = ada_w.shape

    def body(s_ref, w_ref, b_ref, o_ref):
        s = s_ref[...]
        o_ref[...] = _dotf((s * _sigmoid(s)).astype(BF16), w_ref[...].astype(BF16)) + b_ref[...]

    return _pcall(
        body,
        name="ada_fwd",
        grid=(depth,),
        in_specs=[pl.BlockSpec((ADA_ROWS, D), lambda i: (0, 0)), pl.BlockSpec((None, D, n), lambda i: (i, 0, 0)),
                  pl.BlockSpec((None, 1, n), lambda i: (i, 0, 0))],
        out_specs=pl.BlockSpec((None, ADA_ROWS, n), lambda i: (i, 0, 0)),
        out_shape=jax.ShapeDtypeStruct((depth, ADA_ROWS, n), F32),
        compiler_params=pltpu.CompilerParams(dimension_semantics=("arbitrary",), vmem_limit_bytes=VMEM_LIMIT),
    )(s9, ada_w, ada_b)


def ada_bwd(s9, dm, ada_w):
    depth, _, n = ada_w.shape

    def body(s_ref, dm_ref, w_ref, gw_ref, ds_ref):
        _acc_init(pl.program_id(0), ds_ref)
        s = s_ref[...]
        dmb = dm_ref[...].astype(BF16)
        gw_ref[...] = _dot_tn((s * _sigmoid(s)).astype(BF16), dmb)
        ds_ref[...] += _dot_nt(dmb, w_ref[...].astype(BF16))

    return _pcall(
        body,
        name="ada_bwd",
        grid=(depth,),
        in_specs=[pl.BlockSpec((ADA_ROWS, D), lambda i: (0, 0)), pl.BlockSpec((None, ADA_ROWS, n), lambda i: (i, 0, 0)),
                  pl.BlockSpec((None, D, n), lambda i: (i, 0, 0))],
        out_specs=[pl.BlockSpec((None, D, n), lambda i: (i, 0, 0)), pl.BlockSpec((ADA_ROWS, D), lambda i: (0, 0))],
        out_shape=[jax.ShapeDtypeStruct((depth, D, n), F32), jax.ShapeDtypeStruct((ADA_ROWS, D), F32)],
        compiler_params=pltpu.CompilerParams(dimension_semantics=("arbitrary",), vmem_limit_bytes=VMEM_LIMIT),
    )(s9, dm, ada_w)


def cctx_grad(parts, c_ctx):
    def body(p_ref, c_ref, o_ref):
        acc = p_ref[0, ADA_CTX : ADA_CTX + 1, :]
        for chip in range(1, N_CHIP):
            acc = acc + p_ref[2 * chip, ADA_CTX : ADA_CTX + 1, :]
        c = c_ref[...]
        sg = _sigmoid(c)
        o_ref[...] = acc * (sg * (1.0 + c * (1.0 - sg)))

    return _pcall(body, name="cctx_grad", out_shape=jax.ShapeDtypeStruct((1, D), F32))(parts, c_ctx)


def _pack(arrs):
    flat = jnp.concatenate([a.astype(F32).reshape(-1) for a in arrs])
    rows = -(-flat.shape[0] // (8 * D)) * 8
    return jnp.pad(flat, (0, rows * D - flat.shape[0])).reshape(rows, D)


def _unpack(slab, shapes):
    lead = slab.shape[:-2]
    flat = slab.reshape(lead + (-1,))
    out, off = [], 0
    for shp in shapes:
        n = 1
        for d in shp:
            n *= d
        out.append(flat[..., off : off + n].reshape(lead + tuple(shp)))
        off += n
    return out


def _unshard(per_dev, axis):
    return jnp.concatenate([per_dev[2 * chip] for chip in range(N_CHIP)], axis=axis)


BIG = (("pool_w", 1), ("attn_w_qkv", 1), ("attn_w_o", 0), ("ret_w_in", 1), ("ret_w_out", 0), ("ffn_w_up", 1), ("ffn_w_down", 0))
WEIGHTS = ("c_ctx", "ada_w", "ada_b", "norm_w", "pool_w", "pool_b", "pool_scale", "attn_w_qkv", "attn_q_gain", "attn_k_gain",
           "attn_w_o", "ret_w_in", "ret_decay_logit", "ret_gn_w", "ret_w_out", "ffn_w_up", "ffn_conv_w", "ffn_conv_b", "ffn_w_down")
SMALL = tuple(n for n in WEIGHTS if n != "ada_w" and n not in dict(BIG))
SMALL_SHARD_AXIS = {"norm_w": 2, "pool_b": 1, "pool_scale": 1, "ret_gn_w": 1, "ffn_conv_w": 2}


def kernel(x, c, ctx, c_ctx, ada_w, ada_b, norm_w, pool_w, pool_b, pool_scale, attn_w_qkv, attn_q_gain, attn_k_gain, attn_w_o, ret_w_in, ret_decay_logit, ret_gn_w, ret_w_out, ffn_w_up, ffn_conv_w, ffn_conv_b, ffn_w_down, loss_target, m_c_ctx, m_ada_w, m_ada_b, m_norm_w, m_pool_w, m_pool_b, m_pool_scale, m_attn_w_qkv, m_attn_q_gain, m_attn_k_gain, m_attn_w_o, m_ret_w_in, m_ret_decay_logit, m_ret_gn_w, m_ret_w_out, m_ffn_w_up, m_ffn_conv_w, m_ffn_conv_b, m_ffn_w_down, v_c_ctx, v_ada_w, v_ada_b, v_norm_w, v_pool_w, v_pool_b, v_pool_scale, v_attn_w_qkv, v_attn_q_gain, v_attn_k_gain, v_attn_w_o, v_ret_w_in, v_ret_decay_logit, v_ret_gn_w, v_ret_w_out, v_ffn_w_up, v_ffn_conv_w, v_ffn_conv_b, v_ffn_w_down):
    P = dict(zip(WEIGHTS, (c_ctx, ada_w, ada_b, norm_w, pool_w, pool_b, pool_scale, attn_w_qkv, attn_q_gain, attn_k_gain, attn_w_o,
                           ret_w_in, ret_decay_logit, ret_gn_w, ret_w_out, ffn_w_up, ffn_conv_w, ffn_conv_b, ffn_w_down)))
    M = dict(zip(WEIGHTS, (m_c_ctx, m_ada_w, m_ada_b, m_norm_w, m_pool_w, m_pool_b, m_pool_scale, m_attn_w_qkv, m_attn_q_gain,
                           m_attn_k_gain, m_attn_w_o, m_ret_w_in, m_ret_decay_logit, m_ret_gn_w, m_ret_w_out, m_ffn_w_up,
                           m_ffn_conv_w, m_ffn_conv_b, m_ffn_w_down)))
    V = dict(zip(WEIGHTS, (v_c_ctx, v_ada_w, v_ada_b, v_norm_w, v_pool_w, v_pool_b, v_pool_scale, v_attn_w_qkv, v_attn_q_gain,
                           v_attn_k_gain, v_attn_w_o, v_ret_w_in, v_ret_decay_logit, v_ret_gn_w, v_ret_w_out, v_ffn_w_up,
                           v_ffn_conv_w, v_ffn_conv_b, v_ffn_w_down)))
    depth, s, l = ada_w.shape[0], x.shape[1], ctx.shape[1]
    assert l == BM and s % BM == 0 and s % GRID_W == 0
    nlb = s // BM
    n_pool = pool_w.shape[0]
    mx, my, mc = _pos()
    chip, dev = 2 * mx + my, 4 * mx + 2 * my + mc
    nada = ada_w.shape[2]

    sharded = [n for n in SMALL if n in SMALL_SHARD_AXIS]
    got = small_all_gather("gather_small", _pack([c[0]] + [P[n] for n in sharded]))
    got = _unpack(got, [(D,)] + [P[n].shape for n in sharded])
    c_all = got[0]
    full = {n: _unshard(g_, SMALL_SHARD_AXIS[n]) for n, g_ in zip(sharded, got[1:])}

    s9 = jnp.concatenate([c_all, c_ctx[None, :], jnp.zeros((ADA_ROWS - N_DEV - 1, D), F32)], axis=0)
    ada_b_mine = lax.dynamic_slice_in_dim(ada_b, chip * nada, nada, axis=1)[:, None, :]
    mod_part = ada_fwd(s9, ada_w, ada_b_mine)
    mod_all = _unshard(small_all_gather("gather_mod", mod_part.reshape(depth * ADA_ROWS, nada)), 1).reshape(depth, ADA_ROWS, 6, D)
    mod_mine = lax.dynamic_index_in_dim(mod_all, dev, axis=1, keepdims=False)
    mods_all = jnp.pad(jnp.stack([mod_mine, mod_all[:, ADA_CTX]], axis=1), ((0, 0), (0, 0), (0, 2), (0, 0)))
    mods = [mods_all[i] for i in range(depth)]

    names = [n for n, _ in BIG]
    layers = _layer_matrices(depth)
    shards = [P[n].astype(BF16) for n in names]
    gsems, shards, fulls = gather_start(shards, names, layers, full_buffers(shards, names, layers), mods_all)
    first = [sum(len(layer) for layer in layers[:i]) for i in range(depth + 1)]
    flight = {"shards": shards}

    def fetch(i, after):
        flight["shards"], mats = gather_wait(i, gsems[i], flight["shards"], names, layers, fulls[first[i] : first[i + 1]], after)
        return {name: m for (name, _, _), m in zip(layers[i], mats)}

    lands = {n: lax.empty((len(CHIP_FLIPS),) + P[n].shape, BF16) for n in names}
    sent = {}

    def emit(i, big):
        lnames = [name for name, _, _ in layers[i]]
        sems, gl, ll, token = scatter_start(i, layers[i], [big[n] for n in lnames], [lands[n] for n in lnames])
        lands.update(zip(lnames, ll))
        sent[i] = (sems, gl)
        return token

    w = {
        "nw": full["norm_w"],
        "pbs": jnp.concatenate([full["pool_b"][:, None], full["pool_scale"][:, None], jnp.zeros((n_pool, 6, D), F32)], axis=1),
        "gains": jnp.concatenate([attn_q_gain, attn_k_gain, jnp.zeros((6, HD), F32)], axis=0),
        "gnw": full["ret_gn_w"],
        "logit_b": jnp.broadcast_to(ret_decay_logit[0][:, :, None, None], (2, RET_HEADS, 8, 128)),
        "cw": jnp.concatenate([full["ffn_conv_w"], ffn_conv_b[:, None, :], jnp.zeros((depth, 4, 2 * D_FF), F32)], axis=1),
    }
    w["cos"], w["sin"] = rope_tables(s, l)

    xs = jnp.concatenate([x[0], ctx[0]], axis=0)
    loss_tile, dxs, g = local_step(xs, loss_target[0], mods, w, nlb, depth, fetch, emit)
    loss = lax.psum(loss_tile[0, 0], MESH_AXES)
    grad_x = dxs[:s][None]

    small_shapes = [(depth, 2, 8, D), (depth, 2, D), (n_pool, 2, D), (2, HD), (2, RET_HEADS), (RNV,), (depth, 4, 2 * D_FF)]
    slab = _pack([
        jnp.stack(g["dmod"]),
        jnp.stack([jnp.stack([g["dnw1"][i][0], g["dnw2"][i][0]]) for i in range(depth)]),
        jnp.stack([g["dpbs"][j][0:2] for j in range(n_pool)]),
        g["dgains"][0:2], g["dlogit"][:, :, 0, 0], g["dgnw"][0], jnp.stack([g["dcw"][i][0:4] for i in range(depth)]),
    ])
    slabs = small_all_gather("gather_small_grads", slab)
    dmod_dev = _unpack(slabs, small_shapes[:1])[0]
    t_dmod, t_nw, t_pbs, t_gains, t_logit, t_gnw, t_cw = _unpack(reduce_devices("reduce_small_grads", slabs), small_shapes)

    def cols(a):
        return lax.dynamic_slice_in_dim(a, chip * nada, nada, axis=a.ndim - 1)

    dm_lat = jnp.swapaxes(cols(dmod_dev[:, :, 0, :6].reshape(N_DEV, depth, 6 * D)), 0, 1)
    dm_ctx = cols(t_dmod[:, 1, :6].reshape(depth, 1, 6 * D))
    dm = jnp.concatenate([dm_lat, dm_ctx, jnp.zeros((depth, ADA_ROWS - N_DEV - 1, nada), F32)], axis=1)
    g_ada_w, ds9 = ada_bwd(s9, dm, ada_w)
    g_c_ctx = cctx_grad(small_all_gather("gather_dcctx", ds9), c_ctx[None, :])[0]

    def mine(a, name):
        n = P[name].shape[SMALL_SHARD_AXIS[name]]
        return lax.dynamic_slice_in_dim(a, chip * n, n, axis=SMALL_SHARD_AXIS[name])

    G = {
        "c_ctx": g_c_ctx,
        "ada_b": (t_dmod[:, 0, :6] + t_dmod[:, 1, :6]).reshape(depth, 6 * D),
        "norm_w": mine(t_nw, "norm_w"),
        "pool_b": mine(t_pbs[:, 0], "pool_b"), "pool_scale": mine(t_pbs[:, 1], "pool_scale"),
        "attn_q_gain": t_gains[0:1], "attn_k_gain": t_gains[1:2],
        "ret_decay_logit": t_logit[None], "ret_gn_w": mine(t_gnw[None], "ret_gn_w"),
        "ffn_conv_w": mine(t_cw[:, 0:3], "ffn_conv_w"), "ffn_conv_b": t_cw[:, 3],
    }
    sw, sg, sm, sv = (_pack([d_[n] for n in SMALL]) for d_ in (P, G, M, V))
    outs = adamw_one("adamw_small", sw, sm, sv, sg)
    D_, NM, NV = ({n: a for n, a in zip(SMALL, _unpack(o, [P[n].shape for n in SMALL]))} for o in outs)

    flat2 = lambda a: a.reshape(-1, a.shape[-1])
    G["ada_w"] = g_ada_w
    o3 = adamw_one("adamw_ada", flat2(ada_w), flat2(M["ada_w"]), flat2(V["ada_w"]), flat2(g_ada_w))
    D_["ada_w"], NM["ada_w"], NV["ada_w"] = (o.reshape(ada_w.shape) for o in o3)

    sent_grads, landed = scatter_wait([sent[i][0] for i in range(depth)], layers, [a for i in range(depth) for a in sent[i][1]],
                                      names, [lands[n] for n in names], dxs)
    own = {n: [None] * P[n].shape[0] for n in names}
    for (name, idx, axis), a in zip([e for layer in layers for e in layer], sent_grads):
        n_ = a.shape[axis] // N_CHIP
        own[name][idx] = lax.dynamic_slice_in_dim(a, chip * n_, n_, axis=axis)
    partial = [sum_slots("sum_" + n, jnp.stack(own[n]).reshape(-1, lnd.shape[-1]), lnd.reshape(len(CHIP_FLIPS), -1, lnd.shape[-1]))
               for n, lnd in zip(names, landed)]
    theirs = sibling_swap(partial)
    for (n, _), p, ps in zip(BIG, partial, theirs):
        o4 = adamw_pair("adamw_" + n, flat2(P[n]), flat2(M[n]), flat2(V[n]), p, ps)
        G[n], D_[n], NM[n], NV[n] = (o.reshape(P[n].shape) for o in o4)

    return (loss, grad_x, *[G[n] for n in WEIGHTS], *[D_[n] for n in WEIGHTS], *[NM[n] for n in WEIGHTS], *[NV[n] for n in WEIGHTS])
```

```python
import functools

import jax
import jax.numpy as jnp
from jax import lax
from jax.experimental import pallas as pl
from jax.experimental.pallas import tpu as pltpu

F32 = jnp.float32
BF16 = jnp.bfloat16

D = 1024
BM = 256
EPS = 1e-6
POOL_WINDOWS = (2, 4, 8, 16)
POOL_GROUP = D // 4
ATTN_HEADS = 8
ATTN_KV_HEADS = 2
HD = D // ATTN_HEADS
ATTN_GROUP = ATTN_HEADS // ATTN_KV_HEADS
NQ = ATTN_HEADS * HD
NKV = ATTN_KV_HEADS * HD
GRID_W = 64
ROPE_THETA = 10000.0
RET_HEADS = 4
RET_DK = D // RET_HEADS
RET_DV = 2 * D // RET_HEADS
RNQ = RET_HEADS * RET_DK
RNV = RET_HEADS * RET_DV
D_FF = 2816
FF_CHUNK = 256
ADAM_LR, ADAM_B1, ADAM_B2, ADAM_EPS, ADAM_WD, ADAM_STEP = 0.001, 0.9, 0.999, 1e-08, 0.01, 10
HALO_F32 = 8
HALO_BF16 = 16
VMEM_LIMIT = 60 * 1024 * 1024

MESH_AXES = ("x", "y", "c")
N_DEV = 8
N_CHIP = 4


def _pcall(body, **kw):
    return pl.pallas_call(body, **kw)


def _spec(shape, kind, nlb, nblk):
    nd = len(shape)
    if kind == "row":
        return pl.BlockSpec((BM, shape[1]), lambda i: (i, 0))
    if kind == "row_lat":
        return pl.BlockSpec((BM, shape[1]), lambda i: (jnp.minimum(i, nlb - 1), 0))
    if kind == "full":
        return pl.BlockSpec(tuple(shape), lambda i: (0,) * nd, pipeline_mode=pl.Buffered(1))
    if isinstance(kind, tuple) and kind[0] == "fullat":
        return pl.BlockSpec((None,) + tuple(shape[1:]), lambda i: (kind[1],) + (0,) * (nd - 1), pipeline_mode=pl.Buffered(1))
    if kind == "acc":
        return pl.BlockSpec(tuple(shape), lambda i: (0,) * nd)
    if kind == "any":
        return pl.BlockSpec(memory_space=pl.ANY)
    if kind == "stream":
        return pl.BlockSpec((1,) + tuple(shape[1:]), lambda i: (i // nlb,) + (0,) * (nd - 1))
    if kind in ("prev8", "prev16"):
        hb = int(kind[4:])
        return pl.BlockSpec((hb, shape[1]), lambda i: (jnp.maximum(i * (BM // hb) - 1, 0), 0))
    if kind in ("next8", "next16"):
        hb = int(kind[4:])
        last = nblk * BM // hb - 1
        return pl.BlockSpec((hb, shape[1]), lambda i: (jnp.minimum((i + 1) * (BM // hb), last), 0))
    raise ValueError(kind)


def _rowcall(name, body, nblk, nlb, ins, outs, scratch=()):
    return _pcall(
        body,
        name=name,
        grid=(nblk,),
        in_specs=[_spec(a.shape, k, nlb, nblk) for a, k in ins],
        out_specs=[_spec(s, k, nlb, nblk) for s, _, k in outs],
        out_shape=[jax.ShapeDtypeStruct(s, d) for s, d, _ in outs],
        scratch_shapes=list(scratch),
        compiler_params=pltpu.CompilerParams(dimension_semantics=("arbitrary",), vmem_limit_bytes=VMEM_LIMIT),
    )(*[a for a, _ in ins])


def _wk(w):
    return (w[0], ("fullat", w[1])) if isinstance(w, tuple) else (w, "full")


def _stream_edges(i, nlb):
    is_ctx = i == nlb
    return (i == 0) | is_ctx, (i == nlb - 1) | is_ctx, is_ctx


def _dotf(a, b):
    return jnp.dot(a, b, preferred_element_type=F32)


def _dot_nt(a, b):
    return lax.dot_general(a, b, (((1,), (1,)), ((), ())), preferred_element_type=F32)


def _dot_tn(a, b):
    return lax.dot_general(a, b, (((0,), (0,)), ((), ())), preferred_element_type=F32)


def _sigmoid(x):
    return 0.5 * jnp.tanh(0.5 * x) + 0.5


def _norm_mod(x, nw, sh, sc):
    r = lax.rsqrt(jnp.mean(x * x, axis=-1, keepdims=True) + EPS)
    xhat = x * r
    n = xhat * nw
    return n * (1.0 + sc) + sh, n, xhat, r


def _norm_mod_bwd(dh, n, xhat, r, nw, sc):
    dsh = jnp.sum(dh, axis=0, keepdims=True)
    dsc = jnp.sum(dh * n, axis=0, keepdims=True)
    dn = dh * (1.0 + sc)
    dnw = jnp.sum(dn * xhat, axis=0, keepdims=True)
    dxhat = dn * nw
    dx = r * (dxhat - xhat * jnp.mean(dxhat * xhat, axis=-1, keepdims=True))
    return dx, dsh, dsc, dnw


def _acc_init(i, *refs):
    @pl.when(i == 0)
    def _():
        for r in refs:
            r[...] = jnp.zeros(r.shape, r.dtype)


SH1, SC1, G1, SH2, SC2, G2 = range(6)


def _mrow(mod_ref, k):
    return mod_ref[0, k : k + 1, :]


def _shift_rows(x_ref, xp_ref, xn_ref, cs, first, last):
    cur = x_ref[:, cs].astype(F32)
    rows = lax.broadcasted_iota(jnp.int32, cur.shape, 0)
    pr = jnp.where(first, 0.0, xp_ref[HALO_BF16 - 1 : HALO_BF16, cs].astype(F32))
    nx = jnp.where(last, 0.0, xn_ref[0:1, cs].astype(F32))
    dn = jnp.where(rows == 0, pr, pltpu.roll(cur, 1, 0))
    up = jnp.where(rows == BM - 1, nx, pltpu.roll(cur, BM - 1, 0))
    return dn, cur, up


def ffn_up(x1, mod, nw2, w_up, nblk, nlb):
    r = x1.shape[0]

    def body(x_ref, mod_ref, nw_ref, w_ref, u_ref, h_ref):
        h, _, _, _ = _norm_mod(x_ref[...], nw_ref[...], _mrow(mod_ref, SH2), _mrow(mod_ref, SC2))
        hb = h.astype(BF16)
        h_ref[...] = hb
        u_ref[...] = _dotf(hb, w_ref[...]).astype(BF16)

    return _rowcall(
        "ffn_up", body, nblk, nlb,
        [(x1, "row"), (mod, "stream"), (nw2, "full"), _wk(w_up)],
        [((r, 2 * D_FF), BF16, "row"), ((r, D), BF16, "row")],
    )


def _conv_gate_chunk(u_ref, up_ref, un_ref, cw_ref, j, first, last):
    res = []
    for half in range(2):
        c0 = half * D_FF + j * FF_CHUNK
        cs = slice(c0, c0 + FF_CHUNK)
        dn, cur, up = _shift_rows(u_ref, up_ref, un_ref, cs, first, last)
        val = dn * cw_ref[0:1, cs] + cur * cw_ref[1:2, cs] + up * cw_ref[2:3, cs] + cw_ref[3:4, cs]
        res.append((val, dn, cur, up, cs))
    return res


def ffn_down(u, cw, w_down, x1, mod, nblk, nlb):
    r = x1.shape[0]

    def body(u_ref, up_ref, un_ref, cw_ref, w_ref, x_ref, mod_ref, x2_ref, f_ref, uc_ref, gated_ref):
        first, last, _ = _stream_edges(pl.program_id(0), nlb)
        f = jnp.zeros((BM, D), F32)
        for j in range(D_FF // FF_CHUNK):
            (a, _, _, _, acs), (v, _, _, _, vcs) = _conv_gate_chunk(u_ref, up_ref, un_ref, cw_ref, j, first, last)
            uc_ref[:, acs] = a.astype(BF16)
            uc_ref[:, vcs] = v.astype(BF16)
            gated = (a * _sigmoid(a) * v).astype(BF16)
            gated_ref[:, acs] = gated
            f = f + _dotf(gated, w_ref[acs, :])
        f_ref[...] = f.astype(BF16)
        x2_ref[...] = x_ref[...] + _mrow(mod_ref, G2) * f

    return _rowcall(
        "ffn_down", body, nblk, nlb,
        [(u, "row"), (u, "prev16"), (u, "next16"), _wk(cw), _wk(w_down), (x1, "row"), (mod, "stream")],
        [((r, D), F32, "row"), ((r, D), BF16, "row"), ((r, 2 * D_FF), BF16, "row"), ((r, D_FF), BF16, "row")],
    )


def ffn_bwd1(dx2, f, uc, w_down, mod, dep, nblk, nlb):
    r = dx2.shape[0]

    def body(dx_ref, f_ref, uc_ref, w_ref, mod_ref, dep_ref, df_ref, duc_ref, dcb_ref, dmod_ref):
        i = pl.program_id(0)
        _acc_init(i, dcb_ref, dmod_ref)
        dx = dx_ref[...]
        df = (_mrow(mod_ref, G2) * dx).astype(BF16)
        df_ref[...] = df
        dmod_ref[i // nlb, G2 : G2 + 1, :] += jnp.sum(dx * f_ref[...].astype(F32), axis=0, keepdims=True)
        for j in range(D_FF // FF_CHUNK):
            acs = slice(j * FF_CHUNK, (j + 1) * FF_CHUNK)
            vcs = slice(D_FF + j * FF_CHUNK, D_FF + (j + 1) * FF_CHUNK)
            a, v = uc_ref[:, acs].astype(F32), uc_ref[:, vcs].astype(F32)
            sa = _sigmoid(a)
            dg = _dot_nt(df, w_ref[acs, :])
            for dval, cs in ((dg * v * (sa * (1.0 + a * (1.0 - sa))), acs), (dg * (a * sa), vcs)):
                duc_ref[:, cs] = dval.astype(BF16)
                dcb_ref[3:4, cs] += jnp.sum(dval, axis=0, keepdims=True)

    return _rowcall(
        "ffn_bwd1", body, nblk, nlb,
        [(dx2, "row"), (f, "row"), (uc, "row"), _wk(w_down), (mod, "stream"), (dep, "any")],
        [((r, D), BF16, "row"), ((r, 2 * D_FF), BF16, "row"), ((8, 2 * D_FF), F32, "acc"), ((2, 8, D), F32, "acc")],
    )


def ffn_bwd3(duc, u, cw, w_up, x1, dx2, mod, nw2, nblk, nlb):
    r = dx2.shape[0]

    def body(d_ref, dp_ref, dn_ref, u_ref, cw_ref, w_ref, x_ref, dx_ref, mod_ref, nw_ref, du_ref, dx1_ref, dcw_ref, dnw_ref, dmod_ref):
        i = pl.program_id(0)
        first, last, _ = _stream_edges(i, nlb)
        _acc_init(i, dcw_ref, dnw_ref, dmod_ref)
        dh = jnp.zeros((BM, D), F32)
        for j in range(2 * D_FF // FF_CHUNK):
            cs = slice(j * FF_CHUNK, (j + 1) * FF_CHUNK)
            dn, cur, up = _shift_rows(d_ref, dp_ref, dn_ref, cs, first, last)
            du = (up * cw_ref[0:1, cs] + cur * cw_ref[1:2, cs] + dn * cw_ref[2:3, cs]).astype(BF16)
            du_ref[:, cs] = du
            dh = dh + _dot_nt(du, w_ref[:, cs])
            uu = u_ref[:, cs].astype(F32)
            dcw_ref[0:1, cs] += jnp.sum(up * uu, axis=0, keepdims=True)
            dcw_ref[1:2, cs] += jnp.sum(cur * uu, axis=0, keepdims=True)
            dcw_ref[2:3, cs] += jnp.sum(dn * uu, axis=0, keepdims=True)
        sc = _mrow(mod_ref, SC2)
        nw = nw_ref[...]
        _, n, xhat, rr = _norm_mod(x_ref[...], nw, _mrow(mod_ref, SH2), sc)
        dxn, dsh, dsc, dnw = _norm_mod_bwd(dh, n, xhat, rr, nw, sc)
        dx1_ref[...] = dx_ref[...] + dxn
        s = i // nlb
        dmod_ref[s, SH2 : SH2 + 1, :] += dsh
        dmod_ref[s, SC2 : SC2 + 1, :] += dsc
        dnw_ref[0:1, :] += dnw

    return _rowcall(
        "ffn_bwd3", body, nblk, nlb,
        [(duc, "row"), (duc, "prev16"), (duc, "next16"), (u, "row"), _wk(cw), _wk(w_up), (x1, "row"), (dx2, "row"),
         (mod, "stream"), (nw2, "full")],
        [((r, 2 * D_FF), BF16, "row"), ((r, D), F32, "row"), ((8, 2 * D_FF), F32, "acc"), ((8, D), F32, "acc"),
         ((2, 8, D), F32, "acc")],
    )


def matmul_tn(a, b, nblk, tn=None):
    k, n = a.shape[1], b.shape[1]
    rows = nblk * BM
    tr = 768 if rows % 768 == 0 else (1024 if rows % 1024 == 0 else BM)
    if tn is None:
        tn = n
        while k * tn * 4 > 6 * 1024 * 1024 and tn % 256 == 0:
            tn //= 2
    steps = rows // tr

    def body(a_ref, b_ref, o_ref, acc):
        t = pl.program_id(1)

        @pl.when(t == 0)
        def _():
            acc[...] = jnp.zeros(acc.shape, acc.dtype)

        acc[...] += _dot_tn(a_ref[...], b_ref[...])

        @pl.when(t == steps - 1)
        def _():
            o_ref[...] = acc[...].astype(o_ref.dtype)

    return _pcall(
        body,
        name="matmul_tn",
        grid=(n // tn, steps),
        in_specs=[pl.BlockSpec((tr, k), lambda j, t: (t, 0)), pl.BlockSpec((tr, tn), lambda j, t: (t, j))],
        out_specs=pl.BlockSpec((k, tn), lambda j, t: (0, j)),
        out_shape=jax.ShapeDtypeStruct((k, n), BF16),
        scratch_shapes=[pltpu.VMEM((k, tn), F32)],
        compiler_params=pltpu.CompilerParams(dimension_semantics=("parallel", "arbitrary"), vmem_limit_bytes=VMEM_LIMIT),
    )(a, b)


EXT = BM + 2 * HALO_F32


def _pool_positions(i, nlb, nrows, row0):
    is_ctx = i == nlb
    t = (i - jnp.where(is_ctx, nlb, 0)) * BM + row0 + lax.broadcasted_iota(jnp.int32, (nrows, 1), 0)
    return t, jnp.where(is_ctx, BM, nlb * BM)


def _pool_cnt(t, win, slen):
    return (jnp.minimum(t + win // 2, slen) - jnp.maximum(t - win // 2, 0)).astype(F32)


def _pool_fill_ext(ext, i, nlb, x_ref, xp_ref, xn_ref, mod_ref, nw_ref):
    first, last, _ = _stream_edges(i, nlb)
    sh, sc, nw = _mrow(mod_ref, SH1), _mrow(mod_ref, SC1), nw_ref[...]
    hcur, n, xhat, r = _norm_mod(x_ref[...], nw, sh, sc)
    ext[0:HALO_F32, :] = jnp.where(first, 0.0, _norm_mod(xp_ref[...], nw, sh, sc)[0])
    ext[HALO_F32 : HALO_F32 + BM, :] = hcur
    ext[HALO_F32 + BM :, :] = jnp.where(last, 0.0, _norm_mod(xn_ref[...], nw, sh, sc)[0])
    return n, xhat, r


def _window_sum(ref, cols, offs):
    acc = None
    for o in offs:
        v = ref[HALO_F32 + o : HALO_F32 + o + BM, cols]
        acc = v if acc is None else acc + v
    return acc


def _pool_diff(ext, g, win, t, slen):
    cols = slice(g * POOL_GROUP, (g + 1) * POOL_GROUP)
    ssum = _window_sum(ext, cols, range(-(win // 2), win // 2))
    return ssum / _pool_cnt(t, win, slen) - ext[HALO_F32 : HALO_F32 + BM, cols]


def pool_fwd(x, mod, nw1, pw, pbs, nblk, nlb):
    r = x.shape[0]

    def body(x_ref, xp_ref, xn_ref, mod_ref, nw_ref, pw_ref, pbs_ref, x1_ref, ypre_ref, ext):
        i = pl.program_id(0)
        _pool_fill_ext(ext, i, nlb, x_ref, xp_ref, xn_ref, mod_ref, nw_ref)
        t, slen = _pool_positions(i, nlb, BM, 0)
        for g, win in enumerate(POOL_WINDOWS):
            cols = slice(g * POOL_GROUP, (g + 1) * POOL_GROUP)
            diff = _pool_diff(ext, g, win, t, slen)
            ypre = _dotf(diff.astype(BF16), pw_ref[g]) + pbs_ref[0:1, cols]
            ypre_ref[:, cols] = ypre
            x1_ref[:, cols] = x_ref[:, cols] + mod_ref[0, G1 : G1 + 1, cols] * (ypre * pbs_ref[1:2, cols])

    return _rowcall(
        "pool_fwd", body, nblk, nlb,
        [(x, "row"), (x, "prev8"), (x, "next8"), (mod, "stream"), (nw1, "full"), _wk(pw), _wk(pbs)],
        [((r, D), F32, "row"), ((r, D), F32, "row")],
        scratch=[pltpu.VMEM((EXT, D), F32)],
    )


def pool_bwd(dx1, x, ypre, mod, nw1, pw, pbs, dep, nblk, nlb, lat_dx):
    r = x.shape[0]

    def body(d_ref, dp_ref, dn_ref, x_ref, xp_ref, xn_ref, ypre_ref, mod_ref, nw_ref, pw_ref, pbs_ref, dep_ref,
             dx_ref, dpw_ref, dpbs_ref, dnw_ref, dmod_ref, ext, dext, eext, dh):
        i = pl.program_id(0)
        first, last, _ = _stream_edges(i, nlb)
        _acc_init(i, dpw_ref, dpbs_ref, dnw_ref, dmod_ref)
        n, xhat, rr = _pool_fill_ext(ext, i, nlb, x_ref, xp_ref, xn_ref, mod_ref, nw_ref)
        g1, scale = _mrow(mod_ref, G1), pbs_ref[1:2, :]
        dcur = d_ref[...]
        ypre = ypre_ref[...]
        s = i // nlb
        dmod_ref[s, G1 : G1 + 1, :] += jnp.sum(dcur * (ypre * scale), axis=0, keepdims=True)
        dy = g1 * dcur
        dpbs_ref[1:2, :] += jnp.sum(dy * ypre, axis=0, keepdims=True)
        dpbs_ref[0:1, :] += jnp.sum(dy * scale, axis=0, keepdims=True)
        gs = g1 * scale
        dext[0:HALO_F32, :] = jnp.where(first, 0.0, gs * dp_ref[...])
        dext[HALO_F32 : HALO_F32 + BM, :] = dy * scale
        dext[HALO_F32 + BM :, :] = jnp.where(last, 0.0, gs * dn_ref[...])
        t, slen = _pool_positions(i, nlb, BM, 0)
        text, _ = _pool_positions(i, nlb, EXT, -HALO_F32)
        for g, win in enumerate(POOL_WINDOWS):
            cols = slice(g * POOL_GROUP, (g + 1) * POOL_GROUP)
            diff = _pool_diff(ext, g, win, t, slen)
            dpre = dext[:, cols].astype(BF16)
            ddiff = _dot_nt(dpre, pw_ref[g])
            eext[...] = ddiff / jnp.maximum(_pool_cnt(text, win, slen), 1.0)
            dh[:, cols] = _window_sum(eext, slice(None), range(-(win // 2) + 1, win // 2 + 1)) - ddiff[HALO_F32 : HALO_F32 + BM, :]
            dpw_ref[g] += _dot_tn(diff.astype(BF16), dpre[HALO_F32 : HALO_F32 + BM, :])
        sc, nw = _mrow(mod_ref, SC1), nw_ref[...]
        dxn, dsh, dsc, dnw = _norm_mod_bwd(dh[...], n, xhat, rr, nw, sc)
        if lat_dx:
            @pl.when(i < nlb)
            def _():
                dx_ref[...] = dcur + dxn
        else:
            dx_ref[...] = dcur + dxn
        dmod_ref[s, SH1 : SH1 + 1, :] += dsh
        dmod_ref[s, SC1 : SC1 + 1, :] += dsc
        dnw_ref[0:1, :] += dnw

    return _rowcall(
        "pool_bwd", body, nblk, nlb,
        [(dx1, "row"), (dx1, "prev8"), (dx1, "next8"), (x, "row"), (x, "prev8"), (x, "next8"), (ypre, "row"),
         (mod, "stream"), (nw1, "full"), _wk(pw), _wk(pbs), (dep, "any")],
        [((nlb * BM, D), F32, "row_lat") if lat_dx else ((r, D), F32, "row"), ((4, POOL_GROUP, POOL_GROUP), F32, "acc"), ((8, D), F32, "acc"), ((8, D), F32, "acc"),
         ((2, 8, D), F32, "acc")],
        scratch=[pltpu.VMEM((EXT, D), F32), pltpu.VMEM((EXT, D), F32), pltpu.VMEM((EXT, POOL_GROUP), F32), pltpu.VMEM((BM, D), F32)],
    )


def rope_tables(s, l):
    rows = s // GRID_W
    row = jnp.broadcast_to(jnp.arange(rows)[:, None], (rows, GRID_W)).reshape(-1).astype(F32)
    col = jnp.broadcast_to(jnp.arange(GRID_W)[None, :], (rows, GRID_W)).reshape(-1).astype(F32)
    axis_dim = HD // 2
    inv = ROPE_THETA ** (-jnp.arange(0, axis_dim, 2, dtype=F32) / axis_dim)
    ar, ac = row[:, None] * inv, col[:, None] * inv
    cos = jnp.concatenate([jnp.cos(ar), jnp.cos(ar), jnp.cos(ac), jnp.cos(ac)], axis=-1)
    sin = jnp.concatenate([-jnp.sin(ar), jnp.sin(ar), -jnp.sin(ac), jnp.sin(ac)], axis=-1)
    return (jnp.concatenate([cos, jnp.ones((l, HD), F32)], axis=0), jnp.concatenate([sin, jnp.zeros((l, HD), F32)], axis=0))


def _partner(x):
    q = HD // 4
    lane = lax.broadcasted_iota(jnp.int32, x.shape, 1)
    return jnp.where((lane // q) % 2 == 0, pltpu.roll(x, HD - q, 1), pltpu.roll(x, q, 1))


def _head_norm(raw, gain):
    r = lax.rsqrt(jnp.mean(raw * raw, axis=-1, keepdims=True) + EPS)
    return raw * r, r


ATTN_SCALE = HD ** -0.5


def qkv_fwd(x, mod, nw1, w_qkv, gains, cos_t, sin_t, nblk, nlb):
    r = x.shape[0]

    def body(x_ref, mod_ref, nw_ref, w_ref, g_ref, c_ref, s_ref, raw_ref, q_ref, k_ref, v_ref, h_ref):
        h = _norm_mod(x_ref[...], nw_ref[...], _mrow(mod_ref, SH1), _mrow(mod_ref, SC1))[0].astype(BF16)
        h_ref[...] = h
        raw_ref[...] = _dotf(h, w_ref[...])
        cos, sin = c_ref[...], s_ref[...]
        for j in range(ATTN_HEADS + ATTN_KV_HEADS):
            isq = j < ATTN_HEADS
            xn = _head_norm(raw_ref[:, j * HD : (j + 1) * HD], None)[0] * (g_ref[0:1, :] if isq else g_ref[1:2, :])
            rot = xn * cos + _partner(xn) * sin
            if isq:
                rot = rot * ATTN_SCALE
            rot = rot.astype(BF16)
            if isq:
                q_ref[:, j * HD : (j + 1) * HD] = rot
            else:
                k_ref[:, (j - ATTN_HEADS) * HD : (j - ATTN_HEADS + 1) * HD] = rot
        v_ref[...] = raw_ref[:, NQ + NKV :].astype(BF16)

    return _rowcall(
        "qkv_fwd", body, nblk, nlb,
        [(x, "row"), (mod, "stream"), (nw1, "full"), (w_qkv, "full"), (gains, "full"), (cos_t, "row"), (sin_t, "row")],
        [((r, NQ + 2 * NKV), F32, "row"), ((r, NQ), BF16, "row"), ((r, NKV), BF16, "row"), ((r, NKV), BF16, "row"),
         ((r, D), BF16, "row")],
    )


def attn_fwd(q, k, v, nblk, nlb):
    r = q.shape[0]

    def body(q_ref, k_ref, v_ref, o_ref, lse_ref):
        is_ctx = pl.program_id(0) == nlb
        hide = is_ctx & (lax.broadcasted_iota(jnp.int32, (1, r), 1) < nlb * BM)
        for kvh in range(ATTN_KV_HEADS):
            kh = k_ref[:, kvh * HD : (kvh + 1) * HD]
            vh = v_ref[:, kvh * HD : (kvh + 1) * HD]
            for g in range(ATTN_GROUP):
                cs = slice((kvh * ATTN_GROUP + g) * HD, (kvh * ATTN_GROUP + g + 1) * HD)
                s = jnp.where(hide, -1e30, _dot_nt(q_ref[:, cs], kh))
                m = jnp.max(s, axis=-1, keepdims=True)
                p = jnp.exp(s - m)
                l = jnp.sum(p, axis=-1, keepdims=True)
                o_ref[:, cs] = (_dotf(p.astype(BF16), vh) / l).astype(BF16)
                j = kvh * ATTN_GROUP + g
                lse_ref[:, j : j + 1] = m + jnp.log(l)

    return _rowcall(
        "attn_fwd", body, nblk, nlb,
        [(q, "row"), (k, "full"), (v, "full")],
        [((r, NQ), BF16, "row"), ((r, ATTN_HEADS), F32, "row")],
    )


def attn_out_fwd(o, w_o, x, mod, nblk, nlb):
    r = x.shape[0]

    def body(o_ref, w_ref, x_ref, mod_ref, x1_ref, y_ref):
        y = _dotf(o_ref[...], w_ref[...])
        y_ref[...] = y
        x1_ref[...] = x_ref[...] + _mrow(mod_ref, G1) * y

    return _rowcall(
        "attn_out_fwd", body, nblk, nlb,
        [(o, "row"), (w_o, "full"), (x, "row"), (mod, "stream")],
        [((r, D), F32, "row"), ((r, D), F32, "row")],
    )


def mix_out_bwd(name, dx1, y, w_out, mod, dep, nblk, nlb):
    r = dx1.shape[0]
    kin = w_out.shape[0]

    def body(d_ref, y_ref, w_ref, mod_ref, dep_ref, dy_ref, do_ref, dmod_ref):
        i = pl.program_id(0)
        _acc_init(i, dmod_ref)
        d = d_ref[...]
        dmod_ref[i // nlb, G1 : G1 + 1, :] += jnp.sum(d * y_ref[...], axis=0, keepdims=True)
        dy = (_mrow(mod_ref, G1) * d).astype(BF16)
        dy_ref[...] = dy
        do_ref[...] = _dot_nt(dy, w_ref[...]).astype(do_ref.dtype)

    return _rowcall(
        name, body, nblk, nlb,
        [(dx1, "row"), (y, "row"), (w_out, "full"), (mod, "stream"), (dep, "any")],
        [((r, D), BF16, "row"), ((r, kin), BF16, "row"), ((2, 8, D), F32, "acc")],
    )


ATTN_KCHUNK = 11 * BM


def attn_bwd(q, k, v, o, do, lse, nblk, nlb):
    r = q.shape[0]
    kc = ATTN_KCHUNK if r % ATTN_KCHUNK == 0 else BM
    nkc = r // kc

    def body(q_ref, k_ref, v_ref, o_ref, do_ref, lse_ref, dq_ref, dk_ref, dv_ref):
        i = pl.program_id(0)
        _acc_init(i, dk_ref, dv_ref)
        is_ctx = i == nlb
        for kvh in range(ATTN_KV_HEADS):
            ks = slice(kvh * HD, (kvh + 1) * HD)
            for g in range(ATTN_GROUP):
                j = kvh * ATTN_GROUP + g
                cs = slice(j * HD, (j + 1) * HD)
                qh, doh = q_ref[:, cs], do_ref[:, cs]
                delta = jnp.sum(doh.astype(F32) * o_ref[:, cs].astype(F32), axis=-1, keepdims=True)
                lse = lse_ref[:, j : j + 1]
                dq = jnp.zeros((BM, HD), F32)
                for c in range(nkc):
                    rs = slice(c * kc, (c + 1) * kc)
                    kh, vh = k_ref[rs, ks], v_ref[rs, ks]
                    hide = is_ctx & (c * kc + lax.broadcasted_iota(jnp.int32, (1, kc), 1) < nlb * BM)
                    p = jnp.where(hide, 0.0, jnp.exp(_dot_nt(qh, kh) - lse))
                    ds = (p * (_dot_nt(doh, vh) - delta)).astype(BF16)
                    dq = dq + _dotf(ds, kh)
                    dk_ref[rs, ks] += _dot_tn(ds, qh)
                    dv_ref[rs, ks] += _dot_tn(p.astype(BF16), doh)
                dq_ref[:, cs] = dq * ATTN_SCALE

    return _rowcall(
        "attn_bwd", body, nblk, nlb,
        [(q, "row"), (k, "full"), (v, "full"), (o, "row"), (do, "row"), (lse, "row")],
        [((r, NQ), F32, "row"), ((r, NKV), F32, "acc"), ((r, NKV), F32, "acc")],
    )


def qkv_bwd(dq, dk, dv, raw, gains, cos_t, sin_t, w_qkv, x, dx1, mod, nw1, nblk, nlb, ctx_dx_zero):
    r = x.shape[0]

    def body(dq_ref, dk_ref, dv_ref, raw_ref, g_ref, c_ref, s_ref, w_ref, x_ref, dx1_ref, mod_ref, nw_ref,
             dx_ref, draw_ref, dg_ref, dnw_ref, dmod_ref):
        i = pl.program_id(0)
        _acc_init(i, dg_ref, dnw_ref, dmod_ref)
        cos, sin = c_ref[...], s_ref[...]
        for j in range(ATTN_HEADS + ATTN_KV_HEADS):
            isq = j < ATTN_HEADS
            cs = slice(j * HD, (j + 1) * HD)
            dr = dq_ref[:, cs] if isq else dk_ref[:, (j - ATTN_HEADS) * HD : (j - ATTN_HEADS + 1) * HD]
            dxn = dr * cos + _partner(dr * sin)
            xhat, rr = _head_norm(raw_ref[:, cs], None)
            gi = 0 if isq else 1
            dg_ref[gi : gi + 1, :] += jnp.sum(dxn * xhat, axis=0, keepdims=True)
            dxhat = dxn * g_ref[gi : gi + 1, :]
            draw_ref[:, cs] = (rr * (dxhat - xhat * jnp.mean(dxhat * xhat, axis=-1, keepdims=True))).astype(BF16)
        draw_ref[:, NQ + NKV :] = dv_ref[...].astype(BF16)
        dh = _dot_nt(draw_ref[...], w_ref[...])
        sc, nw = _mrow(mod_ref, SC1), nw_ref[...]
        _, n, xhat, rr = _norm_mod(x_ref[...], nw, _mrow(mod_ref, SH1), sc)
        dxn, dsh, dsc, dnw = _norm_mod_bwd(dh, n, xhat, rr, nw, sc)
        dres = dx1_ref[...]
        if ctx_dx_zero:
            dres = jnp.where(i == nlb, 0.0, dres)
        dx_ref[...] = dres + dxn
        s = i // nlb
        dmod_ref[s, SH1 : SH1 + 1, :] += dsh
        dmod_ref[s, SC1 : SC1 + 1, :] += dsc
        dnw_ref[0:1, :] += dnw

    return _rowcall(
        "qkv_bwd", body, nblk, nlb,
        [(dq, "row"), (dk, "row"), (dv, "row"), (raw, "row"), (gains, "full"), (cos_t, "row"), (sin_t, "row"),
         (w_qkv, "full"), (x, "row"), (dx1, "row"), (mod, "stream"), (nw1, "full")],
        [((r, D), F32, "row"), ((r, NQ + 2 * NKV), BF16, "row"), ((8, HD), F32, "acc"), ((8, D), F32, "acc"),
         ((2, 8, D), F32, "acc")],
    )


RET_KSCALE = RET_DK ** -0.5


def ret_in_fwd(x, mod, nw1, w_in, nblk, nlb):
    r = x.shape[0]

    def body(x_ref, mod_ref, nw_ref, w_ref, q_ref, k_ref, v_ref, g_ref, h_ref):
        h = _norm_mod(x_ref[...], nw_ref[...], _mrow(mod_ref, SH1), _mrow(mod_ref, SC1))[0].astype(BF16)
        h_ref[...] = h
        q_ref[...] = _dotf(h, w_ref[:, 0:RNQ]).astype(BF16)
        k_ref[...] = (_dotf(h, w_ref[:, RNQ : 2 * RNQ]) * RET_KSCALE).astype(BF16)
        v_ref[...] = _dotf(h, w_ref[:, 2 * RNQ : 2 * RNQ + RNV]).astype(BF16)
        g_ref[...] = _dotf(h, w_ref[:, 2 * RNQ + RNV :]).astype(BF16)

    return _rowcall(
        "ret_in_fwd", body, nblk, nlb,
        [(x, "row"), (mod, "stream"), (nw1, "full"), (w_in, "full")],
        [((r, RNQ), BF16, "row"), ((r, RNQ), BF16, "row"), ((r, RNV), BF16, "row"), ((r, RNV), BF16, "row"), ((r, D), BF16, "row")],
    )


def _log_sigmoid(x):
    return jnp.minimum(x, 0.0) - jnp.log(1.0 + jnp.exp(-jnp.abs(x)))


def _ret_decays(lg, reverse):
    c = BM
    i = lax.broadcasted_iota(jnp.int32, (c, c), 0)
    j = lax.broadcasted_iota(jnp.int32, (c, c), 1)
    diff = (j - i) if reverse else (i - j)
    ediff = jnp.maximum(diff, 0).astype(F32)
    dm = jnp.where(diff >= 0, jnp.exp(ediff * lg), 0.0)
    rr = lax.broadcasted_iota(jnp.int32, (c, 1), 0).astype(F32)
    eq = (c - rr) if reverse else (rr + 1.0)
    ek = rr if reverse else (c - 1.0 - rr)
    return dm, ediff, jnp.exp(eq * lg), eq, jnp.exp(ek * lg), ek, jnp.exp(c * lg)


def _ret_chunk_index(nlb):
    return (lambda s: jnp.where(s == 0, nlb, s - 1)), (lambda s: jnp.where(s == 0, nlb, nlb - s))


def ret_scan_fwd(q, k, v, logit_b, nlb):
    r = q.shape[0]
    nb = nlb + 1
    fidx, bidx = _ret_chunk_index(nlb)

    def body(qf, kf, vf, qb, kb, vb, lg_ref, of_ref, ob_ref, rf_ref, rb_ref, stf, stb):
        s = pl.program_id(1)

        @pl.when(s == 0)
        def _():
            stf[...] = jnp.zeros(stf.shape, F32)
            stb[...] = jnp.zeros(stb.shape, F32)

        for d, (q_ref, k_ref, v_ref, o_ref, rs_ref, st) in enumerate(((qf, kf, vf, of_ref, rf_ref, stf), (qb, kb, vb, ob_ref, rb_ref, stb))):
            lg = _log_sigmoid(lg_ref[d, 0])[0:1, 0:1]
            dm, _, qd, _, kd, _, gc = _ret_decays(lg, d == 1)
            qq, kk, vv, st0 = q_ref[...], k_ref[...], v_ref[...], st[...]
            rs_ref[0, 0] = st0
            a = _dot_nt(qq, kk) * dm
            o = _dotf(a.astype(BF16), vv) + _dotf(qq, st0.astype(BF16)) * qd
            o_ref[...] = jnp.where(s == 0, 0.0, o)
            st[...] = st0 * gc + _dot_tn((kk.astype(F32) * kd).astype(BF16), vv)

    qspec = lambda f: pl.BlockSpec((BM, RET_DK), lambda h, s: (f(s), h))
    vspec = lambda f: pl.BlockSpec((BM, RET_DV), lambda h, s: (f(s), h))
    sspec = pl.BlockSpec((1, 1, RET_DK, RET_DV), lambda h, s: (h, s, 0, 0))
    return _pcall(
        body,
        name="ret_scan_fwd",
        grid=(RET_HEADS, nb),
        in_specs=[qspec(fidx), qspec(fidx), vspec(fidx), qspec(bidx), qspec(bidx), vspec(bidx),
                  pl.BlockSpec((2, 1, 8, 128), lambda h, s: (0, h, 0, 0))],
        out_specs=[vspec(fidx), vspec(bidx), sspec, sspec],
        out_shape=[jax.ShapeDtypeStruct((r, RNV), F32), jax.ShapeDtypeStruct((r, RNV), F32),
                   jax.ShapeDtypeStruct((RET_HEADS, nb, RET_DK, RET_DV), F32), jax.ShapeDtypeStruct((RET_HEADS, nb, RET_DK, RET_DV), F32)],
        scratch_shapes=[pltpu.VMEM((RET_DK, RET_DV), F32), pltpu.VMEM((RET_DK, RET_DV), F32)],
        compiler_params=pltpu.CompilerParams(dimension_semantics=("parallel", "arbitrary"), vmem_limit_bytes=VMEM_LIMIT),
    )(q, k, v, q, k, v, logit_b)


def _group_norm(o):
    mu = jnp.mean(o, axis=-1, keepdims=True)
    oc = o - mu
    rstd = lax.rsqrt(jnp.mean(oc * oc, axis=-1, keepdims=True) + EPS)
    return oc * rstd, rstd


def ret_out_fwd(o_f, o_b, g, gnw, w_out, x, mod, nblk, nlb):
    r = x.shape[0]

    def body(of_ref, ob_ref, g_ref, gn_ref, w_ref, x_ref, mod_ref, x1_ref, y_ref, z_ref):
        for hh in range(RET_HEADS):
            cs = slice(hh * RET_DV, (hh + 1) * RET_DV)
            yhat, _ = _group_norm(of_ref[:, cs] + ob_ref[:, cs])
            gg = g_ref[:, cs].astype(F32)
            z_ref[:, cs] = (gg * _sigmoid(gg) * (yhat * gn_ref[0:1, cs])).astype(BF16)
        y = _dotf(z_ref[...], w_ref[...])
        y_ref[...] = y
        x1_ref[...] = x_ref[...] + _mrow(mod_ref, G1) * y

    return _rowcall(
        "ret_out_fwd", body, nblk, nlb,
        [(o_f, "row"), (o_b, "row"), (g, "row"), (gnw, "full"), (w_out, "full"), (x, "row"), (mod, "stream")],
        [((r, D), F32, "row"), ((r, D), F32, "row"), ((r, RNV), BF16, "row")],
    )


def ret_out_bwd(dx1, y, o_f, o_b, g, gnw, w_out, mod, dep, nblk, nlb):
    r = dx1.shape[0]

    def body(d_ref, y_ref, of_ref, ob_ref, g_ref, gn_ref, w_ref, mod_ref, dep_ref, dy_ref, do_ref, dg_ref, dgn_ref, dmod_ref, dz):
        i = pl.program_id(0)
        _acc_init(i, dgn_ref, dmod_ref)
        d = d_ref[...]
        dmod_ref[i // nlb, G1 : G1 + 1, :] += jnp.sum(d * y_ref[...], axis=0, keepdims=True)
        dy = (_mrow(mod_ref, G1) * d).astype(BF16)
        dy_ref[...] = dy
        dz[...] = _dot_nt(dy, w_ref[...])
        for hh in range(RET_HEADS):
            cs = slice(hh * RET_DV, (hh + 1) * RET_DV)
            yhat, rstd = _group_norm(of_ref[:, cs] + ob_ref[:, cs])
            gg = g_ref[:, cs].astype(F32)
            sg = _sigmoid(gg)
            gn = gn_ref[0:1, cs]
            dzz = dz[:, cs]
            dg_ref[:, cs] = (dzz * (yhat * gn) * (sg * (1.0 + gg * (1.0 - sg)))).astype(BF16)
            dyn = dzz * (gg * sg)
            dgn_ref[0:1, cs] += jnp.sum(dyn * yhat, axis=0, keepdims=True)
            dyh = dyn * gn
            do = rstd * (dyh - jnp.mean(dyh, axis=-1, keepdims=True) - yhat * jnp.mean(dyh * yhat, axis=-1, keepdims=True))
            do_ref[:, cs] = do.astype(BF16)

    return _rowcall(
        "ret_out_bwd", body, nblk, nlb,
        [(dx1, "row"), (y, "row"), (o_f, "row"), (o_b, "row"), (g, "row"), (gnw, "full"), (w_out, "full"), (mod, "stream"), (dep, "any")],
        [((r, D), BF16, "row"), ((r, RNV), BF16, "row"), ((r, RNV), BF16, "row"), ((8, RNV), F32, "acc"), ((2, 8, D), F32, "acc")],
        scratch=[pltpu.VMEM((BM, RNV), F32)],
    )


def ret_scan_bwd(q, k, v, do, st_f, st_b, logit_b, nlb):
    r = q.shape[0]
    nb = nlb + 1
    fidx, bidx = _ret_chunk_index(nlb)
    step = lambda t: nb - 1 - t

    def body(qf, kf, vf, dof, rf, qb, kb, vb, dob, rb, lg_ref,
             dqf, dkf, dvf, dqb, dkb, dvb, dlg_ref, drf, drb):
        t = pl.program_id(1)
        s = step(t)

        @pl.when(t == 0)
        def _():
            drf[...] = jnp.zeros(drf.shape, F32)
            drb[...] = jnp.zeros(drb.shape, F32)
            dlg_ref[...] = jnp.zeros(dlg_ref.shape, F32)

        dirs = ((qf, kf, vf, dof, rf, dqf, dkf, dvf, drf), (qb, kb, vb, dob, rb, dqb, dkb, dvb, drb))
        for d, (q_ref, k_ref, v_ref, do_ref, rs_ref, dq_ref, dk_ref, dv_ref, dr) in enumerate(dirs):
            lg = _log_sigmoid(lg_ref[d, 0])[0:1, 0:1]
            dm, ediff, qd, eq, kd, ek, gc = _ret_decays(lg, d == 1)
            qq, kk, vv = q_ref[...], k_ref[...], v_ref[...]
            dob16 = jnp.where(s == 0, jnp.zeros((), BF16), do_ref[...])
            do32 = dob16.astype(F32)
            st0 = rs_ref[0, 0]
            st16 = st0.astype(BF16)
            dr0 = dr[...]
            dr16 = dr0.astype(BF16)
            a = _dot_nt(qq, kk) * dm
            daf = _dot_nt(dob16, vv)
            ds = (daf * dm).astype(BF16)
            qr = _dotf(qq, st16)
            k32 = kk.astype(F32)
            kdec = (k32 * kd).astype(BF16)
            dv_ref[...] = (_dot_tn(a.astype(BF16), dob16) + _dotf(kdec, dr16)).astype(BF16)
            dq_ref[...] = (_dotf(ds, kk) + _dot_nt(dob16, st16) * qd).astype(BF16)
            vdr = _dot_nt(vv, dr16)
            dk_ref[...] = (_dot_tn(ds, qq) + vdr * kd).astype(BF16)
            tot = (jnp.sum(daf * a * ediff)
                   + jnp.sum(eq * qd * jnp.sum(do32 * qr, axis=-1, keepdims=True))
                   + jnp.sum(ek * kd * jnp.sum(k32 * vdr, axis=-1, keepdims=True))
                   + jnp.sum(BM * gc * jnp.sum(dr0 * st0, axis=-1, keepdims=True)))
            dlg_ref[d, 0] += tot
            dr[...] = gc * dr0 + _dot_tn(qq, (do32 * qd).astype(BF16))

        @pl.when(t == nb - 1)
        def _():
            dlg_ref[...] = dlg_ref[...] * _sigmoid(-lg_ref[...])

    qspec = lambda f: pl.BlockSpec((BM, RET_DK), lambda h, t: (f(step(t)), h))
    vspec = lambda f: pl.BlockSpec((BM, RET_DV), lambda h, t: (f(step(t)), h))
    sspec = pl.BlockSpec((1, 1, RET_DK, RET_DV), lambda h, t: (h, step(t), 0, 0))
    lspec = pl.BlockSpec((2, 1, 8, 128), lambda h, t: (0, h, 0, 0))
    sq, sv = jax.ShapeDtypeStruct((r, RNQ), BF16), jax.ShapeDtypeStruct((r, RNV), BF16)
    return _pcall(
        body,
        name="ret_scan_bwd",
        grid=(RET_HEADS, nb),
        in_specs=[qspec(fidx), qspec(fidx), vspec(fidx), vspec(fidx), sspec,
                  qspec(bidx), qspec(bidx), vspec(bidx), vspec(bidx), sspec, lspec],
        out_specs=[qspec(fidx), qspec(fidx), vspec(fidx), qspec(bidx), qspec(bidx), vspec(bidx), lspec],
        out_shape=[sq, sq, sv, sq, sq, sv, jax.ShapeDtypeStruct((2, RET_HEADS, 8, 128), F32)],
        scratch_shapes=[pltpu.VMEM((RET_DK, RET_DV), F32), pltpu.VMEM((RET_DK, RET_DV), F32)],
        compiler_params=pltpu.CompilerParams(dimension_semantics=("parallel", "arbitrary"), vmem_limit_bytes=VMEM_LIMIT),
    )(q, k, v, do, st_f, q, k, v, do, st_b, logit_b)


def ret_in_bwd(dqf, dkf, dvf, dqb, dkb, dvb, dgate, w_in, x, dx1, mod, nw1, nblk, nlb):
    r = x.shape[0]
    nin = 2 * RNQ + 2 * RNV

    def body(dqf_ref, dkf_ref, dvf_ref, dqb_ref, dkb_ref, dvb_ref, dg_ref, w_ref, x_ref, dx1_ref, mod_ref, nw_ref,
             dx_ref, din_ref, dnw_ref, dmod_ref):
        i = pl.program_id(0)
        _acc_init(i, dnw_ref, dmod_ref)
        is_ctx = i == nlb
        din_ref[:, 0:RNQ] = (dqf_ref[...].astype(F32) + dqb_ref[...].astype(F32)).astype(BF16)
        din_ref[:, RNQ : 2 * RNQ] = ((dkf_ref[...].astype(F32) + dkb_ref[...].astype(F32)) * RET_KSCALE).astype(BF16)
        din_ref[:, 2 * RNQ : 2 * RNQ + RNV] = (dvf_ref[...].astype(F32) + dvb_ref[...].astype(F32)).astype(BF16)
        din_ref[:, 2 * RNQ + RNV :] = jnp.where(is_ctx, jnp.zeros((), BF16), dg_ref[...])
        dh = _dot_nt(din_ref[...], w_ref[...])
        sc, nw = _mrow(mod_ref, SC1), nw_ref[...]
        _, n, xhat, rr = _norm_mod(x_ref[...], nw, _mrow(mod_ref, SH1), sc)
        dxn, dsh, dsc, dnw = _norm_mod_bwd(dh, n, xhat, rr, nw, sc)
        dx_ref[...] = jnp.where(is_ctx, 0.0, dx1_ref[...]) + dxn
        s = i // nlb
        dmod_ref[s, SH1 : SH1 + 1, :] += dsh
        dmod_ref[s, SC1 : SC1 + 1, :] += dsc
        dnw_ref[0:1, :] += dnw

    return _rowcall(
        "ret_in_bwd", body, nblk, nlb,
        [(dqf, "row"), (dkf, "row"), (dvf, "row"), (dqb, "row"), (dkb, "row"), (dvb, "row"), (dgate, "row"),
         (w_in, "full"), (x, "row"), (dx1, "row"), (mod, "stream"), (nw1, "full")],
        [((r, D), F32, "row"), ((r, nin), BF16, "row"), ((8, D), F32, "acc"), ((2, 8, D), F32, "acc")],
    )


def loss_head(xout, target, nlb):
    r = xout.shape[0]

    def body(x_ref, t_ref, dx_ref, l_ref):
        _acc_init(pl.program_id(0), l_ref)
        err = x_ref[...] - t_ref[...]
        dx_ref[...] = err * (1.0 / D)
        l_ref[...] += 0.5 * jnp.sum(jnp.mean(err * err, axis=-1, keepdims=True))

    return _rowcall(
        "loss_head", body, nlb, nlb,
        [(xout, "row"), (target, "row")],
        [((r, D), F32, "row"), ((8, 128), F32, "acc")],
    )


N_MIXERS = 3
POOL, ATTN, RET = range(3)


def _layer_plan(depth):
    plan = []
    for i in range(depth):
        kind = i % N_MIXERS
        ctx_out = any(k % N_MIXERS != POOL for k in range(i + 1, depth))
        plan.append((kind, i // N_MIXERS, ctx_out or kind != POOL, ctx_out))
    return plan


def local_step(xs, target, mods, w, nlb, depth, fetch, emit):
    nb = nlb + 1
    plan = _layer_plan(depth)
    saved = []
    x = xs
    for i, (kind, j, ctx_in, ctx_out) in enumerate(plan):
        nmix = nb if ctx_out else nlb
        mod, nw1, nw2 = mods[i], w["nw"][i, 0:1], w["nw"][i, 1:2]
        lw = fetch(i, MIX, x)
        sv = {"x": x, "lw": lw}
        if kind == POOL:
            x1, sv["ypre"] = pool_fwd(x, mod, nw1, lw["pool_w"], (w["pbs"], j), nmix, nlb)
        elif kind == ATTN:
            assert ctx_out
            sv["raw"], sv["q"], sv["k"], sv["v"], sv["h"] = qkv_fwd(x, mod, nw1, lw["attn_w_qkv"], w["gains"], w["cos"], w["sin"], nb, nlb)
            sv["o"], sv["lse"] = attn_fwd(sv["q"], sv["k"], sv["v"], nb, nlb)
            x1, sv["y"] = attn_out_fwd(sv["o"], lw["attn_w_o"], x, mod, nb, nlb)
        else:
            assert ctx_in and not ctx_out
            sv["q"], sv["k"], sv["v"], sv["g"], sv["h"] = ret_in_fwd(x, mod, nw1, lw["ret_w_in"], nb, nlb)
            sv["o_f"], sv["o_b"], sv["st_f"], sv["st_b"] = ret_scan_fwd(sv["q"], sv["k"], sv["v"], w["logit_b"], nlb)
            x1, sv["y"], sv["z"] = ret_out_fwd(sv["o_f"], sv["o_b"], sv["g"], w["gnw"], lw["ret_w_out"], x, mod, nlb, nlb)
        sv["x1"] = x1
        lw.update(fetch(i, FFN, x1))
        sv["u"], sv["h2"] = ffn_up(x1, mod, nw2, lw["ffn_w_up"], nmix, nlb)
        x, sv["f"], sv["uc"], sv["gated"] = ffn_down(sv["u"], (w["cw"], i), lw["ffn_w_down"], x1, mod, nmix, nlb)
        saved.append(sv)

    dx, loss_tile = loss_head(x, target, nlb)
    g = {k: [None] * depth for k in ("dcw", "dnw1", "dnw2", "dmod")}
    dep = loss_tile
    for i in reversed(range(depth)):
        kind, j, ctx_in, ctx_out = plan[i]
        sv = saved[i]
        lw, big = sv["lw"], {}
        nmix = nb if ctx_out else nlb
        mod, nw1, nw2 = mods[i], w["nw"][i, 0:1], w["nw"][i, 1:2]
        df, duc, dcb, dmod = ffn_bwd1(dx, sv["f"], sv["uc"], lw["ffn_w_down"], mod, dep, nmix, nlb)
        big["ffn_w_down"] = matmul_tn(sv["gated"], df, nmix)
        du, dx1, dcw, g["dnw2"][i], dm = ffn_bwd3(duc, sv["u"], (w["cw"], i), lw["ffn_w_up"], sv["x1"], dx, mod, nw2, nmix, nlb)
        g["dcw"][i] = dcw + dcb
        dmod = dmod + dm
        big["ffn_w_up"] = matmul_tn(sv["h2"], du, nmix)
        dep, big = emit(i, FFN, big), {}
        if kind == POOL:
            dx, dpw, dpbs, g["dnw1"][i], dm = pool_bwd(dx1, sv["x"], sv["ypre"], mod, nw1, lw["pool_w"], (w["pbs"], j), dep, nmix, nlb, i == 0)
            big["pool_w"] = dpw.astype(BF16)
            g.setdefault("dpbs", {})[j] = dpbs
        elif kind == ATTN:
            dy, do, dm1 = mix_out_bwd("attn_out_bwd", dx1, sv["y"], lw["attn_w_o"], mod, dep, nb, nlb)
            big["attn_w_o"] = matmul_tn(sv["o"], dy, nb)
            dq, dk, dv = attn_bwd(sv["q"], sv["k"], sv["v"], sv["o"], do, sv["lse"], nb, nlb)
            dx, draw, g["dgains"], g["dnw1"][i], dm = qkv_bwd(
                dq, dk, dv, sv["raw"], w["gains"], w["cos"], w["sin"], lw["attn_w_qkv"], sv["x"], dx1, mod, nw1, nb, nlb, False)
            big["attn_w_qkv"] = matmul_tn(sv["h"], draw, nb)
            dm = dm + dm1
        else:
            dy, do, dgate, g["dgnw"], dm1 = ret_out_bwd(dx1, sv["y"], sv["o_f"], sv["o_b"], sv["g"], w["gnw"], lw["ret_w_out"], mod, dep, nlb, nlb)
            big["ret_w_out"] = matmul_tn(sv["z"], dy, nlb)
            dqf, dkf, dvf, dqb, dkb, dvb, g["dlogit"] = ret_scan_bwd(sv["q"], sv["k"], sv["v"], do, sv["st_f"], sv["st_b"], w["logit_b"], nlb)
            dx, din, g["dnw1"][i], dm = ret_in_bwd(dqf, dkf, dvf, dqb, dkb, dvb, dgate, lw["ret_w_in"], sv["x"], dx1, mod, nw1, nb, nlb)
            big["ret_w_in"] = matmul_tn(sv["h"], din, nb)
            dm = dm + dm1
        g["dmod"][i] = dmod + dm
        dep = emit(i, MIX, big)
    return loss_tile, dx, g


MESH_ID = pl.DeviceIdType.MESH
CHIP_FLIPS = ((1, 0), (0, 1), (1, 1))


def _pos():
    return lax.axis_index("x"), lax.axis_index("y"), lax.axis_index("c")


def _flip(v, b):
    return 1 - v if b else v


def small_all_gather(name, x):
    rows, n = x.shape

    def body(x_ref, out_ref, send_sems, recv_sems, local_sem):
        mx, my, mc = _pos()
        me = 4 * mx + 2 * my + mc
        mine = pltpu.make_async_copy(x_ref, out_ref.at[me], local_sem)
        mine.start()
        sends, peers = [], []
        for kk in range(1, N_DEV):
            peer = (_flip(mx, (kk >> 2) & 1), _flip(my, (kk >> 1) & 1), _flip(mc, kk & 1))
            cp = pltpu.make_async_remote_copy(src_ref=x_ref, dst_ref=out_ref.at[me], send_sem=send_sems.at[kk - 1],
                                              recv_sem=recv_sems.at[kk - 1], device_id=peer, device_id_type=MESH_ID)
            cp.start()
            sends.append(cp)
            peers.append(peer)
        for kk, peer in enumerate(peers):
            pidx = 4 * peer[0] + 2 * peer[1] + peer[2]
            pltpu.make_async_remote_copy(src_ref=x_ref, dst_ref=out_ref.at[pidx], send_sem=send_sems.at[kk],
                                         recv_sem=recv_sems.at[kk], device_id=peer, device_id_type=MESH_ID).wait_recv()
        for cp in sends:
            cp.wait_send()
        mine.wait()

    return _pcall(
        body,
        name=name,
        out_shape=jax.ShapeDtypeStruct((N_DEV, rows, n), x.dtype),
        in_specs=[pl.BlockSpec(memory_space=pltpu.VMEM)],
        out_specs=pl.BlockSpec(memory_space=pltpu.VMEM),
        scratch_shapes=[pltpu.SemaphoreType.DMA((N_DEV - 1,)), pltpu.SemaphoreType.DMA((N_DEV - 1,)), pltpu.SemaphoreType.DMA],
        compiler_params=pltpu.CompilerParams(vmem_limit_bytes=VMEM_LIMIT),
    )(x)


def _hbm_exchange(name, ins, out_shapes, plan):
    n_in = len(ins)
    probe_local, probe_remote = plan([None] * n_in, [None] * len(out_shapes), probe=True)

    def body(*refs):
        in_refs, out_refs = refs[:n_in], refs[n_in : n_in + len(out_shapes)]
        send_sems, recv_sems, local_sems = refs[n_in + len(out_shapes) :]
        local, remote = plan(in_refs, out_refs, probe=False)
        lcs = [pltpu.make_async_copy(s, d, local_sems.at[k]) for k, (s, d) in enumerate(local)]
        for cp in lcs:
            cp.start()
        rcs = []
        for k, (s, d, peer, _) in enumerate(remote):
            cp = pltpu.make_async_remote_copy(src_ref=s, dst_ref=d, send_sem=send_sems.at[k], recv_sem=recv_sems.at[k],
                                              device_id=peer, device_id_type=MESH_ID)
            cp.start()
            rcs.append(cp)
        for k, (s, _, peer, here) in enumerate(remote):
            pltpu.make_async_remote_copy(src_ref=s, dst_ref=here, send_sem=send_sems.at[k], recv_sem=recv_sems.at[k],
                                         device_id=peer, device_id_type=MESH_ID).wait_recv()
        for cp in rcs:
            cp.wait_send()
        for cp in lcs:
            cp.wait()

    return _pcall(
        body,
        name=name,
        out_shape=list(out_shapes),
        in_specs=[pl.BlockSpec(memory_space=pl.ANY)] * n_in,
        out_specs=[pl.BlockSpec(memory_space=pl.ANY)] * len(out_shapes),
        scratch_shapes=[pltpu.SemaphoreType.DMA((max(probe_remote, 1),)), pltpu.SemaphoreType.DMA((max(probe_remote, 1),)),
                        pltpu.SemaphoreType.DMA((max(probe_local, 1),))],
    )(*ins)


def _at_axis(ref, axis, start, size):
    return ref.at[tuple(pl.ds(start, size) if a == axis else slice(None) for a in range(len(ref.shape)))]


HBM_SPEC = pl.BlockSpec(memory_space=pltpu.HBM)
SEM_SPEC = pl.BlockSpec(memory_space=pltpu.SEMAPHORE)
SIDE_EFFECT = pltpu.SideEffectType.DATAFLOW_SIDE_EFFECTING


def _in_hbm(a):
    return pltpu.with_memory_space_constraint(a, pltpu.HBM)


def _copies_start(name, bufs, counts, plan, after=None):
    n, ng = len(bufs), len(counts)
    extra = [] if after is None else [after]
    n_in = n + len(extra)

    def body(*refs):
        sems = refs[n_in : n_in + 2 * ng]
        token = refs[n_in + 2 * ng + n]
        for gi, copies in enumerate(plan(refs[:n])):
            for k, (s, d, peer) in enumerate(copies):
                pltpu.make_async_remote_copy(src_ref=s, dst_ref=d, send_sem=sems[2 * gi].at[k], recv_sem=sems[2 * gi + 1].at[k],
                                             device_id=peer, device_id_type=MESH_ID).start()
        token[...] = jnp.zeros(token.shape, token.dtype)

    out = _pcall(
        body,
        name=name,
        out_shape=tuple(pltpu.SemaphoreType.DMA((c,)) for c in counts for _ in range(2))
        + tuple(pltpu.HBM(b.shape, b.dtype) for b in bufs) + (jax.ShapeDtypeStruct((8, 128), F32),),
        in_specs=(HBM_SPEC,) * n + (pl.BlockSpec(memory_space=pl.ANY),) * len(extra),
        out_specs=(SEM_SPEC,) * (2 * ng) + (HBM_SPEC,) * n + (pl.BlockSpec(memory_space=pltpu.VMEM),),
        input_output_aliases={i: 2 * ng + i for i in range(n)},
        compiler_params=pltpu.CompilerParams(has_side_effects=SIDE_EFFECT),
    )(*[_in_hbm(b) for b in bufs], *extra)
    return [(out[2 * g], out[2 * g + 1]) for g in range(ng)], list(out[2 * ng : 2 * ng + n]), out[2 * ng + n]


def _copies_wait(name, sems, bufs, plan, after):
    n, ng = len(bufs), len(sems)

    def body(*refs):
        for gi, copies in enumerate(plan(refs[:n])):
            for k, (s, d, peer) in enumerate(copies):
                cp = pltpu.make_async_remote_copy(src_ref=s, dst_ref=d, send_sem=refs[n + 2 * gi].at[k], recv_sem=refs[n + 2 * gi + 1].at[k],
                                                  device_id=peer, device_id_type=MESH_ID)
                cp.wait_send()
                cp.wait_recv()

    out = _pcall(
        body,
        name=name,
        out_shape=tuple(pltpu.HBM(b.shape, b.dtype) for b in bufs),
        in_specs=(HBM_SPEC,) * n + (SEM_SPEC,) * (2 * ng) + (pl.BlockSpec(memory_space=pl.ANY),),
        out_specs=(HBM_SPEC,) * n,
        input_output_aliases={i: i for i in range(n)},
        compiler_params=pltpu.CompilerParams(has_side_effects=SIDE_EFFECT),
    )(*bufs, *[s for pair in sems for s in pair], after)
    return list(out)


def _matrix_groups(depth):
    out = []
    for i, (kind, j, _, _) in enumerate(_layer_plan(depth)):
        out.append(([("pool_w", j, 1)], [("attn_w_qkv", j, 1), ("attn_w_o", j, 0)], [("ret_w_in", j, 1), ("ret_w_out", j, 0)])[kind])
        out.append([("ffn_w_up", i, 1), ("ffn_w_down", i, 0)])
    return out


MIX, FFN = 0, 1


def _peers(mx, my, mc):
    out = []
    for fx, fy in CHIP_FLIPS:
        px, py = _flip(mx, fx), _flip(my, fy)
        out.append(((px, py, mc), 2 * px + py))
    return out


def full_buffers(shards, names, layers):
    out = []
    for name, _, axis in [e for layer in layers for e in layer]:
        shp = list(shards[names.index(name)].shape[1:])
        shp[axis] *= N_CHIP
        out.append(lax.empty(tuple(shp), BF16))
    return out


def _gather_plan(names, layers, group, n_shards, here):
    def plan(refs):
        mx, my, mc = _pos()
        s_refs, f_refs = refs[:n_shards], refs[n_shards:]
        groups, k = [], 0
        for gi, layer in enumerate(layers):
            if group is not None and gi != group:
                continue
            copies = []
            for name, idx, axis in layer:
                src = s_refs[names.index(name)].at[idx]
                n = src.shape[axis]
                for peer, pchip in _peers(mx, my, mc) + [((mx, my, 1 - mc), 2 * mx + my)]:
                    at = pchip if here else 2 * mx + my
                    copies.append((src, _at_axis(f_refs[k], axis, at * n, n), peer))
                k += 1
            groups.append(copies)
        return groups

    return plan


def gather_start(shards, names, layers, fulls, after):
    counts = [len(layer) * (len(CHIP_FLIPS) + 1) for layer in layers]
    sems, bufs, _ = _copies_start("gather_start", list(shards) + list(fulls), counts,
                                  _gather_plan(names, layers, None, len(shards), False), after)
    return sems, bufs[: len(shards)], bufs[len(shards) :]


def gather_wait(g, sems_g, shards, names, layers, fulls_g, after):
    bufs = _copies_wait(f"gather_wait_{g}", [sems_g], list(shards) + list(fulls_g), _gather_plan(names, layers, g, len(shards), True), after)
    return bufs[: len(shards)], bufs[len(shards) :]


def _scatter_plan(layer_entries, n_grads, land_of):
    def plan(refs):
        mx, my, mc = _pos()
        groups, k = [], 0
        for layer in layer_entries:
            copies = []
            for name, idx, axis in layer:
                gref, land = refs[k], refs[n_grads + land_of(k, name)]
                n = gref.shape[axis] // N_CHIP
                for slot, (peer, pchip) in enumerate(_peers(mx, my, mc)):
                    copies.append((_at_axis(gref, axis, pchip * n, n), land.at[slot, idx], peer))
                k += 1
            groups.append(copies)
        return groups

    return plan


def scatter_start(i, layer, grads, lands):
    sems, bufs, token = _copies_start(f"scatter_start_{i}", list(grads) + list(lands), [len(layer) * len(CHIP_FLIPS)],
                                      _scatter_plan([layer], len(grads), lambda k, name: k))
    return sems[0], bufs[: len(grads)], bufs[len(grads) :], token


def scatter_wait(sems, layers, grads, names, lands, after):
    bufs = _copies_wait("scatter_wait", sems, list(grads) + list(lands),
                        _scatter_plan(layers, len(grads), lambda k, name: names.index(name)), after)
    return bufs[: len(grads)], bufs[len(grads) :]


def sibling_swap(parts):
    def plan(in_refs, out_refs, probe):
        if probe:
            return 0, len(parts)
        mx, my, mc = _pos()
        return [], [(s, o, (mx, my, 1 - mc), o) for s, o in zip(in_refs, out_refs)]

    return _hbm_exchange("sibling_swap", parts, [jax.ShapeDtypeStruct(p.shape, p.dtype) for p in parts], plan)


EW_ROWS = 256


def _ew_call(name, fn, ins, n_out):
    rows, cols = ins[0].shape[-2:]
    tr = EW_ROWS if rows % EW_ROWS == 0 else rows

    def body(*refs):
        outs = fn(*[r[...] for r in refs[: len(ins)]])
        for o_ref, o in zip(refs[len(ins) :], outs):
            o_ref[...] = o

    def spec(a):
        if a.ndim == 3:
            return pl.BlockSpec((a.shape[0], tr, cols), lambda i: (0, i, 0))
        return pl.BlockSpec((tr, cols), lambda i: (i, 0))

    return _pcall(
        body,
        name=name,
        grid=(rows // tr,),
        in_specs=[spec(a) for a in ins],
        out_specs=[pl.BlockSpec((tr, cols), lambda i: (i, 0))] * n_out,
        out_shape=[jax.ShapeDtypeStruct((rows, cols), F32)] * n_out,
        compiler_params=pltpu.CompilerParams(dimension_semantics=("parallel",), vmem_limit_bytes=VMEM_LIMIT),
    )(*ins)


def _adamw(w, g, m, v):
    m = ADAM_B1 * m + (1.0 - ADAM_B1) * g
    v = ADAM_B2 * v + (1.0 - ADAM_B2) * (g * g)
    m_hat = m / (1.0 - ADAM_B1 ** ADAM_STEP)
    v_hat = v / (1.0 - ADAM_B2 ** ADAM_STEP)
    return -ADAM_LR * (m_hat / (jnp.sqrt(v_hat) + ADAM_EPS) + ADAM_WD * w), m, v


def sum_slots(name, own, landing):
    def fn(o, l):
        acc = o.astype(F32)
        for k in range(l.shape[0]):
            acc = acc + l[k].astype(F32)
        return (acc,)

    return _ew_call(name, fn, [own, landing], 1)[0]


def adamw_pair(name, w, m, v, p, ps):
    def fn(w, m, v, p, ps):
        g = p + ps
        return (g,) + _adamw(w, g, m, v)

    return _ew_call(name, fn, [w, m, v, p, ps], 4)


def adamw_one(name, w, m, v, g):
    return _ew_call(name, lambda w, m, v, g: _adamw(w, g, m, v), [w, m, v, g], 3)


def reduce_devices(name, x):
    def fn(a):
        acc = a[0]
        for k in range(1, a.shape[0]):
            acc = acc + a[k]
        return (acc,)

    return _ew_call(name, fn, [x], 1)[0]


ADA_ROWS = 16
ADA_CTX = N_DEV


def ada_fwd(s9, ada_w, ada_b):
    depth, _, n = ada_w.shape

    def body(s_ref, w_ref, b_ref, o_ref):
        s = s_ref[...]
        o_ref[...] = _dotf((s * _sigmoid(s)).astype(BF16), w_ref[...].astype(BF16)) + b_ref[...]

    return _pcall(
        body,
        name="ada_fwd",
        grid=(depth,),
        in_specs=[pl.BlockSpec((ADA_ROWS, D), lambda i: (0, 0)), pl.BlockSpec((None, D, n), lambda i: (i, 0, 0)),
                  pl.BlockSpec((None, 1, n), lambda i: (i, 0, 0))],
        out_specs=pl.BlockSpec((None, ADA_ROWS, n), lambda i: (i, 0, 0)),
        out_shape=jax.ShapeDtypeStruct((depth, ADA_ROWS, n), F32),
        compiler_params=pltpu.CompilerParams(dimension_semantics=("arbitrary",), vmem_limit_bytes=VMEM_LIMIT),
    )(s9, ada_w, ada_b)


def ada_bwd(s9, dm, ada_w):
    depth, _, n = ada_w.shape

    def body(s_ref, dm_ref, w_ref, gw_ref, ds_ref):
        _acc_init(pl.program_id(0), ds_ref)
        s = s_ref[...]
        dmb = dm_ref[...].astype(BF16)
        gw_ref[...] = _dot_tn((s * _sigmoid(s)).astype(BF16), dmb)
        ds_ref[...] += _dot_nt(dmb, w_ref[...].astype(BF16))

    return _pcall(
        body,
        name="ada_bwd",
        grid=(depth,),
        in_specs=[pl.BlockSpec((ADA_ROWS, D), lambda i: (0, 0)), pl.BlockSpec((None, ADA_ROWS, n), lambda i: (i, 0, 0)),
                  pl.BlockSpec((None, D, n), lambda i: (i, 0, 0))],
        out_specs=[pl.BlockSpec((None, D, n), lambda i: (i, 0, 0)), pl.BlockSpec((ADA_ROWS, D), lambda i: (0, 0))],
        out_shape=[jax.ShapeDtypeStruct((depth, D, n), F32), jax.ShapeDtypeStruct((ADA_ROWS, D), F32)],
        compiler_params=pltpu.CompilerParams(dimension_semantics=("arbitrary",), vmem_limit_bytes=VMEM_LIMIT),
    )(s9, dm, ada_w)


def cctx_grad(parts, c_ctx):
    def body(p_ref, c_ref, o_ref):
        acc = p_ref[0, ADA_CTX : ADA_CTX + 1, :]
        for chip in range(1, N_CHIP):
            acc = acc + p_ref[2 * chip, ADA_CTX : ADA_CTX + 1, :]
        c = c_ref[...]
        sg = _sigmoid(c)
        o_ref[...] = acc * (sg * (1.0 + c * (1.0 - sg)))

    return _pcall(body, name="cctx_grad", out_shape=jax.ShapeDtypeStruct((1, D), F32))(parts, c_ctx)


def _pack(arrs):
    flat = jnp.concatenate([a.astype(F32).reshape(-1) for a in arrs])
    rows = -(-flat.shape[0] // (8 * D)) * 8
    return jnp.pad(flat, (0, rows * D - flat.shape[0])).reshape(rows, D)


def _unpack(slab, shapes):
    lead = slab.shape[:-2]
    flat = slab.reshape(lead + (-1,))
    out, off = [], 0
    for shp in shapes:
        n = 1
        for d in shp:
            n *= d
        out.append(flat[..., off : off + n].reshape(lead + tuple(shp)))
        off += n
    return out


def _unshard(per_dev, axis):
    return jnp.concatenate([per_dev[2 * chip] for chip in range(N_CHIP)], axis=axis)


BIG = (("pool_w", 1), ("attn_w_qkv", 1), ("attn_w_o", 0), ("ret_w_in", 1), ("ret_w_out", 0), ("ffn_w_up", 1), ("ffn_w_down", 0))
WEIGHTS = ("c_ctx", "ada_w", "ada_b", "norm_w", "pool_w", "pool_b", "pool_scale", "attn_w_qkv", "attn_q_gain", "attn_k_gain",
           "attn_w_o", "ret_w_in", "ret_decay_logit", "ret_gn_w", "ret_w_out", "ffn_w_up", "ffn_conv_w", "ffn_conv_b", "ffn_w_down")
SMALL = tuple(n for n in WEIGHTS if n != "ada_w" and n not in dict(BIG))
SMALL_SHARD_AXIS = {"norm_w": 2, "pool_b": 1, "pool_scale": 1, "ret_gn_w": 1, "ffn_conv_w": 2}


def kernel(x, c, ctx, c_ctx, ada_w, ada_b, norm_w, pool_w, pool_b, pool_scale, attn_w_qkv, attn_q_gain, attn_k_gain, attn_w_o, ret_w_in, ret_decay_logit, ret_gn_w, ret_w_out, ffn_w_up, ffn_conv_w, ffn_conv_b, ffn_w_down, loss_target, m_c_ctx, m_ada_w, m_ada_b, m_norm_w, m_pool_w, m_pool_b, m_pool_scale, m_attn_w_qkv, m_attn_q_gain, m_attn_k_gain, m_attn_w_o, m_ret_w_in, m_ret_decay_logit, m_ret_gn_w, m_ret_w_out, m_ffn_w_up, m_ffn_conv_w, m_ffn_conv_b, m_ffn_w_down, v_c_ctx, v_ada_w, v_ada_b, v_norm_w, v_pool_w, v_pool_b, v_pool_scale, v_attn_w_qkv, v_attn_q_gain, v_attn_k_gain, v_attn_w_o, v_ret_w_in, v_ret_decay_logit, v_ret_gn_w, v_ret_w_out, v_ffn_w_up, v_ffn_conv_w, v_ffn_conv_b, v_ffn_w_down):
    P = dict(zip(WEIGHTS, (c_ctx, ada_w, ada_b, norm_w, pool_w, pool_b, pool_scale, attn_w_qkv, attn_q_gain, attn_k_gain, attn_w_o,
                           ret_w_in, ret_decay_logit, ret_gn_w, ret_w_out, ffn_w_up, ffn_conv_w, ffn_conv_b, ffn_w_down)))
    M = dict(zip(WEIGHTS, (m_c_ctx, m_ada_w, m_ada_b, m_norm_w, m_pool_w, m_pool_b, m_pool_scale, m_attn_w_qkv, m_attn_q_gain,
                           m_attn_k_gain, m_attn_w_o, m_ret_w_in, m_ret_decay_logit, m_ret_gn_w, m_ret_w_out, m_ffn_w_up,
                           m_ffn_conv_w, m_ffn_conv_b, m_ffn_w_down)))
    V = dict(zip(WEIGHTS, (v_c_ctx, v_ada_w, v_ada_b, v_norm_w, v_pool_w, v_pool_b, v_pool_scale, v_attn_w_qkv, v_attn_q_gain,
                           v_attn_k_gain, v_attn_w_o, v_ret_w_in, v_ret_decay_logit, v_ret_gn_w, v_ret_w_out, v_ffn_w_up,
                           v_ffn_conv_w, v_ffn_conv_b, v_ffn_w_down)))
    depth, s, l = ada_w.shape[0], x.shape[1], ctx.shape[1]
    assert l == BM and s % BM == 0 and s % GRID_W == 0
    nlb = s // BM
    n_pool = pool_w.shape[0]
    mx, my, mc = _pos()
    chip, dev = 2 * mx + my, 4 * mx + 2 * my + mc
    nada = ada_w.shape[2]

    sharded = [n for n in SMALL if n in SMALL_SHARD_AXIS]
    got = small_all_gather("gather_small", _pack([c[0]] + [P[n] for n in sharded]))
    got = _unpack(got, [(D,)] + [P[n].shape for n in sharded])
    c_all = got[0]
    full = {n: _unshard(g_, SMALL_SHARD_AXIS[n]) for n, g_ in zip(sharded, got[1:])}

    s9 = jnp.concatenate([c_all, c_ctx[None, :], jnp.zeros((ADA_ROWS - N_DEV - 1, D), F32)], axis=0)
    ada_b_mine = lax.dynamic_slice_in_dim(ada_b, chip * nada, nada, axis=1)[:, None, :]
    mod_part = ada_fwd(s9, ada_w, ada_b_mine)
    mod_all = _unshard(small_all_gather("gather_mod", mod_part.reshape(depth * ADA_ROWS, nada)), 1).reshape(depth, ADA_ROWS, 6, D)
    mod_mine = lax.dynamic_index_in_dim(mod_all, dev, axis=1, keepdims=False)
    mods_all = jnp.pad(jnp.stack([mod_mine, mod_all[:, ADA_CTX]], axis=1), ((0, 0), (0, 0), (0, 2), (0, 0)))
    mods = [mods_all[i] for i in range(depth)]

    names = [n for n, _ in BIG]
    layers = _matrix_groups(depth)
    shards = [P[n].astype(BF16) for n in names]
    gsems, shards, fulls = gather_start(shards, names, layers, full_buffers(shards, names, layers), mods_all)
    first = [sum(len(layer) for layer in layers[:g_]) for g_ in range(len(layers) + 1)]
    flight = {"shards": shards}

    def fetch(i, part, after):
        g_ = 2 * i + part
        flight["shards"], mats = gather_wait(g_, gsems[g_], flight["shards"], names, layers, fulls[first[g_] : first[g_ + 1]], after)
        return {name: m for (name, _, _), m in zip(layers[g_], mats)}

    lands = {n: lax.empty((len(CHIP_FLIPS),) + P[n].shape, BF16) for n in names}
    sent = {}

    def emit(i, part, big):
        g_ = 2 * i + part
        lnames = [name for name, _, _ in layers[g_]]
        sems, gl, ll, token = scatter_start(g_, layers[g_], [big[n] for n in lnames], [lands[n] for n in lnames])
        lands.update(zip(lnames, ll))
        sent[g_] = (sems, gl)
        return token

    w = {
        "nw": full["norm_w"],
        "pbs": jnp.concatenate([full["pool_b"][:, None], full["pool_scale"][:, None], jnp.zeros((n_pool, 6, D), F32)], axis=1),
        "gains": jnp.concatenate([attn_q_gain, attn_k_gain, jnp.zeros((6, HD), F32)], axis=0),
        "gnw": full["ret_gn_w"],
        "logit_b": jnp.broadcast_to(ret_decay_logit[0][:, :, None, None], (2, RET_HEADS, 8, 128)),
        "cw": jnp.concatenate([full["ffn_conv_w"], ffn_conv_b[:, None, :], jnp.zeros((depth, 4, 2 * D_FF), F32)], axis=1),
    }
    w["cos"], w["sin"] = rope_tables(s, l)

    xs = jnp.concatenate([x[0], ctx[0]], axis=0)
    loss_tile, dxs, g = local_step(xs, loss_target[0], mods, w, nlb, depth, fetch, emit)
    loss = lax.psum(loss_tile[0, 0], MESH_AXES)
    grad_x = dxs[None]

    small_shapes = [(depth, 2, 8, D), (depth, 2, D), (n_pool, 2, D), (2, HD), (2, RET_HEADS), (RNV,), (depth, 4, 2 * D_FF)]
    slab = _pack([
        jnp.stack(g["dmod"]),
        jnp.stack([jnp.stack([g["dnw1"][i][0], g["dnw2"][i][0]]) for i in range(depth)]),
        jnp.stack([g["dpbs"][j][0:2] for j in range(n_pool)]),
        g["dgains"][0:2], g["dlogit"][:, :, 0, 0], g["dgnw"][0], jnp.stack([g["dcw"][i][0:4] for i in range(depth)]),
    ])
    slabs = small_all_gather("gather_small_grads", slab)
    dmod_dev = _unpack(slabs, small_shapes[:1])[0]
    t_dmod, t_nw, t_pbs, t_gains, t_logit, t_gnw, t_cw = _unpack(reduce_devices("reduce_small_grads", slabs), small_shapes)

    def cols(a):
        return lax.dynamic_slice_in_dim(a, chip * nada, nada, axis=a.ndim - 1)

    dm_lat = jnp.swapaxes(cols(dmod_dev[:, :, 0, :6].reshape(N_DEV, depth, 6 * D)), 0, 1)
    dm_ctx = cols(t_dmod[:, 1, :6].reshape(depth, 1, 6 * D))
    dm = jnp.concatenate([dm_lat, dm_ctx, jnp.zeros((depth, ADA_ROWS - N_DEV - 1, nada), F32)], axis=1)
    g_ada_w, ds9 = ada_bwd(s9, dm, ada_w)
    g_c_ctx = cctx_grad(small_all_gather("gather_dcctx", ds9), c_ctx[None, :])[0]

    def mine(a, name):
        n = P[name].shape[SMALL_SHARD_AXIS[name]]
        return lax.dynamic_slice_in_dim(a, chip * n, n, axis=SMALL_SHARD_AXIS[name])

    G = {
        "c_ctx": g_c_ctx,
        "ada_b": (t_dmod[:, 0, :6] + t_dmod[:, 1, :6]).reshape(depth, 6 * D),
        "norm_w": mine(t_nw, "norm_w"),
        "pool_b": mine(t_pbs[:, 0], "pool_b"), "pool_scale": mine(t_pbs[:, 1], "pool_scale"),
        "attn_q_gain": t_gains[0:1], "attn_k_gain": t_gains[1:2],
        "ret_decay_logit": t_logit[None], "ret_gn_w": mine(t_gnw[None], "ret_gn_w"),
        "ffn_conv_w": mine(t_cw[:, 0:3], "ffn_conv_w"), "ffn_conv_b": t_cw[:, 3],
    }
    sw, sg, sm, sv = (_pack([d_[n] for n in SMALL]) for d_ in (P, G, M, V))
    outs = adamw_one("adamw_small", sw, sm, sv, sg)
    D_, NM, NV = ({n: a for n, a in zip(SMALL, _unpack(o, [P[n].shape for n in SMALL]))} for o in outs)

    flat2 = lambda a: a.reshape(-1, a.shape[-1])
    G["ada_w"] = g_ada_w
    o3 = adamw_one("adamw_ada", flat2(ada_w), flat2(M["ada_w"]), flat2(V["ada_w"]), flat2(g_ada_w))
    D_["ada_w"], NM["ada_w"], NV["ada_w"] = (o.reshape(ada_w.shape) for o in o3)

    sent_grads, landed = scatter_wait([sent[g_][0] for g_ in range(len(layers))], layers,
                                      [a for g_ in range(len(layers)) for a in sent[g_][1]], names, [lands[n] for n in names], o3[0])
    own = {n: [None] * P[n].shape[0] for n in names}
    for (name, idx, axis), a in zip([e for layer in layers for e in layer], sent_grads):
        n_ = a.shape[axis] // N_CHIP
        own[name][idx] = lax.dynamic_slice_in_dim(a, chip * n_, n_, axis=axis)
    partial = [sum_slots("sum_" + n, jnp.stack(own[n]).reshape(-1, lnd.shape[-1]), lnd.reshape(len(CHIP_FLIPS), -1, lnd.shape[-1]))
               for n, lnd in zip(names, landed)]
    theirs = sibling_swap(partial)
    for (n, _), p, ps in zip(BIG, partial, theirs):
        o4 = adamw_pair("adamw_" + n, flat2(P[n]), flat2(M[n]), flat2(V[n]), p, ps)
        G[n], D_[n], NM[n], NV[n] = (o.reshape(P[n].shape) for o in o4)

    return (loss, grad_x, *[G[n] for n in WEIGHTS], *[D_[n] for n in WEIGHTS], *[NM[n] for n in WEIGHTS], *[NV[n] for n in WEIGHTS])
```

```python
import functools

import jax
import jax.numpy as jnp
from jax import lax
from jax.experimental import pallas as pl
from jax.experimental.pallas import tpu as pltpu

F32 = jnp.float32
BF16 = jnp.bfloat16

D = 1024
BM = 256
EPS = 1e-6
POOL_WINDOWS = (2, 4, 8, 16)
POOL_GROUP = D // 4
ATTN_HEADS = 8
ATTN_KV_HEADS = 2
HD = D // ATTN_HEADS
ATTN_GROUP = ATTN_HEADS // ATTN_KV_HEADS
NQ = ATTN_HEADS * HD
NKV = ATTN_KV_HEADS * HD
GRID_W = 64
ROPE_THETA = 10000.0
RET_HEADS = 4
RET_DK = D // RET_HEADS
RET_DV = 2 * D // RET_HEADS
RNQ = RET_HEADS * RET_DK
RNV = RET_HEADS * RET_DV
D_FF = 2816
FF_CHUNK = 256
ADAM_LR, ADAM_B1, ADAM_B2, ADAM_EPS, ADAM_WD, ADAM_STEP = 0.001, 0.9, 0.999, 1e-08, 0.01, 10
HALO_F32 = 8
HALO_BF16 = 16
VMEM_LIMIT = 60 * 1024 * 1024

MESH_AXES = ("x", "y", "c")
N_DEV = 8
N_CHIP = 4


def _pcall(body, **kw):
    return pl.pallas_call(body, **kw)


def _spec(shape, kind, nlb, nblk):
    nd = len(shape)
    if kind == "row":
        return pl.BlockSpec((BM, shape[1]), lambda i: (i, 0))
    if kind == "row_lat":
        return pl.BlockSpec((BM, shape[1]), lambda i: (jnp.minimum(i, nlb - 1), 0))
    if kind == "full":
        return pl.BlockSpec(tuple(shape), lambda i: (0,) * nd, pipeline_mode=pl.Buffered(1))
    if isinstance(kind, tuple) and kind[0] == "fullat":
        return pl.BlockSpec((None,) + tuple(shape[1:]), lambda i: (kind[1],) + (0,) * (nd - 1), pipeline_mode=pl.Buffered(1))
    if kind == "acc":
        return pl.BlockSpec(tuple(shape), lambda i: (0,) * nd)
    if kind == "any":
        return pl.BlockSpec(memory_space=pl.ANY)
    if kind == "stream":
        return pl.BlockSpec((1,) + tuple(shape[1:]), lambda i: (i // nlb,) + (0,) * (nd - 1))
    if kind in ("prev8", "prev16"):
        hb = int(kind[4:])
        return pl.BlockSpec((hb, shape[1]), lambda i: (jnp.maximum(i * (BM // hb) - 1, 0), 0))
    if kind in ("next8", "next16"):
        hb = int(kind[4:])
        last = nblk * BM // hb - 1
        return pl.BlockSpec((hb, shape[1]), lambda i: (jnp.minimum((i + 1) * (BM // hb), last), 0))
    raise ValueError(kind)


def _rowcall(name, body, nblk, nlb, ins, outs, scratch=()):
    return _pcall(
        body,
        name=name,
        grid=(nblk,),
        in_specs=[_spec(a.shape, k, nlb, nblk) for a, k in ins],
        out_specs=[_spec(s, k, nlb, nblk) for s, _, k in outs],
        out_shape=[jax.ShapeDtypeStruct(s, d) for s, d, _ in outs],
        scratch_shapes=list(scratch),
        compiler_params=pltpu.CompilerParams(dimension_semantics=("arbitrary",), vmem_limit_bytes=VMEM_LIMIT),
    )(*[a for a, _ in ins])


def _wk(w):
    return (w[0], ("fullat", w[1])) if isinstance(w, tuple) else (w, "full")


def _stream_edges(i, nlb):
    is_ctx = i == nlb
    return (i == 0) | is_ctx, (i == nlb - 1) | is_ctx, is_ctx


def _dotf(a, b):
    return jnp.dot(a, b, preferred_element_type=F32)


def _dot_nt(a, b):
    return lax.dot_general(a, b, (((1,), (1,)), ((), ())), preferred_element_type=F32)


def _dot_tn(a, b):
    return lax.dot_general(a, b, (((0,), (0,)), ((), ())), preferred_element_type=F32)


def _sigmoid(x):
    return 0.5 * jnp.tanh(0.5 * x) + 0.5


def _norm_mod(x, nw, sh, sc):
    r = lax.rsqrt(jnp.mean(x * x, axis=-1, keepdims=True) + EPS)
    xhat = x * r
    n = xhat * nw
    return n * (1.0 + sc) + sh, n, xhat, r


def _norm_mod_bwd(dh, n, xhat, r, nw, sc):
    dsh = jnp.sum(dh, axis=0, keepdims=True)
    dsc = jnp.sum(dh * n, axis=0, keepdims=True)
    dn = dh * (1.0 + sc)
    dnw = jnp.sum(dn * xhat, axis=0, keepdims=True)
    dxhat = dn * nw
    dx = r * (dxhat - xhat * jnp.mean(dxhat * xhat, axis=-1, keepdims=True))
    return dx, dsh, dsc, dnw


def _acc_init(i, *refs):
    @pl.when(i == 0)
    def _():
        for r in refs:
            r[...] = jnp.zeros(r.shape, r.dtype)


SH1, SC1, G1, SH2, SC2, G2 = range(6)


def _mrow(mod_ref, k):
    return mod_ref[0, k : k + 1, :]


def _shift_rows(x_ref, xp_ref, xn_ref, cs, first, last):
    cur = x_ref[:, cs].astype(F32)
    rows = lax.broadcasted_iota(jnp.int32, cur.shape, 0)
    pr = jnp.where(first, 0.0, xp_ref[HALO_BF16 - 1 : HALO_BF16, cs].astype(F32))
    nx = jnp.where(last, 0.0, xn_ref[0:1, cs].astype(F32))
    dn = jnp.where(rows == 0, pr, pltpu.roll(cur, 1, 0))
    up = jnp.where(rows == BM - 1, nx, pltpu.roll(cur, BM - 1, 0))
    return dn, cur, up


def ffn_up(x1, mod, nw2, w_up, nblk, nlb):
    r = x1.shape[0]

    def body(x_ref, mod_ref, nw_ref, w_ref, u_ref, h_ref):
        h, _, _, _ = _norm_mod(x_ref[...], nw_ref[...], _mrow(mod_ref, SH2), _mrow(mod_ref, SC2))
        hb = h.astype(BF16)
        h_ref[...] = hb
        u_ref[...] = _dotf(hb, w_ref[...]).astype(BF16)

    return _rowcall(
        "ffn_up", body, nblk, nlb,
        [(x1, "row"), (mod, "stream"), (nw2, "full"), _wk(w_up)],
        [((r, 2 * D_FF), BF16, "row"), ((r, D), BF16, "row")],
    )


def _conv_gate_chunk(u_ref, up_ref, un_ref, cw_ref, j, first, last):
    res = []
    for half in range(2):
        c0 = half * D_FF + j * FF_CHUNK
        cs = slice(c0, c0 + FF_CHUNK)
        dn, cur, up = _shift_rows(u_ref, up_ref, un_ref, cs, first, last)
        val = dn * cw_ref[0:1, cs] + cur * cw_ref[1:2, cs] + up * cw_ref[2:3, cs] + cw_ref[3:4, cs]
        res.append((val, dn, cur, up, cs))
    return res


def ffn_down(u, cw, w_down, x1, mod, nblk, nlb):
    r = x1.shape[0]

    def body(u_ref, up_ref, un_ref, cw_ref, w_ref, x_ref, mod_ref, x2_ref, f_ref, uc_ref, gated_ref):
        first, last, _ = _stream_edges(pl.program_id(0), nlb)
        f = jnp.zeros((BM, D), F32)
        for j in range(D_FF // FF_CHUNK):
            (a, _, _, _, acs), (v, _, _, _, vcs) = _conv_gate_chunk(u_ref, up_ref, un_ref, cw_ref, j, first, last)
            uc_ref[:, acs] = a.astype(BF16)
            uc_ref[:, vcs] = v.astype(BF16)
            gated = (a * _sigmoid(a) * v).astype(BF16)
            gated_ref[:, acs] = gated
            f = f + _dotf(gated, w_ref[acs, :])
        f_ref[...] = f.astype(BF16)
        x2_ref[...] = x_ref[...] + _mrow(mod_ref, G2) * f

    return _rowcall(
        "ffn_down", body, nblk, nlb,
        [(u, "row"), (u, "prev16"), (u, "next16"), _wk(cw), _wk(w_down), (x1, "row"), (mod, "stream")],
        [((r, D), F32, "row"), ((r, D), BF16, "row"), ((r, 2 * D_FF), BF16, "row"), ((r, D_FF), BF16, "row")],
    )


def ffn_bwd1(dx2, f, uc, w_down, mod, dep, nblk, nlb):
    r = dx2.shape[0]

    def body(dx_ref, f_ref, uc_ref, w_ref, mod_ref, dep_ref, df_ref, duc_ref, dcb_ref, dmod_ref):
        i = pl.program_id(0)
        _acc_init(i, dcb_ref, dmod_ref)
        dx = dx_ref[...]
        df = (_mrow(mod_ref, G2) * dx).astype(BF16)
        df_ref[...] = df
        dmod_ref[i // nlb, G2 : G2 + 1, :] += jnp.sum(dx * f_ref[...].astype(F32), axis=0, keepdims=True)
        for j in range(D_FF // FF_CHUNK):
            acs = slice(j * FF_CHUNK, (j + 1) * FF_CHUNK)
            vcs = slice(D_FF + j * FF_CHUNK, D_FF + (j + 1) * FF_CHUNK)
            a, v = uc_ref[:, acs].astype(F32), uc_ref[:, vcs].astype(F32)
            sa = _sigmoid(a)
            dg = _dot_nt(df, w_ref[acs, :])
            for dval, cs in ((dg * v * (sa * (1.0 + a * (1.0 - sa))), acs), (dg * (a * sa), vcs)):
                duc_ref[:, cs] = dval.astype(BF16)
                dcb_ref[3:4, cs] += jnp.sum(dval, axis=0, keepdims=True)

    return _rowcall(
        "ffn_bwd1", body, nblk, nlb,
        [(dx2, "row"), (f, "row"), (uc, "row"), _wk(w_down), (mod, "stream"), (dep, "any")],
        [((r, D), BF16, "row"), ((r, 2 * D_FF), BF16, "row"), ((8, 2 * D_FF), F32, "acc"), ((2, 8, D), F32, "acc")],
    )


def ffn_bwd3(duc, u, cw, w_up, x1, dx2, mod, nw2, nblk, nlb):
    r = dx2.shape[0]

    def body(d_ref, dp_ref, dn_ref, u_ref, cw_ref, w_ref, x_ref, dx_ref, mod_ref, nw_ref, du_ref, dx1_ref, dcw_ref, dnw_ref, dmod_ref):
        i = pl.program_id(0)
        first, last, _ = _stream_edges(i, nlb)
        _acc_init(i, dcw_ref, dnw_ref, dmod_ref)
        dh = jnp.zeros((BM, D), F32)
        for j in range(2 * D_FF // FF_CHUNK):
            cs = slice(j * FF_CHUNK, (j + 1) * FF_CHUNK)
            dn, cur, up = _shift_rows(d_ref, dp_ref, dn_ref, cs, first, last)
            du = (up * cw_ref[0:1, cs] + cur * cw_ref[1:2, cs] + dn * cw_ref[2:3, cs]).astype(BF16)
            du_ref[:, cs] = du
            dh = dh + _dot_nt(du, w_ref[:, cs])
            uu = u_ref[:, cs].astype(F32)
            dcw_ref[0:1, cs] += jnp.sum(up * uu, axis=0, keepdims=True)
            dcw_ref[1:2, cs] += jnp.sum(cur * uu, axis=0, keepdims=True)
            dcw_ref[2:3, cs] += jnp.sum(dn * uu, axis=0, keepdims=True)
        sc = _mrow(mod_ref, SC2)
        nw = nw_ref[...]
        _, n, xhat, rr = _norm_mod(x_ref[...], nw, _mrow(mod_ref, SH2), sc)
        dxn, dsh, dsc, dnw = _norm_mod_bwd(dh, n, xhat, rr, nw, sc)
        dx1_ref[...] = dx_ref[...] + dxn
        s = i // nlb
        dmod_ref[s, SH2 : SH2 + 1, :] += dsh
        dmod_ref[s, SC2 : SC2 + 1, :] += dsc
        dnw_ref[0:1, :] += dnw

    return _rowcall(
        "ffn_bwd3", body, nblk, nlb,
        [(duc, "row"), (duc, "prev16"), (duc, "next16"), (u, "row"), _wk(cw), _wk(w_up), (x1, "row"), (dx2, "row"),
         (mod, "stream"), (nw2, "full")],
        [((r, 2 * D_FF), BF16, "row"), ((r, D), F32, "row"), ((8, 2 * D_FF), F32, "acc"), ((8, D), F32, "acc"),
         ((2, 8, D), F32, "acc")],
    )


def matmul_tn(a, b, nblk, tn=None):
    k, n = a.shape[1], b.shape[1]
    rows = nblk * BM
    tr = 768 if rows % 768 == 0 else (1024 if rows % 1024 == 0 else BM)
    if tn is None:
        tn = n
        while k * tn * 4 > 6 * 1024 * 1024 and tn % 256 == 0:
            tn //= 2
    steps = rows // tr

    def body(a_ref, b_ref, o_ref, acc):
        t = pl.program_id(1)

        @pl.when(t == 0)
        def _():
            acc[...] = jnp.zeros(acc.shape, acc.dtype)

        acc[...] += _dot_tn(a_ref[...], b_ref[...])

        @pl.when(t == steps - 1)
        def _():
            o_ref[...] = acc[...].astype(o_ref.dtype)

    return _pcall(
        body,
        name="matmul_tn",
        grid=(n // tn, steps),
        in_specs=[pl.BlockSpec((tr, k), lambda j, t: (t, 0)), pl.BlockSpec((tr, tn), lambda j, t: (t, j))],
        out_specs=pl.BlockSpec((k, tn), lambda j, t: (0, j)),
        out_shape=jax.ShapeDtypeStruct((k, n), BF16),
        scratch_shapes=[pltpu.VMEM((k, tn), F32)],
        compiler_params=pltpu.CompilerParams(dimension_semantics=("parallel", "arbitrary"), vmem_limit_bytes=VMEM_LIMIT),
    )(a, b)


EXT = BM + 2 * HALO_F32


def _pool_positions(i, nlb, nrows, row0):
    is_ctx = i == nlb
    t = (i - jnp.where(is_ctx, nlb, 0)) * BM + row0 + lax.broadcasted_iota(jnp.int32, (nrows, 1), 0)
    return t, jnp.where(is_ctx, BM, nlb * BM)


def _pool_cnt(t, win, slen):
    return (jnp.minimum(t + win // 2, slen) - jnp.maximum(t - win // 2, 0)).astype(F32)


def _pool_fill_ext(ext, i, nlb, x_ref, xp_ref, xn_ref, mod_ref, nw_ref):
    first, last, _ = _stream_edges(i, nlb)
    sh, sc, nw = _mrow(mod_ref, SH1), _mrow(mod_ref, SC1), nw_ref[...]
    hcur, n, xhat, r = _norm_mod(x_ref[...], nw, sh, sc)
    ext[0:HALO_F32, :] = jnp.where(first, 0.0, _norm_mod(xp_ref[...], nw, sh, sc)[0])
    ext[HALO_F32 : HALO_F32 + BM, :] = hcur
    ext[HALO_F32 + BM :, :] = jnp.where(last, 0.0, _norm_mod(xn_ref[...], nw, sh, sc)[0])
    return n, xhat, r


def _window_sum(ref, cols, offs):
    acc = None
    for o in offs:
        v = ref[HALO_F32 + o : HALO_F32 + o + BM, cols]
        acc = v if acc is None else acc + v
    return acc


def _pool_diff(ext, g, win, t, slen):
    cols = slice(g * POOL_GROUP, (g + 1) * POOL_GROUP)
    ssum = _window_sum(ext, cols, range(-(win // 2), win // 2))
    return ssum / _pool_cnt(t, win, slen) - ext[HALO_F32 : HALO_F32 + BM, cols]


def pool_fwd(x, mod, nw1, pw, pbs, nblk, nlb):
    r = x.shape[0]

    def body(x_ref, xp_ref, xn_ref, mod_ref, nw_ref, pw_ref, pbs_ref, x1_ref, ypre_ref, ext):
        i = pl.program_id(0)
        _pool_fill_ext(ext, i, nlb, x_ref, xp_ref, xn_ref, mod_ref, nw_ref)
        t, slen = _pool_positions(i, nlb, BM, 0)
        for g, win in enumerate(POOL_WINDOWS):
            cols = slice(g * POOL_GROUP, (g + 1) * POOL_GROUP)
            diff = _pool_diff(ext, g, win, t, slen)
            ypre = _dotf(diff.astype(BF16), pw_ref[g]) + pbs_ref[0:1, cols]
            ypre_ref[:, cols] = ypre
            x1_ref[:, cols] = x_ref[:, cols] + mod_ref[0, G1 : G1 + 1, cols] * (ypre * pbs_ref[1:2, cols])

    return _rowcall(
        "pool_fwd", body, nblk, nlb,
        [(x, "row"), (x, "prev8"), (x, "next8"), (mod, "stream"), (nw1, "full"), _wk(pw), _wk(pbs)],
        [((r, D), F32, "row"), ((r, D), F32, "row")],
        scratch=[pltpu.VMEM((EXT, D), F32)],
    )


def pool_bwd(dx1, x, ypre, mod, nw1, pw, pbs, dep, nblk, nlb, lat_dx):
    r = x.shape[0]

    def body(d_ref, dp_ref, dn_ref, x_ref, xp_ref, xn_ref, ypre_ref, mod_ref, nw_ref, pw_ref, pbs_ref, dep_ref,
             dx_ref, dpw_ref, dpbs_ref, dnw_ref, dmod_ref, ext, dext, eext, dh):
        i = pl.program_id(0)
        first, last, _ = _stream_edges(i, nlb)
        _acc_init(i, dpw_ref, dpbs_ref, dnw_ref, dmod_ref)
        n, xhat, rr = _pool_fill_ext(ext, i, nlb, x_ref, xp_ref, xn_ref, mod_ref, nw_ref)
        g1, scale = _mrow(mod_ref, G1), pbs_ref[1:2, :]
        dcur = d_ref[...]
        ypre = ypre_ref[...]
        s = i // nlb
        dmod_ref[s, G1 : G1 + 1, :] += jnp.sum(dcur * (ypre * scale), axis=0, keepdims=True)
        dy = g1 * dcur
        dpbs_ref[1:2, :] += jnp.sum(dy * ypre, axis=0, keepdims=True)
        dpbs_ref[0:1, :] += jnp.sum(dy * scale, axis=0, keepdims=True)
        gs = g1 * scale
        dext[0:HALO_F32, :] = jnp.where(first, 0.0, gs * dp_ref[...])
        dext[HALO_F32 : HALO_F32 + BM, :] = dy * scale
        dext[HALO_F32 + BM :, :] = jnp.where(last, 0.0, gs * dn_ref[...])
        t, slen = _pool_positions(i, nlb, BM, 0)
        text, _ = _pool_positions(i, nlb, EXT, -HALO_F32)
        for g, win in enumerate(POOL_WINDOWS):
            cols = slice(g * POOL_GROUP, (g + 1) * POOL_GROUP)
            diff = _pool_diff(ext, g, win, t, slen)
            dpre = dext[:, cols].astype(BF16)
            ddiff = _dot_nt(dpre, pw_ref[g])
            eext[...] = ddiff / jnp.maximum(_pool_cnt(text, win, slen), 1.0)
            dh[:, cols] = _window_sum(eext, slice(None), range(-(win // 2) + 1, win // 2 + 1)) - ddiff[HALO_F32 : HALO_F32 + BM, :]
            dpw_ref[g] += _dot_tn(diff.astype(BF16), dpre[HALO_F32 : HALO_F32 + BM, :])
        sc, nw = _mrow(mod_ref, SC1), nw_ref[...]
        dxn, dsh, dsc, dnw = _norm_mod_bwd(dh[...], n, xhat, rr, nw, sc)
        if lat_dx:
            @pl.when(i < nlb)
            def _():
                dx_ref[...] = dcur + dxn
        else:
            dx_ref[...] = dcur + dxn
        dmod_ref[s, SH1 : SH1 + 1, :] += dsh
        dmod_ref[s, SC1 : SC1 + 1, :] += dsc
        dnw_ref[0:1, :] += dnw

    return _rowcall(
        "pool_bwd", body, nblk, nlb,
        [(dx1, "row"), (dx1, "prev8"), (dx1, "next8"), (x, "row"), (x, "prev8"), (x, "next8"), (ypre, "row"),
         (mod, "stream"), (nw1, "full"), _wk(pw), _wk(pbs), (dep, "any")],
        [((nlb * BM, D), F32, "row_lat") if lat_dx else ((r, D), F32, "row"), ((4, POOL_GROUP, POOL_GROUP), F32, "acc"), ((8, D), F32, "acc"), ((8, D), F32, "acc"),
         ((2, 8, D), F32, "acc")],
        scratch=[pltpu.VMEM((EXT, D), F32), pltpu.VMEM((EXT, D), F32), pltpu.VMEM((EXT, POOL_GROUP), F32), pltpu.VMEM((BM, D), F32)],
    )


def rope_tables(s, l):
    rows = s // GRID_W
    row = jnp.broadcast_to(jnp.arange(rows)[:, None], (rows, GRID_W)).reshape(-1).astype(F32)
    col = jnp.broadcast_to(jnp.arange(GRID_W)[None, :], (rows, GRID_W)).reshape(-1).astype(F32)
    axis_dim = HD // 2
    inv = ROPE_THETA ** (-jnp.arange(0, axis_dim, 2, dtype=F32) / axis_dim)
    ar, ac = row[:, None] * inv, col[:, None] * inv
    cos = jnp.concatenate([jnp.cos(ar), jnp.cos(ar), jnp.cos(ac), jnp.cos(ac)], axis=-1)
    sin = jnp.concatenate([-jnp.sin(ar), jnp.sin(ar), -jnp.sin(ac), jnp.sin(ac)], axis=-1)
    return (jnp.concatenate([cos, jnp.ones((l, HD), F32)], axis=0), jnp.concatenate([sin, jnp.zeros((l, HD), F32)], axis=0))


def _partner(x):
    q = HD // 4
    lane = lax.broadcasted_iota(jnp.int32, x.shape, 1)
    return jnp.where((lane // q) % 2 == 0, pltpu.roll(x, HD - q, 1), pltpu.roll(x, q, 1))


def _head_norm(raw, gain):
    r = lax.rsqrt(jnp.mean(raw * raw, axis=-1, keepdims=True) + EPS)
    return raw * r, r


ATTN_SCALE = HD ** -0.5


def qkv_fwd(x, mod, nw1, w_qkv, gains, cos_t, sin_t, nblk, nlb):
    r = x.shape[0]

    def body(x_ref, mod_ref, nw_ref, w_ref, g_ref, c_ref, s_ref, raw_ref, q_ref, k_ref, v_ref, h_ref):
        h = _norm_mod(x_ref[...], nw_ref[...], _mrow(mod_ref, SH1), _mrow(mod_ref, SC1))[0].astype(BF16)
        h_ref[...] = h
        raw_ref[...] = _dotf(h, w_ref[...])
        cos, sin = c_ref[...], s_ref[...]
        for j in range(ATTN_HEADS + ATTN_KV_HEADS):
            isq = j < ATTN_HEADS
            xn = _head_norm(raw_ref[:, j * HD : (j + 1) * HD], None)[0] * (g_ref[0:1, :] if isq else g_ref[1:2, :])
            rot = xn * cos + _partner(xn) * sin
            if isq:
                rot = rot * ATTN_SCALE
            rot = rot.astype(BF16)
            if isq:
                q_ref[:, j * HD : (j + 1) * HD] = rot
            else:
                k_ref[:, (j - ATTN_HEADS) * HD : (j - ATTN_HEADS + 1) * HD] = rot
        v_ref[...] = raw_ref[:, NQ + NKV :].astype(BF16)

    return _rowcall(
        "qkv_fwd", body, nblk, nlb,
        [(x, "row"), (mod, "stream"), (nw1, "full"), (w_qkv, "full"), (gains, "full"), (cos_t, "row"), (sin_t, "row")],
        [((r, NQ + 2 * NKV), F32, "row"), ((r, NQ), BF16, "row"), ((r, NKV), BF16, "row"), ((r, NKV), BF16, "row"),
         ((r, D), BF16, "row")],
    )


def attn_fwd(q, k, v, nblk, nlb):
    r = q.shape[0]

    def body(q_ref, k_ref, v_ref, o_ref, lse_ref):
        def heads(keys):
            for kvh in range(ATTN_KV_HEADS):
                kh = k_ref[keys, kvh * HD : (kvh + 1) * HD]
                vh = v_ref[keys, kvh * HD : (kvh + 1) * HD]
                for g in range(ATTN_GROUP):
                    cs = slice((kvh * ATTN_GROUP + g) * HD, (kvh * ATTN_GROUP + g + 1) * HD)
                    s = _dot_nt(q_ref[:, cs], kh)
                    m = jnp.max(s, axis=-1, keepdims=True)
                    p = jnp.exp(s - m)
                    l = jnp.sum(p, axis=-1, keepdims=True)
                    o_ref[:, cs] = (_dotf(p.astype(BF16), vh) / l).astype(BF16)
                    j = kvh * ATTN_GROUP + g
                    lse_ref[:, j : j + 1] = m + jnp.log(l)

        i = pl.program_id(0)
        pl.when(i < nlb)(lambda: heads(slice(0, r)))
        pl.when(i == nlb)(lambda: heads(slice(nlb * BM, r)))

    return _rowcall(
        "attn_fwd", body, nblk, nlb,
        [(q, "row"), (k, "full"), (v, "full")],
        [((r, NQ), BF16, "row"), ((r, ATTN_HEADS), F32, "row")],
    )


def attn_out_fwd(o, w_o, x, mod, nblk, nlb):
    r = x.shape[0]

    def body(o_ref, w_ref, x_ref, mod_ref, x1_ref, y_ref):
        y = _dotf(o_ref[...], w_ref[...])
        y_ref[...] = y
        x1_ref[...] = x_ref[...] + _mrow(mod_ref, G1) * y

    return _rowcall(
        "attn_out_fwd", body, nblk, nlb,
        [(o, "row"), (w_o, "full"), (x, "row"), (mod, "stream")],
        [((r, D), F32, "row"), ((r, D), F32, "row")],
    )


def mix_out_bwd(name, dx1, y, w_out, mod, dep, nblk, nlb):
    r = dx1.shape[0]
    kin = w_out.shape[0]

    def body(d_ref, y_ref, w_ref, mod_ref, dep_ref, dy_ref, do_ref, dmod_ref):
        i = pl.program_id(0)
        _acc_init(i, dmod_ref)
        d = d_ref[...]
        dmod_ref[i // nlb, G1 : G1 + 1, :] += jnp.sum(d * y_ref[...], axis=0, keepdims=True)
        dy = (_mrow(mod_ref, G1) * d).astype(BF16)
        dy_ref[...] = dy
        do_ref[...] = _dot_nt(dy, w_ref[...]).astype(do_ref.dtype)

    return _rowcall(
        name, body, nblk, nlb,
        [(dx1, "row"), (y, "row"), (w_out, "full"), (mod, "stream"), (dep, "any")],
        [((r, D), BF16, "row"), ((r, kin), BF16, "row"), ((2, 8, D), F32, "acc")],
    )


ATTN_KCHUNK = 11 * BM


def attn_bwd(q, k, v, o, do, lse, nblk, nlb):
    r = q.shape[0]
    kc = ATTN_KCHUNK if r % ATTN_KCHUNK == 0 else BM
    nkc = r // kc

    def body(q_ref, k_ref, v_ref, o_ref, do_ref, lse_ref, dq_ref, dk_ref, dv_ref):
        i = pl.program_id(0)
        _acc_init(i, dk_ref, dv_ref)

        def heads(chunks):
            for kvh in range(ATTN_KV_HEADS):
                ks = slice(kvh * HD, (kvh + 1) * HD)
                for g in range(ATTN_GROUP):
                    j = kvh * ATTN_GROUP + g
                    cs = slice(j * HD, (j + 1) * HD)
                    qh, doh = q_ref[:, cs], do_ref[:, cs]
                    delta = jnp.sum(doh.astype(F32) * o_ref[:, cs].astype(F32), axis=-1, keepdims=True)
                    lse = lse_ref[:, j : j + 1]
                    dq = jnp.zeros((BM, HD), F32)
                    for rs in chunks:
                        kh, vh = k_ref[rs, ks], v_ref[rs, ks]
                        p = jnp.exp(_dot_nt(qh, kh) - lse)
                        ds = (p * (_dot_nt(doh, vh) - delta)).astype(BF16)
                        dq = dq + _dotf(ds, kh)
                        dk_ref[rs, ks] += _dot_tn(ds, qh)
                        dv_ref[rs, ks] += _dot_tn(p.astype(BF16), doh)
                    dq_ref[:, cs] = dq * ATTN_SCALE

        pl.when(i < nlb)(lambda: heads([slice(c * kc, (c + 1) * kc) for c in range(nkc)]))
        pl.when(i == nlb)(lambda: heads([slice(nlb * BM, r)]))

    return _rowcall(
        "attn_bwd", body, nblk, nlb,
        [(q, "row"), (k, "full"), (v, "full"), (o, "row"), (do, "row"), (lse, "row")],
        [((r, NQ), F32, "row"), ((r, NKV), F32, "acc"), ((r, NKV), F32, "acc")],
    )


def qkv_bwd(dq, dk, dv, raw, gains, cos_t, sin_t, w_qkv, x, dx1, mod, nw1, nblk, nlb, ctx_dx_zero):
    r = x.shape[0]

    def body(dq_ref, dk_ref, dv_ref, raw_ref, g_ref, c_ref, s_ref, w_ref, x_ref, dx1_ref, mod_ref, nw_ref,
             dx_ref, draw_ref, dg_ref, dnw_ref, dmod_ref):
        i = pl.program_id(0)
        _acc_init(i, dg_ref, dnw_ref, dmod_ref)
        cos, sin = c_ref[...], s_ref[...]
        for j in range(ATTN_HEADS + ATTN_KV_HEADS):
            isq = j < ATTN_HEADS
            cs = slice(j * HD, (j + 1) * HD)
            dr = dq_ref[:, cs] if isq else dk_ref[:, (j - ATTN_HEADS) * HD : (j - ATTN_HEADS + 1) * HD]
            dxn = dr * cos + _partner(dr * sin)
            xhat, rr = _head_norm(raw_ref[:, cs], None)
            gi = 0 if isq else 1
            dg_ref[gi : gi + 1, :] += jnp.sum(dxn * xhat, axis=0, keepdims=True)
            dxhat = dxn * g_ref[gi : gi + 1, :]
            draw_ref[:, cs] = (rr * (dxhat - xhat * jnp.mean(dxhat * xhat, axis=-1, keepdims=True))).astype(BF16)
        draw_ref[:, NQ + NKV :] = dv_ref[...].astype(BF16)
        dh = _dot_nt(draw_ref[...], w_ref[...])
        sc, nw = _mrow(mod_ref, SC1), nw_ref[...]
        _, n, xhat, rr = _norm_mod(x_ref[...], nw, _mrow(mod_ref, SH1), sc)
        dxn, dsh, dsc, dnw = _norm_mod_bwd(dh, n, xhat, rr, nw, sc)
        dres = dx1_ref[...]
        if ctx_dx_zero:
            dres = jnp.where(i == nlb, 0.0, dres)
        dx_ref[...] = dres + dxn
        s = i // nlb
        dmod_ref[s, SH1 : SH1 + 1, :] += dsh
        dmod_ref[s, SC1 : SC1 + 1, :] += dsc
        dnw_ref[0:1, :] += dnw

    return _rowcall(
        "qkv_bwd", body, nblk, nlb,
        [(dq, "row"), (dk, "row"), (dv, "row"), (raw, "row"), (gains, "full"), (cos_t, "row"), (sin_t, "row"),
         (w_qkv, "full"), (x, "row"), (dx1, "row"), (mod, "stream"), (nw1, "full")],
        [((r, D), F32, "row"), ((r, NQ + 2 * NKV), BF16, "row"), ((8, HD), F32, "acc"), ((8, D), F32, "acc"),
         ((2, 8, D), F32, "acc")],
    )


RET_KSCALE = RET_DK ** -0.5
RET_HP = 2


def ret_in_fwd(x, mod, nw1, w_in, nblk, nlb):
    r = x.shape[0]

    def body(x_ref, mod_ref, nw_ref, w_ref, q_ref, k_ref, v_ref, g_ref, h_ref):
        h = _norm_mod(x_ref[...], nw_ref[...], _mrow(mod_ref, SH1), _mrow(mod_ref, SC1))[0].astype(BF16)
        h_ref[...] = h
        q_ref[...] = _dotf(h, w_ref[:, 0:RNQ]).astype(BF16)
        k_ref[...] = (_dotf(h, w_ref[:, RNQ : 2 * RNQ]) * RET_KSCALE).astype(BF16)
        v_ref[...] = _dotf(h, w_ref[:, 2 * RNQ : 2 * RNQ + RNV]).astype(BF16)
        g_ref[...] = _dotf(h, w_ref[:, 2 * RNQ + RNV :]).astype(BF16)

    return _rowcall(
        "ret_in_fwd", body, nblk, nlb,
        [(x, "row"), (mod, "stream"), (nw1, "full"), (w_in, "full")],
        [((r, RNQ), BF16, "row"), ((r, RNQ), BF16, "row"), ((r, RNV), BF16, "row"), ((r, RNV), BF16, "row"), ((r, D), BF16, "row")],
    )


def _log_sigmoid(x):
    return jnp.minimum(x, 0.0) - jnp.log(1.0 + jnp.exp(-jnp.abs(x)))


def _ret_decays(lg, reverse):
    c = BM
    i = lax.broadcasted_iota(jnp.int32, (c, c), 0)
    j = lax.broadcasted_iota(jnp.int32, (c, c), 1)
    diff = (j - i) if reverse else (i - j)
    ediff = jnp.maximum(diff, 0).astype(F32)
    dm = jnp.where(diff >= 0, jnp.exp(ediff * lg), 0.0)
    rr = lax.broadcasted_iota(jnp.int32, (c, 1), 0).astype(F32)
    eq = (c - rr) if reverse else (rr + 1.0)
    ek = rr if reverse else (c - 1.0 - rr)
    return dm, ediff, jnp.exp(eq * lg), eq, jnp.exp(ek * lg), ek, jnp.exp(c * lg)


def _ret_chunk_index(nlb):
    return (lambda s: jnp.where(s == 0, nlb, s - 1)), (lambda s: jnp.where(s == 0, nlb, nlb - s))


def ret_scan_fwd(q, k, v, logit_b, nlb):
    r = q.shape[0]
    nb = nlb + 1
    fidx, bidx = _ret_chunk_index(nlb)

    def body(qf, kf, vf, qb, kb, vb, lg_ref, of_ref, ob_ref, rf_ref, rb_ref, stf, stb):
        s = pl.program_id(1)

        @pl.when(s == 0)
        def _():
            stf[...] = jnp.zeros(stf.shape, F32)
            stb[...] = jnp.zeros(stb.shape, F32)

        for d, (q_ref, k_ref, v_ref, o_ref, rs_ref, st) in enumerate(((qf, kf, vf, of_ref, rf_ref, stf), (qb, kb, vb, ob_ref, rb_ref, stb))):
            for hp in range(RET_HP):
                kcs, vcs = slice(hp * RET_DK, (hp + 1) * RET_DK), slice(hp * RET_DV, (hp + 1) * RET_DV)
                lg = _log_sigmoid(lg_ref[d, hp])[0:1, 0:1]
                dm, _, qd, _, kd, _, gc = _ret_decays(lg, d == 1)
                qq, kk, vv, st0 = q_ref[:, kcs], k_ref[:, kcs], v_ref[:, vcs], st[hp]
                rs_ref[hp, 0] = st0
                a = _dot_nt(qq, kk) * dm
                o = _dotf(a.astype(BF16), vv) + _dotf(qq, st0.astype(BF16)) * qd
                o_ref[:, vcs] = jnp.where(s == 0, 0.0, o)
                st[hp] = st0 * gc + _dot_tn((kk.astype(F32) * kd).astype(BF16), vv)

    qspec = lambda f: pl.BlockSpec((BM, RET_HP * RET_DK), lambda h, s: (f(s), h))
    vspec = lambda f: pl.BlockSpec((BM, RET_HP * RET_DV), lambda h, s: (f(s), h))
    sspec = pl.BlockSpec((RET_HP, 1, RET_DK, RET_DV), lambda h, s: (h, s, 0, 0))
    return _pcall(
        body,
        name="ret_scan_fwd",
        grid=(RET_HEADS // RET_HP, nb),
        in_specs=[qspec(fidx), qspec(fidx), vspec(fidx), qspec(bidx), qspec(bidx), vspec(bidx),
                  pl.BlockSpec((2, RET_HP, 8, 128), lambda h, s: (0, h, 0, 0))],
        out_specs=[vspec(fidx), vspec(bidx), sspec, sspec],
        out_shape=[jax.ShapeDtypeStruct((r, RNV), F32), jax.ShapeDtypeStruct((r, RNV), F32),
                   jax.ShapeDtypeStruct((RET_HEADS, nb, RET_DK, RET_DV), F32), jax.ShapeDtypeStruct((RET_HEADS, nb, RET_DK, RET_DV), F32)],
        scratch_shapes=[pltpu.VMEM((RET_HP, RET_DK, RET_DV), F32), pltpu.VMEM((RET_HP, RET_DK, RET_DV), F32)],
        compiler_params=pltpu.CompilerParams(dimension_semantics=("parallel", "arbitrary"), vmem_limit_bytes=VMEM_LIMIT),
    )(q, k, v, q, k, v, logit_b)


def _group_norm(o):
    mu = jnp.mean(o, axis=-1, keepdims=True)
    oc = o - mu
    rstd = lax.rsqrt(jnp.mean(oc * oc, axis=-1, keepdims=True) + EPS)
    return oc * rstd, rstd


def ret_out_fwd(o_f, o_b, g, gnw, w_out, x, mod, nblk, nlb):
    r = x.shape[0]

    def body(of_ref, ob_ref, g_ref, gn_ref, w_ref, x_ref, mod_ref, x1_ref, y_ref, z_ref):
        for hh in range(RET_HEADS):
            cs = slice(hh * RET_DV, (hh + 1) * RET_DV)
            yhat, _ = _group_norm(of_ref[:, cs] + ob_ref[:, cs])
            gg = g_ref[:, cs].astype(F32)
            z_ref[:, cs] = (gg * _sigmoid(gg) * (yhat * gn_ref[0:1, cs])).astype(BF16)
        y = _dotf(z_ref[...], w_ref[...])
        y_ref[...] = y
        x1_ref[...] = x_ref[...] + _mrow(mod_ref, G1) * y

    return _rowcall(
        "ret_out_fwd", body, nblk, nlb,
        [(o_f, "row"), (o_b, "row"), (g, "row"), (gnw, "full"), (w_out, "full"), (x, "row"), (mod, "stream")],
        [((r, D), F32, "row"), ((r, D), F32, "row"), ((r, RNV), BF16, "row")],
    )


def ret_out_bwd(dx1, y, o_f, o_b, g, gnw, w_out, mod, dep, nblk, nlb):
    r = dx1.shape[0]

    def body(d_ref, y_ref, of_ref, ob_ref, g_ref, gn_ref, w_ref, mod_ref, dep_ref, dy_ref, do_ref, dg_ref, dgn_ref, dmod_ref, dz):
        i = pl.program_id(0)
        _acc_init(i, dgn_ref, dmod_ref)
        d = d_ref[...]
        dmod_ref[i // nlb, G1 : G1 + 1, :] += jnp.sum(d * y_ref[...], axis=0, keepdims=True)
        dy = (_mrow(mod_ref, G1) * d).astype(BF16)
        dy_ref[...] = dy
        dz[...] = _dot_nt(dy, w_ref[...])
        for hh in range(RET_HEADS):
            cs = slice(hh * RET_DV, (hh + 1) * RET_DV)
            yhat, rstd = _group_norm(of_ref[:, cs] + ob_ref[:, cs])
            gg = g_ref[:, cs].astype(F32)
            sg = _sigmoid(gg)
            gn = gn_ref[0:1, cs]
            dzz = dz[:, cs]
            dg_ref[:, cs] = (dzz * (yhat * gn) * (sg * (1.0 + gg * (1.0 - sg)))).astype(BF16)
            dyn = dzz * (gg * sg)
            dgn_ref[0:1, cs] += jnp.sum(dyn * yhat, axis=0, keepdims=True)
            dyh = dyn * gn
            do = rstd * (dyh - jnp.mean(dyh, axis=-1, keepdims=True) - yhat * jnp.mean(dyh * yhat, axis=-1, keepdims=True))
            do_ref[:, cs] = do.astype(BF16)

    return _rowcall(
        "ret_out_bwd", body, nblk, nlb,
        [(dx1, "row"), (y, "row"), (o_f, "row"), (o_b, "row"), (g, "row"), (gnw, "full"), (w_out, "full"), (mod, "stream"), (dep, "any")],
        [((r, D), BF16, "row"), ((r, RNV), BF16, "row"), ((r, RNV), BF16, "row"), ((8, RNV), F32, "acc"), ((2, 8, D), F32, "acc")],
        scratch=[pltpu.VMEM((BM, RNV), F32)],
    )


def ret_scan_bwd(q, k, v, do, st_f, st_b, logit_b, nlb):
    r = q.shape[0]
    nb = nlb + 1
    fidx, bidx = _ret_chunk_index(nlb)
    step = lambda t: nb - 1 - t

    def body(qf, kf, vf, dof, rf, qb, kb, vb, dob, rb, lg_ref,
             dqf, dkf, dvf, dqb, dkb, dvb, dlg_ref, drf, drb):
        t = pl.program_id(1)
        s = step(t)

        @pl.when(t == 0)
        def _():
            drf[...] = jnp.zeros(drf.shape, F32)
            drb[...] = jnp.zeros(drb.shape, F32)
            dlg_ref[...] = jnp.zeros(dlg_ref.shape, F32)

        dirs = ((qf, kf, vf, dof, rf, dqf, dkf, dvf, drf), (qb, kb, vb, dob, rb, dqb, dkb, dvb, drb))
        for d, (q_ref, k_ref, v_ref, do_ref, rs_ref, dq_ref, dk_ref, dv_ref, dr) in enumerate(dirs):
            for hp in range(RET_HP):
                kcs, vcs = slice(hp * RET_DK, (hp + 1) * RET_DK), slice(hp * RET_DV, (hp + 1) * RET_DV)
                lg = _log_sigmoid(lg_ref[d, hp])[0:1, 0:1]
                dm, ediff, qd, eq, kd, ek, gc = _ret_decays(lg, d == 1)
                qq, kk, vv = q_ref[:, kcs], k_ref[:, kcs], v_ref[:, vcs]
                dob16 = jnp.where(s == 0, jnp.zeros((), BF16), do_ref[:, vcs])
                do32 = dob16.astype(F32)
                st0 = rs_ref[hp, 0]
                st16 = st0.astype(BF16)
                dr0 = dr[hp]
                dr16 = dr0.astype(BF16)
                a = _dot_nt(qq, kk) * dm
                daf = _dot_nt(dob16, vv)
                ds = (daf * dm).astype(BF16)
                qr = _dotf(qq, st16)
                k32 = kk.astype(F32)
                kdec = (k32 * kd).astype(BF16)
                dv_ref[:, vcs] = (_dot_tn(a.astype(BF16), dob16) + _dotf(kdec, dr16)).astype(BF16)
                dq_ref[:, kcs] = (_dotf(ds, kk) + _dot_nt(dob16, st16) * qd).astype(BF16)
                vdr = _dot_nt(vv, dr16)
                dk_ref[:, kcs] = (_dot_tn(ds, qq) + vdr * kd).astype(BF16)
                tot = (jnp.sum(daf * a * ediff)
                       + jnp.sum(eq * qd * jnp.sum(do32 * qr, axis=-1, keepdims=True))
                       + jnp.sum(ek * kd * jnp.sum(k32 * vdr, axis=-1, keepdims=True))
                       + jnp.sum(BM * gc * jnp.sum(dr0 * st0, axis=-1, keepdims=True)))
                dlg_ref[d, hp] += tot
                dr[hp] = gc * dr0 + _dot_tn(qq, (do32 * qd).astype(BF16))

        @pl.when(t == nb - 1)
        def _():
            dlg_ref[...] = dlg_ref[...] * _sigmoid(-lg_ref[...])

    qspec = lambda f: pl.BlockSpec((BM, RET_HP * RET_DK), lambda h, t: (f(step(t)), h))
    vspec = lambda f: pl.BlockSpec((BM, RET_HP * RET_DV), lambda h, t: (f(step(t)), h))
    sspec = pl.BlockSpec((RET_HP, 1, RET_DK, RET_DV), lambda h, t: (h, step(t), 0, 0))
    lspec = pl.BlockSpec((2, RET_HP, 8, 128), lambda h, t: (0, h, 0, 0))
    sq, sv = jax.ShapeDtypeStruct((r, RNQ), BF16), jax.ShapeDtypeStruct((r, RNV), BF16)
    return _pcall(
        body,
        name="ret_scan_bwd",
        grid=(RET_HEADS // RET_HP, nb),
        in_specs=[qspec(fidx), qspec(fidx), vspec(fidx), vspec(fidx), sspec,
                  qspec(bidx), qspec(bidx), vspec(bidx), vspec(bidx), sspec, lspec],
        out_specs=[qspec(fidx), qspec(fidx), vspec(fidx), qspec(bidx), qspec(bidx), vspec(bidx), lspec],
        out_shape=[sq, sq, sv, sq, sq, sv, jax.ShapeDtypeStruct((2, RET_HEADS, 8, 128), F32)],
        scratch_shapes=[pltpu.VMEM((RET_HP, RET_DK, RET_DV), F32), pltpu.VMEM((RET_HP, RET_DK, RET_DV), F32)],
        compiler_params=pltpu.CompilerParams(dimension_semantics=("parallel", "arbitrary"), vmem_limit_bytes=VMEM_LIMIT),
    )(q, k, v, do, st_f, q, k, v, do, st_b, logit_b)


def ret_in_bwd(dqf, dkf, dvf, dqb, dkb, dvb, dgate, w_in, x, dx1, mod, nw1, nblk, nlb):
    r = x.shape[0]
    nin = 2 * RNQ + 2 * RNV

    def body(dqf_ref, dkf_ref, dvf_ref, dqb_ref, dkb_ref, dvb_ref, dg_ref, w_ref, x_ref, dx1_ref, mod_ref, nw_ref,
             dx_ref, din_ref, dnw_ref, dmod_ref):
        i = pl.program_id(0)
        _acc_init(i, dnw_ref, dmod_ref)
        is_ctx = i == nlb
        din_ref[:, 0:RNQ] = (dqf_ref[...].astype(F32) + dqb_ref[...].astype(F32)).astype(BF16)
        din_ref[:, RNQ : 2 * RNQ] = ((dkf_ref[...].astype(F32) + dkb_ref[...].astype(F32)) * RET_KSCALE).astype(BF16)
        din_ref[:, 2 * RNQ : 2 * RNQ + RNV] = (dvf_ref[...].astype(F32) + dvb_ref[...].astype(F32)).astype(BF16)
        din_ref[:, 2 * RNQ + RNV :] = jnp.where(is_ctx, jnp.zeros((), BF16), dg_ref[...])
        dh = _dot_nt(din_ref[...], w_ref[...])
        sc, nw = _mrow(mod_ref, SC1), nw_ref[...]
        _, n, xhat, rr = _norm_mod(x_ref[...], nw, _mrow(mod_ref, SH1), sc)
        dxn, dsh, dsc, dnw = _norm_mod_bwd(dh, n, xhat, rr, nw, sc)
        dx_ref[...] = jnp.where(is_ctx, 0.0, dx1_ref[...]) + dxn
        s = i // nlb
        dmod_ref[s, SH1 : SH1 + 1, :] += dsh
        dmod_ref[s, SC1 : SC1 + 1, :] += dsc
        dnw_ref[0:1, :] += dnw

    return _rowcall(
        "ret_in_bwd", body, nblk, nlb,
        [(dqf, "row"), (dkf, "row"), (dvf, "row"), (dqb, "row"), (dkb, "row"), (dvb, "row"), (dgate, "row"),
         (w_in, "full"), (x, "row"), (dx1, "row"), (mod, "stream"), (nw1, "full")],
        [((r, D), F32, "row"), ((r, nin), BF16, "row"), ((8, D), F32, "acc"), ((2, 8, D), F32, "acc")],
    )


def loss_head(xout, target, nlb):
    r = xout.shape[0]

    def body(x_ref, t_ref, dx_ref, l_ref):
        _acc_init(pl.program_id(0), l_ref)
        err = x_ref[...] - t_ref[...]
        dx_ref[...] = err * (1.0 / D)
        l_ref[...] += 0.5 * jnp.sum(jnp.mean(err * err, axis=-1, keepdims=True))

    return _rowcall(
        "loss_head", body, nlb, nlb,
        [(xout, "row"), (target, "row")],
        [((r, D), F32, "row"), ((8, 128), F32, "acc")],
    )


N_MIXERS = 3
POOL, ATTN, RET = range(3)


def _layer_plan(depth):
    plan = []
    for i in range(depth):
        kind = i % N_MIXERS
        ctx_out = any(k % N_MIXERS != POOL for k in range(i + 1, depth))
        plan.append((kind, i // N_MIXERS, ctx_out or kind != POOL, ctx_out))
    return plan


def local_step(xs, target, mods, w, nlb, depth, fetch, emit):
    nb = nlb + 1
    plan = _layer_plan(depth)
    saved = []
    x = xs
    for i, (kind, j, ctx_in, ctx_out) in enumerate(plan):
        nmix = nb if ctx_out else nlb
        mod, nw1, nw2 = mods[i], w["nw"][i, 0:1], w["nw"][i, 1:2]
        lw = fetch(i, MIX, x)
        sv = {"x": x, "lw": lw}
        if kind == POOL:
            x1, sv["ypre"] = pool_fwd(x, mod, nw1, lw["pool_w"], (w["pbs"], j), nmix, nlb)
        elif kind == ATTN:
            assert ctx_out
            sv["raw"], sv["q"], sv["k"], sv["v"], sv["h"] = qkv_fwd(x, mod, nw1, lw["attn_w_qkv"], w["gains"], w["cos"], w["sin"], nb, nlb)
            sv["o"], sv["lse"] = attn_fwd(sv["q"], sv["k"], sv["v"], nb, nlb)
            x1, sv["y"] = attn_out_fwd(sv["o"], lw["attn_w_o"], x, mod, nb, nlb)
        else:
            assert ctx_in and not ctx_out
            sv["q"], sv["k"], sv["v"], sv["g"], sv["h"] = ret_in_fwd(x, mod, nw1, lw["ret_w_in"], nb, nlb)
            sv["o_f"], sv["o_b"], sv["st_f"], sv["st_b"] = ret_scan_fwd(sv["q"], sv["k"], sv["v"], w["logit_b"], nlb)
            x1, sv["y"], sv["z"] = ret_out_fwd(sv["o_f"], sv["o_b"], sv["g"], w["gnw"], lw["ret_w_out"], x, mod, nlb, nlb)
        sv["x1"] = x1
        lw.update(fetch(i, FFN, x1))
        sv["u"], sv["h2"] = ffn_up(x1, mod, nw2, lw["ffn_w_up"], nmix, nlb)
        x, sv["f"], sv["uc"], sv["gated"] = ffn_down(sv["u"], (w["cw"], i), lw["ffn_w_down"], x1, mod, nmix, nlb)
        saved.append(sv)

    dx, loss_tile = loss_head(x, target, nlb)
    g = {k: [None] * depth for k in ("dcw", "dnw1", "dnw2", "dmod")}
    dep = loss_tile
    for i in reversed(range(depth)):
        kind, j, ctx_in, ctx_out = plan[i]
        sv = saved[i]
        lw, big = sv["lw"], {}
        nmix = nb if ctx_out else nlb
        mod, nw1, nw2 = mods[i], w["nw"][i, 0:1], w["nw"][i, 1:2]
        df, duc, dcb, dmod = ffn_bwd1(dx, sv["f"], sv["uc"], lw["ffn_w_down"], mod, dep, nmix, nlb)
        big["ffn_w_down"] = matmul_tn(sv["gated"], df, nmix)
        du, dx1, dcw, g["dnw2"][i], dm = ffn_bwd3(duc, sv["u"], (w["cw"], i), lw["ffn_w_up"], sv["x1"], dx, mod, nw2, nmix, nlb)
        g["dcw"][i] = dcw + dcb
        dmod = dmod + dm
        big["ffn_w_up"] = matmul_tn(sv["h2"], du, nmix)
        dep, big = emit(i, FFN, big), {}
        if kind == POOL:
            dx, dpw, dpbs, g["dnw1"][i], dm = pool_bwd(dx1, sv["x"], sv["ypre"], mod, nw1, lw["pool_w"], (w["pbs"], j), dep, nmix, nlb, i == 0)
            big["pool_w"] = dpw.astype(BF16)
            g.setdefault("dpbs", {})[j] = dpbs
        elif kind == ATTN:
            dy, do, dm1 = mix_out_bwd("attn_out_bwd", dx1, sv["y"], lw["attn_w_o"], mod, dep, nb, nlb)
            big["attn_w_o"] = matmul_tn(sv["o"], dy, nb)
            dq, dk, dv = attn_bwd(sv["q"], sv["k"], sv["v"], sv["o"], do, sv["lse"], nb, nlb)
            dx, draw, g["dgains"], g["dnw1"][i], dm = qkv_bwd(
                dq, dk, dv, sv["raw"], w["gains"], w["cos"], w["sin"], lw["attn_w_qkv"], sv["x"], dx1, mod, nw1, nb, nlb, False)
            big["attn_w_qkv"] = matmul_tn(sv["h"], draw, nb)
            dm = dm + dm1
        else:
            dy, do, dgate, g["dgnw"], dm1 = ret_out_bwd(dx1, sv["y"], sv["o_f"], sv["o_b"], sv["g"], w["gnw"], lw["ret_w_out"], mod, dep, nlb, nlb)
            big["ret_w_out"] = matmul_tn(sv["z"], dy, nlb)
            dqf, dkf, dvf, dqb, dkb, dvb, g["dlogit"] = ret_scan_bwd(sv["q"], sv["k"], sv["v"], do, sv["st_f"], sv["st_b"], w["logit_b"], nlb)
            dx, din, g["dnw1"][i], dm = ret_in_bwd(dqf, dkf, dvf, dqb, dkb, dvb, dgate, lw["ret_w_in"], sv["x"], dx1, mod, nw1, nb, nlb)
            big["ret_w_in"] = matmul_tn(sv["h"], din, nb)
            dm = dm + dm1
        g["dmod"][i] = dmod + dm
        dep = emit(i, MIX, big)
    return loss_tile, dx, g


MESH_ID = pl.DeviceIdType.MESH
CHIP_FLIPS = ((1, 0), (0, 1), (1, 1))


def _pos():
    return lax.axis_index("x"), lax.axis_index("y"), lax.axis_index("c")


def _flip(v, b):
    return 1 - v if b else v


def small_all_gather(name, x):
    rows, n = x.shape

    def body(x_ref, out_ref, send_sems, recv_sems, local_sem):
        mx, my, mc = _pos()
        me = 4 * mx + 2 * my + mc
        mine = pltpu.make_async_copy(x_ref, out_ref.at[me], local_sem)
        mine.start()
        sends, peers = [], []
        for kk in range(1, N_DEV):
            peer = (_flip(mx, (kk >> 2) & 1), _flip(my, (kk >> 1) & 1), _flip(mc, kk & 1))
            cp = pltpu.make_async_remote_copy(src_ref=x_ref, dst_ref=out_ref.at[me], send_sem=send_sems.at[kk - 1],
                                              recv_sem=recv_sems.at[kk - 1], device_id=peer, device_id_type=MESH_ID)
            cp.start()
            sends.append(cp)
            peers.append(peer)
        for kk, peer in enumerate(peers):
            pidx = 4 * peer[0] + 2 * peer[1] + peer[2]
            pltpu.make_async_remote_copy(src_ref=x_ref, dst_ref=out_ref.at[pidx], send_sem=send_sems.at[kk],
                                         recv_sem=recv_sems.at[kk], device_id=peer, device_id_type=MESH_ID).wait_recv()
        for cp in sends:
            cp.wait_send()
        mine.wait()

    return _pcall(
        body,
        name=name,
        out_shape=jax.ShapeDtypeStruct((N_DEV, rows, n), x.dtype),
        in_specs=[pl.BlockSpec(memory_space=pltpu.VMEM)],
        out_specs=pl.BlockSpec(memory_space=pltpu.VMEM),
        scratch_shapes=[pltpu.SemaphoreType.DMA((N_DEV - 1,)), pltpu.SemaphoreType.DMA((N_DEV - 1,)), pltpu.SemaphoreType.DMA],
        compiler_params=pltpu.CompilerParams(vmem_limit_bytes=VMEM_LIMIT),
    )(x)


def _hbm_exchange(name, ins, out_shapes, plan):
    n_in = len(ins)
    probe_local, probe_remote = plan([None] * n_in, [None] * len(out_shapes), probe=True)

    def body(*refs):
        in_refs, out_refs = refs[:n_in], refs[n_in : n_in + len(out_shapes)]
        send_sems, recv_sems, local_sems = refs[n_in + len(out_shapes) :]
        local, remote = plan(in_refs, out_refs, probe=False)
        lcs = [pltpu.make_async_copy(s, d, local_sems.at[k]) for k, (s, d) in enumerate(local)]
        for cp in lcs:
            cp.start()
        rcs = []
        for k, (s, d, peer, _) in enumerate(remote):
            cp = pltpu.make_async_remote_copy(src_ref=s, dst_ref=d, send_sem=send_sems.at[k], recv_sem=recv_sems.at[k],
                                              device_id=peer, device_id_type=MESH_ID)
            cp.start()
            rcs.append(cp)
        for k, (s, _, peer, here) in enumerate(remote):
            pltpu.make_async_remote_copy(src_ref=s, dst_ref=here, send_sem=send_sems.at[k], recv_sem=recv_sems.at[k],
                                         device_id=peer, device_id_type=MESH_ID).wait_recv()
        for cp in rcs:
            cp.wait_send()
        for cp in lcs:
            cp.wait()

    return _pcall(
        body,
        name=name,
        out_shape=list(out_shapes),
        in_specs=[pl.BlockSpec(memory_space=pl.ANY)] * n_in,
        out_specs=[pl.BlockSpec(memory_space=pl.ANY)] * len(out_shapes),
        scratch_shapes=[pltpu.SemaphoreType.DMA((max(probe_remote, 1),)), pltpu.SemaphoreType.DMA((max(probe_remote, 1),)),
                        pltpu.SemaphoreType.DMA((max(probe_local, 1),))],
    )(*ins)


def _at_axis(ref, axis, start, size):
    return ref.at[tuple(pl.ds(start, size) if a == axis else slice(None) for a in range(len(ref.shape)))]


HBM_SPEC = pl.BlockSpec(memory_space=pltpu.HBM)
SEM_SPEC = pl.BlockSpec(memory_space=pltpu.SEMAPHORE)
SIDE_EFFECT = pltpu.SideEffectType.DATAFLOW_SIDE_EFFECTING


def _in_hbm(a):
    return pltpu.with_memory_space_constraint(a, pltpu.HBM)


def _copies_start(name, bufs, counts, plan, after=None):
    n, ng = len(bufs), len(counts)
    extra = [] if after is None else [after]
    n_in = n + len(extra)

    def body(*refs):
        sems = refs[n_in : n_in + 2 * ng]
        token = refs[n_in + 2 * ng + n]
        for gi, copies in enumerate(plan(refs[:n])):
            for k, (s, d, peer) in enumerate(copies):
                pltpu.make_async_remote_copy(src_ref=s, dst_ref=d, send_sem=sems[2 * gi].at[k], recv_sem=sems[2 * gi + 1].at[k],
                                             device_id=peer, device_id_type=MESH_ID).start()
        token[...] = jnp.zeros(token.shape, token.dtype)

    out = _pcall(
        body,
        name=name,
        out_shape=tuple(pltpu.SemaphoreType.DMA((c,)) for c in counts for _ in range(2))
        + tuple(pltpu.HBM(b.shape, b.dtype) for b in bufs) + (jax.ShapeDtypeStruct((8, 128), F32),),
        in_specs=(HBM_SPEC,) * n + (pl.BlockSpec(memory_space=pl.ANY),) * len(extra),
        out_specs=(SEM_SPEC,) * (2 * ng) + (HBM_SPEC,) * n + (pl.BlockSpec(memory_space=pltpu.VMEM),),
        input_output_aliases={i: 2 * ng + i for i in range(n)},
        compiler_params=pltpu.CompilerParams(has_side_effects=SIDE_EFFECT),
    )(*[_in_hbm(b) for b in bufs], *extra)
    return [(out[2 * g], out[2 * g + 1]) for g in range(ng)], list(out[2 * ng : 2 * ng + n]), out[2 * ng + n]


def _copies_wait(name, sems, bufs, plan, after):
    n, ng = len(bufs), len(sems)

    def body(*refs):
        for gi, copies in enumerate(plan(refs[:n])):
            for k, (s, d, peer) in enumerate(copies):
                cp = pltpu.make_async_remote_copy(src_ref=s, dst_ref=d, send_sem=refs[n + 2 * gi].at[k], recv_sem=refs[n + 2 * gi + 1].at[k],
                                                  device_id=peer, device_id_type=MESH_ID)
                cp.wait_send()
                cp.wait_recv()

    out = _pcall(
        body,
        name=name,
        out_shape=tuple(pltpu.HBM(b.shape, b.dtype) for b in bufs),
        in_specs=(HBM_SPEC,) * n + (SEM_SPEC,) * (2 * ng) + (pl.BlockSpec(memory_space=pl.ANY),),
        out_specs=(HBM_SPEC,) * n,
        input_output_aliases={i: i for i in range(n)},
        compiler_params=pltpu.CompilerParams(has_side_effects=SIDE_EFFECT),
    )(*bufs, *[s for pair in sems for s in pair], after)
    return list(out)


def _matrix_groups(depth):
    out = []
    for i, (kind, j, _, _) in enumerate(_layer_plan(depth)):
        out.append(([("pool_w", j, 1)], [("attn_w_qkv", j, 1), ("attn_w_o", j, 0)], [("ret_w_in", j, 1), ("ret_w_out", j, 0)])[kind])
        out.append([("ffn_w_up", i, 1), ("ffn_w_down", i, 0)])
    return out


MIX, FFN = 0, 1


def _peers(mx, my, mc):
    out = []
    for fx, fy in CHIP_FLIPS:
        px, py = _flip(mx, fx), _flip(my, fy)
        out.append(((px, py, mc), 2 * px + py))
    return out


def full_buffers(shards, names, layers):
    out = []
    for name, _, axis in [e for layer in layers for e in layer]:
        shp = list(shards[names.index(name)].shape[1:])
        shp[axis] *= N_CHIP
        out.append(lax.empty(tuple(shp), BF16))
    return out


def _gather_plan(names, layers, group, n_shards, here):
    def plan(refs):
        mx, my, mc = _pos()
        s_refs, f_refs = refs[:n_shards], refs[n_shards:]
        groups, k = [], 0
        for gi, layer in enumerate(layers):
            if group is not None and gi != group:
                continue
            copies = []
            for name, idx, axis in layer:
                src = s_refs[names.index(name)].at[idx]
                n = src.shape[axis]
                for peer, pchip in _peers(mx, my, mc) + [((mx, my, 1 - mc), 2 * mx + my)]:
                    at = pchip if here else 2 * mx + my
                    copies.append((src, _at_axis(f_refs[k], axis, at * n, n), peer))
                k += 1
            groups.append(copies)
        return groups

    return plan


def gather_start(shards, names, layers, fulls, after):
    counts = [len(layer) * (len(CHIP_FLIPS) + 1) for layer in layers]
    sems, bufs, _ = _copies_start("gather_start", list(shards) + list(fulls), counts,
                                  _gather_plan(names, layers, None, len(shards), False), after)
    return sems, bufs[: len(shards)], bufs[len(shards) :]


def gather_wait(g, sems_g, shards, names, layers, fulls_g, after):
    bufs = _copies_wait(f"gather_wait_{g}", [sems_g], list(shards) + list(fulls_g), _gather_plan(names, layers, g, len(shards), True), after)
    return bufs[: len(shards)], bufs[len(shards) :]


def _scatter_plan(layer_entries, n_grads, land_of):
    def plan(refs):
        mx, my, mc = _pos()
        groups, k = [], 0
        for layer in layer_entries:
            copies = []
            for name, idx, axis in layer:
                gref, land = refs[k], refs[n_grads + land_of(k, name)]
                n = gref.shape[axis] // N_CHIP
                for slot, (peer, pchip) in enumerate(_peers(mx, my, mc)):
                    copies.append((_at_axis(gref, axis, pchip * n, n), land.at[slot, idx], peer))
                k += 1
            groups.append(copies)
        return groups

    return plan


def scatter_start(i, layer, grads, lands):
    sems, bufs, token = _copies_start(f"scatter_start_{i}", list(grads) + list(lands), [len(layer) * len(CHIP_FLIPS)],
                                      _scatter_plan([layer], len(grads), lambda k, name: k))
    return sems[0], bufs[: len(grads)], bufs[len(grads) :], token


def scatter_wait(sems, layers, grads, names, lands, after):
    bufs = _copies_wait("scatter_wait", sems, list(grads) + list(lands),
                        _scatter_plan(layers, len(grads), lambda k, name: names.index(name)), after)
    return bufs[: len(grads)], bufs[len(grads) :]


def sibling_swap(parts):
    def plan(in_refs, out_refs, probe):
        if probe:
            return 0, len(parts)
        mx, my, mc = _pos()
        return [], [(s, o, (mx, my, 1 - mc), o) for s, o in zip(in_refs, out_refs)]

    return _hbm_exchange("sibling_swap", parts, [jax.ShapeDtypeStruct(p.shape, p.dtype) for p in parts], plan)


EW_ROWS = 256


def _ew_call(name, fn, ins, n_out):
    rows, cols = ins[0].shape[-2:]
    tr = EW_ROWS if rows % EW_ROWS == 0 else rows

    def body(*refs):
        outs = fn(*[r[...] for r in refs[: len(ins)]])
        for o_ref, o in zip(refs[len(ins) :], outs):
            o_ref[...] = o

    def spec(a):
        if a.ndim == 3:
            return pl.BlockSpec((a.shape[0], tr, cols), lambda i: (0, i, 0))
        return pl.BlockSpec((tr, cols), lambda i: (i, 0))

    return _pcall(
        body,
        name=name,
        grid=(rows // tr,),
        in_specs=[spec(a) for a in ins],
        out_specs=[pl.BlockSpec((tr, cols), lambda i: (i, 0))] * n_out,
        out_shape=[jax.ShapeDtypeStruct((rows, cols), F32)] * n_out,
        compiler_params=pltpu.CompilerParams(dimension_semantics=("parallel",), vmem_limit_bytes=VMEM_LIMIT),
    )(*ins)


def _adamw(w, g, m, v):
    m = ADAM_B1 * m + (1.0 - ADAM_B1) * g
    v = ADAM_B2 * v + (1.0 - ADAM_B2) * (g * g)
    m_hat = m / (1.0 - ADAM_B1 ** ADAM_STEP)
    v_hat = v / (1.0 - ADAM_B2 ** ADAM_STEP)
    return -ADAM_LR * (m_hat / (jnp.sqrt(v_hat) + ADAM_EPS) + ADAM_WD * w), m, v


def sum_slots(name, own, landing):
    def fn(o, l):
        acc = o.astype(F32)
        for k in range(l.shape[0]):
            acc = acc + l[k].astype(F32)
        return (acc,)

    return _ew_call(name, fn, [own, landing], 1)[0]


def adamw_pair(name, w, m, v, p, ps):
    def fn(w, m, v, p, ps):
        g = p + ps
        return (g,) + _adamw(w, g, m, v)

    return _ew_call(name, fn, [w, m, v, p, ps], 4)


def adamw_one(name, w, m, v, g):
    return _ew_call(name, lambda w, m, v, g: _adamw(w, g, m, v), [w, m, v, g], 3)


def reduce_devices(name, x):
    def fn(a):
        acc = a[0]
        for k in range(1, a.shape[0]):
            acc = acc + a[k]
        return (acc,)

    return _ew_call(name, fn, [x], 1)[0]


ADA_ROWS = 16
ADA_CTX = N_DEV


def ada_fwd(s9, ada_w, ada_b):
    depth, _, n = ada_w.shape

    def body(s_ref, w_ref, b_ref, o_ref):
        s = s_ref[...]
        o_ref[...] = _dotf((s * _sigmoid(s)).astype(BF16), w_ref[...].astype(BF16)) + b_ref[...]

    return _pcall(
        body,
        name="ada_fwd",
        grid=(depth,),
        in_specs=[pl.BlockSpec((ADA_ROWS, D), lambda i: (0, 0)), pl.BlockSpec((None, D, n), lambda i: (i, 0, 0)),
                  pl.BlockSpec((None, 1, n), lambda i: (i, 0, 0))],
        out_specs=pl.BlockSpec((None, ADA_ROWS, n), lambda i: (i, 0, 0)),
        out_shape=jax.ShapeDtypeStruct((depth, ADA_ROWS, n), F32),
        compiler_params=pltpu.CompilerParams(dimension_semantics=("arbitrary",), vmem_limit_bytes=VMEM_LIMIT),
    )(s9, ada_w, ada_b)


def ada_bwd(s9, dm, ada_w):
    depth, _, n = ada_w.shape

    def body(s_ref, dm_ref, w_ref, gw_ref, ds_ref):
        _acc_init(pl.program_id(0), ds_ref)
        s = s_ref[...]
        dmb = dm_ref[...].astype(BF16)
        gw_ref[...] = _dot_tn((s * _sigmoid(s)).astype(BF16), dmb)
        ds_ref[...] += _dot_nt(dmb, w_ref[...].astype(BF16))

    return _pcall(
        body,
        name="ada_bwd",
        grid=(depth,),
        in_specs=[pl.BlockSpec((ADA_ROWS, D), lambda i: (0, 0)), pl.BlockSpec((None, ADA_ROWS, n), lambda i: (i, 0, 0)),
                  pl.BlockSpec((None, D, n), lambda i: (i, 0, 0))],
        out_specs=[pl.BlockSpec((None, D, n), lambda i: (i, 0, 0)), pl.BlockSpec((ADA_ROWS, D), lambda i: (0, 0))],
        out_shape=[jax.ShapeDtypeStruct((depth, D, n), F32), jax.ShapeDtypeStruct((ADA_ROWS, D), F32)],
        compiler_params=pltpu.CompilerParams(dimension_semantics=("arbitrary",), vmem_limit_bytes=VMEM_LIMIT),
    )(s9, dm, ada_w)


def cctx_grad(parts, c_ctx):
    def body(p_ref, c_ref, o_ref):
        acc = p_ref[0, ADA_CTX : ADA_CTX + 1, :]
        for chip in range(1, N_CHIP):
            acc = acc + p_ref[2 * chip, ADA_CTX : ADA_CTX + 1, :]
        c = c_ref[...]
        sg = _sigmoid(c)
        o_ref[...] = acc * (sg * (1.0 + c * (1.0 - sg)))

    return _pcall(body, name="cctx_grad", out_shape=jax.ShapeDtypeStruct((1, D), F32))(parts, c_ctx)


def _pack(arrs):
    flat = jnp.concatenate([a.astype(F32).reshape(-1) for a in arrs])
    rows = -(-flat.shape[0] // (8 * D)) * 8
    return jnp.pad(flat, (0, rows * D - flat.shape[0])).reshape(rows, D)


def _unpack(slab, shapes):
    lead = slab.shape[:-2]
    flat = slab.reshape(lead + (-1,))
    out, off = [], 0
    for shp in shapes:
        n = 1
        for d in shp:
            n *= d
        out.append(flat[..., off : off + n].reshape(lead + tuple(shp)))
        off += n
    return out


def _unshard(per_dev, axis):
    return jnp.concatenate([per_dev[2 * chip] for chip in range(N_CHIP)], axis=axis)


BIG = (("pool_w", 1), ("attn_w_qkv", 1), ("attn_w_o", 0), ("ret_w_in", 1), ("ret_w_out", 0), ("ffn_w_up", 1), ("ffn_w_down", 0))
WEIGHTS = ("c_ctx", "ada_w", "ada_b", "norm_w", "pool_w", "pool_b", "pool_scale", "attn_w_qkv", "attn_q_gain", "attn_k_gain",
           "attn_w_o", "ret_w_in", "ret_decay_logit", "ret_gn_w", "ret_w_out", "ffn_w_up", "ffn_conv_w", "ffn_conv_b", "ffn_w_down")
SMALL = tuple(n for n in WEIGHTS if n != "ada_w" and n not in dict(BIG))
SMALL_SHARD_AXIS = {"norm_w": 2, "pool_b": 1, "pool_scale": 1, "ret_gn_w": 1, "ffn_conv_w": 2}


def kernel(x, c, ctx, c_ctx, ada_w, ada_b, norm_w, pool_w, pool_b, pool_scale, attn_w_qkv, attn_q_gain, attn_k_gain, attn_w_o, ret_w_in, ret_decay_logit, ret_gn_w, ret_w_out, ffn_w_up, ffn_conv_w, ffn_conv_b, ffn_w_down, loss_target, m_c_ctx, m_ada_w, m_ada_b, m_norm_w, m_pool_w, m_pool_b, m_pool_scale, m_attn_w_qkv, m_attn_q_gain, m_attn_k_gain, m_attn_w_o, m_ret_w_in, m_ret_decay_logit, m_ret_gn_w, m_ret_w_out, m_ffn_w_up, m_ffn_conv_w, m_ffn_conv_b, m_ffn_w_down, v_c_ctx, v_ada_w, v_ada_b, v_norm_w, v_pool_w, v_pool_b, v_pool_scale, v_attn_w_qkv, v_attn_q_gain, v_attn_k_gain, v_attn_w_o, v_ret_w_in, v_ret_decay_logit, v_ret_gn_w, v_ret_w_out, v_ffn_w_up, v_ffn_conv_w, v_ffn_conv_b, v_ffn_w_down):
    P = dict(zip(WEIGHTS, (c_ctx, ada_w, ada_b, norm_w, pool_w, pool_b, pool_scale, attn_w_qkv, attn_q_gain, attn_k_gain, attn_w_o,
                           ret_w_in, ret_decay_logit, ret_gn_w, ret_w_out, ffn_w_up, ffn_conv_w, ffn_conv_b, ffn_w_down)))
    M = dict(zip(WEIGHTS, (m_c_ctx, m_ada_w, m_ada_b, m_norm_w, m_pool_w, m_pool_b, m_pool_scale, m_attn_w_qkv, m_attn_q_gain,
                           m_attn_k_gain, m_attn_w_o, m_ret_w_in, m_ret_decay_logit, m_ret_gn_w, m_ret_w_out, m_ffn_w_up,
                           m_ffn_conv_w, m_ffn_conv_b, m_ffn_w_down)))
    V = dict(zip(WEIGHTS, (v_c_ctx, v_ada_w, v_ada_b, v_norm_w, v_pool_w, v_pool_b, v_pool_scale, v_attn_w_qkv, v_attn_q_gain,
                           v_attn_k_gain, v_attn_w_o, v_ret_w_in, v_ret_decay_logit, v_ret_gn_w, v_ret_w_out, v_ffn_w_up,
                           v_ffn_conv_w, v_ffn_conv_b, v_ffn_w_down)))
    depth, s, l = ada_w.shape[0], x.shape[1], ctx.shape[1]
    assert l == BM and s % BM == 0 and s % GRID_W == 0
    nlb = s // BM
    n_pool = pool_w.shape[0]
    mx, my, mc = _pos()
    chip, dev = 2 * mx + my, 4 * mx + 2 * my + mc
    nada = ada_w.shape[2]

    sharded = [n for n in SMALL if n in SMALL_SHARD_AXIS]
    got = small_all_gather("gather_small", _pack([c[0]] + [P[n] for n in sharded]))
    got = _unpack(got, [(D,)] + [P[n].shape for n in sharded])
    c_all = got[0]
    full = {n: _unshard(g_, SMALL_SHARD_AXIS[n]) for n, g_ in zip(sharded, got[1:])}

    s9 = jnp.concatenate([c_all, c_ctx[None, :], jnp.zeros((ADA_ROWS - N_DEV - 1, D), F32)], axis=0)
    ada_b_mine = lax.dynamic_slice_in_dim(ada_b, chip * nada, nada, axis=1)[:, None, :]
    mod_part = ada_fwd(s9, ada_w, ada_b_mine)
    mod_all = _unshard(small_all_gather("gather_mod", mod_part.reshape(depth * ADA_ROWS, nada)), 1).reshape(depth, ADA_ROWS, 6, D)
    mod_mine = lax.dynamic_index_in_dim(mod_all, dev, axis=1, keepdims=False)
    mods_all = jnp.pad(jnp.stack([mod_mine, mod_all[:, ADA_CTX]], axis=1), ((0, 0), (0, 0), (0, 2), (0, 0)))
    mods = [mods_all[i] for i in range(depth)]

    names = [n for n, _ in BIG]
    layers = _matrix_groups(depth)
    shards = [P[n].astype(BF16) for n in names]
    gsems, shards, fulls = gather_start(shards, names, layers, full_buffers(shards, names, layers), mods_all)
    first = [sum(len(layer) for layer in layers[:g_]) for g_ in range(len(layers) + 1)]
    flight = {"shards": shards}

    def fetch(i, part, after):
        g_ = 2 * i + part
        flight["shards"], mats = gather_wait(g_, gsems[g_], flight["shards"], names, layers, fulls[first[g_] : first[g_ + 1]], after)
        return {name: m for (name, _, _), m in zip(layers[g_], mats)}

    lands = {n: lax.empty((len(CHIP_FLIPS),) + P[n].shape, BF16) for n in names}
    sent = {}

    def emit(i, part, big):
        g_ = 2 * i + part
        lnames = [name for name, _, _ in layers[g_]]
        sems, gl, ll, token = scatter_start(g_, layers[g_], [big[n] for n in lnames], [lands[n] for n in lnames])
        lands.update(zip(lnames, ll))
        sent[g_] = (sems, gl)
        return token

    w = {
        "nw": full["norm_w"],
        "pbs": jnp.concatenate([full["pool_b"][:, None], full["pool_scale"][:, None], jnp.zeros((n_pool, 6, D), F32)], axis=1),
        "gains": jnp.concatenate([attn_q_gain, attn_k_gain, jnp.zeros((6, HD), F32)], axis=0),
        "gnw": full["ret_gn_w"],
        "logit_b": jnp.broadcast_to(ret_decay_logit[0][:, :, None, None], (2, RET_HEADS, 8, 128)),
        "cw": jnp.concatenate([full["ffn_conv_w"], ffn_conv_b[:, None, :], jnp.zeros((depth, 4, 2 * D_FF), F32)], axis=1),
    }
    w["cos"], w["sin"] = rope_tables(s, l)

    xs = jnp.concatenate([x[0], ctx[0]], axis=0)
    loss_tile, dxs, g = local_step(xs, loss_target[0], mods, w, nlb, depth, fetch, emit)
    loss = lax.psum(loss_tile[0, 0], MESH_AXES)
    grad_x = dxs[None]

    small_shapes = [(depth, 2, 8, D), (depth, 2, D), (n_pool, 2, D), (2, HD), (2, RET_HEADS), (RNV,), (depth, 4, 2 * D_FF)]
    slab = _pack([
        jnp.stack(g["dmod"]),
        jnp.stack([jnp.stack([g["dnw1"][i][0], g["dnw2"][i][0]]) for i in range(depth)]),
        jnp.stack([g["dpbs"][j][0:2] for j in range(n_pool)]),
        g["dgains"][0:2], g["dlogit"][:, :, 0, 0], g["dgnw"][0], jnp.stack([g["dcw"][i][0:4] for i in range(depth)]),
    ])
    slabs = small_all_gather("gather_small_grads", slab)
    dmod_dev = _unpack(slabs, small_shapes[:1])[0]
    t_dmod, t_nw, t_pbs, t_gains, t_logit, t_gnw, t_cw = _unpack(reduce_devices("reduce_small_grads", slabs), small_shapes)

    def cols(a):
        return lax.dynamic_slice_in_dim(a, chip * nada, nada, axis=a.ndim - 1)

    dm_lat = jnp.swapaxes(cols(dmod_dev[:, :, 0, :6].reshape(N_DEV, depth, 6 * D)), 0, 1)
    dm_ctx = cols(t_dmod[:, 1, :6].reshape(depth, 1, 6 * D))
    dm = jnp.concatenate([dm_lat, dm_ctx, jnp.zeros((depth, ADA_ROWS - N_DEV - 1, nada), F32)], axis=1)
    g_ada_w, ds9 = ada_bwd(s9, dm, ada_w)
    g_c_ctx = cctx_grad(small_all_gather("gather_dcctx", ds9), c_ctx[None, :])[0]

    def mine(a, name):
        n = P[name].shape[SMALL_SHARD_AXIS[name]]
        return lax.dynamic_slice_in_dim(a, chip * n, n, axis=SMALL_SHARD_AXIS[name])

    G = {
        "c_ctx": g_c_ctx,
        "ada_b": (t_dmod[:, 0, :6] + t_dmod[:, 1, :6]).reshape(depth, 6 * D),
        "norm_w": mine(t_nw, "norm_w"),
        "pool_b": mine(t_pbs[:, 0], "pool_b"), "pool_scale": mine(t_pbs[:, 1], "pool_scale"),
        "attn_q_gain": t_gains[0:1], "attn_k_gain": t_gains[1:2],
        "ret_decay_logit": t_logit[None], "ret_gn_w": mine(t_gnw[None], "ret_gn_w"),
        "ffn_conv_w": mine(t_cw[:, 0:3], "ffn_conv_w"), "ffn_conv_b": t_cw[:, 3],
    }
    sw, sg, sm, sv = (_pack([d_[n] for n in SMALL]) for d_ in (P, G, M, V))
    outs = adamw_one("adamw_small", sw, sm, sv, sg)
    D_, NM, NV = ({n: a for n, a in zip(SMALL, _unpack(o, [P[n].shape for n in SMALL]))} for o in outs)

    flat2 = lambda a: a.reshape(-1, a.shape[-1])
    G["ada_w"] = g_ada_w
    o3 = adamw_one("adamw_ada", flat2(ada_w), flat2(M["ada_w"]), flat2(V["ada_w"]), flat2(g_ada_w))
    D_["ada_w"], NM["ada_w"], NV["ada_w"] = (o.reshape(ada_w.shape) for o in o3)

    sent_grads, landed = scatter_wait([sent[g_][0] for g_ in range(len(layers))], layers,
                                      [a for g_ in range(len(layers)) for a in sent[g_][1]], names, [lands[n] for n in names], o3[0])
    own = {n: [None] * P[n].shape[0] for n in names}
    for (name, idx, axis), a in zip([e for layer in layers for e in layer], sent_grads):
        n_ = a.shape[axis] // N_CHIP
        own[name][idx] = lax.dynamic_slice_in_dim(a, chip * n_, n_, axis=axis)
    partial = [sum_slots("sum_" + n, jnp.stack(own[n]).reshape(-1, lnd.shape[-1]), lnd.reshape(len(CHIP_FLIPS), -1, lnd.shape[-1]))
               for n, lnd in zip(names, landed)]
    theirs = sibling_swap(partial)
    for (n, _), p, ps in zip(BIG, partial, theirs):
        o4 = adamw_pair("adamw_" + n, flat2(P[n]), flat2(M[n]), flat2(V[n]), p, ps)
        G[n], D_[n], NM[n], NV[n] = (o.reshape(P[n].shape) for o in o4)

    return (loss, grad_x, *[G[n] for n in WEIGHTS], *[D_[n] for n in WEIGHTS], *[NM[n] for n in WEIGHTS], *[NV[n] for n in WEIGHTS])
```

```python
import functools

import jax
import jax.numpy as jnp
from jax import lax
from jax.experimental import pallas as pl
from jax.experimental.pallas import tpu as pltpu

F32 = jnp.float32
BF16 = jnp.bfloat16

D = 1024
BM = 256
EPS = 1e-6
POOL_WINDOWS = (2, 4, 8, 16)
POOL_GROUP = D // 4
ATTN_HEADS = 8
ATTN_KV_HEADS = 2
HD = D // ATTN_HEADS
ATTN_GROUP = ATTN_HEADS // ATTN_KV_HEADS
NQ = ATTN_HEADS * HD
NKV = ATTN_KV_HEADS * HD
GRID_W = 64
ROPE_THETA = 10000.0
RET_HEADS = 4
RET_DK = D // RET_HEADS
RET_DV = 2 * D // RET_HEADS
RNQ = RET_HEADS * RET_DK
RNV = RET_HEADS * RET_DV
D_FF = 2816
FF_CHUNK = 256
ADAM_LR, ADAM_B1, ADAM_B2, ADAM_EPS, ADAM_WD, ADAM_STEP = 0.001, 0.9, 0.999, 1e-08, 0.01, 10
HALO_F32 = 8
HALO_BF16 = 16
VMEM_LIMIT = 60 * 1024 * 1024

MESH_AXES = ("x", "y", "c")
N_DEV = 8
N_CHIP = 4


def _pcall(body, **kw):
    return pl.pallas_call(body, **kw)


def _spec(shape, kind, nlb, nblk):
    nd = len(shape)
    if kind == "row":
        return pl.BlockSpec((BM, shape[1]), lambda i: (i, 0))
    if kind == "row_lat":
        return pl.BlockSpec((BM, shape[1]), lambda i: (jnp.minimum(i, nlb - 1), 0))
    if kind == "full":
        return pl.BlockSpec(tuple(shape), lambda i: (0,) * nd, pipeline_mode=pl.Buffered(1))
    if isinstance(kind, tuple) and kind[0] == "fullat":
        return pl.BlockSpec((None,) + tuple(shape[1:]), lambda i: (kind[1],) + (0,) * (nd - 1), pipeline_mode=pl.Buffered(1))
    if kind == "acc":
        return pl.BlockSpec(tuple(shape), lambda i: (0,) * nd)
    if kind == "any":
        return pl.BlockSpec(memory_space=pl.ANY)
    if kind == "stream":
        return pl.BlockSpec((1,) + tuple(shape[1:]), lambda i: (i // nlb,) + (0,) * (nd - 1))
    if kind in ("prev8", "prev16"):
        hb = int(kind[4:])
        return pl.BlockSpec((hb, shape[1]), lambda i: (jnp.maximum(i * (BM // hb) - 1, 0), 0))
    if kind in ("next8", "next16"):
        hb = int(kind[4:])
        last = nblk * BM // hb - 1
        return pl.BlockSpec((hb, shape[1]), lambda i: (jnp.minimum((i + 1) * (BM // hb), last), 0))
    raise ValueError(kind)


def _rowcall(name, body, nblk, nlb, ins, outs, scratch=()):
    return _pcall(
        body,
        name=name,
        grid=(nblk,),
        in_specs=[_spec(a.shape, k, nlb, nblk) for a, k in ins],
        out_specs=[_spec(s, k, nlb, nblk) for s, _, k in outs],
        out_shape=[jax.ShapeDtypeStruct(s, d) for s, d, _ in outs],
        scratch_shapes=list(scratch),
        compiler_params=pltpu.CompilerParams(dimension_semantics=("arbitrary",), vmem_limit_bytes=VMEM_LIMIT),
    )(*[a for a, _ in ins])


def _wk(w):
    return (w[0], ("fullat", w[1])) if isinstance(w, tuple) else (w, "full")


def _stream_edges(i, nlb):
    is_ctx = i == nlb
    return (i == 0) | is_ctx, (i == nlb - 1) | is_ctx, is_ctx


def _dotf(a, b):
    return jnp.dot(a, b, preferred_element_type=F32)


def _dot_nt(a, b):
    return lax.dot_general(a, b, (((1,), (1,)), ((), ())), preferred_element_type=F32)


def _dot_tn(a, b):
    return lax.dot_general(a, b, (((0,), (0,)), ((), ())), preferred_element_type=F32)


def _sigmoid(x):
    return 0.5 * jnp.tanh(0.5 * x) + 0.5


def _norm_mod(x, nw, sh, sc):
    r = lax.rsqrt(jnp.mean(x * x, axis=-1, keepdims=True) + EPS)
    xhat = x * r
    n = xhat * nw
    return n * (1.0 + sc) + sh, n, xhat, r


def _norm_mod_bwd(dh, n, xhat, r, nw, sc):
    dsh = jnp.sum(dh, axis=0, keepdims=True)
    dsc = jnp.sum(dh * n, axis=0, keepdims=True)
    dn = dh * (1.0 + sc)
    dnw = jnp.sum(dn * xhat, axis=0, keepdims=True)
    dxhat = dn * nw
    dx = r * (dxhat - xhat * jnp.mean(dxhat * xhat, axis=-1, keepdims=True))
    return dx, dsh, dsc, dnw


def _acc_init(i, *refs):
    @pl.when(i == 0)
    def _():
        for r in refs:
            r[...] = jnp.zeros(r.shape, r.dtype)


SH1, SC1, G1, SH2, SC2, G2 = range(6)


def _mrow(mod_ref, k):
    return mod_ref[0, k : k + 1, :]


def _shift_rows(x_ref, xp_ref, xn_ref, cs, first, last):
    cur = x_ref[:, cs].astype(F32)
    rows = lax.broadcasted_iota(jnp.int32, cur.shape, 0)
    pr = jnp.where(first, 0.0, xp_ref[HALO_BF16 - 1 : HALO_BF16, cs].astype(F32))
    nx = jnp.where(last, 0.0, xn_ref[0:1, cs].astype(F32))
    dn = jnp.where(rows == 0, pr, pltpu.roll(cur, 1, 0))
    up = jnp.where(rows == BM - 1, nx, pltpu.roll(cur, BM - 1, 0))
    return dn, cur, up


def ffn_up(x1, mod, nw2, w_up, nblk, nlb):
    r = x1.shape[0]

    def body(x_ref, mod_ref, nw_ref, w_ref, u_ref, h_ref):
        h, _, _, _ = _norm_mod(x_ref[...], nw_ref[...], _mrow(mod_ref, SH2), _mrow(mod_ref, SC2))
        hb = h.astype(BF16)
        h_ref[...] = hb
        u_ref[...] = _dotf(hb, w_ref[...]).astype(BF16)

    return _rowcall(
        "ffn_up", body, nblk, nlb,
        [(x1, "row"), (mod, "stream"), (nw2, "full"), _wk(w_up)],
        [((r, 2 * D_FF), BF16, "row"), ((r, D), BF16, "row")],
    )


def _conv_gate_chunk(u_ref, up_ref, un_ref, cw_ref, j, first, last):
    res = []
    for half in range(2):
        c0 = half * D_FF + j * FF_CHUNK
        cs = slice(c0, c0 + FF_CHUNK)
        dn, cur, up = _shift_rows(u_ref, up_ref, un_ref, cs, first, last)
        val = dn * cw_ref[0:1, cs] + cur * cw_ref[1:2, cs] + up * cw_ref[2:3, cs] + cw_ref[3:4, cs]
        res.append((val, dn, cur, up, cs))
    return res


def ffn_down(u, cw, w_down, x1, mod, nblk, nlb):
    r = x1.shape[0]

    def body(u_ref, up_ref, un_ref, cw_ref, w_ref, x_ref, mod_ref, x2_ref, f_ref, uc_ref):
        first, last, _ = _stream_edges(pl.program_id(0), nlb)
        f = jnp.zeros((BM, D), F32)
        for j in range(D_FF // FF_CHUNK):
            (a, _, _, _, acs), (v, _, _, _, vcs) = _conv_gate_chunk(u_ref, up_ref, un_ref, cw_ref, j, first, last)
            uc_ref[:, acs] = a.astype(BF16)
            uc_ref[:, vcs] = v.astype(BF16)
            f = f + _dotf((a * _sigmoid(a) * v).astype(BF16), w_ref[acs, :])
        f_ref[...] = f.astype(BF16)
        x2_ref[...] = x_ref[...] + _mrow(mod_ref, G2) * f

    return _rowcall(
        "ffn_down", body, nblk, nlb,
        [(u, "row"), (u, "prev16"), (u, "next16"), _wk(cw), _wk(w_down), (x1, "row"), (mod, "stream")],
        [((r, D), F32, "row"), ((r, D), BF16, "row"), ((r, 2 * D_FF), BF16, "row")],
    )


def ffn_bwd1(dx2, f, uc, w_down, mod, dep, nblk, nlb):
    r = dx2.shape[0]

    def body(dx_ref, f_ref, uc_ref, w_ref, mod_ref, dep_ref, duc_ref, dw_ref, dcb_ref, dmod_ref, dw_acc):
        i = pl.program_id(0)
        _acc_init(i, dcb_ref, dmod_ref, dw_acc)
        dx = dx_ref[...]
        df = (_mrow(mod_ref, G2) * dx).astype(BF16)
        dmod_ref[i // nlb, G2 : G2 + 1, :] += jnp.sum(dx * f_ref[...].astype(F32), axis=0, keepdims=True)
        for j in range(D_FF // FF_CHUNK):
            acs = slice(j * FF_CHUNK, (j + 1) * FF_CHUNK)
            vcs = slice(D_FF + j * FF_CHUNK, D_FF + (j + 1) * FF_CHUNK)
            a, v = uc_ref[:, acs].astype(F32), uc_ref[:, vcs].astype(F32)
            sa = _sigmoid(a)
            asa = a * sa
            dw_acc[acs, :] += _dot_tn((asa * v).astype(BF16), df)
            dg = _dot_nt(df, w_ref[acs, :])
            for dval, cs in ((dg * v * (sa * (1.0 + a * (1.0 - sa))), acs), (dg * asa, vcs)):
                duc_ref[:, cs] = dval.astype(BF16)
                dcb_ref[3:4, cs] += jnp.sum(dval, axis=0, keepdims=True)

        @pl.when(i == nblk - 1)
        def _():
            dw_ref[...] = dw_acc[...].astype(BF16)

    return _rowcall(
        "ffn_bwd1", body, nblk, nlb,
        [(dx2, "row"), (f, "row"), (uc, "row"), _wk(w_down), (mod, "stream"), (dep, "any")],
        [((r, 2 * D_FF), BF16, "row"), ((D_FF, D), BF16, "acc"), ((8, 2 * D_FF), F32, "acc"), ((2, 8, D), F32, "acc")],
        scratch=[pltpu.VMEM((D_FF, D), F32)],
    )


def ffn_bwd3(duc, u, cw, w_up, x1, dx2, mod, nw2, nblk, nlb):
    r = dx2.shape[0]

    def body(d_ref, dp_ref, dn_ref, u_ref, cw_ref, w_ref, x_ref, dx_ref, mod_ref, nw_ref, du_ref, dx1_ref, dcw_ref, dnw_ref, dmod_ref):
        i = pl.program_id(0)
        first, last, _ = _stream_edges(i, nlb)
        _acc_init(i, dcw_ref, dnw_ref, dmod_ref)
        dh = jnp.zeros((BM, D), F32)
        for j in range(2 * D_FF // FF_CHUNK):
            cs = slice(j * FF_CHUNK, (j + 1) * FF_CHUNK)
            dn, cur, up = _shift_rows(d_ref, dp_ref, dn_ref, cs, first, last)
            du = (up * cw_ref[0:1, cs] + cur * cw_ref[1:2, cs] + dn * cw_ref[2:3, cs]).astype(BF16)
            du_ref[:, cs] = du
            dh = dh + _dot_nt(du, w_ref[:, cs])
            uu = u_ref[:, cs].astype(F32)
            dcw_ref[0:1, cs] += jnp.sum(up * uu, axis=0, keepdims=True)
            dcw_ref[1:2, cs] += jnp.sum(cur * uu, axis=0, keepdims=True)
            dcw_ref[2:3, cs] += jnp.sum(dn * uu, axis=0, keepdims=True)
        sc = _mrow(mod_ref, SC2)
        nw = nw_ref[...]
        _, n, xhat, rr = _norm_mod(x_ref[...], nw, _mrow(mod_ref, SH2), sc)
        dxn, dsh, dsc, dnw = _norm_mod_bwd(dh, n, xhat, rr, nw, sc)
        dx1_ref[...] = dx_ref[...] + dxn
        s = i // nlb
        dmod_ref[s, SH2 : SH2 + 1, :] += dsh
        dmod_ref[s, SC2 : SC2 + 1, :] += dsc
        dnw_ref[0:1, :] += dnw

    return _rowcall(
        "ffn_bwd3", body, nblk, nlb,
        [(duc, "row"), (duc, "prev16"), (duc, "next16"), (u, "row"), _wk(cw), _wk(w_up), (x1, "row"), (dx2, "row"),
         (mod, "stream"), (nw2, "full")],
        [((r, 2 * D_FF), BF16, "row"), ((r, D), F32, "row"), ((8, 2 * D_FF), F32, "acc"), ((8, D), F32, "acc"),
         ((2, 8, D), F32, "acc")],
    )


def matmul_tn(a, b, nblk, tn=None):
    k, n = a.shape[1], b.shape[1]
    rows = nblk * BM
    tr = 768 if rows % 768 == 0 else (1024 if rows % 1024 == 0 else BM)
    if tn is None:
        tn = n
        while k * tn * 4 > 6 * 1024 * 1024 and tn % 256 == 0:
            tn //= 2
    steps = rows // tr

    def body(a_ref, b_ref, o_ref, acc):
        t = pl.program_id(1)

        @pl.when(t == 0)
        def _():
            acc[...] = jnp.zeros(acc.shape, acc.dtype)

        acc[...] += _dot_tn(a_ref[...], b_ref[...])

        @pl.when(t == steps - 1)
        def _():
            o_ref[...] = acc[...].astype(o_ref.dtype)

    return _pcall(
        body,
        name="matmul_tn",
        grid=(n // tn, steps),
        in_specs=[pl.BlockSpec((tr, k), lambda j, t: (t, 0)), pl.BlockSpec((tr, tn), lambda j, t: (t, j))],
        out_specs=pl.BlockSpec((k, tn), lambda j, t: (0, j)),
        out_shape=jax.ShapeDtypeStruct((k, n), BF16),
        scratch_shapes=[pltpu.VMEM((k, tn), F32)],
        compiler_params=pltpu.CompilerParams(dimension_semantics=("parallel", "arbitrary"), vmem_limit_bytes=VMEM_LIMIT),
    )(a, b)


EXT = BM + 2 * HALO_F32


def _pool_positions(i, nlb, nrows, row0):
    is_ctx = i == nlb
    t = (i - jnp.where(is_ctx, nlb, 0)) * BM + row0 + lax.broadcasted_iota(jnp.int32, (nrows, 1), 0)
    return t, jnp.where(is_ctx, BM, nlb * BM)


def _pool_cnt(t, win, slen):
    return (jnp.minimum(t + win // 2, slen) - jnp.maximum(t - win // 2, 0)).astype(F32)


def _pool_fill_ext(ext, i, nlb, x_ref, xp_ref, xn_ref, mod_ref, nw_ref):
    first, last, _ = _stream_edges(i, nlb)
    sh, sc, nw = _mrow(mod_ref, SH1), _mrow(mod_ref, SC1), nw_ref[...]
    hcur, n, xhat, r = _norm_mod(x_ref[...], nw, sh, sc)
    ext[0:HALO_F32, :] = jnp.where(first, 0.0, _norm_mod(xp_ref[...], nw, sh, sc)[0])
    ext[HALO_F32 : HALO_F32 + BM, :] = hcur
    ext[HALO_F32 + BM :, :] = jnp.where(last, 0.0, _norm_mod(xn_ref[...], nw, sh, sc)[0])
    return n, xhat, r


def _window_sum(ref, cols, offs):
    acc = None
    for o in offs:
        v = ref[HALO_F32 + o : HALO_F32 + o + BM, cols]
        acc = v if acc is None else acc + v
    return acc


def _pool_diff(ext, g, win, t, slen):
    cols = slice(g * POOL_GROUP, (g + 1) * POOL_GROUP)
    ssum = _window_sum(ext, cols, range(-(win // 2), win // 2))
    return ssum / _pool_cnt(t, win, slen) - ext[HALO_F32 : HALO_F32 + BM, cols]


def pool_fwd(x, mod, nw1, pw, pbs, nblk, nlb):
    r = x.shape[0]

    def body(x_ref, xp_ref, xn_ref, mod_ref, nw_ref, pw_ref, pbs_ref, x1_ref, ypre_ref, ext):
        i = pl.program_id(0)
        _pool_fill_ext(ext, i, nlb, x_ref, xp_ref, xn_ref, mod_ref, nw_ref)
        t, slen = _pool_positions(i, nlb, BM, 0)
        for g, win in enumerate(POOL_WINDOWS):
            cols = slice(g * POOL_GROUP, (g + 1) * POOL_GROUP)
            diff = _pool_diff(ext, g, win, t, slen)
            ypre = _dotf(diff.astype(BF16), pw_ref[g]) + pbs_ref[0:1, cols]
            ypre_ref[:, cols] = ypre
            x1_ref[:, cols] = x_ref[:, cols] + mod_ref[0, G1 : G1 + 1, cols] * (ypre * pbs_ref[1:2, cols])

    return _rowcall(
        "pool_fwd", body, nblk, nlb,
        [(x, "row"), (x, "prev8"), (x, "next8"), (mod, "stream"), (nw1, "full"), _wk(pw), _wk(pbs)],
        [((r, D), F32, "row"), ((r, D), F32, "row")],
        scratch=[pltpu.VMEM((EXT, D), F32)],
    )


def pool_bwd(dx1, x, ypre, mod, nw1, pw, pbs, dep, nblk, nlb, lat_dx):
    r = x.shape[0]

    def body(d_ref, dp_ref, dn_ref, x_ref, xp_ref, xn_ref, ypre_ref, mod_ref, nw_ref, pw_ref, pbs_ref, dep_ref,
             dx_ref, dpw_ref, dpbs_ref, dnw_ref, dmod_ref, ext, dext, eext, dh):
        i = pl.program_id(0)
        first, last, _ = _stream_edges(i, nlb)
        _acc_init(i, dpw_ref, dpbs_ref, dnw_ref, dmod_ref)
        n, xhat, rr = _pool_fill_ext(ext, i, nlb, x_ref, xp_ref, xn_ref, mod_ref, nw_ref)
        g1, scale = _mrow(mod_ref, G1), pbs_ref[1:2, :]
        dcur = d_ref[...]
        ypre = ypre_ref[...]
        s = i // nlb
        dmod_ref[s, G1 : G1 + 1, :] += jnp.sum(dcur * (ypre * scale), axis=0, keepdims=True)
        dy = g1 * dcur
        dpbs_ref[1:2, :] += jnp.sum(dy * ypre, axis=0, keepdims=True)
        dpbs_ref[0:1, :] += jnp.sum(dy * scale, axis=0, keepdims=True)
        gs = g1 * scale
        dext[0:HALO_F32, :] = jnp.where(first, 0.0, gs * dp_ref[...])
        dext[HALO_F32 : HALO_F32 + BM, :] = dy * scale
        dext[HALO_F32 + BM :, :] = jnp.where(last, 0.0, gs * dn_ref[...])
        t, slen = _pool_positions(i, nlb, BM, 0)
        text, _ = _pool_positions(i, nlb, EXT, -HALO_F32)
        for g, win in enumerate(POOL_WINDOWS):
            cols = slice(g * POOL_GROUP, (g + 1) * POOL_GROUP)
            diff = _pool_diff(ext, g, win, t, slen)
            dpre = dext[:, cols].astype(BF16)
            ddiff = _dot_nt(dpre, pw_ref[g])
            eext[...] = ddiff / jnp.maximum(_pool_cnt(text, win, slen), 1.0)
            dh[:, cols] = _window_sum(eext, slice(None), range(-(win // 2) + 1, win // 2 + 1)) - ddiff[HALO_F32 : HALO_F32 + BM, :]
            dpw_ref[g] += _dot_tn(diff.astype(BF16), dpre[HALO_F32 : HALO_F32 + BM, :])
        sc, nw = _mrow(mod_ref, SC1), nw_ref[...]
        dxn, dsh, dsc, dnw = _norm_mod_bwd(dh[...], n, xhat, rr, nw, sc)
        if lat_dx:
            @pl.when(i < nlb)
            def _():
                dx_ref[...] = dcur + dxn
        else:
            dx_ref[...] = dcur + dxn
        dmod_ref[s, SH1 : SH1 + 1, :] += dsh
        dmod_ref[s, SC1 : SC1 + 1, :] += dsc
        dnw_ref[0:1, :] += dnw

    return _rowcall(
        "pool_bwd", body, nblk, nlb,
        [(dx1, "row"), (dx1, "prev8"), (dx1, "next8"), (x, "row"), (x, "prev8"), (x, "next8"), (ypre, "row"),
         (mod, "stream"), (nw1, "full"), _wk(pw), _wk(pbs), (dep, "any")],
        [((nlb * BM, D), F32, "row_lat") if lat_dx else ((r, D), F32, "row"), ((4, POOL_GROUP, POOL_GROUP), F32, "acc"), ((8, D), F32, "acc"), ((8, D), F32, "acc"),
         ((2, 8, D), F32, "acc")],
        scratch=[pltpu.VMEM((EXT, D), F32), pltpu.VMEM((EXT, D), F32), pltpu.VMEM((EXT, POOL_GROUP), F32), pltpu.VMEM((BM, D), F32)],
    )


def rope_tables(s, l):
    rows = s // GRID_W
    row = jnp.broadcast_to(jnp.arange(rows)[:, None], (rows, GRID_W)).reshape(-1).astype(F32)
    col = jnp.broadcast_to(jnp.arange(GRID_W)[None, :], (rows, GRID_W)).reshape(-1).astype(F32)
    axis_dim = HD // 2
    inv = ROPE_THETA ** (-jnp.arange(0, axis_dim, 2, dtype=F32) / axis_dim)
    ar, ac = row[:, None] * inv, col[:, None] * inv
    cos = jnp.concatenate([jnp.cos(ar), jnp.cos(ar), jnp.cos(ac), jnp.cos(ac)], axis=-1)
    sin = jnp.concatenate([-jnp.sin(ar), jnp.sin(ar), -jnp.sin(ac), jnp.sin(ac)], axis=-1)
    return (jnp.concatenate([cos, jnp.ones((l, HD), F32)], axis=0), jnp.concatenate([sin, jnp.zeros((l, HD), F32)], axis=0))


def _partner(x):
    q = HD // 4
    lane = lax.broadcasted_iota(jnp.int32, x.shape, 1)
    return jnp.where((lane // q) % 2 == 0, pltpu.roll(x, HD - q, 1), pltpu.roll(x, q, 1))


def _head_norm(raw, gain):
    r = lax.rsqrt(jnp.mean(raw * raw, axis=-1, keepdims=True) + EPS)
    return raw * r, r


ATTN_SCALE = HD ** -0.5


def qkv_fwd(x, mod, nw1, w_qkv, gains, cos_t, sin_t, nblk, nlb):
    r = x.shape[0]

    def body(x_ref, mod_ref, nw_ref, w_ref, g_ref, c_ref, s_ref, raw_ref, q_ref, k_ref, v_ref, h_ref):
        h = _norm_mod(x_ref[...], nw_ref[...], _mrow(mod_ref, SH1), _mrow(mod_ref, SC1))[0].astype(BF16)
        h_ref[...] = h
        raw_ref[...] = _dotf(h, w_ref[...])
        cos, sin = c_ref[...], s_ref[...]
        for j in range(ATTN_HEADS + ATTN_KV_HEADS):
            isq = j < ATTN_HEADS
            xn = _head_norm(raw_ref[:, j * HD : (j + 1) * HD], None)[0] * (g_ref[0:1, :] if isq else g_ref[1:2, :])
            rot = xn * cos + _partner(xn) * sin
            if isq:
                rot = rot * ATTN_SCALE
            rot = rot.astype(BF16)
            if isq:
                q_ref[:, j * HD : (j + 1) * HD] = rot
            else:
                k_ref[:, (j - ATTN_HEADS) * HD : (j - ATTN_HEADS + 1) * HD] = rot
        v_ref[...] = raw_ref[:, NQ + NKV :].astype(BF16)

    return _rowcall(
        "qkv_fwd", body, nblk, nlb,
        [(x, "row"), (mod, "stream"), (nw1, "full"), (w_qkv, "full"), (gains, "full"), (cos_t, "row"), (sin_t, "row")],
        [((r, NQ + 2 * NKV), F32, "row"), ((r, NQ), BF16, "row"), ((r, NKV), BF16, "row"), ((r, NKV), BF16, "row"),
         ((r, D), BF16, "row")],
    )


def attn_fwd(q, k, v, nblk, nlb):
    r = q.shape[0]

    def body(q_ref, k_ref, v_ref, o_ref, lse_ref):
        def heads(keys):
            for kvh in range(ATTN_KV_HEADS):
                kh = k_ref[keys, kvh * HD : (kvh + 1) * HD]
                vh = v_ref[keys, kvh * HD : (kvh + 1) * HD]
                for g in range(ATTN_GROUP):
                    cs = slice((kvh * ATTN_GROUP + g) * HD, (kvh * ATTN_GROUP + g + 1) * HD)
                    s = _dot_nt(q_ref[:, cs], kh)
                    m = jnp.max(s, axis=-1, keepdims=True)
                    p = jnp.exp(s - m)
                    l = jnp.sum(p, axis=-1, keepdims=True)
                    o_ref[:, cs] = (_dotf(p.astype(BF16), vh) / l).astype(BF16)
                    j = kvh * ATTN_GROUP + g
                    lse_ref[:, j : j + 1] = m + jnp.log(l)

        i = pl.program_id(0)
        pl.when(i < nlb)(lambda: heads(slice(0, r)))
        pl.when(i == nlb)(lambda: heads(slice(nlb * BM, r)))

    return _rowcall(
        "attn_fwd", body, nblk, nlb,
        [(q, "row"), (k, "full"), (v, "full")],
        [((r, NQ), BF16, "row"), ((r, ATTN_HEADS), F32, "row")],
    )


def attn_out_fwd(o, w_o, x, mod, nblk, nlb):
    r = x.shape[0]

    def body(o_ref, w_ref, x_ref, mod_ref, x1_ref, y_ref):
        y = _dotf(o_ref[...], w_ref[...])
        y_ref[...] = y
        x1_ref[...] = x_ref[...] + _mrow(mod_ref, G1) * y

    return _rowcall(
        "attn_out_fwd", body, nblk, nlb,
        [(o, "row"), (w_o, "full"), (x, "row"), (mod, "stream")],
        [((r, D), F32, "row"), ((r, D), F32, "row")],
    )


def mix_out_bwd(name, dx1, y, w_out, mod, dep, nblk, nlb):
    r = dx1.shape[0]
    kin = w_out.shape[0]

    def body(d_ref, y_ref, w_ref, mod_ref, dep_ref, dy_ref, do_ref, dmod_ref):
        i = pl.program_id(0)
        _acc_init(i, dmod_ref)
        d = d_ref[...]
        dmod_ref[i // nlb, G1 : G1 + 1, :] += jnp.sum(d * y_ref[...], axis=0, keepdims=True)
        dy = (_mrow(mod_ref, G1) * d).astype(BF16)
        dy_ref[...] = dy
        do_ref[...] = _dot_nt(dy, w_ref[...]).astype(do_ref.dtype)

    return _rowcall(
        name, body, nblk, nlb,
        [(dx1, "row"), (y, "row"), (w_out, "full"), (mod, "stream"), (dep, "any")],
        [((r, D), BF16, "row"), ((r, kin), BF16, "row"), ((2, 8, D), F32, "acc")],
    )


ATTN_KCHUNK = 11 * BM


def attn_bwd(q, k, v, o, do, lse, nblk, nlb):
    r = q.shape[0]
    kc = ATTN_KCHUNK if r % ATTN_KCHUNK == 0 else BM
    nkc = r // kc

    def body(q_ref, k_ref, v_ref, o_ref, do_ref, lse_ref, dq_ref, dk_ref, dv_ref):
        i = pl.program_id(0)
        _acc_init(i, dk_ref, dv_ref)

        def heads(chunks):
            for kvh in range(ATTN_KV_HEADS):
                ks = slice(kvh * HD, (kvh + 1) * HD)
                for g in range(ATTN_GROUP):
                    j = kvh * ATTN_GROUP + g
                    cs = slice(j * HD, (j + 1) * HD)
                    qh, doh = q_ref[:, cs], do_ref[:, cs]
                    delta = jnp.sum(doh.astype(F32) * o_ref[:, cs].astype(F32), axis=-1, keepdims=True)
                    lse = lse_ref[:, j : j + 1]
                    dq = jnp.zeros((BM, HD), F32)
                    for rs in chunks:
                        kh, vh = k_ref[rs, ks], v_ref[rs, ks]
                        p = jnp.exp(_dot_nt(qh, kh) - lse)
                        ds = (p * (_dot_nt(doh, vh) - delta)).astype(BF16)
                        dq = dq + _dotf(ds, kh)
                        dk_ref[rs, ks] += _dot_tn(ds, qh)
                        dv_ref[rs, ks] += _dot_tn(p.astype(BF16), doh)
                    dq_ref[:, cs] = dq * ATTN_SCALE

        pl.when(i < nlb)(lambda: heads([slice(c * kc, (c + 1) * kc) for c in range(nkc)]))
        pl.when(i == nlb)(lambda: heads([slice(nlb * BM, r)]))

    return _rowcall(
        "attn_bwd", body, nblk, nlb,
        [(q, "row"), (k, "full"), (v, "full"), (o, "row"), (do, "row"), (lse, "row")],
        [((r, NQ), F32, "row"), ((r, NKV), F32, "acc"), ((r, NKV), F32, "acc")],
    )


def qkv_bwd(dq, dk, dv, raw, gains, cos_t, sin_t, w_qkv, x, dx1, mod, nw1, nblk, nlb, ctx_dx_zero):
    r = x.shape[0]

    def body(dq_ref, dk_ref, dv_ref, raw_ref, g_ref, c_ref, s_ref, w_ref, x_ref, dx1_ref, mod_ref, nw_ref,
             dx_ref, draw_ref, dg_ref, dnw_ref, dmod_ref):
        i = pl.program_id(0)
        _acc_init(i, dg_ref, dnw_ref, dmod_ref)
        cos, sin = c_ref[...], s_ref[...]
        for j in range(ATTN_HEADS + ATTN_KV_HEADS):
            isq = j < ATTN_HEADS
            cs = slice(j * HD, (j + 1) * HD)
            dr = dq_ref[:, cs] if isq else dk_ref[:, (j - ATTN_HEADS) * HD : (j - ATTN_HEADS + 1) * HD]
            dxn = dr * cos + _partner(dr * sin)
            xhat, rr = _head_norm(raw_ref[:, cs], None)
            gi = 0 if isq else 1
            dg_ref[gi : gi + 1, :] += jnp.sum(dxn * xhat, axis=0, keepdims=True)
            dxhat = dxn * g_ref[gi : gi + 1, :]
            draw_ref[:, cs] = (rr * (dxhat - xhat * jnp.mean(dxhat * xhat, axis=-1, keepdims=True))).astype(BF16)
        draw_ref[:, NQ + NKV :] = dv_ref[...].astype(BF16)
        dh = _dot_nt(draw_ref[...], w_ref[...])
        sc, nw = _mrow(mod_ref, SC1), nw_ref[...]
        _, n, xhat, rr = _norm_mod(x_ref[...], nw, _mrow(mod_ref, SH1), sc)
        dxn, dsh, dsc, dnw = _norm_mod_bwd(dh, n, xhat, rr, nw, sc)
        dres = dx1_ref[...]
        if ctx_dx_zero:
            dres = jnp.where(i == nlb, 0.0, dres)
        dx_ref[...] = dres + dxn
        s = i // nlb
        dmod_ref[s, SH1 : SH1 + 1, :] += dsh
        dmod_ref[s, SC1 : SC1 + 1, :] += dsc
        dnw_ref[0:1, :] += dnw

    return _rowcall(
        "qkv_bwd", body, nblk, nlb,
        [(dq, "row"), (dk, "row"), (dv, "row"), (raw, "row"), (gains, "full"), (cos_t, "row"), (sin_t, "row"),
         (w_qkv, "full"), (x, "row"), (dx1, "row"), (mod, "stream"), (nw1, "full")],
        [((r, D), F32, "row"), ((r, NQ + 2 * NKV), BF16, "row"), ((8, HD), F32, "acc"), ((8, D), F32, "acc"),
         ((2, 8, D), F32, "acc")],
    )


RET_KSCALE = RET_DK ** -0.5
RET_HP = 2


def ret_in_fwd(x, mod, nw1, w_in, nblk, nlb):
    r = x.shape[0]

    def body(x_ref, mod_ref, nw_ref, w_ref, q_ref, k_ref, v_ref, g_ref, h_ref):
        h = _norm_mod(x_ref[...], nw_ref[...], _mrow(mod_ref, SH1), _mrow(mod_ref, SC1))[0].astype(BF16)
        h_ref[...] = h
        q_ref[...] = _dotf(h, w_ref[:, 0:RNQ]).astype(BF16)
        k_ref[...] = (_dotf(h, w_ref[:, RNQ : 2 * RNQ]) * RET_KSCALE).astype(BF16)
        v_ref[...] = _dotf(h, w_ref[:, 2 * RNQ : 2 * RNQ + RNV]).astype(BF16)
        g_ref[...] = _dotf(h, w_ref[:, 2 * RNQ + RNV :]).astype(BF16)

    return _rowcall(
        "ret_in_fwd", body, nblk, nlb,
        [(x, "row"), (mod, "stream"), (nw1, "full"), (w_in, "full")],
        [((r, RNQ), BF16, "row"), ((r, RNQ), BF16, "row"), ((r, RNV), BF16, "row"), ((r, RNV), BF16, "row"), ((r, D), BF16, "row")],
    )


def _log_sigmoid(x):
    return jnp.minimum(x, 0.0) - jnp.log(1.0 + jnp.exp(-jnp.abs(x)))


def _ret_decays(lg, reverse):
    c = BM
    i = lax.broadcasted_iota(jnp.int32, (c, c), 0)
    j = lax.broadcasted_iota(jnp.int32, (c, c), 1)
    diff = (j - i) if reverse else (i - j)
    ediff = jnp.maximum(diff, 0).astype(F32)
    dm = jnp.where(diff >= 0, jnp.exp(ediff * lg), 0.0)
    rr = lax.broadcasted_iota(jnp.int32, (c, 1), 0).astype(F32)
    eq = (c - rr) if reverse else (rr + 1.0)
    ek = rr if reverse else (c - 1.0 - rr)
    return dm, ediff, jnp.exp(eq * lg), eq, jnp.exp(ek * lg), ek, jnp.exp(c * lg)


def _ret_chunk_index(nlb):
    return (lambda s: jnp.where(s == 0, nlb, s - 1)), (lambda s: jnp.where(s == 0, nlb, nlb - s))


def ret_scan_fwd(q, k, v, logit_b, nlb):
    r = q.shape[0]
    nb = nlb + 1
    fidx, bidx = _ret_chunk_index(nlb)

    def body(qf, kf, vf, qb, kb, vb, lg_ref, of_ref, ob_ref, rf_ref, rb_ref, stf, stb):
        s = pl.program_id(1)

        @pl.when(s == 0)
        def _():
            stf[...] = jnp.zeros(stf.shape, F32)
            stb[...] = jnp.zeros(stb.shape, F32)

        for d, (q_ref, k_ref, v_ref, o_ref, rs_ref, st) in enumerate(((qf, kf, vf, of_ref, rf_ref, stf), (qb, kb, vb, ob_ref, rb_ref, stb))):
            for hp in range(RET_HP):
                kcs, vcs = slice(hp * RET_DK, (hp + 1) * RET_DK), slice(hp * RET_DV, (hp + 1) * RET_DV)
                lg = _log_sigmoid(lg_ref[d, hp])[0:1, 0:1]
                dm, _, qd, _, kd, _, gc = _ret_decays(lg, d == 1)
                qq, kk, vv, st0 = q_ref[:, kcs], k_ref[:, kcs], v_ref[:, vcs], st[hp]
                rs_ref[hp, 0] = st0
                a = _dot_nt(qq, kk) * dm
                o = _dotf(a.astype(BF16), vv) + _dotf(qq, st0.astype(BF16)) * qd
                o_ref[:, vcs] = jnp.where(s == 0, 0.0, o)
                st[hp] = st0 * gc + _dot_tn((kk.astype(F32) * kd).astype(BF16), vv)

    qspec = lambda f: pl.BlockSpec((BM, RET_HP * RET_DK), lambda h, s: (f(s), h))
    vspec = lambda f: pl.BlockSpec((BM, RET_HP * RET_DV), lambda h, s: (f(s), h))
    sspec = pl.BlockSpec((RET_HP, 1, RET_DK, RET_DV), lambda h, s: (h, s, 0, 0))
    return _pcall(
        body,
        name="ret_scan_fwd",
        grid=(RET_HEADS // RET_HP, nb),
        in_specs=[qspec(fidx), qspec(fidx), vspec(fidx), qspec(bidx), qspec(bidx), vspec(bidx),
                  pl.BlockSpec((2, RET_HP, 8, 128), lambda h, s: (0, h, 0, 0))],
        out_specs=[vspec(fidx), vspec(bidx), sspec, sspec],
        out_shape=[jax.ShapeDtypeStruct((r, RNV), F32), jax.ShapeDtypeStruct((r, RNV), F32),
                   jax.ShapeDtypeStruct((RET_HEADS, nb, RET_DK, RET_DV), F32), jax.ShapeDtypeStruct((RET_HEADS, nb, RET_DK, RET_DV), F32)],
        scratch_shapes=[pltpu.VMEM((RET_HP, RET_DK, RET_DV), F32), pltpu.VMEM((RET_HP, RET_DK, RET_DV), F32)],
        compiler_params=pltpu.CompilerParams(dimension_semantics=("parallel", "arbitrary"), vmem_limit_bytes=VMEM_LIMIT),
    )(q, k, v, q, k, v, logit_b)


def _group_norm(o):
    mu = jnp.mean(o, axis=-1, keepdims=True)
    oc = o - mu
    rstd = lax.rsqrt(jnp.mean(oc * oc, axis=-1, keepdims=True) + EPS)
    return oc * rstd, rstd


def ret_out_fwd(o_f, o_b, g, gnw, w_out, x, mod, nblk, nlb):
    r = x.shape[0]

    def body(of_ref, ob_ref, g_ref, gn_ref, w_ref, x_ref, mod_ref, x1_ref, y_ref, z_ref):
        for hh in range(RET_HEADS):
            cs = slice(hh * RET_DV, (hh + 1) * RET_DV)
            yhat, _ = _group_norm(of_ref[:, cs] + ob_ref[:, cs])
            gg = g_ref[:, cs].astype(F32)
            z_ref[:, cs] = (gg * _sigmoid(gg) * (yhat * gn_ref[0:1, cs])).astype(BF16)
        y = _dotf(z_ref[...], w_ref[...])
        y_ref[...] = y
        x1_ref[...] = x_ref[...] + _mrow(mod_ref, G1) * y

    return _rowcall(
        "ret_out_fwd", body, nblk, nlb,
        [(o_f, "row"), (o_b, "row"), (g, "row"), (gnw, "full"), (w_out, "full"), (x, "row"), (mod, "stream")],
        [((r, D), F32, "row"), ((r, D), F32, "row"), ((r, RNV), BF16, "row")],
    )


def ret_out_bwd(dx1, y, o_f, o_b, g, gnw, w_out, mod, dep, nblk, nlb):
    r = dx1.shape[0]

    def body(d_ref, y_ref, of_ref, ob_ref, g_ref, gn_ref, w_ref, mod_ref, dep_ref, dy_ref, do_ref, dg_ref, dgn_ref, dmod_ref, dz):
        i = pl.program_id(0)
        _acc_init(i, dgn_ref, dmod_ref)
        d = d_ref[...]
        dmod_ref[i // nlb, G1 : G1 + 1, :] += jnp.sum(d * y_ref[...], axis=0, keepdims=True)
        dy = (_mrow(mod_ref, G1) * d).astype(BF16)
        dy_ref[...] = dy
        dz[...] = _dot_nt(dy, w_ref[...])
        for hh in range(RET_HEADS):
            cs = slice(hh * RET_DV, (hh + 1) * RET_DV)
            yhat, rstd = _group_norm(of_ref[:, cs] + ob_ref[:, cs])
            gg = g_ref[:, cs].astype(F32)
            sg = _sigmoid(gg)
            gn = gn_ref[0:1, cs]
            dzz = dz[:, cs]
            dg_ref[:, cs] = (dzz * (yhat * gn) * (sg * (1.0 + gg * (1.0 - sg)))).astype(BF16)
            dyn = dzz * (gg * sg)
            dgn_ref[0:1, cs] += jnp.sum(dyn * yhat, axis=0, keepdims=True)
            dyh = dyn * gn
            do = rstd * (dyh - jnp.mean(dyh, axis=-1, keepdims=True) - yhat * jnp.mean(dyh * yhat, axis=-1, keepdims=True))
            do_ref[:, cs] = do.astype(BF16)

    return _rowcall(
        "ret_out_bwd", body, nblk, nlb,
        [(dx1, "row"), (y, "row"), (o_f, "row"), (o_b, "row"), (g, "row"), (gnw, "full"), (w_out, "full"), (mod, "stream"), (dep, "any")],
        [((r, D), BF16, "row"), ((r, RNV), BF16, "row"), ((r, RNV), BF16, "row"), ((8, RNV), F32, "acc"), ((2, 8, D), F32, "acc")],
        scratch=[pltpu.VMEM((BM, RNV), F32)],
    )


def ret_scan_bwd(q, k, v, do, st_f, st_b, logit_b, nlb):
    r = q.shape[0]
    nb = nlb + 1
    fidx, bidx = _ret_chunk_index(nlb)
    step = lambda t: nb - 1 - t

    def body(qf, kf, vf, dof, rf, qb, kb, vb, dob, rb, lg_ref,
             dqf, dkf, dvf, dqb, dkb, dvb, dlg_ref, drf, drb):
        t = pl.program_id(1)
        s = step(t)

        @pl.when(t == 0)
        def _():
            drf[...] = jnp.zeros(drf.shape, F32)
            drb[...] = jnp.zeros(drb.shape, F32)
            dlg_ref[...] = jnp.zeros(dlg_ref.shape, F32)

        dirs = ((qf, kf, vf, dof, rf, dqf, dkf, dvf, drf), (qb, kb, vb, dob, rb, dqb, dkb, dvb, drb))
        for d, (q_ref, k_ref, v_ref, do_ref, rs_ref, dq_ref, dk_ref, dv_ref, dr) in enumerate(dirs):
            for hp in range(RET_HP):
                kcs, vcs = slice(hp * RET_DK, (hp + 1) * RET_DK), slice(hp * RET_DV, (hp + 1) * RET_DV)
                lg = _log_sigmoid(lg_ref[d, hp])[0:1, 0:1]
                dm, ediff, qd, eq, kd, ek, gc = _ret_decays(lg, d == 1)
                qq, kk, vv = q_ref[:, kcs], k_ref[:, kcs], v_ref[:, vcs]
                dob16 = jnp.where(s == 0, jnp.zeros((), BF16), do_ref[:, vcs])
                do32 = dob16.astype(F32)
                st0 = rs_ref[hp, 0]
                st16 = st0.astype(BF16)
                dr0 = dr[hp]
                dr16 = dr0.astype(BF16)
                a = _dot_nt(qq, kk) * dm
                daf = _dot_nt(dob16, vv)
                ds = (daf * dm).astype(BF16)
                qr = _dotf(qq, st16)
                k32 = kk.astype(F32)
                kdec = (k32 * kd).astype(BF16)
                dv_ref[:, vcs] = (_dot_tn(a.astype(BF16), dob16) + _dotf(kdec, dr16)).astype(BF16)
                dq_ref[:, kcs] = (_dotf(ds, kk) + _dot_nt(dob16, st16) * qd).astype(BF16)
                vdr = _dot_nt(vv, dr16)
                dk_ref[:, kcs] = (_dot_tn(ds, qq) + vdr * kd).astype(BF16)
                tot = (jnp.sum(daf * a * ediff)
                       + jnp.sum(eq * qd * jnp.sum(do32 * qr, axis=-1, keepdims=True))
                       + jnp.sum(ek * kd * jnp.sum(k32 * vdr, axis=-1, keepdims=True))
                       + jnp.sum(BM * gc * jnp.sum(dr0 * st0, axis=-1, keepdims=True)))
                dlg_ref[d, hp] += tot
                dr[hp] = gc * dr0 + _dot_tn(qq, (do32 * qd).astype(BF16))

        @pl.when(t == nb - 1)
        def _():
            dlg_ref[...] = dlg_ref[...] * _sigmoid(-lg_ref[...])

    qspec = lambda f: pl.BlockSpec((BM, RET_HP * RET_DK), lambda h, t: (f(step(t)), h))
    vspec = lambda f: pl.BlockSpec((BM, RET_HP * RET_DV), lambda h, t: (f(step(t)), h))
    sspec = pl.BlockSpec((RET_HP, 1, RET_DK, RET_DV), lambda h, t: (h, step(t), 0, 0))
    lspec = pl.BlockSpec((2, RET_HP, 8, 128), lambda h, t: (0, h, 0, 0))
    sq, sv = jax.ShapeDtypeStruct((r, RNQ), BF16), jax.ShapeDtypeStruct((r, RNV), BF16)
    return _pcall(
        body,
        name="ret_scan_bwd",
        grid=(RET_HEADS // RET_HP, nb),
        in_specs=[qspec(fidx), qspec(fidx), vspec(fidx), vspec(fidx), sspec,
                  qspec(bidx), qspec(bidx), vspec(bidx), vspec(bidx), sspec, lspec],
        out_specs=[qspec(fidx), qspec(fidx), vspec(fidx), qspec(bidx), qspec(bidx), vspec(bidx), lspec],
        out_shape=[sq, sq, sv, sq, sq, sv, jax.ShapeDtypeStruct((2, RET_HEADS, 8, 128), F32)],
        scratch_shapes=[pltpu.VMEM((RET_HP, RET_DK, RET_DV), F32), pltpu.VMEM((RET_HP, RET_DK, RET_DV), F32)],
        compiler_params=pltpu.CompilerParams(dimension_semantics=("parallel", "arbitrary"), vmem_limit_bytes=VMEM_LIMIT),
    )(q, k, v, do, st_f, q, k, v, do, st_b, logit_b)


def ret_in_bwd(dqf, dkf, dvf, dqb, dkb, dvb, dgate, w_in, x, dx1, mod, nw1, nblk, nlb):
    r = x.shape[0]
    nin = 2 * RNQ + 2 * RNV

    def body(dqf_ref, dkf_ref, dvf_ref, dqb_ref, dkb_ref, dvb_ref, dg_ref, w_ref, x_ref, dx1_ref, mod_ref, nw_ref,
             dx_ref, din_ref, dnw_ref, dmod_ref):
        i = pl.program_id(0)
        _acc_init(i, dnw_ref, dmod_ref)
        is_ctx = i == nlb
        din_ref[:, 0:RNQ] = (dqf_ref[...].astype(F32) + dqb_ref[...].astype(F32)).astype(BF16)
        din_ref[:, RNQ : 2 * RNQ] = ((dkf_ref[...].astype(F32) + dkb_ref[...].astype(F32)) * RET_KSCALE).astype(BF16)
        din_ref[:, 2 * RNQ : 2 * RNQ + RNV] = (dvf_ref[...].astype(F32) + dvb_ref[...].astype(F32)).astype(BF16)
        din_ref[:, 2 * RNQ + RNV :] = jnp.where(is_ctx, jnp.zeros((), BF16), dg_ref[...])
        dh = _dot_nt(din_ref[...], w_ref[...])
        sc, nw = _mrow(mod_ref, SC1), nw_ref[...]
        _, n, xhat, rr = _norm_mod(x_ref[...], nw, _mrow(mod_ref, SH1), sc)
        dxn, dsh, dsc, dnw = _norm_mod_bwd(dh, n, xhat, rr, nw, sc)
        dx_ref[...] = jnp.where(is_ctx, 0.0, dx1_ref[...]) + dxn
        s = i // nlb
        dmod_ref[s, SH1 : SH1 + 1, :] += dsh
        dmod_ref[s, SC1 : SC1 + 1, :] += dsc
        dnw_ref[0:1, :] += dnw

    return _rowcall(
        "ret_in_bwd", body, nblk, nlb,
        [(dqf, "row"), (dkf, "row"), (dvf, "row"), (dqb, "row"), (dkb, "row"), (dvb, "row"), (dgate, "row"),
         (w_in, "full"), (x, "row"), (dx1, "row"), (mod, "stream"), (nw1, "full")],
        [((r, D), F32, "row"), ((r, nin), BF16, "row"), ((8, D), F32, "acc"), ((2, 8, D), F32, "acc")],
    )


def loss_head(xout, target, nlb):
    r = xout.shape[0]

    def body(x_ref, t_ref, dx_ref, l_ref):
        _acc_init(pl.program_id(0), l_ref)
        err = x_ref[...] - t_ref[...]
        dx_ref[...] = err * (1.0 / D)
        l_ref[...] += 0.5 * jnp.sum(jnp.mean(err * err, axis=-1, keepdims=True))

    return _rowcall(
        "loss_head", body, nlb, nlb,
        [(xout, "row"), (target, "row")],
        [((r, D), F32, "row"), ((8, 128), F32, "acc")],
    )


N_MIXERS = 3
POOL, ATTN, RET = range(3)


def _layer_plan(depth):
    plan = []
    for i in range(depth):
        kind = i % N_MIXERS
        ctx_out = any(k % N_MIXERS != POOL for k in range(i + 1, depth))
        plan.append((kind, i // N_MIXERS, ctx_out or kind != POOL, ctx_out))
    return plan


def local_step(xs, target, mods, w, nlb, depth, fetch, emit):
    nb = nlb + 1
    plan = _layer_plan(depth)
    saved = []
    x = xs
    for i, (kind, j, ctx_in, ctx_out) in enumerate(plan):
        nmix = nb if ctx_out else nlb
        mod, nw1, nw2 = mods[i], w["nw"][i, 0:1], w["nw"][i, 1:2]
        lw = fetch(i, MIX, x)
        sv = {"x": x, "lw": lw}
        if kind == POOL:
            x1, sv["ypre"] = pool_fwd(x, mod, nw1, lw["pool_w"], (w["pbs"], j), nmix, nlb)
        elif kind == ATTN:
            assert ctx_out
            sv["raw"], sv["q"], sv["k"], sv["v"], sv["h"] = qkv_fwd(x, mod, nw1, lw["attn_w_qkv"], w["gains"], w["cos"], w["sin"], nb, nlb)
            sv["o"], sv["lse"] = attn_fwd(sv["q"], sv["k"], sv["v"], nb, nlb)
            x1, sv["y"] = attn_out_fwd(sv["o"], lw["attn_w_o"], x, mod, nb, nlb)
        else:
            assert ctx_in and not ctx_out
            sv["q"], sv["k"], sv["v"], sv["g"], sv["h"] = ret_in_fwd(x, mod, nw1, lw["ret_w_in"], nb, nlb)
            sv["o_f"], sv["o_b"], sv["st_f"], sv["st_b"] = ret_scan_fwd(sv["q"], sv["k"], sv["v"], w["logit_b"], nlb)
            x1, sv["y"], sv["z"] = ret_out_fwd(sv["o_f"], sv["o_b"], sv["g"], w["gnw"], lw["ret_w_out"], x, mod, nlb, nlb)
        sv["x1"] = x1
        lw.update(fetch(i, FFN, x1))
        sv["u"], sv["h2"] = ffn_up(x1, mod, nw2, lw["ffn_w_up"], nmix, nlb)
        x, sv["f"], sv["uc"] = ffn_down(sv["u"], (w["cw"], i), lw["ffn_w_down"], x1, mod, nmix, nlb)
        saved.append(sv)

    dx, loss_tile = loss_head(x, target, nlb)
    g = {k: [None] * depth for k in ("dcw", "dnw1", "dnw2", "dmod")}
    dep = loss_tile
    for i in reversed(range(depth)):
        kind, j, ctx_in, ctx_out = plan[i]
        sv = saved[i]
        lw, big = sv["lw"], {}
        nmix = nb if ctx_out else nlb
        mod, nw1, nw2 = mods[i], w["nw"][i, 0:1], w["nw"][i, 1:2]
        duc, big["ffn_w_down"], dcb, dmod = ffn_bwd1(dx, sv["f"], sv["uc"], lw["ffn_w_down"], mod, dep, nmix, nlb)
        du, dx1, dcw, g["dnw2"][i], dm = ffn_bwd3(duc, sv["u"], (w["cw"], i), lw["ffn_w_up"], sv["x1"], dx, mod, nw2, nmix, nlb)
        g["dcw"][i] = dcw + dcb
        dmod = dmod + dm
        big["ffn_w_up"] = matmul_tn(sv["h2"], du, nmix)
        dep, big = emit(i, FFN, big), {}
        if kind == POOL:
            dx, dpw, dpbs, g["dnw1"][i], dm = pool_bwd(dx1, sv["x"], sv["ypre"], mod, nw1, lw["pool_w"], (w["pbs"], j), dep, nmix, nlb, i == 0)
            big["pool_w"] = dpw.astype(BF16)
            g.setdefault("dpbs", {})[j] = dpbs
        elif kind == ATTN:
            dy, do, dm1 = mix_out_bwd("attn_out_bwd", dx1, sv["y"], lw["attn_w_o"], mod, dep, nb, nlb)
            big["attn_w_o"] = matmul_tn(sv["o"], dy, nb)
            dq, dk, dv = attn_bwd(sv["q"], sv["k"], sv["v"], sv["o"], do, sv["lse"], nb, nlb)
            dx, draw, g["dgains"], g["dnw1"][i], dm = qkv_bwd(
                dq, dk, dv, sv["raw"], w["gains"], w["cos"], w["sin"], lw["attn_w_qkv"], sv["x"], dx1, mod, nw1, nb, nlb, False)
            big["attn_w_qkv"] = matmul_tn(sv["h"], draw, nb)
            dm = dm + dm1
        else:
            dy, do, dgate, g["dgnw"], dm1 = ret_out_bwd(dx1, sv["y"], sv["o_f"], sv["o_b"], sv["g"], w["gnw"], lw["ret_w_out"], mod, dep, nlb, nlb)
            big["ret_w_out"] = matmul_tn(sv["z"], dy, nlb)
            dqf, dkf, dvf, dqb, dkb, dvb, g["dlogit"] = ret_scan_bwd(sv["q"], sv["k"], sv["v"], do, sv["st_f"], sv["st_b"], w["logit_b"], nlb)
            dx, din, g["dnw1"][i], dm = ret_in_bwd(dqf, dkf, dvf, dqb, dkb, dvb, dgate, lw["ret_w_in"], sv["x"], dx1, mod, nw1, nb, nlb)
            big["ret_w_in"] = matmul_tn(sv["h"], din, nb)
            dm = dm + dm1
        g["dmod"][i] = dmod + dm
        dep = emit(i, MIX, big)
    return loss_tile, dx, g


MESH_ID = pl.DeviceIdType.MESH
CHIP_FLIPS = ((1, 0), (0, 1), (1, 1))


def _pos():
    return lax.axis_index("x"), lax.axis_index("y"), lax.axis_index("c")


def _flip(v, b):
    return 1 - v if b else v


def small_all_gather(name, x):
    rows, n = x.shape

    def body(x_ref, out_ref, send_sems, recv_sems, local_sem):
        mx, my, mc = _pos()
        me = 4 * mx + 2 * my + mc
        mine = pltpu.make_async_copy(x_ref, out_ref.at[me], local_sem)
        mine.start()
        sends, peers = [], []
        for kk in range(1, N_DEV):
            peer = (_flip(mx, (kk >> 2) & 1), _flip(my, (kk >> 1) & 1), _flip(mc, kk & 1))
            cp = pltpu.make_async_remote_copy(src_ref=x_ref, dst_ref=out_ref.at[me], send_sem=send_sems.at[kk - 1],
                                              recv_sem=recv_sems.at[kk - 1], device_id=peer, device_id_type=MESH_ID)
            cp.start()
            sends.append(cp)
            peers.append(peer)
        for kk, peer in enumerate(peers):
            pidx = 4 * peer[0] + 2 * peer[1] + peer[2]
            pltpu.make_async_remote_copy(src_ref=x_ref, dst_ref=out_ref.at[pidx], send_sem=send_sems.at[kk],
                                         recv_sem=recv_sems.at[kk], device_id=peer, device_id_type=MESH_ID).wait_recv()
        for cp in sends:
            cp.wait_send()
        mine.wait()

    return _pcall(
        body,
        name=name,
        out_shape=jax.ShapeDtypeStruct((N_DEV, rows, n), x.dtype),
        in_specs=[pl.BlockSpec(memory_space=pltpu.VMEM)],
        out_specs=pl.BlockSpec(memory_space=pltpu.VMEM),
        scratch_shapes=[pltpu.SemaphoreType.DMA((N_DEV - 1,)), pltpu.SemaphoreType.DMA((N_DEV - 1,)), pltpu.SemaphoreType.DMA],
        compiler_params=pltpu.CompilerParams(vmem_limit_bytes=VMEM_LIMIT),
    )(x)


def _hbm_exchange(name, ins, out_shapes, plan):
    n_in = len(ins)
    probe_local, probe_remote = plan([None] * n_in, [None] * len(out_shapes), probe=True)

    def body(*refs):
        in_refs, out_refs = refs[:n_in], refs[n_in : n_in + len(out_shapes)]
        send_sems, recv_sems, local_sems = refs[n_in + len(out_shapes) :]
        local, remote = plan(in_refs, out_refs, probe=False)
        lcs = [pltpu.make_async_copy(s, d, local_sems.at[k]) for k, (s, d) in enumerate(local)]
        for cp in lcs:
            cp.start()
        rcs = []
        for k, (s, d, peer, _) in enumerate(remote):
            cp = pltpu.make_async_remote_copy(src_ref=s, dst_ref=d, send_sem=send_sems.at[k], recv_sem=recv_sems.at[k],
                                              device_id=peer, device_id_type=MESH_ID)
            cp.start()
            rcs.append(cp)
        for k, (s, _, peer, here) in enumerate(remote):
            pltpu.make_async_remote_copy(src_ref=s, dst_ref=here, send_sem=send_sems.at[k], recv_sem=recv_sems.at[k],
                                         device_id=peer, device_id_type=MESH_ID).wait_recv()
        for cp in rcs:
            cp.wait_send()
        for cp in lcs:
            cp.wait()

    return _pcall(
        body,
        name=name,
        out_shape=list(out_shapes),
        in_specs=[pl.BlockSpec(memory_space=pl.ANY)] * n_in,
        out_specs=[pl.BlockSpec(memory_space=pl.ANY)] * len(out_shapes),
        scratch_shapes=[pltpu.SemaphoreType.DMA((max(probe_remote, 1),)), pltpu.SemaphoreType.DMA((max(probe_remote, 1),)),
                        pltpu.SemaphoreType.DMA((max(probe_local, 1),))],
    )(*ins)


def _at_axis(ref, axis, start, size):
    return ref.at[tuple(pl.ds(start, size) if a == axis else slice(None) for a in range(len(ref.shape)))]


HBM_SPEC = pl.BlockSpec(memory_space=pltpu.HBM)
SEM_SPEC = pl.BlockSpec(memory_space=pltpu.SEMAPHORE)
SIDE_EFFECT = pltpu.SideEffectType.DATAFLOW_SIDE_EFFECTING


def _in_hbm(a):
    return pltpu.with_memory_space_constraint(a, pltpu.HBM)


def _copies_start(name, bufs, counts, plan, after=None):
    n, ng = len(bufs), len(counts)
    extra = [] if after is None else [after]
    n_in = n + len(extra)

    def body(*refs):
        sems = refs[n_in : n_in + 2 * ng]
        token = refs[n_in + 2 * ng + n]
        for gi, copies in enumerate(plan(refs[:n])):
            for k, (s, d, peer) in enumerate(copies):
                pltpu.make_async_remote_copy(src_ref=s, dst_ref=d, send_sem=sems[2 * gi].at[k], recv_sem=sems[2 * gi + 1].at[k],
                                             device_id=peer, device_id_type=MESH_ID).start()
        token[...] = jnp.zeros(token.shape, token.dtype)

    out = _pcall(
        body,
        name=name,
        out_shape=tuple(pltpu.SemaphoreType.DMA((c,)) for c in counts for _ in range(2))
        + tuple(pltpu.HBM(b.shape, b.dtype) for b in bufs) + (jax.ShapeDtypeStruct((8, 128), F32),),
        in_specs=(HBM_SPEC,) * n + (pl.BlockSpec(memory_space=pl.ANY),) * len(extra),
        out_specs=(SEM_SPEC,) * (2 * ng) + (HBM_SPEC,) * n + (pl.BlockSpec(memory_space=pltpu.VMEM),),
        input_output_aliases={i: 2 * ng + i for i in range(n)},
        compiler_params=pltpu.CompilerParams(has_side_effects=SIDE_EFFECT),
    )(*[_in_hbm(b) for b in bufs], *extra)
    return [(out[2 * g], out[2 * g + 1]) for g in range(ng)], list(out[2 * ng : 2 * ng + n]), out[2 * ng + n]


def _copies_wait(name, sems, bufs, plan, after):
    n, ng = len(bufs), len(sems)

    def body(*refs):
        for gi, copies in enumerate(plan(refs[:n])):
            for k, (s, d, peer) in enumerate(copies):
                cp = pltpu.make_async_remote_copy(src_ref=s, dst_ref=d, send_sem=refs[n + 2 * gi].at[k], recv_sem=refs[n + 2 * gi + 1].at[k],
                                                  device_id=peer, device_id_type=MESH_ID)
                cp.wait_send()
                cp.wait_recv()

    out = _pcall(
        body,
        name=name,
        out_shape=tuple(pltpu.HBM(b.shape, b.dtype) for b in bufs),
        in_specs=(HBM_SPEC,) * n + (SEM_SPEC,) * (2 * ng) + (pl.BlockSpec(memory_space=pl.ANY),),
        out_specs=(HBM_SPEC,) * n,
        input_output_aliases={i: i for i in range(n)},
        compiler_params=pltpu.CompilerParams(has_side_effects=SIDE_EFFECT),
    )(*bufs, *[s for pair in sems for s in pair], after)
    return list(out)


def _matrix_groups(depth):
    out = []
    for i, (kind, j, _, _) in enumerate(_layer_plan(depth)):
        out.append(([("pool_w", j, 1)], [("attn_w_qkv", j, 1), ("attn_w_o", j, 0)], [("ret_w_in", j, 1), ("ret_w_out", j, 0)])[kind])
        out.append([("ffn_w_up", i, 1), ("ffn_w_down", i, 0)])
    return out


MIX, FFN = 0, 1


def _peers(mx, my, mc):
    out = []
    for fx, fy in CHIP_FLIPS:
        px, py = _flip(mx, fx), _flip(my, fy)
        out.append(((px, py, mc), 2 * px + py))
    return out


def full_buffers(shards, names, layers):
    out = []
    for name, _, axis in [e for layer in layers for e in layer]:
        shp = list(shards[names.index(name)].shape[1:])
        shp[axis] *= N_CHIP
        out.append(lax.empty(tuple(shp), BF16))
    return out


def _gather_plan(names, layers, group, n_shards, here):
    def plan(refs):
        mx, my, mc = _pos()
        s_refs, f_refs = refs[:n_shards], refs[n_shards:]
        groups, k = [], 0
        for gi, layer in enumerate(layers):
            if group is not None and gi != group:
                continue
            copies = []
            for name, idx, axis in layer:
                src = s_refs[names.index(name)].at[idx]
                n = src.shape[axis]
                for peer, pchip in _peers(mx, my, mc) + [((mx, my, 1 - mc), 2 * mx + my)]:
                    at = pchip if here else 2 * mx + my
                    copies.append((src, _at_axis(f_refs[k], axis, at * n, n), peer))
                k += 1
            groups.append(copies)
        return groups

    return plan


def gather_start(shards, names, layers, fulls, after):
    counts = [len(layer) * (len(CHIP_FLIPS) + 1) for layer in layers]
    sems, bufs, _ = _copies_start("gather_start", list(shards) + list(fulls), counts,
                                  _gather_plan(names, layers, None, len(shards), False), after)
    return sems, bufs[: len(shards)], bufs[len(shards) :]


def gather_wait(g, sems_g, shards, names, layers, fulls_g, after):
    bufs = _copies_wait(f"gather_wait_{g}", [sems_g], list(shards) + list(fulls_g), _gather_plan(names, layers, g, len(shards), True), after)
    return bufs[: len(shards)], bufs[len(shards) :]


def _scatter_plan(layer_entries, n_grads, land_of):
    def plan(refs):
        mx, my, mc = _pos()
        groups, k = [], 0
        for layer in layer_entries:
            copies = []
            for name, idx, axis in layer:
                gref, land = refs[k], refs[n_grads + land_of(k, name)]
                n = gref.shape[axis] // N_CHIP
                for slot, (peer, pchip) in enumerate(_peers(mx, my, mc)):
                    copies.append((_at_axis(gref, axis, pchip * n, n), land.at[slot, idx], peer))
                k += 1
            groups.append(copies)
        return groups

    return plan


def scatter_start(i, layer, grads, lands):
    sems, bufs, token = _copies_start(f"scatter_start_{i}", list(grads) + list(lands), [len(layer) * len(CHIP_FLIPS)],
                                      _scatter_plan([layer], len(grads), lambda k, name: k))
    return sems[0], bufs[: len(grads)], bufs[len(grads) :], token


def scatter_wait(sems, layers, grads, names, lands, after):
    bufs = _copies_wait("scatter_wait", sems, list(grads) + list(lands),
                        _scatter_plan(layers, len(grads), lambda k, name: names.index(name)), after)
    return bufs[: len(grads)], bufs[len(grads) :]


def sibling_swap(parts):
    def plan(in_refs, out_refs, probe):
        if probe:
            return 0, len(parts)
        mx, my, mc = _pos()
        return [], [(s, o, (mx, my, 1 - mc), o) for s, o in zip(in_refs, out_refs)]

    return _hbm_exchange("sibling_swap", parts, [jax.ShapeDtypeStruct(p.shape, p.dtype) for p in parts], plan)


EW_ROWS = 256


def _ew_call(name, fn, ins, n_out):
    rows, cols = ins[0].shape[-2:]
    tr = EW_ROWS if rows % EW_ROWS == 0 else rows

    def body(*refs):
        outs = fn(*[r[...] for r in refs[: len(ins)]])
        for o_ref, o in zip(refs[len(ins) :], outs):
            o_ref[...] = o

    def spec(a):
        if a.ndim == 3:
            return pl.BlockSpec((a.shape[0], tr, cols), lambda i: (0, i, 0))
        return pl.BlockSpec((tr, cols), lambda i: (i, 0))

    return _pcall(
        body,
        name=name,
        grid=(rows // tr,),
        in_specs=[spec(a) for a in ins],
        out_specs=[pl.BlockSpec((tr, cols), lambda i: (i, 0))] * n_out,
        out_shape=[jax.ShapeDtypeStruct((rows, cols), F32)] * n_out,
        compiler_params=pltpu.CompilerParams(dimension_semantics=("parallel",), vmem_limit_bytes=VMEM_LIMIT),
    )(*ins)


def _adamw(w, g, m, v):
    m = ADAM_B1 * m + (1.0 - ADAM_B1) * g
    v = ADAM_B2 * v + (1.0 - ADAM_B2) * (g * g)
    m_hat = m / (1.0 - ADAM_B1 ** ADAM_STEP)
    v_hat = v / (1.0 - ADAM_B2 ** ADAM_STEP)
    return -ADAM_LR * (m_hat / (jnp.sqrt(v_hat) + ADAM_EPS) + ADAM_WD * w), m, v


def sum_slots(name, own, landing):
    def fn(o, l):
        acc = o.astype(F32)
        for k in range(l.shape[0]):
            acc = acc + l[k].astype(F32)
        return (acc,)

    return _ew_call(name, fn, [own, landing], 1)[0]


def adamw_pair(name, w, m, v, p, ps):
    def fn(w, m, v, p, ps):
        g = p + ps
        return (g,) + _adamw(w, g, m, v)

    return _ew_call(name, fn, [w, m, v, p, ps], 4)


def adamw_one(name, w, m, v, g):
    return _ew_call(name, lambda w, m, v, g: _adamw(w, g, m, v), [w, m, v, g], 3)


def reduce_devices(name, x):
    def fn(a):
        acc = a[0]
        for k in range(1, a.shape[0]):
            acc = acc + a[k]
        return (acc,)

    return _ew_call(name, fn, [x], 1)[0]


ADA_ROWS = 16
ADA_CTX = N_DEV


def ada_fwd(s9, ada_w, ada_b):
    depth, _, n = ada_w.shape

    def body(s_ref, w_ref, b_ref, o_ref):
        s = s_ref[...]
        o_ref[...] = _dotf((s * _sigmoid(s)).astype(BF16), w_ref[...].astype(BF16)) + b_ref[...]

    return _pcall(
        body,
        name="ada_fwd",
        grid=(depth,),
        in_specs=[pl.BlockSpec((ADA_ROWS, D), lambda i: (0, 0)), pl.BlockSpec((None, D, n), lambda i: (i, 0, 0)),
                  pl.BlockSpec((None, 1, n), lambda i: (i, 0, 0))],
        out_specs=pl.BlockSpec((None, ADA_ROWS, n), lambda i: (i, 0, 0)),
        out_shape=jax.ShapeDtypeStruct((depth, ADA_ROWS, n), F32),
        compiler_params=pltpu.CompilerParams(dimension_semantics=("arbitrary",), vmem_limit_bytes=VMEM_LIMIT),
    )(s9, ada_w, ada_b)


def ada_bwd(s9, dm, ada_w):
    depth, _, n = ada_w.shape

    def body(s_ref, dm_ref, w_ref, gw_ref, ds_ref):
        _acc_init(pl.program_id(0), ds_ref)
        s = s_ref[...]
        dmb = dm_ref[...].astype(BF16)
        gw_ref[...] = _dot_tn((s * _sigmoid(s)).astype(BF16), dmb)
        ds_ref[...] += _dot_nt(dmb, w_ref[...].astype(BF16))

    return _pcall(
        body,
        name="ada_bwd",
        grid=(depth,),
        in_specs=[pl.BlockSpec((ADA_ROWS, D), lambda i: (0, 0)), pl.BlockSpec((None, ADA_ROWS, n), lambda i: (i, 0, 0)),
                  pl.BlockSpec((None, D, n), lambda i: (i, 0, 0))],
        out_specs=[pl.BlockSpec((None, D, n), lambda i: (i, 0, 0)), pl.BlockSpec((ADA_ROWS, D), lambda i: (0, 0))],
        out_shape=[jax.ShapeDtypeStruct((depth, D, n), F32), jax.ShapeDtypeStruct((ADA_ROWS, D), F32)],
        compiler_params=pltpu.CompilerParams(dimension_semantics=("arbitrary",), vmem_limit_bytes=VMEM_LIMIT),
    )(s9, dm, ada_w)


def cctx_grad(parts, c_ctx):
    def body(p_ref, c_ref, o_ref):
        acc = p_ref[0, ADA_CTX : ADA_CTX + 1, :]
        for chip in range(1, N_CHIP):
            acc = acc + p_ref[2 * chip, ADA_CTX : ADA_CTX + 1, :]
        c = c_ref[...]
        sg = _sigmoid(c)
        o_ref[...] = acc * (sg * (1.0 + c * (1.0 - sg)))

    return _pcall(body, name="cctx_grad", out_shape=jax.ShapeDtypeStruct((1, D), F32))(parts, c_ctx)


def _pack(arrs):
    flat = jnp.concatenate([a.astype(F32).reshape(-1) for a in arrs])
    rows = -(-flat.shape[0] // (8 * D)) * 8
    return jnp.pad(flat, (0, rows * D - flat.shape[0])).reshape(rows, D)


def _unpack(slab, shapes):
    lead = slab.shape[:-2]
    flat = slab.reshape(lead + (-1,))
    out, off = [], 0
    for shp in shapes:
        n = 1
        for d in shp:
            n *= d
        out.append(flat[..., off : off + n].reshape(lead + tuple(shp)))
        off += n
    return out


def _unshard(per_dev, axis):
    return jnp.concatenate([per_dev[2 * chip] for chip in range(N_CHIP)], axis=axis)


BIG = (("pool_w", 1), ("attn_w_qkv", 1), ("attn_w_o", 0), ("ret_w_in", 1), ("ret_w_out", 0), ("ffn_w_up", 1), ("ffn_w_down", 0))
WEIGHTS = ("c_ctx", "ada_w", "ada_b", "norm_w", "pool_w", "pool_b", "pool_scale", "attn_w_qkv", "attn_q_gain", "attn_k_gain",
           "attn_w_o", "ret_w_in", "ret_decay_logit", "ret_gn_w", "ret_w_out", "ffn_w_up", "ffn_conv_w", "ffn_conv_b", "ffn_w_down")
SMALL = tuple(n for n in WEIGHTS if n != "ada_w" and n not in dict(BIG))
SMALL_SHARD_AXIS = {"norm_w": 2, "pool_b": 1, "pool_scale": 1, "ret_gn_w": 1, "ffn_conv_w": 2}


def kernel(x, c, ctx, c_ctx, ada_w, ada_b, norm_w, pool_w, pool_b, pool_scale, attn_w_qkv, attn_q_gain, attn_k_gain, attn_w_o, ret_w_in, ret_decay_logit, ret_gn_w, ret_w_out, ffn_w_up, ffn_conv_w, ffn_conv_b, ffn_w_down, loss_target, m_c_ctx, m_ada_w, m_ada_b, m_norm_w, m_pool_w, m_pool_b, m_pool_scale, m_attn_w_qkv, m_attn_q_gain, m_attn_k_gain, m_attn_w_o, m_ret_w_in, m_ret_decay_logit, m_ret_gn_w, m_ret_w_out, m_ffn_w_up, m_ffn_conv_w, m_ffn_conv_b, m_ffn_w_down, v_c_ctx, v_ada_w, v_ada_b, v_norm_w, v_pool_w, v_pool_b, v_pool_scale, v_attn_w_qkv, v_attn_q_gain, v_attn_k_gain, v_attn_w_o, v_ret_w_in, v_ret_decay_logit, v_ret_gn_w, v_ret_w_out, v_ffn_w_up, v_ffn_conv_w, v_ffn_conv_b, v_ffn_w_down):
    P = dict(zip(WEIGHTS, (c_ctx, ada_w, ada_b, norm_w, pool_w, pool_b, pool_scale, attn_w_qkv, attn_q_gain, attn_k_gain, attn_w_o,
                           ret_w_in, ret_decay_logit, ret_gn_w, ret_w_out, ffn_w_up, ffn_conv_w, ffn_conv_b, ffn_w_down)))
    M = dict(zip(WEIGHTS, (m_c_ctx, m_ada_w, m_ada_b, m_norm_w, m_pool_w, m_pool_b, m_pool_scale, m_attn_w_qkv, m_attn_q_gain,
                           m_attn_k_gain, m_attn_w_o, m_ret_w_in, m_ret_decay_logit, m_ret_gn_w, m_ret_w_out, m_ffn_w_up,
                           m_ffn_conv_w, m_ffn_conv_b, m_ffn_w_down)))
    V = dict(zip(WEIGHTS, (v_c_ctx, v_ada_w, v_ada_b, v_norm_w, v_pool_w, v_pool_b, v_pool_scale, v_attn_w_qkv, v_attn_q_gain,
                           v_attn_k_gain, v_attn_w_o, v_ret_w_in, v_ret_decay_logit, v_ret_gn_w, v_ret_w_out, v_ffn_w_up,
                           v_ffn_conv_w, v_ffn_conv_b, v_ffn_w_down)))
    depth, s, l = ada_w.shape[0], x.shape[1], ctx.shape[1]
    assert l == BM and s % BM == 0 and s % GRID_W == 0
    nlb = s // BM
    n_pool = pool_w.shape[0]
    mx, my, mc = _pos()
    chip, dev = 2 * mx + my, 4 * mx + 2 * my + mc
    nada = ada_w.shape[2]

    sharded = [n for n in SMALL if n in SMALL_SHARD_AXIS]
    got = small_all_gather("gather_small", _pack([c[0]] + [P[n] for n in sharded]))
    got = _unpack(got, [(D,)] + [P[n].shape for n in sharded])
    c_all = got[0]
    full = {n: _unshard(g_, SMALL_SHARD_AXIS[n]) for n, g_ in zip(sharded, got[1:])}

    s9 = jnp.concatenate([c_all, c_ctx[None, :], jnp.zeros((ADA_ROWS - N_DEV - 1, D), F32)], axis=0)
    ada_b_mine = lax.dynamic_slice_in_dim(ada_b, chip * nada, nada, axis=1)[:, None, :]
    mod_part = ada_fwd(s9, ada_w, ada_b_mine)
    mod_all = _unshard(small_all_gather("gather_mod", mod_part.reshape(depth * ADA_ROWS, nada)), 1).reshape(depth, ADA_ROWS, 6, D)
    mod_mine = lax.dynamic_index_in_dim(mod_all, dev, axis=1, keepdims=False)
    mods_all = jnp.pad(jnp.stack([mod_mine, mod_all[:, ADA_CTX]], axis=1), ((0, 0), (0, 0), (0, 2), (0, 0)))
    mods = [mods_all[i] for i in range(depth)]

    names = [n for n, _ in BIG]
    layers = _matrix_groups(depth)
    shards = [P[n].astype(BF16) for n in names]
    gsems, shards, fulls = gather_start(shards, names, layers, full_buffers(shards, names, layers), mods_all)
    first = [sum(len(layer) for layer in layers[:g_]) for g_ in range(len(layers) + 1)]
    flight = {"shards": shards}

    def fetch(i, part, after):
        g_ = 2 * i + part
        flight["shards"], mats = gather_wait(g_, gsems[g_], flight["shards"], names, layers, fulls[first[g_] : first[g_ + 1]], after)
        return {name: m for (name, _, _), m in zip(layers[g_], mats)}

    lands = {n: lax.empty((len(CHIP_FLIPS),) + P[n].shape, BF16) for n in names}
    sent = {}

    def emit(i, part, big):
        g_ = 2 * i + part
        lnames = [name for name, _, _ in layers[g_]]
        sems, gl, ll, token = scatter_start(g_, layers[g_], [big[n] for n in lnames], [lands[n] for n in lnames])
        lands.update(zip(lnames, ll))
        sent[g_] = (sems, gl)
        return token

    w = {
        "nw": full["norm_w"],
        "pbs": jnp.concatenate([full["pool_b"][:, None], full["pool_scale"][:, None], jnp.zeros((n_pool, 6, D), F32)], axis=1),
        "gains": jnp.concatenate([attn_q_gain, attn_k_gain, jnp.zeros((6, HD), F32)], axis=0),
        "gnw": full["ret_gn_w"],
        "logit_b": jnp.broadcast_to(ret_decay_logit[0][:, :, None, None], (2, RET_HEADS, 8, 128)),
        "cw": jnp.concatenate([full["ffn_conv_w"], ffn_conv_b[:, None, :], jnp.zeros((depth, 4, 2 * D_FF), F32)], axis=1),
    }
    w["cos"], w["sin"] = rope_tables(s, l)

    xs = jnp.concatenate([x[0], ctx[0]], axis=0)
    loss_tile, dxs, g = local_step(xs, loss_target[0], mods, w, nlb, depth, fetch, emit)
    loss = lax.psum(loss_tile[0, 0], MESH_AXES)
    grad_x = dxs[None]

    small_shapes = [(depth, 2, 8, D), (depth, 2, D), (n_pool, 2, D), (2, HD), (2, RET_HEADS), (RNV,), (depth, 4, 2 * D_FF)]
    slab = _pack([
        jnp.stack(g["dmod"]),
        jnp.stack([jnp.stack([g["dnw1"][i][0], g["dnw2"][i][0]]) for i in range(depth)]),
        jnp.stack([g["dpbs"][j][0:2] for j in range(n_pool)]),
        g["dgains"][0:2], g["dlogit"][:, :, 0, 0], g["dgnw"][0], jnp.stack([g["dcw"][i][0:4] for i in range(depth)]),
    ])
    slabs = small_all_gather("gather_small_grads", slab)
    dmod_dev = _unpack(slabs, small_shapes[:1])[0]
    t_dmod, t_nw, t_pbs, t_gains, t_logit, t_gnw, t_cw = _unpack(reduce_devices("reduce_small_grads", slabs), small_shapes)

    def cols(a):
        return lax.dynamic_slice_in_dim(a, chip * nada, nada, axis=a.ndim - 1)

    dm_lat = jnp.swapaxes(cols(dmod_dev[:, :, 0, :6].reshape(N_DEV, depth, 6 * D)), 0, 1)
    dm_ctx = cols(t_dmod[:, 1, :6].reshape(depth, 1, 6 * D))
    dm = jnp.concatenate([dm_lat, dm_ctx, jnp.zeros((depth, ADA_ROWS - N_DEV - 1, nada), F32)], axis=1)
    g_ada_w, ds9 = ada_bwd(s9, dm, ada_w)
    g_c_ctx = cctx_grad(small_all_gather("gather_dcctx", ds9), c_ctx[None, :])[0]

    def mine(a, name):
        n = P[name].shape[SMALL_SHARD_AXIS[name]]
        return lax.dynamic_slice_in_dim(a, chip * n, n, axis=SMALL_SHARD_AXIS[name])

    G = {
        "c_ctx": g_c_ctx,
        "ada_b": (t_dmod[:, 0, :6] + t_dmod[:, 1, :6]).reshape(depth, 6 * D),
        "norm_w": mine(t_nw, "norm_w"),
        "pool_b": mine(t_pbs[:, 0], "pool_b"), "pool_scale": mine(t_pbs[:, 1], "pool_scale"),
        "attn_q_gain": t_gains[0:1], "attn_k_gain": t_gains[1:2],
        "ret_decay_logit": t_logit[None], "ret_gn_w": mine(t_gnw[None], "ret_gn_w"),
        "ffn_conv_w": mine(t_cw[:, 0:3], "ffn_conv_w"), "ffn_conv_b": t_cw[:, 3],
    }
    sw, sg, sm, sv = (_pack([d_[n] for n in SMALL]) for d_ in (P, G, M, V))
    outs = adamw_one("adamw_small", sw, sm, sv, sg)
    D_, NM, NV = ({n: a for n, a in zip(SMALL, _unpack(o, [P[n].shape for n in SMALL]))} for o in outs)

    flat2 = lambda a: a.reshape(-1, a.shape[-1])
    G["ada_w"] = g_ada_w
    o3 = adamw_one("adamw_ada", flat2(ada_w), flat2(M["ada_w"]), flat2(V["ada_w"]), flat2(g_ada_w))
    D_["ada_w"], NM["ada_w"], NV["ada_w"] = (o.reshape(ada_w.shape) for o in o3)

    sent_grads, landed = scatter_wait([sent[g_][0] for g_ in range(len(layers))], layers,
                                      [a for g_ in range(len(layers)) for a in sent[g_][1]], names, [lands[n] for n in names], o3[0])
    own = {n: [None] * P[n].shape[0] for n in names}
    for (name, idx, axis), a in zip([e for layer in layers for e in layer], sent_grads):
        n_ = a.shape[axis] // N_CHIP
        own[name][idx] = lax.dynamic_slice_in_dim(a, chip * n_, n_, axis=axis)
    partial = [sum_slots("sum_" + n, jnp.stack(own[n]).reshape(-1, lnd.shape[-1]), lnd.reshape(len(CHIP_FLIPS), -1, lnd.shape[-1]))
               for n, lnd in zip(names, landed)]
    theirs = sibling_swap(partial)
    for (n, _), p, ps in zip(BIG, partial, theirs):
        o4 = adamw_pair("adamw_" + n, flat2(P[n]), flat2(M[n]), flat2(V[n]), p, ps)
        G[n], D_[n], NM[n], NV[n] = (o.reshape(P[n].shape) for o in o4)

    return (loss, grad_x, *[G[n] for n in WEIGHTS], *[D_[n] for n in WEIGHTS], *[NM[n] for n in WEIGHTS], *[NV[n] for n in WEIGHTS])
```

```python
import functools

import jax
import jax.numpy as jnp
from jax import lax
from jax.experimental import pallas as pl
from jax.experimental.pallas import tpu as pltpu

F32 = jnp.float32
BF16 = jnp.bfloat16

D = 1024
BM = 256
EPS = 1e-6
POOL_WINDOWS = (2, 4, 8, 16)
POOL_GROUP = D // 4
ATTN_HEADS = 8
ATTN_KV_HEADS = 2
HD = D // ATTN_HEADS
ATTN_GROUP = ATTN_HEADS // ATTN_KV_HEADS
NQ = ATTN_HEADS * HD
NKV = ATTN_KV_HEADS * HD
GRID_W = 64
ROPE_THETA = 10000.0
RET_HEADS = 4
RET_DK = D // RET_HEADS
RET_DV = 2 * D // RET_HEADS
RNQ = RET_HEADS * RET_DK
RNV = RET_HEADS * RET_DV
D_FF = 2816
FF_CHUNK = 256
ADAM_LR, ADAM_B1, ADAM_B2, ADAM_EPS, ADAM_WD, ADAM_STEP = 0.001, 0.9, 0.999, 1e-08, 0.01, 10
HALO_F32 = 8
HALO_BF16 = 16
VMEM_LIMIT = 60 * 1024 * 1024

MESH_AXES = ("x", "y", "c")
N_DEV = 8
N_CHIP = 4


def _pcall(body, **kw):
    return pl.pallas_call(body, **kw)


def _spec(shape, kind, nlb, nblk):
    nd = len(shape)
    if kind == "row":
        return pl.BlockSpec((BM, shape[1]), lambda i: (i, 0))
    if kind == "row_lat":
        return pl.BlockSpec((BM, shape[1]), lambda i: (jnp.minimum(i, nlb - 1), 0))
    if kind == "full":
        return pl.BlockSpec(tuple(shape), lambda i: (0,) * nd, pipeline_mode=pl.Buffered(1))
    if isinstance(kind, tuple) and kind[0] == "fullat":
        return pl.BlockSpec((None,) + tuple(shape[1:]), lambda i: (kind[1],) + (0,) * (nd - 1), pipeline_mode=pl.Buffered(1))
    if kind == "acc":
        return pl.BlockSpec(tuple(shape), lambda i: (0,) * nd)
    if kind == "any":
        return pl.BlockSpec(memory_space=pl.ANY)
    if kind == "stream":
        return pl.BlockSpec((1,) + tuple(shape[1:]), lambda i: (i // nlb,) + (0,) * (nd - 1))
    if kind in ("prev8", "prev16"):
        hb = int(kind[4:])
        return pl.BlockSpec((hb, shape[1]), lambda i: (jnp.maximum(i * (BM // hb) - 1, 0), 0))
    if kind in ("next8", "next16"):
        hb = int(kind[4:])
        last = nblk * BM // hb - 1
        return pl.BlockSpec((hb, shape[1]), lambda i: (jnp.minimum((i + 1) * (BM // hb), last), 0))
    raise ValueError(kind)


def _rowcall(name, body, nblk, nlb, ins, outs, scratch=()):
    return _pcall(
        body,
        name=name,
        grid=(nblk,),
        in_specs=[_spec(a.shape, k, nlb, nblk) for a, k in ins],
        out_specs=[_spec(s, k, nlb, nblk) for s, _, k in outs],
        out_shape=[jax.ShapeDtypeStruct(s, d) for s, d, _ in outs],
        scratch_shapes=list(scratch),
        compiler_params=pltpu.CompilerParams(dimension_semantics=("arbitrary",), vmem_limit_bytes=VMEM_LIMIT),
    )(*[a for a, _ in ins])


def _wk(w):
    return (w[0], ("fullat", w[1])) if isinstance(w, tuple) else (w, "full")


def _stream_edges(i, nlb):
    is_ctx = i == nlb
    return (i == 0) | is_ctx, (i == nlb - 1) | is_ctx, is_ctx


def _dotf(a, b):
    return jnp.dot(a, b, preferred_element_type=F32)


def _dot_nt(a, b):
    return lax.dot_general(a, b, (((1,), (1,)), ((), ())), preferred_element_type=F32)


def _dot_tn(a, b):
    return lax.dot_general(a, b, (((0,), (0,)), ((), ())), preferred_element_type=F32)


def _sigmoid(x):
    return 0.5 * jnp.tanh(0.5 * x) + 0.5


def _norm_mod(x, nw, sh, sc):
    r = lax.rsqrt(jnp.mean(x * x, axis=-1, keepdims=True) + EPS)
    xhat = x * r
    n = xhat * nw
    return n * (1.0 + sc) + sh, n, xhat, r


def _norm_mod_bwd(dh, n, xhat, r, nw, sc):
    dsh = jnp.sum(dh, axis=0, keepdims=True)
    dsc = jnp.sum(dh * n, axis=0, keepdims=True)
    dn = dh * (1.0 + sc)
    dnw = jnp.sum(dn * xhat, axis=0, keepdims=True)
    dxhat = dn * nw
    dx = r * (dxhat - xhat * jnp.mean(dxhat * xhat, axis=-1, keepdims=True))
    return dx, dsh, dsc, dnw


def _acc_init(i, *refs):
    @pl.when(i == 0)
    def _():
        for r in refs:
            r[...] = jnp.zeros(r.shape, r.dtype)


SH1, SC1, G1, SH2, SC2, G2 = range(6)


def _mrow(mod_ref, k):
    return mod_ref[0, k : k + 1, :]


def _shift_rows(x_ref, xp_ref, xn_ref, cs, first, last):
    cur = x_ref[:, cs].astype(F32)
    rows = lax.broadcasted_iota(jnp.int32, cur.shape, 0)
    pr = jnp.where(first, 0.0, xp_ref[HALO_BF16 - 1 : HALO_BF16, cs].astype(F32))
    nx = jnp.where(last, 0.0, xn_ref[0:1, cs].astype(F32))
    dn = jnp.where(rows == 0, pr, pltpu.roll(cur, 1, 0))
    up = jnp.where(rows == BM - 1, nx, pltpu.roll(cur, BM - 1, 0))
    return dn, cur, up


def ffn_up(x1, mod, nw2, w_up, nblk, nlb):
    r = x1.shape[0]

    def body(x_ref, mod_ref, nw_ref, w_ref, u_ref, h_ref):
        h, _, _, _ = _norm_mod(x_ref[...], nw_ref[...], _mrow(mod_ref, SH2), _mrow(mod_ref, SC2))
        hb = h.astype(BF16)
        h_ref[...] = hb
        u_ref[...] = _dotf(hb, w_ref[...]).astype(BF16)

    return _rowcall(
        "ffn_up", body, nblk, nlb,
        [(x1, "row"), (mod, "stream"), (nw2, "full"), _wk(w_up)],
        [((r, 2 * D_FF), BF16, "row"), ((r, D), BF16, "row")],
    )


def _conv_gate_chunk(u_ref, up_ref, un_ref, cw_ref, j, first, last):
    res = []
    for half in range(2):
        c0 = half * D_FF + j * FF_CHUNK
        cs = slice(c0, c0 + FF_CHUNK)
        dn, cur, up = _shift_rows(u_ref, up_ref, un_ref, cs, first, last)
        val = dn * cw_ref[0:1, cs] + cur * cw_ref[1:2, cs] + up * cw_ref[2:3, cs] + cw_ref[3:4, cs]
        res.append((val, dn, cur, up, cs))
    return res


def ffn_down(u, cw, w_down, x1, mod, nblk, nlb):
    r = x1.shape[0]

    def body(u_ref, up_ref, un_ref, cw_ref, w_ref, x_ref, mod_ref, x2_ref, f_ref, uc_ref):
        first, last, _ = _stream_edges(pl.program_id(0), nlb)
        f = jnp.zeros((BM, D), F32)
        for j in range(D_FF // FF_CHUNK):
            (a, _, _, _, acs), (v, _, _, _, vcs) = _conv_gate_chunk(u_ref, up_ref, un_ref, cw_ref, j, first, last)
            uc_ref[:, acs] = a.astype(BF16)
            uc_ref[:, vcs] = v.astype(BF16)
            f = f + _dotf((a * _sigmoid(a) * v).astype(BF16), w_ref[acs, :])
        f_ref[...] = f.astype(BF16)
        x2_ref[...] = x_ref[...] + _mrow(mod_ref, G2) * f

    return _rowcall(
        "ffn_down", body, nblk, nlb,
        [(u, "row"), (u, "prev16"), (u, "next16"), _wk(cw), _wk(w_down), (x1, "row"), (mod, "stream")],
        [((r, D), F32, "row"), ((r, D), BF16, "row"), ((r, 2 * D_FF), BF16, "row")],
    )


def ffn_bwd1(dx2, f, uc, w_down, mod, dep, nblk, nlb):
    r = dx2.shape[0]

    def body(dx_ref, f_ref, uc_ref, w_ref, mod_ref, dep_ref, duc_ref, dw_ref, dcb_ref, dmod_ref, dw_acc):
        i = pl.program_id(0)
        _acc_init(i, dcb_ref, dmod_ref, dw_acc)
        dx = dx_ref[...]
        df = (_mrow(mod_ref, G2) * dx).astype(BF16)
        dmod_ref[i // nlb, G2 : G2 + 1, :] += jnp.sum(dx * f_ref[...].astype(F32), axis=0, keepdims=True)
        for j in range(D_FF // FF_CHUNK):
            acs = slice(j * FF_CHUNK, (j + 1) * FF_CHUNK)
            vcs = slice(D_FF + j * FF_CHUNK, D_FF + (j + 1) * FF_CHUNK)
            a, v = uc_ref[:, acs].astype(F32), uc_ref[:, vcs].astype(F32)
            sa = _sigmoid(a)
            asa = a * sa
            dw_acc[acs, :] += _dot_tn((asa * v).astype(BF16), df)
            dg = _dot_nt(df, w_ref[acs, :])
            for dval, cs in ((dg * v * (sa * (1.0 + a * (1.0 - sa))), acs), (dg * asa, vcs)):
                duc_ref[:, cs] = dval.astype(BF16)
                dcb_ref[3:4, cs] += jnp.sum(dval, axis=0, keepdims=True)

        @pl.when(i == nblk - 1)
        def _():
            dw_ref[...] = dw_acc[...].astype(BF16)

    return _rowcall(
        "ffn_bwd1", body, nblk, nlb,
        [(dx2, "row"), (f, "row"), (uc, "row"), _wk(w_down), (mod, "stream"), (dep, "any")],
        [((r, 2 * D_FF), BF16, "row"), ((D_FF, D), BF16, "acc"), ((8, 2 * D_FF), F32, "acc"), ((2, 8, D), F32, "acc")],
        scratch=[pltpu.VMEM((D_FF, D), F32)],
    )


def ffn_bwd3(duc, u, cw, w_up, x1, dx2, mod, nw2, nblk, nlb):
    r = dx2.shape[0]

    def body(d_ref, dp_ref, dn_ref, u_ref, cw_ref, w_ref, x_ref, dx_ref, mod_ref, nw_ref, du_ref, dx1_ref, dcw_ref, dnw_ref, dmod_ref):
        i = pl.program_id(0)
        first, last, _ = _stream_edges(i, nlb)
        _acc_init(i, dcw_ref, dnw_ref, dmod_ref)
        dh = jnp.zeros((BM, D), F32)
        for j in range(2 * D_FF // FF_CHUNK):
            cs = slice(j * FF_CHUNK, (j + 1) * FF_CHUNK)
            dn, cur, up = _shift_rows(d_ref, dp_ref, dn_ref, cs, first, last)
            du = (up * cw_ref[0:1, cs] + cur * cw_ref[1:2, cs] + dn * cw_ref[2:3, cs]).astype(BF16)
            du_ref[:, cs] = du
            dh = dh + _dot_nt(du, w_ref[:, cs])
            uu = u_ref[:, cs].astype(F32)
            dcw_ref[0:1, cs] += jnp.sum(up * uu, axis=0, keepdims=True)
            dcw_ref[1:2, cs] += jnp.sum(cur * uu, axis=0, keepdims=True)
            dcw_ref[2:3, cs] += jnp.sum(dn * uu, axis=0, keepdims=True)
        sc = _mrow(mod_ref, SC2)
        nw = nw_ref[...]
        _, n, xhat, rr = _norm_mod(x_ref[...], nw, _mrow(mod_ref, SH2), sc)
        dxn, dsh, dsc, dnw = _norm_mod_bwd(dh, n, xhat, rr, nw, sc)
        dx1_ref[...] = dx_ref[...] + dxn
        s = i // nlb
        dmod_ref[s, SH2 : SH2 + 1, :] += dsh
        dmod_ref[s, SC2 : SC2 + 1, :] += dsc
        dnw_ref[0:1, :] += dnw

    return _rowcall(
        "ffn_bwd3", body, nblk, nlb,
        [(duc, "row"), (duc, "prev16"), (duc, "next16"), (u, "row"), _wk(cw), _wk(w_up), (x1, "row"), (dx2, "row"),
         (mod, "stream"), (nw2, "full")],
        [((r, 2 * D_FF), BF16, "row"), ((r, D), F32, "row"), ((8, 2 * D_FF), F32, "acc"), ((8, D), F32, "acc"),
         ((2, 8, D), F32, "acc")],
    )


def matmul_tn(a, b, nblk, tn=None):
    k, n = a.shape[1], b.shape[1]
    rows = nblk * BM
    tr = 768 if rows % 768 == 0 else (1024 if rows % 1024 == 0 else BM)
    if tn is None:
        tn = n
        while k * tn * 4 > 6 * 1024 * 1024 and tn % 256 == 0:
            tn //= 2
    steps = rows // tr

    def body(a_ref, b_ref, o_ref, acc):
        t = pl.program_id(1)

        @pl.when(t == 0)
        def _():
            acc[...] = jnp.zeros(acc.shape, acc.dtype)

        acc[...] += _dot_tn(a_ref[...], b_ref[...])

        @pl.when(t == steps - 1)
        def _():
            o_ref[...] = acc[...].astype(o_ref.dtype)

    return _pcall(
        body,
        name="matmul_tn",
        grid=(n // tn, steps),
        in_specs=[pl.BlockSpec((tr, k), lambda j, t: (t, 0)), pl.BlockSpec((tr, tn), lambda j, t: (t, j))],
        out_specs=pl.BlockSpec((k, tn), lambda j, t: (0, j)),
        out_shape=jax.ShapeDtypeStruct((k, n), BF16),
        scratch_shapes=[pltpu.VMEM((k, tn), F32)],
        compiler_params=pltpu.CompilerParams(dimension_semantics=("parallel", "arbitrary"), vmem_limit_bytes=VMEM_LIMIT),
    )(a, b)


EXT = BM + 2 * HALO_F32


def _pool_positions(i, nlb, nrows, row0):
    is_ctx = i == nlb
    t = (i - jnp.where(is_ctx, nlb, 0)) * BM + row0 + lax.broadcasted_iota(jnp.int32, (nrows, 1), 0)
    return t, jnp.where(is_ctx, BM, nlb * BM)


def _pool_cnt(t, win, slen):
    return (jnp.minimum(t + win // 2, slen) - jnp.maximum(t - win // 2, 0)).astype(F32)


def _pool_fill_ext(ext, i, nlb, x_ref, xp_ref, xn_ref, mod_ref, nw_ref):
    first, last, _ = _stream_edges(i, nlb)
    sh, sc, nw = _mrow(mod_ref, SH1), _mrow(mod_ref, SC1), nw_ref[...]
    hcur, n, xhat, r = _norm_mod(x_ref[...], nw, sh, sc)
    ext[0:HALO_F32, :] = jnp.where(first, 0.0, _norm_mod(xp_ref[...], nw, sh, sc)[0])
    ext[HALO_F32 : HALO_F32 + BM, :] = hcur
    ext[HALO_F32 + BM :, :] = jnp.where(last, 0.0, _norm_mod(xn_ref[...], nw, sh, sc)[0])
    return n, xhat, r


def _window_sum(ref, cols, offs):
    acc = None
    for o in offs:
        v = ref[HALO_F32 + o : HALO_F32 + o + BM, cols]
        acc = v if acc is None else acc + v
    return acc


def _pool_diff(ext, g, win, t, slen):
    cols = slice(g * POOL_GROUP, (g + 1) * POOL_GROUP)
    ssum = _window_sum(ext, cols, range(-(win // 2), win // 2))
    return ssum / _pool_cnt(t, win, slen) - ext[HALO_F32 : HALO_F32 + BM, cols]


def pool_fwd(x, mod, nw1, pw, pbs, nblk, nlb):
    r = x.shape[0]

    def body(x_ref, xp_ref, xn_ref, mod_ref, nw_ref, pw_ref, pbs_ref, x1_ref, ypre_ref, ext):
        i = pl.program_id(0)
        _pool_fill_ext(ext, i, nlb, x_ref, xp_ref, xn_ref, mod_ref, nw_ref)
        t, slen = _pool_positions(i, nlb, BM, 0)
        for g, win in enumerate(POOL_WINDOWS):
            cols = slice(g * POOL_GROUP, (g + 1) * POOL_GROUP)
            diff = _pool_diff(ext, g, win, t, slen)
            ypre = _dotf(diff.astype(BF16), pw_ref[g]) + pbs_ref[0:1, cols]
            ypre_ref[:, cols] = ypre
            x1_ref[:, cols] = x_ref[:, cols] + mod_ref[0, G1 : G1 + 1, cols] * (ypre * pbs_ref[1:2, cols])

    return _rowcall(
        "pool_fwd", body, nblk, nlb,
        [(x, "row"), (x, "prev8"), (x, "next8"), (mod, "stream"), (nw1, "full"), _wk(pw), _wk(pbs)],
        [((r, D), F32, "row"), ((r, D), F32, "row")],
        scratch=[pltpu.VMEM((EXT, D), F32)],
    )


def pool_bwd(dx1, x, ypre, mod, nw1, pw, pbs, dep, nblk, nlb, lat_dx):
    r = x.shape[0]

    def body(d_ref, dp_ref, dn_ref, x_ref, xp_ref, xn_ref, ypre_ref, mod_ref, nw_ref, pw_ref, pbs_ref, dep_ref,
             dx_ref, dpw_ref, dpbs_ref, dnw_ref, dmod_ref, ext, dext, eext, dh):
        i = pl.program_id(0)
        first, last, _ = _stream_edges(i, nlb)
        _acc_init(i, dpw_ref, dpbs_ref, dnw_ref, dmod_ref)
        n, xhat, rr = _pool_fill_ext(ext, i, nlb, x_ref, xp_ref, xn_ref, mod_ref, nw_ref)
        g1, scale = _mrow(mod_ref, G1), pbs_ref[1:2, :]
        dcur = d_ref[...]
        ypre = ypre_ref[...]
        s = i // nlb
        dmod_ref[s, G1 : G1 + 1, :] += jnp.sum(dcur * (ypre * scale), axis=0, keepdims=True)
        dy = g1 * dcur
        dpbs_ref[1:2, :] += jnp.sum(dy * ypre, axis=0, keepdims=True)
        dpbs_ref[0:1, :] += jnp.sum(dy * scale, axis=0, keepdims=True)
        gs = g1 * scale
        dext[0:HALO_F32, :] = jnp.where(first, 0.0, gs * dp_ref[...])
        dext[HALO_F32 : HALO_F32 + BM, :] = dy * scale
        dext[HALO_F32 + BM :, :] = jnp.where(last, 0.0, gs * dn_ref[...])
        t, slen = _pool_positions(i, nlb, BM, 0)
        text, _ = _pool_positions(i, nlb, EXT, -HALO_F32)
        for g, win in enumerate(POOL_WINDOWS):
            cols = slice(g * POOL_GROUP, (g + 1) * POOL_GROUP)
            diff = _pool_diff(ext, g, win, t, slen)
            dpre = dext[:, cols].astype(BF16)
            ddiff = _dot_nt(dpre, pw_ref[g])
            eext[...] = ddiff / jnp.maximum(_pool_cnt(text, win, slen), 1.0)
            dh[:, cols] = _window_sum(eext, slice(None), range(-(win // 2) + 1, win // 2 + 1)) - ddiff[HALO_F32 : HALO_F32 + BM, :]
            dpw_ref[g] += _dot_tn(diff.astype(BF16), dpre[HALO_F32 : HALO_F32 + BM, :])
        sc, nw = _mrow(mod_ref, SC1), nw_ref[...]
        dxn, dsh, dsc, dnw = _norm_mod_bwd(dh[...], n, xhat, rr, nw, sc)
        if lat_dx:
            @pl.when(i < nlb)
            def _():
                dx_ref[...] = dcur + dxn
        else:
            dx_ref[...] = dcur + dxn
        dmod_ref[s, SH1 : SH1 + 1, :] += dsh
        dmod_ref[s, SC1 : SC1 + 1, :] += dsc
        dnw_ref[0:1, :] += dnw

    return _rowcall(
        "pool_bwd", body, nblk, nlb,
        [(dx1, "row"), (dx1, "prev8"), (dx1, "next8"), (x, "row"), (x, "prev8"), (x, "next8"), (ypre, "row"),
         (mod, "stream"), (nw1, "full"), _wk(pw), _wk(pbs), (dep, "any")],
        [((nlb * BM, D), F32, "row_lat") if lat_dx else ((r, D), F32, "row"), ((4, POOL_GROUP, POOL_GROUP), F32, "acc"), ((8, D), F32, "acc"), ((8, D), F32, "acc"),
         ((2, 8, D), F32, "acc")],
        scratch=[pltpu.VMEM((EXT, D), F32), pltpu.VMEM((EXT, D), F32), pltpu.VMEM((EXT, POOL_GROUP), F32), pltpu.VMEM((BM, D), F32)],
    )


def rope_tables(s, l):
    rows = s // GRID_W
    row = jnp.broadcast_to(jnp.arange(rows)[:, None], (rows, GRID_W)).reshape(-1).astype(F32)
    col = jnp.broadcast_to(jnp.arange(GRID_W)[None, :], (rows, GRID_W)).reshape(-1).astype(F32)
    axis_dim = HD // 2
    inv = ROPE_THETA ** (-jnp.arange(0, axis_dim, 2, dtype=F32) / axis_dim)
    ar, ac = row[:, None] * inv, col[:, None] * inv
    cos = jnp.concatenate([jnp.cos(ar), jnp.cos(ar), jnp.cos(ac), jnp.cos(ac)], axis=-1)
    sin = jnp.concatenate([-jnp.sin(ar), jnp.sin(ar), -jnp.sin(ac), jnp.sin(ac)], axis=-1)
    return (jnp.concatenate([cos, jnp.ones((l, HD), F32)], axis=0), jnp.concatenate([sin, jnp.zeros((l, HD), F32)], axis=0))


def _partner(x):
    q = HD // 4
    lane = lax.broadcasted_iota(jnp.int32, x.shape, 1)
    return jnp.where((lane // q) % 2 == 0, pltpu.roll(x, HD - q, 1), pltpu.roll(x, q, 1))


def _head_norm(raw, gain):
    r = lax.rsqrt(jnp.mean(raw * raw, axis=-1, keepdims=True) + EPS)
    return raw * r, r


ATTN_SCALE = HD ** -0.5


def qkv_fwd(x, mod, nw1, w_qkv, gains, cos_t, sin_t, nblk, nlb):
    r = x.shape[0]

    def body(x_ref, mod_ref, nw_ref, w_ref, g_ref, c_ref, s_ref, raw_ref, q_ref, k_ref, v_ref, h_ref):
        h = _norm_mod(x_ref[...], nw_ref[...], _mrow(mod_ref, SH1), _mrow(mod_ref, SC1))[0].astype(BF16)
        h_ref[...] = h
        raw_ref[...] = _dotf(h, w_ref[...])
        cos, sin = c_ref[...], s_ref[...]
        for j in range(ATTN_HEADS + ATTN_KV_HEADS):
            isq = j < ATTN_HEADS
            xn = _head_norm(raw_ref[:, j * HD : (j + 1) * HD], None)[0] * (g_ref[0:1, :] if isq else g_ref[1:2, :])
            rot = xn * cos + _partner(xn) * sin
            if isq:
                rot = rot * ATTN_SCALE
            rot = rot.astype(BF16)
            if isq:
                q_ref[:, j * HD : (j + 1) * HD] = rot
            else:
                k_ref[:, (j - ATTN_HEADS) * HD : (j - ATTN_HEADS + 1) * HD] = rot
        v_ref[...] = raw_ref[:, NQ + NKV :].astype(BF16)

    return _rowcall(
        "qkv_fwd", body, nblk, nlb,
        [(x, "row"), (mod, "stream"), (nw1, "full"), (w_qkv, "full"), (gains, "full"), (cos_t, "row"), (sin_t, "row")],
        [((r, NQ + 2 * NKV), F32, "row"), ((r, NQ), BF16, "row"), ((r, NKV), BF16, "row"), ((r, NKV), BF16, "row"),
         ((r, D), BF16, "row")],
    )


def attn_fwd(q, k, v, nblk, nlb):
    r = q.shape[0]

    def body(q_ref, k_ref, v_ref, o_ref, lse_ref):
        def heads(keys):
            for kvh in range(ATTN_KV_HEADS):
                kh = k_ref[keys, kvh * HD : (kvh + 1) * HD]
                vh = v_ref[keys, kvh * HD : (kvh + 1) * HD]
                for g in range(ATTN_GROUP):
                    cs = slice((kvh * ATTN_GROUP + g) * HD, (kvh * ATTN_GROUP + g + 1) * HD)
                    s = _dot_nt(q_ref[:, cs], kh)
                    m = jnp.max(s, axis=-1, keepdims=True)
                    p = jnp.exp(s - m)
                    l = jnp.sum(p, axis=-1, keepdims=True)
                    o_ref[:, cs] = (_dotf(p.astype(BF16), vh) / l).astype(BF16)
                    j = kvh * ATTN_GROUP + g
                    lse_ref[:, j : j + 1] = m + jnp.log(l)

        i = pl.program_id(0)
        pl.when(i < nlb)(lambda: heads(slice(0, r)))
        pl.when(i == nlb)(lambda: heads(slice(nlb * BM, r)))

    return _rowcall(
        "attn_fwd", body, nblk, nlb,
        [(q, "row"), (k, "full"), (v, "full")],
        [((r, NQ), BF16, "row"), ((r, ATTN_HEADS), F32, "row")],
    )


def attn_out_fwd(o, w_o, x, mod, nblk, nlb):
    r = x.shape[0]

    def body(o_ref, w_ref, x_ref, mod_ref, x1_ref, y_ref):
        y = _dotf(o_ref[...], w_ref[...])
        y_ref[...] = y
        x1_ref[...] = x_ref[...] + _mrow(mod_ref, G1) * y

    return _rowcall(
        "attn_out_fwd", body, nblk, nlb,
        [(o, "row"), (w_o, "full"), (x, "row"), (mod, "stream")],
        [((r, D), F32, "row"), ((r, D), F32, "row")],
    )


def mix_out_bwd(name, dx1, y, w_out, mod, dep, nblk, nlb):
    r = dx1.shape[0]
    kin = w_out.shape[0]

    def body(d_ref, y_ref, w_ref, mod_ref, dep_ref, dy_ref, do_ref, dmod_ref):
        i = pl.program_id(0)
        _acc_init(i, dmod_ref)
        d = d_ref[...]
        dmod_ref[i // nlb, G1 : G1 + 1, :] += jnp.sum(d * y_ref[...], axis=0, keepdims=True)
        dy = (_mrow(mod_ref, G1) * d).astype(BF16)
        dy_ref[...] = dy
        do_ref[...] = _dot_nt(dy, w_ref[...]).astype(do_ref.dtype)

    return _rowcall(
        name, body, nblk, nlb,
        [(dx1, "row"), (y, "row"), (w_out, "full"), (mod, "stream"), (dep, "any")],
        [((r, D), BF16, "row"), ((r, kin), BF16, "row"), ((2, 8, D), F32, "acc")],
    )


ATTN_KCHUNK = 11 * BM


def attn_bwd(q, k, v, o, do, lse, nblk, nlb):
    r = q.shape[0]
    kc = ATTN_KCHUNK if r % ATTN_KCHUNK == 0 else BM
    nkc = r // kc

    def body(q_ref, k_ref, v_ref, o_ref, do_ref, lse_ref, dq_ref, dk_ref, dv_ref):
        i = pl.program_id(0)
        _acc_init(i, dk_ref, dv_ref)

        def heads(chunks):
            for kvh in range(ATTN_KV_HEADS):
                ks = slice(kvh * HD, (kvh + 1) * HD)
                for g in range(ATTN_GROUP):
                    j = kvh * ATTN_GROUP + g
                    cs = slice(j * HD, (j + 1) * HD)
                    qh, doh = q_ref[:, cs], do_ref[:, cs]
                    delta = jnp.sum(doh.astype(F32) * o_ref[:, cs].astype(F32), axis=-1, keepdims=True)
                    lse = lse_ref[:, j : j + 1]
                    dq = jnp.zeros((BM, HD), F32)
                    for rs in chunks:
                        kh, vh = k_ref[rs, ks], v_ref[rs, ks]
                        p = jnp.exp(_dot_nt(qh, kh) - lse)
                        ds = (p * (_dot_nt(doh, vh) - delta)).astype(BF16)
                        dq = dq + _dotf(ds, kh)
                        dk_ref[rs, ks] += _dot_tn(ds, qh)
                        dv_ref[rs, ks] += _dot_tn(p.astype(BF16), doh)
                    dq_ref[:, cs] = dq * ATTN_SCALE

        pl.when(i < nlb)(lambda: heads([slice(c * kc, (c + 1) * kc) for c in range(nkc)]))
        pl.when(i == nlb)(lambda: heads([slice(nlb * BM, r)]))

    return _rowcall(
        "attn_bwd", body, nblk, nlb,
        [(q, "row"), (k, "full"), (v, "full"), (o, "row"), (do, "row"), (lse, "row")],
        [((r, NQ), F32, "row"), ((r, NKV), F32, "acc"), ((r, NKV), F32, "acc")],
    )


def qkv_bwd(dq, dk, dv, raw, gains, cos_t, sin_t, w_qkv, x, dx1, mod, nw1, nblk, nlb, ctx_dx_zero):
    r = x.shape[0]

    def body(dq_ref, dk_ref, dv_ref, raw_ref, g_ref, c_ref, s_ref, w_ref, x_ref, dx1_ref, mod_ref, nw_ref,
             dx_ref, draw_ref, dg_ref, dnw_ref, dmod_ref):
        i = pl.program_id(0)
        _acc_init(i, dg_ref, dnw_ref, dmod_ref)
        cos, sin = c_ref[...], s_ref[...]
        for j in range(ATTN_HEADS + ATTN_KV_HEADS):
            isq = j < ATTN_HEADS
            cs = slice(j * HD, (j + 1) * HD)
            dr = dq_ref[:, cs] if isq else dk_ref[:, (j - ATTN_HEADS) * HD : (j - ATTN_HEADS + 1) * HD]
            dxn = dr * cos + _partner(dr * sin)
            xhat, rr = _head_norm(raw_ref[:, cs], None)
            gi = 0 if isq else 1
            dg_ref[gi : gi + 1, :] += jnp.sum(dxn * xhat, axis=0, keepdims=True)
            dxhat = dxn * g_ref[gi : gi + 1, :]
            draw_ref[:, cs] = (rr * (dxhat - xhat * jnp.mean(dxhat * xhat, axis=-1, keepdims=True))).astype(BF16)
        draw_ref[:, NQ + NKV :] = dv_ref[...].astype(BF16)
        dh = _dot_nt(draw_ref[...], w_ref[...])
        sc, nw = _mrow(mod_ref, SC1), nw_ref[...]
        _, n, xhat, rr = _norm_mod(x_ref[...], nw, _mrow(mod_ref, SH1), sc)
        dxn, dsh, dsc, dnw = _norm_mod_bwd(dh, n, xhat, rr, nw, sc)
        dres = dx1_ref[...]
        if ctx_dx_zero:
            dres = jnp.where(i == nlb, 0.0, dres)
        dx_ref[...] = dres + dxn
        s = i // nlb
        dmod_ref[s, SH1 : SH1 + 1, :] += dsh
        dmod_ref[s, SC1 : SC1 + 1, :] += dsc
        dnw_ref[0:1, :] += dnw

    return _rowcall(
        "qkv_bwd", body, nblk, nlb,
        [(dq, "row"), (dk, "row"), (dv, "row"), (raw, "row"), (gains, "full"), (cos_t, "row"), (sin_t, "row"),
         (w_qkv, "full"), (x, "row"), (dx1, "row"), (mod, "stream"), (nw1, "full")],
        [((r, D), F32, "row"), ((r, NQ + 2 * NKV), BF16, "row"), ((8, HD), F32, "acc"), ((8, D), F32, "acc"),
         ((2, 8, D), F32, "acc")],
    )


RET_KSCALE = RET_DK ** -0.5
RET_HP = RET_HEADS


def ret_in_fwd(x, mod, nw1, w_in, nblk, nlb):
    r = x.shape[0]

    def body(x_ref, mod_ref, nw_ref, w_ref, q_ref, k_ref, v_ref, g_ref, h_ref):
        h = _norm_mod(x_ref[...], nw_ref[...], _mrow(mod_ref, SH1), _mrow(mod_ref, SC1))[0].astype(BF16)
        h_ref[...] = h
        q_ref[...] = _dotf(h, w_ref[:, 0:RNQ]).astype(BF16)
        k_ref[...] = (_dotf(h, w_ref[:, RNQ : 2 * RNQ]) * RET_KSCALE).astype(BF16)
        v_ref[...] = _dotf(h, w_ref[:, 2 * RNQ : 2 * RNQ + RNV]).astype(BF16)
        g_ref[...] = _dotf(h, w_ref[:, 2 * RNQ + RNV :]).astype(BF16)

    return _rowcall(
        "ret_in_fwd", body, nblk, nlb,
        [(x, "row"), (mod, "stream"), (nw1, "full"), (w_in, "full")],
        [((r, RNQ), BF16, "row"), ((r, RNQ), BF16, "row"), ((r, RNV), BF16, "row"), ((r, RNV), BF16, "row"), ((r, D), BF16, "row")],
    )


def _log_sigmoid(x):
    return jnp.minimum(x, 0.0) - jnp.log(1.0 + jnp.exp(-jnp.abs(x)))


def _ret_decays(lg, reverse):
    c = BM
    i = lax.broadcasted_iota(jnp.int32, (c, c), 0)
    j = lax.broadcasted_iota(jnp.int32, (c, c), 1)
    diff = (j - i) if reverse else (i - j)
    ediff = jnp.maximum(diff, 0).astype(F32)
    dm = jnp.where(diff >= 0, jnp.exp(ediff * lg), 0.0)
    rr = lax.broadcasted_iota(jnp.int32, (c, 1), 0).astype(F32)
    eq = (c - rr) if reverse else (rr + 1.0)
    ek = rr if reverse else (c - 1.0 - rr)
    return dm, ediff, jnp.exp(eq * lg), eq, jnp.exp(ek * lg), ek, jnp.exp(c * lg)


def _ret_chunk_index(nlb):
    return (lambda s: jnp.where(s == 0, nlb, s - 1)), (lambda s: jnp.where(s == 0, nlb, nlb - s))


def ret_scan_fwd(q, k, v, logit_b, nlb):
    r = q.shape[0]
    nb = nlb + 1
    fidx, bidx = _ret_chunk_index(nlb)

    def body(qf, kf, vf, qb, kb, vb, lg_ref, of_ref, ob_ref, rf_ref, rb_ref, stf, stb):
        s = pl.program_id(1)

        @pl.when(s == 0)
        def _():
            stf[...] = jnp.zeros(stf.shape, F32)
            stb[...] = jnp.zeros(stb.shape, F32)

        for d, (q_ref, k_ref, v_ref, o_ref, rs_ref, st) in enumerate(((qf, kf, vf, of_ref, rf_ref, stf), (qb, kb, vb, ob_ref, rb_ref, stb))):
            for hp in range(RET_HP):
                kcs, vcs = slice(hp * RET_DK, (hp + 1) * RET_DK), slice(hp * RET_DV, (hp + 1) * RET_DV)
                lg = _log_sigmoid(lg_ref[d, hp])[0:1, 0:1]
                dm, _, qd, _, kd, _, gc = _ret_decays(lg, d == 1)
                qq, kk, vv, st0 = q_ref[:, kcs], k_ref[:, kcs], v_ref[:, vcs], st[hp]
                rs_ref[hp, 0] = st0
                a = _dot_nt(qq, kk) * dm
                o = _dotf(a.astype(BF16), vv) + _dotf(qq, st0.astype(BF16)) * qd
                o_ref[:, vcs] = jnp.where(s == 0, 0.0, o)
                st[hp] = st0 * gc + _dot_tn((kk.astype(F32) * kd).astype(BF16), vv)

    qspec = lambda f: pl.BlockSpec((BM, RET_HP * RET_DK), lambda h, s: (f(s), h))
    vspec = lambda f: pl.BlockSpec((BM, RET_HP * RET_DV), lambda h, s: (f(s), h))
    sspec = pl.BlockSpec((RET_HP, 1, RET_DK, RET_DV), lambda h, s: (h, s, 0, 0))
    return _pcall(
        body,
        name="ret_scan_fwd",
        grid=(RET_HEADS // RET_HP, nb),
        in_specs=[qspec(fidx), qspec(fidx), vspec(fidx), qspec(bidx), qspec(bidx), vspec(bidx),
                  pl.BlockSpec((2, RET_HP, 8, 128), lambda h, s: (0, h, 0, 0))],
        out_specs=[vspec(fidx), vspec(bidx), sspec, sspec],
        out_shape=[jax.ShapeDtypeStruct((r, RNV), F32), jax.ShapeDtypeStruct((r, RNV), F32),
                   jax.ShapeDtypeStruct((RET_HEADS, nb, RET_DK, RET_DV), F32), jax.ShapeDtypeStruct((RET_HEADS, nb, RET_DK, RET_DV), F32)],
        scratch_shapes=[pltpu.VMEM((RET_HP, RET_DK, RET_DV), F32), pltpu.VMEM((RET_HP, RET_DK, RET_DV), F32)],
        compiler_params=pltpu.CompilerParams(dimension_semantics=("parallel", "arbitrary"), vmem_limit_bytes=VMEM_LIMIT),
    )(q, k, v, q, k, v, logit_b)


def _group_norm(o):
    mu = jnp.mean(o, axis=-1, keepdims=True)
    oc = o - mu
    rstd = lax.rsqrt(jnp.mean(oc * oc, axis=-1, keepdims=True) + EPS)
    return oc * rstd, rstd


def ret_out_fwd(o_f, o_b, g, gnw, w_out, x, mod, nblk, nlb):
    r = x.shape[0]

    def body(of_ref, ob_ref, g_ref, gn_ref, w_ref, x_ref, mod_ref, x1_ref, y_ref, z_ref):
        for hh in range(RET_HEADS):
            cs = slice(hh * RET_DV, (hh + 1) * RET_DV)
            yhat, _ = _group_norm(of_ref[:, cs] + ob_ref[:, cs])
            gg = g_ref[:, cs].astype(F32)
            z_ref[:, cs] = (gg * _sigmoid(gg) * (yhat * gn_ref[0:1, cs])).astype(BF16)
        y = _dotf(z_ref[...], w_ref[...])
        y_ref[...] = y
        x1_ref[...] = x_ref[...] + _mrow(mod_ref, G1) * y

    return _rowcall(
        "ret_out_fwd", body, nblk, nlb,
        [(o_f, "row"), (o_b, "row"), (g, "row"), (gnw, "full"), (w_out, "full"), (x, "row"), (mod, "stream")],
        [((r, D), F32, "row"), ((r, D), F32, "row"), ((r, RNV), BF16, "row")],
    )


def ret_out_bwd(dx1, y, o_f, o_b, g, gnw, w_out, mod, dep, nblk, nlb):
    r = dx1.shape[0]

    def body(d_ref, y_ref, of_ref, ob_ref, g_ref, gn_ref, w_ref, mod_ref, dep_ref, dy_ref, do_ref, dg_ref, dgn_ref, dmod_ref, dz):
        i = pl.program_id(0)
        _acc_init(i, dgn_ref, dmod_ref)
        d = d_ref[...]
        dmod_ref[i // nlb, G1 : G1 + 1, :] += jnp.sum(d * y_ref[...], axis=0, keepdims=True)
        dy = (_mrow(mod_ref, G1) * d).astype(BF16)
        dy_ref[...] = dy
        dz[...] = _dot_nt(dy, w_ref[...])
        for hh in range(RET_HEADS):
            cs = slice(hh * RET_DV, (hh + 1) * RET_DV)
            yhat, rstd = _group_norm(of_ref[:, cs] + ob_ref[:, cs])
            gg = g_ref[:, cs].astype(F32)
            sg = _sigmoid(gg)
            gn = gn_ref[0:1, cs]
            dzz = dz[:, cs]
            dg_ref[:, cs] = (dzz * (yhat * gn) * (sg * (1.0 + gg * (1.0 - sg)))).astype(BF16)
            dyn = dzz * (gg * sg)
            dgn_ref[0:1, cs] += jnp.sum(dyn * yhat, axis=0, keepdims=True)
            dyh = dyn * gn
            do = rstd * (dyh - jnp.mean(dyh, axis=-1, keepdims=True) - yhat * jnp.mean(dyh * yhat, axis=-1, keepdims=True))
            do_ref[:, cs] = do.astype(BF16)

    return _rowcall(
        "ret_out_bwd", body, nblk, nlb,
        [(dx1, "row"), (y, "row"), (o_f, "row"), (o_b, "row"), (g, "row"), (gnw, "full"), (w_out, "full"), (mod, "stream"), (dep, "any")],
        [((r, D), BF16, "row"), ((r, RNV), BF16, "row"), ((r, RNV), BF16, "row"), ((8, RNV), F32, "acc"), ((2, 8, D), F32, "acc")],
        scratch=[pltpu.VMEM((BM, RNV), F32)],
    )


def ret_scan_bwd(q, k, v, do, st_f, st_b, logit_b, nlb):
    r = q.shape[0]
    nb = nlb + 1
    fidx, bidx = _ret_chunk_index(nlb)
    step = lambda t: nb - 1 - t

    def body(qf, kf, vf, dof, rf, qb, kb, vb, dob, rb, lg_ref,
             dqf, dkf, dvf, dqb, dkb, dvb, dlg_ref, drf, drb):
        t = pl.program_id(1)
        s = step(t)

        @pl.when(t == 0)
        def _():
            drf[...] = jnp.zeros(drf.shape, F32)
            drb[...] = jnp.zeros(drb.shape, F32)
            dlg_ref[...] = jnp.zeros(dlg_ref.shape, F32)

        dirs = ((qf, kf, vf, dof, rf, dqf, dkf, dvf, drf), (qb, kb, vb, dob, rb, dqb, dkb, dvb, drb))
        for d, (q_ref, k_ref, v_ref, do_ref, rs_ref, dq_ref, dk_ref, dv_ref, dr) in enumerate(dirs):
            for hp in range(RET_HP):
                kcs, vcs = slice(hp * RET_DK, (hp + 1) * RET_DK), slice(hp * RET_DV, (hp + 1) * RET_DV)
                lg = _log_sigmoid(lg_ref[d, hp])[0:1, 0:1]
                dm, ediff, qd, eq, kd, ek, gc = _ret_decays(lg, d == 1)
                qq, kk, vv = q_ref[:, kcs], k_ref[:, kcs], v_ref[:, vcs]
                dob16 = jnp.where(s == 0, jnp.zeros((), BF16), do_ref[:, vcs])
                do32 = dob16.astype(F32)
                st0 = rs_ref[hp, 0]
                st16 = st0.astype(BF16)
                dr0 = dr[hp]
                dr16 = dr0.astype(BF16)
                a = _dot_nt(qq, kk) * dm
                daf = _dot_nt(dob16, vv)
                ds = (daf * dm).astype(BF16)
                qr = _dotf(qq, st16)
                k32 = kk.astype(F32)
                kdec = (k32 * kd).astype(BF16)
                dv_ref[:, vcs] = (_dot_tn(a.astype(BF16), dob16) + _dotf(kdec, dr16)).astype(BF16)
                dq_ref[:, kcs] = (_dotf(ds, kk) + _dot_nt(dob16, st16) * qd).astype(BF16)
                vdr = _dot_nt(vv, dr16)
                dk_ref[:, kcs] = (_dot_tn(ds, qq) + vdr * kd).astype(BF16)
                tot = (jnp.sum(daf * a * ediff)
                       + jnp.sum(eq * qd * jnp.sum(do32 * qr, axis=-1, keepdims=True))
                       + jnp.sum(ek * kd * jnp.sum(k32 * vdr, axis=-1, keepdims=True))
                       + jnp.sum(BM * gc * jnp.sum(dr0 * st0, axis=-1, keepdims=True)))
                dlg_ref[d, hp] += tot
                dr[hp] = gc * dr0 + _dot_tn(qq, (do32 * qd).astype(BF16))

        @pl.when(t == nb - 1)
        def _():
            dlg_ref[...] = dlg_ref[...] * _sigmoid(-lg_ref[...])

    qspec = lambda f: pl.BlockSpec((BM, RET_HP * RET_DK), lambda h, t: (f(step(t)), h))
    vspec = lambda f: pl.BlockSpec((BM, RET_HP * RET_DV), lambda h, t: (f(step(t)), h))
    sspec = pl.BlockSpec((RET_HP, 1, RET_DK, RET_DV), lambda h, t: (h, step(t), 0, 0))
    lspec = pl.BlockSpec((2, RET_HP, 8, 128), lambda h, t: (0, h, 0, 0))
    sq, sv = jax.ShapeDtypeStruct((r, RNQ), BF16), jax.ShapeDtypeStruct((r, RNV), BF16)
    return _pcall(
        body,
        name="ret_scan_bwd",
        grid=(RET_HEADS // RET_HP, nb),
        in_specs=[qspec(fidx), qspec(fidx), vspec(fidx), vspec(fidx), sspec,
                  qspec(bidx), qspec(bidx), vspec(bidx), vspec(bidx), sspec, lspec],
        out_specs=[qspec(fidx), qspec(fidx), vspec(fidx), qspec(bidx), qspec(bidx), vspec(bidx), lspec],
        out_shape=[sq, sq, sv, sq, sq, sv, jax.ShapeDtypeStruct((2, RET_HEADS, 8, 128), F32)],
        scratch_shapes=[pltpu.VMEM((RET_HP, RET_DK, RET_DV), F32), pltpu.VMEM((RET_HP, RET_DK, RET_DV), F32)],
        compiler_params=pltpu.CompilerParams(dimension_semantics=("parallel", "arbitrary"), vmem_limit_bytes=VMEM_LIMIT),
    )(q, k, v, do, st_f, q, k, v, do, st_b, logit_b)


def ret_in_bwd(dqf, dkf, dvf, dqb, dkb, dvb, dgate, w_in, x, dx1, mod, nw1, nblk, nlb):
    r = x.shape[0]
    nin = 2 * RNQ + 2 * RNV

    def body(dqf_ref, dkf_ref, dvf_ref, dqb_ref, dkb_ref, dvb_ref, dg_ref, w_ref, x_ref, dx1_ref, mod_ref, nw_ref,
             dx_ref, din_ref, dnw_ref, dmod_ref):
        i = pl.program_id(0)
        _acc_init(i, dnw_ref, dmod_ref)
        is_ctx = i == nlb
        din_ref[:, 0:RNQ] = (dqf_ref[...].astype(F32) + dqb_ref[...].astype(F32)).astype(BF16)
        din_ref[:, RNQ : 2 * RNQ] = ((dkf_ref[...].astype(F32) + dkb_ref[...].astype(F32)) * RET_KSCALE).astype(BF16)
        din_ref[:, 2 * RNQ : 2 * RNQ + RNV] = (dvf_ref[...].astype(F32) + dvb_ref[...].astype(F32)).astype(BF16)
        din_ref[:, 2 * RNQ + RNV :] = jnp.where(is_ctx, jnp.zeros((), BF16), dg_ref[...])
        dh = _dot_nt(din_ref[...], w_ref[...])
        sc, nw = _mrow(mod_ref, SC1), nw_ref[...]
        _, n, xhat, rr = _norm_mod(x_ref[...], nw, _mrow(mod_ref, SH1), sc)
        dxn, dsh, dsc, dnw = _norm_mod_bwd(dh, n, xhat, rr, nw, sc)
        dx_ref[...] = jnp.where(is_ctx, 0.0, dx1_ref[...]) + dxn
        s = i // nlb
        dmod_ref[s, SH1 : SH1 + 1, :] += dsh
        dmod_ref[s, SC1 : SC1 + 1, :] += dsc
        dnw_ref[0:1, :] += dnw

    return _rowcall(
        "ret_in_bwd", body, nblk, nlb,
        [(dqf, "row"), (dkf, "row"), (dvf, "row"), (dqb, "row"), (dkb, "row"), (dvb, "row"), (dgate, "row"),
         (w_in, "full"), (x, "row"), (dx1, "row"), (mod, "stream"), (nw1, "full")],
        [((r, D), F32, "row"), ((r, nin), BF16, "row"), ((8, D), F32, "acc"), ((2, 8, D), F32, "acc")],
    )


def loss_head(xout, target, nlb):
    r = xout.shape[0]

    def body(x_ref, t_ref, dx_ref, l_ref):
        _acc_init(pl.program_id(0), l_ref)
        err = x_ref[...] - t_ref[...]
        dx_ref[...] = err * (1.0 / D)
        l_ref[...] += 0.5 * jnp.sum(jnp.mean(err * err, axis=-1, keepdims=True))

    return _rowcall(
        "loss_head", body, nlb, nlb,
        [(xout, "row"), (target, "row")],
        [((r, D), F32, "row"), ((8, 128), F32, "acc")],
    )


N_MIXERS = 3
POOL, ATTN, RET = range(3)


def _layer_plan(depth):
    plan = []
    for i in range(depth):
        kind = i % N_MIXERS
        ctx_out = any(k % N_MIXERS != POOL for k in range(i + 1, depth))
        plan.append((kind, i // N_MIXERS, ctx_out or kind != POOL, ctx_out))
    return plan


def local_step(xs, target, mods, w, nlb, depth, fetch, emit):
    nb = nlb + 1
    plan = _layer_plan(depth)
    saved = []
    x = xs
    for i, (kind, j, ctx_in, ctx_out) in enumerate(plan):
        nmix = nb if ctx_out else nlb
        mod, nw1, nw2 = mods[i], w["nw"][i, 0:1], w["nw"][i, 1:2]
        lw = fetch(i, MIX, x)
        sv = {"x": x, "lw": lw}
        if kind == POOL:
            x1, sv["ypre"] = pool_fwd(x, mod, nw1, lw["pool_w"], (w["pbs"], j), nmix, nlb)
        elif kind == ATTN:
            assert ctx_out
            sv["raw"], sv["q"], sv["k"], sv["v"], sv["h"] = qkv_fwd(x, mod, nw1, lw["attn_w_qkv"], w["gains"], w["cos"], w["sin"], nb, nlb)
            sv["o"], sv["lse"] = attn_fwd(sv["q"], sv["k"], sv["v"], nb, nlb)
            x1, sv["y"] = attn_out_fwd(sv["o"], lw["attn_w_o"], x, mod, nb, nlb)
        else:
            assert ctx_in and not ctx_out
            sv["q"], sv["k"], sv["v"], sv["g"], sv["h"] = ret_in_fwd(x, mod, nw1, lw["ret_w_in"], nb, nlb)
            sv["o_f"], sv["o_b"], sv["st_f"], sv["st_b"] = ret_scan_fwd(sv["q"], sv["k"], sv["v"], w["logit_b"], nlb)
            x1, sv["y"], sv["z"] = ret_out_fwd(sv["o_f"], sv["o_b"], sv["g"], w["gnw"], lw["ret_w_out"], x, mod, nlb, nlb)
        sv["x1"] = x1
        lw.update(fetch(i, FFN, x1))
        sv["u"], sv["h2"] = ffn_up(x1, mod, nw2, lw["ffn_w_up"], nmix, nlb)
        x, sv["f"], sv["uc"] = ffn_down(sv["u"], (w["cw"], i), lw["ffn_w_down"], x1, mod, nmix, nlb)
        saved.append(sv)

    dx, loss_tile = loss_head(x, target, nlb)
    g = {k: [None] * depth for k in ("dcw", "dnw1", "dnw2", "dmod")}
    dep = loss_tile
    for i in reversed(range(depth)):
        kind, j, ctx_in, ctx_out = plan[i]
        sv = saved[i]
        lw, big = sv["lw"], {}
        nmix = nb if ctx_out else nlb
        mod, nw1, nw2 = mods[i], w["nw"][i, 0:1], w["nw"][i, 1:2]
        duc, big["ffn_w_down"], dcb, dmod = ffn_bwd1(dx, sv["f"], sv["uc"], lw["ffn_w_down"], mod, dep, nmix, nlb)
        du, dx1, dcw, g["dnw2"][i], dm = ffn_bwd3(duc, sv["u"], (w["cw"], i), lw["ffn_w_up"], sv["x1"], dx, mod, nw2, nmix, nlb)
        g["dcw"][i] = dcw + dcb
        dmod = dmod + dm
        big["ffn_w_up"] = matmul_tn(sv["h2"], du, nmix)
        dep, big = emit(i, FFN, big), {}
        if kind == POOL:
            dx, dpw, dpbs, g["dnw1"][i], dm = pool_bwd(dx1, sv["x"], sv["ypre"], mod, nw1, lw["pool_w"], (w["pbs"], j), dep, nmix, nlb, i == 0)
            big["pool_w"] = dpw.astype(BF16)
            g.setdefault("dpbs", {})[j] = dpbs
        elif kind == ATTN:
            dy, do, dm1 = mix_out_bwd("attn_out_bwd", dx1, sv["y"], lw["attn_w_o"], mod, dep, nb, nlb)
            big["attn_w_o"] = matmul_tn(sv["o"], dy, nb)
            dq, dk, dv = attn_bwd(sv["q"], sv["k"], sv["v"], sv["o"], do, sv["lse"], nb, nlb)
            dx, draw, g["dgains"], g["dnw1"][i], dm = qkv_bwd(
                dq, dk, dv, sv["raw"], w["gains"], w["cos"], w["sin"], lw["attn_w_qkv"], sv["x"], dx1, mod, nw1, nb, nlb, False)
            big["attn_w_qkv"] = matmul_tn(sv["h"], draw, nb)
            dm = dm + dm1
        else:
            dy, do, dgate, g["dgnw"], dm1 = ret_out_bwd(dx1, sv["y"], sv["o_f"], sv["o_b"], sv["g"], w["gnw"], lw["ret_w_out"], mod, dep, nlb, nlb)
            big["ret_w_out"] = matmul_tn(sv["z"], dy, nlb)
            dqf, dkf, dvf, dqb, dkb, dvb, g["dlogit"] = ret_scan_bwd(sv["q"], sv["k"], sv["v"], do, sv["st_f"], sv["st_b"], w["logit_b"], nlb)
            dx, din, g["dnw1"][i], dm = ret_in_bwd(dqf, dkf, dvf, dqb, dkb, dvb, dgate, lw["ret_w_in"], sv["x"], dx1, mod, nw1, nb, nlb)
            big["ret_w_in"] = matmul_tn(sv["h"], din, nb)
            dm = dm + dm1
        g["dmod"][i] = dmod + dm
        dep = emit(i, MIX, big)
    return loss_tile, dx, g


MESH_ID = pl.DeviceIdType.MESH
CHIP_FLIPS = ((1, 0), (0, 1), (1, 1))


def _pos():
    return lax.axis_index("x"), lax.axis_index("y"), lax.axis_index("c")


def _flip(v, b):
    return 1 - v if b else v


def small_all_gather(name, x):
    rows, n = x.shape

    def body(x_ref, out_ref, send_sems, recv_sems, local_sem):
        mx, my, mc = _pos()
        me = 4 * mx + 2 * my + mc
        mine = pltpu.make_async_copy(x_ref, out_ref.at[me], local_sem)
        mine.start()
        sends, peers = [], []
        for kk in range(1, N_DEV):
            peer = (_flip(mx, (kk >> 2) & 1), _flip(my, (kk >> 1) & 1), _flip(mc, kk & 1))
            cp = pltpu.make_async_remote_copy(src_ref=x_ref, dst_ref=out_ref.at[me], send_sem=send_sems.at[kk - 1],
                                              recv_sem=recv_sems.at[kk - 1], device_id=peer, device_id_type=MESH_ID)
            cp.start()
            sends.append(cp)
            peers.append(peer)
        for kk, peer in enumerate(peers):
            pidx = 4 * peer[0] + 2 * peer[1] + peer[2]
            pltpu.make_async_remote_copy(src_ref=x_ref, dst_ref=out_ref.at[pidx], send_sem=send_sems.at[kk],
                                         recv_sem=recv_sems.at[kk], device_id=peer, device_id_type=MESH_ID).wait_recv()
        for cp in sends:
            cp.wait_send()
        mine.wait()

    return _pcall(
        body,
        name=name,
        out_shape=jax.ShapeDtypeStruct((N_DEV, rows, n), x.dtype),
        in_specs=[pl.BlockSpec(memory_space=pltpu.VMEM)],
        out_specs=pl.BlockSpec(memory_space=pltpu.VMEM),
        scratch_shapes=[pltpu.SemaphoreType.DMA((N_DEV - 1,)), pltpu.SemaphoreType.DMA((N_DEV - 1,)), pltpu.SemaphoreType.DMA],
        compiler_params=pltpu.CompilerParams(vmem_limit_bytes=VMEM_LIMIT),
    )(x)


def _hbm_exchange(name, ins, out_shapes, plan):
    n_in = len(ins)
    probe_local, probe_remote = plan([None] * n_in, [None] * len(out_shapes), probe=True)

    def body(*refs):
        in_refs, out_refs = refs[:n_in], refs[n_in : n_in + len(out_shapes)]
        send_sems, recv_sems, local_sems = refs[n_in + len(out_shapes) :]
        local, remote = plan(in_refs, out_refs, probe=False)
        lcs = [pltpu.make_async_copy(s, d, local_sems.at[k]) for k, (s, d) in enumerate(local)]
        for cp in lcs:
            cp.start()
        rcs = []
        for k, (s, d, peer, _) in enumerate(remote):
            cp = pltpu.make_async_remote_copy(src_ref=s, dst_ref=d, send_sem=send_sems.at[k], recv_sem=recv_sems.at[k],
                                              device_id=peer, device_id_type=MESH_ID)
            cp.start()
            rcs.append(cp)
        for k, (s, _, peer, here) in enumerate(remote):
            pltpu.make_async_remote_copy(src_ref=s, dst_ref=here, send_sem=send_sems.at[k], recv_sem=recv_sems.at[k],
                                         device_id=peer, device_id_type=MESH_ID).wait_recv()
        for cp in rcs:
            cp.wait_send()
        for cp in lcs:
            cp.wait()

    return _pcall(
        body,
        name=name,
        out_shape=list(out_shapes),
        in_specs=[pl.BlockSpec(memory_space=pl.ANY)] * n_in,
        out_specs=[pl.BlockSpec(memory_space=pl.ANY)] * len(out_shapes),
        scratch_shapes=[pltpu.SemaphoreType.DMA((max(probe_remote, 1),)), pltpu.SemaphoreType.DMA((max(probe_remote, 1),)),
                        pltpu.SemaphoreType.DMA((max(probe_local, 1),))],
    )(*ins)


def _at_axis(ref, axis, start, size):
    return ref.at[tuple(pl.ds(start, size) if a == axis else slice(None) for a in range(len(ref.shape)))]


HBM_SPEC = pl.BlockSpec(memory_space=pltpu.HBM)
SEM_SPEC = pl.BlockSpec(memory_space=pltpu.SEMAPHORE)
SIDE_EFFECT = pltpu.SideEffectType.DATAFLOW_SIDE_EFFECTING


def _in_hbm(a):
    return pltpu.with_memory_space_constraint(a, pltpu.HBM)


def _copies_start(name, bufs, counts, plan, after=None):
    n, ng = len(bufs), len(counts)
    extra = [] if after is None else [after]
    n_in = n + len(extra)

    def body(*refs):
        sems = refs[n_in : n_in + 2 * ng]
        token = refs[n_in + 2 * ng + n]
        for gi, copies in enumerate(plan(refs[:n])):
            for k, (s, d, peer) in enumerate(copies):
                pltpu.make_async_remote_copy(src_ref=s, dst_ref=d, send_sem=sems[2 * gi].at[k], recv_sem=sems[2 * gi + 1].at[k],
                                             device_id=peer, device_id_type=MESH_ID).start()
        token[...] = jnp.zeros(token.shape, token.dtype)

    out = _pcall(
        body,
        name=name,
        out_shape=tuple(pltpu.SemaphoreType.DMA((c,)) for c in counts for _ in range(2))
        + tuple(pltpu.HBM(b.shape, b.dtype) for b in bufs) + (jax.ShapeDtypeStruct((8, 128), F32),),
        in_specs=(HBM_SPEC,) * n + (pl.BlockSpec(memory_space=pl.ANY),) * len(extra),
        out_specs=(SEM_SPEC,) * (2 * ng) + (HBM_SPEC,) * n + (pl.BlockSpec(memory_space=pltpu.VMEM),),
        input_output_aliases={i: 2 * ng + i for i in range(n)},
        compiler_params=pltpu.CompilerParams(has_side_effects=SIDE_EFFECT),
    )(*[_in_hbm(b) for b in bufs], *extra)
    return [(out[2 * g], out[2 * g + 1]) for g in range(ng)], list(out[2 * ng : 2 * ng + n]), out[2 * ng + n]


def _copies_wait(name, sems, bufs, plan, after):
    n, ng = len(bufs), len(sems)

    def body(*refs):
        for gi, copies in enumerate(plan(refs[:n])):
            for k, (s, d, peer) in enumerate(copies):
                cp = pltpu.make_async_remote_copy(src_ref=s, dst_ref=d, send_sem=refs[n + 2 * gi].at[k], recv_sem=refs[n + 2 * gi + 1].at[k],
                                                  device_id=peer, device_id_type=MESH_ID)
                cp.wait_send()
                cp.wait_recv()

    out = _pcall(
        body,
        name=name,
        out_shape=tuple(pltpu.HBM(b.shape, b.dtype) for b in bufs),
        in_specs=(HBM_SPEC,) * n + (SEM_SPEC,) * (2 * ng) + (pl.BlockSpec(memory_space=pl.ANY),),
        out_specs=(HBM_SPEC,) * n,
        input_output_aliases={i: i for i in range(n)},
        compiler_params=pltpu.CompilerParams(has_side_effects=SIDE_EFFECT),
    )(*bufs, *[s for pair in sems for s in pair], after)
    return list(out)


def _matrix_groups(depth):
    out = []
    for i, (kind, j, _, _) in enumerate(_layer_plan(depth)):
        out.append(([("pool_w", j, 1)], [("attn_w_qkv", j, 1), ("attn_w_o", j, 0)], [("ret_w_in", j, 1), ("ret_w_out", j, 0)])[kind])
        out.append([("ffn_w_up", i, 1), ("ffn_w_down", i, 0)])
    return out


MIX, FFN = 0, 1


def _peers(mx, my, mc):
    out = []
    for fx, fy in CHIP_FLIPS:
        px, py = _flip(mx, fx), _flip(my, fy)
        out.append(((px, py, mc), 2 * px + py))
    return out


def full_buffers(shards, names, layers):
    out = []
    for name, _, axis in [e for layer in layers for e in layer]:
        shp = list(shards[names.index(name)].shape[1:])
        shp[axis] *= N_CHIP
        out.append(lax.empty(tuple(shp), BF16))
    return out


def _gather_plan(names, layers, group, n_shards, here):
    def plan(refs):
        mx, my, mc = _pos()
        s_refs, f_refs = refs[:n_shards], refs[n_shards:]
        groups, k = [], 0
        for gi, layer in enumerate(layers):
            if group is not None and gi != group:
                continue
            copies = []
            for name, idx, axis in layer:
                src = s_refs[names.index(name)].at[idx]
                n = src.shape[axis]
                for peer, pchip in _peers(mx, my, mc) + [((mx, my, 1 - mc), 2 * mx + my)]:
                    at = pchip if here else 2 * mx + my
                    copies.append((src, _at_axis(f_refs[k], axis, at * n, n), peer))
                k += 1
            groups.append(copies)
        return groups

    return plan


def gather_start(shards, names, layers, fulls, after):
    counts = [len(layer) * (len(CHIP_FLIPS) + 1) for layer in layers]
    sems, bufs, _ = _copies_start("gather_start", list(shards) + list(fulls), counts,
                                  _gather_plan(names, layers, None, len(shards), False), after)
    return sems, bufs[: len(shards)], bufs[len(shards) :]


def gather_wait(g, sems_g, shards, names, layers, fulls_g, after):
    bufs = _copies_wait(f"gather_wait_{g}", [sems_g], list(shards) + list(fulls_g), _gather_plan(names, layers, g, len(shards), True), after)
    return bufs[: len(shards)], bufs[len(shards) :]


def _scatter_plan(layer_entries, n_grads, land_of):
    def plan(refs):
        mx, my, mc = _pos()
        groups, k = [], 0
        for layer in layer_entries:
            copies = []
            for name, idx, axis in layer:
                gref, land = refs[k], refs[n_grads + land_of(k, name)]
                n = gref.shape[axis] // N_CHIP
                for slot, (peer, pchip) in enumerate(_peers(mx, my, mc)):
                    copies.append((_at_axis(gref, axis, pchip * n, n), land.at[slot, idx], peer))
                k += 1
            groups.append(copies)
        return groups

    return plan


def scatter_start(i, layer, grads, lands):
    sems, bufs, token = _copies_start(f"scatter_start_{i}", list(grads) + list(lands), [len(layer) * len(CHIP_FLIPS)],
                                      _scatter_plan([layer], len(grads), lambda k, name: k))
    return sems[0], bufs[: len(grads)], bufs[len(grads) :], token


def scatter_wait(sems, layers, grads, names, lands, after):
    bufs = _copies_wait("scatter_wait", sems, list(grads) + list(lands),
                        _scatter_plan(layers, len(grads), lambda k, name: names.index(name)), after)
    return bufs[: len(grads)], bufs[len(grads) :]


def sibling_swap(parts):
    def plan(in_refs, out_refs, probe):
        if probe:
            return 0, len(parts)
        mx, my, mc = _pos()
        return [], [(s, o, (mx, my, 1 - mc), o) for s, o in zip(in_refs, out_refs)]

    return _hbm_exchange("sibling_swap", parts, [jax.ShapeDtypeStruct(p.shape, p.dtype) for p in parts], plan)


EW_ROWS = 256


def _ew_call(name, fn, ins, n_out):
    rows, cols = ins[0].shape[-2:]
    tr = EW_ROWS if rows % EW_ROWS == 0 else rows

    def body(*refs):
        outs = fn(*[r[...] for r in refs[: len(ins)]])
        for o_ref, o in zip(refs[len(ins) :], outs):
            o_ref[...] = o

    def spec(a):
        if a.ndim == 3:
            return pl.BlockSpec((a.shape[0], tr, cols), lambda i: (0, i, 0))
        return pl.BlockSpec((tr, cols), lambda i: (i, 0))

    return _pcall(
        body,
        name=name,
        grid=(rows // tr,),
        in_specs=[spec(a) for a in ins],
        out_specs=[pl.BlockSpec((tr, cols), lambda i: (i, 0))] * n_out,
        out_shape=[jax.ShapeDtypeStruct((rows, cols), F32)] * n_out,
        compiler_params=pltpu.CompilerParams(dimension_semantics=("parallel",), vmem_limit_bytes=VMEM_LIMIT),
    )(*ins)


def _adamw(w, g, m, v):
    m = ADAM_B1 * m + (1.0 - ADAM_B1) * g
    v = ADAM_B2 * v + (1.0 - ADAM_B2) * (g * g)
    m_hat = m / (1.0 - ADAM_B1 ** ADAM_STEP)
    v_hat = v / (1.0 - ADAM_B2 ** ADAM_STEP)
    return -ADAM_LR * (m_hat / (jnp.sqrt(v_hat) + ADAM_EPS) + ADAM_WD * w), m, v


def sum_slots(name, own, landing):
    def fn(o, l):
        acc = o.astype(F32)
        for k in range(l.shape[0]):
            acc = acc + l[k].astype(F32)
        return (acc,)

    return _ew_call(name, fn, [own, landing], 1)[0]


def adamw_pair(name, w, m, v, p, ps):
    def fn(w, m, v, p, ps):
        g = p + ps
        return (g,) + _adamw(w, g, m, v)

    return _ew_call(name, fn, [w, m, v, p, ps], 4)


def adamw_one(name, w, m, v, g):
    return _ew_call(name, lambda w, m, v, g: _adamw(w, g, m, v), [w, m, v, g], 3)


def reduce_devices(name, x):
    def fn(a):
        acc = a[0]
        for k in range(1, a.shape[0]):
            acc = acc + a[k]
        return (acc,)

    return _ew_call(name, fn, [x], 1)[0]


ADA_ROWS = 16
ADA_CTX = N_DEV


def ada_fwd(s9, ada_w, ada_b):
    depth, _, n = ada_w.shape

    def body(s_ref, w_ref, b_ref, o_ref):
        s = s_ref[...]
        o_ref[...] = _dotf((s * _sigmoid(s)).astype(BF16), w_ref[...].astype(BF16)) + b_ref[...]

    return _pcall(
        body,
        name="ada_fwd",
        grid=(depth,),
        in_specs=[pl.BlockSpec((ADA_ROWS, D), lambda i: (0, 0)), pl.BlockSpec((None, D, n), lambda i: (i, 0, 0)),
                  pl.BlockSpec((None, 1, n), lambda i: (i, 0, 0))],
        out_specs=pl.BlockSpec((None, ADA_ROWS, n), lambda i: (i, 0, 0)),
        out_shape=jax.ShapeDtypeStruct((depth, ADA_ROWS, n), F32),
        compiler_params=pltpu.CompilerParams(dimension_semantics=("arbitrary",), vmem_limit_bytes=VMEM_LIMIT),
    )(s9, ada_w, ada_b)


def ada_bwd(s9, dm, ada_w):
    depth, _, n = ada_w.shape

    def body(s_ref, dm_ref, w_ref, gw_ref, ds_ref):
        _acc_init(pl.program_id(0), ds_ref)
        s = s_ref[...]
        dmb = dm_ref[...].astype(BF16)
        gw_ref[...] = _dot_tn((s * _sigmoid(s)).astype(BF16), dmb)
        ds_ref[...] += _dot_nt(dmb, w_ref[...].astype(BF16))

    return _pcall(
        body,
        name="ada_bwd",
        grid=(depth,),
        in_specs=[pl.BlockSpec((ADA_ROWS, D), lambda i: (0, 0)), pl.BlockSpec((None, ADA_ROWS, n), lambda i: (i, 0, 0)),
                  pl.BlockSpec((None, D, n), lambda i: (i, 0, 0))],
        out_specs=[pl.BlockSpec((None, D, n), lambda i: (i, 0, 0)), pl.BlockSpec((ADA_ROWS, D), lambda i: (0, 0))],
        out_shape=[jax.ShapeDtypeStruct((depth, D, n), F32), jax.ShapeDtypeStruct((ADA_ROWS, D), F32)],
        compiler_params=pltpu.CompilerParams(dimension_semantics=("arbitrary",), vmem_limit_bytes=VMEM_LIMIT),
    )(s9, dm, ada_w)


def cctx_grad(parts, c_ctx):
    def body(p_ref, c_ref, o_ref):
        acc = p_ref[0, ADA_CTX : ADA_CTX + 1, :]
        for chip in range(1, N_CHIP):
            acc = acc + p_ref[2 * chip, ADA_CTX : ADA_CTX + 1, :]
        c = c_ref[...]
        sg = _sigmoid(c)
        o_ref[...] = acc * (sg * (1.0 + c * (1.0 - sg)))

    return _pcall(body, name="cctx_grad", out_shape=jax.ShapeDtypeStruct((1, D), F32))(parts, c_ctx)


def _pack(arrs):
    flat = jnp.concatenate([a.astype(F32).reshape(-1) for a in arrs])
    rows = -(-flat.shape[0] // (8 * D)) * 8
    return jnp.pad(flat, (0, rows * D - flat.shape[0])).reshape(rows, D)


def _unpack(slab, shapes):
    lead = slab.shape[:-2]
    flat = slab.reshape(lead + (-1,))
    out, off = [], 0
    for shp in shapes:
        n = 1
        for d in shp:
            n *= d
        out.append(flat[..., off : off + n].reshape(lead + tuple(shp)))
        off += n
    return out


def _unshard(per_dev, axis):
    return jnp.concatenate([per_dev[2 * chip] for chip in range(N_CHIP)], axis=axis)


BIG = (("pool_w", 1), ("attn_w_qkv", 1), ("attn_w_o", 0), ("ret_w_in", 1), ("ret_w_out", 0), ("ffn_w_up", 1), ("ffn_w_down", 0))
WEIGHTS = ("c_ctx", "ada_w", "ada_b", "norm_w", "pool_w", "pool_b", "pool_scale", "attn_w_qkv", "attn_q_gain", "attn_k_gain",
           "attn_w_o", "ret_w_in", "ret_decay_logit", "ret_gn_w", "ret_w_out", "ffn_w_up", "ffn_conv_w", "ffn_conv_b", "ffn_w_down")
SMALL = tuple(n for n in WEIGHTS if n != "ada_w" and n not in dict(BIG))
SMALL_SHARD_AXIS = {"norm_w": 2, "pool_b": 1, "pool_scale": 1, "ret_gn_w": 1, "ffn_conv_w": 2}


def kernel(x, c, ctx, c_ctx, ada_w, ada_b, norm_w, pool_w, pool_b, pool_scale, attn_w_qkv, attn_q_gain, attn_k_gain, attn_w_o, ret_w_in, ret_decay_logit, ret_gn_w, ret_w_out, ffn_w_up, ffn_conv_w, ffn_conv_b, ffn_w_down, loss_target, m_c_ctx, m_ada_w, m_ada_b, m_norm_w, m_pool_w, m_pool_b, m_pool_scale, m_attn_w_qkv, m_attn_q_gain, m_attn_k_gain, m_attn_w_o, m_ret_w_in, m_ret_decay_logit, m_ret_gn_w, m_ret_w_out, m_ffn_w_up, m_ffn_conv_w, m_ffn_conv_b, m_ffn_w_down, v_c_ctx, v_ada_w, v_ada_b, v_norm_w, v_pool_w, v_pool_b, v_pool_scale, v_attn_w_qkv, v_attn_q_gain, v_attn_k_gain, v_attn_w_o, v_ret_w_in, v_ret_decay_logit, v_ret_gn_w, v_ret_w_out, v_ffn_w_up, v_ffn_conv_w, v_ffn_conv_b, v_ffn_w_down):
    P = dict(zip(WEIGHTS, (c_ctx, ada_w, ada_b, norm_w, pool_w, pool_b, pool_scale, attn_w_qkv, attn_q_gain, attn_k_gain, attn_w_o,
                           ret_w_in, ret_decay_logit, ret_gn_w, ret_w_out, ffn_w_up, ffn_conv_w, ffn_conv_b, ffn_w_down)))
    M = dict(zip(WEIGHTS, (m_c_ctx, m_ada_w, m_ada_b, m_norm_w, m_pool_w, m_pool_b, m_pool_scale, m_attn_w_qkv, m_attn_q_gain,
                           m_attn_k_gain, m_attn_w_o, m_ret_w_in, m_ret_decay_logit, m_ret_gn_w, m_ret_w_out, m_ffn_w_up,
                           m_ffn_conv_w, m_ffn_conv_b, m_ffn_w_down)))
    V = dict(zip(WEIGHTS, (v_c_ctx, v_ada_w, v_ada_b, v_norm_w, v_pool_w, v_pool_b, v_pool_scale, v_attn_w_qkv, v_attn_q_gain,
                           v_attn_k_gain, v_attn_w_o, v_ret_w_in, v_ret_decay_logit, v_ret_gn_w, v_ret_w_out, v_ffn_w_up,
                           v_ffn_conv_w, v_ffn_conv_b, v_ffn_w_down)))
    depth, s, l = ada_w.shape[0], x.shape[1], ctx.shape[1]
    assert l == BM and s % BM == 0 and s % GRID_W == 0
    nlb = s // BM
    n_pool = pool_w.shape[0]
    mx, my, mc = _pos()
    chip, dev = 2 * mx + my, 4 * mx + 2 * my + mc
    nada = ada_w.shape[2]

    sharded = [n for n in SMALL if n in SMALL_SHARD_AXIS]
    got = small_all_gather("gather_small", _pack([c[0]] + [P[n] for n in sharded]))
    got = _unpack(got, [(D,)] + [P[n].shape for n in sharded])
    c_all = got[0]
    full = {n: _unshard(g_, SMALL_SHARD_AXIS[n]) for n, g_ in zip(sharded, got[1:])}

    s9 = jnp.concatenate([c_all, c_ctx[None, :], jnp.zeros((ADA_ROWS - N_DEV - 1, D), F32)], axis=0)
    ada_b_mine = lax.dynamic_slice_in_dim(ada_b, chip * nada, nada, axis=1)[:, None, :]
    mod_part = ada_fwd(s9, ada_w, ada_b_mine)
    mod_all = _unshard(small_all_gather("gather_mod", mod_part.reshape(depth * ADA_ROWS, nada)), 1).reshape(depth, ADA_ROWS, 6, D)
    mod_mine = lax.dynamic_index_in_dim(mod_all, dev, axis=1, keepdims=False)
    mods_all = jnp.pad(jnp.stack([mod_mine, mod_all[:, ADA_CTX]], axis=1), ((0, 0), (0, 0), (0, 2), (0, 0)))
    mods = [mods_all[i] for i in range(depth)]

    names = [n for n, _ in BIG]
    layers = _matrix_groups(depth)
    shards = [P[n].astype(BF16) for n in names]
    gsems, shards, fulls = gather_start(shards, names, layers, full_buffers(shards, names, layers), mods_all)
    first = [sum(len(layer) for layer in layers[:g_]) for g_ in range(len(layers) + 1)]
    flight = {"shards": shards}

    def fetch(i, part, after):
        g_ = 2 * i + part
        flight["shards"], mats = gather_wait(g_, gsems[g_], flight["shards"], names, layers, fulls[first[g_] : first[g_ + 1]], after)
        return {name: m for (name, _, _), m in zip(layers[g_], mats)}

    lands = {n: lax.empty((len(CHIP_FLIPS),) + P[n].shape, BF16) for n in names}
    sent = {}

    def emit(i, part, big):
        g_ = 2 * i + part
        lnames = [name for name, _, _ in layers[g_]]
        sems, gl, ll, token = scatter_start(g_, layers[g_], [big[n] for n in lnames], [lands[n] for n in lnames])
        lands.update(zip(lnames, ll))
        sent[g_] = (sems, gl)
        return token

    w = {
        "nw": full["norm_w"],
        "pbs": jnp.concatenate([full["pool_b"][:, None], full["pool_scale"][:, None], jnp.zeros((n_pool, 6, D), F32)], axis=1),
        "gains": jnp.concatenate([attn_q_gain, attn_k_gain, jnp.zeros((6, HD), F32)], axis=0),
        "gnw": full["ret_gn_w"],
        "logit_b": jnp.broadcast_to(ret_decay_logit[0][:, :, None, None], (2, RET_HEADS, 8, 128)),
        "cw": jnp.concatenate([full["ffn_conv_w"], ffn_conv_b[:, None, :], jnp.zeros((depth, 4, 2 * D_FF), F32)], axis=1),
    }
    w["cos"], w["sin"] = rope_tables(s, l)

    xs = jnp.concatenate([x[0], ctx[0]], axis=0)
    loss_tile, dxs, g = local_step(xs, loss_target[0], mods, w, nlb, depth, fetch, emit)
    loss = lax.psum(loss_tile[0, 0], MESH_AXES)
    grad_x = dxs[None]

    small_shapes = [(depth, 2, 8, D), (depth, 2, D), (n_pool, 2, D), (2, HD), (2, RET_HEADS), (RNV,), (depth, 4, 2 * D_FF)]
    slab = _pack([
        jnp.stack(g["dmod"]),
        jnp.stack([jnp.stack([g["dnw1"][i][0], g["dnw2"][i][0]]) for i in range(depth)]),
        jnp.stack([g["dpbs"][j][0:2] for j in range(n_pool)]),
        g["dgains"][0:2], g["dlogit"][:, :, 0, 0], g["dgnw"][0], jnp.stack([g["dcw"][i][0:4] for i in range(depth)]),
    ])
    slabs = small_all_gather("gather_small_grads", slab)
    dmod_dev = _unpack(slabs, small_shapes[:1])[0]
    t_dmod, t_nw, t_pbs, t_gains, t_logit, t_gnw, t_cw = _unpack(reduce_devices("reduce_small_grads", slabs), small_shapes)

    def cols(a):
        return lax.dynamic_slice_in_dim(a, chip * nada, nada, axis=a.ndim - 1)

    dm_lat = jnp.swapaxes(cols(dmod_dev[:, :, 0, :6].reshape(N_DEV, depth, 6 * D)), 0, 1)
    dm_ctx = cols(t_dmod[:, 1, :6].reshape(depth, 1, 6 * D))
    dm = jnp.concatenate([dm_lat, dm_ctx, jnp.zeros((depth, ADA_ROWS - N_DEV - 1, nada), F32)], axis=1)
    g_ada_w, ds9 = ada_bwd(s9, dm, ada_w)
    g_c_ctx = cctx_grad(small_all_gather("gather_dcctx", ds9), c_ctx[None, :])[0]

    def mine(a, name):
        n = P[name].shape[SMALL_SHARD_AXIS[name]]
        return lax.dynamic_slice_in_dim(a, chip * n, n, axis=SMALL_SHARD_AXIS[name])

    G = {
        "c_ctx": g_c_ctx,
        "ada_b": (t_dmod[:, 0, :6] + t_dmod[:, 1, :6]).reshape(depth, 6 * D),
        "norm_w": mine(t_nw, "norm_w"),
        "pool_b": mine(t_pbs[:, 0], "pool_b"), "pool_scale": mine(t_pbs[:, 1], "pool_scale"),
        "attn_q_gain": t_gains[0:1], "attn_k_gain": t_gains[1:2],
        "ret_decay_logit": t_logit[None], "ret_gn_w": mine(t_gnw[None], "ret_gn_w"),
        "ffn_conv_w": mine(t_cw[:, 0:3], "ffn_conv_w"), "ffn_conv_b": t_cw[:, 3],
    }
    sw, sg, sm, sv = (_pack([d_[n] for n in SMALL]) for d_ in (P, G, M, V))
    outs = adamw_one("adamw_small", sw, sm, sv, sg)
    D_, NM, NV = ({n: a for n, a in zip(SMALL, _unpack(o, [P[n].shape for n in SMALL]))} for o in outs)

    flat2 = lambda a: a.reshape(-1, a.shape[-1])
    G["ada_w"] = g_ada_w
    o3 = adamw_one("adamw_ada", flat2(ada_w), flat2(M["ada_w"]), flat2(V["ada_w"]), flat2(g_ada_w))
    D_["ada_w"], NM["ada_w"], NV["ada_w"] = (o.reshape(ada_w.shape) for o in o3)

    sent_grads, landed = scatter_wait([sent[g_][0] for g_ in range(len(layers))], layers,
                                      [a for g_ in range(len(layers)) for a in sent[g_][1]], names, [lands[n] for n in names], o3[0])
    own = {n: [None] * P[n].shape[0] for n in names}
    for (name, idx, axis), a in zip([e for layer in layers for e in layer], sent_grads):
        n_ = a.shape[axis] // N_CHIP
        own[name][idx] = lax.dynamic_slice_in_dim(a, chip * n_, n_, axis=axis)
    partial = [sum_slots("sum_" + n, jnp.stack(own[n]).reshape(-1, lnd.shape[-1]), lnd.reshape(len(CHIP_FLIPS), -1, lnd.shape[-1]))
               for n, lnd in zip(names, landed)]
    theirs = sibling_swap(partial)
    for (n, _), p, ps in zip(BIG, partial, theirs):
        o4 = adamw_pair("adamw_" + n, flat2(P[n]), flat2(M[n]), flat2(V[n]), p, ps)
        G[n], D_[n], NM[n], NV[n] = (o.reshape(P[n].shape) for o in o4)

    return (loss, grad_x, *[G[n] for n in WEIGHTS], *[D_[n] for n in WEIGHTS], *[NM[n] for n in WEIGHTS], *[NV[n] for n in WEIGHTS])
```

```python
import functools

import jax
import jax.numpy as jnp
from jax import lax
from jax.experimental import pallas as pl
from jax.experimental.pallas import tpu as pltpu

F32 = jnp.float32
BF16 = jnp.bfloat16

D = 1024
BM = 256
EPS = 1e-6
POOL_WINDOWS = (2, 4, 8, 16)
POOL_GROUP = D // 4
ATTN_HEADS = 8
ATTN_KV_HEADS = 2
HD = D // ATTN_HEADS
ATTN_GROUP = ATTN_HEADS // ATTN_KV_HEADS
NQ = ATTN_HEADS * HD
NKV = ATTN_KV_HEADS * HD
GRID_W = 64
ROPE_THETA = 10000.0
RET_HEADS = 4
RET_DK = D // RET_HEADS
RET_DV = 2 * D // RET_HEADS
RNQ = RET_HEADS * RET_DK
RNV = RET_HEADS * RET_DV
D_FF = 2816
FF_CHUNK = 256
ADAM_LR, ADAM_B1, ADAM_B2, ADAM_EPS, ADAM_WD, ADAM_STEP = 0.001, 0.9, 0.999, 1e-08, 0.01, 10
HALO_F32 = 8
HALO_BF16 = 16
VMEM_LIMIT = 60 * 1024 * 1024

MESH_AXES = ("x", "y", "c")
N_DEV = 8
N_CHIP = 4


def _pcall(body, **kw):
    return pl.pallas_call(body, **kw)


def _spec(shape, kind, nlb, nblk):
    nd = len(shape)
    if kind == "row":
        return pl.BlockSpec((BM, shape[1]), lambda i: (i, 0))
    if kind == "row_lat":
        return pl.BlockSpec((BM, shape[1]), lambda i: (jnp.minimum(i, nlb - 1), 0))
    if kind == "full":
        return pl.BlockSpec(tuple(shape), lambda i: (0,) * nd, pipeline_mode=pl.Buffered(1))
    if isinstance(kind, tuple) and kind[0] == "fullat":
        return pl.BlockSpec((None,) + tuple(shape[1:]), lambda i: (kind[1],) + (0,) * (nd - 1), pipeline_mode=pl.Buffered(1))
    if kind == "acc":
        return pl.BlockSpec(tuple(shape), lambda i: (0,) * nd)
    if kind == "any":
        return pl.BlockSpec(memory_space=pl.ANY)
    if kind == "stream":
        return pl.BlockSpec((1,) + tuple(shape[1:]), lambda i: (i // nlb,) + (0,) * (nd - 1))
    if kind in ("prev8", "prev16"):
        hb = int(kind[4:])
        return pl.BlockSpec((hb, shape[1]), lambda i: (jnp.maximum(i * (BM // hb) - 1, 0), 0))
    if kind in ("next8", "next16"):
        hb = int(kind[4:])
        last = nblk * BM // hb - 1
        return pl.BlockSpec((hb, shape[1]), lambda i: (jnp.minimum((i + 1) * (BM // hb), last), 0))
    raise ValueError(kind)


def _rowcall(name, body, nblk, nlb, ins, outs, scratch=()):
    return _pcall(
        body,
        name=name,
        grid=(nblk,),
        in_specs=[_spec(a.shape, k, nlb, nblk) for a, k in ins],
        out_specs=[_spec(s, k, nlb, nblk) for s, _, k in outs],
        out_shape=[jax.ShapeDtypeStruct(s, d) for s, d, _ in outs],
        scratch_shapes=list(scratch),
        compiler_params=pltpu.CompilerParams(dimension_semantics=("arbitrary",), vmem_limit_bytes=VMEM_LIMIT),
    )(*[a for a, _ in ins])


def _wk(w):
    return (w[0], ("fullat", w[1])) if isinstance(w, tuple) else (w, "full")


def _stream_edges(i, nlb):
    is_ctx = i == nlb
    return (i == 0) | is_ctx, (i == nlb - 1) | is_ctx, is_ctx


def _dotf(a, b):
    return jnp.dot(a, b, preferred_element_type=F32)


def _dot_nt(a, b):
    return lax.dot_general(a, b, (((1,), (1,)), ((), ())), preferred_element_type=F32)


def _dot_tn(a, b):
    return lax.dot_general(a, b, (((0,), (0,)), ((), ())), preferred_element_type=F32)


def _sigmoid(x):
    return 0.5 * jnp.tanh(0.5 * x) + 0.5


def _norm_mod(x, nw, sh, sc):
    r = lax.rsqrt(jnp.mean(x * x, axis=-1, keepdims=True) + EPS)
    xhat = x * r
    n = xhat * nw
    return n * (1.0 + sc) + sh, n, xhat, r


def _norm_mod_bwd(dh, n, xhat, r, nw, sc):
    dsh = jnp.sum(dh, axis=0, keepdims=True)
    dsc = jnp.sum(dh * n, axis=0, keepdims=True)
    dn = dh * (1.0 + sc)
    dnw = jnp.sum(dn * xhat, axis=0, keepdims=True)
    dxhat = dn * nw
    dx = r * (dxhat - xhat * jnp.mean(dxhat * xhat, axis=-1, keepdims=True))
    return dx, dsh, dsc, dnw


def _acc_init(i, *refs):
    @pl.when(i == 0)
    def _():
        for r in refs:
            r[...] = jnp.zeros(r.shape, r.dtype)


SH1, SC1, G1, SH2, SC2, G2 = range(6)


def _mrow(mod_ref, k):
    return mod_ref[0, k : k + 1, :]


def _shift_rows(x_ref, xp_ref, xn_ref, cs, first, last):
    cur = x_ref[:, cs].astype(F32)
    rows = lax.broadcasted_iota(jnp.int32, cur.shape, 0)
    pr = jnp.where(first, 0.0, xp_ref[HALO_BF16 - 1 : HALO_BF16, cs].astype(F32))
    nx = jnp.where(last, 0.0, xn_ref[0:1, cs].astype(F32))
    dn = jnp.where(rows == 0, pr, pltpu.roll(cur, 1, 0))
    up = jnp.where(rows == BM - 1, nx, pltpu.roll(cur, BM - 1, 0))
    return dn, cur, up


def ffn_up(x1, mod, nw2, w_up, nblk, nlb):
    r = x1.shape[0]

    def body(x_ref, mod_ref, nw_ref, w_ref, u_ref, h_ref):
        h, _, _, _ = _norm_mod(x_ref[...], nw_ref[...], _mrow(mod_ref, SH2), _mrow(mod_ref, SC2))
        hb = h.astype(BF16)
        h_ref[...] = hb
        u_ref[...] = _dotf(hb, w_ref[...]).astype(BF16)

    return _rowcall(
        "ffn_up", body, nblk, nlb,
        [(x1, "row"), (mod, "stream"), (nw2, "full"), _wk(w_up)],
        [((r, 2 * D_FF), BF16, "row"), ((r, D), BF16, "row")],
    )


def _conv_gate_chunk(u_ref, up_ref, un_ref, cw_ref, j, first, last):
    res = []
    for half in range(2):
        c0 = half * D_FF + j * FF_CHUNK
        cs = slice(c0, c0 + FF_CHUNK)
        dn, cur, up = _shift_rows(u_ref, up_ref, un_ref, cs, first, last)
        val = dn * cw_ref[0:1, cs] + cur * cw_ref[1:2, cs] + up * cw_ref[2:3, cs] + cw_ref[3:4, cs]
        res.append((val, dn, cur, up, cs))
    return res


def ffn_down(u, cw, w_down, x1, mod, nblk, nlb):
    r = x1.shape[0]

    def body(u_ref, up_ref, un_ref, cw_ref, w_ref, x_ref, mod_ref, x2_ref, f_ref, uc_ref):
        first, last, _ = _stream_edges(pl.program_id(0), nlb)
        f = jnp.zeros((BM, D), F32)
        for j in range(D_FF // FF_CHUNK):
            (a, _, _, _, acs), (v, _, _, _, vcs) = _conv_gate_chunk(u_ref, up_ref, un_ref, cw_ref, j, first, last)
            uc_ref[:, acs] = a.astype(BF16)
            uc_ref[:, vcs] = v.astype(BF16)
            f = f + _dotf((a * _sigmoid(a) * v).astype(BF16), w_ref[acs, :])
        f_ref[...] = f.astype(BF16)
        x2_ref[...] = x_ref[...] + _mrow(mod_ref, G2) * f

    return _rowcall(
        "ffn_down", body, nblk, nlb,
        [(u, "row"), (u, "prev16"), (u, "next16"), _wk(cw), _wk(w_down), (x1, "row"), (mod, "stream")],
        [((r, D), F32, "row"), ((r, D), BF16, "row"), ((r, 2 * D_FF), BF16, "row")],
    )


def ffn_bwd1(dx2, f, uc, w_down, mod, dep, nblk, nlb):
    r = dx2.shape[0]

    def body(dx_ref, f_ref, uc_ref, w_ref, mod_ref, dep_ref, duc_ref, dw_ref, dcb_ref, dmod_ref, dw_acc):
        i = pl.program_id(0)
        _acc_init(i, dcb_ref, dmod_ref, dw_acc)
        dx = dx_ref[...]
        df = (_mrow(mod_ref, G2) * dx).astype(BF16)
        dmod_ref[i // nlb, G2 : G2 + 1, :] += jnp.sum(dx * f_ref[...].astype(F32), axis=0, keepdims=True)
        for j in range(D_FF // FF_CHUNK):
            acs = slice(j * FF_CHUNK, (j + 1) * FF_CHUNK)
            vcs = slice(D_FF + j * FF_CHUNK, D_FF + (j + 1) * FF_CHUNK)
            a, v = uc_ref[:, acs].astype(F32), uc_ref[:, vcs].astype(F32)
            sa = _sigmoid(a)
            asa = a * sa
            dw_acc[acs, :] += _dot_tn((asa * v).astype(BF16), df)
            dg = _dot_nt(df, w_ref[acs, :])
            for dval, cs in ((dg * v * (sa * (1.0 + a * (1.0 - sa))), acs), (dg * asa, vcs)):
                duc_ref[:, cs] = dval.astype(BF16)
                dcb_ref[3:4, cs] += jnp.sum(dval, axis=0, keepdims=True)

        @pl.when(i == nblk - 1)
        def _():
            dw_ref[...] = dw_acc[...].astype(BF16)

    return _rowcall(
        "ffn_bwd1", body, nblk, nlb,
        [(dx2, "row"), (f, "row"), (uc, "row"), _wk(w_down), (mod, "stream"), (dep, "any")],
        [((r, 2 * D_FF), BF16, "row"), ((D_FF, D), BF16, "acc"), ((8, 2 * D_FF), F32, "acc"), ((2, 8, D), F32, "acc")],
        scratch=[pltpu.VMEM((D_FF, D), F32)],
    )


def ffn_bwd3(duc, u, cw, w_up, x1, dx2, mod, nw2, nblk, nlb):
    r = dx2.shape[0]

    def body(d_ref, dp_ref, dn_ref, u_ref, cw_ref, w_ref, x_ref, dx_ref, mod_ref, nw_ref, du_ref, dx1_ref, dcw_ref, dnw_ref, dmod_ref):
        i = pl.program_id(0)
        first, last, _ = _stream_edges(i, nlb)
        _acc_init(i, dcw_ref, dnw_ref, dmod_ref)
        dh = jnp.zeros((BM, D), F32)
        for j in range(2 * D_FF // FF_CHUNK):
            cs = slice(j * FF_CHUNK, (j + 1) * FF_CHUNK)
            dn, cur, up = _shift_rows(d_ref, dp_ref, dn_ref, cs, first, last)
            du = (up * cw_ref[0:1, cs] + cur * cw_ref[1:2, cs] + dn * cw_ref[2:3, cs]).astype(BF16)
            du_ref[:, cs] = du
            dh = dh + _dot_nt(du, w_ref[:, cs])
            uu = u_ref[:, cs].astype(F32)
            dcw_ref[0:1, cs] += jnp.sum(up * uu, axis=0, keepdims=True)
            dcw_ref[1:2, cs] += jnp.sum(cur * uu, axis=0, keepdims=True)
            dcw_ref[2:3, cs] += jnp.sum(dn * uu, axis=0, keepdims=True)
        sc = _mrow(mod_ref, SC2)
        nw = nw_ref[...]
        _, n, xhat, rr = _norm_mod(x_ref[...], nw, _mrow(mod_ref, SH2), sc)
        dxn, dsh, dsc, dnw = _norm_mod_bwd(dh, n, xhat, rr, nw, sc)
        dx1_ref[...] = dx_ref[...] + dxn
        s = i // nlb
        dmod_ref[s, SH2 : SH2 + 1, :] += dsh
        dmod_ref[s, SC2 : SC2 + 1, :] += dsc
        dnw_ref[0:1, :] += dnw

    return _rowcall(
        "ffn_bwd3", body, nblk, nlb,
        [(duc, "row"), (duc, "prev16"), (duc, "next16"), (u, "row"), _wk(cw), _wk(w_up), (x1, "row"), (dx2, "row"),
         (mod, "stream"), (nw2, "full")],
        [((r, 2 * D_FF), BF16, "row"), ((r, D), F32, "row"), ((8, 2 * D_FF), F32, "acc"), ((8, D), F32, "acc"),
         ((2, 8, D), F32, "acc")],
    )


def matmul_tn(a, b, nblk, tn=None):
    k, n = a.shape[1], b.shape[1]
    rows = nblk * BM
    tr = 768 if rows % 768 == 0 else (1024 if rows % 1024 == 0 else BM)
    if tn is None:
        tn = n
        while k * tn * 4 > 6 * 1024 * 1024 and tn % 256 == 0:
            tn //= 2
    steps = rows // tr

    def body(a_ref, b_ref, o_ref, acc):
        t = pl.program_id(1)

        @pl.when(t == 0)
        def _():
            acc[...] = jnp.zeros(acc.shape, acc.dtype)

        acc[...] += _dot_tn(a_ref[...], b_ref[...])

        @pl.when(t == steps - 1)
        def _():
            o_ref[...] = acc[...].astype(o_ref.dtype)

    return _pcall(
        body,
        name="matmul_tn",
        grid=(n // tn, steps),
        in_specs=[pl.BlockSpec((tr, k), lambda j, t: (t, 0)), pl.BlockSpec((tr, tn), lambda j, t: (t, j))],
        out_specs=pl.BlockSpec((k, tn), lambda j, t: (0, j)),
        out_shape=jax.ShapeDtypeStruct((k, n), BF16),
        scratch_shapes=[pltpu.VMEM((k, tn), F32)],
        compiler_params=pltpu.CompilerParams(dimension_semantics=("parallel", "arbitrary"), vmem_limit_bytes=VMEM_LIMIT),
    )(a, b)


EXT = BM + 2 * HALO_F32


def _pool_positions(i, nlb, nrows, row0):
    is_ctx = i == nlb
    t = (i - jnp.where(is_ctx, nlb, 0)) * BM + row0 + lax.broadcasted_iota(jnp.int32, (nrows, 1), 0)
    return t, jnp.where(is_ctx, BM, nlb * BM)


def _pool_cnt(t, win, slen):
    return (jnp.minimum(t + win // 2, slen) - jnp.maximum(t - win // 2, 0)).astype(F32)


def _pool_fill_ext(ext, i, nlb, x_ref, xp_ref, xn_ref, mod_ref, nw_ref):
    first, last, _ = _stream_edges(i, nlb)
    sh, sc, nw = _mrow(mod_ref, SH1), _mrow(mod_ref, SC1), nw_ref[...]
    hcur, n, xhat, r = _norm_mod(x_ref[...], nw, sh, sc)
    ext[0:HALO_F32, :] = jnp.where(first, 0.0, _norm_mod(xp_ref[...], nw, sh, sc)[0])
    ext[HALO_F32 : HALO_F32 + BM, :] = hcur
    ext[HALO_F32 + BM :, :] = jnp.where(last, 0.0, _norm_mod(xn_ref[...], nw, sh, sc)[0])
    return n, xhat, r


def _window_sum(ref, cols, offs):
    acc = None
    for o in offs:
        v = ref[HALO_F32 + o : HALO_F32 + o + BM, cols]
        acc = v if acc is None else acc + v
    return acc


def _pool_diff(ext, g, win, t, slen):
    cols = slice(g * POOL_GROUP, (g + 1) * POOL_GROUP)
    ssum = _window_sum(ext, cols, range(-(win // 2), win // 2))
    return ssum / _pool_cnt(t, win, slen) - ext[HALO_F32 : HALO_F32 + BM, cols]


def pool_fwd(x, mod, nw1, pw, pbs, nblk, nlb):
    r = x.shape[0]

    def body(x_ref, xp_ref, xn_ref, mod_ref, nw_ref, pw_ref, pbs_ref, x1_ref, ypre_ref, ext):
        i = pl.program_id(0)
        _pool_fill_ext(ext, i, nlb, x_ref, xp_ref, xn_ref, mod_ref, nw_ref)
        t, slen = _pool_positions(i, nlb, BM, 0)
        for g, win in enumerate(POOL_WINDOWS):
            cols = slice(g * POOL_GROUP, (g + 1) * POOL_GROUP)
            diff = _pool_diff(ext, g, win, t, slen)
            ypre = _dotf(diff.astype(BF16), pw_ref[g]) + pbs_ref[0:1, cols]
            ypre_ref[:, cols] = ypre
            x1_ref[:, cols] = x_ref[:, cols] + mod_ref[0, G1 : G1 + 1, cols] * (ypre * pbs_ref[1:2, cols])

    return _rowcall(
        "pool_fwd", body, nblk, nlb,
        [(x, "row"), (x, "prev8"), (x, "next8"), (mod, "stream"), (nw1, "full"), _wk(pw), _wk(pbs)],
        [((r, D), F32, "row"), ((r, D), F32, "row")],
        scratch=[pltpu.VMEM((EXT, D), F32)],
    )


def pool_bwd(dx1, x, ypre, mod, nw1, pw, pbs, dep, nblk, nlb, lat_dx):
    r = x.shape[0]

    def body(d_ref, dp_ref, dn_ref, x_ref, xp_ref, xn_ref, ypre_ref, mod_ref, nw_ref, pw_ref, pbs_ref, dep_ref,
             dx_ref, dpw_ref, dpbs_ref, dnw_ref, dmod_ref, ext, dext, eext, dh):
        i = pl.program_id(0)
        first, last, _ = _stream_edges(i, nlb)
        _acc_init(i, dpw_ref, dpbs_ref, dnw_ref, dmod_ref)
        n, xhat, rr = _pool_fill_ext(ext, i, nlb, x_ref, xp_ref, xn_ref, mod_ref, nw_ref)
        g1, scale = _mrow(mod_ref, G1), pbs_ref[1:2, :]
        dcur = d_ref[...]
        ypre = ypre_ref[...]
        s = i // nlb
        dmod_ref[s, G1 : G1 + 1, :] += jnp.sum(dcur * (ypre * scale), axis=0, keepdims=True)
        dy = g1 * dcur
        dpbs_ref[1:2, :] += jnp.sum(dy * ypre, axis=0, keepdims=True)
        dpbs_ref[0:1, :] += jnp.sum(dy * scale, axis=0, keepdims=True)
        gs = g1 * scale
        dext[0:HALO_F32, :] = jnp.where(first, 0.0, gs * dp_ref[...])
        dext[HALO_F32 : HALO_F32 + BM, :] = dy * scale
        dext[HALO_F32 + BM :, :] = jnp.where(last, 0.0, gs * dn_ref[...])
        t, slen = _pool_positions(i, nlb, BM, 0)
        text, _ = _pool_positions(i, nlb, EXT, -HALO_F32)
        for g, win in enumerate(POOL_WINDOWS):
            cols = slice(g * POOL_GROUP, (g + 1) * POOL_GROUP)
            diff = _pool_diff(ext, g, win, t, slen)
            dpre = dext[:, cols].astype(BF16)
            ddiff = _dot_nt(dpre, pw_ref[g])
            eext[...] = ddiff / jnp.maximum(_pool_cnt(text, win, slen), 1.0)
            dh[:, cols] = _window_sum(eext, slice(None), range(-(win // 2) + 1, win // 2 + 1)) - ddiff[HALO_F32 : HALO_F32 + BM, :]
            dpw_ref[g] += _dot_tn(diff.astype(BF16), dpre[HALO_F32 : HALO_F32 + BM, :])
        sc, nw = _mrow(mod_ref, SC1), nw_ref[...]
        dxn, dsh, dsc, dnw = _norm_mod_bwd(dh[...], n, xhat, rr, nw, sc)
        if lat_dx:
            @pl.when(i < nlb)
            def _():
                dx_ref[...] = dcur + dxn
        else:
            dx_ref[...] = dcur + dxn
        dmod_ref[s, SH1 : SH1 + 1, :] += dsh
        dmod_ref[s, SC1 : SC1 + 1, :] += dsc
        dnw_ref[0:1, :] += dnw

    return _rowcall(
        "pool_bwd", body, nblk, nlb,
        [(dx1, "row"), (dx1, "prev8"), (dx1, "next8"), (x, "row"), (x, "prev8"), (x, "next8"), (ypre, "row"),
         (mod, "stream"), (nw1, "full"), _wk(pw), _wk(pbs), (dep, "any")],
        [((nlb * BM, D), F32, "row_lat") if lat_dx else ((r, D), F32, "row"), ((4, POOL_GROUP, POOL_GROUP), F32, "acc"), ((8, D), F32, "acc"), ((8, D), F32, "acc"),
         ((2, 8, D), F32, "acc")],
        scratch=[pltpu.VMEM((EXT, D), F32), pltpu.VMEM((EXT, D), F32), pltpu.VMEM((EXT, POOL_GROUP), F32), pltpu.VMEM((BM, D), F32)],
    )


def rope_tables(s, l):
    rows = s // GRID_W
    row = jnp.broadcast_to(jnp.arange(rows)[:, None], (rows, GRID_W)).reshape(-1).astype(F32)
    col = jnp.broadcast_to(jnp.arange(GRID_W)[None, :], (rows, GRID_W)).reshape(-1).astype(F32)
    axis_dim = HD // 2
    inv = ROPE_THETA ** (-jnp.arange(0, axis_dim, 2, dtype=F32) / axis_dim)
    ar, ac = row[:, None] * inv, col[:, None] * inv
    cos = jnp.concatenate([jnp.cos(ar), jnp.cos(ar), jnp.cos(ac), jnp.cos(ac)], axis=-1)
    sin = jnp.concatenate([-jnp.sin(ar), jnp.sin(ar), -jnp.sin(ac), jnp.sin(ac)], axis=-1)
    return (jnp.concatenate([cos, jnp.ones((l, HD), F32)], axis=0), jnp.concatenate([sin, jnp.zeros((l, HD), F32)], axis=0))


def _partner(x):
    q = HD // 4
    lane = lax.broadcasted_iota(jnp.int32, x.shape, 1)
    return jnp.where((lane // q) % 2 == 0, pltpu.roll(x, HD - q, 1), pltpu.roll(x, q, 1))


def _head_norm(raw, gain):
    r = lax.rsqrt(jnp.mean(raw * raw, axis=-1, keepdims=True) + EPS)
    return raw * r, r


ATTN_SCALE = HD ** -0.5


def qkv_fwd(x, mod, nw1, w_qkv, gains, cos_t, sin_t, nblk, nlb):
    r = x.shape[0]

    def body(x_ref, mod_ref, nw_ref, w_ref, g_ref, c_ref, s_ref, raw_ref, q_ref, k_ref, v_ref, h_ref):
        h = _norm_mod(x_ref[...], nw_ref[...], _mrow(mod_ref, SH1), _mrow(mod_ref, SC1))[0].astype(BF16)
        h_ref[...] = h
        raw_ref[...] = _dotf(h, w_ref[...])
        cos, sin = c_ref[...], s_ref[...]
        for j in range(ATTN_HEADS + ATTN_KV_HEADS):
            isq = j < ATTN_HEADS
            xn = _head_norm(raw_ref[:, j * HD : (j + 1) * HD], None)[0] * (g_ref[0:1, :] if isq else g_ref[1:2, :])
            rot = xn * cos + _partner(xn) * sin
            if isq:
                rot = rot * ATTN_SCALE
            rot = rot.astype(BF16)
            if isq:
                q_ref[:, j * HD : (j + 1) * HD] = rot
            else:
                k_ref[:, (j - ATTN_HEADS) * HD : (j - ATTN_HEADS + 1) * HD] = rot
        v_ref[...] = raw_ref[:, NQ + NKV :].astype(BF16)

    return _rowcall(
        "qkv_fwd", body, nblk, nlb,
        [(x, "row"), (mod, "stream"), (nw1, "full"), (w_qkv, "full"), (gains, "full"), (cos_t, "row"), (sin_t, "row")],
        [((r, NQ + 2 * NKV), F32, "row"), ((r, NQ), BF16, "row"), ((r, NKV), BF16, "row"), ((r, NKV), BF16, "row"),
         ((r, D), BF16, "row")],
    )


def attn_fwd(q, k, v, nblk, nlb):
    r = q.shape[0]

    def body(q_ref, k_ref, v_ref, o_ref, lse_ref):
        def heads(keys):
            for kvh in range(ATTN_KV_HEADS):
                kh = k_ref[keys, kvh * HD : (kvh + 1) * HD]
                vh = v_ref[keys, kvh * HD : (kvh + 1) * HD]
                for g in range(ATTN_GROUP):
                    cs = slice((kvh * ATTN_GROUP + g) * HD, (kvh * ATTN_GROUP + g + 1) * HD)
                    s = _dot_nt(q_ref[:, cs], kh)
                    m = jnp.max(s, axis=-1, keepdims=True)
                    p = jnp.exp(s - m)
                    l = jnp.sum(p, axis=-1, keepdims=True)
                    o_ref[:, cs] = (_dotf(p.astype(BF16), vh) / l).astype(BF16)
                    j = kvh * ATTN_GROUP + g
                    lse_ref[:, j : j + 1] = m + jnp.log(l)

        i = pl.program_id(0)
        pl.when(i < nlb)(lambda: heads(slice(0, r)))
        pl.when(i == nlb)(lambda: heads(slice(nlb * BM, r)))

    return _rowcall(
        "attn_fwd", body, nblk, nlb,
        [(q, "row"), (k, "full"), (v, "full")],
        [((r, NQ), BF16, "row"), ((r, ATTN_HEADS), F32, "row")],
    )


def attn_out_fwd(o, w_o, x, mod, nblk, nlb):
    r = x.shape[0]

    def body(o_ref, w_ref, x_ref, mod_ref, x1_ref, y_ref):
        y = _dotf(o_ref[...], w_ref[...])
        y_ref[...] = y
        x1_ref[...] = x_ref[...] + _mrow(mod_ref, G1) * y

    return _rowcall(
        "attn_out_fwd", body, nblk, nlb,
        [(o, "row"), (w_o, "full"), (x, "row"), (mod, "stream")],
        [((r, D), F32, "row"), ((r, D), F32, "row")],
    )


def mix_out_bwd(name, dx1, y, w_out, mod, dep, nblk, nlb):
    r = dx1.shape[0]
    kin = w_out.shape[0]

    def body(d_ref, y_ref, w_ref, mod_ref, dep_ref, dy_ref, do_ref, dmod_ref):
        i = pl.program_id(0)
        _acc_init(i, dmod_ref)
        d = d_ref[...]
        dmod_ref[i // nlb, G1 : G1 + 1, :] += jnp.sum(d * y_ref[...], axis=0, keepdims=True)
        dy = (_mrow(mod_ref, G1) * d).astype(BF16)
        dy_ref[...] = dy
        do_ref[...] = _dot_nt(dy, w_ref[...]).astype(do_ref.dtype)

    return _rowcall(
        name, body, nblk, nlb,
        [(dx1, "row"), (y, "row"), (w_out, "full"), (mod, "stream"), (dep, "any")],
        [((r, D), BF16, "row"), ((r, kin), BF16, "row"), ((2, 8, D), F32, "acc")],
    )


ATTN_KCHUNK = 11 * 128


def attn_bwd(q, k, v, o, do, lse, nblk, nlb):
    r = q.shape[0]
    kc = ATTN_KCHUNK if r % ATTN_KCHUNK == 0 else BM
    nkc = r // kc

    def body(q_ref, k_ref, v_ref, o_ref, do_ref, lse_ref, dq_ref, dk_ref, dv_ref):
        i = pl.program_id(0)
        _acc_init(i, dk_ref, dv_ref)

        def heads(chunks):
            for kvh in range(ATTN_KV_HEADS):
                ks = slice(kvh * HD, (kvh + 1) * HD)
                for g in range(ATTN_GROUP):
                    j = kvh * ATTN_GROUP + g
                    cs = slice(j * HD, (j + 1) * HD)
                    qh, doh = q_ref[:, cs], do_ref[:, cs]
                    delta = jnp.sum(doh.astype(F32) * o_ref[:, cs].astype(F32), axis=-1, keepdims=True)
                    lse = lse_ref[:, j : j + 1]
                    dq = jnp.zeros((BM, HD), F32)
                    for rs in chunks:
                        kh, vh = k_ref[rs, ks], v_ref[rs, ks]
                        p = jnp.exp(_dot_nt(qh, kh) - lse)
                        ds = (p * (_dot_nt(doh, vh) - delta)).astype(BF16)
                        dq = dq + _dotf(ds, kh)
                        dk_ref[rs, ks] += _dot_tn(ds, qh)
                        dv_ref[rs, ks] += _dot_tn(p.astype(BF16), doh)
                    dq_ref[:, cs] = dq * ATTN_SCALE

        pl.when(i < nlb)(lambda: heads([slice(c * kc, (c + 1) * kc) for c in range(nkc)]))
        pl.when(i == nlb)(lambda: heads([slice(nlb * BM, r)]))

    return _rowcall(
        "attn_bwd", body, nblk, nlb,
        [(q, "row"), (k, "full"), (v, "full"), (o, "row"), (do, "row"), (lse, "row")],
        [((r, NQ), F32, "row"), ((r, NKV), F32, "acc"), ((r, NKV), F32, "acc")],
    )


def qkv_bwd(dq, dk, dv, raw, gains, cos_t, sin_t, w_qkv, x, dx1, mod, nw1, nblk, nlb, ctx_dx_zero):
    r = x.shape[0]

    def body(dq_ref, dk_ref, dv_ref, raw_ref, g_ref, c_ref, s_ref, w_ref, x_ref, dx1_ref, mod_ref, nw_ref,
             dx_ref, draw_ref, dg_ref, dnw_ref, dmod_ref):
        i = pl.program_id(0)
        _acc_init(i, dg_ref, dnw_ref, dmod_ref)
        cos, sin = c_ref[...], s_ref[...]
        for j in range(ATTN_HEADS + ATTN_KV_HEADS):
            isq = j < ATTN_HEADS
            cs = slice(j * HD, (j + 1) * HD)
            dr = dq_ref[:, cs] if isq else dk_ref[:, (j - ATTN_HEADS) * HD : (j - ATTN_HEADS + 1) * HD]
            dxn = dr * cos + _partner(dr * sin)
            xhat, rr = _head_norm(raw_ref[:, cs], None)
            gi = 0 if isq else 1
            dg_ref[gi : gi + 1, :] += jnp.sum(dxn * xhat, axis=0, keepdims=True)
            dxhat = dxn * g_ref[gi : gi + 1, :]
            draw_ref[:, cs] = (rr * (dxhat - xhat * jnp.mean(dxhat * xhat, axis=-1, keepdims=True))).astype(BF16)
        draw_ref[:, NQ + NKV :] = dv_ref[...].astype(BF16)
        dh = _dot_nt(draw_ref[...], w_ref[...])
        sc, nw = _mrow(mod_ref, SC1), nw_ref[...]
        _, n, xhat, rr = _norm_mod(x_ref[...], nw, _mrow(mod_ref, SH1), sc)
        dxn, dsh, dsc, dnw = _norm_mod_bwd(dh, n, xhat, rr, nw, sc)
        dres = dx1_ref[...]
        if ctx_dx_zero:
            dres = jnp.where(i == nlb, 0.0, dres)
        dx_ref[...] = dres + dxn
        s = i // nlb
        dmod_ref[s, SH1 : SH1 + 1, :] += dsh
        dmod_ref[s, SC1 : SC1 + 1, :] += dsc
        dnw_ref[0:1, :] += dnw

    return _rowcall(
        "qkv_bwd", body, nblk, nlb,
        [(dq, "row"), (dk, "row"), (dv, "row"), (raw, "row"), (gains, "full"), (cos_t, "row"), (sin_t, "row"),
         (w_qkv, "full"), (x, "row"), (dx1, "row"), (mod, "stream"), (nw1, "full")],
        [((r, D), F32, "row"), ((r, NQ + 2 * NKV), BF16, "row"), ((8, HD), F32, "acc"), ((8, D), F32, "acc"),
         ((2, 8, D), F32, "acc")],
    )


RET_KSCALE = RET_DK ** -0.5
RET_HP = RET_HEADS


def ret_in_fwd(x, mod, nw1, w_in, nblk, nlb):
    r = x.shape[0]

    def body(x_ref, mod_ref, nw_ref, w_ref, q_ref, k_ref, v_ref, g_ref, h_ref):
        h = _norm_mod(x_ref[...], nw_ref[...], _mrow(mod_ref, SH1), _mrow(mod_ref, SC1))[0].astype(BF16)
        h_ref[...] = h
        q_ref[...] = _dotf(h, w_ref[:, 0:RNQ]).astype(BF16)
        k_ref[...] = (_dotf(h, w_ref[:, RNQ : 2 * RNQ]) * RET_KSCALE).astype(BF16)
        v_ref[...] = _dotf(h, w_ref[:, 2 * RNQ : 2 * RNQ + RNV]).astype(BF16)
        g_ref[...] = _dotf(h, w_ref[:, 2 * RNQ + RNV :]).astype(BF16)

    return _rowcall(
        "ret_in_fwd", body, nblk, nlb,
        [(x, "row"), (mod, "stream"), (nw1, "full"), (w_in, "full")],
        [((r, RNQ), BF16, "row"), ((r, RNQ), BF16, "row"), ((r, RNV), BF16, "row"), ((r, RNV), BF16, "row"), ((r, D), BF16, "row")],
    )


def _log_sigmoid(x):
    return jnp.minimum(x, 0.0) - jnp.log(1.0 + jnp.exp(-jnp.abs(x)))


def _ret_decays(lg, reverse):
    c = BM
    i = lax.broadcasted_iota(jnp.int32, (c, c), 0)
    j = lax.broadcasted_iota(jnp.int32, (c, c), 1)
    diff = (j - i) if reverse else (i - j)
    ediff = jnp.maximum(diff, 0).astype(F32)
    dm = jnp.where(diff >= 0, jnp.exp(ediff * lg), 0.0)
    rr = lax.broadcasted_iota(jnp.int32, (c, 1), 0).astype(F32)
    eq = (c - rr) if reverse else (rr + 1.0)
    ek = rr if reverse else (c - 1.0 - rr)
    return dm, ediff, jnp.exp(eq * lg), eq, jnp.exp(ek * lg), ek, jnp.exp(c * lg)


def _ret_chunk_index(nlb):
    return (lambda s: jnp.where(s == 0, nlb, s - 1)), (lambda s: jnp.where(s == 0, nlb, nlb - s))


def ret_scan_fwd(q, k, v, logit_b, nlb):
    r = q.shape[0]
    nb = nlb + 1
    fidx, bidx = _ret_chunk_index(nlb)

    def body(qf, kf, vf, qb, kb, vb, lg_ref, of_ref, ob_ref, rf_ref, rb_ref, stf, stb):
        s = pl.program_id(1)

        @pl.when(s == 0)
        def _():
            stf[...] = jnp.zeros(stf.shape, F32)
            stb[...] = jnp.zeros(stb.shape, F32)

        for d, (q_ref, k_ref, v_ref, o_ref, rs_ref, st) in enumerate(((qf, kf, vf, of_ref, rf_ref, stf), (qb, kb, vb, ob_ref, rb_ref, stb))):
            for hp in range(RET_HP):
                kcs, vcs = slice(hp * RET_DK, (hp + 1) * RET_DK), slice(hp * RET_DV, (hp + 1) * RET_DV)
                lg = _log_sigmoid(lg_ref[d, hp])[0:1, 0:1]
                dm, _, qd, _, kd, _, gc = _ret_decays(lg, d == 1)
                qq, kk, vv, st0 = q_ref[:, kcs], k_ref[:, kcs], v_ref[:, vcs], st[hp]
                rs_ref[hp, 0] = st0
                a = _dot_nt(qq, kk) * dm
                o = _dotf(a.astype(BF16), vv) + _dotf(qq, st0.astype(BF16)) * qd
                o_ref[:, vcs] = jnp.where(s == 0, 0.0, o)
                st[hp] = st0 * gc + _dot_tn((kk.astype(F32) * kd).astype(BF16), vv)

    qspec = lambda f: pl.BlockSpec((BM, RET_HP * RET_DK), lambda h, s: (f(s), h))
    vspec = lambda f: pl.BlockSpec((BM, RET_HP * RET_DV), lambda h, s: (f(s), h))
    sspec = pl.BlockSpec((RET_HP, 1, RET_DK, RET_DV), lambda h, s: (h, s, 0, 0))
    return _pcall(
        body,
        name="ret_scan_fwd",
        grid=(RET_HEADS // RET_HP, nb),
        in_specs=[qspec(fidx), qspec(fidx), vspec(fidx), qspec(bidx), qspec(bidx), vspec(bidx),
                  pl.BlockSpec((2, RET_HP, 8, 128), lambda h, s: (0, h, 0, 0))],
        out_specs=[vspec(fidx), vspec(bidx), sspec, sspec],
        out_shape=[jax.ShapeDtypeStruct((r, RNV), F32), jax.ShapeDtypeStruct((r, RNV), F32),
                   jax.ShapeDtypeStruct((RET_HEADS, nb, RET_DK, RET_DV), F32), jax.ShapeDtypeStruct((RET_HEADS, nb, RET_DK, RET_DV), F32)],
        scratch_shapes=[pltpu.VMEM((RET_HP, RET_DK, RET_DV), F32), pltpu.VMEM((RET_HP, RET_DK, RET_DV), F32)],
        compiler_params=pltpu.CompilerParams(dimension_semantics=("parallel", "arbitrary"), vmem_limit_bytes=VMEM_LIMIT),
    )(q, k, v, q, k, v, logit_b)


def _group_norm(o):
    mu = jnp.mean(o, axis=-1, keepdims=True)
    oc = o - mu
    rstd = lax.rsqrt(jnp.mean(oc * oc, axis=-1, keepdims=True) + EPS)
    return oc * rstd, rstd


def ret_out_fwd(o_f, o_b, g, gnw, w_out, x, mod, nblk, nlb):
    r = x.shape[0]

    def body(of_ref, ob_ref, g_ref, gn_ref, w_ref, x_ref, mod_ref, x1_ref, y_ref, z_ref):
        for hh in range(RET_HEADS):
            cs = slice(hh * RET_DV, (hh + 1) * RET_DV)
            yhat, _ = _group_norm(of_ref[:, cs] + ob_ref[:, cs])
            gg = g_ref[:, cs].astype(F32)
            z_ref[:, cs] = (gg * _sigmoid(gg) * (yhat * gn_ref[0:1, cs])).astype(BF16)
        y = _dotf(z_ref[...], w_ref[...])
        y_ref[...] = y
        x1_ref[...] = x_ref[...] + _mrow(mod_ref, G1) * y

    return _rowcall(
        "ret_out_fwd", body, nblk, nlb,
        [(o_f, "row"), (o_b, "row"), (g, "row"), (gnw, "full"), (w_out, "full"), (x, "row"), (mod, "stream")],
        [((r, D), F32, "row"), ((r, D), F32, "row"), ((r, RNV), BF16, "row")],
    )


def ret_out_bwd(dx1, y, o_f, o_b, g, gnw, w_out, mod, dep, nblk, nlb):
    r = dx1.shape[0]

    def body(d_ref, y_ref, of_ref, ob_ref, g_ref, gn_ref, w_ref, mod_ref, dep_ref, dy_ref, do_ref, dg_ref, dgn_ref, dmod_ref, dz):
        i = pl.program_id(0)
        _acc_init(i, dgn_ref, dmod_ref)
        d = d_ref[...]
        dmod_ref[i // nlb, G1 : G1 + 1, :] += jnp.sum(d * y_ref[...], axis=0, keepdims=True)
        dy = (_mrow(mod_ref, G1) * d).astype(BF16)
        dy_ref[...] = dy
        dz[...] = _dot_nt(dy, w_ref[...])
        for hh in range(RET_HEADS):
            cs = slice(hh * RET_DV, (hh + 1) * RET_DV)
            yhat, rstd = _group_norm(of_ref[:, cs] + ob_ref[:, cs])
            gg = g_ref[:, cs].astype(F32)
            sg = _sigmoid(gg)
            gn = gn_ref[0:1, cs]
            dzz = dz[:, cs]
            dg_ref[:, cs] = (dzz * (yhat * gn) * (sg * (1.0 + gg * (1.0 - sg)))).astype(BF16)
            dyn = dzz * (gg * sg)
            dgn_ref[0:1, cs] += jnp.sum(dyn * yhat, axis=0, keepdims=True)
            dyh = dyn * gn
            do = rstd * (dyh - jnp.mean(dyh, axis=-1, keepdims=True) - yhat * jnp.mean(dyh * yhat, axis=-1, keepdims=True))
            do_ref[:, cs] = do.astype(BF16)

    return _rowcall(
        "ret_out_bwd", body, nblk, nlb,
        [(dx1, "row"), (y, "row"), (o_f, "row"), (o_b, "row"), (g, "row"), (gnw, "full"), (w_out, "full"), (mod, "stream"), (dep, "any")],
        [((r, D), BF16, "row"), ((r, RNV), BF16, "row"), ((r, RNV), BF16, "row"), ((8, RNV), F32, "acc"), ((2, 8, D), F32, "acc")],
        scratch=[pltpu.VMEM((BM, RNV), F32)],
    )


def ret_scan_bwd(q, k, v, do, st_f, st_b, logit_b, nlb):
    r = q.shape[0]
    nb = nlb + 1
    fidx, bidx = _ret_chunk_index(nlb)
    step = lambda t: nb - 1 - t

    def body(qf, kf, vf, dof, rf, qb, kb, vb, dob, rb, lg_ref,
             dqf, dkf, dvf, dqb, dkb, dvb, dlg_ref, drf, drb):
        t = pl.program_id(1)
        s = step(t)

        @pl.when(t == 0)
        def _():
            drf[...] = jnp.zeros(drf.shape, F32)
            drb[...] = jnp.zeros(drb.shape, F32)
            dlg_ref[...] = jnp.zeros(dlg_ref.shape, F32)

        dirs = ((qf, kf, vf, dof, rf, dqf, dkf, dvf, drf), (qb, kb, vb, dob, rb, dqb, dkb, dvb, drb))
        for d, (q_ref, k_ref, v_ref, do_ref, rs_ref, dq_ref, dk_ref, dv_ref, dr) in enumerate(dirs):
            for hp in range(RET_HP):
                kcs, vcs = slice(hp * RET_DK, (hp + 1) * RET_DK), slice(hp * RET_DV, (hp + 1) * RET_DV)
                lg = _log_sigmoid(lg_ref[d, hp])[0:1, 0:1]
                dm, ediff, qd, eq, kd, ek, gc = _ret_decays(lg, d == 1)
                qq, kk, vv = q_ref[:, kcs], k_ref[:, kcs], v_ref[:, vcs]
                dob16 = jnp.where(s == 0, jnp.zeros((), BF16), do_ref[:, vcs])
                do32 = dob16.astype(F32)
                st0 = rs_ref[hp, 0]
                st16 = st0.astype(BF16)
                dr0 = dr[hp]
                dr16 = dr0.astype(BF16)
                a = _dot_nt(qq, kk) * dm
                daf = _dot_nt(dob16, vv)
                ds = (daf * dm).astype(BF16)
                qr = _dotf(qq, st16)
                k32 = kk.astype(F32)
                kdec = (k32 * kd).astype(BF16)
                dv_ref[:, vcs] = (_dot_tn(a.astype(BF16), dob16) + _dotf(kdec, dr16)).astype(BF16)
                dq_ref[:, kcs] = (_dotf(ds, kk) + _dot_nt(dob16, st16) * qd).astype(BF16)
                vdr = _dot_nt(vv, dr16)
                dk_ref[:, kcs] = (_dot_tn(ds, qq) + vdr * kd).astype(BF16)
                tot = (jnp.sum(daf * a * ediff)
                       + jnp.sum(eq * qd * jnp.sum(do32 * qr, axis=-1, keepdims=True))
                       + jnp.sum(ek * kd * jnp.sum(k32 * vdr, axis=-1, keepdims=True))
                       + jnp.sum(BM * gc * jnp.sum(dr0 * st0, axis=-1, keepdims=True)))
                dlg_ref[d, hp] += tot
                dr[hp] = gc * dr0 + _dot_tn(qq, (do32 * qd).astype(BF16))

        @pl.when(t == nb - 1)
        def _():
            dlg_ref[...] = dlg_ref[...] * _sigmoid(-lg_ref[...])

    qspec = lambda f: pl.BlockSpec((BM, RET_HP * RET_DK), lambda h, t: (f(step(t)), h))
    vspec = lambda f: pl.BlockSpec((BM, RET_HP * RET_DV), lambda h, t: (f(step(t)), h))
    sspec = pl.BlockSpec((RET_HP, 1, RET_DK, RET_DV), lambda h, t: (h, step(t), 0, 0))
    lspec = pl.BlockSpec((2, RET_HP, 8, 128), lambda h, t: (0, h, 0, 0))
    sq, sv = jax.ShapeDtypeStruct((r, RNQ), BF16), jax.ShapeDtypeStruct((r, RNV), BF16)
    return _pcall(
        body,
        name="ret_scan_bwd",
        grid=(RET_HEADS // RET_HP, nb),
        in_specs=[qspec(fidx), qspec(fidx), vspec(fidx), vspec(fidx), sspec,
                  qspec(bidx), qspec(bidx), vspec(bidx), vspec(bidx), sspec, lspec],
        out_specs=[qspec(fidx), qspec(fidx), vspec(fidx), qspec(bidx), qspec(bidx), vspec(bidx), lspec],
        out_shape=[sq, sq, sv, sq, sq, sv, jax.ShapeDtypeStruct((2, RET_HEADS, 8, 128), F32)],
        scratch_shapes=[pltpu.VMEM((RET_HP, RET_DK, RET_DV), F32), pltpu.VMEM((RET_HP, RET_DK, RET_DV), F32)],
        compiler_params=pltpu.CompilerParams(dimension_semantics=("parallel", "arbitrary"), vmem_limit_bytes=VMEM_LIMIT),
    )(q, k, v, do, st_f, q, k, v, do, st_b, logit_b)


def ret_in_bwd(dqf, dkf, dvf, dqb, dkb, dvb, dgate, w_in, x, dx1, mod, nw1, nblk, nlb):
    r = x.shape[0]
    nin = 2 * RNQ + 2 * RNV

    def body(dqf_ref, dkf_ref, dvf_ref, dqb_ref, dkb_ref, dvb_ref, dg_ref, w_ref, x_ref, dx1_ref, mod_ref, nw_ref,
             dx_ref, din_ref, dnw_ref, dmod_ref):
        i = pl.program_id(0)
        _acc_init(i, dnw_ref, dmod_ref)
        is_ctx = i == nlb
        din_ref[:, 0:RNQ] = (dqf_ref[...].astype(F32) + dqb_ref[...].astype(F32)).astype(BF16)
        din_ref[:, RNQ : 2 * RNQ] = ((dkf_ref[...].astype(F32) + dkb_ref[...].astype(F32)) * RET_KSCALE).astype(BF16)
        din_ref[:, 2 * RNQ : 2 * RNQ + RNV] = (dvf_ref[...].astype(F32) + dvb_ref[...].astype(F32)).astype(BF16)
        din_ref[:, 2 * RNQ + RNV :] = jnp.where(is_ctx, jnp.zeros((), BF16), dg_ref[...])
        dh = _dot_nt(din_ref[...], w_ref[...])
        sc, nw = _mrow(mod_ref, SC1), nw_ref[...]
        _, n, xhat, rr = _norm_mod(x_ref[...], nw, _mrow(mod_ref, SH1), sc)
        dxn, dsh, dsc, dnw = _norm_mod_bwd(dh, n, xhat, rr, nw, sc)
        dx_ref[...] = jnp.where(is_ctx, 0.0, dx1_ref[...]) + dxn
        s = i // nlb
        dmod_ref[s, SH1 : SH1 + 1, :] += dsh
        dmod_ref[s, SC1 : SC1 + 1, :] += dsc
        dnw_ref[0:1, :] += dnw

    return _rowcall(
        "ret_in_bwd", body, nblk, nlb,
        [(dqf, "row"), (dkf, "row"), (dvf, "row"), (dqb, "row"), (dkb, "row"), (dvb, "row"), (dgate, "row"),
         (w_in, "full"), (x, "row"), (dx1, "row"), (mod, "stream"), (nw1, "full")],
        [((r, D), F32, "row"), ((r, nin), BF16, "row"), ((8, D), F32, "acc"), ((2, 8, D), F32, "acc")],
    )


def loss_head(xout, target, nlb):
    r = xout.shape[0]

    def body(x_ref, t_ref, dx_ref, l_ref):
        _acc_init(pl.program_id(0), l_ref)
        err = x_ref[...] - t_ref[...]
        dx_ref[...] = err * (1.0 / D)
        l_ref[...] += 0.5 * jnp.sum(jnp.mean(err * err, axis=-1, keepdims=True))

    return _rowcall(
        "loss_head", body, nlb, nlb,
        [(xout, "row"), (target, "row")],
        [((r, D), F32, "row"), ((8, 128), F32, "acc")],
    )


N_MIXERS = 3
POOL, ATTN, RET = range(3)


def _layer_plan(depth):
    plan = []
    for i in range(depth):
        kind = i % N_MIXERS
        ctx_out = any(k % N_MIXERS != POOL for k in range(i + 1, depth))
        plan.append((kind, i // N_MIXERS, ctx_out or kind != POOL, ctx_out))
    return plan


def local_step(xs, target, mods, w, nlb, depth, fetch, emit):
    nb = nlb + 1
    plan = _layer_plan(depth)
    saved = []
    x = xs
    for i, (kind, j, ctx_in, ctx_out) in enumerate(plan):
        nmix = nb if ctx_out else nlb
        mod, nw1, nw2 = mods[i], w["nw"][i, 0:1], w["nw"][i, 1:2]
        lw = fetch(i, MIX, x)
        sv = {"x": x, "lw": lw}
        if kind == POOL:
            x1, sv["ypre"] = pool_fwd(x, mod, nw1, lw["pool_w"], (w["pbs"], j), nmix, nlb)
        elif kind == ATTN:
            assert ctx_out
            sv["raw"], sv["q"], sv["k"], sv["v"], sv["h"] = qkv_fwd(x, mod, nw1, lw["attn_w_qkv"], w["gains"], w["cos"], w["sin"], nb, nlb)
            sv["o"], sv["lse"] = attn_fwd(sv["q"], sv["k"], sv["v"], nb, nlb)
            x1, sv["y"] = attn_out_fwd(sv["o"], lw["attn_w_o"], x, mod, nb, nlb)
        else:
            assert ctx_in and not ctx_out
            sv["q"], sv["k"], sv["v"], sv["g"], sv["h"] = ret_in_fwd(x, mod, nw1, lw["ret_w_in"], nb, nlb)
            sv["o_f"], sv["o_b"], sv["st_f"], sv["st_b"] = ret_scan_fwd(sv["q"], sv["k"], sv["v"], w["logit_b"], nlb)
            x1, sv["y"], sv["z"] = ret_out_fwd(sv["o_f"], sv["o_b"], sv["g"], w["gnw"], lw["ret_w_out"], x, mod, nlb, nlb)
        sv["x1"] = x1
        lw.update(fetch(i, FFN, x1))
        sv["u"], sv["h2"] = ffn_up(x1, mod, nw2, lw["ffn_w_up"], nmix, nlb)
        x, sv["f"], sv["uc"] = ffn_down(sv["u"], (w["cw"], i), lw["ffn_w_down"], x1, mod, nmix, nlb)
        saved.append(sv)

    dx, loss_tile = loss_head(x, target, nlb)
    g = {k: [None] * depth for k in ("dcw", "dnw1", "dnw2", "dmod")}
    dep = loss_tile
    for i in reversed(range(depth)):
        kind, j, ctx_in, ctx_out = plan[i]
        sv = saved[i]
        lw, big = sv["lw"], {}
        nmix = nb if ctx_out else nlb
        mod, nw1, nw2 = mods[i], w["nw"][i, 0:1], w["nw"][i, 1:2]
        duc, big["ffn_w_down"], dcb, dmod = ffn_bwd1(dx, sv["f"], sv["uc"], lw["ffn_w_down"], mod, dep, nmix, nlb)
        du, dx1, dcw, g["dnw2"][i], dm = ffn_bwd3(duc, sv["u"], (w["cw"], i), lw["ffn_w_up"], sv["x1"], dx, mod, nw2, nmix, nlb)
        g["dcw"][i] = dcw + dcb
        dmod = dmod + dm
        big["ffn_w_up"] = matmul_tn(sv["h2"], du, nmix)
        dep, big = emit(i, FFN, big), {}
        if kind == POOL:
            dx, dpw, dpbs, g["dnw1"][i], dm = pool_bwd(dx1, sv["x"], sv["ypre"], mod, nw1, lw["pool_w"], (w["pbs"], j), dep, nmix, nlb, i == 0)
            big["pool_w"] = dpw.astype(BF16)
            g.setdefault("dpbs", {})[j] = dpbs
        elif kind == ATTN:
            dy, do, dm1 = mix_out_bwd("attn_out_bwd", dx1, sv["y"], lw["attn_w_o"], mod, dep, nb, nlb)
            big["attn_w_o"] = matmul_tn(sv["o"], dy, nb)
            dq, dk, dv = attn_bwd(sv["q"], sv["k"], sv["v"], sv["o"], do, sv["lse"], nb, nlb)
            dx, draw, g["dgains"], g["dnw1"][i], dm = qkv_bwd(
                dq, dk, dv, sv["raw"], w["gains"], w["cos"], w["sin"], lw["attn_w_qkv"], sv["x"], dx1, mod, nw1, nb, nlb, False)
            big["attn_w_qkv"] = matmul_tn(sv["h"], draw, nb)
            dm = dm + dm1
        else:
            dy, do, dgate, g["dgnw"], dm1 = ret_out_bwd(dx1, sv["y"], sv["o_f"], sv["o_b"], sv["g"], w["gnw"], lw["ret_w_out"], mod, dep, nlb, nlb)
            big["ret_w_out"] = matmul_tn(sv["z"], dy, nlb)
            dqf, dkf, dvf, dqb, dkb, dvb, g["dlogit"] = ret_scan_bwd(sv["q"], sv["k"], sv["v"], do, sv["st_f"], sv["st_b"], w["logit_b"], nlb)
            dx, din, g["dnw1"][i], dm = ret_in_bwd(dqf, dkf, dvf, dqb, dkb, dvb, dgate, lw["ret_w_in"], sv["x"], dx1, mod, nw1, nb, nlb)
            big["ret_w_in"] = matmul_tn(sv["h"], din, nb)
            dm = dm + dm1
        g["dmod"][i] = dmod + dm
        dep = emit(i, MIX, big)
    return loss_tile, dx, g


MESH_ID = pl.DeviceIdType.MESH
CHIP_FLIPS = ((1, 0), (0, 1), (1, 1))


def _pos():
    return lax.axis_index("x"), lax.axis_index("y"), lax.axis_index("c")


def _flip(v, b):
    return 1 - v if b else v


def small_all_gather(name, x):
    rows, n = x.shape

    def body(x_ref, out_ref, send_sems, recv_sems, local_sem):
        mx, my, mc = _pos()
        me = 4 * mx + 2 * my + mc
        mine = pltpu.make_async_copy(x_ref, out_ref.at[me], local_sem)
        mine.start()
        sends, peers = [], []
        for kk in range(1, N_DEV):
            peer = (_flip(mx, (kk >> 2) & 1), _flip(my, (kk >> 1) & 1), _flip(mc, kk & 1))
            cp = pltpu.make_async_remote_copy(src_ref=x_ref, dst_ref=out_ref.at[me], send_sem=send_sems.at[kk - 1],
                                              recv_sem=recv_sems.at[kk - 1], device_id=peer, device_id_type=MESH_ID)
            cp.start()
            sends.append(cp)
            peers.append(peer)
        for kk, peer in enumerate(peers):
            pidx = 4 * peer[0] + 2 * peer[1] + peer[2]
            pltpu.make_async_remote_copy(src_ref=x_ref, dst_ref=out_ref.at[pidx], send_sem=send_sems.at[kk],
                                         recv_sem=recv_sems.at[kk], device_id=peer, device_id_type=MESH_ID).wait_recv()
        for cp in sends:
            cp.wait_send()
        mine.wait()

    return _pcall(
        body,
        name=name,
        out_shape=jax.ShapeDtypeStruct((N_DEV, rows, n), x.dtype),
        in_specs=[pl.BlockSpec(memory_space=pltpu.VMEM)],
        out_specs=pl.BlockSpec(memory_space=pltpu.VMEM),
        scratch_shapes=[pltpu.SemaphoreType.DMA((N_DEV - 1,)), pltpu.SemaphoreType.DMA((N_DEV - 1,)), pltpu.SemaphoreType.DMA],
        compiler_params=pltpu.CompilerParams(vmem_limit_bytes=VMEM_LIMIT),
    )(x)


def _hbm_exchange(name, ins, out_shapes, plan):
    n_in = len(ins)
    probe_local, probe_remote = plan([None] * n_in, [None] * len(out_shapes), probe=True)

    def body(*refs):
        in_refs, out_refs = refs[:n_in], refs[n_in : n_in + len(out_shapes)]
        send_sems, recv_sems, local_sems = refs[n_in + len(out_shapes) :]
        local, remote = plan(in_refs, out_refs, probe=False)
        lcs = [pltpu.make_async_copy(s, d, local_sems.at[k]) for k, (s, d) in enumerate(local)]
        for cp in lcs:
            cp.start()
        rcs = []
        for k, (s, d, peer, _) in enumerate(remote):
            cp = pltpu.make_async_remote_copy(src_ref=s, dst_ref=d, send_sem=send_sems.at[k], recv_sem=recv_sems.at[k],
                                              device_id=peer, device_id_type=MESH_ID)
            cp.start()
            rcs.append(cp)
        for k, (s, _, peer, here) in enumerate(remote):
            pltpu.make_async_remote_copy(src_ref=s, dst_ref=here, send_sem=send_sems.at[k], recv_sem=recv_sems.at[k],
                                         device_id=peer, device_id_type=MESH_ID).wait_recv()
        for cp in rcs:
            cp.wait_send()
        for cp in lcs:
            cp.wait()

    return _pcall(
        body,
        name=name,
        out_shape=list(out_shapes),
        in_specs=[pl.BlockSpec(memory_space=pl.ANY)] * n_in,
        out_specs=[pl.BlockSpec(memory_space=pl.ANY)] * len(out_shapes),
        scratch_shapes=[pltpu.SemaphoreType.DMA((max(probe_remote, 1),)), pltpu.SemaphoreType.DMA((max(probe_remote, 1),)),
                        pltpu.SemaphoreType.DMA((max(probe_local, 1),))],
    )(*ins)


def _at_axis(ref, axis, start, size):
    return ref.at[tuple(pl.ds(start, size) if a == axis else slice(None) for a in range(len(ref.shape)))]


HBM_SPEC = pl.BlockSpec(memory_space=pltpu.HBM)
SEM_SPEC = pl.BlockSpec(memory_space=pltpu.SEMAPHORE)
SIDE_EFFECT = pltpu.SideEffectType.DATAFLOW_SIDE_EFFECTING


def _in_hbm(a):
    return pltpu.with_memory_space_constraint(a, pltpu.HBM)


def _copies_start(name, bufs, counts, plan, after=None):
    n, ng = len(bufs), len(counts)
    extra = [] if after is None else [after]
    n_in = n + len(extra)

    def body(*refs):
        sems = refs[n_in : n_in + 2 * ng]
        token = refs[n_in + 2 * ng + n]
        for gi, copies in enumerate(plan(refs[:n])):
            for k, (s, d, peer) in enumerate(copies):
                pltpu.make_async_remote_copy(src_ref=s, dst_ref=d, send_sem=sems[2 * gi].at[k], recv_sem=sems[2 * gi + 1].at[k],
                                             device_id=peer, device_id_type=MESH_ID).start()
        token[...] = jnp.zeros(token.shape, token.dtype)

    out = _pcall(
        body,
        name=name,
        out_shape=tuple(pltpu.SemaphoreType.DMA((c,)) for c in counts for _ in range(2))
        + tuple(pltpu.HBM(b.shape, b.dtype) for b in bufs) + (jax.ShapeDtypeStruct((8, 128), F32),),
        in_specs=(HBM_SPEC,) * n + (pl.BlockSpec(memory_space=pl.ANY),) * len(extra),
        out_specs=(SEM_SPEC,) * (2 * ng) + (HBM_SPEC,) * n + (pl.BlockSpec(memory_space=pltpu.VMEM),),
        input_output_aliases={i: 2 * ng + i for i in range(n)},
        compiler_params=pltpu.CompilerParams(has_side_effects=SIDE_EFFECT),
    )(*[_in_hbm(b) for b in bufs], *extra)
    return [(out[2 * g], out[2 * g + 1]) for g in range(ng)], list(out[2 * ng : 2 * ng + n]), out[2 * ng + n]


def _copies_wait(name, sems, bufs, plan, after):
    n, ng = len(bufs), len(sems)

    def body(*refs):
        for gi, copies in enumerate(plan(refs[:n])):
            for k, (s, d, peer) in enumerate(copies):
                cp = pltpu.make_async_remote_copy(src_ref=s, dst_ref=d, send_sem=refs[n + 2 * gi].at[k], recv_sem=refs[n + 2 * gi + 1].at[k],
                                                  device_id=peer, device_id_type=MESH_ID)
                cp.wait_send()
                cp.wait_recv()

    out = _pcall(
        body,
        name=name,
        out_shape=tuple(pltpu.HBM(b.shape, b.dtype) for b in bufs),
        in_specs=(HBM_SPEC,) * n + (SEM_SPEC,) * (2 * ng) + (pl.BlockSpec(memory_space=pl.ANY),),
        out_specs=(HBM_SPEC,) * n,
        input_output_aliases={i: i for i in range(n)},
        compiler_params=pltpu.CompilerParams(has_side_effects=SIDE_EFFECT),
    )(*bufs, *[s for pair in sems for s in pair], after)
    return list(out)


def _matrix_groups(depth):
    out = []
    for i, (kind, j, _, _) in enumerate(_layer_plan(depth)):
        out.append(([("pool_w", j, 1)], [("attn_w_qkv", j, 1), ("attn_w_o", j, 0)], [("ret_w_in", j, 1), ("ret_w_out", j, 0)])[kind])
        out.append([("ffn_w_up", i, 1), ("ffn_w_down", i, 0)])
    return out


MIX, FFN = 0, 1


def _peers(mx, my, mc):
    out = []
    for fx, fy in CHIP_FLIPS:
        px, py = _flip(mx, fx), _flip(my, fy)
        out.append(((px, py, mc), 2 * px + py))
    return out


def full_buffers(shards, names, layers):
    out = []
    for name, _, axis in [e for layer in layers for e in layer]:
        shp = list(shards[names.index(name)].shape[1:])
        shp[axis] *= N_CHIP
        out.append(lax.empty(tuple(shp), BF16))
    return out


def _gather_plan(names, layers, group, n_shards, here):
    def plan(refs):
        mx, my, mc = _pos()
        s_refs, f_refs = refs[:n_shards], refs[n_shards:]
        groups, k = [], 0
        for gi, layer in enumerate(layers):
            if group is not None and gi != group:
                continue
            copies = []
            for name, idx, axis in layer:
                src = s_refs[names.index(name)].at[idx]
                n = src.shape[axis]
                for peer, pchip in _peers(mx, my, mc) + [((mx, my, 1 - mc), 2 * mx + my)]:
                    at = pchip if here else 2 * mx + my
                    copies.append((src, _at_axis(f_refs[k], axis, at * n, n), peer))
                k += 1
            groups.append(copies)
        return groups

    return plan


def gather_start(shards, names, layers, fulls, after):
    counts = [len(layer) * (len(CHIP_FLIPS) + 1) for layer in layers]
    sems, bufs, _ = _copies_start("gather_start", list(shards) + list(fulls), counts,
                                  _gather_plan(names, layers, None, len(shards), False), after)
    return sems, bufs[: len(shards)], bufs[len(shards) :]


def gather_wait(g, sems_g, shards, names, layers, fulls_g, after):
    bufs = _copies_wait(f"gather_wait_{g}", [sems_g], list(shards) + list(fulls_g), _gather_plan(names, layers, g, len(shards), True), after)
    return bufs[: len(shards)], bufs[len(shards) :]


def _scatter_plan(layer_entries, n_grads, land_of):
    def plan(refs):
        mx, my, mc = _pos()
        groups, k = [], 0
        for layer in layer_entries:
            copies = []
            for name, idx, axis in layer:
                gref, land = refs[k], refs[n_grads + land_of(k, name)]
                n = gref.shape[axis] // N_CHIP
                for slot, (peer, pchip) in enumerate(_peers(mx, my, mc)):
                    copies.append((_at_axis(gref, axis, pchip * n, n), land.at[slot, idx], peer))
                k += 1
            groups.append(copies)
        return groups

    return plan


def scatter_start(i, layer, grads, lands):
    sems, bufs, token = _copies_start(f"scatter_start_{i}", list(grads) + list(lands), [len(layer) * len(CHIP_FLIPS)],
                                      _scatter_plan([layer], len(grads), lambda k, name: k))
    return sems[0], bufs[: len(grads)], bufs[len(grads) :], token


def scatter_wait(sems, layers, grads, names, lands, after):
    bufs = _copies_wait("scatter_wait", sems, list(grads) + list(lands),
                        _scatter_plan(layers, len(grads), lambda k, name: names.index(name)), after)
    return bufs[: len(grads)], bufs[len(grads) :]


def sibling_swap(parts):
    def plan(in_refs, out_refs, probe):
        if probe:
            return 0, len(parts)
        mx, my, mc = _pos()
        return [], [(s, o, (mx, my, 1 - mc), o) for s, o in zip(in_refs, out_refs)]

    return _hbm_exchange("sibling_swap", parts, [jax.ShapeDtypeStruct(p.shape, p.dtype) for p in parts], plan)


EW_ROWS = 256


def _ew_call(name, fn, ins, n_out):
    rows, cols = ins[0].shape[-2:]
    tr = EW_ROWS if rows % EW_ROWS == 0 else rows

    def body(*refs):
        outs = fn(*[r[...] for r in refs[: len(ins)]])
        for o_ref, o in zip(refs[len(ins) :], outs):
            o_ref[...] = o

    def spec(a):
        if a.ndim == 3:
            return pl.BlockSpec((a.shape[0], tr, cols), lambda i: (0, i, 0))
        return pl.BlockSpec((tr, cols), lambda i: (i, 0))

    return _pcall(
        body,
        name=name,
        grid=(rows // tr,),
        in_specs=[spec(a) for a in ins],
        out_specs=[pl.BlockSpec((tr, cols), lambda i: (i, 0))] * n_out,
        out_shape=[jax.ShapeDtypeStruct((rows, cols), F32)] * n_out,
        compiler_params=pltpu.CompilerParams(dimension_semantics=("parallel",), vmem_limit_bytes=VMEM_LIMIT),
    )(*ins)


def _adamw(w, g, m, v):
    m = ADAM_B1 * m + (1.0 - ADAM_B1) * g
    v = ADAM_B2 * v + (1.0 - ADAM_B2) * (g * g)
    m_hat = m / (1.0 - ADAM_B1 ** ADAM_STEP)
    v_hat = v / (1.0 - ADAM_B2 ** ADAM_STEP)
    return -ADAM_LR * (m_hat / (jnp.sqrt(v_hat) + ADAM_EPS) + ADAM_WD * w), m, v


def sum_slots(name, own, landing):
    def fn(o, l):
        acc = o.astype(F32)
        for k in range(l.shape[0]):
            acc = acc + l[k].astype(F32)
        return (acc,)

    return _ew_call(name, fn, [own, landing], 1)[0]


def adamw_pair(name, w, m, v, p, ps):
    def fn(w, m, v, p, ps):
        g = p + ps
        return (g,) + _adamw(w, g, m, v)

    return _ew_call(name, fn, [w, m, v, p, ps], 4)


def adamw_one(name, w, m, v, g):
    return _ew_call(name, lambda w, m, v, g: _adamw(w, g, m, v), [w, m, v, g], 3)


def reduce_devices(name, x):
    def fn(a):
        acc = a[0]
        for k in range(1, a.shape[0]):
            acc = acc + a[k]
        return (acc,)

    return _ew_call(name, fn, [x], 1)[0]


ADA_ROWS = 16
ADA_CTX = N_DEV


def ada_fwd(s9, ada_w, ada_b):
    depth, _, n = ada_w.shape

    def body(s_ref, w_ref, b_ref, o_ref):
        s = s_ref[...]
        o_ref[...] = _dotf((s * _sigmoid(s)).astype(BF16), w_ref[...].astype(BF16)) + b_ref[...]

    return _pcall(
        body,
        name="ada_fwd",
        grid=(depth,),
        in_specs=[pl.BlockSpec((ADA_ROWS, D), lambda i: (0, 0)), pl.BlockSpec((None, D, n), lambda i: (i, 0, 0)),
                  pl.BlockSpec((None, 1, n), lambda i: (i, 0, 0))],
        out_specs=pl.BlockSpec((None, ADA_ROWS, n), lambda i: (i, 0, 0)),
        out_shape=jax.ShapeDtypeStruct((depth, ADA_ROWS, n), F32),
        compiler_params=pltpu.CompilerParams(dimension_semantics=("arbitrary",), vmem_limit_bytes=VMEM_LIMIT),
    )(s9, ada_w, ada_b)


def ada_bwd(s9, dm, ada_w):
    depth, _, n = ada_w.shape

    def body(s_ref, dm_ref, w_ref, gw_ref, ds_ref):
        _acc_init(pl.program_id(0), ds_ref)
        s = s_ref[...]
        dmb = dm_ref[...].astype(BF16)
        gw_ref[...] = _dot_tn((s * _sigmoid(s)).astype(BF16), dmb)
        ds_ref[...] += _dot_nt(dmb, w_ref[...].astype(BF16))

    return _pcall(
        body,
        name="ada_bwd",
        grid=(depth,),
        in_specs=[pl.BlockSpec((ADA_ROWS, D), lambda i: (0, 0)), pl.BlockSpec((None, ADA_ROWS, n), lambda i: (i, 0, 0)),
                  pl.BlockSpec((None, D, n), lambda i: (i, 0, 0))],
        out_specs=[pl.BlockSpec((None, D, n), lambda i: (i, 0, 0)), pl.BlockSpec((ADA_ROWS, D), lambda i: (0, 0))],
        out_shape=[jax.ShapeDtypeStruct((depth, D, n), F32), jax.ShapeDtypeStruct((ADA_ROWS, D), F32)],
        compiler_params=pltpu.CompilerParams(dimension_semantics=("arbitrary",), vmem_limit_bytes=VMEM_LIMIT),
    )(s9, dm, ada_w)


def cctx_grad(parts, c_ctx):
    def body(p_ref, c_ref, o_ref):
        acc = p_ref[0, ADA_CTX : ADA_CTX + 1, :]
        for chip in range(1, N_CHIP):
            acc = acc + p_ref[2 * chip, ADA_CTX : ADA_CTX + 1, :]
        c = c_ref[...]
        sg = _sigmoid(c)
        o_ref[...] = acc * (sg * (1.0 + c * (1.0 - sg)))

    return _pcall(body, name="cctx_grad", out_shape=jax.ShapeDtypeStruct((1, D), F32))(parts, c_ctx)


def _pack(arrs):
    flat = jnp.concatenate([a.astype(F32).reshape(-1) for a in arrs])
    rows = -(-flat.shape[0] // (8 * D)) * 8
    return jnp.pad(flat, (0, rows * D - flat.shape[0])).reshape(rows, D)


def _unpack(slab, shapes):
    lead = slab.shape[:-2]
    flat = slab.reshape(lead + (-1,))
    out, off = [], 0
    for shp in shapes:
        n = 1
        for d in shp:
            n *= d
        out.append(flat[..., off : off + n].reshape(lead + tuple(shp)))
        off += n
    return out


def _unshard(per_dev, axis):
    return jnp.concatenate([per_dev[2 * chip] for chip in range(N_CHIP)], axis=axis)


BIG = (("pool_w", 1), ("attn_w_qkv", 1), ("attn_w_o", 0), ("ret_w_in", 1), ("ret_w_out", 0), ("ffn_w_up", 1), ("ffn_w_down", 0))
WEIGHTS = ("c_ctx", "ada_w", "ada_b", "norm_w", "pool_w", "pool_b", "pool_scale", "attn_w_qkv", "attn_q_gain", "attn_k_gain",
           "attn_w_o", "ret_w_in", "ret_decay_logit", "ret_gn_w", "ret_w_out", "ffn_w_up", "ffn_conv_w", "ffn_conv_b", "ffn_w_down")
SMALL = tuple(n for n in WEIGHTS if n != "ada_w" and n not in dict(BIG))
SMALL_SHARD_AXIS = {"norm_w": 2, "pool_b": 1, "pool_scale": 1, "ret_gn_w": 1, "ffn_conv_w": 2}


def kernel(x, c, ctx, c_ctx, ada_w, ada_b, norm_w, pool_w, pool_b, pool_scale, attn_w_qkv, attn_q_gain, attn_k_gain, attn_w_o, ret_w_in, ret_decay_logit, ret_gn_w, ret_w_out, ffn_w_up, ffn_conv_w, ffn_conv_b, ffn_w_down, loss_target, m_c_ctx, m_ada_w, m_ada_b, m_norm_w, m_pool_w, m_pool_b, m_pool_scale, m_attn_w_qkv, m_attn_q_gain, m_attn_k_gain, m_attn_w_o, m_ret_w_in, m_ret_decay_logit, m_ret_gn_w, m_ret_w_out, m_ffn_w_up, m_ffn_conv_w, m_ffn_conv_b, m_ffn_w_down, v_c_ctx, v_ada_w, v_ada_b, v_norm_w, v_pool_w, v_pool_b, v_pool_scale, v_attn_w_qkv, v_attn_q_gain, v_attn_k_gain, v_attn_w_o, v_ret_w_in, v_ret_decay_logit, v_ret_gn_w, v_ret_w_out, v_ffn_w_up, v_ffn_conv_w, v_ffn_conv_b, v_ffn_w_down):
    P = dict(zip(WEIGHTS, (c_ctx, ada_w, ada_b, norm_w, pool_w, pool_b, pool_scale, attn_w_qkv, attn_q_gain, attn_k_gain, attn_w_o,
                           ret_w_in, ret_decay_logit, ret_gn_w, ret_w_out, ffn_w_up, ffn_conv_w, ffn_conv_b, ffn_w_down)))
    M = dict(zip(WEIGHTS, (m_c_ctx, m_ada_w, m_ada_b, m_norm_w, m_pool_w, m_pool_b, m_pool_scale, m_attn_w_qkv, m_attn_q_gain,
                           m_attn_k_gain, m_attn_w_o, m_ret_w_in, m_ret_decay_logit, m_ret_gn_w, m_ret_w_out, m_ffn_w_up,
                           m_ffn_conv_w, m_ffn_conv_b, m_ffn_w_down)))
    V = dict(zip(WEIGHTS, (v_c_ctx, v_ada_w, v_ada_b, v_norm_w, v_pool_w, v_pool_b, v_pool_scale, v_attn_w_qkv, v_attn_q_gain,
                           v_attn_k_gain, v_attn_w_o, v_ret_w_in, v_ret_decay_logit, v_ret_gn_w, v_ret_w_out, v_ffn_w_up,
                           v_ffn_conv_w, v_ffn_conv_b, v_ffn_w_down)))
    depth, s, l = ada_w.shape[0], x.shape[1], ctx.shape[1]
    assert l == BM and s % BM == 0 and s % GRID_W == 0
    nlb = s // BM
    n_pool = pool_w.shape[0]
    mx, my, mc = _pos()
    chip, dev = 2 * mx + my, 4 * mx + 2 * my + mc
    nada = ada_w.shape[2]

    sharded = [n for n in SMALL if n in SMALL_SHARD_AXIS]
    got = small_all_gather("gather_small", _pack([c[0]] + [P[n] for n in sharded]))
    got = _unpack(got, [(D,)] + [P[n].shape for n in sharded])
    c_all = got[0]
    full = {n: _unshard(g_, SMALL_SHARD_AXIS[n]) for n, g_ in zip(sharded, got[1:])}

    s9 = jnp.concatenate([c_all, c_ctx[None, :], jnp.zeros((ADA_ROWS - N_DEV - 1, D), F32)], axis=0)
    ada_b_mine = lax.dynamic_slice_in_dim(ada_b, chip * nada, nada, axis=1)[:, None, :]
    mod_part = ada_fwd(s9, ada_w, ada_b_mine)
    mod_all = _unshard(small_all_gather("gather_mod", mod_part.reshape(depth * ADA_ROWS, nada)), 1).reshape(depth, ADA_ROWS, 6, D)
    mod_mine = lax.dynamic_index_in_dim(mod_all, dev, axis=1, keepdims=False)
    mods_all = jnp.pad(jnp.stack([mod_mine, mod_all[:, ADA_CTX]], axis=1), ((0, 0), (0, 0), (0, 2), (0, 0)))
    mods = [mods_all[i] for i in range(depth)]

    names = [n for n, _ in BIG]
    layers = _matrix_groups(depth)
    shards = [P[n].astype(BF16) for n in names]
    gsems, shards, fulls = gather_start(shards, names, layers, full_buffers(shards, names, layers), mods_all)
    first = [sum(len(layer) for layer in layers[:g_]) for g_ in range(len(layers) + 1)]
    flight = {"shards": shards}

    def fetch(i, part, after):
        g_ = 2 * i + part
        flight["shards"], mats = gather_wait(g_, gsems[g_], flight["shards"], names, layers, fulls[first[g_] : first[g_ + 1]], after)
        return {name: m for (name, _, _), m in zip(layers[g_], mats)}

    lands = {n: lax.empty((len(CHIP_FLIPS),) + P[n].shape, BF16) for n in names}
    sent = {}

    def emit(i, part, big):
        g_ = 2 * i + part
        lnames = [name for name, _, _ in layers[g_]]
        sems, gl, ll, token = scatter_start(g_, layers[g_], [big[n] for n in lnames], [lands[n] for n in lnames])
        lands.update(zip(lnames, ll))
        sent[g_] = (sems, gl)
        return token

    w = {
        "nw": full["norm_w"],
        "pbs": jnp.concatenate([full["pool_b"][:, None], full["pool_scale"][:, None], jnp.zeros((n_pool, 6, D), F32)], axis=1),
        "gains": jnp.concatenate([attn_q_gain, attn_k_gain, jnp.zeros((6, HD), F32)], axis=0),
        "gnw": full["ret_gn_w"],
        "logit_b": jnp.broadcast_to(ret_decay_logit[0][:, :, None, None], (2, RET_HEADS, 8, 128)),
        "cw": jnp.concatenate([full["ffn_conv_w"], ffn_conv_b[:, None, :], jnp.zeros((depth, 4, 2 * D_FF), F32)], axis=1),
    }
    w["cos"], w["sin"] = rope_tables(s, l)

    xs = jnp.concatenate([x[0], ctx[0]], axis=0)
    loss_tile, dxs, g = local_step(xs, loss_target[0], mods, w, nlb, depth, fetch, emit)
    loss = lax.psum(loss_tile[0, 0], MESH_AXES)
    grad_x = dxs[None]

    small_shapes = [(depth, 2, 8, D), (depth, 2, D), (n_pool, 2, D), (2, HD), (2, RET_HEADS), (RNV,), (depth, 4, 2 * D_FF)]
    slab = _pack([
        jnp.stack(g["dmod"]),
        jnp.stack([jnp.stack([g["dnw1"][i][0], g["dnw2"][i][0]]) for i in range(depth)]),
        jnp.stack([g["dpbs"][j][0:2] for j in range(n_pool)]),
        g["dgains"][0:2], g["dlogit"][:, :, 0, 0], g["dgnw"][0], jnp.stack([g["dcw"][i][0:4] for i in range(depth)]),
    ])
    slabs = small_all_gather("gather_small_grads", slab)
    dmod_dev = _unpack(slabs, small_shapes[:1])[0]
    t_dmod, t_nw, t_pbs, t_gains, t_logit, t_gnw, t_cw = _unpack(reduce_devices("reduce_small_grads", slabs), small_shapes)

    def cols(a):
        return lax.dynamic_slice_in_dim(a, chip * nada, nada, axis=a.ndim - 1)

    dm_lat = jnp.swapaxes(cols(dmod_dev[:, :, 0, :6].reshape(N_DEV, depth, 6 * D)), 0, 1)
    dm_ctx = cols(t_dmod[:, 1, :6].reshape(depth, 1, 6 * D))
    dm = jnp.concatenate([dm_lat, dm_ctx, jnp.zeros((depth, ADA_ROWS - N_DEV - 1, nada), F32)], axis=1)
    g_ada_w, ds9 = ada_bwd(s9, dm, ada_w)
    g_c_ctx = cctx_grad(small_all_gather("gather_dcctx", ds9), c_ctx[None, :])[0]

    def mine(a, name):
        n = P[name].shape[SMALL_SHARD_AXIS[name]]
        return lax.dynamic_slice_in_dim(a, chip * n, n, axis=SMALL_SHARD_AXIS[name])

    G = {
        "c_ctx": g_c_ctx,
        "ada_b": (t_dmod[:, 0, :6] + t_dmod[:, 1, :6]).reshape(depth, 6 * D),
        "norm_w": mine(t_nw, "norm_w"),
        "pool_b": mine(t_pbs[:, 0], "pool_b"), "pool_scale": mine(t_pbs[:, 1], "pool_scale"),
        "attn_q_gain": t_gains[0:1], "attn_k_gain": t_gains[1:2],
        "ret_decay_logit": t_logit[None], "ret_gn_w": mine(t_gnw[None], "ret_gn_w"),
        "ffn_conv_w": mine(t_cw[:, 0:3], "ffn_conv_w"), "ffn_conv_b": t_cw[:, 3],
    }
    sw, sg, sm, sv = (_pack([d_[n] for n in SMALL]) for d_ in (P, G, M, V))
    outs = adamw_one("adamw_small", sw, sm, sv, sg)
    D_, NM, NV = ({n: a for n, a in zip(SMALL, _unpack(o, [P[n].shape for n in SMALL]))} for o in outs)

    flat2 = lambda a: a.reshape(-1, a.shape[-1])
    G["ada_w"] = g_ada_w
    o3 = adamw_one("adamw_ada", flat2(ada_w), flat2(M["ada_w"]), flat2(V["ada_w"]), flat2(g_ada_w))
    D_["ada_w"], NM["ada_w"], NV["ada_w"] = (o.reshape(ada_w.shape) for o in o3)

    sent_grads, landed = scatter_wait([sent[g_][0] for g_ in range(len(layers))], layers,
                                      [a for g_ in range(len(layers)) for a in sent[g_][1]], names, [lands[n] for n in names], o3[0])
    own = {n: [None] * P[n].shape[0] for n in names}
    for (name, idx, axis), a in zip([e for layer in layers for e in layer], sent_grads):
        n_ = a.shape[axis] // N_CHIP
        own[name][idx] = lax.dynamic_slice_in_dim(a, chip * n_, n_, axis=axis)
    partial = [sum_slots("sum_" + n, jnp.stack(own[n]).reshape(-1, lnd.shape[-1]), lnd.reshape(len(CHIP_FLIPS), -1, lnd.shape[-1]))
               for n, lnd in zip(names, landed)]
    theirs = sibling_swap(partial)
    for (n, _), p, ps in zip(BIG, partial, theirs):
        o4 = adamw_pair("adamw_" + n, flat2(P[n]), flat2(M[n]), flat2(V[n]), p, ps)
        G[n], D_[n], NM[n], NV[n] = (o.reshape(P[n].shape) for o in o4)

    return (loss, grad_x, *[G[n] for n in WEIGHTS], *[D_[n] for n in WEIGHTS], *[NM[n] for n in WEIGHTS], *[NV[n] for n in WEIGHTS])
```

```python
import functools

import jax
import jax.numpy as jnp
from jax import lax
from jax.experimental import pallas as pl
from jax.experimental.pallas import tpu as pltpu

F32 = jnp.float32
BF16 = jnp.bfloat16

D = 1024
BM = 256
EPS = 1e-6
POOL_WINDOWS = (2, 4, 8, 16)
POOL_GROUP = D // 4
ATTN_HEADS = 8
ATTN_KV_HEADS = 2
HD = D // ATTN_HEADS
ATTN_GROUP = ATTN_HEADS // ATTN_KV_HEADS
NQ = ATTN_HEADS * HD
NKV = ATTN_KV_HEADS * HD
GRID_W = 64
ROPE_THETA = 10000.0
RET_HEADS = 4
RET_DK = D // RET_HEADS
RET_DV = 2 * D // RET_HEADS
RNQ = RET_HEADS * RET_DK
RNV = RET_HEADS * RET_DV
D_FF = 2816
FF_CHUNK = 256
ADAM_LR, ADAM_B1, ADAM_B2, ADAM_EPS, ADAM_WD, ADAM_STEP = 0.001, 0.9, 0.999, 1e-08, 0.01, 10
HALO_F32 = 8
HALO_BF16 = 16
VMEM_LIMIT = 60 * 1024 * 1024

MESH_AXES = ("x", "y", "c")
N_DEV = 8
N_CHIP = 4


def _pcall(body, **kw):
    return pl.pallas_call(body, **kw)


def _spec(shape, kind, nlb, nblk):
    nd = len(shape)
    if kind == "row":
        return pl.BlockSpec((BM, shape[1]), lambda i: (i, 0))
    if kind == "row_lat":
        return pl.BlockSpec((BM, shape[1]), lambda i: (jnp.minimum(i, nlb - 1), 0))
    if kind == "full":
        return pl.BlockSpec(tuple(shape), lambda i: (0,) * nd, pipeline_mode=pl.Buffered(1))
    if isinstance(kind, tuple) and kind[0] == "fullat":
        return pl.BlockSpec((None,) + tuple(shape[1:]), lambda i: (kind[1],) + (0,) * (nd - 1), pipeline_mode=pl.Buffered(1))
    if kind == "acc":
        return pl.BlockSpec(tuple(shape), lambda i: (0,) * nd)
    if kind == "any":
        return pl.BlockSpec(memory_space=pl.ANY)
    if kind == "stream":
        return pl.BlockSpec((1,) + tuple(shape[1:]), lambda i: (i // nlb,) + (0,) * (nd - 1))
    if kind in ("prev8", "prev16"):
        hb = int(kind[4:])
        return pl.BlockSpec((hb, shape[1]), lambda i: (jnp.maximum(i * (BM // hb) - 1, 0), 0))
    if kind in ("next8", "next16"):
        hb = int(kind[4:])
        last = nblk * BM // hb - 1
        return pl.BlockSpec((hb, shape[1]), lambda i: (jnp.minimum((i + 1) * (BM // hb), last), 0))
    raise ValueError(kind)


def _rowcall(name, body, nblk, nlb, ins, outs, scratch=()):
    return _pcall(
        body,
        name=name,
        grid=(nblk,),
        in_specs=[_spec(a.shape, k, nlb, nblk) for a, k in ins],
        out_specs=[_spec(s, k, nlb, nblk) for s, _, k in outs],
        out_shape=[jax.ShapeDtypeStruct(s, d) for s, d, _ in outs],
        scratch_shapes=list(scratch),
        compiler_params=pltpu.CompilerParams(dimension_semantics=("arbitrary",), vmem_limit_bytes=VMEM_LIMIT),
    )(*[a for a, _ in ins])


def _wk(w):
    return (w[0], ("fullat", w[1])) if isinstance(w, tuple) else (w, "full")


def _stream_edges(i, nlb):
    is_ctx = i == nlb
    return (i == 0) | is_ctx, (i == nlb - 1) | is_ctx, is_ctx


def _dotf(a, b):
    return jnp.dot(a, b, preferred_element_type=F32)


def _dot_nt(a, b):
    return lax.dot_general(a, b, (((1,), (1,)), ((), ())), preferred_element_type=F32)


def _dot_tn(a, b):
    return lax.dot_general(a, b, (((0,), (0,)), ((), ())), preferred_element_type=F32)


def _sigmoid(x):
    return 0.5 * jnp.tanh(0.5 * x) + 0.5


def _norm_mod(x, nw, sh, sc):
    r = lax.rsqrt(jnp.mean(x * x, axis=-1, keepdims=True) + EPS)
    xhat = x * r
    n = xhat * nw
    return n * (1.0 + sc) + sh, n, xhat, r


def _norm_mod_bwd(dh, n, xhat, r, nw, sc):
    dsh = jnp.sum(dh, axis=0, keepdims=True)
    dsc = jnp.sum(dh * n, axis=0, keepdims=True)
    dn = dh * (1.0 + sc)
    dnw = jnp.sum(dn * xhat, axis=0, keepdims=True)
    dxhat = dn * nw
    dx = r * (dxhat - xhat * jnp.mean(dxhat * xhat, axis=-1, keepdims=True))
    return dx, dsh, dsc, dnw


def _acc_init(i, *refs):
    @pl.when(i == 0)
    def _():
        for r in refs:
            r[...] = jnp.zeros(r.shape, r.dtype)


SH1, SC1, G1, SH2, SC2, G2 = range(6)


def _mrow(mod_ref, k):
    return mod_ref[0, k : k + 1, :]


def _shift_rows(x_ref, xp_ref, xn_ref, cs, first, last):
    cur = x_ref[:, cs].astype(F32)
    rows = lax.broadcasted_iota(jnp.int32, cur.shape, 0)
    pr = jnp.where(first, 0.0, xp_ref[HALO_BF16 - 1 : HALO_BF16, cs].astype(F32))
    nx = jnp.where(last, 0.0, xn_ref[0:1, cs].astype(F32))
    dn = jnp.where(rows == 0, pr, pltpu.roll(cur, 1, 0))
    up = jnp.where(rows == BM - 1, nx, pltpu.roll(cur, BM - 1, 0))
    return dn, cur, up


def ffn_up(x1, mod, nw2, w_up, nblk, nlb):
    r = x1.shape[0]

    def body(x_ref, mod_ref, nw_ref, w_ref, u_ref, h_ref):
        h, _, _, _ = _norm_mod(x_ref[...], nw_ref[...], _mrow(mod_ref, SH2), _mrow(mod_ref, SC2))
        hb = h.astype(BF16)
        h_ref[...] = hb
        u_ref[...] = _dotf(hb, w_ref[...]).astype(BF16)

    return _rowcall(
        "ffn_up", body, nblk, nlb,
        [(x1, "row"), (mod, "stream"), (nw2, "full"), _wk(w_up)],
        [((r, 2 * D_FF), BF16, "row"), ((r, D), BF16, "row")],
    )


def _conv_gate_chunk(u_ref, up_ref, un_ref, cw_ref, j, first, last):
    res = []
    for half in range(2):
        c0 = half * D_FF + j * FF_CHUNK
        cs = slice(c0, c0 + FF_CHUNK)
        dn, cur, up = _shift_rows(u_ref, up_ref, un_ref, cs, first, last)
        val = dn * cw_ref[0:1, cs] + cur * cw_ref[1:2, cs] + up * cw_ref[2:3, cs] + cw_ref[3:4, cs]
        res.append((val, dn, cur, up, cs))
    return res


def ffn_down(u, cw, w_down, x1, mod, nblk, nlb):
    r = x1.shape[0]

    def body(u_ref, up_ref, un_ref, cw_ref, w_ref, x_ref, mod_ref, x2_ref, f_ref, uc_ref):
        first, last, _ = _stream_edges(pl.program_id(0), nlb)
        f = jnp.zeros((BM, D), F32)
        for j in range(D_FF // FF_CHUNK):
            (a, _, _, _, acs), (v, _, _, _, vcs) = _conv_gate_chunk(u_ref, up_ref, un_ref, cw_ref, j, first, last)
            uc_ref[:, acs] = a.astype(BF16)
            uc_ref[:, vcs] = v.astype(BF16)
            f = f + _dotf((a * _sigmoid(a) * v).astype(BF16), w_ref[acs, :])
        f_ref[...] = f.astype(BF16)
        x2_ref[...] = x_ref[...] + _mrow(mod_ref, G2) * f

    return _rowcall(
        "ffn_down", body, nblk, nlb,
        [(u, "row"), (u, "prev16"), (u, "next16"), _wk(cw), _wk(w_down), (x1, "row"), (mod, "stream")],
        [((r, D), F32, "row"), ((r, D), BF16, "row"), ((r, 2 * D_FF), BF16, "row")],
    )


def ffn_bwd1(dx2, f, uc, w_down, mod, dep, nblk, nlb):
    r = dx2.shape[0]

    def body(dx_ref, f_ref, uc_ref, w_ref, mod_ref, dep_ref, duc_ref, dw_ref, dcb_ref, dmod_ref, dw_acc):
        i = pl.program_id(0)
        _acc_init(i, dcb_ref, dmod_ref, dw_acc)
        dx = dx_ref[...]
        df = (_mrow(mod_ref, G2) * dx).astype(BF16)
        dmod_ref[i // nlb, G2 : G2 + 1, :] += jnp.sum(dx * f_ref[...].astype(F32), axis=0, keepdims=True)
        for j in range(D_FF // FF_CHUNK):
            acs = slice(j * FF_CHUNK, (j + 1) * FF_CHUNK)
            vcs = slice(D_FF + j * FF_CHUNK, D_FF + (j + 1) * FF_CHUNK)
            a, v = uc_ref[:, acs].astype(F32), uc_ref[:, vcs].astype(F32)
            sa = _sigmoid(a)
            asa = a * sa
            dw_acc[acs, :] += _dot_tn((asa * v).astype(BF16), df)
            dg = _dot_nt(df, w_ref[acs, :])
            for dval, cs in ((dg * v * (sa * (1.0 + a * (1.0 - sa))), acs), (dg * asa, vcs)):
                duc_ref[:, cs] = dval.astype(BF16)
                dcb_ref[3:4, cs] += jnp.sum(dval, axis=0, keepdims=True)

        @pl.when(i == nblk - 1)
        def _():
            dw_ref[...] = dw_acc[...].astype(BF16)

    return _rowcall(
        "ffn_bwd1", body, nblk, nlb,
        [(dx2, "row"), (f, "row"), (uc, "row"), _wk(w_down), (mod, "stream"), (dep, "any")],
        [((r, 2 * D_FF), BF16, "row"), ((D_FF, D), BF16, "acc"), ((8, 2 * D_FF), F32, "acc"), ((2, 8, D), F32, "acc")],
        scratch=[pltpu.VMEM((D_FF, D), F32)],
    )


def ffn_bwd3(duc, u, cw, w_up, x1, dx2, mod, nw2, nblk, nlb):
    r = dx2.shape[0]

    def body(d_ref, dp_ref, dn_ref, u_ref, cw_ref, w_ref, x_ref, dx_ref, mod_ref, nw_ref, du_ref, dx1_ref, dcw_ref, dnw_ref, dmod_ref):
        i = pl.program_id(0)
        first, last, _ = _stream_edges(i, nlb)
        _acc_init(i, dcw_ref, dnw_ref, dmod_ref)
        dh = jnp.zeros((BM, D), F32)
        for j in range(2 * D_FF // FF_CHUNK):
            cs = slice(j * FF_CHUNK, (j + 1) * FF_CHUNK)
            dn, cur, up = _shift_rows(d_ref, dp_ref, dn_ref, cs, first, last)
            du = (up * cw_ref[0:1, cs] + cur * cw_ref[1:2, cs] + dn * cw_ref[2:3, cs]).astype(BF16)
            du_ref[:, cs] = du
            dh = dh + _dot_nt(du, w_ref[:, cs])
            uu = u_ref[:, cs].astype(F32)
            dcw_ref[0:1, cs] += jnp.sum(up * uu, axis=0, keepdims=True)
            dcw_ref[1:2, cs] += jnp.sum(cur * uu, axis=0, keepdims=True)
            dcw_ref[2:3, cs] += jnp.sum(dn * uu, axis=0, keepdims=True)
        sc = _mrow(mod_ref, SC2)
        nw = nw_ref[...]
        _, n, xhat, rr = _norm_mod(x_ref[...], nw, _mrow(mod_ref, SH2), sc)
        dxn, dsh, dsc, dnw = _norm_mod_bwd(dh, n, xhat, rr, nw, sc)
        dx1_ref[...] = dx_ref[...] + dxn
        s = i // nlb
        dmod_ref[s, SH2 : SH2 + 1, :] += dsh
        dmod_ref[s, SC2 : SC2 + 1, :] += dsc
        dnw_ref[0:1, :] += dnw

    return _rowcall(
        "ffn_bwd3", body, nblk, nlb,
        [(duc, "row"), (duc, "prev16"), (duc, "next16"), (u, "row"), _wk(cw), _wk(w_up), (x1, "row"), (dx2, "row"),
         (mod, "stream"), (nw2, "full")],
        [((r, 2 * D_FF), BF16, "row"), ((r, D), F32, "row"), ((8, 2 * D_FF), F32, "acc"), ((8, D), F32, "acc"),
         ((2, 8, D), F32, "acc")],
    )


def matmul_tn(a, b, nblk, tn=None):
    k, n = a.shape[1], b.shape[1]
    rows = nblk * BM
    tr = 768 if rows % 768 == 0 else (1024 if rows % 1024 == 0 else BM)
    if tn is None:
        tn = n
        while k * tn * 4 > 6 * 1024 * 1024 and tn % 256 == 0:
            tn //= 2
    steps = rows // tr

    def body(a_ref, b_ref, o_ref, acc):
        t = pl.program_id(1)

        @pl.when(t == 0)
        def _():
            acc[...] = jnp.zeros(acc.shape, acc.dtype)

        acc[...] += _dot_tn(a_ref[...], b_ref[...])

        @pl.when(t == steps - 1)
        def _():
            o_ref[...] = acc[...].astype(o_ref.dtype)

    return _pcall(
        body,
        name="matmul_tn",
        grid=(n // tn, steps),
        in_specs=[pl.BlockSpec((tr, k), lambda j, t: (t, 0)), pl.BlockSpec((tr, tn), lambda j, t: (t, j))],
        out_specs=pl.BlockSpec((k, tn), lambda j, t: (0, j)),
        out_shape=jax.ShapeDtypeStruct((k, n), BF16),
        scratch_shapes=[pltpu.VMEM((k, tn), F32)],
        compiler_params=pltpu.CompilerParams(dimension_semantics=("parallel", "arbitrary"), vmem_limit_bytes=VMEM_LIMIT),
    )(a, b)


EXT = BM + 2 * HALO_F32


def _pool_positions(i, nlb, nrows, row0):
    is_ctx = i == nlb
    t = (i - jnp.where(is_ctx, nlb, 0)) * BM + row0 + lax.broadcasted_iota(jnp.int32, (nrows, 1), 0)
    return t, jnp.where(is_ctx, BM, nlb * BM)


def _pool_cnt(t, win, slen):
    return (jnp.minimum(t + win // 2, slen) - jnp.maximum(t - win // 2, 0)).astype(F32)


def _pool_fill_ext(ext, i, nlb, x_ref, xp_ref, xn_ref, mod_ref, nw_ref):
    first, last, _ = _stream_edges(i, nlb)
    sh, sc, nw = _mrow(mod_ref, SH1), _mrow(mod_ref, SC1), nw_ref[...]
    hcur, n, xhat, r = _norm_mod(x_ref[...], nw, sh, sc)
    ext[0:HALO_F32, :] = jnp.where(first, 0.0, _norm_mod(xp_ref[...], nw, sh, sc)[0])
    ext[HALO_F32 : HALO_F32 + BM, :] = hcur
    ext[HALO_F32 + BM :, :] = jnp.where(last, 0.0, _norm_mod(xn_ref[...], nw, sh, sc)[0])
    return n, xhat, r


def _window_sum(ref, cols, offs):
    acc = None
    for o in offs:
        v = ref[HALO_F32 + o : HALO_F32 + o + BM, cols]
        acc = v if acc is None else acc + v
    return acc


def _pool_diff(ext, g, win, t, slen):
    cols = slice(g * POOL_GROUP, (g + 1) * POOL_GROUP)
    ssum = _window_sum(ext, cols, range(-(win // 2), win // 2))
    return ssum / _pool_cnt(t, win, slen) - ext[HALO_F32 : HALO_F32 + BM, cols]


def pool_fwd(x, mod, nw1, pw, pbs, nblk, nlb):
    r = x.shape[0]

    def body(x_ref, xp_ref, xn_ref, mod_ref, nw_ref, pw_ref, pbs_ref, x1_ref, ypre_ref, ext):
        i = pl.program_id(0)
        _pool_fill_ext(ext, i, nlb, x_ref, xp_ref, xn_ref, mod_ref, nw_ref)
        t, slen = _pool_positions(i, nlb, BM, 0)
        for g, win in enumerate(POOL_WINDOWS):
            cols = slice(g * POOL_GROUP, (g + 1) * POOL_GROUP)
            diff = _pool_diff(ext, g, win, t, slen)
            ypre = _dotf(diff.astype(BF16), pw_ref[g]) + pbs_ref[0:1, cols]
            ypre_ref[:, cols] = ypre
            x1_ref[:, cols] = x_ref[:, cols] + mod_ref[0, G1 : G1 + 1, cols] * (ypre * pbs_ref[1:2, cols])

    return _rowcall(
        "pool_fwd", body, nblk, nlb,
        [(x, "row"), (x, "prev8"), (x, "next8"), (mod, "stream"), (nw1, "full"), _wk(pw), _wk(pbs)],
        [((r, D), F32, "row"), ((r, D), F32, "row")],
        scratch=[pltpu.VMEM((EXT, D), F32)],
    )


def pool_bwd(dx1, x, ypre, mod, nw1, pw, pbs, dep, nblk, nlb, lat_dx):
    r = x.shape[0]

    def body(d_ref, dp_ref, dn_ref, x_ref, xp_ref, xn_ref, ypre_ref, mod_ref, nw_ref, pw_ref, pbs_ref, dep_ref,
             dx_ref, dpw_ref, dpbs_ref, dnw_ref, dmod_ref, ext, dext, eext, dh):
        i = pl.program_id(0)
        first, last, _ = _stream_edges(i, nlb)
        _acc_init(i, dpw_ref, dpbs_ref, dnw_ref, dmod_ref)
        n, xhat, rr = _pool_fill_ext(ext, i, nlb, x_ref, xp_ref, xn_ref, mod_ref, nw_ref)
        g1, scale = _mrow(mod_ref, G1), pbs_ref[1:2, :]
        dcur = d_ref[...]
        ypre = ypre_ref[...]
        s = i // nlb
        dmod_ref[s, G1 : G1 + 1, :] += jnp.sum(dcur * (ypre * scale), axis=0, keepdims=True)
        dy = g1 * dcur
        dpbs_ref[1:2, :] += jnp.sum(dy * ypre, axis=0, keepdims=True)
        dpbs_ref[0:1, :] += jnp.sum(dy * scale, axis=0, keepdims=True)
        gs = g1 * scale
        dext[0:HALO_F32, :] = jnp.where(first, 0.0, gs * dp_ref[...])
        dext[HALO_F32 : HALO_F32 + BM, :] = dy * scale
        dext[HALO_F32 + BM :, :] = jnp.where(last, 0.0, gs * dn_ref[...])
        t, slen = _pool_positions(i, nlb, BM, 0)
        text, _ = _pool_positions(i, nlb, EXT, -HALO_F32)
        for g, win in enumerate(POOL_WINDOWS):
            cols = slice(g * POOL_GROUP, (g + 1) * POOL_GROUP)
            diff = _pool_diff(ext, g, win, t, slen)
            dpre = dext[:, cols].astype(BF16)
            ddiff = _dot_nt(dpre, pw_ref[g])
            eext[...] = ddiff / jnp.maximum(_pool_cnt(text, win, slen), 1.0)
            dh[:, cols] = _window_sum(eext, slice(None), range(-(win // 2) + 1, win // 2 + 1)) - ddiff[HALO_F32 : HALO_F32 + BM, :]
            dpw_ref[g] += _dot_tn(diff.astype(BF16), dpre[HALO_F32 : HALO_F32 + BM, :])
        sc, nw = _mrow(mod_ref, SC1), nw_ref[...]
        dxn, dsh, dsc, dnw = _norm_mod_bwd(dh[...], n, xhat, rr, nw, sc)
        if lat_dx:
            @pl.when(i < nlb)
            def _():
                dx_ref[...] = dcur + dxn
        else:
            dx_ref[...] = dcur + dxn
        dmod_ref[s, SH1 : SH1 + 1, :] += dsh
        dmod_ref[s, SC1 : SC1 + 1, :] += dsc
        dnw_ref[0:1, :] += dnw

    return _rowcall(
        "pool_bwd", body, nblk, nlb,
        [(dx1, "row"), (dx1, "prev8"), (dx1, "next8"), (x, "row"), (x, "prev8"), (x, "next8"), (ypre, "row"),
         (mod, "stream"), (nw1, "full"), _wk(pw), _wk(pbs), (dep, "any")],
        [((nlb * BM, D), F32, "row_lat") if lat_dx else ((r, D), F32, "row"), ((4, POOL_GROUP, POOL_GROUP), F32, "acc"), ((8, D), F32, "acc"), ((8, D), F32, "acc"),
         ((2, 8, D), F32, "acc")],
        scratch=[pltpu.VMEM((EXT, D), F32), pltpu.VMEM((EXT, D), F32), pltpu.VMEM((EXT, POOL_GROUP), F32), pltpu.VMEM((BM, D), F32)],
    )


def rope_tables(s, l):
    rows = s // GRID_W
    row = jnp.broadcast_to(jnp.arange(rows)[:, None], (rows, GRID_W)).reshape(-1).astype(F32)
    col = jnp.broadcast_to(jnp.arange(GRID_W)[None, :], (rows, GRID_W)).reshape(-1).astype(F32)
    axis_dim = HD // 2
    inv = ROPE_THETA ** (-jnp.arange(0, axis_dim, 2, dtype=F32) / axis_dim)
    ar, ac = row[:, None] * inv, col[:, None] * inv
    cos = jnp.concatenate([jnp.cos(ar), jnp.cos(ar), jnp.cos(ac), jnp.cos(ac)], axis=-1)
    sin = jnp.concatenate([-jnp.sin(ar), jnp.sin(ar), -jnp.sin(ac), jnp.sin(ac)], axis=-1)
    return (jnp.concatenate([cos, jnp.ones((l, HD), F32)], axis=0), jnp.concatenate([sin, jnp.zeros((l, HD), F32)], axis=0))


def _partner(x):
    q = HD // 4
    lane = lax.broadcasted_iota(jnp.int32, x.shape, 1)
    return jnp.where((lane // q) % 2 == 0, pltpu.roll(x, HD - q, 1), pltpu.roll(x, q, 1))


def _head_norm(raw, gain):
    r = lax.rsqrt(jnp.mean(raw * raw, axis=-1, keepdims=True) + EPS)
    return raw * r, r


ATTN_SCALE = HD ** -0.5


def qkv_fwd(x, mod, nw1, w_qkv, gains, cos_t, sin_t, nblk, nlb):
    r = x.shape[0]

    def body(x_ref, mod_ref, nw_ref, w_ref, g_ref, c_ref, s_ref, raw_ref, q_ref, k_ref, v_ref, h_ref):
        h = _norm_mod(x_ref[...], nw_ref[...], _mrow(mod_ref, SH1), _mrow(mod_ref, SC1))[0].astype(BF16)
        h_ref[...] = h
        raw_ref[...] = _dotf(h, w_ref[...])
        cos, sin = c_ref[...], s_ref[...]
        for j in range(ATTN_HEADS + ATTN_KV_HEADS):
            isq = j < ATTN_HEADS
            xn = _head_norm(raw_ref[:, j * HD : (j + 1) * HD], None)[0] * (g_ref[0:1, :] if isq else g_ref[1:2, :])
            rot = xn * cos + _partner(xn) * sin
            if isq:
                rot = rot * ATTN_SCALE
            rot = rot.astype(BF16)
            if isq:
                q_ref[:, j * HD : (j + 1) * HD] = rot
            else:
                k_ref[:, (j - ATTN_HEADS) * HD : (j - ATTN_HEADS + 1) * HD] = rot
        v_ref[...] = raw_ref[:, NQ + NKV :].astype(BF16)

    return _rowcall(
        "qkv_fwd", body, nblk, nlb,
        [(x, "row"), (mod, "stream"), (nw1, "full"), (w_qkv, "full"), (gains, "full"), (cos_t, "row"), (sin_t, "row")],
        [((r, NQ + 2 * NKV), F32, "row"), ((r, NQ), BF16, "row"), ((r, NKV), BF16, "row"), ((r, NKV), BF16, "row"),
         ((r, D), BF16, "row")],
    )


def attn_fwd(q, k, v, nblk, nlb):
    r = q.shape[0]

    def body(q_ref, k_ref, v_ref, o_ref, lse_ref):
        def heads(keys):
            for kvh in range(ATTN_KV_HEADS):
                kh = k_ref[keys, kvh * HD : (kvh + 1) * HD]
                vh = v_ref[keys, kvh * HD : (kvh + 1) * HD]
                for g in range(ATTN_GROUP):
                    cs = slice((kvh * ATTN_GROUP + g) * HD, (kvh * ATTN_GROUP + g + 1) * HD)
                    s = _dot_nt(q_ref[:, cs], kh)
                    m = jnp.max(s, axis=-1, keepdims=True)
                    p = jnp.exp(s - m)
                    l = jnp.sum(p, axis=-1, keepdims=True)
                    o_ref[:, cs] = (_dotf(p.astype(BF16), vh) / l).astype(BF16)
                    j = kvh * ATTN_GROUP + g
                    lse_ref[:, j : j + 1] = m + jnp.log(l)

        i = pl.program_id(0)
        pl.when(i < nlb)(lambda: heads(slice(0, r)))
        pl.when(i == nlb)(lambda: heads(slice(nlb * BM, r)))

    return _rowcall(
        "attn_fwd", body, nblk, nlb,
        [(q, "row"), (k, "full"), (v, "full")],
        [((r, NQ), BF16, "row"), ((r, ATTN_HEADS), F32, "row")],
    )


def attn_out_fwd(o, w_o, x, mod, nblk, nlb):
    r = x.shape[0]

    def body(o_ref, w_ref, x_ref, mod_ref, x1_ref, y_ref):
        y = _dotf(o_ref[...], w_ref[...])
        y_ref[...] = y
        x1_ref[...] = x_ref[...] + _mrow(mod_ref, G1) * y

    return _rowcall(
        "attn_out_fwd", body, nblk, nlb,
        [(o, "row"), (w_o, "full"), (x, "row"), (mod, "stream")],
        [((r, D), F32, "row"), ((r, D), F32, "row")],
    )


def mix_out_bwd(name, dx1, y, w_out, mod, dep, nblk, nlb):
    r = dx1.shape[0]
    kin = w_out.shape[0]

    def body(d_ref, y_ref, w_ref, mod_ref, dep_ref, dy_ref, do_ref, dmod_ref):
        i = pl.program_id(0)
        _acc_init(i, dmod_ref)
        d = d_ref[...]
        dmod_ref[i // nlb, G1 : G1 + 1, :] += jnp.sum(d * y_ref[...], axis=0, keepdims=True)
        dy = (_mrow(mod_ref, G1) * d).astype(BF16)
        dy_ref[...] = dy
        do_ref[...] = _dot_nt(dy, w_ref[...]).astype(do_ref.dtype)

    return _rowcall(
        name, body, nblk, nlb,
        [(dx1, "row"), (y, "row"), (w_out, "full"), (mod, "stream"), (dep, "any")],
        [((r, D), BF16, "row"), ((r, kin), BF16, "row"), ((2, 8, D), F32, "acc")],
    )


ATTN_KCHUNK = 3 * BM


def attn_bwd(q, k, v, o, do, lse, nblk, nlb):
    r = q.shape[0]
    kc = ATTN_KCHUNK if r % ATTN_KCHUNK == 0 else BM
    nkc = r // kc

    def body(q_ref, k_ref, v_ref, o_ref, do_ref, lse_ref, dq_ref, dk_ref, dv_ref):
        i = pl.program_id(0)
        _acc_init(i, dk_ref, dv_ref)

        def heads(chunks):
            for kvh in range(ATTN_KV_HEADS):
                ks = slice(kvh * HD, (kvh + 1) * HD)
                for g in range(ATTN_GROUP):
                    j = kvh * ATTN_GROUP + g
                    cs = slice(j * HD, (j + 1) * HD)
                    qh, doh = q_ref[:, cs], do_ref[:, cs]
                    delta = jnp.sum(doh.astype(F32) * o_ref[:, cs].astype(F32), axis=-1, keepdims=True)
                    lse = lse_ref[:, j : j + 1]
                    dq = jnp.zeros((BM, HD), F32)
                    for rs in chunks:
                        kh, vh = k_ref[rs, ks], v_ref[rs, ks]
                        p = jnp.exp(_dot_nt(qh, kh) - lse)
                        ds = (p * (_dot_nt(doh, vh) - delta)).astype(BF16)
                        dq = dq + _dotf(ds, kh)
                        dk_ref[rs, ks] += _dot_tn(ds, qh)
                        dv_ref[rs, ks] += _dot_tn(p.astype(BF16), doh)
                    dq_ref[:, cs] = dq * ATTN_SCALE

        pl.when(i < nlb)(lambda: heads([slice(c * kc, (c + 1) * kc) for c in range(nkc)]))
        pl.when(i == nlb)(lambda: heads([slice(nlb * BM, r)]))

    return _rowcall(
        "attn_bwd", body, nblk, nlb,
        [(q, "row"), (k, "full"), (v, "full"), (o, "row"), (do, "row"), (lse, "row")],
        [((r, NQ), F32, "row"), ((r, NKV), F32, "acc"), ((r, NKV), F32, "acc")],
    )


def qkv_bwd(dq, dk, dv, raw, gains, cos_t, sin_t, w_qkv, x, dx1, mod, nw1, nblk, nlb, ctx_dx_zero):
    r = x.shape[0]

    def body(dq_ref, dk_ref, dv_ref, raw_ref, g_ref, c_ref, s_ref, w_ref, x_ref, dx1_ref, mod_ref, nw_ref,
             dx_ref, draw_ref, dg_ref, dnw_ref, dmod_ref):
        i = pl.program_id(0)
        _acc_init(i, dg_ref, dnw_ref, dmod_ref)
        cos, sin = c_ref[...], s_ref[...]
        for j in range(ATTN_HEADS + ATTN_KV_HEADS):
            isq = j < ATTN_HEADS
            cs = slice(j * HD, (j + 1) * HD)
            dr = dq_ref[:, cs] if isq else dk_ref[:, (j - ATTN_HEADS) * HD : (j - ATTN_HEADS + 1) * HD]
            dxn = dr * cos + _partner(dr * sin)
            xhat, rr = _head_norm(raw_ref[:, cs], None)
            gi = 0 if isq else 1
            dg_ref[gi : gi + 1, :] += jnp.sum(dxn * xhat, axis=0, keepdims=True)
            dxhat = dxn * g_ref[gi : gi + 1, :]
            draw_ref[:, cs] = (rr * (dxhat - xhat * jnp.mean(dxhat * xhat, axis=-1, keepdims=True))).astype(BF16)
        draw_ref[:, NQ + NKV :] = dv_ref[...].astype(BF16)
        dh = _dot_nt(draw_ref[...], w_ref[...])
        sc, nw = _mrow(mod_ref, SC1), nw_ref[...]
        _, n, xhat, rr = _norm_mod(x_ref[...], nw, _mrow(mod_ref, SH1), sc)
        dxn, dsh, dsc, dnw = _norm_mod_bwd(dh, n, xhat, rr, nw, sc)
        dres = dx1_ref[...]
        if ctx_dx_zero:
            dres = jnp.where(i == nlb, 0.0, dres)
        dx_ref[...] = dres + dxn
        s = i // nlb
        dmod_ref[s, SH1 : SH1 + 1, :] += dsh
        dmod_ref[s, SC1 : SC1 + 1, :] += dsc
        dnw_ref[0:1, :] += dnw

    return _rowcall(
        "qkv_bwd", body, nblk, nlb,
        [(dq, "row"), (dk, "row"), (dv, "row"), (raw, "row"), (gains, "full"), (cos_t, "row"), (sin_t, "row"),
         (w_qkv, "full"), (x, "row"), (dx1, "row"), (mod, "stream"), (nw1, "full")],
        [((r, D), F32, "row"), ((r, NQ + 2 * NKV), BF16, "row"), ((8, HD), F32, "acc"), ((8, D), F32, "acc"),
         ((2, 8, D), F32, "acc")],
    )


RET_KSCALE = RET_DK ** -0.5
RET_HP = RET_HEADS


def ret_in_fwd(x, mod, nw1, w_in, nblk, nlb):
    r = x.shape[0]

    def body(x_ref, mod_ref, nw_ref, w_ref, q_ref, k_ref, v_ref, g_ref, h_ref):
        h = _norm_mod(x_ref[...], nw_ref[...], _mrow(mod_ref, SH1), _mrow(mod_ref, SC1))[0].astype(BF16)
        h_ref[...] = h
        q_ref[...] = _dotf(h, w_ref[:, 0:RNQ]).astype(BF16)
        k_ref[...] = (_dotf(h, w_ref[:, RNQ : 2 * RNQ]) * RET_KSCALE).astype(BF16)
        v_ref[...] = _dotf(h, w_ref[:, 2 * RNQ : 2 * RNQ + RNV]).astype(BF16)
        g_ref[...] = _dotf(h, w_ref[:, 2 * RNQ + RNV :]).astype(BF16)

    return _rowcall(
        "ret_in_fwd", body, nblk, nlb,
        [(x, "row"), (mod, "stream"), (nw1, "full"), (w_in, "full")],
        [((r, RNQ), BF16, "row"), ((r, RNQ), BF16, "row"), ((r, RNV), BF16, "row"), ((r, RNV), BF16, "row"), ((r, D), BF16, "row")],
    )


def _log_sigmoid(x):
    return jnp.minimum(x, 0.0) - jnp.log(1.0 + jnp.exp(-jnp.abs(x)))


def _ret_decays(lg, reverse):
    c = BM
    i = lax.broadcasted_iota(jnp.int32, (c, c), 0)
    j = lax.broadcasted_iota(jnp.int32, (c, c), 1)
    diff = (j - i) if reverse else (i - j)
    ediff = jnp.maximum(diff, 0).astype(F32)
    dm = jnp.where(diff >= 0, jnp.exp(ediff * lg), 0.0)
    rr = lax.broadcasted_iota(jnp.int32, (c, 1), 0).astype(F32)
    eq = (c - rr) if reverse else (rr + 1.0)
    ek = rr if reverse else (c - 1.0 - rr)
    return dm, ediff, jnp.exp(eq * lg), eq, jnp.exp(ek * lg), ek, jnp.exp(c * lg)


def _ret_chunk_index(nlb):
    return (lambda s: jnp.where(s == 0, nlb, s - 1)), (lambda s: jnp.where(s == 0, nlb, nlb - s))


def ret_scan_fwd(q, k, v, logit_b, nlb):
    r = q.shape[0]
    nb = nlb + 1
    fidx, bidx = _ret_chunk_index(nlb)

    def body(qf, kf, vf, qb, kb, vb, lg_ref, of_ref, ob_ref, rf_ref, rb_ref, stf, stb):
        s = pl.program_id(1)

        @pl.when(s == 0)
        def _():
            stf[...] = jnp.zeros(stf.shape, F32)
            stb[...] = jnp.zeros(stb.shape, F32)

        for d, (q_ref, k_ref, v_ref, o_ref, rs_ref, st) in enumerate(((qf, kf, vf, of_ref, rf_ref, stf), (qb, kb, vb, ob_ref, rb_ref, stb))):
            for hp in range(RET_HP):
                kcs, vcs = slice(hp * RET_DK, (hp + 1) * RET_DK), slice(hp * RET_DV, (hp + 1) * RET_DV)
                lg = _log_sigmoid(lg_ref[d, hp])[0:1, 0:1]
                dm, _, qd, _, kd, _, gc = _ret_decays(lg, d == 1)
                qq, kk, vv, st0 = q_ref[:, kcs], k_ref[:, kcs], v_ref[:, vcs], st[hp]
                rs_ref[hp, 0] = st0
                a = _dot_nt(qq, kk) * dm
                o = _dotf(a.astype(BF16), vv) + _dotf(qq, st0.astype(BF16)) * qd
                o_ref[:, vcs] = jnp.where(s == 0, 0.0, o)
                st[hp] = st0 * gc + _dot_tn((kk.astype(F32) * kd).astype(BF16), vv)

    qspec = lambda f: pl.BlockSpec((BM, RET_HP * RET_DK), lambda h, s: (f(s), h))
    vspec = lambda f: pl.BlockSpec((BM, RET_HP * RET_DV), lambda h, s: (f(s), h))
    sspec = pl.BlockSpec((RET_HP, 1, RET_DK, RET_DV), lambda h, s: (h, s, 0, 0))
    return _pcall(
        body,
        name="ret_scan_fwd",
        grid=(RET_HEADS // RET_HP, nb),
        in_specs=[qspec(fidx), qspec(fidx), vspec(fidx), qspec(bidx), qspec(bidx), vspec(bidx),
                  pl.BlockSpec((2, RET_HP, 8, 128), lambda h, s: (0, h, 0, 0))],
        out_specs=[vspec(fidx), vspec(bidx), sspec, sspec],
        out_shape=[jax.ShapeDtypeStruct((r, RNV), F32), jax.ShapeDtypeStruct((r, RNV), F32),
                   jax.ShapeDtypeStruct((RET_HEADS, nb, RET_DK, RET_DV), F32), jax.ShapeDtypeStruct((RET_HEADS, nb, RET_DK, RET_DV), F32)],
        scratch_shapes=[pltpu.VMEM((RET_HP, RET_DK, RET_DV), F32), pltpu.VMEM((RET_HP, RET_DK, RET_DV), F32)],
        compiler_params=pltpu.CompilerParams(dimension_semantics=("parallel", "arbitrary"), vmem_limit_bytes=VMEM_LIMIT),
    )(q, k, v, q, k, v, logit_b)


def _group_norm(o):
    mu = jnp.mean(o, axis=-1, keepdims=True)
    oc = o - mu
    rstd = lax.rsqrt(jnp.mean(oc * oc, axis=-1, keepdims=True) + EPS)
    return oc * rstd, rstd


def ret_out_fwd(o_f, o_b, g, gnw, w_out, x, mod, nblk, nlb):
    r = x.shape[0]

    def body(of_ref, ob_ref, g_ref, gn_ref, w_ref, x_ref, mod_ref, x1_ref, y_ref, z_ref):
        for hh in range(RET_HEADS):
            cs = slice(hh * RET_DV, (hh + 1) * RET_DV)
            yhat, _ = _group_norm(of_ref[:, cs] + ob_ref[:, cs])
            gg = g_ref[:, cs].astype(F32)
            z_ref[:, cs] = (gg * _sigmoid(gg) * (yhat * gn_ref[0:1, cs])).astype(BF16)
        y = _dotf(z_ref[...], w_ref[...])
        y_ref[...] = y
        x1_ref[...] = x_ref[...] + _mrow(mod_ref, G1) * y

    return _rowcall(
        "ret_out_fwd", body, nblk, nlb,
        [(o_f, "row"), (o_b, "row"), (g, "row"), (gnw, "full"), (w_out, "full"), (x, "row"), (mod, "stream")],
        [((r, D), F32, "row"), ((r, D), F32, "row"), ((r, RNV), BF16, "row")],
    )


def ret_out_bwd(dx1, y, o_f, o_b, g, gnw, w_out, mod, dep, nblk, nlb):
    r = dx1.shape[0]

    def body(d_ref, y_ref, of_ref, ob_ref, g_ref, gn_ref, w_ref, mod_ref, dep_ref, dy_ref, do_ref, dg_ref, dgn_ref, dmod_ref, dz):
        i = pl.program_id(0)
        _acc_init(i, dgn_ref, dmod_ref)
        d = d_ref[...]
        dmod_ref[i // nlb, G1 : G1 + 1, :] += jnp.sum(d * y_ref[...], axis=0, keepdims=True)
        dy = (_mrow(mod_ref, G1) * d).astype(BF16)
        dy_ref[...] = dy
        dz[...] = _dot_nt(dy, w_ref[...])
        for hh in range(RET_HEADS):
            cs = slice(hh * RET_DV, (hh + 1) * RET_DV)
            yhat, rstd = _group_norm(of_ref[:, cs] + ob_ref[:, cs])
            gg = g_ref[:, cs].astype(F32)
            sg = _sigmoid(gg)
            gn = gn_ref[0:1, cs]
            dzz = dz[:, cs]
            dg_ref[:, cs] = (dzz * (yhat * gn) * (sg * (1.0 + gg * (1.0 - sg)))).astype(BF16)
            dyn = dzz * (gg * sg)
            dgn_ref[0:1, cs] += jnp.sum(dyn * yhat, axis=0, keepdims=True)
            dyh = dyn * gn
            do = rstd * (dyh - jnp.mean(dyh, axis=-1, keepdims=True) - yhat * jnp.mean(dyh * yhat, axis=-1, keepdims=True))
            do_ref[:, cs] = do.astype(BF16)

    return _rowcall(
        "ret_out_bwd", body, nblk, nlb,
        [(dx1, "row"), (y, "row"), (o_f, "row"), (o_b, "row"), (g, "row"), (gnw, "full"), (w_out, "full"), (mod, "stream"), (dep, "any")],
        [((r, D), BF16, "row"), ((r, RNV), BF16, "row"), ((r, RNV), BF16, "row"), ((8, RNV), F32, "acc"), ((2, 8, D), F32, "acc")],
        scratch=[pltpu.VMEM((BM, RNV), F32)],
    )


def ret_scan_bwd(q, k, v, do, st_f, st_b, logit_b, nlb):
    r = q.shape[0]
    nb = nlb + 1
    fidx, bidx = _ret_chunk_index(nlb)
    step = lambda t: nb - 1 - t

    def body(qf, kf, vf, dof, rf, qb, kb, vb, dob, rb, lg_ref,
             dqf, dkf, dvf, dqb, dkb, dvb, dlg_ref, drf, drb):
        t = pl.program_id(1)
        s = step(t)

        @pl.when(t == 0)
        def _():
            drf[...] = jnp.zeros(drf.shape, F32)
            drb[...] = jnp.zeros(drb.shape, F32)
            dlg_ref[...] = jnp.zeros(dlg_ref.shape, F32)

        dirs = ((qf, kf, vf, dof, rf, dqf, dkf, dvf, drf), (qb, kb, vb, dob, rb, dqb, dkb, dvb, drb))
        for d, (q_ref, k_ref, v_ref, do_ref, rs_ref, dq_ref, dk_ref, dv_ref, dr) in enumerate(dirs):
            for hp in range(RET_HP):
                kcs, vcs = slice(hp * RET_DK, (hp + 1) * RET_DK), slice(hp * RET_DV, (hp + 1) * RET_DV)
                lg = _log_sigmoid(lg_ref[d, hp])[0:1, 0:1]
                dm, ediff, qd, eq, kd, ek, gc = _ret_decays(lg, d == 1)
                qq, kk, vv = q_ref[:, kcs], k_ref[:, kcs], v_ref[:, vcs]
                dob16 = jnp.where(s == 0, jnp.zeros((), BF16), do_ref[:, vcs])
                do32 = dob16.astype(F32)
                st0 = rs_ref[hp, 0]
                st16 = st0.astype(BF16)
                dr0 = dr[hp]
                dr16 = dr0.astype(BF16)
                a = _dot_nt(qq, kk) * dm
                daf = _dot_nt(dob16, vv)
                ds = (daf * dm).astype(BF16)
                qr = _dotf(qq, st16)
                k32 = kk.astype(F32)
                kdec = (k32 * kd).astype(BF16)
                dv_ref[:, vcs] = (_dot_tn(a.astype(BF16), dob16) + _dotf(kdec, dr16)).astype(BF16)
                dq_ref[:, kcs] = (_dotf(ds, kk) + _dot_nt(dob16, st16) * qd).astype(BF16)
                vdr = _dot_nt(vv, dr16)
                dk_ref[:, kcs] = (_dot_tn(ds, qq) + vdr * kd).astype(BF16)
                tot = (jnp.sum(daf * a * ediff)
                       + jnp.sum(eq * qd * jnp.sum(do32 * qr, axis=-1, keepdims=True))
                       + jnp.sum(ek * kd * jnp.sum(k32 * vdr, axis=-1, keepdims=True))
                       + jnp.sum(BM * gc * jnp.sum(dr0 * st0, axis=-1, keepdims=True)))
                dlg_ref[d, hp] += tot
                dr[hp] = gc * dr0 + _dot_tn(qq, (do32 * qd).astype(BF16))

        @pl.when(t == nb - 1)
        def _():
            dlg_ref[...] = dlg_ref[...] * _sigmoid(-lg_ref[...])

    qspec = lambda f: pl.BlockSpec((BM, RET_HP * RET_DK), lambda h, t: (f(step(t)), h))
    vspec = lambda f: pl.BlockSpec((BM, RET_HP * RET_DV), lambda h, t: (f(step(t)), h))
    sspec = pl.BlockSpec((RET_HP, 1, RET_DK, RET_DV), lambda h, t: (h, step(t), 0, 0))
    lspec = pl.BlockSpec((2, RET_HP, 8, 128), lambda h, t: (0, h, 0, 0))
    sq, sv = jax.ShapeDtypeStruct((r, RNQ), BF16), jax.ShapeDtypeStruct((r, RNV), BF16)
    return _pcall(
        body,
        name="ret_scan_bwd",
        grid=(RET_HEADS // RET_HP, nb),
        in_specs=[qspec(fidx), qspec(fidx), vspec(fidx), vspec(fidx), sspec,
                  qspec(bidx), qspec(bidx), vspec(bidx), vspec(bidx), sspec, lspec],
        out_specs=[qspec(fidx), qspec(fidx), vspec(fidx), qspec(bidx), qspec(bidx), vspec(bidx), lspec],
        out_shape=[sq, sq, sv, sq, sq, sv, jax.ShapeDtypeStruct((2, RET_HEADS, 8, 128), F32)],
        scratch_shapes=[pltpu.VMEM((RET_HP, RET_DK, RET_DV), F32), pltpu.VMEM((RET_HP, RET_DK, RET_DV), F32)],
        compiler_params=pltpu.CompilerParams(dimension_semantics=("parallel", "arbitrary"), vmem_limit_bytes=VMEM_LIMIT),
    )(q, k, v, do, st_f, q, k, v, do, st_b, logit_b)


def ret_in_bwd(dqf, dkf, dvf, dqb, dkb, dvb, dgate, w_in, x, dx1, mod, nw1, nblk, nlb):
    r = x.shape[0]
    nin = 2 * RNQ + 2 * RNV

    def body(dqf_ref, dkf_ref, dvf_ref, dqb_ref, dkb_ref, dvb_ref, dg_ref, w_ref, x_ref, dx1_ref, mod_ref, nw_ref,
             dx_ref, din_ref, dnw_ref, dmod_ref):
        i = pl.program_id(0)
        _acc_init(i, dnw_ref, dmod_ref)
        is_ctx = i == nlb
        din_ref[:, 0:RNQ] = (dqf_ref[...].astype(F32) + dqb_ref[...].astype(F32)).astype(BF16)
        din_ref[:, RNQ : 2 * RNQ] = ((dkf_ref[...].astype(F32) + dkb_ref[...].astype(F32)) * RET_KSCALE).astype(BF16)
        din_ref[:, 2 * RNQ : 2 * RNQ + RNV] = (dvf_ref[...].astype(F32) + dvb_ref[...].astype(F32)).astype(BF16)
        din_ref[:, 2 * RNQ + RNV :] = jnp.where(is_ctx, jnp.zeros((), BF16), dg_ref[...])
        dh = _dot_nt(din_ref[...], w_ref[...])
        sc, nw = _mrow(mod_ref, SC1), nw_ref[...]
        _, n, xhat, rr = _norm_mod(x_ref[...], nw, _mrow(mod_ref, SH1), sc)
        dxn, dsh, dsc, dnw = _norm_mod_bwd(dh, n, xhat, rr, nw, sc)
        dx_ref[...] = jnp.where(is_ctx, 0.0, dx1_ref[...]) + dxn
        s = i // nlb
        dmod_ref[s, SH1 : SH1 + 1, :] += dsh
        dmod_ref[s, SC1 : SC1 + 1, :] += dsc
        dnw_ref[0:1, :] += dnw

    return _rowcall(
        "ret_in_bwd", body, nblk, nlb,
        [(dqf, "row"), (dkf, "row"), (dvf, "row"), (dqb, "row"), (dkb, "row"), (dvb, "row"), (dgate, "row"),
         (w_in, "full"), (x, "row"), (dx1, "row"), (mod, "stream"), (nw1, "full")],
        [((r, D), F32, "row"), ((r, nin), BF16, "row"), ((8, D), F32, "acc"), ((2, 8, D), F32, "acc")],
    )


def loss_head(xout, target, nlb):
    r = xout.shape[0]

    def body(x_ref, t_ref, dx_ref, l_ref):
        _acc_init(pl.program_id(0), l_ref)
        err = x_ref[...] - t_ref[...]
        dx_ref[...] = err * (1.0 / D)
        l_ref[...] += 0.5 * jnp.sum(jnp.mean(err * err, axis=-1, keepdims=True))

    return _rowcall(
        "loss_head", body, nlb, nlb,
        [(xout, "row"), (target, "row")],
        [((r, D), F32, "row"), ((8, 128), F32, "acc")],
    )


N_MIXERS = 3
POOL, ATTN, RET = range(3)


def _layer_plan(depth):
    plan = []
    for i in range(depth):
        kind = i % N_MIXERS
        ctx_out = any(k % N_MIXERS != POOL for k in range(i + 1, depth))
        plan.append((kind, i // N_MIXERS, ctx_out or kind != POOL, ctx_out))
    return plan


def local_step(xs, target, mods, w, nlb, depth, fetch, emit):
    nb = nlb + 1
    plan = _layer_plan(depth)
    saved = []
    x = xs
    for i, (kind, j, ctx_in, ctx_out) in enumerate(plan):
        nmix = nb if ctx_out else nlb
        mod, nw1, nw2 = mods[i], w["nw"][i, 0:1], w["nw"][i, 1:2]
        lw = fetch(i, MIX, x)
        sv = {"x": x, "lw": lw}
        if kind == POOL:
            x1, sv["ypre"] = pool_fwd(x, mod, nw1, lw["pool_w"], (w["pbs"], j), nmix, nlb)
        elif kind == ATTN:
            assert ctx_out
            sv["raw"], sv["q"], sv["k"], sv["v"], sv["h"] = qkv_fwd(x, mod, nw1, lw["attn_w_qkv"], w["gains"], w["cos"], w["sin"], nb, nlb)
            sv["o"], sv["lse"] = attn_fwd(sv["q"], sv["k"], sv["v"], nb, nlb)
            x1, sv["y"] = attn_out_fwd(sv["o"], lw["attn_w_o"], x, mod, nb, nlb)
        else:
            assert ctx_in and not ctx_out
            sv["q"], sv["k"], sv["v"], sv["g"], sv["h"] = ret_in_fwd(x, mod, nw1, lw["ret_w_in"], nb, nlb)
            sv["o_f"], sv["o_b"], sv["st_f"], sv["st_b"] = ret_scan_fwd(sv["q"], sv["k"], sv["v"], w["logit_b"], nlb)
            x1, sv["y"], sv["z"] = ret_out_fwd(sv["o_f"], sv["o_b"], sv["g"], w["gnw"], lw["ret_w_out"], x, mod, nlb, nlb)
        sv["x1"] = x1
        lw.update(fetch(i, FFN, x1))
        sv["u"], sv["h2"] = ffn_up(x1, mod, nw2, lw["ffn_w_up"], nmix, nlb)
        x, sv["f"], sv["uc"] = ffn_down(sv["u"], (w["cw"], i), lw["ffn_w_down"], x1, mod, nmix, nlb)
        saved.append(sv)

    dx, loss_tile = loss_head(x, target, nlb)
    g = {k: [None] * depth for k in ("dcw", "dnw1", "dnw2", "dmod")}
    dep = loss_tile
    for i in reversed(range(depth)):
        kind, j, ctx_in, ctx_out = plan[i]
        sv = saved[i]
        lw, big = sv["lw"], {}
        nmix = nb if ctx_out else nlb
        mod, nw1, nw2 = mods[i], w["nw"][i, 0:1], w["nw"][i, 1:2]
        duc, big["ffn_w_down"], dcb, dmod = ffn_bwd1(dx, sv["f"], sv["uc"], lw["ffn_w_down"], mod, dep, nmix, nlb)
        du, dx1, dcw, g["dnw2"][i], dm = ffn_bwd3(duc, sv["u"], (w["cw"], i), lw["ffn_w_up"], sv["x1"], dx, mod, nw2, nmix, nlb)
        g["dcw"][i] = dcw + dcb
        dmod = dmod + dm
        big["ffn_w_up"] = matmul_tn(sv["h2"], du, nmix)
        dep, big = emit(i, FFN, big), {}
        if kind == POOL:
            dx, dpw, dpbs, g["dnw1"][i], dm = pool_bwd(dx1, sv["x"], sv["ypre"], mod, nw1, lw["pool_w"], (w["pbs"], j), dep, nmix, nlb, i == 0)
            big["pool_w"] = dpw.astype(BF16)
            g.setdefault("dpbs", {})[j] = dpbs
        elif kind == ATTN:
            dy, do, dm1 = mix_out_bwd("attn_out_bwd", dx1, sv["y"], lw["attn_w_o"], mod, dep, nb, nlb)
            big["attn_w_o"] = matmul_tn(sv["o"], dy, nb)
            dq, dk, dv = attn_bwd(sv["q"], sv["k"], sv["v"], sv["o"], do, sv["lse"], nb, nlb)
            dx, draw, g["dgains"], g["dnw1"][i], dm = qkv_bwd(
                dq, dk, dv, sv["raw"], w["gains"], w["cos"], w["sin"], lw["attn_w_qkv"], sv["x"], dx1, mod, nw1, nb, nlb, False)
            big["attn_w_qkv"] = matmul_tn(sv["h"], draw, nb)
            dm = dm + dm1
        else:
            dy, do, dgate, g["dgnw"], dm1 = ret_out_bwd(dx1, sv["y"], sv["o_f"], sv["o_b"], sv["g"], w["gnw"], lw["ret_w_out"], mod, dep, nlb, nlb)
            big["ret_w_out"] = matmul_tn(sv["z"], dy, nlb)
            dqf, dkf, dvf, dqb, dkb, dvb, g["dlogit"] = ret_scan_bwd(sv["q"], sv["k"], sv["v"], do, sv["st_f"], sv["st_b"], w["logit_b"], nlb)
            dx, din, g["dnw1"][i], dm = ret_in_bwd(dqf, dkf, dvf, dqb, dkb, dvb, dgate, lw["ret_w_in"], sv["x"], dx1, mod, nw1, nb, nlb)
            big["ret_w_in"] = matmul_tn(sv["h"], din, nb)
            dm = dm + dm1
        g["dmod"][i] = dmod + dm
        dep = emit(i, MIX, big)
    return loss_tile, dx, g


MESH_ID = pl.DeviceIdType.MESH
CHIP_FLIPS = ((1, 0), (0, 1), (1, 1))


def _pos():
    return lax.axis_index("x"), lax.axis_index("y"), lax.axis_index("c")


def _flip(v, b):
    return 1 - v if b else v


def small_all_gather(name, x):
    rows, n = x.shape

    def body(x_ref, out_ref, send_sems, recv_sems, local_sem):
        mx, my, mc = _pos()
        me = 4 * mx + 2 * my + mc
        mine = pltpu.make_async_copy(x_ref, out_ref.at[me], local_sem)
        mine.start()
        sends, peers = [], []
        for kk in range(1, N_DEV):
            peer = (_flip(mx, (kk >> 2) & 1), _flip(my, (kk >> 1) & 1), _flip(mc, kk & 1))
            cp = pltpu.make_async_remote_copy(src_ref=x_ref, dst_ref=out_ref.at[me], send_sem=send_sems.at[kk - 1],
                                              recv_sem=recv_sems.at[kk - 1], device_id=peer, device_id_type=MESH_ID)
            cp.start()
            sends.append(cp)
            peers.append(peer)
        for kk, peer in enumerate(peers):
            pidx = 4 * peer[0] + 2 * peer[1] + peer[2]
            pltpu.make_async_remote_copy(src_ref=x_ref, dst_ref=out_ref.at[pidx], send_sem=send_sems.at[kk],
                                         recv_sem=recv_sems.at[kk], device_id=peer, device_id_type=MESH_ID).wait_recv()
        for cp in sends:
            cp.wait_send()
        mine.wait()

    return _pcall(
        body,
        name=name,
        out_shape=jax.ShapeDtypeStruct((N_DEV, rows, n), x.dtype),
        in_specs=[pl.BlockSpec(memory_space=pltpu.VMEM)],
        out_specs=pl.BlockSpec(memory_space=pltpu.VMEM),
        scratch_shapes=[pltpu.SemaphoreType.DMA((N_DEV - 1,)), pltpu.SemaphoreType.DMA((N_DEV - 1,)), pltpu.SemaphoreType.DMA],
        compiler_params=pltpu.CompilerParams(vmem_limit_bytes=VMEM_LIMIT),
    )(x)


def _hbm_exchange(name, ins, out_shapes, plan):
    n_in = len(ins)
    probe_local, probe_remote = plan([None] * n_in, [None] * len(out_shapes), probe=True)

    def body(*refs):
        in_refs, out_refs = refs[:n_in], refs[n_in : n_in + len(out_shapes)]
        send_sems, recv_sems, local_sems = refs[n_in + len(out_shapes) :]
        local, remote = plan(in_refs, out_refs, probe=False)
        lcs = [pltpu.make_async_copy(s, d, local_sems.at[k]) for k, (s, d) in enumerate(local)]
        for cp in lcs:
            cp.start()
        rcs = []
        for k, (s, d, peer, _) in enumerate(remote):
            cp = pltpu.make_async_remote_copy(src_ref=s, dst_ref=d, send_sem=send_sems.at[k], recv_sem=recv_sems.at[k],
                                              device_id=peer, device_id_type=MESH_ID)
            cp.start()
            rcs.append(cp)
        for k, (s, _, peer, here) in enumerate(remote):
            pltpu.make_async_remote_copy(src_ref=s, dst_ref=here, send_sem=send_sems.at[k], recv_sem=recv_sems.at[k],
                                         device_id=peer, device_id_type=MESH_ID).wait_recv()
        for cp in rcs:
            cp.wait_send()
        for cp in lcs:
            cp.wait()

    return _pcall(
        body,
        name=name,
        out_shape=list(out_shapes),
        in_specs=[pl.BlockSpec(memory_space=pl.ANY)] * n_in,
        out_specs=[pl.BlockSpec(memory_space=pl.ANY)] * len(out_shapes),
        scratch_shapes=[pltpu.SemaphoreType.DMA((max(probe_remote, 1),)), pltpu.SemaphoreType.DMA((max(probe_remote, 1),)),
                        pltpu.SemaphoreType.DMA((max(probe_local, 1),))],
    )(*ins)


def _at_axis(ref, axis, start, size):
    return ref.at[tuple(pl.ds(start, size) if a == axis else slice(None) for a in range(len(ref.shape)))]


HBM_SPEC = pl.BlockSpec(memory_space=pltpu.HBM)
SEM_SPEC = pl.BlockSpec(memory_space=pltpu.SEMAPHORE)
SIDE_EFFECT = pltpu.SideEffectType.DATAFLOW_SIDE_EFFECTING


def _in_hbm(a):
    return pltpu.with_memory_space_constraint(a, pltpu.HBM)


def _copies_start(name, bufs, counts, plan, after=None):
    n, ng = len(bufs), len(counts)
    extra = [] if after is None else [after]
    n_in = n + len(extra)

    def body(*refs):
        sems = refs[n_in : n_in + 2 * ng]
        token = refs[n_in + 2 * ng + n]
        for gi, copies in enumerate(plan(refs[:n])):
            for k, (s, d, peer) in enumerate(copies):
                pltpu.make_async_remote_copy(src_ref=s, dst_ref=d, send_sem=sems[2 * gi].at[k], recv_sem=sems[2 * gi + 1].at[k],
                                             device_id=peer, device_id_type=MESH_ID).start()
        token[...] = jnp.zeros(token.shape, token.dtype)

    out = _pcall(
        body,
        name=name,
        out_shape=tuple(pltpu.SemaphoreType.DMA((c,)) for c in counts for _ in range(2))
        + tuple(pltpu.HBM(b.shape, b.dtype) for b in bufs) + (jax.ShapeDtypeStruct((8, 128), F32),),
        in_specs=(HBM_SPEC,) * n + (pl.BlockSpec(memory_space=pl.ANY),) * len(extra),
        out_specs=(SEM_SPEC,) * (2 * ng) + (HBM_SPEC,) * n + (pl.BlockSpec(memory_space=pltpu.VMEM),),
        input_output_aliases={i: 2 * ng + i for i in range(n)},
        compiler_params=pltpu.CompilerParams(has_side_effects=SIDE_EFFECT),
    )(*[_in_hbm(b) for b in bufs], *extra)
    return [(out[2 * g], out[2 * g + 1]) for g in range(ng)], list(out[2 * ng : 2 * ng + n]), out[2 * ng + n]


def _copies_wait(name, sems, bufs, plan, after):
    n, ng = len(bufs), len(sems)

    def body(*refs):
        for gi, copies in enumerate(plan(refs[:n])):
            for k, (s, d, peer) in enumerate(copies):
                cp = pltpu.make_async_remote_copy(src_ref=s, dst_ref=d, send_sem=refs[n + 2 * gi].at[k], recv_sem=refs[n + 2 * gi + 1].at[k],
                                                  device_id=peer, device_id_type=MESH_ID)
                cp.wait_send()
                cp.wait_recv()

    out = _pcall(
        body,
        name=name,
        out_shape=tuple(pltpu.HBM(b.shape, b.dtype) for b in bufs),
        in_specs=(HBM_SPEC,) * n + (SEM_SPEC,) * (2 * ng) + (pl.BlockSpec(memory_space=pl.ANY),),
        out_specs=(HBM_SPEC,) * n,
        input_output_aliases={i: i for i in range(n)},
        compiler_params=pltpu.CompilerParams(has_side_effects=SIDE_EFFECT),
    )(*bufs, *[s for pair in sems for s in pair], after)
    return list(out)


def _matrix_groups(depth):
    out = []
    for i, (kind, j, _, _) in enumerate(_layer_plan(depth)):
        out.append(([("pool_w", j, 1)], [("attn_w_qkv", j, 1), ("attn_w_o", j, 0)], [("ret_w_in", j, 1), ("ret_w_out", j, 0)])[kind])
        out.append([("ffn_w_up", i, 1), ("ffn_w_down", i, 0)])
    return out


MIX, FFN = 0, 1


def _peers(mx, my, mc):
    out = []
    for fx, fy in CHIP_FLIPS:
        px, py = _flip(mx, fx), _flip(my, fy)
        out.append(((px, py, mc), 2 * px + py))
    return out


def full_buffers(shards, names, layers):
    out = []
    for name, _, axis in [e for layer in layers for e in layer]:
        shp = list(shards[names.index(name)].shape[1:])
        shp[axis] *= N_CHIP
        out.append(lax.empty(tuple(shp), BF16))
    return out


def _gather_plan(names, layers, group, n_shards, here):
    def plan(refs):
        mx, my, mc = _pos()
        s_refs, f_refs = refs[:n_shards], refs[n_shards:]
        groups, k = [], 0
        for gi, layer in enumerate(layers):
            if group is not None and gi != group:
                continue
            copies = []
            for name, idx, axis in layer:
                src = s_refs[names.index(name)].at[idx]
                n = src.shape[axis]
                for peer, pchip in _peers(mx, my, mc) + [((mx, my, 1 - mc), 2 * mx + my)]:
                    at = pchip if here else 2 * mx + my
                    copies.append((src, _at_axis(f_refs[k], axis, at * n, n), peer))
                k += 1
            groups.append(copies)
        return groups

    return plan


def gather_start(shards, names, layers, fulls, after):
    counts = [len(layer) * (len(CHIP_FLIPS) + 1) for layer in layers]
    sems, bufs, _ = _copies_start("gather_start", list(shards) + list(fulls), counts,
                                  _gather_plan(names, layers, None, len(shards), False), after)
    return sems, bufs[: len(shards)], bufs[len(shards) :]


def gather_wait(g, sems_g, shards, names, layers, fulls_g, after):
    bufs = _copies_wait(f"gather_wait_{g}", [sems_g], list(shards) + list(fulls_g), _gather_plan(names, layers, g, len(shards), True), after)
    return bufs[: len(shards)], bufs[len(shards) :]


def _scatter_plan(layer_entries, n_grads, land_of):
    def plan(refs):
        mx, my, mc = _pos()
        groups, k = [], 0
        for layer in layer_entries:
            copies = []
            for name, idx, axis in layer:
                gref, land = refs[k], refs[n_grads + land_of(k, name)]
                n = gref.shape[axis] // N_CHIP
                for slot, (peer, pchip) in enumerate(_peers(mx, my, mc)):
                    copies.append((_at_axis(gref, axis, pchip * n, n), land.at[slot, idx], peer))
                k += 1
            groups.append(copies)
        return groups

    return plan


def scatter_start(i, layer, grads, lands):
    sems, bufs, token = _copies_start(f"scatter_start_{i}", list(grads) + list(lands), [len(layer) * len(CHIP_FLIPS)],
                                      _scatter_plan([layer], len(grads), lambda k, name: k))
    return sems[0], bufs[: len(grads)], bufs[len(grads) :], token


def scatter_wait(sems, layers, grads, names, lands, after):
    bufs = _copies_wait("scatter_wait", sems, list(grads) + list(lands),
                        _scatter_plan(layers, len(grads), lambda k, name: names.index(name)), after)
    return bufs[: len(grads)], bufs[len(grads) :]


def sibling_swap(parts):
    def plan(in_refs, out_refs, probe):
        if probe:
            return 0, len(parts)
        mx, my, mc = _pos()
        return [], [(s, o, (mx, my, 1 - mc), o) for s, o in zip(in_refs, out_refs)]

    return _hbm_exchange("sibling_swap", parts, [jax.ShapeDtypeStruct(p.shape, p.dtype) for p in parts], plan)


EW_ROWS = 256


def _ew_call(name, fn, ins, n_out):
    rows, cols = ins[0].shape[-2:]
    tr = EW_ROWS if rows % EW_ROWS == 0 else rows

    def body(*refs):
        outs = fn(*[r[...] for r in refs[: len(ins)]])
        for o_ref, o in zip(refs[len(ins) :], outs):
            o_ref[...] = o

    def spec(a):
        if a.ndim == 3:
            return pl.BlockSpec((a.shape[0], tr, cols), lambda i: (0, i, 0))
        return pl.BlockSpec((tr, cols), lambda i: (i, 0))

    return _pcall(
        body,
        name=name,
        grid=(rows // tr,),
        in_specs=[spec(a) for a in ins],
        out_specs=[pl.BlockSpec((tr, cols), lambda i: (i, 0))] * n_out,
        out_shape=[jax.ShapeDtypeStruct((rows, cols), F32)] * n_out,
        compiler_params=pltpu.CompilerParams(dimension_semantics=("parallel",), vmem_limit_bytes=VMEM_LIMIT),
    )(*ins)


def _adamw(w, g, m, v):
    m = ADAM_B1 * m + (1.0 - ADAM_B1) * g
    v = ADAM_B2 * v + (1.0 - ADAM_B2) * (g * g)
    m_hat = m / (1.0 - ADAM_B1 ** ADAM_STEP)
    v_hat = v / (1.0 - ADAM_B2 ** ADAM_STEP)
    return -ADAM_LR * (m_hat / (jnp.sqrt(v_hat) + ADAM_EPS) + ADAM_WD * w), m, v


def sum_slots(name, own, landing):
    def fn(o, l):
        acc = o.astype(F32)
        for k in range(l.shape[0]):
            acc = acc + l[k].astype(F32)
        return (acc,)

    return _ew_call(name, fn, [own, landing], 1)[0]


def adamw_pair(name, w, m, v, p, ps):
    def fn(w, m, v, p, ps):
        g = p + ps
        return (g,) + _adamw(w, g, m, v)

    return _ew_call(name, fn, [w, m, v, p, ps], 4)


def adamw_one(name, w, m, v, g):
    return _ew_call(name, lambda w, m, v, g: _adamw(w, g, m, v), [w, m, v, g], 3)


def reduce_devices(name, x):
    def fn(a):
        acc = a[0]
        for k in range(1, a.shape[0]):
            acc = acc + a[k]
        return (acc,)

    return _ew_call(name, fn, [x], 1)[0]


ADA_ROWS = 16
ADA_CTX = N_DEV


def ada_fwd(s9, ada_w, ada_b):
    depth, _, n = ada_w.shape

    def body(s_ref, w_ref, b_ref, o_ref):
        s = s_ref[...]
        o_ref[...] = _dotf((s * _sigmoid(s)).astype(BF16), w_ref[...].astype(BF16)) + b_ref[...]

    return _pcall(
        body,
        name="ada_fwd",
        grid=(depth,),
        in_specs=[pl.BlockSpec((ADA_ROWS, D), lambda i: (0, 0)), pl.BlockSpec((None, D, n), lambda i: (i, 0, 0)),
                  pl.BlockSpec((None, 1, n), lambda i: (i, 0, 0))],
        out_specs=pl.BlockSpec((None, ADA_ROWS, n), lambda i: (i, 0, 0)),
        out_shape=jax.ShapeDtypeStruct((depth, ADA_ROWS, n), F32),
        compiler_params=pltpu.CompilerParams(dimension_semantics=("arbitrary",), vmem_limit_bytes=VMEM_LIMIT),
    )(s9, ada_w, ada_b)


def ada_bwd(s9, dm, ada_w):
    depth, _, n = ada_w.shape

    def body(s_ref, dm_ref, w_ref, gw_ref, ds_ref):
        _acc_init(pl.program_id(0), ds_ref)
        s = s_ref[...]
        dmb = dm_ref[...].astype(BF16)
        gw_ref[...] = _dot_tn((s * _sigmoid(s)).astype(BF16), dmb)
        ds_ref[...] += _dot_nt(dmb, w_ref[...].astype(BF16))

    return _pcall(
        body,
        name="ada_bwd",
        grid=(depth,),
        in_specs=[pl.BlockSpec((ADA_ROWS, D), lambda i: (0, 0)), pl.BlockSpec((None, ADA_ROWS, n), lambda i: (i, 0, 0)),
                  pl.BlockSpec((None, D, n), lambda i: (i, 0, 0))],
        out_specs=[pl.BlockSpec((None, D, n), lambda i: (i, 0, 0)), pl.BlockSpec((ADA_ROWS, D), lambda i: (0, 0))],
        out_shape=[jax.ShapeDtypeStruct((depth, D, n), F32), jax.ShapeDtypeStruct((ADA_ROWS, D), F32)],
        compiler_params=pltpu.CompilerParams(dimension_semantics=("arbitrary",), vmem_limit_bytes=VMEM_LIMIT),
    )(s9, dm, ada_w)


def cctx_grad(parts, c_ctx):
    def body(p_ref, c_ref, o_ref):
        acc = p_ref[0, ADA_CTX : ADA_CTX + 1, :]
        for chip in range(1, N_CHIP):
            acc = acc + p_ref[2 * chip, ADA_CTX : ADA_CTX + 1, :]
        c = c_ref[...]
        sg = _sigmoid(c)
        o_ref[...] = acc * (sg * (1.0 + c * (1.0 - sg)))

    return _pcall(body, name="cctx_grad", out_shape=jax.ShapeDtypeStruct((1, D), F32))(parts, c_ctx)


def _pack(arrs):
    flat = jnp.concatenate([a.astype(F32).reshape(-1) for a in arrs])
    rows = -(-flat.shape[0] // (8 * D)) * 8
    return jnp.pad(flat, (0, rows * D - flat.shape[0])).reshape(rows, D)


def _unpack(slab, shapes):
    lead = slab.shape[:-2]
    flat = slab.reshape(lead + (-1,))
    out, off = [], 0
    for shp in shapes:
        n = 1
        for d in shp:
            n *= d
        out.append(flat[..., off : off + n].reshape(lead + tuple(shp)))
        off += n
    return out


def _unshard(per_dev, axis):
    return jnp.concatenate([per_dev[2 * chip] for chip in range(N_CHIP)], axis=axis)


BIG = (("pool_w", 1), ("attn_w_qkv", 1), ("attn_w_o", 0), ("ret_w_in", 1), ("ret_w_out", 0), ("ffn_w_up", 1), ("ffn_w_down", 0))
WEIGHTS = ("c_ctx", "ada_w", "ada_b", "norm_w", "pool_w", "pool_b", "pool_scale", "attn_w_qkv", "attn_q_gain", "attn_k_gain",
           "attn_w_o", "ret_w_in", "ret_decay_logit", "ret_gn_w", "ret_w_out", "ffn_w_up", "ffn_conv_w", "ffn_conv_b", "ffn_w_down")
SMALL = tuple(n for n in WEIGHTS if n != "ada_w" and n not in dict(BIG))
SMALL_SHARD_AXIS = {"norm_w": 2, "pool_b": 1, "pool_scale": 1, "ret_gn_w": 1, "ffn_conv_w": 2}


def kernel(x, c, ctx, c_ctx, ada_w, ada_b, norm_w, pool_w, pool_b, pool_scale, attn_w_qkv, attn_q_gain, attn_k_gain, attn_w_o, ret_w_in, ret_decay_logit, ret_gn_w, ret_w_out, ffn_w_up, ffn_conv_w, ffn_conv_b, ffn_w_down, loss_target, m_c_ctx, m_ada_w, m_ada_b, m_norm_w, m_pool_w, m_pool_b, m_pool_scale, m_attn_w_qkv, m_attn_q_gain, m_attn_k_gain, m_attn_w_o, m_ret_w_in, m_ret_decay_logit, m_ret_gn_w, m_ret_w_out, m_ffn_w_up, m_ffn_conv_w, m_ffn_conv_b, m_ffn_w_down, v_c_ctx, v_ada_w, v_ada_b, v_norm_w, v_pool_w, v_pool_b, v_pool_scale, v_attn_w_qkv, v_attn_q_gain, v_attn_k_gain, v_attn_w_o, v_ret_w_in, v_ret_decay_logit, v_ret_gn_w, v_ret_w_out, v_ffn_w_up, v_ffn_conv_w, v_ffn_conv_b, v_ffn_w_down):
    P = dict(zip(WEIGHTS, (c_ctx, ada_w, ada_b, norm_w, pool_w, pool_b, pool_scale, attn_w_qkv, attn_q_gain, attn_k_gain, attn_w_o,
                           ret_w_in, ret_decay_logit, ret_gn_w, ret_w_out, ffn_w_up, ffn_conv_w, ffn_conv_b, ffn_w_down)))
    M = dict(zip(WEIGHTS, (m_c_ctx, m_ada_w, m_ada_b, m_norm_w, m_pool_w, m_pool_b, m_pool_scale, m_attn_w_qkv, m_attn_q_gain,
                           m_attn_k_gain, m_attn_w_o, m_ret_w_in, m_ret_decay_logit, m_ret_gn_w, m_ret_w_out, m_ffn_w_up,
                           m_ffn_conv_w, m_ffn_conv_b, m_ffn_w_down)))
    V = dict(zip(WEIGHTS, (v_c_ctx, v_ada_w, v_ada_b, v_norm_w, v_pool_w, v_pool_b, v_pool_scale, v_attn_w_qkv, v_attn_q_gain,
                           v_attn_k_gain, v_attn_w_o, v_ret_w_in, v_ret_decay_logit, v_ret_gn_w, v_ret_w_out, v_ffn_w_up,
                           v_ffn_conv_w, v_ffn_conv_b, v_ffn_w_down)))
    depth, s, l = ada_w.shape[0], x.shape[1], ctx.shape[1]
    assert l == BM and s % BM == 0 and s % GRID_W == 0
    nlb = s // BM
    n_pool = pool_w.shape[0]
    mx, my, mc = _pos()
    chip, dev = 2 * mx + my, 4 * mx + 2 * my + mc
    nada = ada_w.shape[2]

    sharded = [n for n in SMALL if n in SMALL_SHARD_AXIS]
    got = small_all_gather("gather_small", _pack([c[0]] + [P[n] for n in sharded]))
    got = _unpack(got, [(D,)] + [P[n].shape for n in sharded])
    c_all = got[0]
    full = {n: _unshard(g_, SMALL_SHARD_AXIS[n]) for n, g_ in zip(sharded, got[1:])}

    s9 = jnp.concatenate([c_all, c_ctx[None, :], jnp.zeros((ADA_ROWS - N_DEV - 1, D), F32)], axis=0)
    ada_b_mine = lax.dynamic_slice_in_dim(ada_b, chip * nada, nada, axis=1)[:, None, :]
    mod_part = ada_fwd(s9, ada_w, ada_b_mine)
    mod_all = _unshard(small_all_gather("gather_mod", mod_part.reshape(depth * ADA_ROWS, nada)), 1).reshape(depth, ADA_ROWS, 6, D)
    mod_mine = lax.dynamic_index_in_dim(mod_all, dev, axis=1, keepdims=False)
    mods_all = jnp.pad(jnp.stack([mod_mine, mod_all[:, ADA_CTX]], axis=1), ((0, 0), (0, 0), (0, 2), (0, 0)))
    mods = [mods_all[i] for i in range(depth)]

    names = [n for n, _ in BIG]
    layers = _matrix_groups(depth)
    shards = [P[n].astype(BF16) for n in names]
    gsems, shards, fulls = gather_start(shards, names, layers, full_buffers(shards, names, layers), mods_all)
    first = [sum(len(layer) for layer in layers[:g_]) for g_ in range(len(layers) + 1)]
    flight = {"shards": shards}

    def fetch(i, part, after):
        g_ = 2 * i + part
        flight["shards"], mats = gather_wait(g_, gsems[g_], flight["shards"], names, layers, fulls[first[g_] : first[g_ + 1]], after)
        return {name: m for (name, _, _), m in zip(layers[g_], mats)}

    lands = {n: lax.empty((len(CHIP_FLIPS),) + P[n].shape, BF16) for n in names}
    sent = {}

    def emit(i, part, big):
        g_ = 2 * i + part
        lnames = [name for name, _, _ in layers[g_]]
        sems, gl, ll, token = scatter_start(g_, layers[g_], [big[n] for n in lnames], [lands[n] for n in lnames])
        lands.update(zip(lnames, ll))
        sent[g_] = (sems, gl)
        return token

    w = {
        "nw": full["norm_w"],
        "pbs": jnp.concatenate([full["pool_b"][:, None], full["pool_scale"][:, None], jnp.zeros((n_pool, 6, D), F32)], axis=1),
        "gains": jnp.concatenate([attn_q_gain, attn_k_gain, jnp.zeros((6, HD), F32)], axis=0),
        "gnw": full["ret_gn_w"],
        "logit_b": jnp.broadcast_to(ret_decay_logit[0][:, :, None, None], (2, RET_HEADS, 8, 128)),
        "cw": jnp.concatenate([full["ffn_conv_w"], ffn_conv_b[:, None, :], jnp.zeros((depth, 4, 2 * D_FF), F32)], axis=1),
    }
    w["cos"], w["sin"] = rope_tables(s, l)

    xs = jnp.concatenate([x[0], ctx[0]], axis=0)
    loss_tile, dxs, g = local_step(xs, loss_target[0], mods, w, nlb, depth, fetch, emit)
    loss = lax.psum(loss_tile[0, 0], MESH_AXES)
    grad_x = dxs[None]

    small_shapes = [(depth, 2, 8, D), (depth, 2, D), (n_pool, 2, D), (2, HD), (2, RET_HEADS), (RNV,), (depth, 4, 2 * D_FF)]
    slab = _pack([
        jnp.stack(g["dmod"]),
        jnp.stack([jnp.stack([g["dnw1"][i][0], g["dnw2"][i][0]]) for i in range(depth)]),
        jnp.stack([g["dpbs"][j][0:2] for j in range(n_pool)]),
        g["dgains"][0:2], g["dlogit"][:, :, 0, 0], g["dgnw"][0], jnp.stack([g["dcw"][i][0:4] for i in range(depth)]),
    ])
    slabs = small_all_gather("gather_small_grads", slab)
    dmod_dev = _unpack(slabs, small_shapes[:1])[0]
    t_dmod, t_nw, t_pbs, t_gains, t_logit, t_gnw, t_cw = _unpack(reduce_devices("reduce_small_grads", slabs), small_shapes)

    def cols(a):
        return lax.dynamic_slice_in_dim(a, chip * nada, nada, axis=a.ndim - 1)

    dm_lat = jnp.swapaxes(cols(dmod_dev[:, :, 0, :6].reshape(N_DEV, depth, 6 * D)), 0, 1)
    dm_ctx = cols(t_dmod[:, 1, :6].reshape(depth, 1, 6 * D))
    dm = jnp.concatenate([dm_lat, dm_ctx, jnp.zeros((depth, ADA_ROWS - N_DEV - 1, nada), F32)], axis=1)
    g_ada_w, ds9 = ada_bwd(s9, dm, ada_w)
    g_c_ctx = cctx_grad(small_all_gather("gather_dcctx", ds9), c_ctx[None, :])[0]

    def mine(a, name):
        n = P[name].shape[SMALL_SHARD_AXIS[name]]
        return lax.dynamic_slice_in_dim(a, chip * n, n, axis=SMALL_SHARD_AXIS[name])

    G = {
        "c_ctx": g_c_ctx,
        "ada_b": (t_dmod[:, 0, :6] + t_dmod[:, 1, :6]).reshape(depth, 6 * D),
        "norm_w": mine(t_nw, "norm_w"),
        "pool_b": mine(t_pbs[:, 0], "pool_b"), "pool_scale": mine(t_pbs[:, 1], "pool_scale"),
        "attn_q_gain": t_gains[0:1], "attn_k_gain": t_gains[1:2],
        "ret_decay_logit": t_logit[None], "ret_gn_w": mine(t_gnw[None], "ret_gn_w"),
        "ffn_conv_w": mine(t_cw[:, 0:3], "ffn_conv_w"), "ffn_conv_b": t_cw[:, 3],
    }
    sw, sg, sm, sv = (_pack([d_[n] for n in SMALL]) for d_ in (P, G, M, V))
    outs = adamw_one("adamw_small", sw, sm, sv, sg)
    D_, NM, NV = ({n: a for n, a in zip(SMALL, _unpack(o, [P[n].shape for n in SMALL]))} for o in outs)

    flat2 = lambda a: a.reshape(-1, a.shape[-1])
    G["ada_w"] = g_ada_w
    o3 = adamw_one("adamw_ada", flat2(ada_w), flat2(M["ada_w"]), flat2(V["ada_w"]), flat2(g_ada_w))
    D_["ada_w"], NM["ada_w"], NV["ada_w"] = (o.reshape(ada_w.shape) for o in o3)

    sent_grads, landed = scatter_wait([sent[g_][0] for g_ in range(len(layers))], layers,
                                      [a for g_ in range(len(layers)) for a in sent[g_][1]], names, [lands[n] for n in names], o3[0])
    own = {n: [None] * P[n].shape[0] for n in names}
    for (name, idx, axis), a in zip([e for layer in layers for e in layer], sent_grads):
        n_ = a.shape[axis] // N_CHIP
        own[name][idx] = lax.dynamic_slice_in_dim(a, chip * n_, n_, axis=axis)
    partial = [sum_slots("sum_" + n, jnp.stack(own[n]).reshape(-1, lnd.shape[-1]), lnd.reshape(len(CHIP_FLIPS), -1, lnd.shape[-1]))
               for n, lnd in zip(names, landed)]
    theirs = sibling_swap(partial)
    for (n, _), p, ps in zip(BIG, partial, theirs):
        o4 = adamw_pair("adamw_" + n, flat2(P[n]), flat2(M[n]), flat2(V[n]), p, ps)
        G[n], D_[n], NM[n], NV[n] = (o.reshape(P[n].shape) for o in o4)

    return (loss, grad_x, *[G[n] for n in WEIGHTS], *[D_[n] for n in WEIGHTS], *[NM[n] for n in WEIGHTS], *[NV[n] for n in WEIGHTS])
```
